```python
import jax
import jax.numpy as jnp
from jax import lax
import numpy as np

D_MODEL = 1024
BATCH = 16
SEQ = 2048
DEPTH = 1

GRID_W = 64
CTX_LEN = 256
EPS = 1e-6
N_MOD = 6

MLA_V = 128
MLA_HEADS = D_MODEL // MLA_V
MLA_NOPE = 128
MLA_ROPE = 64
MLA_QK = MLA_NOPE + MLA_ROPE
MLA_Q_RANK = 3 * D_MODEL // 4
MLA_KV_RANK = D_MODEL // 4
MLA_WIDTH = MLA_HEADS * MLA_V
ROPE_BASE = 10000.0
Q_BLOCK = 128

HG_DK = 128
HG_HEADS = D_MODEL // HG_DK
HG_DV = D_MODEL // HG_HEADS
HG_FDIM = HG_HEADS * HG_DK
HG_WIDTH = HG_HEADS * HG_DV
HG_CHUNK = 64

N_EXPERTS = 32
TOP_K = 4
D_EXPERT = D_MODEL
SWIGLU_LIMIT = 7.0
SWIGLU_ALPHA = 1.702
MOE_BLOCK = 128

IN_SPLITS = (MLA_Q_RANK, MLA_KV_RANK + MLA_ROPE, HG_FDIM, HG_FDIM, HG_FDIM, HG_WIDTH, HG_WIDTH, D_MODEL, D_MODEL)
D_IN = MLA_Q_RANK + MLA_KV_RANK + MLA_ROPE + 3 * HG_FDIM + 2 * HG_WIDTH + 2 * D_MODEL

kernel_name = 'hybrid_mla_hgrn2_moe_diffusion_block'


def rmsnorm(x, g):
    xf = x.astype(jnp.float32)
    y = xf * lax.rsqrt(jnp.mean(xf * xf, axis=-1, keepdims=True) + EPS)
    return (y * g.astype(jnp.float32)).astype(x.dtype)


def split_cols(p):
    return jnp.split(p, np.cumsum(IN_SPLITS)[:-1].tolist(), axis=-1)


def axial_rope_tables(rows):
    row = jnp.repeat(jnp.arange(rows), GRID_W).astype(jnp.float32)
    col = jnp.tile(jnp.arange(GRID_W), rows).astype(jnp.float32)
    n_freq = MLA_ROPE // 4
    inv = ROPE_BASE ** (-jnp.arange(n_freq, dtype=jnp.float32) / n_freq)
    ang = jnp.concatenate([row[:, None] * inv, col[:, None] * inv], axis=-1)
    return jnp.cos(ang), jnp.sin(ang)


def apply_rope(x, cos, sin):
    half = MLA_ROPE // 2
    x1 = x[..., :half].astype(jnp.float32)
    x2 = x[..., half:].astype(jnp.float32)
    return jnp.concatenate([x1 * cos - x2 * sin, x1 * sin + x2 * cos], axis=-1).astype(x.dtype)


def mla_queries(q_in, q_norm_g, w_q_b, cos, sin):
    b, n, _ = q_in.shape
    q = (rmsnorm(q_in, q_norm_g) @ w_q_b).reshape(b, n, MLA_HEADS, MLA_QK)
    q_nope, q_rope = q[..., :MLA_NOPE], q[..., MLA_NOPE:]
    if cos is not None:
        q_rope = apply_rope(q_rope, cos[:, None, :], sin[:, None, :])
    return jnp.concatenate([q_nope, q_rope], axis=-1)


def mla_keys_values(kv_in, kv_norm_g, w_kv_b, cos, sin):
    b, n, _ = kv_in.shape
    c_kv, k_rope = kv_in[..., :MLA_KV_RANK], kv_in[..., MLA_KV_RANK:]
    kv = (rmsnorm(c_kv, kv_norm_g) @ w_kv_b).reshape(b, n, MLA_HEADS, MLA_NOPE + MLA_V)
    k_nope, v = kv[..., :MLA_NOPE], kv[..., MLA_NOPE:]
    if cos is not None:
        k_rope = apply_rope(k_rope, cos, sin)
    k_rope = jnp.broadcast_to(k_rope[:, :, None, :], (b, n, MLA_HEADS, MLA_ROPE))
    return jnp.concatenate([k_nope, k_rope], axis=-1), v


def block_attention(q, k, v):
    b, n, h, dk = q.shape
    nb = n // Q_BLOCK
    scale = dk ** -0.5
    q_blocks = q.reshape(b, nb, Q_BLOCK, h, dk).transpose(1, 0, 2, 3, 4)

    def one_block(qb):
        s = jnp.einsum('bqhd,bkhd->bhqk', qb, k, preferred_element_type=jnp.float32) * scale
        p = jax.nn.softmax(s, axis=-1).astype(v.dtype)
        return jnp.einsum('bhqk,bkhd->bqhd', p, v)

    out = lax.map(one_block, q_blocks)
    return out.transpose(1, 0, 2, 3, 4).reshape(b, n, h * v.shape[-1])


def hgrn2_chunk_scan(q, k, v, log_f, s0):
    b, n, h, dk = q.shape
    dv = v.shape[-1]
    nc = n // HG_CHUNK

    def chunks(a):
        return a.reshape(b, nc, HG_CHUNK, h, a.shape[-1]).transpose(1, 0, 3, 2, 4)

    lower_tri = jnp.tril(jnp.ones((HG_CHUNK, HG_CHUNK), dtype=bool))[:, :, None]

    def step(state, inp):
        qc, kc, vc, gc = inp
        cum = jnp.cumsum(gc, axis=2)
        o_inter = jnp.einsum('bhtk,bhkv->bhtv', qc * jnp.exp(cum), state)
        rel = cum[:, :, :, None, :] - cum[:, :, None, :, :]
        decay = jnp.where(lower_tri, jnp.exp(jnp.minimum(rel, 0.0)), 0.0)
        att = jnp.einsum('bhtk,bhtsk,bhsk->bhts', qc, decay, kc)
        o_intra = jnp.einsum('bhts,bhsv->bhtv', att, vc)
        last = cum[:, :, -1:, :]
        state = jnp.exp(last[:, :, 0, :])[..., None] * state + jnp.einsum('bhsk,bhsv->bhkv', kc * jnp.exp(last - cum), vc)
        return state, o_inter + o_intra

    s_final, o = lax.scan(step, s0, (chunks(q), chunks(k), chunks(v), chunks(log_f)))
    return o.transpose(1, 0, 3, 2, 4).reshape(b, n, h, dv), s_final


def hgrn2_gates(z, lb):
    b, n = z.shape[:2]
    f = lb + (1.0 - lb) * jax.nn.sigmoid(z.astype(jnp.float32))
    return (1.0 - f).reshape(b, n, HG_HEADS, HG_DK), jnp.log(f).reshape(b, n, HG_HEADS, HG_DK)


def hgrn2_qv(p):
    b, n = p[2].shape[:2]
    q = (jax.nn.silu(p[2].astype(jnp.float32)) * HG_DK ** -0.5).reshape(b, n, HG_HEADS, HG_DK)
    v = p[5].astype(jnp.float32).reshape(b, n, HG_HEADS, HG_DV)
    return q, v


def hgrn2_readout(o, g, hg_norm_g):
    b, n = g.shape[:2]
    gate = jax.nn.silu(g.astype(jnp.float32)).reshape(b, n, HG_HEADS, HG_DV)
    return (rmsnorm(o, hg_norm_g) * gate).reshape(b, n, HG_WIDTH).astype(g.dtype)


def merge_branches(p, y_mla, y_hg, w_out):
    g_mla = jax.nn.sigmoid(p[7].astype(jnp.float32))
    g_hg = jax.nn.sigmoid(p[8].astype(jnp.float32))
    y = g_mla * y_mla.astype(jnp.float32) + g_hg * y_hg.astype(jnp.float32)
    return y.astype(p[7].dtype) @ w_out


def token_mixer(h_lat, h_ctx, cos, sin, lb_fwd, lb_bwd, w_in, q_norm_g, w_q_b, kv_norm_g, w_kv_b, hg_norm_g, w_out, with_ctx_out):
    lat = split_cols(h_lat @ w_in)
    ctx = split_cols(h_ctx @ w_in)
    k_lat, v_lat = mla_keys_values(lat[1], kv_norm_g, w_kv_b, cos, sin)
    k_ctx, v_ctx = mla_keys_values(ctx[1], kv_norm_g, w_kv_b, None, None)
    q_lat = mla_queries(lat[0], q_norm_g, w_q_b, cos, sin)
    y_mla_lat = block_attention(q_lat, jnp.concatenate([k_lat, k_ctx], axis=1), jnp.concatenate([v_lat, v_ctx], axis=1))
    q_l, v_l = hgrn2_qv(lat)
    q_c, v_c = hgrn2_qv(ctx)
    kf_l, gf_l = hgrn2_gates(lat[3], lb_fwd)
    kb_l, gb_l = hgrn2_gates(lat[4], lb_bwd)
    kf_c, gf_c = hgrn2_gates(ctx[3], lb_fwd)
    kb_c, gb_c = hgrn2_gates(ctx[4], lb_bwd)
    b = h_lat.shape[0]
    s0 = jnp.zeros((b, HG_HEADS, HG_DK, HG_DV), jnp.float32)
    flip = lambda a: jnp.flip(a, axis=1)
    o_cf, s_cf = hgrn2_chunk_scan(q_c, kf_c, v_c, gf_c, s0)
    o_cb, s_cb = hgrn2_chunk_scan(flip(q_c), flip(kb_c), flip(v_c), flip(gb_c), s0)
    o_lf, _ = hgrn2_chunk_scan(q_l, kf_l, v_l, gf_l, s_cf)
    o_lb, _ = hgrn2_chunk_scan(flip(q_l), flip(kb_l), flip(v_l), flip(gb_l), s_cb)
    y_hg_lat = hgrn2_readout(o_lf + flip(o_lb), lat[6], hg_norm_g)
    out_lat = merge_branches(lat, y_mla_lat, y_hg_lat, w_out)
    if not with_ctx_out:
        return out_lat, None
    q_ctx = mla_queries(ctx[0], q_norm_g, w_q_b, None, None)
    y_mla_ctx = block_attention(q_ctx, k_ctx, v_ctx)
    y_hg_ctx = hgrn2_readout(o_cf + flip(o_cb), ctx[6], hg_norm_g)
    return out_lat, merge_branches(ctx, y_mla_ctx, y_hg_ctx, w_out)


def moe_ffn(h, w_router, b_router, w_gate_up, b_gate_up, w_down, b_down):
    shp = h.shape
    t = h.reshape(-1, shp[-1])
    n_tok = t.shape[0]
    n_pair = n_tok * TOP_K
    logits = (t @ w_router + b_router).astype(jnp.float32)
    top_val, top_idx = lax.top_k(logits, TOP_K)
    probs = jax.nn.softmax(top_val, axis=-1)
    flat_e = top_idx.reshape(-1)
    flat_tok = jnp.arange(n_pair, dtype=jnp.int32) // TOP_K
    order = jnp.argsort(flat_e)
    sorted_e = flat_e[order]
    counts = jnp.bincount(flat_e, length=N_EXPERTS)
    padded = (counts + MOE_BLOCK - 1) // MOE_BLOCK * MOE_BLOCK
    start = jnp.cumsum(counts) - counts
    pad_end = jnp.cumsum(padded)
    pad_start = pad_end - padded
    dest_sorted = pad_start[sorted_e] + jnp.arange(n_pair, dtype=jnp.int32) - start[sorted_e]
    dest = jnp.zeros((n_pair,), jnp.int32).at[order].set(dest_sorted.astype(jnp.int32))
    n_rows = n_pair + N_EXPERTS * MOE_BLOCK
    n_blocks = n_rows // MOE_BLOCK
    row_tok = jnp.full((n_rows,), n_tok, jnp.int32).at[dest].set(flat_tok)
    t_ext = jnp.concatenate([t, jnp.zeros((1, t.shape[-1]), t.dtype)], axis=0)
    x_rows = t_ext[row_tok].reshape(n_blocks, MOE_BLOCK, -1)
    block_start = jnp.arange(n_blocks, dtype=jnp.int32) * MOE_BLOCK
    block_e = jnp.minimum(jnp.searchsorted(pad_end, block_start, side='right'), N_EXPERTS - 1)

    def expert_block(args):
        xb, e = args
        gu = xb @ w_gate_up[e] + b_gate_up[e]
        x_glu = jnp.minimum(gu[:, 0::2], SWIGLU_LIMIT)
        x_lin = jnp.clip(gu[:, 1::2], -SWIGLU_LIMIT, SWIGLU_LIMIT)
        act = x_glu * jax.nn.sigmoid(SWIGLU_ALPHA * x_glu) * (x_lin + 1)
        return act @ w_down[e] + b_down[e]

    y_rows = lax.map(expert_block, (x_rows, block_e)).reshape(n_rows, -1)
    y_pairs = y_rows[dest].reshape(n_tok, TOP_K, -1)
    out = jnp.einsum('tk,tkd->td', probs.astype(y_pairs.dtype), y_pairs)
    return out.reshape(shp)


def setup_inputs(seed: int = 0) -> dict:
    key = jax.random.key(seed)
    ks = jax.random.split(key, 24)
    f32 = jnp.float32
    nrm = lambda k, shape, scale: jax.random.normal(k, shape, f32) * scale
    gain = lambda k, shape: 1.0 + 0.02 * jax.random.normal(k, shape, f32)
    return {
        'x': nrm(ks[0], (BATCH, SEQ, D_MODEL), 1.0),
        'c': nrm(ks[1], (BATCH, D_MODEL), 1.0),
        'ctx': nrm(ks[2], (BATCH, CTX_LEN, D_MODEL), 1.0),
        'c_ctx': nrm(ks[3], (D_MODEL,), 1.0),
        'w_mod': nrm(ks[4], (DEPTH, D_MODEL, N_MOD * D_MODEL), 0.5 * D_MODEL ** -0.5),
        'b_mod': nrm(ks[5], (DEPTH, N_MOD * D_MODEL), 0.02),
        'norm_mix_g': gain(ks[6], (DEPTH, D_MODEL)),
        'w_in': nrm(ks[7], (DEPTH, D_MODEL, D_IN), D_MODEL ** -0.5),
        'mla_q_norm_g': gain(ks[8], (DEPTH, MLA_Q_RANK)),
        'w_q_b': nrm(ks[9], (DEPTH, MLA_Q_RANK, MLA_HEADS * MLA_QK), MLA_Q_RANK ** -0.5),
        'mla_kv_norm_g': gain(ks[10], (DEPTH, MLA_KV_RANK)),
        'w_kv_b': nrm(ks[11], (DEPTH, MLA_KV_RANK, MLA_HEADS * (MLA_NOPE + MLA_V)), MLA_KV_RANK ** -0.5),
        'hg_lb_logits': nrm(ks[12], (2, DEPTH + 1, HG_FDIM), 1.0),
        'hg_norm_g': gain(ks[13], (DEPTH, HG_DV)),
        'w_out': nrm(ks[14], (DEPTH, D_MODEL, D_MODEL), D_MODEL ** -0.5),
        'norm_ffn_g': gain(ks[15], (DEPTH, D_MODEL)),
        'w_router': nrm(ks[16], (DEPTH, D_MODEL, N_EXPERTS), D_MODEL ** -0.5),
        'b_router': nrm(ks[17], (DEPTH, N_EXPERTS), 0.01),
        'w_gate_up': nrm(ks[18], (DEPTH, N_EXPERTS, D_MODEL, 2 * D_EXPERT), D_MODEL ** -0.5),
        'b_gate_up': nrm(ks[19], (DEPTH, N_EXPERTS, 2 * D_EXPERT), 0.01),
        'w_down': nrm(ks[20], (DEPTH, N_EXPERTS, D_EXPERT, D_MODEL), D_EXPERT ** -0.5),
        'b_down': nrm(ks[21], (DEPTH, N_EXPERTS, D_MODEL), 0.01),
        'final_norm_g': gain(ks[22], (D_MODEL,)),
    }


def reference(x, c, ctx, c_ctx, w_mod, b_mod, norm_mix_g, w_in, mla_q_norm_g, w_q_b, mla_kv_norm_g, w_kv_b, hg_lb_logits, hg_norm_g, w_out, norm_ffn_g, w_router, b_router, w_gate_up, b_gate_up, w_down, b_down, final_norm_g):
    rows = x.shape[1] // GRID_W
    cos, sin = axial_rope_tables(rows)
    lb = jnp.cumsum(jax.nn.softmax(hg_lb_logits.astype(jnp.float32), axis=1), axis=1)
    silu_c = jax.nn.silu(c)
    silu_cc = jax.nn.silu(c_ctx)
    n_ctx = ctx.shape[1]
    for l in range(DEPTH):
        last = l == DEPTH - 1
        mod_lat = jnp.split((silu_c @ w_mod[l] + b_mod[l])[:, None, :], N_MOD, axis=-1)
        mod_ctx = jnp.split((silu_cc @ w_mod[l] + b_mod[l])[None, None, :], N_MOD, axis=-1)
        h_lat = rmsnorm(x, norm_mix_g[l]) * (1 + mod_lat[1]) + mod_lat[0]
        h_ctx = rmsnorm(ctx, norm_mix_g[l]) * (1 + mod_ctx[1]) + mod_ctx[0]
        mix_lat, mix_ctx = token_mixer(h_lat, h_ctx, cos, sin, lb[0, l], lb[1, l], w_in[l], mla_q_norm_g[l], w_q_b[l], mla_kv_norm_g[l], w_kv_b[l], hg_norm_g[l], w_out[l], not last)
        x = x + mod_lat[2] * mix_lat
        h_lat = rmsnorm(x, norm_ffn_g[l]) * (1 + mod_lat[4]) + mod_lat[3]
        if last:
            x = x + mod_lat[5] * moe_ffn(h_lat, w_router[l], b_router[l], w_gate_up[l], b_gate_up[l], w_down[l], b_down[l])
        else:
            ctx = ctx + mod_ctx[2] * mix_ctx
            h_ctx = rmsnorm(ctx, norm_ffn_g[l]) * (1 + mod_ctx[4]) + mod_ctx[3]
            y = moe_ffn(jnp.concatenate([h_ctx, h_lat], axis=1), w_router[l], b_router[l], w_gate_up[l], b_gate_up[l], w_down[l], b_down[l])
            ctx = ctx + mod_ctx[5] * y[:, :n_ctx]
            x = x + mod_lat[5] * y[:, n_ctx:]
    return rmsnorm(x, final_norm_g)
```

```python
import functools

import jax
import jax.numpy as jnp
import numpy as np
from jax import lax
from jax.experimental import pallas as pl
from jax.experimental.pallas import tpu as pltpu

F32 = jnp.float32
BF16 = jnp.bfloat16

D_MODEL = 1024
EPS = 1e-6
N_MOD = 6
GRID_W = 64
ROPE_BASE = 10000.0

HEADS = 8
HEAD_DIM = 128
MLA_ROPE = 64
MLA_QK = HEAD_DIM + MLA_ROPE
MLA_QK_PAD = 256
MLA_Q_RANK = 768
MLA_KV_RANK = 256

N_EXPERTS = 32
TOP_K = 4
D_EXPERT = 1024
SWIGLU_LIMIT = 7.0
SWIGLU_ALPHA = 1.702

HG_CHUNK = 128
FFN_ROWS = 512

COL_HQ, COL_FF, COL_FB, COL_I, COL_GO, COL_MM, COL_MH = (i * D_MODEL for i in range(7))
COL_QA = 7 * D_MODEL
COL_CKV = COL_QA + MLA_Q_RANK
COL_KR = COL_CKV + MLA_KV_RANK
N_PROJ = COL_KR + 2 * MLA_ROPE
PROJ_TN = 640

VMEM_LIMIT = 56 * 1024 * 1024


def _cparams(sem, vmem=None):
    return pltpu.CompilerParams(dimension_semantics=sem, vmem_limit_bytes=vmem)


def _tile(n, pref, mult=8):
    best = None
    for t in range(mult, min(n, pref) + 1, mult):
        if n % t == 0:
            best = t
    assert best is not None, (n, pref, mult)
    return best


def _nt(a, b):
    return lax.dot_general(a, b, (((1,), (1,)), ((), ())), preferred_element_type=F32)


def _mm(a, b):
    return jnp.dot(a, b, preferred_element_type=F32)


def _split2(a):
    hi = a.astype(BF16)
    lo = (a - hi.astype(F32)).astype(BF16)
    return hi, lo


def _sigmoid(x):
    return 1.0 / (1.0 + jnp.exp(-x))


def _mod_kernel(c_ref, w_ref, b_ref, o_ref):
    c = c_ref[...]
    s = c * _sigmoid(c)
    s_hi, s_lo = _split2(s)
    w_hi, w_lo = _split2(w_ref[...])
    o_ref[...] = _mm(s_hi, w_hi) + _mm(s_lo, w_hi) + _mm(s_hi, w_lo) + b_ref[...]


def _modulation(cc, w_mod, b_mod):
    r, d = cc.shape
    n = w_mod.shape[1]
    tn = _tile(n, 1536, 128)
    return pl.pallas_call(
        _mod_kernel,
        out_shape=jax.ShapeDtypeStruct((r, n), F32),
        grid=(n // tn,),
        in_specs=[pl.BlockSpec((r, d), lambda j: (0, 0)),
                  pl.BlockSpec((d, tn), lambda j: (0, j)),
                  pl.BlockSpec((1, tn), lambda j: (0, j))],
        out_specs=pl.BlockSpec((r, tn), lambda j: (0, j)),
        compiler_params=_cparams(("parallel",), VMEM_LIMIT),
        name="mod",
    )(cc, w_mod, b_mod)


def _inproj_kernel(x_ref, mod_ref, g_ref, w_ref, o_ref, hn_ref, *, n_lat, tm, ctx_row):
    b = pl.program_id(0)
    t = pl.program_id(1)

    @pl.when(pl.program_id(2) == 0)
    def _():
        x = x_ref[0]
        ms = jnp.mean(x * x, axis=-1, keepdims=True)
        y = x * lax.rsqrt(ms + EPS) * g_ref[...]
        row = t * tm + lax.broadcasted_iota(jnp.int32, (tm, 1), 0)
        is_ctx = row >= n_lat
        m_lat = mod_ref[pl.ds(b, 1), :]
        m_ctx = mod_ref[pl.ds(ctx_row, 1), :]
        shift = jnp.where(is_ctx, m_ctx[:, 0:D_MODEL], m_lat[:, 0:D_MODEL])
        scale = jnp.where(is_ctx, m_ctx[:, D_MODEL:2 * D_MODEL], m_lat[:, D_MODEL:2 * D_MODEL])
        hn_ref[...] = (y * (1.0 + scale) + shift).astype(BF16)

    o_ref[0] = _mm(hn_ref[...], w_ref[...])


def _inproj(xa, mod, g, w_all, n_lat):
    bsz, rows, d = xa.shape
    tm = _tile(rows, 1152)
    n = w_all.shape[1]
    tn = PROJ_TN
    kern = functools.partial(_inproj_kernel, n_lat=n_lat, tm=tm, ctx_row=bsz)
    return pl.pallas_call(
        kern,
        out_shape=jax.ShapeDtypeStruct((bsz, rows, n), F32),
        grid=(bsz, rows // tm, n // tn),
        in_specs=[pl.BlockSpec((1, tm, d), lambda b, t, j: (b, t, 0)),
                  pl.BlockSpec(mod.shape, lambda b, t, j: (0, 0)),
                  pl.BlockSpec((1, d), lambda b, t, j: (0, 0)),
                  pl.BlockSpec((d, tn), lambda b, t, j: (0, j))],
        out_specs=pl.BlockSpec((1, tm, tn), lambda b, t, j: (b, t, j)),
        scratch_shapes=[pltpu.VMEM((tm, d), BF16)],
        compiler_params=_cparams(("parallel", "parallel", "arbitrary"), VMEM_LIMIT),
        name="inproj",
    )(xa, mod, g, w_all)


def _mlaq_kernel(a0_ref, a1_ref, a2_ref, g_ref, w_ref, cos_ref, sin_ref, o_ref, *, tm):
    parts = [a0_ref[0], a1_ref[0], a2_ref[0]]
    ss = sum(jnp.sum(p * p, axis=-1, keepdims=True) for p in parts)
    r = lax.rsqrt(ss * (1.0 / MLA_Q_RANK) + EPS)
    acc = None
    for j, p in enumerate(parts):
        hj = (p * r * g_ref[:, j * 256:(j + 1) * 256]).astype(BF16)
        d = _mm(hj, w_ref[j * 256:(j + 1) * 256, :])
        acc = d if acc is None else acc + d
    lane = lax.broadcasted_iota(jnp.int32, (tm, HEAD_DIM), 1)
    scale = MLA_QK ** -0.5
    cos = cos_ref[...]
    sin = sin_ref[...]
    for h in range(HEADS):
        nope = acc[:, h * MLA_QK_PAD:h * MLA_QK_PAD + HEAD_DIM]
        rp = acc[:, h * MLA_QK_PAD + HEAD_DIM:(h + 1) * MLA_QK_PAD]
        swapped = jnp.where(lane < 32, pltpu.roll(rp, 96, 1), pltpu.roll(rp, 32, 1))
        rot = rp * cos + swapped * sin
        o_ref[0, h, :, 0:HEAD_DIM] = (nope * scale).astype(BF16)
        o_ref[0, h, :, HEAD_DIM:MLA_QK_PAD] = (rot * scale).astype(BF16)


def _mla_q(p, g, w, cos_q, sin_q, n_lat):
    bsz = p.shape[0]
    tm = _tile(n_lat, 512)
    cb = COL_QA // 256
    kern = functools.partial(_mlaq_kernel, tm=tm)
    return pl.pallas_call(
        kern,
        out_shape=jax.ShapeDtypeStruct((bsz, HEADS, n_lat, MLA_QK_PAD), BF16),
        grid=(bsz, n_lat // tm),
        in_specs=[pl.BlockSpec((1, tm, 256), lambda b, t: (b, t, cb)),
                  pl.BlockSpec((1, tm, 256), lambda b, t: (b, t, cb + 1)),
                  pl.BlockSpec((1, tm, 256), lambda b, t: (b, t, cb + 2)),
                  pl.BlockSpec((1, MLA_Q_RANK), lambda b, t: (0, 0)),
                  pl.BlockSpec(w.shape, lambda b, t: (0, 0)),
                  pl.BlockSpec((tm, HEAD_DIM), lambda b, t: (t, 0)),
                  pl.BlockSpec((tm, HEAD_DIM), lambda b, t: (t, 0))],
        out_specs=pl.BlockSpec((1, HEADS, tm, MLA_QK_PAD), lambda b, t: (b, 0, t, 0)),
        compiler_params=_cparams(("parallel", "parallel"), VMEM_LIMIT),
        name="mla_q",
    )(p, p, p, g, w, cos_q, sin_q)


def _mlakv_kernel(c_ref, kr_ref, g_ref, w_ref, cos_ref, sin_ref, k_ref, v_ref):
    c = c_ref[0]
    ms = jnp.mean(c * c, axis=-1, keepdims=True)
    hn = (c * lax.rsqrt(ms + EPS) * g_ref[...]).astype(BF16)
    kv = _mm(hn, w_ref[...])
    grp = kr_ref[0]
    rot = (grp * cos_ref[...] + pltpu.roll(grp, 64, 1) * sin_ref[...]).astype(BF16)
    for h in range(HEADS):
        k_ref[0, h, :, 0:HEAD_DIM] = kv[:, h * HEAD_DIM:(h + 1) * HEAD_DIM].astype(BF16)
        k_ref[0, h, :, HEAD_DIM:MLA_QK_PAD] = rot
        v_ref[0, h] = kv[:, D_MODEL + h * HEAD_DIM:D_MODEL + (h + 1) * HEAD_DIM].astype(BF16)


def _mla_kv(p, g, w, cos_k, sin_k):
    bsz, rows, _ = p.shape
    tm = _tile(rows, 768)
    return pl.pallas_call(
        _mlakv_kernel,
        out_shape=(jax.ShapeDtypeStruct((bsz, HEADS, rows, MLA_QK_PAD), BF16),
                   jax.ShapeDtypeStruct((bsz, HEADS, rows, HEAD_DIM), BF16)),
        grid=(bsz, rows // tm),
        in_specs=[pl.BlockSpec((1, tm, MLA_KV_RANK), lambda b, t: (b, t, COL_CKV // MLA_KV_RANK)),
                  pl.BlockSpec((1, tm, 128), lambda b, t: (b, t, COL_KR // 128)),
                  pl.BlockSpec((1, MLA_KV_RANK), lambda b, t: (0, 0)),
                  pl.BlockSpec(w.shape, lambda b, t: (0, 0)),
                  pl.BlockSpec((tm, 128), lambda b, t: (t, 0)),
                  pl.BlockSpec((tm, 128), lambda b, t: (t, 0))],
        out_specs=(pl.BlockSpec((1, HEADS, tm, MLA_QK_PAD), lambda b, t: (b, 0, t, 0)),
                   pl.BlockSpec((1, HEADS, tm, HEAD_DIM), lambda b, t: (b, 0, t, 0))),
        compiler_params=_cparams(("parallel", "parallel"), VMEM_LIMIT),
        name="mla_kv",
    )(p, p, g, w, cos_k, sin_k)


def _attn_kernel(q_ref, k_ref, v_ref, o_ref):
    s = _nt(q_ref[0, 0], k_ref[0, 0])
    m = jnp.max(s, axis=-1, keepdims=True)
    p = jnp.exp(s - m)
    l = jnp.sum(p, axis=-1, keepdims=True)
    o_ref[0] = _mm(p.astype(BF16), v_ref[0, 0]) * (1.0 / l)


def _attention(q, k, v):
    bsz, _, n, _ = q.shape
    m = k.shape[2]
    tq = _tile(n, 512)
    return pl.pallas_call(
        _attn_kernel,
        out_shape=jax.ShapeDtypeStruct((bsz, n, HEADS * HEAD_DIM), F32),
        grid=(bsz, HEADS, n // tq),
        in_specs=[pl.BlockSpec((1, 1, tq, MLA_QK_PAD), lambda b, h, t: (b, h, t, 0)),
                  pl.BlockSpec((1, 1, m, MLA_QK_PAD), lambda b, h, t: (b, h, 0, 0)),
                  pl.BlockSpec((1, 1, m, HEAD_DIM), lambda b, h, t: (b, h, 0, 0))],
        out_specs=pl.BlockSpec((1, tq, HEAD_DIM), lambda b, h, t: (b, t, h)),
        compiler_params=_cparams(("parallel", "parallel", "arbitrary"), VMEM_LIMIT),
        name="attention",
    )(q, k, v)


def _level_ref(cum, blk, reverse):
    c = cum.shape[0]
    half = blk // 2
    r = half if reverse else half - 1
    if blk >= 8:
        x = cum.reshape(c // blk, blk, HEAD_DIM)
        e = jnp.broadcast_to(x[:, r:r + 1, :], x.shape)
        return e.reshape(c, HEAD_DIM)
    x = cum.reshape(c // 8, 8, HEAD_DIM)
    sub = lax.broadcasted_iota(jnp.int32, x.shape, 1)
    e = None
    for jb in range(8 // blk):
        cand = jnp.broadcast_to(x[:, jb * blk + r:jb * blk + r + 1, :], x.shape)
        e = cand if e is None else jnp.where(sub >= jb * blk, cand, e)
    return e.reshape(c, HEAD_DIM)


def _hgrn_kernel(q_ref, z_ref, i_ref, lb_ref, tri_ref, lvl_ref, o_ref, st_ref, *, chunk, reverse):
    @pl.when(pl.program_id(2) == 0)
    def _():
        st_ref[...] = jnp.zeros_like(st_ref)

    q = q_ref[0]
    qh = q * _sigmoid(q) * (HEAD_DIM ** -0.5)
    lb = lb_ref[0]
    f = lb + (1.0 - lb) * _sigmoid(z_ref[0])
    k = 1.0 - f
    g = jnp.log(f)
    v = i_ref[0]
    vb = v.astype(BF16)

    g1 = g.astype(BF16)
    r1 = g - g1.astype(F32)
    g2 = r1.astype(BF16)
    g3 = (r1 - g2.astype(F32)).astype(BF16)
    cs = _mm(tri_ref[...], jnp.concatenate([g1, g2, g3], axis=1))
    cum = cs[:, 0:HEAD_DIM] + cs[:, HEAD_DIM:2 * HEAD_DIM] + cs[:, 2 * HEAD_DIM:3 * HEAD_DIM]

    lvl = lvl_ref[...]
    att = jnp.where(lvl == 0, _nt(qh.astype(BF16), k.astype(BF16)), 0.0)
    n_lvl = chunk.bit_length() - 1
    for lv in range(1, n_lvl + 1):
        e = jnp.exp(-jnp.abs(cum - _level_ref(cum, 1 << lv, reverse)))
        att = jnp.where(lvl == lv, _nt((qh * e).astype(BF16), (k * e).astype(BF16)), att)

    st = st_ref[...]
    last = 0 if reverse else chunk - 1
    tot = cum[last:last + 1, :]
    o = _nt((qh * jnp.exp(cum)).astype(BF16), st.astype(BF16)) + _mm(att.astype(BF16), vb)
    o_ref[0, 0] = o
    kt = (k * jnp.exp(tot - cum)).astype(BF16)
    st_ref[...] = st * jnp.exp(tot) + _mm(v.T.astype(BF16), kt)


def _hgrn_consts(chunk, reverse):
    t = np.arange(chunk)[:, None]
    s = np.arange(chunk)[None, :]
    x = t ^ s
    bitlen = np.zeros_like(x)
    for b in range(chunk.bit_length()):
        bitlen = np.where(x >> b > 0, b + 1, bitlen)
    valid = (s > t) if reverse else (t > s)
    lvl = np.where(t == s, 0, np.where(valid, bitlen, -1)).astype(np.int32)
    tri = ((s >= t) if reverse else (t >= s)).astype(np.float32)
    return jnp.asarray(tri, dtype=BF16), jnp.asarray(lvl)


def _hgrn_scan(p, lb, n_lat, reverse):
    bsz, rows, _ = p.shape
    c = HG_CHUNK
    n_chunks = rows // c
    n_lat_c = n_lat // c
    n_ctx_c = n_chunks - n_lat_c
    tri, lvl = _hgrn_consts(c, reverse)
    col_f = COL_FB if reverse else COL_FF

    if reverse:
        def cidx(i):
            return n_chunks - 1 - i
    else:
        def cidx(i):
            return jnp.where(i < n_ctx_c, n_lat_c + i, i - n_ctx_c)

    def col_spec(col):
        return pl.BlockSpec((1, c, HEAD_DIM), lambda b, h, i: (b, cidx(i), col // HEAD_DIM + h))

    kern = functools.partial(_hgrn_kernel, chunk=c, reverse=reverse)
    return pl.pallas_call(
        kern,
        out_shape=jax.ShapeDtypeStruct((bsz, HEADS, rows, HEAD_DIM), F32),
        grid=(bsz, HEADS, n_chunks),
        in_specs=[col_spec(COL_HQ), col_spec(col_f), col_spec(COL_I),
                  pl.BlockSpec((1, 1, HEAD_DIM), lambda b, h, i: (h, 0, 0)),
                  pl.BlockSpec((c, c), lambda b, h, i: (0, 0)),
                  pl.BlockSpec((c, c), lambda b, h, i: (0, 0))],
        out_specs=pl.BlockSpec((1, 1, c, HEAD_DIM), lambda b, h, i: (b, h, cidx(i), 0)),
        scratch_shapes=[pltpu.VMEM((HEAD_DIM, HEAD_DIM), F32)],
        compiler_params=_cparams(("parallel", "parallel", "arbitrary"), VMEM_LIMIT),
        name="hgrn_bwd" if reverse else "hgrn_fwd",
    )(p, p, p, lb, tri, lvl)


def _merge_kernel(ym_ref, of_ref, ob_ref, go_ref, gm_ref, gh_ref, x_ref, mod_ref, hgg_ref, wout_ref,
                  gffn_ref, wr_ref, br_ref, x1_ref, h2_ref, lg_ref, y_scr):
    b = pl.program_id(0)
    for h in range(HEADS):
        sl = slice(h * HEAD_DIM, (h + 1) * HEAD_DIM)
        o = of_ref[0, h] + ob_ref[0, h]
        ms = jnp.mean(o * o, axis=-1, keepdims=True)
        g = go_ref[0, :, sl]
        yh = o * lax.rsqrt(ms + EPS) * hgg_ref[...] * (g * _sigmoid(g))
        y = _sigmoid(gm_ref[0, :, sl]) * ym_ref[0, :, sl] + _sigmoid(gh_ref[0, :, sl]) * yh
        y_scr[:, sl] = y.astype(BF16)
    mix = _mm(y_scr[...], wout_ref[...])
    m = mod_ref[pl.ds(b, 1), :]
    x1 = x_ref[0] + m[:, 2 * D_MODEL:3 * D_MODEL] * mix
    x1_ref[0] = x1
    ms = jnp.mean(x1 * x1, axis=-1, keepdims=True)
    h2 = (x1 * lax.rsqrt(ms + EPS) * gffn_ref[...]) * (1.0 + m[:, 4 * D_MODEL:5 * D_MODEL]) + m[:, 3 * D_MODEL:4 * D_MODEL]
    h_hi = h2.astype(BF16)
    u = pltpu.bitcast(h_hi.astype(F32), jnp.uint32)
    half = D_MODEL // 2
    h2_ref[...] = (u[:, 0:half] >> 16) | u[:, half:D_MODEL]
    h_lo = (h2 - h_hi.astype(F32)).astype(BF16)
    w_hi, w_lo = _split2(wr_ref[...])
    lg_ref[...] = _nt(w_hi, h_hi) + _nt(w_lo, h_hi) + _nt(w_hi, h_lo) + br_ref[...]


def _merge(y_mla, o_f, o_b, p, x, mod, hg_g, w_out, g_ffn, w_r_t, b_r):
    bsz, n, d = x.shape
    tm = _tile(n, 256, 128)
    nt = n // tm

    def pcol(col):
        return pl.BlockSpec((1, tm, d), lambda b, t: (b, t, col // d))

    tok = lambda b, t: (b, t, 0)
    const2 = lambda b, t: (0, 0)
    return pl.pallas_call(
        _merge_kernel,
        out_shape=(jax.ShapeDtypeStruct((bsz, n, d), F32),
                   jax.ShapeDtypeStruct((bsz * n, d // 2), jnp.uint32),
                   jax.ShapeDtypeStruct((N_EXPERTS, bsz * n), F32)),
        grid=(bsz, nt),
        in_specs=[pl.BlockSpec((1, tm, d), tok),
                  pl.BlockSpec((1, HEADS, tm, HEAD_DIM), lambda b, t: (b, 0, t, 0)),
                  pl.BlockSpec((1, HEADS, tm, HEAD_DIM), lambda b, t: (b, 0, t, 0)),
                  pcol(COL_GO), pcol(COL_MM), pcol(COL_MH),
                  pl.BlockSpec((1, tm, d), tok),
                  pl.BlockSpec(mod.shape, const2),
                  pl.BlockSpec((1, HEAD_DIM), const2),
                  pl.BlockSpec((d, d), const2),
                  pl.BlockSpec((1, d), const2),
                  pl.BlockSpec((N_EXPERTS, d), const2),
                  pl.BlockSpec((N_EXPERTS, 1), const2)],
        out_specs=(pl.BlockSpec((1, tm, d), tok),
                   pl.BlockSpec((tm, d // 2), lambda b, t: (b * nt + t, 0)),
                   pl.BlockSpec((N_EXPERTS, tm), lambda b, t: (0, b * nt + t))),
        scratch_shapes=[pltpu.VMEM((tm, d), BF16)],
        compiler_params=_cparams(("parallel", "parallel"), VMEM_LIMIT),
        name="merge",
    )(y_mla, o_f, o_b, p, p, p, x, mod, hg_g, w_out, g_ffn, w_r_t, b_r)


def _router_kernel(lg_ref, upper_ref, idx_ref, prob_ref, rank_ref, cnt_ref, run_ref, *, tm):
    @pl.when(pl.program_id(0) == 0)
    def _():
        run_ref[...] = jnp.zeros_like(run_ref)

    l = lg_ref[...]
    eidx = lax.broadcasted_iota(jnp.int32, (N_EXPERTS, tm), 0)
    vals, sel = [], []
    for _ in range(TOP_K):
        m = jnp.max(l, axis=0, keepdims=True)
        first = jnp.min(jnp.where(l == m, eidx, N_EXPERTS), axis=0, keepdims=True)
        vals.append(m)
        sel.append(first)
        l = jnp.where(eidx == first, -jnp.inf, l)
    ex = [jnp.exp(v - vals[0]) for v in vals]
    inv = 1.0 / (ex[0] + ex[1] + ex[2] + ex[3])
    onehot = jnp.zeros((N_EXPERTS, tm), F32)
    for k in range(TOP_K):
        onehot = onehot + jnp.where(eidx == sel[k], 1.0, 0.0)
    before = _mm(onehot.astype(BF16), upper_ref[...]) + run_ref[:, 0:1]
    for k in range(TOP_K):
        idx_ref[k:k + 1, :] = sel[k]
        prob_ref[k:k + 1, :] = ex[k] * inv
        rank_ref[k:k + 1, :] = jnp.sum(jnp.where(eidx == sel[k], before, 0.0), axis=0, keepdims=True).astype(jnp.int32)
    run_ref[...] = run_ref[...] + jnp.sum(onehot, axis=1, keepdims=True)
    cnt_ref[...] = run_ref[...]


def _router(logits_t):
    _, t = logits_t.shape
    tm = _tile(t, 1024, 128)
    upper = jnp.asarray(np.triu(np.ones((tm, tm), np.float32), 1), dtype=BF16)
    kern = functools.partial(_router_kernel, tm=tm)
    tok = pl.BlockSpec((TOP_K, tm), lambda i: (0, i))
    return pl.pallas_call(
        kern,
        out_shape=(jax.ShapeDtypeStruct((TOP_K, t), jnp.int32),
                   jax.ShapeDtypeStruct((TOP_K, t), F32),
                   jax.ShapeDtypeStruct((TOP_K, t), jnp.int32),
                   jax.ShapeDtypeStruct((N_EXPERTS, 128), F32)),
        grid=(t // tm,),
        in_specs=[pl.BlockSpec((N_EXPERTS, tm), lambda i: (0, i)),
                  pl.BlockSpec((tm, tm), lambda i: (0, 0))],
        out_specs=(tok, tok, tok, pl.BlockSpec((N_EXPERTS, 128), lambda i: (0, 0))),
        scratch_shapes=[pltpu.VMEM((N_EXPERTS, 128), F32)],
        compiler_params=_cparams(("arbitrary",), VMEM_LIMIT),
        name="router",
    )(logits_t, upper)


def _dispatch_kernel(dest_ref, h2_ref, xs_in_ref, xs_ref, sem, *, tm):
    del xs_in_ref
    base = pl.program_id(0) * tm

    def row_copy(src_row, dst_row):
        return pltpu.make_async_copy(h2_ref.at[pl.ds(src_row, 1)], xs_ref.at[pl.ds(dst_row, 1)], sem)

    def body(r, carry):
        for k in range(TOP_K):
            row_copy(base + r, dest_ref[0, 0, k * tm + r]).start()
        return carry

    lax.fori_loop(0, tm, body, 0)
    pltpu.make_async_copy(h2_ref.at[pl.ds(0, TOP_K * tm)], xs_ref.at[pl.ds(0, TOP_K * tm)], sem).wait()


def _dispatch(dest_tiles, h2p, n_rows, tm):
    t, w = h2p.shape
    xs0 = jnp.zeros((n_rows, w), h2p.dtype)
    kern = functools.partial(_dispatch_kernel, tm=tm)
    return pl.pallas_call(
        kern,
        out_shape=jax.ShapeDtypeStruct((n_rows, w), h2p.dtype),
        grid=(t // tm,),
        in_specs=[pl.BlockSpec((1, 1, TOP_K * tm), lambda i: (i, 0, 0), memory_space=pltpu.SMEM),
                  pl.BlockSpec(memory_space=pl.ANY),
                  pl.BlockSpec(memory_space=pl.ANY)],
        out_specs=pl.BlockSpec(memory_space=pl.ANY),
        scratch_shapes=[pltpu.SemaphoreType.DMA],
        input_output_aliases={2: 0},
        compiler_params=_cparams(("arbitrary",), VMEM_LIMIT),
        name="dispatch",
    )(dest_tiles, h2p, xs0)


def _ffn_kernel(be_ref, nu_ref, x_ref, wgu_ref, bgu_ref, wd_ref, bd_ref, y_ref):
    del be_ref

    @pl.when(pl.program_id(0) < nu_ref[0])
    def _():
        u = x_ref[...]
        lo = pltpu.bitcast(u << 16, F32).astype(BF16)
        hi = pltpu.bitcast(u & jnp.uint32(0xFFFF0000), F32).astype(BF16)
        x = jnp.concatenate([lo, hi], axis=1)
        gu = _mm(x, wgu_ref[0]) + bgu_ref[0]
        glu = jnp.minimum(gu[:, 0:D_EXPERT], SWIGLU_LIMIT)
        lin = jnp.clip(gu[:, D_EXPERT:2 * D_EXPERT], -SWIGLU_LIMIT, SWIGLU_LIMIT)
        act = glu * _sigmoid(SWIGLU_ALPHA * glu) * (lin + 1.0)
        y_ref[...] = _mm(act.astype(BF16), wd_ref[0]) + bd_ref[0]

    @pl.when(pl.program_id(0) >= nu_ref[0])
    def _():
        y_ref[...] = jnp.zeros_like(y_ref)


def _ffn(block_e, n_used, xs, w_gu, b_gu, w_d, b_d):
    n_rows, w = xs.shape
    bm = FFN_ROWS
    nblk = n_rows // bm
    d = w * 2

    def xmap(i, be, nu):
        return (jnp.minimum(i, nu[0] - 1), 0)

    def wmap(i, be, nu):
        return (be[i], 0, 0)

    grid_spec = pltpu.PrefetchScalarGridSpec(
        num_scalar_prefetch=2,
        grid=(nblk,),
        in_specs=[pl.BlockSpec((bm, w), xmap),
                  pl.BlockSpec((1, d, 2 * D_EXPERT), wmap),
                  pl.BlockSpec((1, 1, 2 * D_EXPERT), wmap),
                  pl.BlockSpec((1, D_EXPERT, d), wmap),
                  pl.BlockSpec((1, 1, d), wmap)],
        out_specs=pl.BlockSpec((bm, d), lambda i, be, nu: (i, 0)),
    )
    return pl.pallas_call(
        _ffn_kernel,
        out_shape=jax.ShapeDtypeStruct((n_rows, d), F32),
        grid_spec=grid_spec,
        compiler_params=_cparams(("arbitrary",), VMEM_LIMIT),
        name="ffn",
    )(block_e, n_used, xs, w_gu, b_gu, w_d, b_d)


def _combine_kernel(dest_ref, y_ref, p_ref, x1_ref, mod_ref, g_ref, o_ref, buf_ref, sem, *, tm):
    b = pl.program_id(0)

    def body(r, carry):
        for k in range(TOP_K):
            pltpu.make_async_copy(y_ref.at[pl.ds(dest_ref[0, 0, k * tm + r], 1)],
                                  buf_ref.at[k, pl.ds(r, 1)], sem).start()
        return carry

    lax.fori_loop(0, tm, body, 0)
    for k in range(TOP_K):
        pltpu.make_async_copy(y_ref.at[pl.ds(0, tm)], buf_ref.at[k], sem).wait()
    p = p_ref[...]
    acc = p[:, 0:1] * buf_ref[0]
    for k in range(1, TOP_K):
        acc = acc + p[:, k:k + 1] * buf_ref[k]
    m = mod_ref[pl.ds(b, 1), :]
    xo = x1_ref[0] + m[:, 5 * D_MODEL:6 * D_MODEL] * acc
    ms = jnp.mean(xo * xo, axis=-1, keepdims=True)
    o_ref[0] = xo * lax.rsqrt(ms + EPS) * g_ref[...]


def _combine(dest_tiles, y, probs_t, x1, mod, g_fin, tm):
    bsz, n, d = x1.shape
    nt = n // tm
    kern = functools.partial(_combine_kernel, tm=tm)
    return pl.pallas_call(
        kern,
        out_shape=jax.ShapeDtypeStruct((bsz, n, d), F32),
        grid=(bsz, nt),
        in_specs=[pl.BlockSpec((1, 1, TOP_K * tm), lambda b, t: (b * nt + t, 0, 0), memory_space=pltpu.SMEM),
                  pl.BlockSpec(memory_space=pl.ANY),
                  pl.BlockSpec((tm, TOP_K), lambda b, t: (b * nt + t, 0)),
                  pl.BlockSpec((1, tm, d), lambda b, t: (b, t, 0)),
                  pl.BlockSpec(mod.shape, lambda b, t: (0, 0)),
                  pl.BlockSpec((1, d), lambda b, t: (0, 0))],
        out_specs=pl.BlockSpec((1, tm, d), lambda b, t: (b, t, 0)),
        scratch_shapes=[pltpu.VMEM((TOP_K, tm, d), F32), pltpu.SemaphoreType.DMA],
        compiler_params=_cparams(("arbitrary", "arbitrary"), VMEM_LIMIT),
        name="combine",
    )(dest_tiles, y, probs_t, x1, mod, g_fin)


def _prep_w_in(w_in):
    q_a, kv, hq, ff, fb, hi, go, mm, mh = jnp.split(
        w_in, np.cumsum((MLA_Q_RANK, MLA_KV_RANK + MLA_ROPE) + (D_MODEL,) * 6).tolist(), axis=1)
    c_kv, kr = kv[:, :MLA_KV_RANK], kv[:, MLA_KV_RANK:]
    half = MLA_ROPE // 2
    kr_sw = jnp.concatenate([-kr[:, half:], kr[:, :half]], axis=1)
    return jnp.concatenate([hq, ff, fb, hi, go, mm, mh, q_a, c_kv, kr, kr_sw], axis=1).astype(BF16)


def _prep_w_q(w_q_b):
    w = w_q_b.reshape(MLA_Q_RANK, HEADS, MLA_QK)
    w = jnp.pad(w, ((0, 0), (0, 0), (0, MLA_QK_PAD - MLA_QK)))
    return w.reshape(MLA_Q_RANK, HEADS * MLA_QK_PAD).astype(BF16)


def _prep_w_kv(w_kv_b):
    w = w_kv_b.reshape(MLA_KV_RANK, HEADS, 2 * HEAD_DIM)
    k = w[:, :, :HEAD_DIM].reshape(MLA_KV_RANK, HEADS * HEAD_DIM)
    v = w[:, :, HEAD_DIM:].reshape(MLA_KV_RANK, HEADS * HEAD_DIM)
    return jnp.concatenate([k, v], axis=1).astype(BF16)


def _rope_tables(n_lat, n_ctx):
    rows = n_lat // GRID_W
    row = jnp.repeat(jnp.arange(rows), GRID_W).astype(F32)
    col = jnp.tile(jnp.arange(GRID_W), rows).astype(F32)
    n_freq = MLA_ROPE // 4
    inv = ROPE_BASE ** (-jnp.arange(n_freq, dtype=F32) / n_freq)
    ang = jnp.concatenate([row[:, None] * inv, col[:, None] * inv], axis=-1)
    cos, sin = jnp.cos(ang), jnp.sin(ang)
    z64 = jnp.zeros((n_lat, 64), F32)
    cos_q = jnp.concatenate([cos, cos, z64], axis=1)
    sin_q = jnp.concatenate([-sin, sin, z64], axis=1)
    cos_k = jnp.concatenate([cos, cos, z64], axis=1)
    sin_k = jnp.concatenate([sin, sin, z64], axis=1)
    ctx_cos = jnp.concatenate([jnp.ones((n_ctx, 64), F32), jnp.zeros((n_ctx, 64), F32)], axis=1)
    cos_k = jnp.concatenate([cos_k, ctx_cos], axis=0)
    sin_k = jnp.concatenate([sin_k, jnp.zeros((n_ctx, 128), F32)], axis=0)
    return cos_q, sin_q, cos_k, sin_k


def kernel(x, c, ctx, c_ctx, w_mod, b_mod, norm_mix_g, w_in, mla_q_norm_g, w_q_b, mla_kv_norm_g, w_kv_b,
           hg_lb_logits, hg_norm_g, w_out, norm_ffn_g, w_router, b_router, w_gate_up, b_gate_up, w_down,
           b_down, final_norm_g):
    bsz, n_lat, d = x.shape
    n_ctx = ctx.shape[1]
    assert d == D_MODEL and w_mod.shape[0] == 1
    assert n_lat % HG_CHUNK == 0 and n_ctx % HG_CHUNK == 0 and n_lat % GRID_W == 0
    t_tok = bsz * n_lat

    mod_rows = -(-(bsz + 1) // 8) * 8
    cc = jnp.concatenate([c, c_ctx[None, :], jnp.zeros((mod_rows - bsz - 1, d), F32)], axis=0)
    mod = _modulation(cc, w_mod[0], b_mod[0][None, :])

    xa = jnp.concatenate([x, ctx], axis=1)
    p = _inproj(xa, mod, norm_mix_g[0][None, :], _prep_w_in(w_in[0]), n_lat)

    cos_q, sin_q, cos_k, sin_k = _rope_tables(n_lat, n_ctx)
    q = _mla_q(p, mla_q_norm_g[0][None, :], _prep_w_q(w_q_b[0]), cos_q, sin_q, n_lat)
    k, v = _mla_kv(p, mla_kv_norm_g[0][None, :], _prep_w_kv(w_kv_b[0]), cos_k, sin_k)
    y_mla = _attention(q, k, v)

    lb = jax.nn.softmax(hg_lb_logits.astype(F32), axis=1)[:, 0, :]
    o_f = _hgrn_scan(p, lb[0].reshape(HEADS, 1, HEAD_DIM), n_lat, reverse=False)
    o_b = _hgrn_scan(p, lb[1].reshape(HEADS, 1, HEAD_DIM), n_lat, reverse=True)

    x1, h2p, logits_t = _merge(y_mla, o_f, o_b, p, x, mod, hg_norm_g[0][None, :], w_out[0].astype(BF16),
                               norm_ffn_g[0][None, :], w_router[0].T, b_router[0][:, None])

    idx_t, prob_t, rank_t, cnt = _router(logits_t)

    counts = cnt[:, 0].astype(jnp.int32)
    padded = (counts + FFN_ROWS - 1) // FFN_ROWS * FFN_ROWS
    pad_end = jnp.cumsum(padded)
    pad_start = pad_end - padded
    dest = pad_start[idx_t] + rank_t
    n_rows = -(-(t_tok * TOP_K + N_EXPERTS * (FFN_ROWS - 1)) // FFN_ROWS) * FFN_ROWS
    nblk = n_rows // FFN_ROWS
    block_start = jnp.arange(nblk, dtype=jnp.int32) * FFN_ROWS
    block_e = jnp.minimum(jnp.searchsorted(pad_end, block_start, side='right'), N_EXPERTS - 1).astype(jnp.int32)
    n_used = (pad_end[-1:] // FFN_ROWS).astype(jnp.int32)

    tm_c = _tile(n_lat, 256)
    dest_tiles = dest.reshape(TOP_K, t_tok // tm_c, tm_c).transpose(1, 0, 2).reshape(t_tok // tm_c, 1, TOP_K * tm_c)

    xs = _dispatch(dest_tiles, h2p, n_rows, tm_c)

    w_gu = jnp.concatenate([w_gate_up[0][:, :, 0::2], w_gate_up[0][:, :, 1::2]], axis=-1).astype(BF16)
    b_gu = jnp.concatenate([b_gate_up[0][:, 0::2], b_gate_up[0][:, 1::2]], axis=-1)[:, None, :]
    y = _ffn(block_e, n_used, xs, w_gu, b_gu, w_down[0].astype(BF16), b_down[0][:, None, :])

    return _combine(dest_tiles, y, prob_t.T, x1, mod, final_norm_g[None, :], tm_c)
```

```python
import functools

import jax
import jax.numpy as jnp
import numpy as np
from jax import lax
from jax.experimental import pallas as pl
from jax.experimental.pallas import tpu as pltpu

F32 = jnp.float32
BF16 = jnp.bfloat16

D_MODEL = 1024
EPS = 1e-6
LOG2_E = 1.4426950408889634
N_MOD = 6
GRID_W = 64
ROPE_BASE = 10000.0

HEADS = 8
HEAD_DIM = 128
MLA_ROPE = 64
MLA_QK = HEAD_DIM + MLA_ROPE
MLA_QK_PAD = 256
MLA_Q_RANK = 768
MLA_KV_RANK = 256

N_EXPERTS = 32
TOP_K = 4
D_EXPERT = 1024
SWIGLU_LIMIT = 7.0
SWIGLU_ALPHA = 1.702

HG_CHUNK = 128
FFN_ROWS = 512

COL_HQ, COL_FF, COL_FB, COL_I, COL_GO, COL_MM, COL_MH = (i * D_MODEL for i in range(7))
COL_QA = 7 * D_MODEL
COL_CKV = COL_QA + MLA_Q_RANK
COL_KR = COL_CKV + MLA_KV_RANK
N_PROJ = COL_KR + 2 * MLA_ROPE
PROJ_TN = 640

VMEM_LIMIT = 56 * 1024 * 1024


def _cparams(sem, vmem=None):
    return pltpu.CompilerParams(dimension_semantics=sem, vmem_limit_bytes=vmem)


def _tile(n, pref, mult=8):
    best = None
    for t in range(mult, min(n, pref) + 1, mult):
        if n % t == 0:
            best = t
    assert best is not None, (n, pref, mult)
    return best


def _nt(a, b):
    return lax.dot_general(a, b, (((1,), (1,)), ((), ())), preferred_element_type=F32)


def _mm(a, b):
    return jnp.dot(a, b, preferred_element_type=F32)


def _split2(a):
    hi = a.astype(BF16)
    lo = (a - hi.astype(F32)).astype(BF16)
    return hi, lo


def _sigmoid(x):
    return 1.0 / (1.0 + jnp.exp(-x))


def _mod_kernel(c_ref, w_ref, b_ref, o_ref):
    c = c_ref[...]
    s = c * _sigmoid(c)
    s_hi, s_lo = _split2(s)
    w_hi, w_lo = _split2(w_ref[...])
    o_ref[...] = _mm(s_hi, w_hi) + _mm(s_lo, w_hi) + _mm(s_hi, w_lo) + b_ref[...]


def _modulation(cc, w_mod, b_mod):
    r, d = cc.shape
    n = w_mod.shape[1]
    tn = _tile(n, 1536, 128)
    return pl.pallas_call(
        _mod_kernel,
        out_shape=jax.ShapeDtypeStruct((r, n), F32),
        grid=(n // tn,),
        in_specs=[pl.BlockSpec((r, d), lambda j: (0, 0)),
                  pl.BlockSpec((d, tn), lambda j: (0, j)),
                  pl.BlockSpec((1, tn), lambda j: (0, j))],
        out_specs=pl.BlockSpec((r, tn), lambda j: (0, j)),
        compiler_params=_cparams(("parallel",), VMEM_LIMIT),
        name="mod",
    )(cc, w_mod, b_mod)


def _inproj_kernel(x_ref, mod_ref, g_ref, w_ref, o_ref, hn_ref, *, n_lat, tm, ctx_row):
    b = pl.program_id(0)
    t = pl.program_id(1)

    @pl.when(pl.program_id(2) == 0)
    def _():
        x = x_ref[0]
        ms = jnp.mean(x * x, axis=-1, keepdims=True)
        y = x * lax.rsqrt(ms + EPS) * g_ref[...]
        row = t * tm + lax.broadcasted_iota(jnp.int32, (tm, 1), 0)
        is_ctx = row >= n_lat
        m_lat = mod_ref[pl.ds(b, 1), :]
        m_ctx = mod_ref[pl.ds(ctx_row, 1), :]
        shift = jnp.where(is_ctx, m_ctx[:, 0:D_MODEL], m_lat[:, 0:D_MODEL])
        scale = jnp.where(is_ctx, m_ctx[:, D_MODEL:2 * D_MODEL], m_lat[:, D_MODEL:2 * D_MODEL])
        hn_ref[...] = (y * (1.0 + scale) + shift).astype(BF16)

    o_ref[0] = _mm(hn_ref[...], w_ref[...])


def _inproj(xa, mod, g, w_all, n_lat):
    bsz, rows, d = xa.shape
    tm = _tile(rows, 1152)
    n = w_all.shape[1]
    tn = PROJ_TN
    kern = functools.partial(_inproj_kernel, n_lat=n_lat, tm=tm, ctx_row=bsz)
    return pl.pallas_call(
        kern,
        out_shape=jax.ShapeDtypeStruct((bsz, rows, n), F32),
        grid=(bsz, rows // tm, n // tn),
        in_specs=[pl.BlockSpec((1, tm, d), lambda b, t, j: (b, t, 0)),
                  pl.BlockSpec(mod.shape, lambda b, t, j: (0, 0)),
                  pl.BlockSpec((1, d), lambda b, t, j: (0, 0)),
                  pl.BlockSpec((d, tn), lambda b, t, j: (0, j))],
        out_specs=pl.BlockSpec((1, tm, tn), lambda b, t, j: (b, t, j)),
        scratch_shapes=[pltpu.VMEM((tm, d), BF16)],
        compiler_params=_cparams(("parallel", "parallel", "arbitrary"), VMEM_LIMIT),
        name="inproj",
    )(xa, mod, g, w_all)


def _mlaq_kernel(a0_ref, a1_ref, a2_ref, g_ref, w_ref, cos_ref, sin_ref, o_ref, *, tm):
    parts = [a0_ref[0], a1_ref[0], a2_ref[0]]
    ss = sum(jnp.sum(p * p, axis=-1, keepdims=True) for p in parts)
    r = lax.rsqrt(ss * (1.0 / MLA_Q_RANK) + EPS)
    acc = None
    for j, p in enumerate(parts):
        hj = (p * r * g_ref[:, j * 256:(j + 1) * 256]).astype(BF16)
        d = _mm(hj, w_ref[j * 256:(j + 1) * 256, :])
        acc = d if acc is None else acc + d
    lane = lax.broadcasted_iota(jnp.int32, (tm, HEAD_DIM), 1)
    scale = MLA_QK ** -0.5
    cos = cos_ref[...]
    sin = sin_ref[...]
    for h in range(HEADS):
        nope = acc[:, h * MLA_QK_PAD:h * MLA_QK_PAD + HEAD_DIM]
        rp = acc[:, h * MLA_QK_PAD + HEAD_DIM:(h + 1) * MLA_QK_PAD]
        swapped = jnp.where(lane < 32, pltpu.roll(rp, 96, 1), pltpu.roll(rp, 32, 1))
        rot = rp * cos + swapped * sin
        o_ref[0, h, :, 0:HEAD_DIM] = (nope * scale).astype(BF16)
        o_ref[0, h, :, HEAD_DIM:MLA_QK_PAD] = (rot * scale).astype(BF16)


def _mla_q(p, g, w, cos_q, sin_q, n_lat):
    bsz = p.shape[0]
    tm = _tile(n_lat, 512)
    cb = COL_QA // 256
    kern = functools.partial(_mlaq_kernel, tm=tm)
    return pl.pallas_call(
        kern,
        out_shape=jax.ShapeDtypeStruct((bsz, HEADS, n_lat, MLA_QK_PAD), BF16),
        grid=(bsz, n_lat // tm),
        in_specs=[pl.BlockSpec((1, tm, 256), lambda b, t: (b, t, cb)),
                  pl.BlockSpec((1, tm, 256), lambda b, t: (b, t, cb + 1)),
                  pl.BlockSpec((1, tm, 256), lambda b, t: (b, t, cb + 2)),
                  pl.BlockSpec((1, MLA_Q_RANK), lambda b, t: (0, 0)),
                  pl.BlockSpec(w.shape, lambda b, t: (0, 0)),
                  pl.BlockSpec((tm, HEAD_DIM), lambda b, t: (t, 0)),
                  pl.BlockSpec((tm, HEAD_DIM), lambda b, t: (t, 0))],
        out_specs=pl.BlockSpec((1, HEADS, tm, MLA_QK_PAD), lambda b, t: (b, 0, t, 0)),
        compiler_params=_cparams(("parallel", "parallel"), VMEM_LIMIT),
        name="mla_q",
    )(p, p, p, g, w, cos_q, sin_q)


def _mlakv_kernel(c_ref, kr_ref, g_ref, w_ref, cos_ref, sin_ref, k_ref, v_ref):
    c = c_ref[0]
    ms = jnp.mean(c * c, axis=-1, keepdims=True)
    hn = (c * lax.rsqrt(ms + EPS) * g_ref[...]).astype(BF16)
    kv = _mm(hn, w_ref[...])
    grp = kr_ref[0]
    rot = (grp * cos_ref[...] + pltpu.roll(grp, 64, 1) * sin_ref[...]).astype(BF16)
    for h in range(HEADS):
        k_ref[0, h, :, 0:HEAD_DIM] = kv[:, h * HEAD_DIM:(h + 1) * HEAD_DIM].astype(BF16)
        k_ref[0, h, :, HEAD_DIM:MLA_QK_PAD] = rot
        v_ref[0, h] = kv[:, D_MODEL + h * HEAD_DIM:D_MODEL + (h + 1) * HEAD_DIM].astype(BF16)


def _mla_kv(p, g, w, cos_k, sin_k):
    bsz, rows, _ = p.shape
    tm = _tile(rows, 768)
    return pl.pallas_call(
        _mlakv_kernel,
        out_shape=(jax.ShapeDtypeStruct((bsz, HEADS, rows, MLA_QK_PAD), BF16),
                   jax.ShapeDtypeStruct((bsz, HEADS, rows, HEAD_DIM), BF16)),
        grid=(bsz, rows // tm),
        in_specs=[pl.BlockSpec((1, tm, MLA_KV_RANK), lambda b, t: (b, t, COL_CKV // MLA_KV_RANK)),
                  pl.BlockSpec((1, tm, 128), lambda b, t: (b, t, COL_KR // 128)),
                  pl.BlockSpec((1, MLA_KV_RANK), lambda b, t: (0, 0)),
                  pl.BlockSpec(w.shape, lambda b, t: (0, 0)),
                  pl.BlockSpec((tm, 128), lambda b, t: (t, 0)),
                  pl.BlockSpec((tm, 128), lambda b, t: (t, 0))],
        out_specs=(pl.BlockSpec((1, HEADS, tm, MLA_QK_PAD), lambda b, t: (b, 0, t, 0)),
                   pl.BlockSpec((1, HEADS, tm, HEAD_DIM), lambda b, t: (b, 0, t, 0))),
        compiler_params=_cparams(("parallel", "parallel"), VMEM_LIMIT),
        name="mla_kv",
    )(p, p, g, w, cos_k, sin_k)


def _attn_kernel(q_ref, k_ref, v_ref, o_ref):
    s = _nt(q_ref[0, 0], k_ref[0, 0])
    m = jnp.max(s, axis=-1, keepdims=True)
    p = jnp.exp(s - m)
    l = jnp.sum(p, axis=-1, keepdims=True)
    o_ref[0] = _mm(p.astype(BF16), v_ref[0, 0]) * (1.0 / l)


def _attention(q, k, v):
    bsz, _, n, _ = q.shape
    m = k.shape[2]
    tq = _tile(n, 512)
    return pl.pallas_call(
        _attn_kernel,
        out_shape=jax.ShapeDtypeStruct((bsz, n, HEADS * HEAD_DIM), F32),
        grid=(bsz, HEADS, n // tq),
        in_specs=[pl.BlockSpec((1, 1, tq, MLA_QK_PAD), lambda b, h, t: (b, h, t, 0)),
                  pl.BlockSpec((1, 1, m, MLA_QK_PAD), lambda b, h, t: (b, h, 0, 0)),
                  pl.BlockSpec((1, 1, m, HEAD_DIM), lambda b, h, t: (b, h, 0, 0))],
        out_specs=pl.BlockSpec((1, tq, HEAD_DIM), lambda b, h, t: (b, t, h)),
        compiler_params=_cparams(("parallel", "parallel", "arbitrary"), VMEM_LIMIT),
        name="attention",
    )(q, k, v)


def _level_ref(cum, blk, reverse):
    c = cum.shape[0]
    half = blk // 2
    r = half if reverse else half - 1
    if blk >= 8:
        x = cum.reshape(c // blk, blk, HEAD_DIM)
        e = jnp.broadcast_to(x[:, r:r + 1, :], x.shape)
        return e.reshape(c, HEAD_DIM)
    x = cum.reshape(c // 8, 8, HEAD_DIM)
    sub = lax.broadcasted_iota(jnp.int32, x.shape, 1)
    e = None
    for jb in range(8 // blk):
        cand = jnp.broadcast_to(x[:, jb * blk + r:jb * blk + r + 1, :], x.shape)
        e = cand if e is None else jnp.where(sub >= jb * blk, cand, e)
    return e.reshape(c, HEAD_DIM)


def _hgrn_chunk(q, z, v, lb, tri, lvl, st, *, chunk, reverse):
    qb = (q * _sigmoid(q) * (HEAD_DIM ** -0.5)).astype(BF16)
    f = lb + (1.0 - lb) * _sigmoid(z)
    kb = (1.0 - f).astype(BF16)
    g = jnp.log(f) * LOG2_E
    vb = v.astype(BF16)

    g_hi, g_lo = _split2(g)
    cs = _mm(tri, jnp.concatenate([g_hi, g_lo], axis=1))
    cum = cs[:, 0:HEAD_DIM] + cs[:, HEAD_DIM:2 * HEAD_DIM]

    att = jnp.where(lvl == 0, _nt(qb, kb).astype(BF16), jnp.zeros((), BF16))
    n_lvl = chunk.bit_length() - 1
    for lv in range(1, n_lvl + 1):
        zrel = cum - _level_ref(cum, 1 << lv, reverse)
        neg_abs = pltpu.bitcast(pltpu.bitcast(zrel, jnp.uint32) | jnp.uint32(0x80000000), F32)
        e = jnp.exp2(neg_abs).astype(BF16)
        att = jnp.where(lvl == lv, _nt(qb * e, kb * e).astype(BF16), att)

    last = 0 if reverse else chunk - 1
    tot = cum[last:last + 1, :]
    o = _nt(qb * jnp.exp2(cum).astype(BF16), st.astype(BF16)) + _mm(att, vb)
    kt = kb * jnp.exp2(tot - cum).astype(BF16)
    return o, st * jnp.exp2(tot) + _mm(v.T.astype(BF16), kt)


def _hgrn_kernel(qf_ref, zf_ref, if_ref, qb_ref, zb_ref, ib_ref, lb_ref, tri_ref, lvl_ref, of_ref, ob_ref, st_ref,
                 *, chunk, group):
    @pl.when(pl.program_id(2) == 0)
    def _():
        st_ref[...] = jnp.zeros_like(st_ref)

    ins = ((qf_ref, zf_ref, if_ref), (qb_ref, zb_ref, ib_ref))
    outs = (of_ref, ob_ref)
    chains = [(d, j) for j in range(group) for d in range(2)]
    states = [st_ref[d, j] for d, j in chains]
    results = []
    for (d, j), st in zip(chains, states):
        q_ref, z_ref, i_ref = ins[d]
        sl = slice(j * HEAD_DIM, (j + 1) * HEAD_DIM)
        results.append(_hgrn_chunk(q_ref[0, :, sl], z_ref[0, :, sl], i_ref[0, :, sl], lb_ref[d, j], tri_ref[d],
                                   lvl_ref[d], st, chunk=chunk, reverse=bool(d)))
    for (d, j), (o, st) in zip(chains, results):
        outs[d][0, j] = o
        st_ref[d, j] = st


def _hgrn_consts(chunk, reverse):
    t = np.arange(chunk)[:, None]
    s = np.arange(chunk)[None, :]
    x = t ^ s
    bitlen = np.zeros_like(x)
    for b in range(chunk.bit_length()):
        bitlen = np.where(x >> b > 0, b + 1, bitlen)
    valid = (s > t) if reverse else (t > s)
    lvl = np.where(t == s, 0, np.where(valid, bitlen, -1)).astype(np.int32)
    tri = ((s >= t) if reverse else (t >= s)).astype(np.float32)
    return tri, lvl


HG_GROUP = 2


def _hgrn_scans(p, lb, n_lat):
    bsz, rows, _ = p.shape
    c = HG_CHUNK
    grp = HG_GROUP
    n_chunks = rows // c
    n_lat_c = n_lat // c
    n_ctx_c = n_chunks - n_lat_c
    consts = [_hgrn_consts(c, False), _hgrn_consts(c, True)]
    tri = jnp.asarray(np.stack([consts[0][0], consts[1][0]]), dtype=BF16)
    lvl = jnp.asarray(np.stack([consts[0][1], consts[1][1]]).astype(np.float32), dtype=BF16)

    def cidx_f(i):
        return jnp.where(i < n_ctx_c, n_lat_c + i, i - n_ctx_c)

    def cidx_b(i):
        return n_chunks - 1 - i

    def col_spec(col, cidx):
        return pl.BlockSpec((1, c, grp * HEAD_DIM), lambda b, h, i: (b, cidx(i), col // (grp * HEAD_DIM) + h))

    def out_spec(cidx):
        return pl.BlockSpec((1, grp, c, HEAD_DIM), lambda b, h, i: (b, h, cidx(i), 0))

    o_shape = jax.ShapeDtypeStruct((bsz, HEADS, rows, HEAD_DIM), F32)
    kern = functools.partial(_hgrn_kernel, chunk=c, group=grp)
    return pl.pallas_call(
        kern,
        out_shape=(o_shape, o_shape),
        grid=(bsz, HEADS // grp, n_chunks),
        in_specs=[col_spec(COL_HQ, cidx_f), col_spec(COL_FF, cidx_f), col_spec(COL_I, cidx_f),
                  col_spec(COL_HQ, cidx_b), col_spec(COL_FB, cidx_b), col_spec(COL_I, cidx_b),
                  pl.BlockSpec((2, grp, 1, HEAD_DIM), lambda b, h, i: (0, h, 0, 0)),
                  pl.BlockSpec((2, c, c), lambda b, h, i: (0, 0, 0)),
                  pl.BlockSpec((2, c, c), lambda b, h, i: (0, 0, 0))],
        out_specs=(out_spec(cidx_f), out_spec(cidx_b)),
        scratch_shapes=[pltpu.VMEM((2, grp, HEAD_DIM, HEAD_DIM), F32)],
        compiler_params=_cparams(("parallel", "parallel", "arbitrary"), VMEM_LIMIT),
        name="hgrn",
    )(p, p, p, p, p, p, lb, tri, lvl)


def _merge_kernel(ym_ref, of_ref, ob_ref, go_ref, gm_ref, gh_ref, x_ref, mod_ref, hgg_ref, wout_ref,
                  gffn_ref, wr_ref, br_ref, x1_ref, h2_ref, lg_ref, y_scr):
    b = pl.program_id(0)
    for h in range(HEADS):
        sl = slice(h * HEAD_DIM, (h + 1) * HEAD_DIM)
        o = of_ref[0, h] + ob_ref[0, h]
        ms = jnp.mean(o * o, axis=-1, keepdims=True)
        g = go_ref[0, :, sl]
        yh = o * lax.rsqrt(ms + EPS) * hgg_ref[...] * (g * _sigmoid(g))
        y = _sigmoid(gm_ref[0, :, sl]) * ym_ref[0, :, sl] + _sigmoid(gh_ref[0, :, sl]) * yh
        y_scr[:, sl] = y.astype(BF16)
    mix = _mm(y_scr[...], wout_ref[...])
    m = mod_ref[pl.ds(b, 1), :]
    x1 = x_ref[0] + m[:, 2 * D_MODEL:3 * D_MODEL] * mix
    x1_ref[0] = x1
    ms = jnp.mean(x1 * x1, axis=-1, keepdims=True)
    h2 = (x1 * lax.rsqrt(ms + EPS) * gffn_ref[...]) * (1.0 + m[:, 4 * D_MODEL:5 * D_MODEL]) + m[:, 3 * D_MODEL:4 * D_MODEL]
    h_hi = h2.astype(BF16)
    u = pltpu.bitcast(h_hi.astype(F32), jnp.uint32)
    half = D_MODEL // 2
    h2_ref[...] = (u[:, 0:half] >> 16) | u[:, half:D_MODEL]
    h_lo = (h2 - h_hi.astype(F32)).astype(BF16)
    w_hi, w_lo = _split2(wr_ref[...])
    lg_ref[...] = _nt(w_hi, h_hi) + _nt(w_lo, h_hi) + _nt(w_hi, h_lo) + br_ref[...]


def _merge(y_mla, o_f, o_b, p, x, mod, hg_g, w_out, g_ffn, w_r_t, b_r):
    bsz, n, d = x.shape
    tm = _tile(n, 256, 128)
    nt = n // tm

    def pcol(col):
        return pl.BlockSpec((1, tm, d), lambda b, t: (b, t, col // d))

    tok = lambda b, t: (b, t, 0)
    const2 = lambda b, t: (0, 0)
    return pl.pallas_call(
        _merge_kernel,
        out_shape=(jax.ShapeDtypeStruct((bsz, n, d), F32),
                   jax.ShapeDtypeStruct((bsz * n, d // 2), jnp.uint32),
                   jax.ShapeDtypeStruct((N_EXPERTS, bsz * n), F32)),
        grid=(bsz, nt),
        in_specs=[pl.BlockSpec((1, tm, d), tok),
                  pl.BlockSpec((1, HEADS, tm, HEAD_DIM), lambda b, t: (b, 0, t, 0)),
                  pl.BlockSpec((1, HEADS, tm, HEAD_DIM), lambda b, t: (b, 0, t, 0)),
                  pcol(COL_GO), pcol(COL_MM), pcol(COL_MH),
                  pl.BlockSpec((1, tm, d), tok),
                  pl.BlockSpec(mod.shape, const2),
                  pl.BlockSpec((1, HEAD_DIM), const2),
                  pl.BlockSpec((d, d), const2),
                  pl.BlockSpec((1, d), const2),
                  pl.BlockSpec((N_EXPERTS, d), const2),
                  pl.BlockSpec((N_EXPERTS, 1), const2)],
        out_specs=(pl.BlockSpec((1, tm, d), tok),
                   pl.BlockSpec((tm, d // 2), lambda b, t: (b * nt + t, 0)),
                   pl.BlockSpec((N_EXPERTS, tm), lambda b, t: (0, b * nt + t))),
        scratch_shapes=[pltpu.VMEM((tm, d), BF16)],
        compiler_params=_cparams(("parallel", "parallel"), VMEM_LIMIT),
        name="merge",
    )(y_mla, o_f, o_b, p, p, p, x, mod, hg_g, w_out, g_ffn, w_r_t, b_r)


def _router_kernel(lg_ref, upper_ref, idx_ref, prob_ref, rank_ref, cnt_ref, run_ref, *, tm):
    @pl.when(pl.program_id(0) == 0)
    def _():
        run_ref[...] = jnp.zeros_like(run_ref)

    l = lg_ref[...]
    eidx = lax.broadcasted_iota(jnp.int32, (N_EXPERTS, tm), 0)
    vals, sel = [], []
    for _ in range(TOP_K):
        m = jnp.max(l, axis=0, keepdims=True)
        first = jnp.min(jnp.where(l == m, eidx, N_EXPERTS), axis=0, keepdims=True)
        vals.append(m)
        sel.append(first)
        l = jnp.where(eidx == first, -jnp.inf, l)
    ex = [jnp.exp(v - vals[0]) for v in vals]
    inv = 1.0 / (ex[0] + ex[1] + ex[2] + ex[3])
    onehot = jnp.zeros((N_EXPERTS, tm), F32)
    for k in range(TOP_K):
        onehot = onehot + jnp.where(eidx == sel[k], 1.0, 0.0)
    before = _mm(onehot.astype(BF16), upper_ref[...]) + run_ref[:, 0:1]
    for k in range(TOP_K):
        idx_ref[k:k + 1, :] = sel[k]
        prob_ref[k:k + 1, :] = ex[k] * inv
        rank_ref[k:k + 1, :] = jnp.sum(jnp.where(eidx == sel[k], before, 0.0), axis=0, keepdims=True).astype(jnp.int32)
    run_ref[...] = run_ref[...] + jnp.sum(onehot, axis=1, keepdims=True)
    cnt_ref[...] = run_ref[...]


def _router(logits_t):
    _, t = logits_t.shape
    tm = _tile(t, 1024, 128)
    upper = jnp.asarray(np.triu(np.ones((tm, tm), np.float32), 1), dtype=BF16)
    kern = functools.partial(_router_kernel, tm=tm)
    tok = pl.BlockSpec((TOP_K, tm), lambda i: (0, i))
    return pl.pallas_call(
        kern,
        out_shape=(jax.ShapeDtypeStruct((TOP_K, t), jnp.int32),
                   jax.ShapeDtypeStruct((TOP_K, t), F32),
                   jax.ShapeDtypeStruct((TOP_K, t), jnp.int32),
                   jax.ShapeDtypeStruct((N_EXPERTS, 128), F32)),
        grid=(t // tm,),
        in_specs=[pl.BlockSpec((N_EXPERTS, tm), lambda i: (0, i)),
                  pl.BlockSpec((tm, tm), lambda i: (0, 0))],
        out_specs=(tok, tok, tok, pl.BlockSpec((N_EXPERTS, 128), lambda i: (0, 0))),
        scratch_shapes=[pltpu.VMEM((N_EXPERTS, 128), F32)],
        compiler_params=_cparams(("arbitrary",), VMEM_LIMIT),
        name="router",
    )(logits_t, upper)


def _dest_kernel(ps_ref, idx_ref, rank_ref, o_ref):
    idx = idx_ref[...]
    dest = rank_ref[...]
    for e in range(N_EXPERTS):
        dest = dest + jnp.where(idx == e, ps_ref[e], 0)
    o_ref[...] = dest


def _dest_rows(pad_start, idx_t, rank_t):
    k, t = idx_t.shape
    tm = _tile(t, 4096, 128)
    tok = pl.BlockSpec((k, tm), lambda i, ps: (0, i))
    return pl.pallas_call(
        _dest_kernel,
        out_shape=jax.ShapeDtypeStruct((k, t), jnp.int32),
        grid_spec=pltpu.PrefetchScalarGridSpec(num_scalar_prefetch=1, grid=(t // tm,), in_specs=[tok, tok], out_specs=tok),
        compiler_params=_cparams(("parallel",), VMEM_LIMIT),
        name="dest_rows",
    )(pad_start, idx_t, rank_t)


def _dispatch_kernel(dest_ref, h2_ref, xs_in_ref, xs_ref, sem, *, tm):
    del xs_in_ref

    def body(r, carry):
        for k in range(TOP_K):
            pltpu.make_async_copy(h2_ref.at[pl.ds(r, 1)], xs_ref.at[pl.ds(dest_ref[0, 0, k * tm + r], 1)], sem).start()
        return carry

    lax.fori_loop(0, tm, body, 0)
    for _ in range(TOP_K):
        pltpu.make_async_copy(h2_ref, xs_ref.at[pl.ds(0, tm)], sem).wait()


def _dispatch(dest_tiles, h2p, n_rows, tm):
    t, w = h2p.shape
    xs0 = jnp.zeros((n_rows, w), h2p.dtype)
    kern = functools.partial(_dispatch_kernel, tm=tm)
    return pl.pallas_call(
        kern,
        out_shape=jax.ShapeDtypeStruct((n_rows, w), h2p.dtype),
        grid=(t // tm,),
        in_specs=[pl.BlockSpec((1, 1, TOP_K * tm), lambda i: (i, 0, 0), memory_space=pltpu.SMEM),
                  pl.BlockSpec((tm, w), lambda i: (i, 0)),
                  pl.BlockSpec(memory_space=pl.ANY)],
        out_specs=pl.BlockSpec(memory_space=pl.ANY),
        scratch_shapes=[pltpu.SemaphoreType.DMA],
        input_output_aliases={2: 0},
        compiler_params=_cparams(("arbitrary",), VMEM_LIMIT),
        name="dispatch",
    )(dest_tiles, h2p, xs0)


GU_GROUP = 256


def _ffn_kernel(be_ref, nu_ref, x_ref, wgu_ref, bgu_ref, wd_ref, bd_ref, perm_ref, y_ref, wgu_s, wd_s):
    i = pl.program_id(0)
    live = i < nu_ref[0]
    new_expert = jnp.logical_or(i == 0, be_ref[i] != be_ref[jnp.maximum(i - 1, 0)])

    @pl.when(jnp.logical_and(live, new_expert))
    def _():
        for g in range(2 * D_EXPERT // GU_GROUP):
            sl = slice(g * GU_GROUP, (g + 1) * GU_GROUP)
            wgu_s[:, sl] = _mm(wgu_ref[0, :, sl].astype(BF16), perm_ref[...]).astype(BF16)
        wd_s[...] = wd_ref[0].astype(BF16)

    @pl.when(live)
    def _():
        u = x_ref[...]
        lo = pltpu.bitcast(u << 16, F32).astype(BF16)
        hi = pltpu.bitcast(u & jnp.uint32(0xFFFF0000), F32).astype(BF16)
        x = jnp.concatenate([lo, hi], axis=1)
        gu = _mm(x, wgu_s[...]) + bgu_ref[0]
        half = GU_GROUP // 2
        n_grp = 2 * D_EXPERT // GU_GROUP
        glu = jnp.concatenate([gu[:, g * GU_GROUP:g * GU_GROUP + half] for g in range(n_grp)], axis=1)
        lin = jnp.concatenate([gu[:, g * GU_GROUP + half:(g + 1) * GU_GROUP] for g in range(n_grp)], axis=1)
        glu = jnp.minimum(glu, SWIGLU_LIMIT)
        lin = jnp.clip(lin, -SWIGLU_LIMIT, SWIGLU_LIMIT)
        act = glu * _sigmoid(SWIGLU_ALPHA * glu) * (lin + 1.0)
        y_ref[...] = _mm(act.astype(BF16), wd_s[...]) + bd_ref[0]

    @pl.when(jnp.logical_not(live))
    def _():
        y_ref[...] = jnp.zeros_like(y_ref)


def _ffn(block_e, n_used, xs, w_gu, b_gu, w_d, b_d):
    n_rows, w = xs.shape
    bm = FFN_ROWS
    nblk = n_rows // bm
    d = w * 2
    half = GU_GROUP // 2
    perm = np.zeros((GU_GROUP, GU_GROUP), np.float32)
    perm[2 * np.arange(half), np.arange(half)] = 1.0
    perm[2 * np.arange(half) + 1, half + np.arange(half)] = 1.0

    def xmap(i, be, nu):
        return (jnp.minimum(i, nu[0] - 1), 0)

    def wmap(i, be, nu):
        return (be[i], 0, 0)

    grid_spec = pltpu.PrefetchScalarGridSpec(
        num_scalar_prefetch=2,
        grid=(nblk,),
        in_specs=[pl.BlockSpec((bm, w), xmap),
                  pl.BlockSpec((1, d, 2 * D_EXPERT), wmap),
                  pl.BlockSpec((1, 1, 2 * D_EXPERT), wmap),
                  pl.BlockSpec((1, D_EXPERT, d), wmap),
                  pl.BlockSpec((1, 1, d), wmap),
                  pl.BlockSpec((GU_GROUP, GU_GROUP), lambda i, be, nu: (0, 0))],
        out_specs=pl.BlockSpec((bm, d), lambda i, be, nu: (i, 0)),
        scratch_shapes=[pltpu.VMEM((d, 2 * D_EXPERT), BF16), pltpu.VMEM((D_EXPERT, d), BF16)],
    )
    return pl.pallas_call(
        _ffn_kernel,
        out_shape=jax.ShapeDtypeStruct((n_rows, d), F32),
        grid_spec=grid_spec,
        compiler_params=_cparams(("arbitrary",), VMEM_LIMIT),
        name="ffn",
    )(block_e, n_used, xs, w_gu, b_gu, w_d, b_d, jnp.asarray(perm, dtype=BF16))


def _combine_kernel(dest_ref, y_ref, p_ref, x1_ref, mod_ref, g_ref, o_ref, buf_ref, sem, *, tm):
    b = pl.program_id(0)

    def body(r, carry):
        for k in range(TOP_K):
            pltpu.make_async_copy(y_ref.at[pl.ds(dest_ref[0, 0, k * tm + r], 1)],
                                  buf_ref.at[k, pl.ds(r, 1)], sem).start()
        return carry

    lax.fori_loop(0, tm, body, 0)
    for k in range(TOP_K):
        pltpu.make_async_copy(y_ref.at[pl.ds(0, tm)], buf_ref.at[k], sem).wait()
    p = p_ref[...]
    acc = p[:, 0:1] * buf_ref[0]
    for k in range(1, TOP_K):
        acc = acc + p[:, k:k + 1] * buf_ref[k]
    m = mod_ref[pl.ds(b, 1), :]
    xo = x1_ref[0] + m[:, 5 * D_MODEL:6 * D_MODEL] * acc
    ms = jnp.mean(xo * xo, axis=-1, keepdims=True)
    o_ref[0] = xo * lax.rsqrt(ms + EPS) * g_ref[...]


def _combine(dest_tiles, y, probs_t, x1, mod, g_fin, tm):
    bsz, n, d = x1.shape
    nt = n // tm
    kern = functools.partial(_combine_kernel, tm=tm)
    return pl.pallas_call(
        kern,
        out_shape=jax.ShapeDtypeStruct((bsz, n, d), F32),
        grid=(bsz, nt),
        in_specs=[pl.BlockSpec((1, 1, TOP_K * tm), lambda b, t: (b * nt + t, 0, 0), memory_space=pltpu.SMEM),
                  pl.BlockSpec(memory_space=pl.ANY),
                  pl.BlockSpec((tm, TOP_K), lambda b, t: (b * nt + t, 0)),
                  pl.BlockSpec((1, tm, d), lambda b, t: (b, t, 0)),
                  pl.BlockSpec(mod.shape, lambda b, t: (0, 0)),
                  pl.BlockSpec((1, d), lambda b, t: (0, 0))],
        out_specs=pl.BlockSpec((1, tm, d), lambda b, t: (b, t, 0)),
        scratch_shapes=[pltpu.VMEM((TOP_K, tm, d), F32), pltpu.SemaphoreType.DMA],
        compiler_params=_cparams(("arbitrary", "arbitrary"), VMEM_LIMIT),
        name="combine",
    )(dest_tiles, y, probs_t, x1, mod, g_fin)


def _prep_w_in(w_in):
    q_a, kv, hq, ff, fb, hi, go, mm, mh = jnp.split(
        w_in, np.cumsum((MLA_Q_RANK, MLA_KV_RANK + MLA_ROPE) + (D_MODEL,) * 6).tolist(), axis=1)
    c_kv, kr = kv[:, :MLA_KV_RANK], kv[:, MLA_KV_RANK:]
    half = MLA_ROPE // 2
    kr_sw = jnp.concatenate([-kr[:, half:], kr[:, :half]], axis=1)
    return jnp.concatenate([hq, ff, fb, hi, go, mm, mh, q_a, c_kv, kr, kr_sw], axis=1).astype(BF16)


def _prep_w_q(w_q_b):
    w = w_q_b.reshape(MLA_Q_RANK, HEADS, MLA_QK)
    w = jnp.pad(w, ((0, 0), (0, 0), (0, MLA_QK_PAD - MLA_QK)))
    return w.reshape(MLA_Q_RANK, HEADS * MLA_QK_PAD).astype(BF16)


def _prep_w_kv(w_kv_b):
    w = w_kv_b.reshape(MLA_KV_RANK, HEADS, 2 * HEAD_DIM)
    k = w[:, :, :HEAD_DIM].reshape(MLA_KV_RANK, HEADS * HEAD_DIM)
    v = w[:, :, HEAD_DIM:].reshape(MLA_KV_RANK, HEADS * HEAD_DIM)
    return jnp.concatenate([k, v], axis=1).astype(BF16)


def _rope_tables(n_lat, n_ctx):
    f32 = np.float32
    rows = n_lat // GRID_W
    row = np.repeat(np.arange(rows), GRID_W).astype(f32)
    col = np.tile(np.arange(GRID_W), rows).astype(f32)
    n_freq = MLA_ROPE // 4
    inv = (f32(ROPE_BASE) ** (-np.arange(n_freq, dtype=f32) / f32(n_freq))).astype(f32)
    ang = np.concatenate([row[:, None] * inv, col[:, None] * inv], axis=-1).astype(f32)
    cos, sin = np.cos(ang).astype(f32), np.sin(ang).astype(f32)
    z64 = np.zeros((n_lat, 64), f32)
    cos_q = np.concatenate([cos, cos, z64], axis=1)
    sin_q = np.concatenate([-sin, sin, z64], axis=1)
    cos_k = np.concatenate([cos, cos, z64], axis=1)
    sin_k = np.concatenate([sin, sin, z64], axis=1)
    ctx_cos = np.concatenate([np.ones((n_ctx, 64), f32), np.zeros((n_ctx, 64), f32)], axis=1)
    cos_k = np.concatenate([cos_k, ctx_cos], axis=0)
    sin_k = np.concatenate([sin_k, np.zeros((n_ctx, 128), f32)], axis=0)
    return jnp.asarray(cos_q), jnp.asarray(sin_q), jnp.asarray(cos_k), jnp.asarray(sin_k)


def kernel(x, c, ctx, c_ctx, w_mod, b_mod, norm_mix_g, w_in, mla_q_norm_g, w_q_b, mla_kv_norm_g, w_kv_b,
           hg_lb_logits, hg_norm_g, w_out, norm_ffn_g, w_router, b_router, w_gate_up, b_gate_up, w_down,
           b_down, final_norm_g):
    bsz, n_lat, d = x.shape
    n_ctx = ctx.shape[1]
    assert d == D_MODEL and w_mod.shape[0] == 1
    assert n_lat % HG_CHUNK == 0 and n_ctx % HG_CHUNK == 0 and n_lat % GRID_W == 0
    t_tok = bsz * n_lat

    mod_rows = -(-(bsz + 1) // 8) * 8
    cc = jnp.concatenate([c, c_ctx[None, :], jnp.zeros((mod_rows - bsz - 1, d), F32)], axis=0)
    mod = _modulation(cc, w_mod[0], b_mod[0][None, :])

    xa = jnp.concatenate([x, ctx], axis=1)
    p = _inproj(xa, mod, norm_mix_g[0][None, :], _prep_w_in(w_in[0]), n_lat)

    cos_q, sin_q, cos_k, sin_k = _rope_tables(n_lat, n_ctx)
    q = _mla_q(p, mla_q_norm_g[0][None, :], _prep_w_q(w_q_b[0]), cos_q, sin_q, n_lat)
    k, v = _mla_kv(p, mla_kv_norm_g[0][None, :], _prep_w_kv(w_kv_b[0]), cos_k, sin_k)
    y_mla = _attention(q, k, v)

    lb = jax.nn.softmax(hg_lb_logits.astype(F32), axis=1)[:, 0, :]
    o_f, o_b = _hgrn_scans(p, lb.reshape(2, HEADS, 1, HEAD_DIM), n_lat)

    x1, h2p, logits_t = _merge(y_mla, o_f, o_b, p, x, mod, hg_norm_g[0][None, :], w_out[0].astype(BF16),
                               norm_ffn_g[0][None, :], w_router[0].T, b_router[0][:, None])

    idx_t, prob_t, rank_t, cnt = _router(logits_t)

    counts = cnt[:, 0].astype(jnp.int32)
    padded = (counts + FFN_ROWS - 1) // FFN_ROWS * FFN_ROWS
    pad_end = jnp.cumsum(padded)
    pad_start = pad_end - padded
    dest = _dest_rows(pad_start.astype(jnp.int32), idx_t, rank_t)
    n_rows = -(-(t_tok * TOP_K + N_EXPERTS * (FFN_ROWS - 1)) // FFN_ROWS) * FFN_ROWS
    nblk = n_rows // FFN_ROWS
    block_start = jnp.arange(nblk, dtype=jnp.int32) * FFN_ROWS
    block_e = jnp.minimum(jnp.sum(pad_end[None, :] <= block_start[:, None], axis=1), N_EXPERTS - 1).astype(jnp.int32)
    n_used = (pad_end[-1:] // FFN_ROWS).astype(jnp.int32)

    tm_c = _tile(n_lat, 256)
    dest_tiles = dest.reshape(TOP_K, t_tok // tm_c, tm_c).transpose(1, 0, 2).reshape(t_tok // tm_c, 1, TOP_K * tm_c)

    xs = _dispatch(dest_tiles, h2p, n_rows, tm_c)

    n_grp = 2 * D_EXPERT // GU_GROUP
    b_gu = b_gate_up[0].reshape(N_EXPERTS, n_grp, GU_GROUP // 2, 2).transpose(0, 1, 3, 2).reshape(N_EXPERTS, 1, 2 * D_EXPERT)
    y = _ffn(block_e, n_used, xs, w_gate_up[0], b_gu, w_down[0], b_down[0][:, None, :])

    return _combine(dest_tiles, y, prob_t.T, x1, mod, final_norm_g[None, :], tm_c)
```

```python
import functools

import jax
import jax.numpy as jnp
import numpy as np
from jax import lax
from jax.experimental import pallas as pl
from jax.experimental.pallas import tpu as pltpu

F32 = jnp.float32
BF16 = jnp.bfloat16

D_MODEL = 1024
EPS = 1e-6
LOG2_E = 1.4426950408889634
N_MOD = 6
GRID_W = 64
ROPE_BASE = 10000.0

HEADS = 8
HEAD_DIM = 128
MLA_ROPE = 64
MLA_QK = HEAD_DIM + MLA_ROPE
MLA_QK_PAD = 256
MLA_Q_RANK = 768
MLA_KV_RANK = 256

N_EXPERTS = 32
TOP_K = 4
D_EXPERT = 1024
SWIGLU_LIMIT = 7.0
SWIGLU_ALPHA = 1.702

HG_CHUNK = 128
FFN_ROWS = 512

COL_HQ, COL_I, COL_GO, COL_MM, COL_MH = (i * D_MODEL for i in range(5))
COL_QA = 5 * D_MODEL
COL_CKV = COL_QA + MLA_Q_RANK
COL_KR = COL_CKV + MLA_KV_RANK
PROJ16_TN = 1280
N_PROJ16 = -(-(COL_KR + 2 * MLA_ROPE) // PROJ16_TN) * PROJ16_TN
COL_FF, COL_FB = 0, D_MODEL
N_PROJ32 = 2 * D_MODEL
PROJ32_TN = 1024

VMEM_LIMIT = 56 * 1024 * 1024


def _cparams(sem, vmem=None):
    return pltpu.CompilerParams(dimension_semantics=sem, vmem_limit_bytes=vmem)


def _tile(n, pref, mult=8):
    best = None
    for t in range(mult, min(n, pref) + 1, mult):
        if n % t == 0:
            best = t
    assert best is not None, (n, pref, mult)
    return best


def _nt(a, b):
    return lax.dot_general(a, b, (((1,), (1,)), ((), ())), preferred_element_type=F32)


def _mm(a, b):
    return jnp.dot(a, b, preferred_element_type=F32)


def _split2(a):
    hi = a.astype(BF16)
    lo = (a - hi.astype(F32)).astype(BF16)
    return hi, lo


def _sigmoid(x):
    return 1.0 / (1.0 + jnp.exp(-x))


def _mod_kernel(c_ref, w_ref, b_ref, o_ref):
    c = c_ref[...]
    s = c * _sigmoid(c)
    s_hi, s_lo = _split2(s)
    w_hi, w_lo = _split2(w_ref[...])
    o_ref[...] = _mm(s_hi, w_hi) + _mm(s_lo, w_hi) + _mm(s_hi, w_lo) + b_ref[...]


def _modulation(cc, w_mod, b_mod):
    r, d = cc.shape
    n = w_mod.shape[1]
    tn = _tile(n, 1536, 128)
    return pl.pallas_call(
        _mod_kernel,
        out_shape=jax.ShapeDtypeStruct((r, n), F32),
        grid=(n // tn,),
        in_specs=[pl.BlockSpec((r, d), lambda j: (0, 0)),
                  pl.BlockSpec((d, tn), lambda j: (0, j)),
                  pl.BlockSpec((1, tn), lambda j: (0, j))],
        out_specs=pl.BlockSpec((r, tn), lambda j: (0, j)),
        compiler_params=_cparams(("parallel",), VMEM_LIMIT),
        name="mod",
    )(cc, w_mod, b_mod)


def _inproj_kernel(x_ref, mod_ref, g_ref, w16_ref, w32_ref, o16_ref, o32_ref, hn_ref, *, n_lat, tm, ctx_row, n16):
    b = pl.program_id(0)
    t = pl.program_id(1)
    j = pl.program_id(2)

    @pl.when(j == 0)
    def _():
        x = x_ref[0]
        ms = jnp.mean(x * x, axis=-1, keepdims=True)
        y = x * lax.rsqrt(ms + EPS) * g_ref[...]
        row = t * tm + lax.broadcasted_iota(jnp.int32, (tm, 1), 0)
        is_ctx = row >= n_lat
        m_lat = mod_ref[pl.ds(b, 1), :]
        m_ctx = mod_ref[pl.ds(ctx_row, 1), :]
        shift = jnp.where(is_ctx, m_ctx[:, 0:D_MODEL], m_lat[:, 0:D_MODEL])
        scale = jnp.where(is_ctx, m_ctx[:, D_MODEL:2 * D_MODEL], m_lat[:, D_MODEL:2 * D_MODEL])
        hn_ref[...] = (y * (1.0 + scale) + shift).astype(BF16)

    @pl.when(j < n16)
    def _():
        o16_ref[0] = _mm(hn_ref[...], w16_ref[...]).astype(BF16)

    @pl.when(j >= n16)
    def _():
        o32_ref[0] = _mm(hn_ref[...], w32_ref[...])


def _inproj(xa, mod, g, w16, w32, n_lat):
    bsz, rows, d = xa.shape
    tm = _tile(rows, 1152)
    n16 = N_PROJ16 // PROJ16_TN
    n32 = N_PROJ32 // PROJ32_TN

    def j16(j):
        return jnp.minimum(j, n16 - 1)

    def j32(j):
        return jnp.maximum(j - n16, 0)

    kern = functools.partial(_inproj_kernel, n_lat=n_lat, tm=tm, ctx_row=bsz, n16=n16)
    return pl.pallas_call(
        kern,
        out_shape=(jax.ShapeDtypeStruct((bsz, rows, N_PROJ16), BF16),
                   jax.ShapeDtypeStruct((bsz, rows, N_PROJ32), F32)),
        grid=(bsz, rows // tm, n16 + n32),
        in_specs=[pl.BlockSpec((1, tm, d), lambda b, t, j: (b, t, 0)),
                  pl.BlockSpec(mod.shape, lambda b, t, j: (0, 0)),
                  pl.BlockSpec((1, d), lambda b, t, j: (0, 0)),
                  pl.BlockSpec((d, PROJ16_TN), lambda b, t, j: (0, j16(j))),
                  pl.BlockSpec((d, PROJ32_TN), lambda b, t, j: (0, j32(j)))],
        out_specs=(pl.BlockSpec((1, tm, PROJ16_TN), lambda b, t, j: (b, t, j16(j))),
                   pl.BlockSpec((1, tm, PROJ32_TN), lambda b, t, j: (b, t, j32(j)))),
        scratch_shapes=[pltpu.VMEM((tm, d), BF16)],
        compiler_params=_cparams(("parallel", "parallel", "arbitrary"), VMEM_LIMIT),
        name="inproj",
    )(xa, mod, g, w16, w32)


def _mlaq_kernel(a0_ref, a1_ref, a2_ref, g_ref, w_ref, cos_ref, sin_ref, o_ref, *, tm):
    parts = [a0_ref[0].astype(F32), a1_ref[0].astype(F32), a2_ref[0].astype(F32)]
    ss = sum(jnp.sum(p * p, axis=-1, keepdims=True) for p in parts)
    r = lax.rsqrt(ss * (1.0 / MLA_Q_RANK) + EPS)
    acc = None
    for j, p in enumerate(parts):
        hj = (p * r * g_ref[:, j * 256:(j + 1) * 256]).astype(BF16)
        d = _mm(hj, w_ref[j * 256:(j + 1) * 256, :])
        acc = d if acc is None else acc + d
    lane = lax.broadcasted_iota(jnp.int32, (tm, HEAD_DIM), 1)
    scale = MLA_QK ** -0.5 * LOG2_E
    cos = cos_ref[...]
    sin = sin_ref[...]
    for h in range(HEADS):
        nope = acc[:, h * MLA_QK_PAD:h * MLA_QK_PAD + HEAD_DIM]
        rp = acc[:, h * MLA_QK_PAD + HEAD_DIM:(h + 1) * MLA_QK_PAD]
        swapped = jnp.where(lane < 32, pltpu.roll(rp, 96, 1), pltpu.roll(rp, 32, 1))
        rot = rp * cos + swapped * sin
        o_ref[0, h, :, 0:HEAD_DIM] = (nope * scale).astype(BF16)
        o_ref[0, h, :, HEAD_DIM:MLA_QK_PAD] = (rot * scale).astype(BF16)


def _mla_q(p, g, w, cos_q, sin_q, n_lat):
    bsz = p.shape[0]
    tm = _tile(n_lat, 512)
    cb = COL_QA // 256
    kern = functools.partial(_mlaq_kernel, tm=tm)
    return pl.pallas_call(
        kern,
        out_shape=jax.ShapeDtypeStruct((bsz, HEADS, n_lat, MLA_QK_PAD), BF16),
        grid=(bsz, n_lat // tm),
        in_specs=[pl.BlockSpec((1, tm, 256), lambda b, t: (b, t, cb)),
                  pl.BlockSpec((1, tm, 256), lambda b, t: (b, t, cb + 1)),
                  pl.BlockSpec((1, tm, 256), lambda b, t: (b, t, cb + 2)),
                  pl.BlockSpec((1, MLA_Q_RANK), lambda b, t: (0, 0)),
                  pl.BlockSpec(w.shape, lambda b, t: (0, 0)),
                  pl.BlockSpec((tm, HEAD_DIM), lambda b, t: (t, 0)),
                  pl.BlockSpec((tm, HEAD_DIM), lambda b, t: (t, 0))],
        out_specs=pl.BlockSpec((1, HEADS, tm, MLA_QK_PAD), lambda b, t: (b, 0, t, 0)),
        compiler_params=_cparams(("parallel", "parallel"), VMEM_LIMIT),
        name="mla_q",
    )(p, p, p, g, w, cos_q, sin_q)


def _mlakv_kernel(c_ref, kr_ref, g_ref, w_ref, cos_ref, sin_ref, k_ref, v_ref):
    c = c_ref[0].astype(F32)
    ms = jnp.mean(c * c, axis=-1, keepdims=True)
    hn = (c * lax.rsqrt(ms + EPS) * g_ref[...]).astype(BF16)
    kv = _mm(hn, w_ref[...])
    grp = kr_ref[0].astype(F32)
    rot = (grp * cos_ref[...] + pltpu.roll(grp, 64, 1) * sin_ref[...]).astype(BF16)
    for h in range(HEADS):
        k_ref[0, h, :, 0:HEAD_DIM] = kv[:, h * HEAD_DIM:(h + 1) * HEAD_DIM].astype(BF16)
        k_ref[0, h, :, HEAD_DIM:MLA_QK_PAD] = rot
        v_ref[0, h] = kv[:, D_MODEL + h * HEAD_DIM:D_MODEL + (h + 1) * HEAD_DIM].astype(BF16)


def _mla_kv(p, g, w, cos_k, sin_k):
    bsz, rows, _ = p.shape
    tm = _tile(rows, 768)
    return pl.pallas_call(
        _mlakv_kernel,
        out_shape=(jax.ShapeDtypeStruct((bsz, HEADS, rows, MLA_QK_PAD), BF16),
                   jax.ShapeDtypeStruct((bsz, HEADS, rows, HEAD_DIM), BF16)),
        grid=(bsz, rows // tm),
        in_specs=[pl.BlockSpec((1, tm, MLA_KV_RANK), lambda b, t: (b, t, COL_CKV // MLA_KV_RANK)),
                  pl.BlockSpec((1, tm, 128), lambda b, t: (b, t, COL_KR // 128)),
                  pl.BlockSpec((1, MLA_KV_RANK), lambda b, t: (0, 0)),
                  pl.BlockSpec(w.shape, lambda b, t: (0, 0)),
                  pl.BlockSpec((tm, 128), lambda b, t: (t, 0)),
                  pl.BlockSpec((tm, 128), lambda b, t: (t, 0))],
        out_specs=(pl.BlockSpec((1, HEADS, tm, MLA_QK_PAD), lambda b, t: (b, 0, t, 0)),
                   pl.BlockSpec((1, HEADS, tm, HEAD_DIM), lambda b, t: (b, 0, t, 0))),
        compiler_params=_cparams(("parallel", "parallel"), VMEM_LIMIT),
        name="mla_kv",
    )(p, p, g, w, cos_k, sin_k)


def _attn_kernel(q_ref, k_ref, v_ref, o_ref, *, tq, sub):
    k = k_ref[0, 0]
    v = v_ref[0, 0]
    v1 = jnp.concatenate([v, jnp.ones_like(v)], axis=1)
    for r in range(tq // sub):
        rows = slice(r * sub, (r + 1) * sub)
        s = _nt(q_ref[0, 0, rows, :], k)
        m = jnp.max(s, axis=-1, keepdims=True)
        p = jnp.exp2(s - m).astype(BF16)
        ol = _mm(p, v1)
        o_ref[0, rows, :] = ol[:, 0:HEAD_DIM] * (1.0 / ol[:, HEAD_DIM:HEAD_DIM + 1])


def _attention(q, k, v):
    bsz, _, n, _ = q.shape
    m = k.shape[2]
    tq = _tile(n, 2048)
    sub = _tile(tq, 512)
    kern = functools.partial(_attn_kernel, tq=tq, sub=sub)
    return pl.pallas_call(
        kern,
        out_shape=jax.ShapeDtypeStruct((bsz, n, HEADS * HEAD_DIM), F32),
        grid=(bsz, HEADS, n // tq),
        in_specs=[pl.BlockSpec((1, 1, tq, MLA_QK_PAD), lambda b, h, t: (b, h, t, 0)),
                  pl.BlockSpec((1, 1, m, MLA_QK_PAD), lambda b, h, t: (b, h, 0, 0)),
                  pl.BlockSpec((1, 1, m, HEAD_DIM), lambda b, h, t: (b, h, 0, 0))],
        out_specs=pl.BlockSpec((1, tq, HEAD_DIM), lambda b, h, t: (b, t, h)),
        compiler_params=_cparams(("parallel", "parallel", "arbitrary"), VMEM_LIMIT),
        name="attention",
    )(q, k, v)


def _level_ref(cum, blk, reverse):
    c = cum.shape[0]
    half = blk // 2
    r = half if reverse else half - 1
    if blk >= 8:
        x = cum.reshape(c // blk, blk, HEAD_DIM)
        e = jnp.broadcast_to(x[:, r:r + 1, :], x.shape)
        return e.reshape(c, HEAD_DIM)
    x = cum.reshape(c // 8, 8, HEAD_DIM)
    sub = lax.broadcasted_iota(jnp.int32, x.shape, 1)
    e = None
    for jb in range(8 // blk):
        cand = jnp.broadcast_to(x[:, jb * blk + r:jb * blk + r + 1, :], x.shape)
        e = cand if e is None else jnp.where(sub >= jb * blk, cand, e)
    return e.reshape(c, HEAD_DIM)


def _hgrn_chunk(q, z, v, lb, tri, lvl, st, *, chunk, reverse):
    q = q.astype(F32)
    qb = (q * _sigmoid(q) * (HEAD_DIM ** -0.5)).astype(BF16)
    f = lb + (1.0 - lb) * _sigmoid(z)
    kb = (1.0 - f).astype(BF16)
    g = jnp.log(f) * LOG2_E
    vb = v.astype(BF16)

    g_hi, g_lo = _split2(g)
    cum = _mm(tri, jnp.concatenate([g_hi, g_lo], axis=0))

    att = jnp.where(lvl == 0, _nt(qb, kb).astype(BF16), jnp.zeros((), BF16))
    n_lvl = chunk.bit_length() - 1
    for lv in range(1, n_lvl + 1):
        zrel = cum - _level_ref(cum, 1 << lv, reverse)
        neg_abs = pltpu.bitcast(pltpu.bitcast(zrel, jnp.uint32) | jnp.uint32(0x80000000), F32)
        e = jnp.exp2(neg_abs).astype(BF16)
        att = jnp.where(lvl == lv, _nt(qb * e, kb * e).astype(BF16), att)

    last = 0 if reverse else chunk - 1
    tot = cum[last:last + 1, :]
    o = _nt(qb * jnp.exp2(cum).astype(BF16), st.astype(BF16)) + _mm(att, vb)
    kt = kb * jnp.exp2(tot - cum).astype(BF16)
    return o, st * jnp.exp2(tot) + _mm(v.astype(F32).T.astype(BF16), kt)


def _hgrn_kernel(qf_ref, zf_ref, if_ref, qb_ref, zb_ref, ib_ref, lb_ref, tri_ref, lvl_ref, of_ref, ob_ref, st_ref,
                 *, chunk, group):
    @pl.when(pl.program_id(2) == 0)
    def _():
        st_ref[...] = jnp.zeros_like(st_ref)

    ins = ((qf_ref, zf_ref, if_ref), (qb_ref, zb_ref, ib_ref))
    outs = (of_ref, ob_ref)
    chains = [(d, j) for j in range(group) for d in range(2)]
    states = [st_ref[d, j] for d, j in chains]
    results = []
    for (d, j), st in zip(chains, states):
        q_ref, z_ref, i_ref = ins[d]
        sl = slice(j * HEAD_DIM, (j + 1) * HEAD_DIM)
        results.append(_hgrn_chunk(q_ref[0, :, sl], z_ref[0, :, sl], i_ref[0, :, sl], lb_ref[d, j], tri_ref[d],
                                   lvl_ref[d], st, chunk=chunk, reverse=bool(d)))
    for (d, j), (o, st) in zip(chains, results):
        outs[d][0, j] = o
        st_ref[d, j] = st


def _hgrn_consts(chunk, reverse):
    t = np.arange(chunk)[:, None]
    s = np.arange(chunk)[None, :]
    x = t ^ s
    bitlen = np.zeros_like(x)
    for b in range(chunk.bit_length()):
        bitlen = np.where(x >> b > 0, b + 1, bitlen)
    valid = (s > t) if reverse else (t > s)
    lvl = np.where(t == s, 0, np.where(valid, bitlen, -1)).astype(np.float32)
    tri = ((s >= t) if reverse else (t >= s)).astype(np.float32)
    return np.concatenate([tri, tri], axis=1), lvl


HG_GROUP = 4


def _hgrn_scans(p, pf, lb, n_lat):
    bsz, rows, _ = p.shape
    c = HG_CHUNK
    grp = HG_GROUP
    n_chunks = rows // c
    n_lat_c = n_lat // c
    n_ctx_c = n_chunks - n_lat_c
    consts = [_hgrn_consts(c, False), _hgrn_consts(c, True)]
    tri = jnp.asarray(np.stack([consts[0][0], consts[1][0]]), dtype=BF16)
    lvl = jnp.asarray(np.stack([consts[0][1], consts[1][1]]), dtype=BF16)

    def cidx_f(i):
        return jnp.where(i < n_ctx_c, n_lat_c + i, i - n_ctx_c)

    def cidx_b(i):
        return n_chunks - 1 - i

    def col_spec(col, cidx):
        return pl.BlockSpec((1, c, grp * HEAD_DIM), lambda b, h, i: (b, cidx(i), col // (grp * HEAD_DIM) + h))

    def out_spec(cidx):
        return pl.BlockSpec((1, grp, c, HEAD_DIM), lambda b, h, i: (b, h, cidx(i), 0))

    o_shape = jax.ShapeDtypeStruct((bsz, HEADS, rows, HEAD_DIM), F32)
    kern = functools.partial(_hgrn_kernel, chunk=c, group=grp)
    return pl.pallas_call(
        kern,
        out_shape=(o_shape, o_shape),
        grid=(bsz, HEADS // grp, n_chunks),
        in_specs=[col_spec(COL_HQ, cidx_f), col_spec(COL_FF, cidx_f), col_spec(COL_I, cidx_f),
                  col_spec(COL_HQ, cidx_b), col_spec(COL_FB, cidx_b), col_spec(COL_I, cidx_b),
                  pl.BlockSpec((2, grp, 1, HEAD_DIM), lambda b, h, i: (0, h, 0, 0)),
                  pl.BlockSpec((2, c, 2 * c), lambda b, h, i: (0, 0, 0)),
                  pl.BlockSpec((2, c, c), lambda b, h, i: (0, 0, 0))],
        out_specs=(out_spec(cidx_f), out_spec(cidx_b)),
        scratch_shapes=[pltpu.VMEM((2, grp, HEAD_DIM, HEAD_DIM), F32)],
        compiler_params=_cparams(("parallel", "parallel", "arbitrary"), VMEM_LIMIT),
        name="hgrn",
    )(p, pf, p, p, pf, p, lb, tri, lvl)


def _merge_kernel(ym_ref, of_ref, ob_ref, go_ref, gm_ref, gh_ref, x_ref, mod_ref, hgg_ref, wout_ref,
                  gffn_ref, wr_ref, br_ref, x1_ref, h2_ref, lg_ref, y_scr):
    b = pl.program_id(0)
    for h in range(HEADS):
        sl = slice(h * HEAD_DIM, (h + 1) * HEAD_DIM)
        o = of_ref[0, h] + ob_ref[0, h]
        ms = jnp.mean(o * o, axis=-1, keepdims=True)
        g = go_ref[0, :, sl].astype(F32)
        yh = o * lax.rsqrt(ms + EPS) * hgg_ref[...] * (g * _sigmoid(g))
        y = (_sigmoid(gm_ref[0, :, sl].astype(F32)) * ym_ref[0, :, sl]
             + _sigmoid(gh_ref[0, :, sl].astype(F32)) * yh)
        y_scr[:, sl] = y.astype(BF16)
    mix = _mm(y_scr[...], wout_ref[...])
    m = mod_ref[pl.ds(b, 1), :]
    x1 = x_ref[0] + m[:, 2 * D_MODEL:3 * D_MODEL] * mix
    x1_ref[0] = x1
    ms = jnp.mean(x1 * x1, axis=-1, keepdims=True)
    h2 = (x1 * lax.rsqrt(ms + EPS) * gffn_ref[...]) * (1.0 + m[:, 4 * D_MODEL:5 * D_MODEL]) + m[:, 3 * D_MODEL:4 * D_MODEL]
    h_hi = h2.astype(BF16)
    u = pltpu.bitcast(h_hi.astype(F32), jnp.uint32)
    half = D_MODEL // 2
    h2_ref[...] = (u[:, 0:half] >> 16) | u[:, half:D_MODEL]
    h_lo = (h2 - h_hi.astype(F32)).astype(BF16)
    w_hi, w_lo = _split2(wr_ref[...])
    lg_ref[...] = _nt(w_hi, h_hi) + _nt(w_lo, h_hi) + _nt(w_hi, h_lo) + br_ref[...]


def _merge(y_mla, o_f, o_b, p, x, mod, hg_g, w_out, g_ffn, w_r_t, b_r):
    bsz, n, d = x.shape
    tm = _tile(n, 256, 128)
    nt = n // tm

    def pcol(col):
        return pl.BlockSpec((1, tm, d), lambda b, t: (b, t, col // d))

    tok = lambda b, t: (b, t, 0)
    const2 = lambda b, t: (0, 0)
    return pl.pallas_call(
        _merge_kernel,
        out_shape=(jax.ShapeDtypeStruct((bsz, n, d), F32),
                   jax.ShapeDtypeStruct((bsz * n, d // 2), jnp.uint32),
                   jax.ShapeDtypeStruct((N_EXPERTS, bsz * n), F32)),
        grid=(bsz, nt),
        in_specs=[pl.BlockSpec((1, tm, d), tok),
                  pl.BlockSpec((1, HEADS, tm, HEAD_DIM), lambda b, t: (b, 0, t, 0)),
                  pl.BlockSpec((1, HEADS, tm, HEAD_DIM), lambda b, t: (b, 0, t, 0)),
                  pcol(COL_GO), pcol(COL_MM), pcol(COL_MH),
                  pl.BlockSpec((1, tm, d), tok),
                  pl.BlockSpec(mod.shape, const2),
                  pl.BlockSpec((1, HEAD_DIM), const2),
                  pl.BlockSpec((d, d), const2),
                  pl.BlockSpec((1, d), const2),
                  pl.BlockSpec((N_EXPERTS, d), const2),
                  pl.BlockSpec((N_EXPERTS, 1), const2)],
        out_specs=(pl.BlockSpec((1, tm, d), tok),
                   pl.BlockSpec((tm, d // 2), lambda b, t: (b * nt + t, 0)),
                   pl.BlockSpec((N_EXPERTS, tm), lambda b, t: (0, b * nt + t))),
        scratch_shapes=[pltpu.VMEM((tm, d), BF16)],
        compiler_params=_cparams(("parallel", "parallel"), VMEM_LIMIT),
        name="merge",
    )(y_mla, o_f, o_b, p, p, p, x, mod, hg_g, w_out, g_ffn, w_r_t, b_r)


def _router_kernel(lg_ref, upper_ref, idx_ref, prob_ref, rank_ref, cnt_ref, run_ref, *, tm):
    @pl.when(pl.program_id(0) == 0)
    def _():
        run_ref[...] = jnp.zeros_like(run_ref)

    l = lg_ref[...]
    eidx = lax.broadcasted_iota(jnp.int32, (N_EXPERTS, tm), 0)
    vals, sel = [], []
    for _ in range(TOP_K):
        m = jnp.max(l, axis=0, keepdims=True)
        first = jnp.min(jnp.where(l == m, eidx, N_EXPERTS), axis=0, keepdims=True)
        vals.append(m)
        sel.append(first)
        l = jnp.where(eidx == first, -jnp.inf, l)
    ex = [jnp.exp(v - vals[0]) for v in vals]
    inv = 1.0 / (ex[0] + ex[1] + ex[2] + ex[3])
    onehot = jnp.zeros((N_EXPERTS, tm), F32)
    for k in range(TOP_K):
        onehot = onehot + jnp.where(eidx == sel[k], 1.0, 0.0)
    before = _mm(onehot.astype(BF16), upper_ref[...]) + run_ref[:, 0:1]
    for k in range(TOP_K):
        idx_ref[k:k + 1, :] = sel[k]
        prob_ref[k:k + 1, :] = ex[k] * inv
        rank_ref[k:k + 1, :] = jnp.sum(jnp.where(eidx == sel[k], before, 0.0), axis=0, keepdims=True).astype(jnp.int32)
    run_ref[...] = run_ref[...] + jnp.sum(onehot, axis=1, keepdims=True)
    cnt_ref[...] = run_ref[...]


def _router(logits_t):
    _, t = logits_t.shape
    tm = _tile(t, 1024, 128)
    upper = jnp.asarray(np.triu(np.ones((tm, tm), np.float32), 1), dtype=BF16)
    kern = functools.partial(_router_kernel, tm=tm)
    tok = pl.BlockSpec((TOP_K, tm), lambda i: (0, i))
    return pl.pallas_call(
        kern,
        out_shape=(jax.ShapeDtypeStruct((TOP_K, t), jnp.int32),
                   jax.ShapeDtypeStruct((TOP_K, t), F32),
                   jax.ShapeDtypeStruct((TOP_K, t), jnp.int32),
                   jax.ShapeDtypeStruct((N_EXPERTS, 128), F32)),
        grid=(t // tm,),
        in_specs=[pl.BlockSpec((N_EXPERTS, tm), lambda i: (0, i)),
                  pl.BlockSpec((tm, tm), lambda i: (0, 0))],
        out_specs=(tok, tok, tok, pl.BlockSpec((N_EXPERTS, 128), lambda i: (0, 0))),
        scratch_shapes=[pltpu.VMEM((N_EXPERTS, 128), F32)],
        compiler_params=_cparams(("arbitrary",), VMEM_LIMIT),
        name="router",
    )(logits_t, upper)


def _dest_kernel(ps_ref, idx_ref, rank_ref, o_ref):
    idx = idx_ref[...]
    dest = rank_ref[...]
    for e in range(N_EXPERTS):
        dest = dest + jnp.where(idx == e, ps_ref[e], 0)
    o_ref[...] = dest


def _dest_rows(pad_start, idx_t, rank_t):
    k, t = idx_t.shape
    tm = _tile(t, 4096, 128)
    tok = pl.BlockSpec((k, tm), lambda i, ps: (0, i))
    return pl.pallas_call(
        _dest_kernel,
        out_shape=jax.ShapeDtypeStruct((k, t), jnp.int32),
        grid_spec=pltpu.PrefetchScalarGridSpec(num_scalar_prefetch=1, grid=(t // tm,), in_specs=[tok, tok], out_specs=tok),
        compiler_params=_cparams(("parallel",), VMEM_LIMIT),
        name="dest_rows",
    )(pad_start, idx_t, rank_t)


def _dispatch_kernel(dest_ref, h2_ref, xs_in_ref, xs_ref, sem, *, tm):
    del xs_in_ref

    def body(r, carry):
        for k in range(TOP_K):
            pltpu.make_async_copy(h2_ref.at[pl.ds(r, 1)], xs_ref.at[pl.ds(dest_ref[0, 0, k * tm + r], 1)], sem).start()
        return carry

    lax.fori_loop(0, tm, body, 0)
    for _ in range(TOP_K):
        pltpu.make_async_copy(h2_ref, xs_ref.at[pl.ds(0, tm)], sem).wait()


def _dispatch(dest_tiles, h2p, n_rows, tm):
    t, w = h2p.shape
    xs0 = jnp.zeros((n_rows, w), h2p.dtype)
    kern = functools.partial(_dispatch_kernel, tm=tm)
    return pl.pallas_call(
        kern,
        out_shape=jax.ShapeDtypeStruct((n_rows, w), h2p.dtype),
        grid=(t // tm,),
        in_specs=[pl.BlockSpec((1, 1, TOP_K * tm), lambda i: (i, 0, 0), memory_space=pltpu.SMEM),
                  pl.BlockSpec((tm, w), lambda i: (i, 0)),
                  pl.BlockSpec(memory_space=pl.ANY)],
        out_specs=pl.BlockSpec(memory_space=pl.ANY),
        scratch_shapes=[pltpu.SemaphoreType.DMA],
        input_output_aliases={2: 0},
        compiler_params=_cparams(("arbitrary",), VMEM_LIMIT),
        name="dispatch",
    )(dest_tiles, h2p, xs0)


GU_GROUP = 256


def _ffn_kernel(be_ref, nu_ref, x_ref, wgu_ref, bgu_ref, wd_ref, bd_ref, perm_ref, y_ref, wgu_s, wd_s):
    i = pl.program_id(0)
    live = i < nu_ref[0]
    new_expert = jnp.logical_or(i == 0, be_ref[i] != be_ref[jnp.maximum(i - 1, 0)])

    @pl.when(jnp.logical_and(live, new_expert))
    def _():
        for g in range(2 * D_EXPERT // GU_GROUP):
            sl = slice(g * GU_GROUP, (g + 1) * GU_GROUP)
            wgu_s[:, sl] = _mm(wgu_ref[0, :, sl].astype(BF16), perm_ref[...]).astype(BF16)
        wd_s[...] = wd_ref[0].astype(BF16)

    @pl.when(live)
    def _():
        u = x_ref[...]
        lo = pltpu.bitcast(u << 16, F32).astype(BF16)
        hi = pltpu.bitcast(u & jnp.uint32(0xFFFF0000), F32).astype(BF16)
        x = jnp.concatenate([lo, hi], axis=1)
        gu = _mm(x, wgu_s[...]) + bgu_ref[0]
        half = GU_GROUP // 2
        n_grp = 2 * D_EXPERT // GU_GROUP
        glu = jnp.concatenate([gu[:, g * GU_GROUP:g * GU_GROUP + half] for g in range(n_grp)], axis=1)
        lin = jnp.concatenate([gu[:, g * GU_GROUP + half:(g + 1) * GU_GROUP] for g in range(n_grp)], axis=1)
        glu = jnp.minimum(glu, SWIGLU_LIMIT)
        lin = jnp.clip(lin, -SWIGLU_LIMIT, SWIGLU_LIMIT)
        act = glu * _sigmoid(SWIGLU_ALPHA * glu) * (lin + 1.0)
        y_ref[...] = _mm(act.astype(BF16), wd_s[...]) + bd_ref[0]

    @pl.when(jnp.logical_not(live))
    def _():
        y_ref[...] = jnp.zeros_like(y_ref)


def _ffn(block_e, n_used, xs, w_gu, b_gu, w_d, b_d):
    n_rows, w = xs.shape
    bm = FFN_ROWS
    nblk = n_rows // bm
    d = w * 2
    half = GU_GROUP // 2
    perm = np.zeros((GU_GROUP, GU_GROUP), np.float32)
    perm[2 * np.arange(half), np.arange(half)] = 1.0
    perm[2 * np.arange(half) + 1, half + np.arange(half)] = 1.0

    def xmap(i, be, nu):
        return (jnp.minimum(i, nu[0] - 1), 0)

    def wmap(i, be, nu):
        return (be[i], 0, 0)

    grid_spec = pltpu.PrefetchScalarGridSpec(
        num_scalar_prefetch=2,
        grid=(nblk,),
        in_specs=[pl.BlockSpec((bm, w), xmap),
                  pl.BlockSpec((1, d, 2 * D_EXPERT), wmap),
                  pl.BlockSpec((1, 1, 2 * D_EXPERT), wmap),
                  pl.BlockSpec((1, D_EXPERT, d), wmap),
                  pl.BlockSpec((1, 1, d), wmap),
                  pl.BlockSpec((GU_GROUP, GU_GROUP), lambda i, be, nu: (0, 0))],
        out_specs=pl.BlockSpec((bm, d), lambda i, be, nu: (i, 0)),
        scratch_shapes=[pltpu.VMEM((d, 2 * D_EXPERT), BF16), pltpu.VMEM((D_EXPERT, d), BF16)],
    )
    return pl.pallas_call(
        _ffn_kernel,
        out_shape=jax.ShapeDtypeStruct((n_rows, d), F32),
        grid_spec=grid_spec,
        compiler_params=_cparams(("arbitrary",), VMEM_LIMIT),
        name="ffn",
    )(block_e, n_used, xs, w_gu, b_gu, w_d, b_d, jnp.asarray(perm, dtype=BF16))


def _combine_kernel(dest_ref, y_ref, p_ref, x1_ref, mod_ref, g_ref, o_ref, buf_ref, sem, *, tm):
    b = pl.program_id(0)

    def body(r, carry):
        for k in range(TOP_K):
            pltpu.make_async_copy(y_ref.at[pl.ds(dest_ref[0, 0, k * tm + r], 1)],
                                  buf_ref.at[k, pl.ds(r, 1)], sem).start()
        return carry

    lax.fori_loop(0, tm, body, 0)
    for k in range(TOP_K):
        pltpu.make_async_copy(y_ref.at[pl.ds(0, tm)], buf_ref.at[k], sem).wait()
    p = p_ref[...]
    acc = p[:, 0:1] * buf_ref[0]
    for k in range(1, TOP_K):
        acc = acc + p[:, k:k + 1] * buf_ref[k]
    m = mod_ref[pl.ds(b, 1), :]
    xo = x1_ref[0] + m[:, 5 * D_MODEL:6 * D_MODEL] * acc
    ms = jnp.mean(xo * xo, axis=-1, keepdims=True)
    o_ref[0] = xo * lax.rsqrt(ms + EPS) * g_ref[...]


def _combine(dest_tiles, y, probs_t, x1, mod, g_fin, tm):
    bsz, n, d = x1.shape
    nt = n // tm
    kern = functools.partial(_combine_kernel, tm=tm)
    return pl.pallas_call(
        kern,
        out_shape=jax.ShapeDtypeStruct((bsz, n, d), F32),
        grid=(bsz, nt),
        in_specs=[pl.BlockSpec((1, 1, TOP_K * tm), lambda b, t: (b * nt + t, 0, 0), memory_space=pltpu.SMEM),
                  pl.BlockSpec(memory_space=pl.ANY),
                  pl.BlockSpec((tm, TOP_K), lambda b, t: (b * nt + t, 0)),
                  pl.BlockSpec((1, tm, d), lambda b, t: (b, t, 0)),
                  pl.BlockSpec(mod.shape, lambda b, t: (0, 0)),
                  pl.BlockSpec((1, d), lambda b, t: (0, 0))],
        out_specs=pl.BlockSpec((1, tm, d), lambda b, t: (b, t, 0)),
        scratch_shapes=[pltpu.VMEM((TOP_K, tm, d), F32), pltpu.SemaphoreType.DMA],
        compiler_params=_cparams(("arbitrary", "arbitrary"), VMEM_LIMIT),
        name="combine",
    )(dest_tiles, y, probs_t, x1, mod, g_fin)


def _prep_w_in(w_in):
    q_a, kv, hq, ff, fb, hi, go, mm, mh = jnp.split(
        w_in, np.cumsum((MLA_Q_RANK, MLA_KV_RANK + MLA_ROPE) + (D_MODEL,) * 6).tolist(), axis=1)
    c_kv, kr = kv[:, :MLA_KV_RANK], kv[:, MLA_KV_RANK:]
    half = MLA_ROPE // 2
    kr_sw = jnp.concatenate([-kr[:, half:], kr[:, :half]], axis=1)
    w16 = jnp.concatenate([hq, hi, go, mm, mh, q_a, c_kv, kr, kr_sw], axis=1)
    w16 = jnp.pad(w16, ((0, 0), (0, N_PROJ16 - w16.shape[1])))
    return w16.astype(BF16), jnp.concatenate([ff, fb], axis=1).astype(BF16)


def _prep_w_q(w_q_b):
    w = w_q_b.reshape(MLA_Q_RANK, HEADS, MLA_QK)
    w = jnp.pad(w, ((0, 0), (0, 0), (0, MLA_QK_PAD - MLA_QK)))
    return w.reshape(MLA_Q_RANK, HEADS * MLA_QK_PAD).astype(BF16)


def _prep_w_kv(w_kv_b):
    w = w_kv_b.reshape(MLA_KV_RANK, HEADS, 2 * HEAD_DIM)
    k = w[:, :, :HEAD_DIM].reshape(MLA_KV_RANK, HEADS * HEAD_DIM)
    v = w[:, :, HEAD_DIM:].reshape(MLA_KV_RANK, HEADS * HEAD_DIM)
    return jnp.concatenate([k, v], axis=1).astype(BF16)


def _rope_tables(n_lat, n_ctx):
    f32 = np.float32
    rows = n_lat // GRID_W
    row = np.repeat(np.arange(rows), GRID_W).astype(f32)
    col = np.tile(np.arange(GRID_W), rows).astype(f32)
    n_freq = MLA_ROPE // 4
    inv = (f32(ROPE_BASE) ** (-np.arange(n_freq, dtype=f32) / f32(n_freq))).astype(f32)
    ang = np.concatenate([row[:, None] * inv, col[:, None] * inv], axis=-1).astype(f32)
    cos, sin = np.cos(ang).astype(f32), np.sin(ang).astype(f32)
    z64 = np.zeros((n_lat, 64), f32)
    cos_q = np.concatenate([cos, cos, z64], axis=1)
    sin_q = np.concatenate([-sin, sin, z64], axis=1)
    cos_k = np.concatenate([cos, cos, z64], axis=1)
    sin_k = np.concatenate([sin, sin, z64], axis=1)
    ctx_cos = np.concatenate([np.ones((n_ctx, 64), f32), np.zeros((n_ctx, 64), f32)], axis=1)
    cos_k = np.concatenate([cos_k, ctx_cos], axis=0)
    sin_k = np.concatenate([sin_k, np.zeros((n_ctx, 128), f32)], axis=0)
    return jnp.asarray(cos_q), jnp.asarray(sin_q), jnp.asarray(cos_k), jnp.asarray(sin_k)


def kernel(x, c, ctx, c_ctx, w_mod, b_mod, norm_mix_g, w_in, mla_q_norm_g, w_q_b, mla_kv_norm_g, w_kv_b,
           hg_lb_logits, hg_norm_g, w_out, norm_ffn_g, w_router, b_router, w_gate_up, b_gate_up, w_down,
           b_down, final_norm_g):
    bsz, n_lat, d = x.shape
    n_ctx = ctx.shape[1]
    assert d == D_MODEL and w_mod.shape[0] == 1
    assert n_lat % HG_CHUNK == 0 and n_ctx % HG_CHUNK == 0 and n_lat % GRID_W == 0
    t_tok = bsz * n_lat

    mod_rows = -(-(bsz + 1) // 8) * 8
    cc = jnp.concatenate([c, c_ctx[None, :], jnp.zeros((mod_rows - bsz - 1, d), F32)], axis=0)
    mod = _modulation(cc, w_mod[0], b_mod[0][None, :])

    xa = jnp.concatenate([x, ctx], axis=1)
    w16, w32 = _prep_w_in(w_in[0])
    p, pf = _inproj(xa, mod, norm_mix_g[0][None, :], w16, w32, n_lat)

    cos_q, sin_q, cos_k, sin_k = _rope_tables(n_lat, n_ctx)
    q = _mla_q(p, mla_q_norm_g[0][None, :], _prep_w_q(w_q_b[0]), cos_q, sin_q, n_lat)
    k, v = _mla_kv(p, mla_kv_norm_g[0][None, :], _prep_w_kv(w_kv_b[0]), cos_k, sin_k)
    y_mla = _attention(q, k, v)

    lb = jax.nn.softmax(hg_lb_logits.astype(F32), axis=1)[:, 0, :]
    o_f, o_b = _hgrn_scans(p, pf, lb.reshape(2, HEADS, 1, HEAD_DIM), n_lat)

    x1, h2p, logits_t = _merge(y_mla, o_f, o_b, p, x, mod, hg_norm_g[0][None, :], w_out[0].astype(BF16),
                               norm_ffn_g[0][None, :], w_router[0].T, b_router[0][:, None])

    idx_t, prob_t, rank_t, cnt = _router(logits_t)

    counts = cnt[:, 0].astype(jnp.int32)
    padded = (counts + FFN_ROWS - 1) // FFN_ROWS * FFN_ROWS
    pad_end = jnp.cumsum(padded)
    pad_start = pad_end - padded
    dest = _dest_rows(pad_start.astype(jnp.int32), idx_t, rank_t)
    n_rows = -(-(t_tok * TOP_K + N_EXPERTS * (FFN_ROWS - 1)) // FFN_ROWS) * FFN_ROWS
    nblk = n_rows // FFN_ROWS
    block_start = jnp.arange(nblk, dtype=jnp.int32) * FFN_ROWS
    block_e = jnp.minimum(jnp.sum(pad_end[None, :] <= block_start[:, None], axis=1), N_EXPERTS - 1).astype(jnp.int32)
    n_used = (pad_end[-1:] // FFN_ROWS).astype(jnp.int32)

    tm_c = _tile(n_lat, 256)
    dest_tiles = dest.reshape(TOP_K, t_tok // tm_c, tm_c).transpose(1, 0, 2).reshape(t_tok // tm_c, 1, TOP_K * tm_c)

    xs = _dispatch(dest_tiles, h2p, n_rows, tm_c)

    n_grp = 2 * D_EXPERT // GU_GROUP
    b_gu = b_gate_up[0].reshape(N_EXPERTS, n_grp, GU_GROUP // 2, 2).transpose(0, 1, 3, 2).reshape(N_EXPERTS, 1, 2 * D_EXPERT)
    y = _ffn(block_e, n_used, xs, w_gate_up[0], b_gu, w_down[0], b_down[0][:, None, :])

    return _combine(dest_tiles, y, prob_t.T, x1, mod, final_norm_g[None, :], tm_c)
```

```python
import functools

import jax
import jax.numpy as jnp
import numpy as np
from jax import lax
from jax.experimental import pallas as pl
from jax.experimental.pallas import tpu as pltpu

F32 = jnp.float32
BF16 = jnp.bfloat16

D_MODEL = 1024
EPS = 1e-6
LOG2_E = 1.4426950408889634
N_MOD = 6
GRID_W = 64
ROPE_BASE = 10000.0

HEADS = 8
HEAD_DIM = 128
MLA_ROPE = 64
MLA_QK = HEAD_DIM + MLA_ROPE
MLA_QK_PAD = 256
MLA_Q_RANK = 768
MLA_KV_RANK = 256

N_EXPERTS = 32
TOP_K = 4
D_EXPERT = 1024
SWIGLU_LIMIT = 7.0
SWIGLU_ALPHA = 1.702

HG_CHUNK = 128
FFN_ROWS = 512

COL_HQ, COL_I, COL_GO, COL_MM, COL_MH = (i * D_MODEL for i in range(5))
COL_QA = 5 * D_MODEL
COL_CKV = COL_QA + MLA_Q_RANK
COL_KR = COL_CKV + MLA_KV_RANK
PROJ16_TN = 1280
N_PROJ16 = -(-(COL_KR + 2 * MLA_ROPE) // PROJ16_TN) * PROJ16_TN
COL_FF, COL_FB = 0, D_MODEL
N_PROJ32 = 2 * D_MODEL
PROJ32_TN = 1024

VMEM_LIMIT = 56 * 1024 * 1024


def _cparams(sem, vmem=None):
    return pltpu.CompilerParams(dimension_semantics=sem, vmem_limit_bytes=vmem)


def _tile(n, pref, mult=8):
    best = None
    for t in range(mult, min(n, pref) + 1, mult):
        if n % t == 0:
            best = t
    assert best is not None, (n, pref, mult)
    return best


def _nt(a, b):
    return lax.dot_general(a, b, (((1,), (1,)), ((), ())), preferred_element_type=F32)


def _mm(a, b):
    return jnp.dot(a, b, preferred_element_type=F32)


def _split2(a):
    hi = a.astype(BF16)
    lo = (a - hi.astype(F32)).astype(BF16)
    return hi, lo


def _sigmoid(x):
    return 1.0 / (1.0 + jnp.exp(-x))


def _mod_kernel(c_ref, w_ref, b_ref, o_ref):
    c = c_ref[...]
    s = c * _sigmoid(c)
    s_hi, s_lo = _split2(s)
    w_hi, w_lo = _split2(w_ref[...])
    o_ref[...] = _mm(s_hi, w_hi) + _mm(s_lo, w_hi) + _mm(s_hi, w_lo) + b_ref[...]


def _modulation(cc, w_mod, b_mod):
    r, d = cc.shape
    n = w_mod.shape[1]
    tn = _tile(n, 1536, 128)
    return pl.pallas_call(
        _mod_kernel,
        out_shape=jax.ShapeDtypeStruct((r, n), F32),
        grid=(n // tn,),
        in_specs=[pl.BlockSpec((r, d), lambda j: (0, 0)),
                  pl.BlockSpec((d, tn), lambda j: (0, j)),
                  pl.BlockSpec((1, tn), lambda j: (0, j))],
        out_specs=pl.BlockSpec((r, tn), lambda j: (0, j)),
        compiler_params=_cparams(("parallel",), VMEM_LIMIT),
        name="mod",
    )(cc, w_mod, b_mod)


def _inproj_kernel(x_ref, mod_ref, g_ref, w16_ref, w32_ref, o16_ref, o32_ref, hn_ref, *, n_lat, tm, ctx_row, n16):
    b = pl.program_id(0)
    t = pl.program_id(1)
    j = pl.program_id(2)

    @pl.when(j == 0)
    def _():
        x = x_ref[0]
        ms = jnp.mean(x * x, axis=-1, keepdims=True)
        y = x * lax.rsqrt(ms + EPS) * g_ref[...]
        row = t * tm + lax.broadcasted_iota(jnp.int32, (tm, 1), 0)
        is_ctx = row >= n_lat
        m_lat = mod_ref[pl.ds(b, 1), :]
        m_ctx = mod_ref[pl.ds(ctx_row, 1), :]
        shift = jnp.where(is_ctx, m_ctx[:, 0:D_MODEL], m_lat[:, 0:D_MODEL])
        scale = jnp.where(is_ctx, m_ctx[:, D_MODEL:2 * D_MODEL], m_lat[:, D_MODEL:2 * D_MODEL])
        hn_ref[...] = (y * (1.0 + scale) + shift).astype(BF16)

    @pl.when(j < n16)
    def _():
        o16_ref[0] = _mm(hn_ref[...], w16_ref[...]).astype(BF16)

    @pl.when(j >= n16)
    def _():
        o32_ref[0] = _mm(hn_ref[...], w32_ref[...])


def _inproj(xa, mod, g, w16, w32, n_lat):
    bsz, rows, d = xa.shape
    tm = _tile(rows, 1152)
    n16 = N_PROJ16 // PROJ16_TN
    n32 = N_PROJ32 // PROJ32_TN

    def j16(j):
        return jnp.minimum(j, n16 - 1)

    def j32(j):
        return jnp.maximum(j - n16, 0)

    kern = functools.partial(_inproj_kernel, n_lat=n_lat, tm=tm, ctx_row=bsz, n16=n16)
    return pl.pallas_call(
        kern,
        out_shape=(jax.ShapeDtypeStruct((bsz, rows, N_PROJ16), BF16),
                   jax.ShapeDtypeStruct((bsz, rows, N_PROJ32), F32)),
        grid=(bsz, rows // tm, n16 + n32),
        in_specs=[pl.BlockSpec((1, tm, d), lambda b, t, j: (b, t, 0)),
                  pl.BlockSpec(mod.shape, lambda b, t, j: (0, 0)),
                  pl.BlockSpec((1, d), lambda b, t, j: (0, 0)),
                  pl.BlockSpec((d, PROJ16_TN), lambda b, t, j: (0, j16(j))),
                  pl.BlockSpec((d, PROJ32_TN), lambda b, t, j: (0, j32(j)))],
        out_specs=(pl.BlockSpec((1, tm, PROJ16_TN), lambda b, t, j: (b, t, j16(j))),
                   pl.BlockSpec((1, tm, PROJ32_TN), lambda b, t, j: (b, t, j32(j)))),
        scratch_shapes=[pltpu.VMEM((tm, d), BF16)],
        compiler_params=_cparams(("parallel", "parallel", "arbitrary"), VMEM_LIMIT),
        name="inproj",
    )(xa, mod, g, w16, w32)


def _mlaq_kernel(a0_ref, a1_ref, a2_ref, g_ref, w_ref, cos_ref, sin_ref, o_ref, *, tm):
    parts = [a0_ref[0].astype(F32), a1_ref[0].astype(F32), a2_ref[0].astype(F32)]
    ss = sum(jnp.sum(p * p, axis=-1, keepdims=True) for p in parts)
    r = lax.rsqrt(ss * (1.0 / MLA_Q_RANK) + EPS)
    acc = None
    for j, p in enumerate(parts):
        hj = (p * r * g_ref[:, j * 256:(j + 1) * 256]).astype(BF16)
        d = _mm(hj, w_ref[j * 256:(j + 1) * 256, :])
        acc = d if acc is None else acc + d
    lane = lax.broadcasted_iota(jnp.int32, (tm, HEAD_DIM), 1)
    scale = MLA_QK ** -0.5 * LOG2_E
    cos = cos_ref[...]
    sin = sin_ref[...]
    for h in range(HEADS):
        nope = acc[:, h * MLA_QK_PAD:h * MLA_QK_PAD + HEAD_DIM]
        rp = acc[:, h * MLA_QK_PAD + HEAD_DIM:(h + 1) * MLA_QK_PAD]
        swapped = jnp.where(lane < 32, pltpu.roll(rp, 96, 1), pltpu.roll(rp, 32, 1))
        rot = rp * cos + swapped * sin
        o_ref[0, h, :, 0:HEAD_DIM] = (nope * scale).astype(BF16)
        o_ref[0, h, :, HEAD_DIM:MLA_QK_PAD] = (rot * scale).astype(BF16)


def _mla_q(p, g, w, cos_q, sin_q, n_lat):
    bsz = p.shape[0]
    tm = _tile(n_lat, 512)
    cb = COL_QA // 256
    kern = functools.partial(_mlaq_kernel, tm=tm)
    return pl.pallas_call(
        kern,
        out_shape=jax.ShapeDtypeStruct((bsz, HEADS, n_lat, MLA_QK_PAD), BF16),
        grid=(bsz, n_lat // tm),
        in_specs=[pl.BlockSpec((1, tm, 256), lambda b, t: (b, t, cb)),
                  pl.BlockSpec((1, tm, 256), lambda b, t: (b, t, cb + 1)),
                  pl.BlockSpec((1, tm, 256), lambda b, t: (b, t, cb + 2)),
                  pl.BlockSpec((1, MLA_Q_RANK), lambda b, t: (0, 0)),
                  pl.BlockSpec(w.shape, lambda b, t: (0, 0)),
                  pl.BlockSpec((tm, HEAD_DIM), lambda b, t: (t, 0)),
                  pl.BlockSpec((tm, HEAD_DIM), lambda b, t: (t, 0))],
        out_specs=pl.BlockSpec((1, HEADS, tm, MLA_QK_PAD), lambda b, t: (b, 0, t, 0)),
        compiler_params=_cparams(("parallel", "parallel"), VMEM_LIMIT),
        name="mla_q",
    )(p, p, p, g, w, cos_q, sin_q)


def _mlakv_kernel(c_ref, kr_ref, g_ref, w_ref, cos_ref, sin_ref, k_ref, v_ref):
    c = c_ref[0].astype(F32)
    ms = jnp.mean(c * c, axis=-1, keepdims=True)
    hn = (c * lax.rsqrt(ms + EPS) * g_ref[...]).astype(BF16)
    kv = _mm(hn, w_ref[...])
    grp = kr_ref[0].astype(F32)
    rot = (grp * cos_ref[...] + pltpu.roll(grp, 64, 1) * sin_ref[...]).astype(BF16)
    for h in range(HEADS):
        k_ref[0, h, :, 0:HEAD_DIM] = kv[:, h * HEAD_DIM:(h + 1) * HEAD_DIM].astype(BF16)
        k_ref[0, h, :, HEAD_DIM:MLA_QK_PAD] = rot
        v_ref[0, h] = kv[:, D_MODEL + h * HEAD_DIM:D_MODEL + (h + 1) * HEAD_DIM].astype(BF16)


def _mla_kv(p, g, w, cos_k, sin_k):
    bsz, rows, _ = p.shape
    tm = _tile(rows, 768)
    return pl.pallas_call(
        _mlakv_kernel,
        out_shape=(jax.ShapeDtypeStruct((bsz, HEADS, rows, MLA_QK_PAD), BF16),
                   jax.ShapeDtypeStruct((bsz, HEADS, rows, HEAD_DIM), BF16)),
        grid=(bsz, rows // tm),
        in_specs=[pl.BlockSpec((1, tm, MLA_KV_RANK), lambda b, t: (b, t, COL_CKV // MLA_KV_RANK)),
                  pl.BlockSpec((1, tm, 128), lambda b, t: (b, t, COL_KR // 128)),
                  pl.BlockSpec((1, MLA_KV_RANK), lambda b, t: (0, 0)),
                  pl.BlockSpec(w.shape, lambda b, t: (0, 0)),
                  pl.BlockSpec((tm, 128), lambda b, t: (t, 0)),
                  pl.BlockSpec((tm, 128), lambda b, t: (t, 0))],
        out_specs=(pl.BlockSpec((1, HEADS, tm, MLA_QK_PAD), lambda b, t: (b, 0, t, 0)),
                   pl.BlockSpec((1, HEADS, tm, HEAD_DIM), lambda b, t: (b, 0, t, 0))),
        compiler_params=_cparams(("parallel", "parallel"), VMEM_LIMIT),
        name="mla_kv",
    )(p, p, g, w, cos_k, sin_k)


def _attn_kernel(q_ref, k_ref, v_ref, o_ref, *, tq, sub):
    k = k_ref[0, 0]
    v = v_ref[0, 0]
    v1 = jnp.concatenate([v, jnp.ones_like(v)], axis=1)
    for r in range(tq // sub):
        rows = slice(r * sub, (r + 1) * sub)
        s = _nt(q_ref[0, 0, rows, :], k)
        m = jnp.max(s, axis=-1, keepdims=True)
        p = jnp.exp2(s - m).astype(BF16)
        ol = _mm(p, v1)
        o_ref[0, rows, :] = ol[:, 0:HEAD_DIM] * (1.0 / ol[:, HEAD_DIM:HEAD_DIM + 1])


def _attention(q, k, v):
    bsz, _, n, _ = q.shape
    m = k.shape[2]
    tq = _tile(n, 2048)
    sub = _tile(tq, 512)
    kern = functools.partial(_attn_kernel, tq=tq, sub=sub)
    return pl.pallas_call(
        kern,
        out_shape=jax.ShapeDtypeStruct((bsz, n, HEADS * HEAD_DIM), F32),
        grid=(bsz, HEADS, n // tq),
        in_specs=[pl.BlockSpec((1, 1, tq, MLA_QK_PAD), lambda b, h, t: (b, h, t, 0)),
                  pl.BlockSpec((1, 1, m, MLA_QK_PAD), lambda b, h, t: (b, h, 0, 0)),
                  pl.BlockSpec((1, 1, m, HEAD_DIM), lambda b, h, t: (b, h, 0, 0))],
        out_specs=pl.BlockSpec((1, tq, HEAD_DIM), lambda b, h, t: (b, t, h)),
        compiler_params=_cparams(("parallel", "parallel", "arbitrary"), VMEM_LIMIT),
        name="attention",
    )(q, k, v)


def _level_ref(cum, blk, reverse):
    c = cum.shape[0]
    half = blk // 2
    r = half if reverse else half - 1
    if blk >= 8:
        x = cum.reshape(c // blk, blk, HEAD_DIM)
        e = jnp.broadcast_to(x[:, r:r + 1, :], x.shape)
        return e.reshape(c, HEAD_DIM)
    x = cum.reshape(c // 8, 8, HEAD_DIM)
    sub = lax.broadcasted_iota(jnp.int32, x.shape, 1)
    e = None
    for jb in range(8 // blk):
        cand = jnp.broadcast_to(x[:, jb * blk + r:jb * blk + r + 1, :], x.shape)
        e = cand if e is None else jnp.where(sub >= jb * blk, cand, e)
    return e.reshape(c, HEAD_DIM)


def _hgrn_chunk(q, z, v, lb, tri, lvl, st, *, chunk, reverse, want_out):
    f = lb + (1.0 - lb) * _sigmoid(z)
    kb = (1.0 - f).astype(BF16)
    g = jnp.log(f) * LOG2_E
    vb = v.astype(BF16)

    g_hi, g_lo = _split2(g)
    cum = _mm(tri, jnp.concatenate([g_hi, g_lo], axis=0))
    last = 0 if reverse else chunk - 1
    tot = cum[last:last + 1, :]
    kt = kb * jnp.exp2(tot - cum).astype(BF16)
    st_new = st * jnp.exp2(tot) + _mm(v.astype(F32).T.astype(BF16), kt)
    if not want_out:
        return None, st_new

    q = q.astype(F32)
    qb = (q * _sigmoid(q) * (HEAD_DIM ** -0.5)).astype(BF16)

    att = jnp.where(lvl == 0, _nt(qb, kb).astype(BF16), jnp.zeros((), BF16))
    n_lvl = chunk.bit_length() - 1
    for lv in range(1, n_lvl + 1):
        zrel = cum - _level_ref(cum, 1 << lv, reverse)
        neg_abs = pltpu.bitcast(pltpu.bitcast(zrel, jnp.uint32) | jnp.uint32(0x80000000), F32)
        e = jnp.exp2(neg_abs).astype(BF16)
        att = jnp.where(lvl == lv, _nt(qb * e, kb * e).astype(BF16), att)

    o = _nt(qb * jnp.exp2(cum).astype(BF16), st.astype(BF16)) + _mm(att, vb)
    return o, st_new


def _hgrn_kernel(qf_ref, zf_ref, if_ref, qb_ref, zb_ref, ib_ref, lb_ref, tri_ref, lvl_ref, of_ref, ob_ref, st_ref,
                 *, chunk, group, n_ctx_chunks):
    step = pl.program_id(2)

    @pl.when(step == 0)
    def _():
        st_ref[...] = jnp.zeros_like(st_ref)

    ins = ((qf_ref, zf_ref, if_ref), (qb_ref, zb_ref, ib_ref))
    outs = (of_ref, ob_ref)
    chains = [(d, j) for j in range(group) for d in range(2)]

    def advance(want_out):
        states = [st_ref[d, j] for d, j in chains]
        results = []
        for (d, j), st in zip(chains, states):
            q_ref, z_ref, i_ref = ins[d]
            sl = slice(j * HEAD_DIM, (j + 1) * HEAD_DIM)
            results.append(_hgrn_chunk(q_ref[0, :, sl], z_ref[0, :, sl], i_ref[0, :, sl], lb_ref[d, j], tri_ref[d],
                                       lvl_ref[d], st, chunk=chunk, reverse=bool(d), want_out=want_out))
        for (d, j), (o, st) in zip(chains, results):
            if want_out:
                outs[d][0, j] = o
            st_ref[d, j] = st

    @pl.when(step < n_ctx_chunks)
    def _():
        advance(False)

    @pl.when(step >= n_ctx_chunks)
    def _():
        advance(True)


def _hgrn_consts(chunk, reverse):
    t = np.arange(chunk)[:, None]
    s = np.arange(chunk)[None, :]
    x = t ^ s
    bitlen = np.zeros_like(x)
    for b in range(chunk.bit_length()):
        bitlen = np.where(x >> b > 0, b + 1, bitlen)
    valid = (s > t) if reverse else (t > s)
    lvl = np.where(t == s, 0, np.where(valid, bitlen, -1)).astype(np.float32)
    tri = ((s >= t) if reverse else (t >= s)).astype(np.float32)
    return np.concatenate([tri, tri], axis=1), lvl


HG_GROUP = 4


def _hgrn_scans(p, pf, lb, n_lat):
    bsz, rows, _ = p.shape
    c = HG_CHUNK
    grp = HG_GROUP
    n_chunks = rows // c
    n_lat_c = n_lat // c
    n_ctx_c = n_chunks - n_lat_c
    consts = [_hgrn_consts(c, False), _hgrn_consts(c, True)]
    tri = jnp.asarray(np.stack([consts[0][0], consts[1][0]]), dtype=BF16)
    lvl = jnp.asarray(np.stack([consts[0][1], consts[1][1]]), dtype=BF16)

    def cidx_f(i):
        return jnp.where(i < n_ctx_c, n_lat_c + i, i - n_ctx_c)

    def cidx_b(i):
        return n_chunks - 1 - i

    def col_spec(col, cidx):
        return pl.BlockSpec((1, c, grp * HEAD_DIM), lambda b, h, i: (b, cidx(i), col // (grp * HEAD_DIM) + h))

    def out_spec(cidx):
        return pl.BlockSpec((1, grp, c, HEAD_DIM), lambda b, h, i: (b, h, cidx(jnp.maximum(i, n_ctx_c)), 0))

    o_shape = jax.ShapeDtypeStruct((bsz, HEADS, n_lat, HEAD_DIM), F32)
    kern = functools.partial(_hgrn_kernel, chunk=c, group=grp, n_ctx_chunks=n_ctx_c)
    return pl.pallas_call(
        kern,
        out_shape=(o_shape, o_shape),
        grid=(bsz, HEADS // grp, n_chunks),
        in_specs=[col_spec(COL_HQ, cidx_f), col_spec(COL_FF, cidx_f), col_spec(COL_I, cidx_f),
                  col_spec(COL_HQ, cidx_b), col_spec(COL_FB, cidx_b), col_spec(COL_I, cidx_b),
                  pl.BlockSpec((2, grp, 1, HEAD_DIM), lambda b, h, i: (0, h, 0, 0)),
                  pl.BlockSpec((2, c, 2 * c), lambda b, h, i: (0, 0, 0)),
                  pl.BlockSpec((2, c, c), lambda b, h, i: (0, 0, 0))],
        out_specs=(out_spec(cidx_f), out_spec(cidx_b)),
        scratch_shapes=[pltpu.VMEM((2, grp, HEAD_DIM, HEAD_DIM), F32)],
        compiler_params=_cparams(("parallel", "parallel", "arbitrary"), VMEM_LIMIT),
        name="hgrn",
    )(p, pf, p, p, pf, p, lb, tri, lvl)


def _merge_kernel(ym_ref, of_ref, ob_ref, go_ref, gm_ref, gh_ref, x_ref, mod_ref, hgg_ref, wout_ref,
                  gffn_ref, wr_ref, br_ref, x1_ref, h2_ref, lg_ref, y_scr):
    b = pl.program_id(0)
    for h in range(HEADS):
        sl = slice(h * HEAD_DIM, (h + 1) * HEAD_DIM)
        o = of_ref[0, h] + ob_ref[0, h]
        ms = jnp.mean(o * o, axis=-1, keepdims=True)
        g = go_ref[0, :, sl].astype(F32)
        yh = o * lax.rsqrt(ms + EPS) * hgg_ref[...] * (g * _sigmoid(g))
        y = (_sigmoid(gm_ref[0, :, sl].astype(F32)) * ym_ref[0, :, sl]
             + _sigmoid(gh_ref[0, :, sl].astype(F32)) * yh)
        y_scr[:, sl] = y.astype(BF16)
    mix = _mm(y_scr[...], wout_ref[...])
    m = mod_ref[pl.ds(b, 1), :]
    x1 = x_ref[0] + m[:, 2 * D_MODEL:3 * D_MODEL] * mix
    x1_ref[0] = x1
    ms = jnp.mean(x1 * x1, axis=-1, keepdims=True)
    h2 = (x1 * lax.rsqrt(ms + EPS) * gffn_ref[...]) * (1.0 + m[:, 4 * D_MODEL:5 * D_MODEL]) + m[:, 3 * D_MODEL:4 * D_MODEL]
    h_hi = h2.astype(BF16)
    u = pltpu.bitcast(h_hi.astype(F32), jnp.uint32)
    half = D_MODEL // 2
    h2_ref[...] = (u[:, 0:half] >> 16) | u[:, half:D_MODEL]
    h_lo = (h2 - h_hi.astype(F32)).astype(BF16)
    w_hi, w_lo = _split2(wr_ref[...])
    lg_ref[...] = _nt(w_hi, h_hi) + _nt(w_lo, h_hi) + _nt(w_hi, h_lo) + br_ref[...]


def _merge(y_mla, o_f, o_b, p, x, mod, hg_g, w_out, g_ffn, w_r_t, b_r):
    bsz, n, d = x.shape
    tm = _tile(n, 256, 128)
    nt = n // tm

    def pcol(col):
        return pl.BlockSpec((1, tm, d), lambda b, t: (b, t, col // d))

    tok = lambda b, t: (b, t, 0)
    const2 = lambda b, t: (0, 0)
    return pl.pallas_call(
        _merge_kernel,
        out_shape=(jax.ShapeDtypeStruct((bsz, n, d), F32),
                   jax.ShapeDtypeStruct((bsz * n, d // 2), jnp.uint32),
                   jax.ShapeDtypeStruct((N_EXPERTS, bsz * n), F32)),
        grid=(bsz, nt),
        in_specs=[pl.BlockSpec((1, tm, d), tok),
                  pl.BlockSpec((1, HEADS, tm, HEAD_DIM), lambda b, t: (b, 0, t, 0)),
                  pl.BlockSpec((1, HEADS, tm, HEAD_DIM), lambda b, t: (b, 0, t, 0)),
                  pcol(COL_GO), pcol(COL_MM), pcol(COL_MH),
                  pl.BlockSpec((1, tm, d), tok),
                  pl.BlockSpec(mod.shape, const2),
                  pl.BlockSpec((1, HEAD_DIM), const2),
                  pl.BlockSpec((d, d), const2),
                  pl.BlockSpec((1, d), const2),
                  pl.BlockSpec((N_EXPERTS, d), const2),
                  pl.BlockSpec((N_EXPERTS, 1), const2)],
        out_specs=(pl.BlockSpec((1, tm, d), tok),
                   pl.BlockSpec((tm, d // 2), lambda b, t: (b * nt + t, 0)),
                   pl.BlockSpec((N_EXPERTS, tm), lambda b, t: (0, b * nt + t))),
        scratch_shapes=[pltpu.VMEM((tm, d), BF16)],
        compiler_params=_cparams(("parallel", "parallel"), VMEM_LIMIT),
        name="merge",
    )(y_mla, o_f, o_b, p, p, p, x, mod, hg_g, w_out, g_ffn, w_r_t, b_r)


def _router_kernel(lg_ref, upper_ref, idx_ref, prob_ref, rank_ref, cnt_ref, run_ref, *, tm):
    @pl.when(pl.program_id(0) == 0)
    def _():
        run_ref[...] = jnp.zeros_like(run_ref)

    l = lg_ref[...]
    eidx = lax.broadcasted_iota(jnp.int32, (N_EXPERTS, tm), 0)
    vals, sel = [], []
    for _ in range(TOP_K):
        m = jnp.max(l, axis=0, keepdims=True)
        first = jnp.min(jnp.where(l == m, eidx, N_EXPERTS), axis=0, keepdims=True)
        vals.append(m)
        sel.append(first)
        l = jnp.where(eidx == first, -jnp.inf, l)
    ex = [jnp.exp(v - vals[0]) for v in vals]
    inv = 1.0 / (ex[0] + ex[1] + ex[2] + ex[3])
    onehot = jnp.zeros((N_EXPERTS, tm), F32)
    for k in range(TOP_K):
        onehot = onehot + jnp.where(eidx == sel[k], 1.0, 0.0)
    before = _mm(onehot.astype(BF16), upper_ref[...]) + run_ref[:, 0:1]
    for k in range(TOP_K):
        idx_ref[k:k + 1, :] = sel[k]
        prob_ref[k:k + 1, :] = ex[k] * inv
        rank_ref[k:k + 1, :] = jnp.sum(jnp.where(eidx == sel[k], before, 0.0), axis=0, keepdims=True).astype(jnp.int32)
    run_ref[...] = run_ref[...] + jnp.sum(onehot, axis=1, keepdims=True)
    cnt_ref[...] = run_ref[...]


def _router(logits_t):
    _, t = logits_t.shape
    tm = _tile(t, 1024, 128)
    upper = jnp.asarray(np.triu(np.ones((tm, tm), np.float32), 1), dtype=BF16)
    kern = functools.partial(_router_kernel, tm=tm)
    tok = pl.BlockSpec((TOP_K, tm), lambda i: (0, i))
    return pl.pallas_call(
        kern,
        out_shape=(jax.ShapeDtypeStruct((TOP_K, t), jnp.int32),
                   jax.ShapeDtypeStruct((TOP_K, t), F32),
                   jax.ShapeDtypeStruct((TOP_K, t), jnp.int32),
                   jax.ShapeDtypeStruct((N_EXPERTS, 128), F32)),
        grid=(t // tm,),
        in_specs=[pl.BlockSpec((N_EXPERTS, tm), lambda i: (0, i)),
                  pl.BlockSpec((tm, tm), lambda i: (0, 0))],
        out_specs=(tok, tok, tok, pl.BlockSpec((N_EXPERTS, 128), lambda i: (0, 0))),
        scratch_shapes=[pltpu.VMEM((N_EXPERTS, 128), F32)],
        compiler_params=_cparams(("arbitrary",), VMEM_LIMIT),
        name="router",
    )(logits_t, upper)


def _dest_kernel(ps_ref, idx_ref, rank_ref, o_ref):
    idx = idx_ref[...]
    dest = rank_ref[...]
    for e in range(N_EXPERTS):
        dest = dest + jnp.where(idx == e, ps_ref[e], 0)
    o_ref[...] = dest


def _dest_rows(pad_start, idx_t, rank_t):
    k, t = idx_t.shape
    tm = _tile(t, 4096, 128)
    tok = pl.BlockSpec((k, tm), lambda i, ps: (0, i))
    return pl.pallas_call(
        _dest_kernel,
        out_shape=jax.ShapeDtypeStruct((k, t), jnp.int32),
        grid_spec=pltpu.PrefetchScalarGridSpec(num_scalar_prefetch=1, grid=(t // tm,), in_specs=[tok, tok], out_specs=tok),
        compiler_params=_cparams(("parallel",), VMEM_LIMIT),
        name="dest_rows",
    )(pad_start, idx_t, rank_t)


def _dispatch_kernel(dest_ref, h2_ref, xs_in_ref, xs_ref, sem, *, tm):
    del xs_in_ref

    def body(r, carry):
        for k in range(TOP_K):
            pltpu.make_async_copy(h2_ref.at[pl.ds(r, 1)], xs_ref.at[pl.ds(dest_ref[0, 0, k * tm + r], 1)], sem).start()
        return carry

    lax.fori_loop(0, tm, body, 0)
    for _ in range(TOP_K):
        pltpu.make_async_copy(h2_ref, xs_ref.at[pl.ds(0, tm)], sem).wait()


def _dispatch(dest_tiles, h2p, n_rows, tm):
    t, w = h2p.shape
    xs0 = jnp.zeros((n_rows, w), h2p.dtype)
    kern = functools.partial(_dispatch_kernel, tm=tm)
    return pl.pallas_call(
        kern,
        out_shape=jax.ShapeDtypeStruct((n_rows, w), h2p.dtype),
        grid=(t // tm,),
        in_specs=[pl.BlockSpec((1, 1, TOP_K * tm), lambda i: (i, 0, 0), memory_space=pltpu.SMEM),
                  pl.BlockSpec((tm, w), lambda i: (i, 0)),
                  pl.BlockSpec(memory_space=pl.ANY)],
        out_specs=pl.BlockSpec(memory_space=pl.ANY),
        scratch_shapes=[pltpu.SemaphoreType.DMA],
        input_output_aliases={2: 0},
        compiler_params=_cparams(("arbitrary",), VMEM_LIMIT),
        name="dispatch",
    )(dest_tiles, h2p, xs0)


GU_GROUP = 256


def _ffn_kernel(be_ref, nu_ref, x_ref, wgu_ref, bgu_ref, wd_ref, bd_ref, perm_ref, y_ref, wgu_s, wd_s):
    i = pl.program_id(0)
    live = i < nu_ref[0]
    new_expert = jnp.logical_or(i == 0, be_ref[i] != be_ref[jnp.maximum(i - 1, 0)])

    @pl.when(jnp.logical_and(live, new_expert))
    def _():
        for g in range(2 * D_EXPERT // GU_GROUP):
            sl = slice(g * GU_GROUP, (g + 1) * GU_GROUP)
            wgu_s[:, sl] = _mm(wgu_ref[0, :, sl].astype(BF16), perm_ref[...]).astype(BF16)
        wd_s[...] = wd_ref[0].astype(BF16)

    @pl.when(live)
    def _():
        u = x_ref[...]
        lo = pltpu.bitcast(u << 16, F32).astype(BF16)
        hi = pltpu.bitcast(u & jnp.uint32(0xFFFF0000), F32).astype(BF16)
        x = jnp.concatenate([lo, hi], axis=1)
        gu = _mm(x, wgu_s[...]) + bgu_ref[0]
        half = GU_GROUP // 2
        n_grp = 2 * D_EXPERT // GU_GROUP
        glu = jnp.concatenate([gu[:, g * GU_GROUP:g * GU_GROUP + half] for g in range(n_grp)], axis=1)
        lin = jnp.concatenate([gu[:, g * GU_GROUP + half:(g + 1) * GU_GROUP] for g in range(n_grp)], axis=1)
        glu = jnp.minimum(glu, SWIGLU_LIMIT)
        lin = jnp.clip(lin, -SWIGLU_LIMIT, SWIGLU_LIMIT)
        act = glu * _sigmoid(SWIGLU_ALPHA * glu) * (lin + 1.0)
        y_ref[...] = _mm(act.astype(BF16), wd_s[...]) + bd_ref[0]

    @pl.when(jnp.logical_not(live))
    def _():
        y_ref[...] = jnp.zeros_like(y_ref)


def _ffn(block_e, n_used, xs, w_gu, b_gu, w_d, b_d):
    n_rows, w = xs.shape
    bm = FFN_ROWS
    nblk = n_rows // bm
    d = w * 2
    half = GU_GROUP // 2
    perm = np.zeros((GU_GROUP, GU_GROUP), np.float32)
    perm[2 * np.arange(half), np.arange(half)] = 1.0
    perm[2 * np.arange(half) + 1, half + np.arange(half)] = 1.0

    def xmap(i, be, nu):
        return (jnp.minimum(i, nu[0] - 1), 0)

    def wmap(i, be, nu):
        return (be[i], 0, 0)

    grid_spec = pltpu.PrefetchScalarGridSpec(
        num_scalar_prefetch=2,
        grid=(nblk,),
        in_specs=[pl.BlockSpec((bm, w), xmap),
                  pl.BlockSpec((1, d, 2 * D_EXPERT), wmap),
                  pl.BlockSpec((1, 1, 2 * D_EXPERT), wmap),
                  pl.BlockSpec((1, D_EXPERT, d), wmap),
                  pl.BlockSpec((1, 1, d), wmap),
                  pl.BlockSpec((GU_GROUP, GU_GROUP), lambda i, be, nu: (0, 0))],
        out_specs=pl.BlockSpec((bm, d), lambda i, be, nu: (i, 0)),
        scratch_shapes=[pltpu.VMEM((d, 2 * D_EXPERT), BF16), pltpu.VMEM((D_EXPERT, d), BF16)],
    )
    return pl.pallas_call(
        _ffn_kernel,
        out_shape=jax.ShapeDtypeStruct((n_rows, d), F32),
        grid_spec=grid_spec,
        compiler_params=_cparams(("arbitrary",), VMEM_LIMIT),
        name="ffn",
    )(block_e, n_used, xs, w_gu, b_gu, w_d, b_d, jnp.asarray(perm, dtype=BF16))


def _combine_kernel(dest_ref, dest_next_ref, y_ref, p_ref, x1_ref, mod_ref, g_ref, o_ref, buf_ref, sem, *, tm, nt):
    b = pl.program_id(0)
    step = b * nt + pl.program_id(1)
    n_steps = pl.num_programs(0) * nt
    slot = step % 2

    def gather(d_ref, s):
        def body(r, carry):
            for k in range(TOP_K):
                pltpu.make_async_copy(y_ref.at[pl.ds(d_ref[0, 0, k * tm + r], 1)],
                                      buf_ref.at[s, k, pl.ds(r, 1)], sem.at[s]).start()
            return carry
        lax.fori_loop(0, tm, body, 0)

    @pl.when(step == 0)
    def _():
        gather(dest_ref, 0)

    @pl.when(step + 1 < n_steps)
    def _():
        gather(dest_next_ref, 1 - slot)

    for k in range(TOP_K):
        pltpu.make_async_copy(y_ref.at[pl.ds(0, tm)], buf_ref.at[slot, k], sem.at[slot]).wait()
    p = p_ref[...]
    acc = p[:, 0:1] * buf_ref[slot, 0]
    for k in range(1, TOP_K):
        acc = acc + p[:, k:k + 1] * buf_ref[slot, k]
    m = mod_ref[pl.ds(b, 1), :]
    xo = x1_ref[0] + m[:, 5 * D_MODEL:6 * D_MODEL] * acc
    ms = jnp.mean(xo * xo, axis=-1, keepdims=True)
    o_ref[0] = xo * lax.rsqrt(ms + EPS) * g_ref[...]


def _combine(dest_tiles, y, probs_t, x1, mod, g_fin, tm):
    bsz, n, d = x1.shape
    nt = n // tm
    kern = functools.partial(_combine_kernel, tm=tm, nt=nt)
    last = bsz * nt - 1
    return pl.pallas_call(
        kern,
        out_shape=jax.ShapeDtypeStruct((bsz, n, d), F32),
        grid=(bsz, nt),
        in_specs=[pl.BlockSpec((1, 1, TOP_K * tm), lambda b, t: (b * nt + t, 0, 0), memory_space=pltpu.SMEM),
                  pl.BlockSpec((1, 1, TOP_K * tm), lambda b, t: (jnp.minimum(b * nt + t + 1, last), 0, 0),
                               memory_space=pltpu.SMEM),
                  pl.BlockSpec(memory_space=pl.ANY),
                  pl.BlockSpec((tm, TOP_K), lambda b, t: (b * nt + t, 0)),
                  pl.BlockSpec((1, tm, d), lambda b, t: (b, t, 0)),
                  pl.BlockSpec(mod.shape, lambda b, t: (0, 0)),
                  pl.BlockSpec((1, d), lambda b, t: (0, 0))],
        out_specs=pl.BlockSpec((1, tm, d), lambda b, t: (b, t, 0)),
        scratch_shapes=[pltpu.VMEM((2, TOP_K, tm, d), F32), pltpu.SemaphoreType.DMA((2,))],
        compiler_params=_cparams(("arbitrary", "arbitrary"), VMEM_LIMIT),
        name="combine",
    )(dest_tiles, dest_tiles, y, probs_t, x1, mod, g_fin)


def _prep_w_in(w_in):
    q_a, kv, hq, ff, fb, hi, go, mm, mh = jnp.split(
        w_in, np.cumsum((MLA_Q_RANK, MLA_KV_RANK + MLA_ROPE) + (D_MODEL,) * 6).tolist(), axis=1)
    c_kv, kr = kv[:, :MLA_KV_RANK], kv[:, MLA_KV_RANK:]
    half = MLA_ROPE // 2
    kr_sw = jnp.concatenate([-kr[:, half:], kr[:, :half]], axis=1)
    w16 = jnp.concatenate([hq, hi, go, mm, mh, q_a, c_kv, kr, kr_sw], axis=1)
    w16 = jnp.pad(w16, ((0, 0), (0, N_PROJ16 - w16.shape[1])))
    return w16.astype(BF16), jnp.concatenate([ff, fb], axis=1).astype(BF16)


def _prep_w_q(w_q_b):
    w = w_q_b.reshape(MLA_Q_RANK, HEADS, MLA_QK)
    w = jnp.pad(w, ((0, 0), (0, 0), (0, MLA_QK_PAD - MLA_QK)))
    return w.reshape(MLA_Q_RANK, HEADS * MLA_QK_PAD).astype(BF16)


def _prep_w_kv(w_kv_b):
    w = w_kv_b.reshape(MLA_KV_RANK, HEADS, 2 * HEAD_DIM)
    k = w[:, :, :HEAD_DIM].reshape(MLA_KV_RANK, HEADS * HEAD_DIM)
    v = w[:, :, HEAD_DIM:].reshape(MLA_KV_RANK, HEADS * HEAD_DIM)
    return jnp.concatenate([k, v], axis=1).astype(BF16)


def _rope_tables(n_lat, n_ctx):
    f32 = np.float32
    rows = n_lat // GRID_W
    row = np.repeat(np.arange(rows), GRID_W).astype(f32)
    col = np.tile(np.arange(GRID_W), rows).astype(f32)
    n_freq = MLA_ROPE // 4
    inv = (f32(ROPE_BASE) ** (-np.arange(n_freq, dtype=f32) / f32(n_freq))).astype(f32)
    ang = np.concatenate([row[:, None] * inv, col[:, None] * inv], axis=-1).astype(f32)
    cos, sin = np.cos(ang).astype(f32), np.sin(ang).astype(f32)
    z64 = np.zeros((n_lat, 64), f32)
    cos_q = np.concatenate([cos, cos, z64], axis=1)
    sin_q = np.concatenate([-sin, sin, z64], axis=1)
    cos_k = np.concatenate([cos, cos, z64], axis=1)
    sin_k = np.concatenate([sin, sin, z64], axis=1)
    ctx_cos = np.concatenate([np.ones((n_ctx, 64), f32), np.zeros((n_ctx, 64), f32)], axis=1)
    cos_k = np.concatenate([cos_k, ctx_cos], axis=0)
    sin_k = np.concatenate([sin_k, np.zeros((n_ctx, 128), f32)], axis=0)
    return jnp.asarray(cos_q), jnp.asarray(sin_q), jnp.asarray(cos_k), jnp.asarray(sin_k)


def kernel(x, c, ctx, c_ctx, w_mod, b_mod, norm_mix_g, w_in, mla_q_norm_g, w_q_b, mla_kv_norm_g, w_kv_b,
           hg_lb_logits, hg_norm_g, w_out, norm_ffn_g, w_router, b_router, w_gate_up, b_gate_up, w_down,
           b_down, final_norm_g):
    bsz, n_lat, d = x.shape
    n_ctx = ctx.shape[1]
    assert d == D_MODEL and w_mod.shape[0] == 1
    assert n_lat % HG_CHUNK == 0 and n_ctx % HG_CHUNK == 0 and n_lat % GRID_W == 0
    t_tok = bsz * n_lat

    mod_rows = -(-(bsz + 1) // 8) * 8
    cc = jnp.concatenate([c, c_ctx[None, :], jnp.zeros((mod_rows - bsz - 1, d), F32)], axis=0)
    mod = _modulation(cc, w_mod[0], b_mod[0][None, :])

    xa = jnp.concatenate([x, ctx], axis=1)
    w16, w32 = _prep_w_in(w_in[0])
    p, pf = _inproj(xa, mod, norm_mix_g[0][None, :], w16, w32, n_lat)

    cos_q, sin_q, cos_k, sin_k = _rope_tables(n_lat, n_ctx)
    q = _mla_q(p, mla_q_norm_g[0][None, :], _prep_w_q(w_q_b[0]), cos_q, sin_q, n_lat)
    k, v = _mla_kv(p, mla_kv_norm_g[0][None, :], _prep_w_kv(w_kv_b[0]), cos_k, sin_k)
    y_mla = _attention(q, k, v)

    lb = jax.nn.softmax(hg_lb_logits.astype(F32), axis=1)[:, 0, :]
    o_f, o_b = _hgrn_scans(p, pf, lb.reshape(2, HEADS, 1, HEAD_DIM), n_lat)

    x1, h2p, logits_t = _merge(y_mla, o_f, o_b, p, x, mod, hg_norm_g[0][None, :], w_out[0].astype(BF16),
                               norm_ffn_g[0][None, :], w_router[0].T, b_router[0][:, None])

    idx_t, prob_t, rank_t, cnt = _router(logits_t)

    counts = cnt[:, 0].astype(jnp.int32)
    padded = (counts + FFN_ROWS - 1) // FFN_ROWS * FFN_ROWS
    pad_end = jnp.cumsum(padded)
    pad_start = pad_end - padded
    dest = _dest_rows(pad_start.astype(jnp.int32), idx_t, rank_t)
    n_rows = -(-(t_tok * TOP_K + N_EXPERTS * (FFN_ROWS - 1)) // FFN_ROWS) * FFN_ROWS
    nblk = n_rows // FFN_ROWS
    block_start = jnp.arange(nblk, dtype=jnp.int32) * FFN_ROWS
    block_e = jnp.minimum(jnp.sum(pad_end[None, :] <= block_start[:, None], axis=1), N_EXPERTS - 1).astype(jnp.int32)
    n_used = (pad_end[-1:] // FFN_ROWS).astype(jnp.int32)

    tm_c = _tile(n_lat, 256)
    dest_tiles = dest.reshape(TOP_K, t_tok // tm_c, tm_c).transpose(1, 0, 2).reshape(t_tok // tm_c, 1, TOP_K * tm_c)

    xs = _dispatch(dest_tiles, h2p, n_rows, tm_c)

    n_grp = 2 * D_EXPERT // GU_GROUP
    b_gu = b_gate_up[0].reshape(N_EXPERTS, n_grp, GU_GROUP // 2, 2).transpose(0, 1, 3, 2).reshape(N_EXPERTS, 1, 2 * D_EXPERT)
    y = _ffn(block_e, n_used, xs, w_gate_up[0], b_gu, w_down[0], b_down[0][:, None, :])

    return _combine(dest_tiles, y, prob_t.T, x1, mod, final_norm_g[None, :], tm_c)
```

```python
import functools

import jax
import jax.numpy as jnp
import numpy as np
from jax import lax
from jax.experimental import pallas as pl
from jax.experimental.pallas import tpu as pltpu

F32 = jnp.float32
BF16 = jnp.bfloat16

D_MODEL = 1024
EPS = 1e-6
LOG2_E = 1.4426950408889634
N_MOD = 6
GRID_W = 64
ROPE_BASE = 10000.0

HEADS = 8
HEAD_DIM = 128
MLA_ROPE = 64
MLA_QK = HEAD_DIM + MLA_ROPE
MLA_QK_PAD = 256
MLA_Q_RANK = 768
MLA_KV_RANK = 256

N_EXPERTS = 32
TOP_K = 4
D_EXPERT = 1024
SWIGLU_LIMIT = 7.0
SWIGLU_ALPHA = 1.702

HG_CHUNK = 128
FFN_ROWS = 512

COL_HQ, COL_I, COL_GO, COL_MM, COL_MH = (i * D_MODEL for i in range(5))
COL_QA = 5 * D_MODEL
COL_CKV = COL_QA + MLA_Q_RANK
COL_KR = COL_CKV + MLA_KV_RANK
PROJ16_TN = 1280
N_PROJ16 = -(-(COL_KR + 2 * MLA_ROPE) // PROJ16_TN) * PROJ16_TN
COL_FF, COL_FB = 0, D_MODEL
N_PROJ32 = 2 * D_MODEL
PROJ32_TN = 1024

VMEM_LIMIT = 56 * 1024 * 1024


def _cparams(sem, vmem=None):
    return pltpu.CompilerParams(dimension_semantics=sem, vmem_limit_bytes=vmem)


def _tile(n, pref, mult=8):
    best = None
    for t in range(mult, min(n, pref) + 1, mult):
        if n % t == 0:
            best = t
    assert best is not None, (n, pref, mult)
    return best


def _nt(a, b):
    return lax.dot_general(a, b, (((1,), (1,)), ((), ())), preferred_element_type=F32)


def _mm(a, b):
    return jnp.dot(a, b, preferred_element_type=F32)


def _split2(a):
    hi = a.astype(BF16)
    lo = (a - hi.astype(F32)).astype(BF16)
    return hi, lo


def _sigmoid(x):
    return 1.0 / (1.0 + jnp.exp(-x))


def _mod_kernel(c_ref, w_ref, b_ref, o_ref):
    c = c_ref[...]
    s = c * _sigmoid(c)
    s_hi, s_lo = _split2(s)
    w_hi, w_lo = _split2(w_ref[...])
    o_ref[...] = _mm(s_hi, w_hi) + _mm(s_lo, w_hi) + _mm(s_hi, w_lo) + b_ref[...]


def _modulation(cc, w_mod, b_mod):
    r, d = cc.shape
    n = w_mod.shape[1]
    tn = _tile(n, 1536, 128)
    return pl.pallas_call(
        _mod_kernel,
        out_shape=jax.ShapeDtypeStruct((r, n), F32),
        grid=(n // tn,),
        in_specs=[pl.BlockSpec((r, d), lambda j: (0, 0)),
                  pl.BlockSpec((d, tn), lambda j: (0, j)),
                  pl.BlockSpec((1, tn), lambda j: (0, j))],
        out_specs=pl.BlockSpec((r, tn), lambda j: (0, j)),
        compiler_params=_cparams(("parallel",), VMEM_LIMIT),
        name="mod",
    )(cc, w_mod, b_mod)


def _inproj_kernel(x_ref, mod_ref, g_ref, w16_ref, w32_ref, o16_ref, o32_ref, hn_ref, *, n_lat, tm, ctx_row, n16):
    b = pl.program_id(0)
    t = pl.program_id(1)
    j = pl.program_id(2)

    @pl.when(j == 0)
    def _():
        x = x_ref[0]
        ms = jnp.mean(x * x, axis=-1, keepdims=True)
        y = x * lax.rsqrt(ms + EPS) * g_ref[...]
        row = t * tm + lax.broadcasted_iota(jnp.int32, (tm, 1), 0)
        is_ctx = row >= n_lat
        m_lat = mod_ref[pl.ds(b, 1), :]
        m_ctx = mod_ref[pl.ds(ctx_row, 1), :]
        shift = jnp.where(is_ctx, m_ctx[:, 0:D_MODEL], m_lat[:, 0:D_MODEL])
        scale = jnp.where(is_ctx, m_ctx[:, D_MODEL:2 * D_MODEL], m_lat[:, D_MODEL:2 * D_MODEL])
        hn_ref[...] = (y * (1.0 + scale) + shift).astype(BF16)

    @pl.when(j < n16)
    def _():
        o16_ref[0] = _mm(hn_ref[...], w16_ref[...]).astype(BF16)

    @pl.when(j >= n16)
    def _():
        o32_ref[0] = _mm(hn_ref[...], w32_ref[...])


def _inproj(xa, mod, g, w16, w32, n_lat):
    bsz, rows, d = xa.shape
    tm = _tile(rows, 1152)
    n16 = N_PROJ16 // PROJ16_TN
    n32 = N_PROJ32 // PROJ32_TN

    def j16(j):
        return jnp.minimum(j, n16 - 1)

    def j32(j):
        return jnp.maximum(j - n16, 0)

    kern = functools.partial(_inproj_kernel, n_lat=n_lat, tm=tm, ctx_row=bsz, n16=n16)
    return pl.pallas_call(
        kern,
        out_shape=(jax.ShapeDtypeStruct((bsz, rows, N_PROJ16), BF16),
                   jax.ShapeDtypeStruct((bsz, rows, N_PROJ32), F32)),
        grid=(bsz, rows // tm, n16 + n32),
        in_specs=[pl.BlockSpec((1, tm, d), lambda b, t, j: (b, t, 0)),
                  pl.BlockSpec(mod.shape, lambda b, t, j: (0, 0)),
                  pl.BlockSpec((1, d), lambda b, t, j: (0, 0)),
                  pl.BlockSpec((d, PROJ16_TN), lambda b, t, j: (0, j16(j))),
                  pl.BlockSpec((d, PROJ32_TN), lambda b, t, j: (0, j32(j)))],
        out_specs=(pl.BlockSpec((1, tm, PROJ16_TN), lambda b, t, j: (b, t, j16(j))),
                   pl.BlockSpec((1, tm, PROJ32_TN), lambda b, t, j: (b, t, j32(j)))),
        scratch_shapes=[pltpu.VMEM((tm, d), BF16)],
        compiler_params=_cparams(("parallel", "parallel", "arbitrary"), VMEM_LIMIT),
        name="inproj",
    )(xa, mod, g, w16, w32)


def _mlaq_kernel(a0_ref, a1_ref, a2_ref, g_ref, w_ref, cos_ref, sin_ref, o_ref, *, tm):
    parts = [a0_ref[0].astype(F32), a1_ref[0].astype(F32), a2_ref[0].astype(F32)]
    ss = sum(jnp.sum(p * p, axis=-1, keepdims=True) for p in parts)
    r = lax.rsqrt(ss * (1.0 / MLA_Q_RANK) + EPS)
    acc = None
    for j, p in enumerate(parts):
        hj = (p * r * g_ref[:, j * 256:(j + 1) * 256]).astype(BF16)
        d = _mm(hj, w_ref[j * 256:(j + 1) * 256, :])
        acc = d if acc is None else acc + d
    lane = lax.broadcasted_iota(jnp.int32, (tm, HEAD_DIM), 1)
    scale = MLA_QK ** -0.5 * LOG2_E
    cos = cos_ref[...]
    sin = sin_ref[...]
    for h in range(HEADS):
        nope = acc[:, h * MLA_QK_PAD:h * MLA_QK_PAD + HEAD_DIM]
        rp = acc[:, h * MLA_QK_PAD + HEAD_DIM:(h + 1) * MLA_QK_PAD]
        swapped = jnp.where(lane < 32, pltpu.roll(rp, 96, 1), pltpu.roll(rp, 32, 1))
        rot = rp * cos + swapped * sin
        o_ref[0, h, :, 0:HEAD_DIM] = (nope * scale).astype(BF16)
        o_ref[0, h, :, HEAD_DIM:MLA_QK_PAD] = (rot * scale).astype(BF16)


def _mla_q(p, g, w, cos_q, sin_q, n_lat):
    bsz = p.shape[0]
    tm = _tile(n_lat, 512)
    cb = COL_QA // 256
    kern = functools.partial(_mlaq_kernel, tm=tm)
    return pl.pallas_call(
        kern,
        out_shape=jax.ShapeDtypeStruct((bsz, HEADS, n_lat, MLA_QK_PAD), BF16),
        grid=(bsz, n_lat // tm),
        in_specs=[pl.BlockSpec((1, tm, 256), lambda b, t: (b, t, cb)),
                  pl.BlockSpec((1, tm, 256), lambda b, t: (b, t, cb + 1)),
                  pl.BlockSpec((1, tm, 256), lambda b, t: (b, t, cb + 2)),
                  pl.BlockSpec((1, MLA_Q_RANK), lambda b, t: (0, 0)),
                  pl.BlockSpec(w.shape, lambda b, t: (0, 0)),
                  pl.BlockSpec((tm, HEAD_DIM), lambda b, t: (t, 0)),
                  pl.BlockSpec((tm, HEAD_DIM), lambda b, t: (t, 0))],
        out_specs=pl.BlockSpec((1, HEADS, tm, MLA_QK_PAD), lambda b, t: (b, 0, t, 0)),
        compiler_params=_cparams(("parallel", "parallel"), VMEM_LIMIT),
        name="mla_q",
    )(p, p, p, g, w, cos_q, sin_q)


def _mlakv_kernel(c_ref, kr_ref, g_ref, w_ref, cos_ref, sin_ref, k_ref, v_ref):
    c = c_ref[0].astype(F32)
    ms = jnp.mean(c * c, axis=-1, keepdims=True)
    hn = (c * lax.rsqrt(ms + EPS) * g_ref[...]).astype(BF16)
    kv = _mm(hn, w_ref[...])
    grp = kr_ref[0].astype(F32)
    rot = (grp * cos_ref[...] + pltpu.roll(grp, 64, 1) * sin_ref[...]).astype(BF16)
    for h in range(HEADS):
        k_ref[0, h, :, 0:HEAD_DIM] = kv[:, h * HEAD_DIM:(h + 1) * HEAD_DIM].astype(BF16)
        k_ref[0, h, :, HEAD_DIM:MLA_QK_PAD] = rot
        v_ref[0, h] = kv[:, D_MODEL + h * HEAD_DIM:D_MODEL + (h + 1) * HEAD_DIM].astype(BF16)


def _mla_kv(p, g, w, cos_k, sin_k):
    bsz, rows, _ = p.shape
    tm = _tile(rows, 768)
    return pl.pallas_call(
        _mlakv_kernel,
        out_shape=(jax.ShapeDtypeStruct((bsz, HEADS, rows, MLA_QK_PAD), BF16),
                   jax.ShapeDtypeStruct((bsz, HEADS, rows, HEAD_DIM), BF16)),
        grid=(bsz, rows // tm),
        in_specs=[pl.BlockSpec((1, tm, MLA_KV_RANK), lambda b, t: (b, t, COL_CKV // MLA_KV_RANK)),
                  pl.BlockSpec((1, tm, 128), lambda b, t: (b, t, COL_KR // 128)),
                  pl.BlockSpec((1, MLA_KV_RANK), lambda b, t: (0, 0)),
                  pl.BlockSpec(w.shape, lambda b, t: (0, 0)),
                  pl.BlockSpec((tm, 128), lambda b, t: (t, 0)),
                  pl.BlockSpec((tm, 128), lambda b, t: (t, 0))],
        out_specs=(pl.BlockSpec((1, HEADS, tm, MLA_QK_PAD), lambda b, t: (b, 0, t, 0)),
                   pl.BlockSpec((1, HEADS, tm, HEAD_DIM), lambda b, t: (b, 0, t, 0))),
        compiler_params=_cparams(("parallel", "parallel"), VMEM_LIMIT),
        name="mla_kv",
    )(p, p, g, w, cos_k, sin_k)


def _attn_kernel(q_ref, k_ref, v_ref, o_ref, *, tq, sub):
    k = k_ref[0, 0]
    v = v_ref[0, 0]
    v1 = jnp.concatenate([v, jnp.ones_like(v)], axis=1)
    for r in range(tq // sub):
        rows = slice(r * sub, (r + 1) * sub)
        s = _nt(q_ref[0, 0, rows, :], k)
        m = jnp.max(s, axis=-1, keepdims=True)
        p = jnp.exp2(s - m).astype(BF16)
        ol = _mm(p, v1)
        o_ref[0, rows, :] = (ol[:, 0:HEAD_DIM] * (1.0 / ol[:, HEAD_DIM:HEAD_DIM + 1])).astype(BF16)


def _attention(q, k, v):
    bsz, _, n, _ = q.shape
    m = k.shape[2]
    tq = _tile(n, 2048)
    sub = _tile(tq, 512)
    kern = functools.partial(_attn_kernel, tq=tq, sub=sub)
    return pl.pallas_call(
        kern,
        out_shape=jax.ShapeDtypeStruct((bsz, n, HEADS * HEAD_DIM), BF16),
        grid=(bsz, HEADS, n // tq),
        in_specs=[pl.BlockSpec((1, 1, tq, MLA_QK_PAD), lambda b, h, t: (b, h, t, 0)),
                  pl.BlockSpec((1, 1, m, MLA_QK_PAD), lambda b, h, t: (b, h, 0, 0)),
                  pl.BlockSpec((1, 1, m, HEAD_DIM), lambda b, h, t: (b, h, 0, 0))],
        out_specs=pl.BlockSpec((1, tq, HEAD_DIM), lambda b, h, t: (b, t, h)),
        compiler_params=_cparams(("parallel", "parallel", "arbitrary"), VMEM_LIMIT),
        name="attention",
    )(q, k, v)


def _level_ref(cum, blk, reverse):
    c = cum.shape[0]
    half = blk // 2
    r = half if reverse else half - 1
    if blk >= 8:
        x = cum.reshape(c // blk, blk, HEAD_DIM)
        e = jnp.broadcast_to(x[:, r:r + 1, :], x.shape)
        return e.reshape(c, HEAD_DIM)
    x = cum.reshape(c // 8, 8, HEAD_DIM)
    sub = lax.broadcasted_iota(jnp.int32, x.shape, 1)
    e = None
    for jb in range(8 // blk):
        cand = jnp.broadcast_to(x[:, jb * blk + r:jb * blk + r + 1, :], x.shape)
        e = cand if e is None else jnp.where(sub >= jb * blk, cand, e)
    return e.reshape(c, HEAD_DIM)


def _hgrn_chunk(q, z, v, lb, tri, lvl, st, *, chunk, reverse, want_out):
    f = lb + (1.0 - lb) * _sigmoid(z)
    kb = (1.0 - f).astype(BF16)
    g = jnp.log(f) * LOG2_E
    vb = v.astype(BF16)

    g_hi, g_lo = _split2(g)
    cum = _mm(tri, jnp.concatenate([g_hi, g_lo], axis=0))
    last = 0 if reverse else chunk - 1
    tot = cum[last:last + 1, :]
    kt = kb * jnp.exp2(tot - cum).astype(BF16)
    st_new = st * jnp.exp2(tot) + _mm(v.astype(F32).T.astype(BF16), kt)
    if not want_out:
        return None, st_new

    q = q.astype(F32)
    qb = (q * _sigmoid(q) * (HEAD_DIM ** -0.5)).astype(BF16)

    att = jnp.where(lvl == 0, _nt(qb, kb).astype(BF16), jnp.zeros((), BF16))
    n_lvl = chunk.bit_length() - 1
    for lv in range(1, n_lvl + 1):
        zrel = cum - _level_ref(cum, 1 << lv, reverse)
        neg_abs = pltpu.bitcast(pltpu.bitcast(zrel, jnp.uint32) | jnp.uint32(0x80000000), F32)
        e = jnp.exp2(neg_abs).astype(BF16)
        att = jnp.where(lvl == lv, _nt(qb * e, kb * e).astype(BF16), att)

    o = _nt(qb * jnp.exp2(cum).astype(BF16), st.astype(BF16)) + _mm(att, vb)
    return o, st_new


def _hgrn_kernel(qf_ref, zf_ref, if_ref, qb_ref, zb_ref, ib_ref, lb_ref, tri_ref, lvl_ref, of_ref, ob_ref, st_ref,
                 *, chunk, group, n_ctx_chunks):
    step = pl.program_id(2)

    @pl.when(step == 0)
    def _():
        st_ref[...] = jnp.zeros_like(st_ref)

    ins = ((qf_ref, zf_ref, if_ref), (qb_ref, zb_ref, ib_ref))
    outs = (of_ref, ob_ref)
    chains = [(d, j) for j in range(group) for d in range(2)]

    def advance(want_out):
        states = [st_ref[d, j] for d, j in chains]
        results = []
        for (d, j), st in zip(chains, states):
            q_ref, z_ref, i_ref = ins[d]
            sl = slice(j * HEAD_DIM, (j + 1) * HEAD_DIM)
            results.append(_hgrn_chunk(q_ref[0, :, sl], z_ref[0, :, sl], i_ref[0, :, sl], lb_ref[d, j], tri_ref[d],
                                       lvl_ref[d], st, chunk=chunk, reverse=bool(d), want_out=want_out))
        for (d, j), (o, st) in zip(chains, results):
            if want_out:
                outs[d][0, j] = o.astype(BF16)
            st_ref[d, j] = st

    @pl.when(step < n_ctx_chunks)
    def _():
        advance(False)

    @pl.when(step >= n_ctx_chunks)
    def _():
        advance(True)


def _hgrn_consts(chunk, reverse):
    t = np.arange(chunk)[:, None]
    s = np.arange(chunk)[None, :]
    x = t ^ s
    bitlen = np.zeros_like(x)
    for b in range(chunk.bit_length()):
        bitlen = np.where(x >> b > 0, b + 1, bitlen)
    valid = (s > t) if reverse else (t > s)
    lvl = np.where(t == s, 0, np.where(valid, bitlen, -1)).astype(np.float32)
    tri = ((s >= t) if reverse else (t >= s)).astype(np.float32)
    return np.concatenate([tri, tri], axis=1), lvl


HG_GROUP = 8


def _hgrn_scans(p, pf, lb, n_lat):
    bsz, rows, _ = p.shape
    c = HG_CHUNK
    grp = HG_GROUP
    n_chunks = rows // c
    n_lat_c = n_lat // c
    n_ctx_c = n_chunks - n_lat_c
    consts = [_hgrn_consts(c, False), _hgrn_consts(c, True)]
    tri = jnp.asarray(np.stack([consts[0][0], consts[1][0]]), dtype=BF16)
    lvl = jnp.asarray(np.stack([consts[0][1], consts[1][1]]), dtype=BF16)

    def cidx_f(i):
        return jnp.where(i < n_ctx_c, n_lat_c + i, i - n_ctx_c)

    def cidx_b(i):
        return n_chunks - 1 - i

    def col_spec(col, cidx):
        return pl.BlockSpec((1, c, grp * HEAD_DIM), lambda b, h, i: (b, cidx(i), col // (grp * HEAD_DIM) + h))

    def out_spec(cidx):
        return pl.BlockSpec((1, grp, c, HEAD_DIM), lambda b, h, i: (b, h, cidx(jnp.maximum(i, n_ctx_c)), 0))

    o_shape = jax.ShapeDtypeStruct((bsz, HEADS, n_lat, HEAD_DIM), BF16)
    kern = functools.partial(_hgrn_kernel, chunk=c, group=grp, n_ctx_chunks=n_ctx_c)
    return pl.pallas_call(
        kern,
        out_shape=(o_shape, o_shape),
        grid=(bsz, HEADS // grp, n_chunks),
        in_specs=[col_spec(COL_HQ, cidx_f), col_spec(COL_FF, cidx_f), col_spec(COL_I, cidx_f),
                  col_spec(COL_HQ, cidx_b), col_spec(COL_FB, cidx_b), col_spec(COL_I, cidx_b),
                  pl.BlockSpec((2, grp, 1, HEAD_DIM), lambda b, h, i: (0, h, 0, 0)),
                  pl.BlockSpec((2, c, 2 * c), lambda b, h, i: (0, 0, 0)),
                  pl.BlockSpec((2, c, c), lambda b, h, i: (0, 0, 0))],
        out_specs=(out_spec(cidx_f), out_spec(cidx_b)),
        scratch_shapes=[pltpu.VMEM((2, grp, HEAD_DIM, HEAD_DIM), F32)],
        compiler_params=_cparams(("parallel", "parallel", "arbitrary"), VMEM_LIMIT),
        name="hgrn",
    )(p, pf, p, p, pf, p, lb, tri, lvl)


def _merge_kernel(ym_ref, of_ref, ob_ref, go_ref, gm_ref, gh_ref, x_ref, mod_ref, hgg_ref, wout_ref,
                  gffn_ref, wr_ref, br_ref, x1_ref, h2_ref, lg_ref, y_scr):
    b = pl.program_id(0)
    for h in range(HEADS):
        sl = slice(h * HEAD_DIM, (h + 1) * HEAD_DIM)
        o = of_ref[0, h].astype(F32) + ob_ref[0, h].astype(F32)
        ms = jnp.mean(o * o, axis=-1, keepdims=True)
        g = go_ref[0, :, sl].astype(F32)
        yh = o * lax.rsqrt(ms + EPS) * hgg_ref[...] * (g * _sigmoid(g))
        y = (_sigmoid(gm_ref[0, :, sl].astype(F32)) * ym_ref[0, :, sl].astype(F32)
             + _sigmoid(gh_ref[0, :, sl].astype(F32)) * yh)
        y_scr[:, sl] = y.astype(BF16)
    mix = _mm(y_scr[...], wout_ref[...])
    m = mod_ref[pl.ds(b, 1), :]
    x1 = x_ref[0] + m[:, 2 * D_MODEL:3 * D_MODEL] * mix
    x1_ref[0] = x1
    ms = jnp.mean(x1 * x1, axis=-1, keepdims=True)
    h2 = (x1 * lax.rsqrt(ms + EPS) * gffn_ref[...]) * (1.0 + m[:, 4 * D_MODEL:5 * D_MODEL]) + m[:, 3 * D_MODEL:4 * D_MODEL]
    h_hi = h2.astype(BF16)
    u = pltpu.bitcast(h_hi.astype(F32), jnp.uint32)
    half = D_MODEL // 2
    h2_ref[...] = (u[:, 0:half] >> 16) | u[:, half:D_MODEL]
    h_lo = (h2 - h_hi.astype(F32)).astype(BF16)
    w_hi, w_lo = _split2(wr_ref[...])
    lg_ref[...] = _nt(w_hi, h_hi) + _nt(w_lo, h_hi) + _nt(w_hi, h_lo) + br_ref[...]


def _merge(y_mla, o_f, o_b, p, x, mod, hg_g, w_out, g_ffn, w_r_t, b_r):
    bsz, n, d = x.shape
    tm = _tile(n, 256, 128)
    nt = n // tm

    def pcol(col):
        return pl.BlockSpec((1, tm, d), lambda b, t: (b, t, col // d))

    tok = lambda b, t: (b, t, 0)
    const2 = lambda b, t: (0, 0)
    return pl.pallas_call(
        _merge_kernel,
        out_shape=(jax.ShapeDtypeStruct((bsz, n, d), F32),
                   jax.ShapeDtypeStruct((bsz * n, d // 2), jnp.uint32),
                   jax.ShapeDtypeStruct((N_EXPERTS, bsz * n), F32)),
        grid=(bsz, nt),
        in_specs=[pl.BlockSpec((1, tm, d), tok),
                  pl.BlockSpec((1, HEADS, tm, HEAD_DIM), lambda b, t: (b, 0, t, 0)),
                  pl.BlockSpec((1, HEADS, tm, HEAD_DIM), lambda b, t: (b, 0, t, 0)),
                  pcol(COL_GO), pcol(COL_MM), pcol(COL_MH),
                  pl.BlockSpec((1, tm, d), tok),
                  pl.BlockSpec(mod.shape, const2),
                  pl.BlockSpec((1, HEAD_DIM), const2),
                  pl.BlockSpec((d, d), const2),
                  pl.BlockSpec((1, d), const2),
                  pl.BlockSpec((N_EXPERTS, d), const2),
                  pl.BlockSpec((N_EXPERTS, 1), const2)],
        out_specs=(pl.BlockSpec((1, tm, d), tok),
                   pl.BlockSpec((tm, d // 2), lambda b, t: (b * nt + t, 0)),
                   pl.BlockSpec((N_EXPERTS, tm), lambda b, t: (0, b * nt + t))),
        scratch_shapes=[pltpu.VMEM((tm, d), BF16)],
        compiler_params=_cparams(("parallel", "parallel"), VMEM_LIMIT),
        name="merge",
    )(y_mla, o_f, o_b, p, p, p, x, mod, hg_g, w_out, g_ffn, w_r_t, b_r)


def _router_kernel(lg_ref, upper_ref, idx_ref, prob_ref, rank_ref, cnt_ref, run_ref, *, tm):
    @pl.when(pl.program_id(0) == 0)
    def _():
        run_ref[...] = jnp.zeros_like(run_ref)

    l = lg_ref[...]
    eidx = lax.broadcasted_iota(jnp.int32, (N_EXPERTS, tm), 0)
    vals, sel = [], []
    for _ in range(TOP_K):
        m = jnp.max(l, axis=0, keepdims=True)
        first = jnp.min(jnp.where(l == m, eidx, N_EXPERTS), axis=0, keepdims=True)
        vals.append(m)
        sel.append(first)
        l = jnp.where(eidx == first, -jnp.inf, l)
    ex = [jnp.exp(v - vals[0]) for v in vals]
    inv = 1.0 / (ex[0] + ex[1] + ex[2] + ex[3])
    onehot = jnp.zeros((N_EXPERTS, tm), F32)
    for k in range(TOP_K):
        onehot = onehot + jnp.where(eidx == sel[k], 1.0, 0.0)
    before = _mm(onehot.astype(BF16), upper_ref[...]) + run_ref[:, 0:1]
    for k in range(TOP_K):
        idx_ref[k:k + 1, :] = sel[k]
        prob_ref[k:k + 1, :] = ex[k] * inv
        rank_ref[k:k + 1, :] = jnp.sum(jnp.where(eidx == sel[k], before, 0.0), axis=0, keepdims=True).astype(jnp.int32)
    run_ref[...] = run_ref[...] + jnp.sum(onehot, axis=1, keepdims=True)
    cnt_ref[...] = run_ref[...]


def _router(logits_t):
    _, t = logits_t.shape
    tm = _tile(t, 1024, 128)
    upper = jnp.asarray(np.triu(np.ones((tm, tm), np.float32), 1), dtype=BF16)
    kern = functools.partial(_router_kernel, tm=tm)
    tok = pl.BlockSpec((TOP_K, tm), lambda i: (0, i))
    return pl.pallas_call(
        kern,
        out_shape=(jax.ShapeDtypeStruct((TOP_K, t), jnp.int32),
                   jax.ShapeDtypeStruct((TOP_K, t), F32),
                   jax.ShapeDtypeStruct((TOP_K, t), jnp.int32),
                   jax.ShapeDtypeStruct((N_EXPERTS, 128), F32)),
        grid=(t // tm,),
        in_specs=[pl.BlockSpec((N_EXPERTS, tm), lambda i: (0, i)),
                  pl.BlockSpec((tm, tm), lambda i: (0, 0))],
        out_specs=(tok, tok, tok, pl.BlockSpec((N_EXPERTS, 128), lambda i: (0, 0))),
        scratch_shapes=[pltpu.VMEM((N_EXPERTS, 128), F32)],
        compiler_params=_cparams(("arbitrary",), VMEM_LIMIT),
        name="router",
    )(logits_t, upper)


def _dest_kernel(ps_ref, idx_ref, rank_ref, o_ref):
    idx = idx_ref[...]
    dest = rank_ref[...]
    for e in range(N_EXPERTS):
        dest = dest + jnp.where(idx == e, ps_ref[e], 0)
    o_ref[...] = dest


def _dest_rows(pad_start, idx_t, rank_t):
    k, t = idx_t.shape
    tm = _tile(t, 4096, 128)
    tok = pl.BlockSpec((k, tm), lambda i, ps: (0, i))
    return pl.pallas_call(
        _dest_kernel,
        out_shape=jax.ShapeDtypeStruct((k, t), jnp.int32),
        grid_spec=pltpu.PrefetchScalarGridSpec(num_scalar_prefetch=1, grid=(t // tm,), in_specs=[tok, tok], out_specs=tok),
        compiler_params=_cparams(("parallel",), VMEM_LIMIT),
        name="dest_rows",
    )(pad_start, idx_t, rank_t)


def _dispatch_kernel(dest_ref, h2_ref, xs_in_ref, xs_ref, sem, *, tm):
    del xs_in_ref

    def body(r, carry):
        for k in range(TOP_K):
            pltpu.make_async_copy(h2_ref.at[pl.ds(r, 1)], xs_ref.at[pl.ds(dest_ref[0, 0, k * tm + r], 1)], sem).start()
        return carry

    lax.fori_loop(0, tm, body, 0)
    for _ in range(TOP_K):
        pltpu.make_async_copy(h2_ref, xs_ref.at[pl.ds(0, tm)], sem).wait()


def _dispatch(dest_tiles, h2p, n_rows, tm):
    t, w = h2p.shape
    xs0 = jnp.zeros((n_rows, w), h2p.dtype)
    kern = functools.partial(_dispatch_kernel, tm=tm)
    return pl.pallas_call(
        kern,
        out_shape=jax.ShapeDtypeStruct((n_rows, w), h2p.dtype),
        grid=(t // tm,),
        in_specs=[pl.BlockSpec((1, 1, TOP_K * tm), lambda i: (i, 0, 0), memory_space=pltpu.SMEM),
                  pl.BlockSpec((tm, w), lambda i: (i, 0)),
                  pl.BlockSpec(memory_space=pl.ANY)],
        out_specs=pl.BlockSpec(memory_space=pl.ANY),
        scratch_shapes=[pltpu.SemaphoreType.DMA],
        input_output_aliases={2: 0},
        compiler_params=_cparams(("arbitrary",), VMEM_LIMIT),
        name="dispatch",
    )(dest_tiles, h2p, xs0)


GU_GROUP = 256


def _ffn_kernel(be_ref, nu_ref, x_ref, wgu_ref, bgu_ref, wd_ref, bd_ref, perm_ref, y_ref, wgu_s, wd_s):
    i = pl.program_id(0)
    live = i < nu_ref[0]
    new_expert = jnp.logical_or(i == 0, be_ref[i] != be_ref[jnp.maximum(i - 1, 0)])

    @pl.when(jnp.logical_and(live, new_expert))
    def _():
        for g in range(2 * D_EXPERT // GU_GROUP):
            sl = slice(g * GU_GROUP, (g + 1) * GU_GROUP)
            wgu_s[:, sl] = _mm(wgu_ref[0, :, sl].astype(BF16), perm_ref[...]).astype(BF16)
        wd_s[...] = wd_ref[0].astype(BF16)

    @pl.when(live)
    def _():
        u = x_ref[...]
        lo = pltpu.bitcast(u << 16, F32).astype(BF16)
        hi = pltpu.bitcast(u & jnp.uint32(0xFFFF0000), F32).astype(BF16)
        x = jnp.concatenate([lo, hi], axis=1)
        gu = _mm(x, wgu_s[...]) + bgu_ref[0]
        half = GU_GROUP // 2
        n_grp = 2 * D_EXPERT // GU_GROUP
        glu = jnp.concatenate([gu[:, g * GU_GROUP:g * GU_GROUP + half] for g in range(n_grp)], axis=1)
        lin = jnp.concatenate([gu[:, g * GU_GROUP + half:(g + 1) * GU_GROUP] for g in range(n_grp)], axis=1)
        glu = jnp.minimum(glu, SWIGLU_LIMIT)
        lin = jnp.clip(lin, -SWIGLU_LIMIT, SWIGLU_LIMIT)
        act = glu * _sigmoid(SWIGLU_ALPHA * glu) * (lin + 1.0)
        y_ref[...] = _mm(act.astype(BF16), wd_s[...]) + bd_ref[0]

    @pl.when(jnp.logical_not(live))
    def _():
        y_ref[...] = jnp.zeros_like(y_ref)


def _ffn(block_e, n_used, xs, w_gu, b_gu, w_d, b_d):
    n_rows, w = xs.shape
    bm = FFN_ROWS
    nblk = n_rows // bm
    d = w * 2
    half = GU_GROUP // 2
    perm = np.zeros((GU_GROUP, GU_GROUP), np.float32)
    perm[2 * np.arange(half), np.arange(half)] = 1.0
    perm[2 * np.arange(half) + 1, half + np.arange(half)] = 1.0

    def xmap(i, be, nu):
        return (jnp.minimum(i, nu[0] - 1), 0)

    def wmap(i, be, nu):
        return (be[i], 0, 0)

    grid_spec = pltpu.PrefetchScalarGridSpec(
        num_scalar_prefetch=2,
        grid=(nblk,),
        in_specs=[pl.BlockSpec((bm, w), xmap),
                  pl.BlockSpec((1, d, 2 * D_EXPERT), wmap),
                  pl.BlockSpec((1, 1, 2 * D_EXPERT), wmap),
                  pl.BlockSpec((1, D_EXPERT, d), wmap),
                  pl.BlockSpec((1, 1, d), wmap),
                  pl.BlockSpec((GU_GROUP, GU_GROUP), lambda i, be, nu: (0, 0))],
        out_specs=pl.BlockSpec((bm, d), lambda i, be, nu: (i, 0)),
        scratch_shapes=[pltpu.VMEM((d, 2 * D_EXPERT), BF16), pltpu.VMEM((D_EXPERT, d), BF16)],
    )
    return pl.pallas_call(
        _ffn_kernel,
        out_shape=jax.ShapeDtypeStruct((n_rows, d), F32),
        grid_spec=grid_spec,
        compiler_params=_cparams(("arbitrary",), VMEM_LIMIT),
        name="ffn",
    )(block_e, n_used, xs, w_gu, b_gu, w_d, b_d, jnp.asarray(perm, dtype=BF16))


def _combine_kernel(dest_ref, dest_next_ref, y_ref, p_ref, x1_ref, mod_ref, g_ref, o_ref, buf_ref, sem, *, tm, nt):
    b = pl.program_id(0)
    step = b * nt + pl.program_id(1)
    n_steps = pl.num_programs(0) * nt
    slot = step % 2

    def gather(d_ref, s):
        def body(r, carry):
            for k in range(TOP_K):
                pltpu.make_async_copy(y_ref.at[pl.ds(d_ref[0, 0, k * tm + r], 1)],
                                      buf_ref.at[s, k, pl.ds(r, 1)], sem.at[s]).start()
            return carry
        lax.fori_loop(0, tm, body, 0)

    @pl.when(step == 0)
    def _():
        gather(dest_ref, 0)

    @pl.when(step + 1 < n_steps)
    def _():
        gather(dest_next_ref, 1 - slot)

    for k in range(TOP_K):
        pltpu.make_async_copy(y_ref.at[pl.ds(0, tm)], buf_ref.at[slot, k], sem.at[slot]).wait()
    p = p_ref[...]
    acc = p[:, 0:1] * buf_ref[slot, 0]
    for k in range(1, TOP_K):
        acc = acc + p[:, k:k + 1] * buf_ref[slot, k]
    m = mod_ref[pl.ds(b, 1), :]
    xo = x1_ref[0] + m[:, 5 * D_MODEL:6 * D_MODEL] * acc
    ms = jnp.mean(xo * xo, axis=-1, keepdims=True)
    o_ref[0] = xo * lax.rsqrt(ms + EPS) * g_ref[...]


def _combine(dest_tiles, y, probs_t, x1, mod, g_fin, tm):
    bsz, n, d = x1.shape
    nt = n // tm
    kern = functools.partial(_combine_kernel, tm=tm, nt=nt)
    last = bsz * nt - 1
    return pl.pallas_call(
        kern,
        out_shape=jax.ShapeDtypeStruct((bsz, n, d), F32),
        grid=(bsz, nt),
        in_specs=[pl.BlockSpec((1, 1, TOP_K * tm), lambda b, t: (b * nt + t, 0, 0), memory_space=pltpu.SMEM),
                  pl.BlockSpec((1, 1, TOP_K * tm), lambda b, t: (jnp.minimum(b * nt + t + 1, last), 0, 0),
                               memory_space=pltpu.SMEM),
                  pl.BlockSpec(memory_space=pl.ANY),
                  pl.BlockSpec((tm, TOP_K), lambda b, t: (b * nt + t, 0)),
                  pl.BlockSpec((1, tm, d), lambda b, t: (b, t, 0)),
                  pl.BlockSpec(mod.shape, lambda b, t: (0, 0)),
                  pl.BlockSpec((1, d), lambda b, t: (0, 0))],
        out_specs=pl.BlockSpec((1, tm, d), lambda b, t: (b, t, 0)),
        scratch_shapes=[pltpu.VMEM((2, TOP_K, tm, d), F32), pltpu.SemaphoreType.DMA((2,))],
        compiler_params=_cparams(("arbitrary", "arbitrary"), VMEM_LIMIT),
        name="combine",
    )(dest_tiles, dest_tiles, y, probs_t, x1, mod, g_fin)


def _prep_w_in(w_in):
    q_a, kv, hq, ff, fb, hi, go, mm, mh = jnp.split(
        w_in, np.cumsum((MLA_Q_RANK, MLA_KV_RANK + MLA_ROPE) + (D_MODEL,) * 6).tolist(), axis=1)
    c_kv, kr = kv[:, :MLA_KV_RANK], kv[:, MLA_KV_RANK:]
    half = MLA_ROPE // 2
    kr_sw = jnp.concatenate([-kr[:, half:], kr[:, :half]], axis=1)
    w16 = jnp.concatenate([hq, hi, go, mm, mh, q_a, c_kv, kr, kr_sw], axis=1)
    w16 = jnp.pad(w16, ((0, 0), (0, N_PROJ16 - w16.shape[1])))
    return w16.astype(BF16), jnp.concatenate([ff, fb], axis=1).astype(BF16)


def _prep_w_q(w_q_b):
    w = w_q_b.reshape(MLA_Q_RANK, HEADS, MLA_QK)
    w = jnp.pad(w, ((0, 0), (0, 0), (0, MLA_QK_PAD - MLA_QK)))
    return w.reshape(MLA_Q_RANK, HEADS * MLA_QK_PAD).astype(BF16)


def _prep_w_kv(w_kv_b):
    w = w_kv_b.reshape(MLA_KV_RANK, HEADS, 2 * HEAD_DIM)
    k = w[:, :, :HEAD_DIM].reshape(MLA_KV_RANK, HEADS * HEAD_DIM)
    v = w[:, :, HEAD_DIM:].reshape(MLA_KV_RANK, HEADS * HEAD_DIM)
    return jnp.concatenate([k, v], axis=1).astype(BF16)


def _rope_tables(n_lat, n_ctx):
    f32 = np.float32
    rows = n_lat // GRID_W
    row = np.repeat(np.arange(rows), GRID_W).astype(f32)
    col = np.tile(np.arange(GRID_W), rows).astype(f32)
    n_freq = MLA_ROPE // 4
    inv = (f32(ROPE_BASE) ** (-np.arange(n_freq, dtype=f32) / f32(n_freq))).astype(f32)
    ang = np.concatenate([row[:, None] * inv, col[:, None] * inv], axis=-1).astype(f32)
    cos, sin = np.cos(ang).astype(f32), np.sin(ang).astype(f32)
    z64 = np.zeros((n_lat, 64), f32)
    cos_q = np.concatenate([cos, cos, z64], axis=1)
    sin_q = np.concatenate([-sin, sin, z64], axis=1)
    cos_k = np.concatenate([cos, cos, z64], axis=1)
    sin_k = np.concatenate([sin, sin, z64], axis=1)
    ctx_cos = np.concatenate([np.ones((n_ctx, 64), f32), np.zeros((n_ctx, 64), f32)], axis=1)
    cos_k = np.concatenate([cos_k, ctx_cos], axis=0)
    sin_k = np.concatenate([sin_k, np.zeros((n_ctx, 128), f32)], axis=0)
    return jnp.asarray(cos_q), jnp.asarray(sin_q), jnp.asarray(cos_k), jnp.asarray(sin_k)


def kernel(x, c, ctx, c_ctx, w_mod, b_mod, norm_mix_g, w_in, mla_q_norm_g, w_q_b, mla_kv_norm_g, w_kv_b,
           hg_lb_logits, hg_norm_g, w_out, norm_ffn_g, w_router, b_router, w_gate_up, b_gate_up, w_down,
           b_down, final_norm_g):
    bsz, n_lat, d = x.shape
    n_ctx = ctx.shape[1]
    assert d == D_MODEL and w_mod.shape[0] == 1
    assert n_lat % HG_CHUNK == 0 and n_ctx % HG_CHUNK == 0 and n_lat % GRID_W == 0
    t_tok = bsz * n_lat

    mod_rows = -(-(bsz + 1) // 8) * 8
    cc = jnp.concatenate([c, c_ctx[None, :], jnp.zeros((mod_rows - bsz - 1, d), F32)], axis=0)
    mod = _modulation(cc, w_mod[0], b_mod[0][None, :])

    xa = jnp.concatenate([x, ctx], axis=1)
    w16, w32 = _prep_w_in(w_in[0])
    p, pf = _inproj(xa, mod, norm_mix_g[0][None, :], w16, w32, n_lat)

    cos_q, sin_q, cos_k, sin_k = _rope_tables(n_lat, n_ctx)
    q = _mla_q(p, mla_q_norm_g[0][None, :], _prep_w_q(w_q_b[0]), cos_q, sin_q, n_lat)
    k, v = _mla_kv(p, mla_kv_norm_g[0][None, :], _prep_w_kv(w_kv_b[0]), cos_k, sin_k)
    y_mla = _attention(q, k, v)

    lb = jax.nn.softmax(hg_lb_logits.astype(F32), axis=1)[:, 0, :]
    o_f, o_b = _hgrn_scans(p, pf, lb.reshape(2, HEADS, 1, HEAD_DIM), n_lat)

    x1, h2p, logits_t = _merge(y_mla, o_f, o_b, p, x, mod, hg_norm_g[0][None, :], w_out[0].astype(BF16),
                               norm_ffn_g[0][None, :], w_router[0].T, b_router[0][:, None])

    idx_t, prob_t, rank_t, cnt = _router(logits_t)

    counts = cnt[:, 0].astype(jnp.int32)
    padded = (counts + FFN_ROWS - 1) // FFN_ROWS * FFN_ROWS
    pad_end = jnp.cumsum(padded)
    pad_start = pad_end - padded
    dest = _dest_rows(pad_start.astype(jnp.int32), idx_t, rank_t)
    n_rows = -(-(t_tok * TOP_K + N_EXPERTS * (FFN_ROWS - 1)) // FFN_ROWS) * FFN_ROWS
    nblk = n_rows // FFN_ROWS
    block_start = jnp.arange(nblk, dtype=jnp.int32) * FFN_ROWS
    block_e = jnp.minimum(jnp.sum(pad_end[None, :] <= block_start[:, None], axis=1), N_EXPERTS - 1).astype(jnp.int32)
    n_used = (pad_end[-1:] // FFN_ROWS).astype(jnp.int32)

    tm_c = _tile(n_lat, 256)
    dest_tiles = dest.reshape(TOP_K, t_tok // tm_c, tm_c).transpose(1, 0, 2).reshape(t_tok // tm_c, 1, TOP_K * tm_c)

    xs = _dispatch(dest_tiles, h2p, n_rows, tm_c)

    n_grp = 2 * D_EXPERT // GU_GROUP
    b_gu = b_gate_up[0].reshape(N_EXPERTS, n_grp, GU_GROUP // 2, 2).transpose(0, 1, 3, 2).reshape(N_EXPERTS, 1, 2 * D_EXPERT)
    y = _ffn(block_e, n_used, xs, w_gate_up[0], b_gu, w_down[0], b_down[0][:, None, :])

    return _combine(dest_tiles, y, prob_t.T, x1, mod, final_norm_g[None, :], tm_c)
```

```python
import functools

import jax
import jax.numpy as jnp
import numpy as np
from jax import lax
from jax.experimental import pallas as pl
from jax.experimental.pallas import tpu as pltpu

F32 = jnp.float32
BF16 = jnp.bfloat16

D_MODEL = 1024
EPS = 1e-6
LOG2_E = 1.4426950408889634
N_MOD = 6
GRID_W = 64
ROPE_BASE = 10000.0

HEADS = 8
HEAD_DIM = 128
MLA_ROPE = 64
MLA_QK = HEAD_DIM + MLA_ROPE
MLA_QK_PAD = 256
MLA_Q_RANK = 768
MLA_KV_RANK = 256

N_EXPERTS = 32
TOP_K = 4
D_EXPERT = 1024
SWIGLU_LIMIT = 7.0
SWIGLU_ALPHA = 1.702

HG_CHUNK = 128
FFN_ROWS = 512

COL_HQ, COL_I, COL_GO, COL_MM, COL_MH = (i * D_MODEL for i in range(5))
COL_QA = 5 * D_MODEL
COL_CKV = COL_QA + MLA_Q_RANK
COL_KR = COL_CKV + MLA_KV_RANK
PROJ16_TN = 1280
N_PROJ16 = -(-(COL_KR + 2 * MLA_ROPE) // PROJ16_TN) * PROJ16_TN
COL_FF, COL_FB = 0, D_MODEL
N_PROJ32 = 2 * D_MODEL
PROJ32_TN = 1024

VMEM_LIMIT = 56 * 1024 * 1024


def _cparams(sem, vmem=None):
    return pltpu.CompilerParams(dimension_semantics=sem, vmem_limit_bytes=vmem)


def _tile(n, pref, mult=8):
    best = None
    for t in range(mult, min(n, pref) + 1, mult):
        if n % t == 0:
            best = t
    assert best is not None, (n, pref, mult)
    return best


def _nt(a, b):
    return lax.dot_general(a, b, (((1,), (1,)), ((), ())), preferred_element_type=F32)


def _mm(a, b):
    return jnp.dot(a, b, preferred_element_type=F32)


def _split2(a):
    hi = a.astype(BF16)
    lo = (a - hi.astype(F32)).astype(BF16)
    return hi, lo


def _sigmoid(x):
    return 1.0 / (1.0 + jnp.exp(-x))


def _mod_kernel(c_ref, w_ref, b_ref, o_ref):
    c = c_ref[...]
    s = c * _sigmoid(c)
    s_hi, s_lo = _split2(s)
    w_hi, w_lo = _split2(w_ref[...])
    o_ref[...] = _mm(s_hi, w_hi) + _mm(s_lo, w_hi) + _mm(s_hi, w_lo) + b_ref[...]


def _modulation(cc, w_mod, b_mod):
    r, d = cc.shape
    n = w_mod.shape[1]
    tn = _tile(n, 1536, 128)
    return pl.pallas_call(
        _mod_kernel,
        out_shape=jax.ShapeDtypeStruct((r, n), F32),
        grid=(n // tn,),
        in_specs=[pl.BlockSpec((r, d), lambda j: (0, 0)),
                  pl.BlockSpec((d, tn), lambda j: (0, j)),
                  pl.BlockSpec((1, tn), lambda j: (0, j))],
        out_specs=pl.BlockSpec((r, tn), lambda j: (0, j)),
        compiler_params=_cparams(("parallel",), VMEM_LIMIT),
        name="mod",
    )(cc, w_mod, b_mod)


def _inproj_kernel(x_ref, mod_ref, g_ref, w16_ref, w32_ref, o16_ref, o32_ref, hn_ref, *, n_lat, tm, ctx_row, n16):
    b = pl.program_id(0)
    t = pl.program_id(1)
    j = pl.program_id(2)

    @pl.when(j == 0)
    def _():
        x = x_ref[0]
        ms = jnp.mean(x * x, axis=-1, keepdims=True)
        y = x * lax.rsqrt(ms + EPS) * g_ref[...]
        row = t * tm + lax.broadcasted_iota(jnp.int32, (tm, 1), 0)
        is_ctx = row >= n_lat
        m_lat = mod_ref[pl.ds(b, 1), :]
        m_ctx = mod_ref[pl.ds(ctx_row, 1), :]
        shift = jnp.where(is_ctx, m_ctx[:, 0:D_MODEL], m_lat[:, 0:D_MODEL])
        scale = jnp.where(is_ctx, m_ctx[:, D_MODEL:2 * D_MODEL], m_lat[:, D_MODEL:2 * D_MODEL])
        hn_ref[...] = (y * (1.0 + scale) + shift).astype(BF16)

    @pl.when(j < n16)
    def _():
        o16_ref[0] = _mm(hn_ref[...], w16_ref[...]).astype(BF16)

    @pl.when(j >= n16)
    def _():
        o32_ref[0] = _mm(hn_ref[...], w32_ref[...])


def _inproj(xa, mod, g, w16, w32, n_lat):
    bsz, rows, d = xa.shape
    tm = _tile(rows, 1152)
    n16 = N_PROJ16 // PROJ16_TN
    n32 = N_PROJ32 // PROJ32_TN

    def j16(j):
        return jnp.minimum(j, n16 - 1)

    def j32(j):
        return jnp.maximum(j - n16, 0)

    kern = functools.partial(_inproj_kernel, n_lat=n_lat, tm=tm, ctx_row=bsz, n16=n16)
    return pl.pallas_call(
        kern,
        out_shape=(jax.ShapeDtypeStruct((bsz, rows, N_PROJ16), BF16),
                   jax.ShapeDtypeStruct((bsz, rows, N_PROJ32), F32)),
        grid=(bsz, rows // tm, n16 + n32),
        in_specs=[pl.BlockSpec((1, tm, d), lambda b, t, j: (b, t, 0)),
                  pl.BlockSpec(mod.shape, lambda b, t, j: (0, 0)),
                  pl.BlockSpec((1, d), lambda b, t, j: (0, 0)),
                  pl.BlockSpec((d, PROJ16_TN), lambda b, t, j: (0, j16(j))),
                  pl.BlockSpec((d, PROJ32_TN), lambda b, t, j: (0, j32(j)))],
        out_specs=(pl.BlockSpec((1, tm, PROJ16_TN), lambda b, t, j: (b, t, j16(j))),
                   pl.BlockSpec((1, tm, PROJ32_TN), lambda b, t, j: (b, t, j32(j)))),
        scratch_shapes=[pltpu.VMEM((tm, d), BF16)],
        compiler_params=_cparams(("parallel", "parallel", "arbitrary"), VMEM_LIMIT),
        name="inproj",
    )(xa, mod, g, w16, w32)


def _mlaq_kernel(a0_ref, a1_ref, a2_ref, g_ref, w_ref, cos_ref, sin_ref, o_ref, *, tm):
    parts = [a0_ref[0].astype(F32), a1_ref[0].astype(F32), a2_ref[0].astype(F32)]
    ss = sum(jnp.sum(p * p, axis=-1, keepdims=True) for p in parts)
    r = lax.rsqrt(ss * (1.0 / MLA_Q_RANK) + EPS)
    acc = None
    for j, p in enumerate(parts):
        hj = (p * r * g_ref[:, j * 256:(j + 1) * 256]).astype(BF16)
        d = _mm(hj, w_ref[j * 256:(j + 1) * 256, :])
        acc = d if acc is None else acc + d
    lane = lax.broadcasted_iota(jnp.int32, (tm, HEAD_DIM), 1)
    scale = MLA_QK ** -0.5 * LOG2_E
    cos = cos_ref[...]
    sin = sin_ref[...]
    for h in range(HEADS):
        nope = acc[:, h * MLA_QK_PAD:h * MLA_QK_PAD + HEAD_DIM]
        rp = acc[:, h * MLA_QK_PAD + HEAD_DIM:(h + 1) * MLA_QK_PAD]
        swapped = jnp.where(lane < 32, pltpu.roll(rp, 96, 1), pltpu.roll(rp, 32, 1))
        rot = rp * cos + swapped * sin
        o_ref[0, h, :, 0:HEAD_DIM] = (nope * scale).astype(BF16)
        o_ref[0, h, :, HEAD_DIM:MLA_QK_PAD] = (rot * scale).astype(BF16)


def _mla_q(p, g, w, cos_q, sin_q, n_lat):
    bsz = p.shape[0]
    tm = _tile(n_lat, 512)
    cb = COL_QA // 256
    kern = functools.partial(_mlaq_kernel, tm=tm)
    return pl.pallas_call(
        kern,
        out_shape=jax.ShapeDtypeStruct((bsz, HEADS, n_lat, MLA_QK_PAD), BF16),
        grid=(bsz, n_lat // tm),
        in_specs=[pl.BlockSpec((1, tm, 256), lambda b, t: (b, t, cb)),
                  pl.BlockSpec((1, tm, 256), lambda b, t: (b, t, cb + 1)),
                  pl.BlockSpec((1, tm, 256), lambda b, t: (b, t, cb + 2)),
                  pl.BlockSpec((1, MLA_Q_RANK), lambda b, t: (0, 0)),
                  pl.BlockSpec(w.shape, lambda b, t: (0, 0)),
                  pl.BlockSpec((tm, HEAD_DIM), lambda b, t: (t, 0)),
                  pl.BlockSpec((tm, HEAD_DIM), lambda b, t: (t, 0))],
        out_specs=pl.BlockSpec((1, HEADS, tm, MLA_QK_PAD), lambda b, t: (b, 0, t, 0)),
        compiler_params=_cparams(("parallel", "parallel"), VMEM_LIMIT),
        name="mla_q",
    )(p, p, p, g, w, cos_q, sin_q)


def _mlakv_kernel(c_ref, kr_ref, g_ref, w_ref, cos_ref, sin_ref, k_ref, v_ref):
    c = c_ref[0].astype(F32)
    ms = jnp.mean(c * c, axis=-1, keepdims=True)
    hn = (c * lax.rsqrt(ms + EPS) * g_ref[...]).astype(BF16)
    kv = _mm(hn, w_ref[...])
    grp = kr_ref[0].astype(F32)
    rot = (grp * cos_ref[...] + pltpu.roll(grp, 64, 1) * sin_ref[...]).astype(BF16)
    for h in range(HEADS):
        k_ref[0, h, :, 0:HEAD_DIM] = kv[:, h * HEAD_DIM:(h + 1) * HEAD_DIM].astype(BF16)
        k_ref[0, h, :, HEAD_DIM:MLA_QK_PAD] = rot
        v_ref[0, h] = kv[:, D_MODEL + h * HEAD_DIM:D_MODEL + (h + 1) * HEAD_DIM].astype(BF16)


def _mla_kv(p, g, w, cos_k, sin_k):
    bsz, rows, _ = p.shape
    tm = _tile(rows, 768)
    return pl.pallas_call(
        _mlakv_kernel,
        out_shape=(jax.ShapeDtypeStruct((bsz, HEADS, rows, MLA_QK_PAD), BF16),
                   jax.ShapeDtypeStruct((bsz, HEADS, rows, HEAD_DIM), BF16)),
        grid=(bsz, rows // tm),
        in_specs=[pl.BlockSpec((1, tm, MLA_KV_RANK), lambda b, t: (b, t, COL_CKV // MLA_KV_RANK)),
                  pl.BlockSpec((1, tm, 128), lambda b, t: (b, t, COL_KR // 128)),
                  pl.BlockSpec((1, MLA_KV_RANK), lambda b, t: (0, 0)),
                  pl.BlockSpec(w.shape, lambda b, t: (0, 0)),
                  pl.BlockSpec((tm, 128), lambda b, t: (t, 0)),
                  pl.BlockSpec((tm, 128), lambda b, t: (t, 0))],
        out_specs=(pl.BlockSpec((1, HEADS, tm, MLA_QK_PAD), lambda b, t: (b, 0, t, 0)),
                   pl.BlockSpec((1, HEADS, tm, HEAD_DIM), lambda b, t: (b, 0, t, 0))),
        compiler_params=_cparams(("parallel", "parallel"), VMEM_LIMIT),
        name="mla_kv",
    )(p, p, g, w, cos_k, sin_k)


def _attn_kernel(q_ref, k_ref, v_ref, o_ref, *, tq, sub):
    k = k_ref[0, 0]
    v = v_ref[0, 0]
    v1 = jnp.concatenate([v, jnp.ones_like(v)], axis=1)
    for r in range(tq // sub):
        rows = slice(r * sub, (r + 1) * sub)
        s = _nt(q_ref[0, 0, rows, :], k)
        m = jnp.max(s, axis=-1, keepdims=True)
        p = jnp.exp2(s - m).astype(BF16)
        ol = _mm(p, v1)
        o_ref[0, rows, :] = (ol[:, 0:HEAD_DIM] * (1.0 / ol[:, HEAD_DIM:HEAD_DIM + 1])).astype(BF16)


def _attention(q, k, v):
    bsz, _, n, _ = q.shape
    m = k.shape[2]
    tq = _tile(n, 2048)
    sub = _tile(tq, 512)
    kern = functools.partial(_attn_kernel, tq=tq, sub=sub)
    return pl.pallas_call(
        kern,
        out_shape=jax.ShapeDtypeStruct((bsz, n, HEADS * HEAD_DIM), BF16),
        grid=(bsz, HEADS, n // tq),
        in_specs=[pl.BlockSpec((1, 1, tq, MLA_QK_PAD), lambda b, h, t: (b, h, t, 0)),
                  pl.BlockSpec((1, 1, m, MLA_QK_PAD), lambda b, h, t: (b, h, 0, 0)),
                  pl.BlockSpec((1, 1, m, HEAD_DIM), lambda b, h, t: (b, h, 0, 0))],
        out_specs=pl.BlockSpec((1, tq, HEAD_DIM), lambda b, h, t: (b, t, h)),
        compiler_params=_cparams(("parallel", "parallel", "arbitrary"), VMEM_LIMIT),
        name="attention",
    )(q, k, v)


def _level_ref(cum, blk, reverse):
    c = cum.shape[0]
    half = blk // 2
    r = half if reverse else half - 1
    if blk >= 8:
        x = cum.reshape(c // blk, blk, HEAD_DIM)
        e = jnp.broadcast_to(x[:, r:r + 1, :], x.shape)
        return e.reshape(c, HEAD_DIM)
    x = cum.reshape(c // 8, 8, HEAD_DIM)
    sub = lax.broadcasted_iota(jnp.int32, x.shape, 1)
    e = None
    for jb in range(8 // blk):
        cand = jnp.broadcast_to(x[:, jb * blk + r:jb * blk + r + 1, :], x.shape)
        e = cand if e is None else jnp.where(sub >= jb * blk, cand, e)
    return e.reshape(c, HEAD_DIM)


HG_SUB = 32
HG_SUB_LEVEL = HG_SUB.bit_length() - 1
HG_MAX_EXP2 = 100.0


def _hgrn_state(z, v, lb, tri, st, *, chunk, reverse):
    f = lb + (1.0 - lb) * _sigmoid(z)
    kb = (1.0 - f).astype(BF16)
    g = jnp.log(f) * LOG2_E

    g_hi, g_lo = _split2(g)
    cum = _mm(tri, jnp.concatenate([g_hi, g_lo], axis=0))
    last = 0 if reverse else chunk - 1
    tot = cum[last:last + 1, :]
    kt = kb * jnp.exp2(tot - cum).astype(BF16)
    st_new = st * jnp.exp2(tot) + _mm(v.astype(F32).T.astype(BF16), kt)
    return kb, cum, st_new


def _sub_block_decay(cum, reverse):
    c = cum.shape[0]
    x = cum.reshape(c // HG_SUB, HG_SUB, HEAD_DIM)
    zero = jnp.zeros((1, 1, HEAD_DIM), F32)
    if reverse:
        edge = jnp.concatenate([x[1:, 0:1, :], zero], axis=0)
    else:
        edge = jnp.concatenate([zero, x[:-1, HG_SUB - 1:HG_SUB, :]], axis=0)
    return (x - edge).reshape(c, HEAD_DIM)


def _hgrn_readout(q, v, kb, cum, sub, lvl, st, *, chunk, reverse, shared):
    q = q.astype(F32)
    qb = (q * _sigmoid(q) * (HEAD_DIM ** -0.5)).astype(BF16)
    if shared:
        att = jnp.where(lvl == 0, _nt(qb * jnp.exp2(sub).astype(BF16), kb * jnp.exp2(-sub).astype(BF16)).astype(BF16),
                        jnp.zeros((), BF16))
        first = HG_SUB_LEVEL + 1
    else:
        att = jnp.where(lvl == 0, _nt(qb, kb).astype(BF16), jnp.zeros((), BF16))
        first = 1
    for lv in range(first, chunk.bit_length()):
        zrel = cum - _level_ref(cum, 1 << lv, reverse)
        neg_abs = pltpu.bitcast(pltpu.bitcast(zrel, jnp.uint32) | jnp.uint32(0x80000000), F32)
        e = jnp.exp2(neg_abs).astype(BF16)
        att = jnp.where(lvl == lv, _nt(qb * e, kb * e).astype(BF16), att)
    return _nt(qb * jnp.exp2(cum).astype(BF16), st.astype(BF16)) + _mm(att, v.astype(BF16))


def _hgrn_kernel(qf_ref, zf_ref, if_ref, qb_ref, zb_ref, ib_ref, lb_ref, tri_ref, lvl_ref, of_ref, ob_ref, st_ref,
                 *, chunk, group, n_ctx_chunks):
    step = pl.program_id(2)

    @pl.when(step == 0)
    def _():
        st_ref[...] = jnp.zeros_like(st_ref)

    ins = ((qf_ref, zf_ref, if_ref), (qb_ref, zb_ref, ib_ref))
    outs = (of_ref, ob_ref)
    chains = [(d, j) for j in range(group) for d in range(2)]

    def cols(j):
        return slice(j * HEAD_DIM, (j + 1) * HEAD_DIM)

    def advance(want_out):
        old = [st_ref[d, j] for d, j in chains]
        parts = [_hgrn_state(ins[d][1][0, :, cols(j)], ins[d][2][0, :, cols(j)], lb_ref[d, j], tri_ref[d], st,
                             chunk=chunk, reverse=bool(d)) for (d, j), st in zip(chains, old)]
        for (d, j), (_, _, st_new) in zip(chains, parts):
            st_ref[d, j] = st_new
        if not want_out:
            return
        subs = [_sub_block_decay(cum, bool(d)) for (d, j), (_, cum, _) in zip(chains, parts)]
        low = subs[0]
        for s in subs[1:]:
            low = jnp.minimum(low, s)
        low = jnp.min(jnp.min(low, axis=0, keepdims=True), axis=1, keepdims=True)
        in_range = low[0, 0] >= -HG_MAX_EXP2

        def readouts(shared):
            for (d, j), (kb, cum, _), sub, st in zip(chains, parts, subs, old):
                o = _hgrn_readout(ins[d][0][0, :, cols(j)], ins[d][2][0, :, cols(j)], kb, cum, sub,
                                  lvl_ref[d, int(shared)], st, chunk=chunk, reverse=bool(d), shared=shared)
                outs[d][0, j] = o.astype(BF16)

        @pl.when(in_range)
        def _():
            readouts(True)

        @pl.when(jnp.logical_not(in_range))
        def _():
            readouts(False)

    @pl.when(step < n_ctx_chunks)
    def _():
        advance(False)

    @pl.when(step >= n_ctx_chunks)
    def _():
        advance(True)


def _hgrn_consts(chunk, reverse):
    t = np.arange(chunk)[:, None]
    s = np.arange(chunk)[None, :]
    x = t ^ s
    bitlen = np.zeros_like(x)
    for b in range(chunk.bit_length()):
        bitlen = np.where(x >> b > 0, b + 1, bitlen)
    valid = (s > t) if reverse else (t > s)
    lvl = np.where(t == s, 0, np.where(valid, bitlen, -1)).astype(np.float32)
    lvl_shared = np.where((lvl >= 0) & (lvl <= HG_SUB_LEVEL), 0, lvl)
    tri = ((s >= t) if reverse else (t >= s)).astype(np.float32)
    return np.concatenate([tri, tri], axis=1), np.stack([lvl, lvl_shared])


HG_GROUP = 8


def _hgrn_scans(p, pf, lb, n_lat):
    bsz, rows, _ = p.shape
    c = HG_CHUNK
    grp = HG_GROUP
    n_chunks = rows // c
    n_lat_c = n_lat // c
    n_ctx_c = n_chunks - n_lat_c
    consts = [_hgrn_consts(c, False), _hgrn_consts(c, True)]
    tri = jnp.asarray(np.stack([consts[0][0], consts[1][0]]), dtype=BF16)
    lvl = jnp.asarray(np.stack([consts[0][1], consts[1][1]]), dtype=BF16)

    def cidx_f(i):
        return jnp.where(i < n_ctx_c, n_lat_c + i, i - n_ctx_c)

    def cidx_b(i):
        return n_chunks - 1 - i

    def col_spec(col, cidx):
        return pl.BlockSpec((1, c, grp * HEAD_DIM), lambda b, h, i: (b, cidx(i), col // (grp * HEAD_DIM) + h))

    def out_spec(cidx):
        return pl.BlockSpec((1, grp, c, HEAD_DIM), lambda b, h, i: (b, h, cidx(jnp.maximum(i, n_ctx_c)), 0))

    o_shape = jax.ShapeDtypeStruct((bsz, HEADS, n_lat, HEAD_DIM), BF16)
    kern = functools.partial(_hgrn_kernel, chunk=c, group=grp, n_ctx_chunks=n_ctx_c)
    return pl.pallas_call(
        kern,
        out_shape=(o_shape, o_shape),
        grid=(bsz, HEADS // grp, n_chunks),
        in_specs=[col_spec(COL_HQ, cidx_f), col_spec(COL_FF, cidx_f), col_spec(COL_I, cidx_f),
                  col_spec(COL_HQ, cidx_b), col_spec(COL_FB, cidx_b), col_spec(COL_I, cidx_b),
                  pl.BlockSpec((2, grp, 1, HEAD_DIM), lambda b, h, i: (0, h, 0, 0)),
                  pl.BlockSpec((2, c, 2 * c), lambda b, h, i: (0, 0, 0)),
                  pl.BlockSpec((2, 2, c, c), lambda b, h, i: (0, 0, 0, 0))],
        out_specs=(out_spec(cidx_f), out_spec(cidx_b)),
        scratch_shapes=[pltpu.VMEM((2, grp, HEAD_DIM, HEAD_DIM), F32)],
        compiler_params=_cparams(("parallel", "parallel", "arbitrary"), VMEM_LIMIT),
        name="hgrn",
    )(p, pf, p, p, pf, p, lb, tri, lvl)


def _merge_kernel(ym_ref, of_ref, ob_ref, go_ref, gm_ref, gh_ref, x_ref, mod_ref, hgg_ref, wout_ref,
                  gffn_ref, wr_ref, br_ref, x1_ref, h2_ref, lg_ref, y_scr):
    b = pl.program_id(0)
    for h in range(HEADS):
        sl = slice(h * HEAD_DIM, (h + 1) * HEAD_DIM)
        o = of_ref[0, h].astype(F32) + ob_ref[0, h].astype(F32)
        ms = jnp.mean(o * o, axis=-1, keepdims=True)
        g = go_ref[0, :, sl].astype(F32)
        yh = o * lax.rsqrt(ms + EPS) * hgg_ref[...] * (g * _sigmoid(g))
        y = (_sigmoid(gm_ref[0, :, sl].astype(F32)) * ym_ref[0, :, sl].astype(F32)
             + _sigmoid(gh_ref[0, :, sl].astype(F32)) * yh)
        y_scr[:, sl] = y.astype(BF16)
    mix = _mm(y_scr[...], wout_ref[...])
    m = mod_ref[pl.ds(b, 1), :]
    x1 = x_ref[0] + m[:, 2 * D_MODEL:3 * D_MODEL] * mix
    x1_ref[0] = x1
    ms = jnp.mean(x1 * x1, axis=-1, keepdims=True)
    h2 = (x1 * lax.rsqrt(ms + EPS) * gffn_ref[...]) * (1.0 + m[:, 4 * D_MODEL:5 * D_MODEL]) + m[:, 3 * D_MODEL:4 * D_MODEL]
    h_hi = h2.astype(BF16)
    u = pltpu.bitcast(h_hi.astype(F32), jnp.uint32)
    half = D_MODEL // 2
    h2_ref[...] = (u[:, 0:half] >> 16) | u[:, half:D_MODEL]
    h_lo = (h2 - h_hi.astype(F32)).astype(BF16)
    w_hi, w_lo = _split2(wr_ref[...])
    lg_ref[...] = _nt(w_hi, h_hi) + _nt(w_lo, h_hi) + _nt(w_hi, h_lo) + br_ref[...]


def _merge(y_mla, o_f, o_b, p, x, mod, hg_g, w_out, g_ffn, w_r_t, b_r):
    bsz, n, d = x.shape
    tm = _tile(n, 256, 128)
    nt = n // tm

    def pcol(col):
        return pl.BlockSpec((1, tm, d), lambda b, t: (b, t, col // d))

    tok = lambda b, t: (b, t, 0)
    const2 = lambda b, t: (0, 0)
    return pl.pallas_call(
        _merge_kernel,
        out_shape=(jax.ShapeDtypeStruct((bsz, n, d), F32),
                   jax.ShapeDtypeStruct((bsz * n, d // 2), jnp.uint32),
                   jax.ShapeDtypeStruct((N_EXPERTS, bsz * n), F32)),
        grid=(bsz, nt),
        in_specs=[pl.BlockSpec((1, tm, d), tok),
                  pl.BlockSpec((1, HEADS, tm, HEAD_DIM), lambda b, t: (b, 0, t, 0)),
                  pl.BlockSpec((1, HEADS, tm, HEAD_DIM), lambda b, t: (b, 0, t, 0)),
                  pcol(COL_GO), pcol(COL_MM), pcol(COL_MH),
                  pl.BlockSpec((1, tm, d), tok),
                  pl.BlockSpec(mod.shape, const2),
                  pl.BlockSpec((1, HEAD_DIM), const2),
                  pl.BlockSpec((d, d), const2),
                  pl.BlockSpec((1, d), const2),
                  pl.BlockSpec((N_EXPERTS, d), const2),
                  pl.BlockSpec((N_EXPERTS, 1), const2)],
        out_specs=(pl.BlockSpec((1, tm, d), tok),
                   pl.BlockSpec((tm, d // 2), lambda b, t: (b * nt + t, 0)),
                   pl.BlockSpec((N_EXPERTS, tm), lambda b, t: (0, b * nt + t))),
        scratch_shapes=[pltpu.VMEM((tm, d), BF16)],
        compiler_params=_cparams(("parallel", "parallel"), VMEM_LIMIT),
        name="merge",
    )(y_mla, o_f, o_b, p, p, p, x, mod, hg_g, w_out, g_ffn, w_r_t, b_r)


def _router_kernel(lg_ref, upper_ref, idx_ref, prob_ref, rank_ref, cnt_ref, run_ref, *, tm):
    @pl.when(pl.program_id(0) == 0)
    def _():
        run_ref[...] = jnp.zeros_like(run_ref)

    l = lg_ref[...]
    eidx = lax.broadcasted_iota(jnp.int32, (N_EXPERTS, tm), 0)
    vals, sel = [], []
    for _ in range(TOP_K):
        m = jnp.max(l, axis=0, keepdims=True)
        first = jnp.min(jnp.where(l == m, eidx, N_EXPERTS), axis=0, keepdims=True)
        vals.append(m)
        sel.append(first)
        l = jnp.where(eidx == first, -jnp.inf, l)
    ex = [jnp.exp(v - vals[0]) for v in vals]
    inv = 1.0 / (ex[0] + ex[1] + ex[2] + ex[3])
    onehot = jnp.zeros((N_EXPERTS, tm), F32)
    for k in range(TOP_K):
        onehot = onehot + jnp.where(eidx == sel[k], 1.0, 0.0)
    before = _mm(onehot.astype(BF16), upper_ref[...]) + run_ref[:, 0:1]
    for k in range(TOP_K):
        idx_ref[k:k + 1, :] = sel[k]
        prob_ref[k:k + 1, :] = ex[k] * inv
        rank_ref[k:k + 1, :] = jnp.sum(jnp.where(eidx == sel[k], before, 0.0), axis=0, keepdims=True).astype(jnp.int32)
    run_ref[...] = run_ref[...] + jnp.sum(onehot, axis=1, keepdims=True)
    cnt_ref[...] = run_ref[...]


def _router(logits_t):
    _, t = logits_t.shape
    tm = _tile(t, 1024, 128)
    upper = jnp.asarray(np.triu(np.ones((tm, tm), np.float32), 1), dtype=BF16)
    kern = functools.partial(_router_kernel, tm=tm)
    tok = pl.BlockSpec((TOP_K, tm), lambda i: (0, i))
    return pl.pallas_call(
        kern,
        out_shape=(jax.ShapeDtypeStruct((TOP_K, t), jnp.int32),
                   jax.ShapeDtypeStruct((TOP_K, t), F32),
                   jax.ShapeDtypeStruct((TOP_K, t), jnp.int32),
                   jax.ShapeDtypeStruct((N_EXPERTS, 128), F32)),
        grid=(t // tm,),
        in_specs=[pl.BlockSpec((N_EXPERTS, tm), lambda i: (0, i)),
                  pl.BlockSpec((tm, tm), lambda i: (0, 0))],
        out_specs=(tok, tok, tok, pl.BlockSpec((N_EXPERTS, 128), lambda i: (0, 0))),
        scratch_shapes=[pltpu.VMEM((N_EXPERTS, 128), F32)],
        compiler_params=_cparams(("arbitrary",), VMEM_LIMIT),
        name="router",
    )(logits_t, upper)


def _dest_kernel(ps_ref, idx_ref, rank_ref, o_ref):
    idx = idx_ref[...]
    dest = rank_ref[...]
    for e in range(N_EXPERTS):
        dest = dest + jnp.where(idx == e, ps_ref[e], 0)
    o_ref[...] = dest


def _dest_rows(pad_start, idx_t, rank_t):
    k, t = idx_t.shape
    tm = _tile(t, 4096, 128)
    tok = pl.BlockSpec((k, tm), lambda i, ps: (0, i))
    return pl.pallas_call(
        _dest_kernel,
        out_shape=jax.ShapeDtypeStruct((k, t), jnp.int32),
        grid_spec=pltpu.PrefetchScalarGridSpec(num_scalar_prefetch=1, grid=(t // tm,), in_specs=[tok, tok], out_specs=tok),
        compiler_params=_cparams(("parallel",), VMEM_LIMIT),
        name="dest_rows",
    )(pad_start, idx_t, rank_t)


def _dispatch_kernel(dest_ref, h2_ref, xs_in_ref, xs_ref, sem, *, tm):
    del xs_in_ref

    def body(r, carry):
        for k in range(TOP_K):
            pltpu.make_async_copy(h2_ref.at[pl.ds(r, 1)], xs_ref.at[pl.ds(dest_ref[0, 0, k * tm + r], 1)], sem).start()
        return carry

    lax.fori_loop(0, tm, body, 0)
    for _ in range(TOP_K):
        pltpu.make_async_copy(h2_ref, xs_ref.at[pl.ds(0, tm)], sem).wait()


def _dispatch(dest_tiles, h2p, n_rows, tm):
    t, w = h2p.shape
    xs0 = jnp.zeros((n_rows, w), h2p.dtype)
    kern = functools.partial(_dispatch_kernel, tm=tm)
    return pl.pallas_call(
        kern,
        out_shape=jax.ShapeDtypeStruct((n_rows, w), h2p.dtype),
        grid=(t // tm,),
        in_specs=[pl.BlockSpec((1, 1, TOP_K * tm), lambda i: (i, 0, 0), memory_space=pltpu.SMEM),
                  pl.BlockSpec((tm, w), lambda i: (i, 0)),
                  pl.BlockSpec(memory_space=pl.ANY)],
        out_specs=pl.BlockSpec(memory_space=pl.ANY),
        scratch_shapes=[pltpu.SemaphoreType.DMA],
        input_output_aliases={2: 0},
        compiler_params=_cparams(("arbitrary",), VMEM_LIMIT),
        name="dispatch",
    )(dest_tiles, h2p, xs0)


GU_GROUP = 256


def _ffn_kernel(be_ref, nu_ref, x_ref, wgu_ref, bgu_ref, wd_ref, bd_ref, perm_ref, y_ref, wgu_s, wd_s):
    i = pl.program_id(0)
    live = i < nu_ref[0]
    new_expert = jnp.logical_or(i == 0, be_ref[i] != be_ref[jnp.maximum(i - 1, 0)])

    @pl.when(jnp.logical_and(live, new_expert))
    def _():
        for g in range(2 * D_EXPERT // GU_GROUP):
            sl = slice(g * GU_GROUP, (g + 1) * GU_GROUP)
            wgu_s[:, sl] = _mm(wgu_ref[0, :, sl].astype(BF16), perm_ref[...]).astype(BF16)
        wd_s[...] = wd_ref[0].astype(BF16)

    @pl.when(live)
    def _():
        u = x_ref[...]
        lo = pltpu.bitcast(u << 16, F32).astype(BF16)
        hi = pltpu.bitcast(u & jnp.uint32(0xFFFF0000), F32).astype(BF16)
        x = jnp.concatenate([lo, hi], axis=1)
        gu = _mm(x, wgu_s[...]) + bgu_ref[0]
        half = GU_GROUP // 2
        n_grp = 2 * D_EXPERT // GU_GROUP
        glu = jnp.concatenate([gu[:, g * GU_GROUP:g * GU_GROUP + half] for g in range(n_grp)], axis=1)
        lin = jnp.concatenate([gu[:, g * GU_GROUP + half:(g + 1) * GU_GROUP] for g in range(n_grp)], axis=1)
        glu = jnp.minimum(glu, SWIGLU_LIMIT)
        lin = jnp.clip(lin, -SWIGLU_LIMIT, SWIGLU_LIMIT)
        act = glu * _sigmoid(SWIGLU_ALPHA * glu) * (lin + 1.0)
        y_ref[...] = _mm(act.astype(BF16), wd_s[...]) + bd_ref[0]

    @pl.when(jnp.logical_not(live))
    def _():
        y_ref[...] = jnp.zeros_like(y_ref)


def _ffn(block_e, n_used, xs, w_gu, b_gu, w_d, b_d):
    n_rows, w = xs.shape
    bm = FFN_ROWS
    nblk = n_rows // bm
    d = w * 2
    half = GU_GROUP // 2
    perm = np.zeros((GU_GROUP, GU_GROUP), np.float32)
    perm[2 * np.arange(half), np.arange(half)] = 1.0
    perm[2 * np.arange(half) + 1, half + np.arange(half)] = 1.0

    def xmap(i, be, nu):
        return (jnp.minimum(i, nu[0] - 1), 0)

    def wmap(i, be, nu):
        return (be[i], 0, 0)

    grid_spec = pltpu.PrefetchScalarGridSpec(
        num_scalar_prefetch=2,
        grid=(nblk,),
        in_specs=[pl.BlockSpec((bm, w), xmap),
                  pl.BlockSpec((1, d, 2 * D_EXPERT), wmap),
                  pl.BlockSpec((1, 1, 2 * D_EXPERT), wmap),
                  pl.BlockSpec((1, D_EXPERT, d), wmap),
                  pl.BlockSpec((1, 1, d), wmap),
                  pl.BlockSpec((GU_GROUP, GU_GROUP), lambda i, be, nu: (0, 0))],
        out_specs=pl.BlockSpec((bm, d), lambda i, be, nu: (i, 0)),
        scratch_shapes=[pltpu.VMEM((d, 2 * D_EXPERT), BF16), pltpu.VMEM((D_EXPERT, d), BF16)],
    )
    return pl.pallas_call(
        _ffn_kernel,
        out_shape=jax.ShapeDtypeStruct((n_rows, d), F32),
        grid_spec=grid_spec,
        compiler_params=_cparams(("arbitrary",), VMEM_LIMIT),
        name="ffn",
    )(block_e, n_used, xs, w_gu, b_gu, w_d, b_d, jnp.asarray(perm, dtype=BF16))


def _combine_kernel(dest_ref, dest_next_ref, y_ref, p_ref, x1_ref, mod_ref, g_ref, o_ref, buf_ref, sem, *, tm, nt):
    b = pl.program_id(0)
    step = b * nt + pl.program_id(1)
    n_steps = pl.num_programs(0) * nt
    slot = step % 2

    def gather(d_ref, s):
        def body(r, carry):
            for k in range(TOP_K):
                pltpu.make_async_copy(y_ref.at[pl.ds(d_ref[0, 0, k * tm + r], 1)],
                                      buf_ref.at[s, k, pl.ds(r, 1)], sem.at[s]).start()
            return carry
        lax.fori_loop(0, tm, body, 0)

    @pl.when(step == 0)
    def _():
        gather(dest_ref, 0)

    @pl.when(step + 1 < n_steps)
    def _():
        gather(dest_next_ref, 1 - slot)

    for k in range(TOP_K):
        pltpu.make_async_copy(y_ref.at[pl.ds(0, tm)], buf_ref.at[slot, k], sem.at[slot]).wait()
    p = p_ref[...]
    acc = p[:, 0:1] * buf_ref[slot, 0]
    for k in range(1, TOP_K):
        acc = acc + p[:, k:k + 1] * buf_ref[slot, k]
    m = mod_ref[pl.ds(b, 1), :]
    xo = x1_ref[0] + m[:, 5 * D_MODEL:6 * D_MODEL] * acc
    ms = jnp.mean(xo * xo, axis=-1, keepdims=True)
    o_ref[0] = xo * lax.rsqrt(ms + EPS) * g_ref[...]


def _combine(dest_tiles, y, probs_t, x1, mod, g_fin, tm):
    bsz, n, d = x1.shape
    nt = n // tm
    kern = functools.partial(_combine_kernel, tm=tm, nt=nt)
    last = bsz * nt - 1
    return pl.pallas_call(
        kern,
        out_shape=jax.ShapeDtypeStruct((bsz, n, d), F32),
        grid=(bsz, nt),
        in_specs=[pl.BlockSpec((1, 1, TOP_K * tm), lambda b, t: (b * nt + t, 0, 0), memory_space=pltpu.SMEM),
                  pl.BlockSpec((1, 1, TOP_K * tm), lambda b, t: (jnp.minimum(b * nt + t + 1, last), 0, 0),
                               memory_space=pltpu.SMEM),
                  pl.BlockSpec(memory_space=pl.ANY),
                  pl.BlockSpec((tm, TOP_K), lambda b, t: (b * nt + t, 0)),
                  pl.BlockSpec((1, tm, d), lambda b, t: (b, t, 0)),
                  pl.BlockSpec(mod.shape, lambda b, t: (0, 0)),
                  pl.BlockSpec((1, d), lambda b, t: (0, 0))],
        out_specs=pl.BlockSpec((1, tm, d), lambda b, t: (b, t, 0)),
        scratch_shapes=[pltpu.VMEM((2, TOP_K, tm, d), F32), pltpu.SemaphoreType.DMA((2,))],
        compiler_params=_cparams(("arbitrary", "arbitrary"), VMEM_LIMIT),
        name="combine",
    )(dest_tiles, dest_tiles, y, probs_t, x1, mod, g_fin)


def _prep_w_in(w_in):
    q_a, kv, hq, ff, fb, hi, go, mm, mh = jnp.split(
        w_in, np.cumsum((MLA_Q_RANK, MLA_KV_RANK + MLA_ROPE) + (D_MODEL,) * 6).tolist(), axis=1)
    c_kv, kr = kv[:, :MLA_KV_RANK], kv[:, MLA_KV_RANK:]
    half = MLA_ROPE // 2
    kr_sw = jnp.concatenate([-kr[:, half:], kr[:, :half]], axis=1)
    w16 = jnp.concatenate([hq, hi, go, mm, mh, q_a, c_kv, kr, kr_sw], axis=1)
    w16 = jnp.pad(w16, ((0, 0), (0, N_PROJ16 - w16.shape[1])))
    return w16.astype(BF16), jnp.concatenate([ff, fb], axis=1).astype(BF16)


def _prep_w_q(w_q_b):
    w = w_q_b.reshape(MLA_Q_RANK, HEADS, MLA_QK)
    w = jnp.pad(w, ((0, 0), (0, 0), (0, MLA_QK_PAD - MLA_QK)))
    return w.reshape(MLA_Q_RANK, HEADS * MLA_QK_PAD).astype(BF16)


def _prep_w_kv(w_kv_b):
    w = w_kv_b.reshape(MLA_KV_RANK, HEADS, 2 * HEAD_DIM)
    k = w[:, :, :HEAD_DIM].reshape(MLA_KV_RANK, HEADS * HEAD_DIM)
    v = w[:, :, HEAD_DIM:].reshape(MLA_KV_RANK, HEADS * HEAD_DIM)
    return jnp.concatenate([k, v], axis=1).astype(BF16)


def _rope_tables(n_lat, n_ctx):
    f32 = np.float32
    rows = n_lat // GRID_W
    row = np.repeat(np.arange(rows), GRID_W).astype(f32)
    col = np.tile(np.arange(GRID_W), rows).astype(f32)
    n_freq = MLA_ROPE // 4
    inv = (f32(ROPE_BASE) ** (-np.arange(n_freq, dtype=f32) / f32(n_freq))).astype(f32)
    ang = np.concatenate([row[:, None] * inv, col[:, None] * inv], axis=-1).astype(f32)
    cos, sin = np.cos(ang).astype(f32), np.sin(ang).astype(f32)
    z64 = np.zeros((n_lat, 64), f32)
    cos_q = np.concatenate([cos, cos, z64], axis=1)
    sin_q = np.concatenate([-sin, sin, z64], axis=1)
    cos_k = np.concatenate([cos, cos, z64], axis=1)
    sin_k = np.concatenate([sin, sin, z64], axis=1)
    ctx_cos = np.concatenate([np.ones((n_ctx, 64), f32), np.zeros((n_ctx, 64), f32)], axis=1)
    cos_k = np.concatenate([cos_k, ctx_cos], axis=0)
    sin_k = np.concatenate([sin_k, np.zeros((n_ctx, 128), f32)], axis=0)
    return jnp.asarray(cos_q), jnp.asarray(sin_q), jnp.asarray(cos_k), jnp.asarray(sin_k)


def kernel(x, c, ctx, c_ctx, w_mod, b_mod, norm_mix_g, w_in, mla_q_norm_g, w_q_b, mla_kv_norm_g, w_kv_b,
           hg_lb_logits, hg_norm_g, w_out, norm_ffn_g, w_router, b_router, w_gate_up, b_gate_up, w_down,
           b_down, final_norm_g):
    bsz, n_lat, d = x.shape
    n_ctx = ctx.shape[1]
    assert d == D_MODEL and w_mod.shape[0] == 1
    assert n_lat % HG_CHUNK == 0 and n_ctx % HG_CHUNK == 0 and n_lat % GRID_W == 0
    t_tok = bsz * n_lat

    mod_rows = -(-(bsz + 1) // 8) * 8
    cc = jnp.concatenate([c, c_ctx[None, :], jnp.zeros((mod_rows - bsz - 1, d), F32)], axis=0)
    mod = _modulation(cc, w_mod[0], b_mod[0][None, :])

    xa = jnp.concatenate([x, ctx], axis=1)
    w16, w32 = _prep_w_in(w_in[0])
    p, pf = _inproj(xa, mod, norm_mix_g[0][None, :], w16, w32, n_lat)

    cos_q, sin_q, cos_k, sin_k = _rope_tables(n_lat, n_ctx)
    q = _mla_q(p, mla_q_norm_g[0][None, :], _prep_w_q(w_q_b[0]), cos_q, sin_q, n_lat)
    k, v = _mla_kv(p, mla_kv_norm_g[0][None, :], _prep_w_kv(w_kv_b[0]), cos_k, sin_k)
    y_mla = _attention(q, k, v)

    lb = jax.nn.softmax(hg_lb_logits.astype(F32), axis=1)[:, 0, :]
    o_f, o_b = _hgrn_scans(p, pf, lb.reshape(2, HEADS, 1, HEAD_DIM), n_lat)

    x1, h2p, logits_t = _merge(y_mla, o_f, o_b, p, x, mod, hg_norm_g[0][None, :], w_out[0].astype(BF16),
                               norm_ffn_g[0][None, :], w_router[0].T, b_router[0][:, None])

    idx_t, prob_t, rank_t, cnt = _router(logits_t)

    counts = cnt[:, 0].astype(jnp.int32)
    padded = (counts + FFN_ROWS - 1) // FFN_ROWS * FFN_ROWS
    pad_end = jnp.cumsum(padded)
    pad_start = pad_end - padded
    dest = _dest_rows(pad_start.astype(jnp.int32), idx_t, rank_t)
    n_rows = -(-(t_tok * TOP_K + N_EXPERTS * (FFN_ROWS - 1)) // FFN_ROWS) * FFN_ROWS
    nblk = n_rows // FFN_ROWS
    block_start = jnp.arange(nblk, dtype=jnp.int32) * FFN_ROWS
    block_e = jnp.minimum(jnp.sum(pad_end[None, :] <= block_start[:, None], axis=1), N_EXPERTS - 1).astype(jnp.int32)
    n_used = (pad_end[-1:] // FFN_ROWS).astype(jnp.int32)

    tm_c = _tile(n_lat, 256)
    dest_tiles = dest.reshape(TOP_K, t_tok // tm_c, tm_c).transpose(1, 0, 2).reshape(t_tok // tm_c, 1, TOP_K * tm_c)

    xs = _dispatch(dest_tiles, h2p, n_rows, tm_c)

    n_grp = 2 * D_EXPERT // GU_GROUP
    b_gu = b_gate_up[0].reshape(N_EXPERTS, n_grp, GU_GROUP // 2, 2).transpose(0, 1, 3, 2).reshape(N_EXPERTS, 1, 2 * D_EXPERT)
    y = _ffn(block_e, n_used, xs, w_gate_up[0], b_gu, w_down[0], b_down[0][:, None, :])

    return _combine(dest_tiles, y, prob_t.T, x1, mod, final_norm_g[None, :], tm_c)
```

```python
import functools

import jax
import jax.numpy as jnp
import numpy as np
from jax import lax
from jax.experimental import pallas as pl
from jax.experimental.pallas import tpu as pltpu

F32 = jnp.float32
BF16 = jnp.bfloat16

D_MODEL = 1024
EPS = 1e-6
LOG2_E = 1.4426950408889634
N_MOD = 6
GRID_W = 64
ROPE_BASE = 10000.0

HEADS = 8
HEAD_DIM = 128
MLA_ROPE = 64
MLA_QK = HEAD_DIM + MLA_ROPE
MLA_QK_PAD = 256
MLA_Q_RANK = 768
MLA_KV_RANK = 256

N_EXPERTS = 32
TOP_K = 4
D_EXPERT = 1024
SWIGLU_LIMIT = 7.0
SWIGLU_ALPHA = 1.702

HG_CHUNK = 128
FFN_ROWS = 512

COL_HQ, COL_I, COL_GO, COL_MM, COL_MH = (i * D_MODEL for i in range(5))
COL_QA = 5 * D_MODEL
COL_CKV = COL_QA + MLA_Q_RANK
COL_KR = COL_CKV + MLA_KV_RANK
PROJ16_TN = 1280
N_PROJ16 = -(-(COL_KR + 2 * MLA_ROPE) // PROJ16_TN) * PROJ16_TN
COL_FF, COL_FB = 0, D_MODEL
N_PROJ32 = 2 * D_MODEL
PROJ32_TN = 1024

VMEM_LIMIT = 56 * 1024 * 1024


def _cparams(sem, vmem=None):
    return pltpu.CompilerParams(dimension_semantics=sem, vmem_limit_bytes=vmem)


def _tile(n, pref, mult=8):
    best = None
    for t in range(mult, min(n, pref) + 1, mult):
        if n % t == 0:
            best = t
    assert best is not None, (n, pref, mult)
    return best


def _nt(a, b):
    return lax.dot_general(a, b, (((1,), (1,)), ((), ())), preferred_element_type=F32)


def _mm(a, b):
    return jnp.dot(a, b, preferred_element_type=F32)


def _split2(a):
    hi = a.astype(BF16)
    lo = (a - hi.astype(F32)).astype(BF16)
    return hi, lo


def _sigmoid(x):
    return 1.0 / (1.0 + jnp.exp(-x))


def _mod_kernel(c_ref, w_ref, b_ref, o_ref):
    c = c_ref[...]
    s = c * _sigmoid(c)
    s_hi, s_lo = _split2(s)
    w_hi, w_lo = _split2(w_ref[...])
    o_ref[...] = _mm(s_hi, w_hi) + _mm(s_lo, w_hi) + _mm(s_hi, w_lo) + b_ref[...]


def _modulation(cc, w_mod, b_mod):
    r, d = cc.shape
    n = w_mod.shape[1]
    tn = _tile(n, 1536, 128)
    return pl.pallas_call(
        _mod_kernel,
        out_shape=jax.ShapeDtypeStruct((r, n), F32),
        grid=(n // tn,),
        in_specs=[pl.BlockSpec((r, d), lambda j: (0, 0)),
                  pl.BlockSpec((d, tn), lambda j: (0, j)),
                  pl.BlockSpec((1, tn), lambda j: (0, j))],
        out_specs=pl.BlockSpec((r, tn), lambda j: (0, j)),
        compiler_params=_cparams(("parallel",), VMEM_LIMIT),
        name="mod",
    )(cc, w_mod, b_mod)


def _inproj_kernel(x_ref, mod_ref, g_ref, w16_ref, w32_ref, o16_ref, o32_ref, hn_ref, *, n_lat, tm, ctx_row, n16):
    b = pl.program_id(0)
    t = pl.program_id(1)
    j = pl.program_id(2)

    @pl.when(j == 0)
    def _():
        x = x_ref[0]
        ms = jnp.mean(x * x, axis=-1, keepdims=True)
        y = x * lax.rsqrt(ms + EPS) * g_ref[...]
        row = t * tm + lax.broadcasted_iota(jnp.int32, (tm, 1), 0)
        is_ctx = row >= n_lat
        m_lat = mod_ref[pl.ds(b, 1), :]
        m_ctx = mod_ref[pl.ds(ctx_row, 1), :]
        shift = jnp.where(is_ctx, m_ctx[:, 0:D_MODEL], m_lat[:, 0:D_MODEL])
        scale = jnp.where(is_ctx, m_ctx[:, D_MODEL:2 * D_MODEL], m_lat[:, D_MODEL:2 * D_MODEL])
        hn_ref[...] = (y * (1.0 + scale) + shift).astype(BF16)

    @pl.when(j < n16)
    def _():
        o16_ref[0] = _mm(hn_ref[...], w16_ref[...]).astype(BF16)

    @pl.when(j >= n16)
    def _():
        o32_ref[0] = _mm(hn_ref[...], w32_ref[...])


def _inproj(xa, mod, g, w16, w32, n_lat):
    bsz, rows, d = xa.shape
    tm = _tile(rows, 1152)
    n16 = N_PROJ16 // PROJ16_TN
    n32 = N_PROJ32 // PROJ32_TN

    def j16(j):
        return jnp.minimum(j, n16 - 1)

    def j32(j):
        return jnp.maximum(j - n16, 0)

    kern = functools.partial(_inproj_kernel, n_lat=n_lat, tm=tm, ctx_row=bsz, n16=n16)
    return pl.pallas_call(
        kern,
        out_shape=(jax.ShapeDtypeStruct((bsz, rows, N_PROJ16), BF16),
                   jax.ShapeDtypeStruct((bsz, rows, N_PROJ32), F32)),
        grid=(bsz, rows // tm, n16 + n32),
        in_specs=[pl.BlockSpec((1, tm, d), lambda b, t, j: (b, t, 0)),
                  pl.BlockSpec(mod.shape, lambda b, t, j: (0, 0)),
                  pl.BlockSpec((1, d), lambda b, t, j: (0, 0)),
                  pl.BlockSpec((d, PROJ16_TN), lambda b, t, j: (0, j16(j))),
                  pl.BlockSpec((d, PROJ32_TN), lambda b, t, j: (0, j32(j)))],
        out_specs=(pl.BlockSpec((1, tm, PROJ16_TN), lambda b, t, j: (b, t, j16(j))),
                   pl.BlockSpec((1, tm, PROJ32_TN), lambda b, t, j: (b, t, j32(j)))),
        scratch_shapes=[pltpu.VMEM((tm, d), BF16)],
        compiler_params=_cparams(("parallel", "parallel", "arbitrary"), VMEM_LIMIT),
        name="inproj",
    )(xa, mod, g, w16, w32)


def _mlaq_kernel(a0_ref, a1_ref, a2_ref, g_ref, w_ref, cos_ref, sin_ref, o_ref, *, tm):
    parts = [a0_ref[0].astype(F32), a1_ref[0].astype(F32), a2_ref[0].astype(F32)]
    ss = sum(jnp.sum(p * p, axis=-1, keepdims=True) for p in parts)
    r = lax.rsqrt(ss * (1.0 / MLA_Q_RANK) + EPS)
    acc = None
    for j, p in enumerate(parts):
        hj = (p * r * g_ref[:, j * 256:(j + 1) * 256]).astype(BF16)
        d = _mm(hj, w_ref[j * 256:(j + 1) * 256, :])
        acc = d if acc is None else acc + d
    lane = lax.broadcasted_iota(jnp.int32, (tm, HEAD_DIM), 1)
    scale = MLA_QK ** -0.5 * LOG2_E
    cos = cos_ref[...]
    sin = sin_ref[...]
    for h in range(HEADS):
        nope = acc[:, h * MLA_QK_PAD:h * MLA_QK_PAD + HEAD_DIM]
        rp = acc[:, h * MLA_QK_PAD + HEAD_DIM:(h + 1) * MLA_QK_PAD]
        swapped = jnp.where(lane < 32, pltpu.roll(rp, 96, 1), pltpu.roll(rp, 32, 1))
        rot = rp * cos + swapped * sin
        o_ref[0, h, :, 0:HEAD_DIM] = (nope * scale).astype(BF16)
        o_ref[0, h, :, HEAD_DIM:MLA_QK_PAD] = (rot * scale).astype(BF16)


def _mla_q(p, g, w, cos_q, sin_q, n_lat):
    bsz = p.shape[0]
    tm = _tile(n_lat, 512)
    cb = COL_QA // 256
    kern = functools.partial(_mlaq_kernel, tm=tm)
    return pl.pallas_call(
        kern,
        out_shape=jax.ShapeDtypeStruct((bsz, HEADS, n_lat, MLA_QK_PAD), BF16),
        grid=(bsz, n_lat // tm),
        in_specs=[pl.BlockSpec((1, tm, 256), lambda b, t: (b, t, cb)),
                  pl.BlockSpec((1, tm, 256), lambda b, t: (b, t, cb + 1)),
                  pl.BlockSpec((1, tm, 256), lambda b, t: (b, t, cb + 2)),
                  pl.BlockSpec((1, MLA_Q_RANK), lambda b, t: (0, 0)),
                  pl.BlockSpec(w.shape, lambda b, t: (0, 0)),
                  pl.BlockSpec((tm, HEAD_DIM), lambda b, t: (t, 0)),
                  pl.BlockSpec((tm, HEAD_DIM), lambda b, t: (t, 0))],
        out_specs=pl.BlockSpec((1, HEADS, tm, MLA_QK_PAD), lambda b, t: (b, 0, t, 0)),
        compiler_params=_cparams(("parallel", "parallel"), VMEM_LIMIT),
        name="mla_q",
    )(p, p, p, g, w, cos_q, sin_q)


def _mlakv_kernel(c_ref, kr_ref, g_ref, w_ref, cos_ref, sin_ref, k_ref, v_ref):
    c = c_ref[0].astype(F32)
    ms = jnp.mean(c * c, axis=-1, keepdims=True)
    hn = (c * lax.rsqrt(ms + EPS) * g_ref[...]).astype(BF16)
    kv = _mm(hn, w_ref[...])
    grp = kr_ref[0].astype(F32)
    rot = (grp * cos_ref[...] + pltpu.roll(grp, 64, 1) * sin_ref[...]).astype(BF16)
    for h in range(HEADS):
        k_ref[0, h, :, 0:HEAD_DIM] = kv[:, h * HEAD_DIM:(h + 1) * HEAD_DIM].astype(BF16)
        k_ref[0, h, :, HEAD_DIM:MLA_QK_PAD] = rot
        v_ref[0, h] = kv[:, D_MODEL + h * HEAD_DIM:D_MODEL + (h + 1) * HEAD_DIM].astype(BF16)


def _mla_kv(p, g, w, cos_k, sin_k):
    bsz, rows, _ = p.shape
    tm = _tile(rows, 768)
    return pl.pallas_call(
        _mlakv_kernel,
        out_shape=(jax.ShapeDtypeStruct((bsz, HEADS, rows, MLA_QK_PAD), BF16),
                   jax.ShapeDtypeStruct((bsz, HEADS, rows, HEAD_DIM), BF16)),
        grid=(bsz, rows // tm),
        in_specs=[pl.BlockSpec((1, tm, MLA_KV_RANK), lambda b, t: (b, t, COL_CKV // MLA_KV_RANK)),
                  pl.BlockSpec((1, tm, 128), lambda b, t: (b, t, COL_KR // 128)),
                  pl.BlockSpec((1, MLA_KV_RANK), lambda b, t: (0, 0)),
                  pl.BlockSpec(w.shape, lambda b, t: (0, 0)),
                  pl.BlockSpec((tm, 128), lambda b, t: (t, 0)),
                  pl.BlockSpec((tm, 128), lambda b, t: (t, 0))],
        out_specs=(pl.BlockSpec((1, HEADS, tm, MLA_QK_PAD), lambda b, t: (b, 0, t, 0)),
                   pl.BlockSpec((1, HEADS, tm, HEAD_DIM), lambda b, t: (b, 0, t, 0))),
        compiler_params=_cparams(("parallel", "parallel"), VMEM_LIMIT),
        name="mla_kv",
    )(p, p, g, w, cos_k, sin_k)


def _attn_kernel(q_ref, k_ref, v_ref, o_ref, *, tq, sub):
    k = k_ref[0, 0]
    v = v_ref[0, 0]
    v1 = jnp.concatenate([v, jnp.ones_like(v)], axis=1)
    for r in range(tq // sub):
        rows = slice(r * sub, (r + 1) * sub)
        s = _nt(q_ref[0, 0, rows, :], k)
        m = jnp.max(s, axis=-1, keepdims=True)
        p = jnp.exp2(s - m).astype(BF16)
        ol = _mm(p, v1)
        o_ref[0, rows, :] = (ol[:, 0:HEAD_DIM] * (1.0 / ol[:, HEAD_DIM:HEAD_DIM + 1])).astype(BF16)


def _attention(q, k, v):
    bsz, _, n, _ = q.shape
    m = k.shape[2]
    tq = _tile(n, 2048)
    sub = _tile(tq, 512)
    kern = functools.partial(_attn_kernel, tq=tq, sub=sub)
    return pl.pallas_call(
        kern,
        out_shape=jax.ShapeDtypeStruct((bsz, n, HEADS * HEAD_DIM), BF16),
        grid=(bsz, HEADS, n // tq),
        in_specs=[pl.BlockSpec((1, 1, tq, MLA_QK_PAD), lambda b, h, t: (b, h, t, 0)),
                  pl.BlockSpec((1, 1, m, MLA_QK_PAD), lambda b, h, t: (b, h, 0, 0)),
                  pl.BlockSpec((1, 1, m, HEAD_DIM), lambda b, h, t: (b, h, 0, 0))],
        out_specs=pl.BlockSpec((1, tq, HEAD_DIM), lambda b, h, t: (b, t, h)),
        compiler_params=_cparams(("parallel", "parallel", "arbitrary"), VMEM_LIMIT),
        name="attention",
    )(q, k, v)


def _level_ref(cum, blk, reverse):
    c = cum.shape[0]
    half = blk // 2
    r = half if reverse else half - 1
    if blk >= 8:
        x = cum.reshape(c // blk, blk, HEAD_DIM)
        e = jnp.broadcast_to(x[:, r:r + 1, :], x.shape)
        return e.reshape(c, HEAD_DIM)
    x = cum.reshape(c // 8, 8, HEAD_DIM)
    sub = lax.broadcasted_iota(jnp.int32, x.shape, 1)
    e = None
    for jb in range(8 // blk):
        cand = jnp.broadcast_to(x[:, jb * blk + r:jb * blk + r + 1, :], x.shape)
        e = cand if e is None else jnp.where(sub >= jb * blk, cand, e)
    return e.reshape(c, HEAD_DIM)


HG_SUB = 32
HG_SUB_LEVEL = HG_SUB.bit_length() - 1
HG_MAX_EXP2 = 100.0


def _hgrn_state(z, v, lb, tri, st, *, chunk, reverse):
    f = lb + (1.0 - lb) * _sigmoid(z)
    kb = (1.0 - f).astype(BF16)
    g = jnp.log(f) * LOG2_E

    g_hi, g_lo = _split2(g)
    yield
    cum = _mm(tri, jnp.concatenate([g_hi, g_lo], axis=0))
    yield
    last = 0 if reverse else chunk - 1
    tot = cum[last:last + 1, :]
    kt = kb * jnp.exp2(tot - cum).astype(BF16)
    vt = v.astype(F32).T.astype(BF16)
    yield
    st_new = st * jnp.exp2(tot) + _mm(vt, kt)
    return kb, cum, st_new


def _sub_block_decay(cum, reverse):
    c = cum.shape[0]
    x = cum.reshape(c // HG_SUB, HG_SUB, HEAD_DIM)
    zero = jnp.zeros((1, 1, HEAD_DIM), F32)
    if reverse:
        edge = jnp.concatenate([x[1:, 0:1, :], zero], axis=0)
    else:
        edge = jnp.concatenate([zero, x[:-1, HG_SUB - 1:HG_SUB, :]], axis=0)
    return (x - edge).reshape(c, HEAD_DIM)


def _hgrn_readout(q, v, kb, cum, sub, lvl, st, *, chunk, reverse, shared):
    q = q.astype(F32)
    qb = (q * _sigmoid(q) * (HEAD_DIM ** -0.5)).astype(BF16)
    yield
    if shared:
        att = jnp.where(lvl == 0, _nt(qb * jnp.exp2(sub).astype(BF16), kb * jnp.exp2(-sub).astype(BF16)).astype(BF16),
                        jnp.zeros((), BF16))
        first = HG_SUB_LEVEL + 1
    else:
        att = jnp.where(lvl == 0, _nt(qb, kb).astype(BF16), jnp.zeros((), BF16))
        first = 1
    for lv in range(first, chunk.bit_length()):
        zrel = cum - _level_ref(cum, 1 << lv, reverse)
        neg_abs = pltpu.bitcast(pltpu.bitcast(zrel, jnp.uint32) | jnp.uint32(0x80000000), F32)
        e = jnp.exp2(neg_abs).astype(BF16)
        yield
        att = jnp.where(lvl == lv, _nt(qb * e, kb * e).astype(BF16), att)
        yield
    return _nt(qb * jnp.exp2(cum).astype(BF16), st.astype(BF16)) + _mm(att, v.astype(BF16))


def _round_robin(gens):
    out = [None] * len(gens)
    active = list(range(len(gens)))
    while active:
        for i in list(active):
            try:
                next(gens[i])
            except StopIteration as e:
                out[i] = e.value
                active.remove(i)
    return out


def _hgrn_kernel(qf_ref, zf_ref, if_ref, qb_ref, zb_ref, ib_ref, lb_ref, tri_ref, lvl_ref, of_ref, ob_ref, st_ref,
                 *, chunk, group, n_ctx_chunks):
    step = pl.program_id(2)

    @pl.when(step == 0)
    def _():
        st_ref[...] = jnp.zeros_like(st_ref)

    ins = ((qf_ref, zf_ref, if_ref), (qb_ref, zb_ref, ib_ref))
    outs = (of_ref, ob_ref)
    chains = [(d, j) for j in range(group) for d in range(2)]

    def cols(j):
        return slice(j * HEAD_DIM, (j + 1) * HEAD_DIM)

    def advance(want_out):
        old = [st_ref[d, j] for d, j in chains]
        parts = _round_robin([_hgrn_state(ins[d][1][0, :, cols(j)], ins[d][2][0, :, cols(j)], lb_ref[d, j],
                                          tri_ref[d], st, chunk=chunk, reverse=bool(d))
                              for (d, j), st in zip(chains, old)])
        for (d, j), (_, _, st_new) in zip(chains, parts):
            st_ref[d, j] = st_new
        if not want_out:
            return
        subs = [_sub_block_decay(cum, bool(d)) for (d, j), (_, cum, _) in zip(chains, parts)]
        low = subs[0]
        for s in subs[1:]:
            low = jnp.minimum(low, s)
        low = jnp.min(jnp.min(low, axis=0, keepdims=True), axis=1, keepdims=True)
        in_range = low[0, 0] >= -HG_MAX_EXP2

        def readouts(shared):
            os_ = _round_robin([_hgrn_readout(ins[d][0][0, :, cols(j)], ins[d][2][0, :, cols(j)], kb, cum, sub,
                                              lvl_ref[d, int(shared)], st, chunk=chunk, reverse=bool(d), shared=shared)
                                for (d, j), (kb, cum, _), sub, st in zip(chains, parts, subs, old)])
            for (d, j), o in zip(chains, os_):
                outs[d][0, j] = o.astype(BF16)

        @pl.when(in_range)
        def _():
            readouts(True)

        @pl.when(jnp.logical_not(in_range))
        def _():
            readouts(False)

    @pl.when(step < n_ctx_chunks)
    def _():
        advance(False)

    @pl.when(step >= n_ctx_chunks)
    def _():
        advance(True)


def _hgrn_consts(chunk, reverse):
    t = np.arange(chunk)[:, None]
    s = np.arange(chunk)[None, :]
    x = t ^ s
    bitlen = np.zeros_like(x)
    for b in range(chunk.bit_length()):
        bitlen = np.where(x >> b > 0, b + 1, bitlen)
    valid = (s > t) if reverse else (t > s)
    lvl = np.where(t == s, 0, np.where(valid, bitlen, -1)).astype(np.float32)
    lvl_shared = np.where((lvl >= 0) & (lvl <= HG_SUB_LEVEL), 0, lvl)
    tri = ((s >= t) if reverse else (t >= s)).astype(np.float32)
    return np.concatenate([tri, tri], axis=1), np.stack([lvl, lvl_shared])


HG_GROUP = 8


def _hgrn_scans(p, pf, lb, n_lat):
    bsz, rows, _ = p.shape
    c = HG_CHUNK
    grp = HG_GROUP
    n_chunks = rows // c
    n_lat_c = n_lat // c
    n_ctx_c = n_chunks - n_lat_c
    consts = [_hgrn_consts(c, False), _hgrn_consts(c, True)]
    tri = jnp.asarray(np.stack([consts[0][0], consts[1][0]]), dtype=BF16)
    lvl = jnp.asarray(np.stack([consts[0][1], consts[1][1]]), dtype=BF16)

    def cidx_f(i):
        return jnp.where(i < n_ctx_c, n_lat_c + i, i - n_ctx_c)

    def cidx_b(i):
        return n_chunks - 1 - i

    def col_spec(col, cidx):
        return pl.BlockSpec((1, c, grp * HEAD_DIM), lambda b, h, i: (b, cidx(i), col // (grp * HEAD_DIM) + h))

    def out_spec(cidx):
        return pl.BlockSpec((1, grp, c, HEAD_DIM), lambda b, h, i: (b, h, cidx(jnp.maximum(i, n_ctx_c)), 0))

    o_shape = jax.ShapeDtypeStruct((bsz, HEADS, n_lat, HEAD_DIM), BF16)
    kern = functools.partial(_hgrn_kernel, chunk=c, group=grp, n_ctx_chunks=n_ctx_c)
    return pl.pallas_call(
        kern,
        out_shape=(o_shape, o_shape),
        grid=(bsz, HEADS // grp, n_chunks),
        in_specs=[col_spec(COL_HQ, cidx_f), col_spec(COL_FF, cidx_f), col_spec(COL_I, cidx_f),
                  col_spec(COL_HQ, cidx_b), col_spec(COL_FB, cidx_b), col_spec(COL_I, cidx_b),
                  pl.BlockSpec((2, grp, 1, HEAD_DIM), lambda b, h, i: (0, h, 0, 0)),
                  pl.BlockSpec((2, c, 2 * c), lambda b, h, i: (0, 0, 0)),
                  pl.BlockSpec((2, 2, c, c), lambda b, h, i: (0, 0, 0, 0))],
        out_specs=(out_spec(cidx_f), out_spec(cidx_b)),
        scratch_shapes=[pltpu.VMEM((2, grp, HEAD_DIM, HEAD_DIM), F32)],
        compiler_params=_cparams(("parallel", "parallel", "arbitrary"), VMEM_LIMIT),
        name="hgrn",
    )(p, pf, p, p, pf, p, lb, tri, lvl)


def _merge_kernel(ym_ref, of_ref, ob_ref, go_ref, gm_ref, gh_ref, x_ref, mod_ref, hgg_ref, wout_ref,
                  gffn_ref, wr_ref, br_ref, x1_ref, h2_ref, lg_ref, y_scr):
    b = pl.program_id(0)
    for h in range(HEADS):
        sl = slice(h * HEAD_DIM, (h + 1) * HEAD_DIM)
        o = of_ref[0, h].astype(F32) + ob_ref[0, h].astype(F32)
        ms = jnp.mean(o * o, axis=-1, keepdims=True)
        g = go_ref[0, :, sl].astype(F32)
        yh = o * lax.rsqrt(ms + EPS) * hgg_ref[...] * (g * _sigmoid(g))
        y = (_sigmoid(gm_ref[0, :, sl].astype(F32)) * ym_ref[0, :, sl].astype(F32)
             + _sigmoid(gh_ref[0, :, sl].astype(F32)) * yh)
        y_scr[:, sl] = y.astype(BF16)
    mix = _mm(y_scr[...], wout_ref[...])
    m = mod_ref[pl.ds(b, 1), :]
    x1 = x_ref[0] + m[:, 2 * D_MODEL:3 * D_MODEL] * mix
    x1_ref[0] = x1
    ms = jnp.mean(x1 * x1, axis=-1, keepdims=True)
    h2 = (x1 * lax.rsqrt(ms + EPS) * gffn_ref[...]) * (1.0 + m[:, 4 * D_MODEL:5 * D_MODEL]) + m[:, 3 * D_MODEL:4 * D_MODEL]
    h_hi = h2.astype(BF16)
    u = pltpu.bitcast(h_hi.astype(F32), jnp.uint32)
    half = D_MODEL // 2
    h2_ref[...] = (u[:, 0:half] >> 16) | u[:, half:D_MODEL]
    h_lo = (h2 - h_hi.astype(F32)).astype(BF16)
    w_hi, w_lo = _split2(wr_ref[...])
    lg_ref[...] = _nt(w_hi, h_hi) + _nt(w_lo, h_hi) + _nt(w_hi, h_lo) + br_ref[...]


def _merge(y_mla, o_f, o_b, p, x, mod, hg_g, w_out, g_ffn, w_r_t, b_r):
    bsz, n, d = x.shape
    tm = _tile(n, 256, 128)
    nt = n // tm

    def pcol(col):
        return pl.BlockSpec((1, tm, d), lambda b, t: (b, t, col // d))

    tok = lambda b, t: (b, t, 0)
    const2 = lambda b, t: (0, 0)
    return pl.pallas_call(
        _merge_kernel,
        out_shape=(jax.ShapeDtypeStruct((bsz, n, d), F32),
                   jax.ShapeDtypeStruct((bsz * n, d // 2), jnp.uint32),
                   jax.ShapeDtypeStruct((N_EXPERTS, bsz * n), F32)),
        grid=(bsz, nt),
        in_specs=[pl.BlockSpec((1, tm, d), tok),
                  pl.BlockSpec((1, HEADS, tm, HEAD_DIM), lambda b, t: (b, 0, t, 0)),
                  pl.BlockSpec((1, HEADS, tm, HEAD_DIM), lambda b, t: (b, 0, t, 0)),
                  pcol(COL_GO), pcol(COL_MM), pcol(COL_MH),
                  pl.BlockSpec((1, tm, d), tok),
                  pl.BlockSpec(mod.shape, const2),
                  pl.BlockSpec((1, HEAD_DIM), const2),
                  pl.BlockSpec((d, d), const2),
                  pl.BlockSpec((1, d), const2),
                  pl.BlockSpec((N_EXPERTS, d), const2),
                  pl.BlockSpec((N_EXPERTS, 1), const2)],
        out_specs=(pl.BlockSpec((1, tm, d), tok),
                   pl.BlockSpec((tm, d // 2), lambda b, t: (b * nt + t, 0)),
                   pl.BlockSpec((N_EXPERTS, tm), lambda b, t: (0, b * nt + t))),
        scratch_shapes=[pltpu.VMEM((tm, d), BF16)],
        compiler_params=_cparams(("parallel", "parallel"), VMEM_LIMIT),
        name="merge",
    )(y_mla, o_f, o_b, p, p, p, x, mod, hg_g, w_out, g_ffn, w_r_t, b_r)


def _router_kernel(lg_ref, upper_ref, idx_ref, prob_ref, rank_ref, cnt_ref, run_ref, *, tm):
    @pl.when(pl.program_id(0) == 0)
    def _():
        run_ref[...] = jnp.zeros_like(run_ref)

    l = lg_ref[...]
    eidx = lax.broadcasted_iota(jnp.int32, (N_EXPERTS, tm), 0)
    vals, sel = [], []
    for _ in range(TOP_K):
        m = jnp.max(l, axis=0, keepdims=True)
        first = jnp.min(jnp.where(l == m, eidx, N_EXPERTS), axis=0, keepdims=True)
        vals.append(m)
        sel.append(first)
        l = jnp.where(eidx == first, -jnp.inf, l)
    ex = [jnp.exp(v - vals[0]) for v in vals]
    inv = 1.0 / (ex[0] + ex[1] + ex[2] + ex[3])
    onehot = jnp.zeros((N_EXPERTS, tm), F32)
    for k in range(TOP_K):
        onehot = onehot + jnp.where(eidx == sel[k], 1.0, 0.0)
    before = _mm(onehot.astype(BF16), upper_ref[...]) + run_ref[:, 0:1]
    for k in range(TOP_K):
        idx_ref[k:k + 1, :] = sel[k]
        prob_ref[k:k + 1, :] = ex[k] * inv
        rank_ref[k:k + 1, :] = jnp.sum(jnp.where(eidx == sel[k], before, 0.0), axis=0, keepdims=True).astype(jnp.int32)
    run_ref[...] = run_ref[...] + jnp.sum(onehot, axis=1, keepdims=True)
    cnt_ref[...] = run_ref[...]


def _router(logits_t):
    _, t = logits_t.shape
    tm = _tile(t, 1024, 128)
    upper = jnp.asarray(np.triu(np.ones((tm, tm), np.float32), 1), dtype=BF16)
    kern = functools.partial(_router_kernel, tm=tm)
    tok = pl.BlockSpec((TOP_K, tm), lambda i: (0, i))
    return pl.pallas_call(
        kern,
        out_shape=(jax.ShapeDtypeStruct((TOP_K, t), jnp.int32),
                   jax.ShapeDtypeStruct((TOP_K, t), F32),
                   jax.ShapeDtypeStruct((TOP_K, t), jnp.int32),
                   jax.ShapeDtypeStruct((N_EXPERTS, 128), F32)),
        grid=(t // tm,),
        in_specs=[pl.BlockSpec((N_EXPERTS, tm), lambda i: (0, i)),
                  pl.BlockSpec((tm, tm), lambda i: (0, 0))],
        out_specs=(tok, tok, tok, pl.BlockSpec((N_EXPERTS, 128), lambda i: (0, 0))),
        scratch_shapes=[pltpu.VMEM((N_EXPERTS, 128), F32)],
        compiler_params=_cparams(("arbitrary",), VMEM_LIMIT),
        name="router",
    )(logits_t, upper)


def _dest_kernel(ps_ref, idx_ref, rank_ref, o_ref):
    idx = idx_ref[...]
    dest = rank_ref[...]
    for e in range(N_EXPERTS):
        dest = dest + jnp.where(idx == e, ps_ref[e], 0)
    o_ref[...] = dest


def _dest_rows(pad_start, idx_t, rank_t):
    k, t = idx_t.shape
    tm = _tile(t, 4096, 128)
    tok = pl.BlockSpec((k, tm), lambda i, ps: (0, i))
    return pl.pallas_call(
        _dest_kernel,
        out_shape=jax.ShapeDtypeStruct((k, t), jnp.int32),
        grid_spec=pltpu.PrefetchScalarGridSpec(num_scalar_prefetch=1, grid=(t // tm,), in_specs=[tok, tok], out_specs=tok),
        compiler_params=_cparams(("parallel",), VMEM_LIMIT),
        name="dest_rows",
    )(pad_start, idx_t, rank_t)


def _dispatch_kernel(dest_ref, h2_ref, xs_in_ref, xs_ref, sem, *, tm):
    del xs_in_ref

    def body(r, carry):
        for k in range(TOP_K):
            pltpu.make_async_copy(h2_ref.at[pl.ds(r, 1)], xs_ref.at[pl.ds(dest_ref[0, 0, k * tm + r], 1)], sem).start()
        return carry

    lax.fori_loop(0, tm, body, 0)
    for _ in range(TOP_K):
        pltpu.make_async_copy(h2_ref, xs_ref.at[pl.ds(0, tm)], sem).wait()


def _dispatch(dest_tiles, h2p, n_rows, tm):
    t, w = h2p.shape
    xs0 = jnp.zeros((n_rows, w), h2p.dtype)
    kern = functools.partial(_dispatch_kernel, tm=tm)
    return pl.pallas_call(
        kern,
        out_shape=jax.ShapeDtypeStruct((n_rows, w), h2p.dtype),
        grid=(t // tm,),
        in_specs=[pl.BlockSpec((1, 1, TOP_K * tm), lambda i: (i, 0, 0), memory_space=pltpu.SMEM),
                  pl.BlockSpec((tm, w), lambda i: (i, 0)),
                  pl.BlockSpec(memory_space=pl.ANY)],
        out_specs=pl.BlockSpec(memory_space=pl.ANY),
        scratch_shapes=[pltpu.SemaphoreType.DMA],
        input_output_aliases={2: 0},
        compiler_params=_cparams(("arbitrary",), VMEM_LIMIT),
        name="dispatch",
    )(dest_tiles, h2p, xs0)


GU_GROUP = 256


def _ffn_kernel(be_ref, nu_ref, x_ref, wgu_ref, bgu_ref, wd_ref, bd_ref, perm_ref, y_ref, wgu_s, wd_s):
    i = pl.program_id(0)
    live = i < nu_ref[0]
    new_expert = jnp.logical_or(i == 0, be_ref[i] != be_ref[jnp.maximum(i - 1, 0)])

    @pl.when(jnp.logical_and(live, new_expert))
    def _():
        for g in range(2 * D_EXPERT // GU_GROUP):
            sl = slice(g * GU_GROUP, (g + 1) * GU_GROUP)
            wgu_s[:, sl] = _mm(wgu_ref[0, :, sl].astype(BF16), perm_ref[...]).astype(BF16)
        wd_s[...] = wd_ref[0].astype(BF16)

    @pl.when(live)
    def _():
        u = x_ref[...]
        lo = pltpu.bitcast(u << 16, F32).astype(BF16)
        hi = pltpu.bitcast(u & jnp.uint32(0xFFFF0000), F32).astype(BF16)
        x = jnp.concatenate([lo, hi], axis=1)
        gu = _mm(x, wgu_s[...]) + bgu_ref[0]
        half = GU_GROUP // 2
        n_grp = 2 * D_EXPERT // GU_GROUP
        glu = jnp.concatenate([gu[:, g * GU_GROUP:g * GU_GROUP + half] for g in range(n_grp)], axis=1)
        lin = jnp.concatenate([gu[:, g * GU_GROUP + half:(g + 1) * GU_GROUP] for g in range(n_grp)], axis=1)
        glu = jnp.minimum(glu, SWIGLU_LIMIT)
        lin = jnp.clip(lin, -SWIGLU_LIMIT, SWIGLU_LIMIT)
        act = glu * _sigmoid(SWIGLU_ALPHA * glu) * (lin + 1.0)
        y_ref[...] = _mm(act.astype(BF16), wd_s[...]) + bd_ref[0]

    @pl.when(jnp.logical_not(live))
    def _():
        y_ref[...] = jnp.zeros_like(y_ref)


def _ffn(block_e, n_used, xs, w_gu, b_gu, w_d, b_d):
    n_rows, w = xs.shape
    bm = FFN_ROWS
    nblk = n_rows // bm
    d = w * 2
    half = GU_GROUP // 2
    perm = np.zeros((GU_GROUP, GU_GROUP), np.float32)
    perm[2 * np.arange(half), np.arange(half)] = 1.0
    perm[2 * np.arange(half) + 1, half + np.arange(half)] = 1.0

    def xmap(i, be, nu):
        return (jnp.minimum(i, nu[0] - 1), 0)

    def wmap(i, be, nu):
        return (be[i], 0, 0)

    grid_spec = pltpu.PrefetchScalarGridSpec(
        num_scalar_prefetch=2,
        grid=(nblk,),
        in_specs=[pl.BlockSpec((bm, w), xmap),
                  pl.BlockSpec((1, d, 2 * D_EXPERT), wmap),
                  pl.BlockSpec((1, 1, 2 * D_EXPERT), wmap),
                  pl.BlockSpec((1, D_EXPERT, d), wmap),
                  pl.BlockSpec((1, 1, d), wmap),
                  pl.BlockSpec((GU_GROUP, GU_GROUP), lambda i, be, nu: (0, 0))],
        out_specs=pl.BlockSpec((bm, d), lambda i, be, nu: (i, 0)),
        scratch_shapes=[pltpu.VMEM((d, 2 * D_EXPERT), BF16), pltpu.VMEM((D_EXPERT, d), BF16)],
    )
    return pl.pallas_call(
        _ffn_kernel,
        out_shape=jax.ShapeDtypeStruct((n_rows, d), F32),
        grid_spec=grid_spec,
        compiler_params=_cparams(("arbitrary",), VMEM_LIMIT),
        name="ffn",
    )(block_e, n_used, xs, w_gu, b_gu, w_d, b_d, jnp.asarray(perm, dtype=BF16))


def _combine_kernel(dest_ref, dest_next_ref, y_ref, p_ref, x1_ref, mod_ref, g_ref, o_ref, buf_ref, sem, *, tm, nt):
    b = pl.program_id(0)
    step = b * nt + pl.program_id(1)
    n_steps = pl.num_programs(0) * nt
    slot = step % 2

    def gather(d_ref, s):
        def body(r, carry):
            for k in range(TOP_K):
                pltpu.make_async_copy(y_ref.at[pl.ds(d_ref[0, 0, k * tm + r], 1)],
                                      buf_ref.at[s, k, pl.ds(r, 1)], sem.at[s]).start()
            return carry
        lax.fori_loop(0, tm, body, 0)

    @pl.when(step == 0)
    def _():
        gather(dest_ref, 0)

    @pl.when(step + 1 < n_steps)
    def _():
        gather(dest_next_ref, 1 - slot)

    for k in range(TOP_K):
        pltpu.make_async_copy(y_ref.at[pl.ds(0, tm)], buf_ref.at[slot, k], sem.at[slot]).wait()
    p = p_ref[...]
    acc = p[:, 0:1] * buf_ref[slot, 0]
    for k in range(1, TOP_K):
        acc = acc + p[:, k:k + 1] * buf_ref[slot, k]
    m = mod_ref[pl.ds(b, 1), :]
    xo = x1_ref[0] + m[:, 5 * D_MODEL:6 * D_MODEL] * acc
    ms = jnp.mean(xo * xo, axis=-1, keepdims=True)
    o_ref[0] = xo * lax.rsqrt(ms + EPS) * g_ref[...]


def _combine(dest_tiles, y, probs_t, x1, mod, g_fin, tm):
    bsz, n, d = x1.shape
    nt = n // tm
    kern = functools.partial(_combine_kernel, tm=tm, nt=nt)
    last = bsz * nt - 1
    return pl.pallas_call(
        kern,
        out_shape=jax.ShapeDtypeStruct((bsz, n, d), F32),
        grid=(bsz, nt),
        in_specs=[pl.BlockSpec((1, 1, TOP_K * tm), lambda b, t: (b * nt + t, 0, 0), memory_space=pltpu.SMEM),
                  pl.BlockSpec((1, 1, TOP_K * tm), lambda b, t: (jnp.minimum(b * nt + t + 1, last), 0, 0),
                               memory_space=pltpu.SMEM),
                  pl.BlockSpec(memory_space=pl.ANY),
                  pl.BlockSpec((tm, TOP_K), lambda b, t: (b * nt + t, 0)),
                  pl.BlockSpec((1, tm, d), lambda b, t: (b, t, 0)),
                  pl.BlockSpec(mod.shape, lambda b, t: (0, 0)),
                  pl.BlockSpec((1, d), lambda b, t: (0, 0))],
        out_specs=pl.BlockSpec((1, tm, d), lambda b, t: (b, t, 0)),
        scratch_shapes=[pltpu.VMEM((2, TOP_K, tm, d), F32), pltpu.SemaphoreType.DMA((2,))],
        compiler_params=_cparams(("arbitrary", "arbitrary"), VMEM_LIMIT),
        name="combine",
    )(dest_tiles, dest_tiles, y, probs_t, x1, mod, g_fin)


def _prep_w_in(w_in):
    q_a, kv, hq, ff, fb, hi, go, mm, mh = jnp.split(
        w_in, np.cumsum((MLA_Q_RANK, MLA_KV_RANK + MLA_ROPE) + (D_MODEL,) * 6).tolist(), axis=1)
    c_kv, kr = kv[:, :MLA_KV_RANK], kv[:, MLA_KV_RANK:]
    half = MLA_ROPE // 2
    kr_sw = jnp.concatenate([-kr[:, half:], kr[:, :half]], axis=1)
    w16 = jnp.concatenate([hq, hi, go, mm, mh, q_a, c_kv, kr, kr_sw], axis=1)
    w16 = jnp.pad(w16, ((0, 0), (0, N_PROJ16 - w16.shape[1])))
    return w16.astype(BF16), jnp.concatenate([ff, fb], axis=1).astype(BF16)


def _prep_w_q(w_q_b):
    w = w_q_b.reshape(MLA_Q_RANK, HEADS, MLA_QK)
    w = jnp.pad(w, ((0, 0), (0, 0), (0, MLA_QK_PAD - MLA_QK)))
    return w.reshape(MLA_Q_RANK, HEADS * MLA_QK_PAD).astype(BF16)


def _prep_w_kv(w_kv_b):
    w = w_kv_b.reshape(MLA_KV_RANK, HEADS, 2 * HEAD_DIM)
    k = w[:, :, :HEAD_DIM].reshape(MLA_KV_RANK, HEADS * HEAD_DIM)
    v = w[:, :, HEAD_DIM:].reshape(MLA_KV_RANK, HEADS * HEAD_DIM)
    return jnp.concatenate([k, v], axis=1).astype(BF16)


def _rope_tables(n_lat, n_ctx):
    f32 = np.float32
    rows = n_lat // GRID_W
    row = np.repeat(np.arange(rows), GRID_W).astype(f32)
    col = np.tile(np.arange(GRID_W), rows).astype(f32)
    n_freq = MLA_ROPE // 4
    inv = (f32(ROPE_BASE) ** (-np.arange(n_freq, dtype=f32) / f32(n_freq))).astype(f32)
    ang = np.concatenate([row[:, None] * inv, col[:, None] * inv], axis=-1).astype(f32)
    cos, sin = np.cos(ang).astype(f32), np.sin(ang).astype(f32)
    z64 = np.zeros((n_lat, 64), f32)
    cos_q = np.concatenate([cos, cos, z64], axis=1)
    sin_q = np.concatenate([-sin, sin, z64], axis=1)
    cos_k = np.concatenate([cos, cos, z64], axis=1)
    sin_k = np.concatenate([sin, sin, z64], axis=1)
    ctx_cos = np.concatenate([np.ones((n_ctx, 64), f32), np.zeros((n_ctx, 64), f32)], axis=1)
    cos_k = np.concatenate([cos_k, ctx_cos], axis=0)
    sin_k = np.concatenate([sin_k, np.zeros((n_ctx, 128), f32)], axis=0)
    return jnp.asarray(cos_q), jnp.asarray(sin_q), jnp.asarray(cos_k), jnp.asarray(sin_k)


def kernel(x, c, ctx, c_ctx, w_mod, b_mod, norm_mix_g, w_in, mla_q_norm_g, w_q_b, mla_kv_norm_g, w_kv_b,
           hg_lb_logits, hg_norm_g, w_out, norm_ffn_g, w_router, b_router, w_gate_up, b_gate_up, w_down,
           b_down, final_norm_g):
    bsz, n_lat, d = x.shape
    n_ctx = ctx.shape[1]
    assert d == D_MODEL and w_mod.shape[0] == 1
    assert n_lat % HG_CHUNK == 0 and n_ctx % HG_CHUNK == 0 and n_lat % GRID_W == 0
    t_tok = bsz * n_lat

    mod_rows = -(-(bsz + 1) // 8) * 8
    cc = jnp.concatenate([c, c_ctx[None, :], jnp.zeros((mod_rows - bsz - 1, d), F32)], axis=0)
    mod = _modulation(cc, w_mod[0], b_mod[0][None, :])

    xa = jnp.concatenate([x, ctx], axis=1)
    w16, w32 = _prep_w_in(w_in[0])
    p, pf = _inproj(xa, mod, norm_mix_g[0][None, :], w16, w32, n_lat)

    cos_q, sin_q, cos_k, sin_k = _rope_tables(n_lat, n_ctx)
    q = _mla_q(p, mla_q_norm_g[0][None, :], _prep_w_q(w_q_b[0]), cos_q, sin_q, n_lat)
    k, v = _mla_kv(p, mla_kv_norm_g[0][None, :], _prep_w_kv(w_kv_b[0]), cos_k, sin_k)
    y_mla = _attention(q, k, v)

    lb = jax.nn.softmax(hg_lb_logits.astype(F32), axis=1)[:, 0, :]
    o_f, o_b = _hgrn_scans(p, pf, lb.reshape(2, HEADS, 1, HEAD_DIM), n_lat)

    x1, h2p, logits_t = _merge(y_mla, o_f, o_b, p, x, mod, hg_norm_g[0][None, :], w_out[0].astype(BF16),
                               norm_ffn_g[0][None, :], w_router[0].T, b_router[0][:, None])

    idx_t, prob_t, rank_t, cnt = _router(logits_t)

    counts = cnt[:, 0].astype(jnp.int32)
    padded = (counts + FFN_ROWS - 1) // FFN_ROWS * FFN_ROWS
    pad_end = jnp.cumsum(padded)
    pad_start = pad_end - padded
    dest = _dest_rows(pad_start.astype(jnp.int32), idx_t, rank_t)
    n_rows = -(-(t_tok * TOP_K + N_EXPERTS * (FFN_ROWS - 1)) // FFN_ROWS) * FFN_ROWS
    nblk = n_rows // FFN_ROWS
    block_start = jnp.arange(nblk, dtype=jnp.int32) * FFN_ROWS
    block_e = jnp.minimum(jnp.sum(pad_end[None, :] <= block_start[:, None], axis=1), N_EXPERTS - 1).astype(jnp.int32)
    n_used = (pad_end[-1:] // FFN_ROWS).astype(jnp.int32)

    tm_c = _tile(n_lat, 256)
    dest_tiles = dest.reshape(TOP_K, t_tok // tm_c, tm_c).transpose(1, 0, 2).reshape(t_tok // tm_c, 1, TOP_K * tm_c)

    xs = _dispatch(dest_tiles, h2p, n_rows, tm_c)

    n_grp = 2 * D_EXPERT // GU_GROUP
    b_gu = b_gate_up[0].reshape(N_EXPERTS, n_grp, GU_GROUP // 2, 2).transpose(0, 1, 3, 2).reshape(N_EXPERTS, 1, 2 * D_EXPERT)
    y = _ffn(block_e, n_used, xs, w_gate_up[0], b_gu, w_down[0], b_down[0][:, None, :])

    return _combine(dest_tiles, y, prob_t.T, x1, mod, final_norm_g[None, :], tm_c)
```

```python
import functools

import jax
import jax.numpy as jnp
import numpy as np
from jax import lax
from jax.experimental import pallas as pl
from jax.experimental.pallas import tpu as pltpu

F32 = jnp.float32
BF16 = jnp.bfloat16

D_MODEL = 1024
EPS = 1e-6
LOG2_E = 1.4426950408889634
N_MOD = 6
GRID_W = 64
ROPE_BASE = 10000.0

HEADS = 8
HEAD_DIM = 128
MLA_ROPE = 64
MLA_QK = HEAD_DIM + MLA_ROPE
MLA_QK_PAD = 256
MLA_Q_RANK = 768
MLA_KV_RANK = 256

N_EXPERTS = 32
TOP_K = 4
D_EXPERT = 1024
SWIGLU_LIMIT = 7.0
SWIGLU_ALPHA = 1.702

HG_CHUNK = 128
FFN_ROWS = 512

COL_HQ, COL_I, COL_GO, COL_MM, COL_MH = (i * D_MODEL for i in range(5))
COL_QA = 5 * D_MODEL
COL_CKV = COL_QA + MLA_Q_RANK
COL_KR = COL_CKV + MLA_KV_RANK
PROJ16_TN = 1280
N_PROJ16 = -(-(COL_KR + 2 * MLA_ROPE) // PROJ16_TN) * PROJ16_TN
COL_FF, COL_FB = 0, D_MODEL
N_PROJ32 = 2 * D_MODEL
PROJ32_TN = 1024

VMEM_LIMIT = 56 * 1024 * 1024


def _cparams(sem, vmem=None):
    return pltpu.CompilerParams(dimension_semantics=sem, vmem_limit_bytes=vmem)


def _tile(n, pref, mult=8):
    best = None
    for t in range(mult, min(n, pref) + 1, mult):
        if n % t == 0:
            best = t
    assert best is not None, (n, pref, mult)
    return best


def _nt(a, b):
    return lax.dot_general(a, b, (((1,), (1,)), ((), ())), preferred_element_type=F32)


def _mm(a, b):
    return jnp.dot(a, b, preferred_element_type=F32)


def _split2(a):
    hi = a.astype(BF16)
    lo = (a - hi.astype(F32)).astype(BF16)
    return hi, lo


def _sigmoid(x):
    return 1.0 / (1.0 + jnp.exp(-x))


def _mod_kernel(c_ref, w_ref, b_ref, o_ref):
    c = c_ref[...]
    s = c * _sigmoid(c)
    s_hi, s_lo = _split2(s)
    w_hi, w_lo = _split2(w_ref[...])
    o_ref[...] = _mm(s_hi, w_hi) + _mm(s_lo, w_hi) + _mm(s_hi, w_lo) + b_ref[...]


def _modulation(cc, w_mod, b_mod):
    r, d = cc.shape
    n = w_mod.shape[1]
    tn = _tile(n, 1536, 128)
    return pl.pallas_call(
        _mod_kernel,
        out_shape=jax.ShapeDtypeStruct((r, n), F32),
        grid=(n // tn,),
        in_specs=[pl.BlockSpec((r, d), lambda j: (0, 0)),
                  pl.BlockSpec((d, tn), lambda j: (0, j)),
                  pl.BlockSpec((1, tn), lambda j: (0, j))],
        out_specs=pl.BlockSpec((r, tn), lambda j: (0, j)),
        compiler_params=_cparams(("parallel",), VMEM_LIMIT),
        name="mod",
    )(cc, w_mod, b_mod)


def _inproj_kernel(x_ref, mod_ref, g_ref, w16_ref, w32_ref, o16_ref, o32_ref, hn_ref, *, n_lat, tm, ctx_row, n16):
    b = pl.program_id(0)
    t = pl.program_id(1)
    j = pl.program_id(2)

    @pl.when(j == 0)
    def _():
        x = x_ref[0]
        ms = jnp.mean(x * x, axis=-1, keepdims=True)
        y = x * lax.rsqrt(ms + EPS) * g_ref[...]
        row = t * tm + lax.broadcasted_iota(jnp.int32, (tm, 1), 0)
        is_ctx = row >= n_lat
        m_lat = mod_ref[pl.ds(b, 1), :]
        m_ctx = mod_ref[pl.ds(ctx_row, 1), :]
        shift = jnp.where(is_ctx, m_ctx[:, 0:D_MODEL], m_lat[:, 0:D_MODEL])
        scale = jnp.where(is_ctx, m_ctx[:, D_MODEL:2 * D_MODEL], m_lat[:, D_MODEL:2 * D_MODEL])
        hn_ref[...] = (y * (1.0 + scale) + shift).astype(BF16)

    @pl.when(j < n16)
    def _():
        o16_ref[0] = _mm(hn_ref[...], w16_ref[...]).astype(BF16)

    @pl.when(j >= n16)
    def _():
        o32_ref[0] = _mm(hn_ref[...], w32_ref[...])


def _inproj(xa, mod, g, w16, w32, n_lat):
    bsz, rows, d = xa.shape
    tm = _tile(rows, 1152)
    n16 = N_PROJ16 // PROJ16_TN
    n32 = N_PROJ32 // PROJ32_TN

    def j16(j):
        return jnp.minimum(j, n16 - 1)

    def j32(j):
        return jnp.maximum(j - n16, 0)

    kern = functools.partial(_inproj_kernel, n_lat=n_lat, tm=tm, ctx_row=bsz, n16=n16)
    return pl.pallas_call(
        kern,
        out_shape=(jax.ShapeDtypeStruct((bsz, rows, N_PROJ16), BF16),
                   jax.ShapeDtypeStruct((bsz, rows, N_PROJ32), F32)),
        grid=(bsz, rows // tm, n16 + n32),
        in_specs=[pl.BlockSpec((1, tm, d), lambda b, t, j: (b, t, 0)),
                  pl.BlockSpec(mod.shape, lambda b, t, j: (0, 0)),
                  pl.BlockSpec((1, d), lambda b, t, j: (0, 0)),
                  pl.BlockSpec((d, PROJ16_TN), lambda b, t, j: (0, j16(j))),
                  pl.BlockSpec((d, PROJ32_TN), lambda b, t, j: (0, j32(j)))],
        out_specs=(pl.BlockSpec((1, tm, PROJ16_TN), lambda b, t, j: (b, t, j16(j))),
                   pl.BlockSpec((1, tm, PROJ32_TN), lambda b, t, j: (b, t, j32(j)))),
        scratch_shapes=[pltpu.VMEM((tm, d), BF16)],
        compiler_params=_cparams(("parallel", "parallel", "arbitrary"), VMEM_LIMIT),
        name="inproj",
    )(xa, mod, g, w16, w32)


def _mlaq_kernel(a0_ref, a1_ref, a2_ref, g_ref, w_ref, cos_ref, sin_ref, o_ref, *, tm):
    parts = [a0_ref[0].astype(F32), a1_ref[0].astype(F32), a2_ref[0].astype(F32)]
    ss = sum(jnp.sum(p * p, axis=-1, keepdims=True) for p in parts)
    r = lax.rsqrt(ss * (1.0 / MLA_Q_RANK) + EPS)
    acc = None
    for j, p in enumerate(parts):
        hj = (p * r * g_ref[:, j * 256:(j + 1) * 256]).astype(BF16)
        d = _mm(hj, w_ref[j * 256:(j + 1) * 256, :])
        acc = d if acc is None else acc + d
    lane = lax.broadcasted_iota(jnp.int32, (tm, HEAD_DIM), 1)
    scale = MLA_QK ** -0.5 * LOG2_E
    cos = cos_ref[...]
    sin = sin_ref[...]
    for h in range(HEADS):
        nope = acc[:, h * MLA_QK_PAD:h * MLA_QK_PAD + HEAD_DIM]
        rp = acc[:, h * MLA_QK_PAD + HEAD_DIM:(h + 1) * MLA_QK_PAD]
        swapped = jnp.where(lane < 32, pltpu.roll(rp, 96, 1), pltpu.roll(rp, 32, 1))
        rot = rp * cos + swapped * sin
        o_ref[0, h, :, 0:HEAD_DIM] = (nope * scale).astype(BF16)
        o_ref[0, h, :, HEAD_DIM:MLA_QK_PAD] = (rot * scale).astype(BF16)


def _mla_q(p, g, w, cos_q, sin_q, n_lat):
    bsz = p.shape[0]
    tm = _tile(n_lat, 512)
    cb = COL_QA // 256
    kern = functools.partial(_mlaq_kernel, tm=tm)
    return pl.pallas_call(
        kern,
        out_shape=jax.ShapeDtypeStruct((bsz, HEADS, n_lat, MLA_QK_PAD), BF16),
        grid=(bsz, n_lat // tm),
        in_specs=[pl.BlockSpec((1, tm, 256), lambda b, t: (b, t, cb)),
                  pl.BlockSpec((1, tm, 256), lambda b, t: (b, t, cb + 1)),
                  pl.BlockSpec((1, tm, 256), lambda b, t: (b, t, cb + 2)),
                  pl.BlockSpec((1, MLA_Q_RANK), lambda b, t: (0, 0)),
                  pl.BlockSpec(w.shape, lambda b, t: (0, 0)),
                  pl.BlockSpec((tm, HEAD_DIM), lambda b, t: (t, 0)),
                  pl.BlockSpec((tm, HEAD_DIM), lambda b, t: (t, 0))],
        out_specs=pl.BlockSpec((1, HEADS, tm, MLA_QK_PAD), lambda b, t: (b, 0, t, 0)),
        compiler_params=_cparams(("parallel", "parallel"), VMEM_LIMIT),
        name="mla_q",
    )(p, p, p, g, w, cos_q, sin_q)


def _mlakv_kernel(c_ref, kr_ref, g_ref, w_ref, cos_ref, sin_ref, k_ref, v_ref):
    c = c_ref[0].astype(F32)
    ms = jnp.mean(c * c, axis=-1, keepdims=True)
    hn = (c * lax.rsqrt(ms + EPS) * g_ref[...]).astype(BF16)
    kv = _mm(hn, w_ref[...])
    grp = kr_ref[0].astype(F32)
    rot = (grp * cos_ref[...] + pltpu.roll(grp, 64, 1) * sin_ref[...]).astype(BF16)
    for h in range(HEADS):
        k_ref[0, h, :, 0:HEAD_DIM] = kv[:, h * HEAD_DIM:(h + 1) * HEAD_DIM].astype(BF16)
        k_ref[0, h, :, HEAD_DIM:MLA_QK_PAD] = rot
        v_ref[0, h] = kv[:, D_MODEL + h * HEAD_DIM:D_MODEL + (h + 1) * HEAD_DIM].astype(BF16)


def _mla_kv(p, g, w, cos_k, sin_k):
    bsz, rows, _ = p.shape
    tm = _tile(rows, 768)
    return pl.pallas_call(
        _mlakv_kernel,
        out_shape=(jax.ShapeDtypeStruct((bsz, HEADS, rows, MLA_QK_PAD), BF16),
                   jax.ShapeDtypeStruct((bsz, HEADS, rows, HEAD_DIM), BF16)),
        grid=(bsz, rows // tm),
        in_specs=[pl.BlockSpec((1, tm, MLA_KV_RANK), lambda b, t: (b, t, COL_CKV // MLA_KV_RANK)),
                  pl.BlockSpec((1, tm, 128), lambda b, t: (b, t, COL_KR // 128)),
                  pl.BlockSpec((1, MLA_KV_RANK), lambda b, t: (0, 0)),
                  pl.BlockSpec(w.shape, lambda b, t: (0, 0)),
                  pl.BlockSpec((tm, 128), lambda b, t: (t, 0)),
                  pl.BlockSpec((tm, 128), lambda b, t: (t, 0))],
        out_specs=(pl.BlockSpec((1, HEADS, tm, MLA_QK_PAD), lambda b, t: (b, 0, t, 0)),
                   pl.BlockSpec((1, HEADS, tm, HEAD_DIM), lambda b, t: (b, 0, t, 0))),
        compiler_params=_cparams(("parallel", "parallel"), VMEM_LIMIT),
        name="mla_kv",
    )(p, p, g, w, cos_k, sin_k)


def _attn_kernel(q_ref, k_ref, v_ref, o_ref, *, tq, sub):
    k = k_ref[0, 0]
    v = v_ref[0, 0]
    v1 = jnp.concatenate([v, jnp.ones_like(v)], axis=1)

    def sub_tile(r):
        rows = slice(r * sub, (r + 1) * sub)
        s = _nt(q_ref[0, 0, rows, :], k)
        yield
        m = jnp.max(s, axis=-1, keepdims=True)
        p = jnp.exp2(s - m).astype(BF16)
        yield
        ol = _mm(p, v1)
        o_ref[0, rows, :] = (ol[:, 0:HEAD_DIM] * (1.0 / ol[:, HEAD_DIM:HEAD_DIM + 1])).astype(BF16)

    _round_robin([sub_tile(r) for r in range(tq // sub)])


def _attention(q, k, v):
    bsz, _, n, _ = q.shape
    m = k.shape[2]
    tq = _tile(n, 2048)
    sub = _tile(tq, 512)
    kern = functools.partial(_attn_kernel, tq=tq, sub=sub)
    return pl.pallas_call(
        kern,
        out_shape=jax.ShapeDtypeStruct((bsz, n, HEADS * HEAD_DIM), BF16),
        grid=(bsz, HEADS, n // tq),
        in_specs=[pl.BlockSpec((1, 1, tq, MLA_QK_PAD), lambda b, h, t: (b, h, t, 0)),
                  pl.BlockSpec((1, 1, m, MLA_QK_PAD), lambda b, h, t: (b, h, 0, 0)),
                  pl.BlockSpec((1, 1, m, HEAD_DIM), lambda b, h, t: (b, h, 0, 0))],
        out_specs=pl.BlockSpec((1, tq, HEAD_DIM), lambda b, h, t: (b, t, h)),
        compiler_params=_cparams(("parallel", "parallel", "arbitrary"), VMEM_LIMIT),
        name="attention",
    )(q, k, v)


def _level_ref(cum, blk, reverse):
    c = cum.shape[0]
    half = blk // 2
    r = half if reverse else half - 1
    if blk >= 8:
        x = cum.reshape(c // blk, blk, HEAD_DIM)
        e = jnp.broadcast_to(x[:, r:r + 1, :], x.shape)
        return e.reshape(c, HEAD_DIM)
    x = cum.reshape(c // 8, 8, HEAD_DIM)
    sub = lax.broadcasted_iota(jnp.int32, x.shape, 1)
    e = None
    for jb in range(8 // blk):
        cand = jnp.broadcast_to(x[:, jb * blk + r:jb * blk + r + 1, :], x.shape)
        e = cand if e is None else jnp.where(sub >= jb * blk, cand, e)
    return e.reshape(c, HEAD_DIM)


HG_SUB = 32
HG_SUB_LEVEL = HG_SUB.bit_length() - 1
HG_MAX_EXP2 = 100.0


def _hgrn_state(z, v, lb, tri, st, *, chunk, reverse):
    f = lb + (1.0 - lb) * _sigmoid(z)
    kb = (1.0 - f).astype(BF16)
    g = jnp.log(f) * LOG2_E

    g_hi, g_lo = _split2(g)
    yield
    cum = _mm(tri, jnp.concatenate([g_hi, g_lo], axis=0))
    yield
    last = 0 if reverse else chunk - 1
    tot = cum[last:last + 1, :]
    kt = kb * jnp.exp2(tot - cum).astype(BF16)
    vt = v.astype(F32).T.astype(BF16)
    yield
    st_new = st * jnp.exp2(tot) + _mm(vt, kt)
    return kb, cum, st_new


def _sub_block_decay(cum, reverse):
    c = cum.shape[0]
    x = cum.reshape(c // HG_SUB, HG_SUB, HEAD_DIM)
    zero = jnp.zeros((1, 1, HEAD_DIM), F32)
    if reverse:
        edge = jnp.concatenate([x[1:, 0:1, :], zero], axis=0)
    else:
        edge = jnp.concatenate([zero, x[:-1, HG_SUB - 1:HG_SUB, :]], axis=0)
    return (x - edge).reshape(c, HEAD_DIM)


def _hgrn_readout(q, v, kb, cum, sub, lvl, st, *, chunk, reverse, shared):
    q = q.astype(F32)
    qb = (q * _sigmoid(q) * (HEAD_DIM ** -0.5)).astype(BF16)
    yield
    if shared:
        att = jnp.where(lvl == 0, _nt(qb * jnp.exp2(sub).astype(BF16), kb * jnp.exp2(-sub).astype(BF16)).astype(BF16),
                        jnp.zeros((), BF16))
        first = HG_SUB_LEVEL + 1
    else:
        att = jnp.where(lvl == 0, _nt(qb, kb).astype(BF16), jnp.zeros((), BF16))
        first = 1
    for lv in range(first, chunk.bit_length()):
        zrel = cum - _level_ref(cum, 1 << lv, reverse)
        neg_abs = pltpu.bitcast(pltpu.bitcast(zrel, jnp.uint32) | jnp.uint32(0x80000000), F32)
        e = jnp.exp2(neg_abs).astype(BF16)
        yield
        att = jnp.where(lvl == lv, _nt(qb * e, kb * e).astype(BF16), att)
        yield
    return _nt(qb * jnp.exp2(cum).astype(BF16), st.astype(BF16)) + _mm(att, v.astype(BF16))


def _round_robin(gens):
    out = [None] * len(gens)
    active = list(range(len(gens)))
    while active:
        for i in list(active):
            try:
                next(gens[i])
            except StopIteration as e:
                out[i] = e.value
                active.remove(i)
    return out


def _hgrn_kernel(qf_ref, zf_ref, if_ref, qb_ref, zb_ref, ib_ref, lb_ref, tri_ref, lvl_ref, of_ref, ob_ref, st_ref,
                 *, chunk, group, n_ctx_chunks):
    step = pl.program_id(2)

    @pl.when(step == 0)
    def _():
        st_ref[...] = jnp.zeros_like(st_ref)

    ins = ((qf_ref, zf_ref, if_ref), (qb_ref, zb_ref, ib_ref))
    outs = (of_ref, ob_ref)
    chains = [(d, j) for j in range(group) for d in range(2)]

    def cols(j):
        return slice(j * HEAD_DIM, (j + 1) * HEAD_DIM)

    def advance(want_out):
        old = [st_ref[d, j] for d, j in chains]
        parts = _round_robin([_hgrn_state(ins[d][1][0, :, cols(j)], ins[d][2][0, :, cols(j)], lb_ref[d, j],
                                          tri_ref[d], st, chunk=chunk, reverse=bool(d))
                              for (d, j), st in zip(chains, old)])
        for (d, j), (_, _, st_new) in zip(chains, parts):
            st_ref[d, j] = st_new
        if not want_out:
            return
        subs = [_sub_block_decay(cum, bool(d)) for (d, j), (_, cum, _) in zip(chains, parts)]
        low = subs[0]
        for s in subs[1:]:
            low = jnp.minimum(low, s)
        low = jnp.min(jnp.min(low, axis=0, keepdims=True), axis=1, keepdims=True)
        in_range = low[0, 0] >= -HG_MAX_EXP2

        def readouts(shared):
            os_ = _round_robin([_hgrn_readout(ins[d][0][0, :, cols(j)], ins[d][2][0, :, cols(j)], kb, cum, sub,
                                              lvl_ref[d, int(shared)], st, chunk=chunk, reverse=bool(d), shared=shared)
                                for (d, j), (kb, cum, _), sub, st in zip(chains, parts, subs, old)])
            for (d, j), o in zip(chains, os_):
                outs[d][0, j] = o.astype(BF16)

        @pl.when(in_range)
        def _():
            readouts(True)

        @pl.when(jnp.logical_not(in_range))
        def _():
            readouts(False)

    @pl.when(step < n_ctx_chunks)
    def _():
        advance(False)

    @pl.when(step >= n_ctx_chunks)
    def _():
        advance(True)


def _hgrn_consts(chunk, reverse):
    t = np.arange(chunk)[:, None]
    s = np.arange(chunk)[None, :]
    x = t ^ s
    bitlen = np.zeros_like(x)
    for b in range(chunk.bit_length()):
        bitlen = np.where(x >> b > 0, b + 1, bitlen)
    valid = (s > t) if reverse else (t > s)
    lvl = np.where(t == s, 0, np.where(valid, bitlen, -1)).astype(np.float32)
    lvl_shared = np.where((lvl >= 0) & (lvl <= HG_SUB_LEVEL), 0, lvl)
    tri = ((s >= t) if reverse else (t >= s)).astype(np.float32)
    return np.concatenate([tri, tri], axis=1), np.stack([lvl, lvl_shared])


HG_GROUP = 8


def _hgrn_scans(p, pf, lb, n_lat):
    bsz, rows, _ = p.shape
    c = HG_CHUNK
    grp = HG_GROUP
    n_chunks = rows // c
    n_lat_c = n_lat // c
    n_ctx_c = n_chunks - n_lat_c
    consts = [_hgrn_consts(c, False), _hgrn_consts(c, True)]
    tri = jnp.asarray(np.stack([consts[0][0], consts[1][0]]), dtype=BF16)
    lvl = jnp.asarray(np.stack([consts[0][1], consts[1][1]]), dtype=BF16)

    def cidx_f(i):
        return jnp.where(i < n_ctx_c, n_lat_c + i, i - n_ctx_c)

    def cidx_b(i):
        return n_chunks - 1 - i

    def col_spec(col, cidx):
        return pl.BlockSpec((1, c, grp * HEAD_DIM), lambda b, h, i: (b, cidx(i), col // (grp * HEAD_DIM) + h))

    def out_spec(cidx):
        return pl.BlockSpec((1, grp, c, HEAD_DIM), lambda b, h, i: (b, h, cidx(jnp.maximum(i, n_ctx_c)), 0))

    o_shape = jax.ShapeDtypeStruct((bsz, HEADS, n_lat, HEAD_DIM), BF16)
    kern = functools.partial(_hgrn_kernel, chunk=c, group=grp, n_ctx_chunks=n_ctx_c)
    return pl.pallas_call(
        kern,
        out_shape=(o_shape, o_shape),
        grid=(bsz, HEADS // grp, n_chunks),
        in_specs=[col_spec(COL_HQ, cidx_f), col_spec(COL_FF, cidx_f), col_spec(COL_I, cidx_f),
                  col_spec(COL_HQ, cidx_b), col_spec(COL_FB, cidx_b), col_spec(COL_I, cidx_b),
                  pl.BlockSpec((2, grp, 1, HEAD_DIM), lambda b, h, i: (0, h, 0, 0)),
                  pl.BlockSpec((2, c, 2 * c), lambda b, h, i: (0, 0, 0)),
                  pl.BlockSpec((2, 2, c, c), lambda b, h, i: (0, 0, 0, 0))],
        out_specs=(out_spec(cidx_f), out_spec(cidx_b)),
        scratch_shapes=[pltpu.VMEM((2, grp, HEAD_DIM, HEAD_DIM), F32)],
        compiler_params=_cparams(("parallel", "parallel", "arbitrary"), VMEM_LIMIT),
        name="hgrn",
    )(p, pf, p, p, pf, p, lb, tri, lvl)


def _merge_kernel(ym_ref, of_ref, ob_ref, go_ref, gm_ref, gh_ref, x_ref, mod_ref, hgg_ref, wout_ref,
                  gffn_ref, wr_ref, br_ref, x1_ref, h2_ref, lg_ref, y_scr):
    b = pl.program_id(0)
    for h in range(HEADS):
        sl = slice(h * HEAD_DIM, (h + 1) * HEAD_DIM)
        o = of_ref[0, h].astype(F32) + ob_ref[0, h].astype(F32)
        ms = jnp.mean(o * o, axis=-1, keepdims=True)
        g = go_ref[0, :, sl].astype(F32)
        yh = o * lax.rsqrt(ms + EPS) * hgg_ref[...] * (g * _sigmoid(g))
        y = (_sigmoid(gm_ref[0, :, sl].astype(F32)) * ym_ref[0, :, sl].astype(F32)
             + _sigmoid(gh_ref[0, :, sl].astype(F32)) * yh)
        y_scr[:, sl] = y.astype(BF16)
    mix = _mm(y_scr[...], wout_ref[...])
    m = mod_ref[pl.ds(b, 1), :]
    x1 = x_ref[0] + m[:, 2 * D_MODEL:3 * D_MODEL] * mix
    x1_ref[0] = x1
    ms = jnp.mean(x1 * x1, axis=-1, keepdims=True)
    h2 = (x1 * lax.rsqrt(ms + EPS) * gffn_ref[...]) * (1.0 + m[:, 4 * D_MODEL:5 * D_MODEL]) + m[:, 3 * D_MODEL:4 * D_MODEL]
    h_hi = h2.astype(BF16)
    u = pltpu.bitcast(h_hi.astype(F32), jnp.uint32)
    half = D_MODEL // 2
    h2_ref[...] = (u[:, 0:half] >> 16) | u[:, half:D_MODEL]
    h_lo = (h2 - h_hi.astype(F32)).astype(BF16)
    w_hi, w_lo = _split2(wr_ref[...])
    lg_ref[...] = _nt(w_hi, h_hi) + _nt(w_lo, h_hi) + _nt(w_hi, h_lo) + br_ref[...]


def _merge(y_mla, o_f, o_b, p, x, mod, hg_g, w_out, g_ffn, w_r_t, b_r):
    bsz, n, d = x.shape
    tm = _tile(n, 256, 128)
    nt = n // tm

    def pcol(col):
        return pl.BlockSpec((1, tm, d), lambda b, t: (b, t, col // d))

    tok = lambda b, t: (b, t, 0)
    const2 = lambda b, t: (0, 0)
    return pl.pallas_call(
        _merge_kernel,
        out_shape=(jax.ShapeDtypeStruct((bsz, n, d), F32),
                   jax.ShapeDtypeStruct((bsz * n, d // 2), jnp.uint32),
                   jax.ShapeDtypeStruct((N_EXPERTS, bsz * n), F32)),
        grid=(bsz, nt),
        in_specs=[pl.BlockSpec((1, tm, d), tok),
                  pl.BlockSpec((1, HEADS, tm, HEAD_DIM), lambda b, t: (b, 0, t, 0)),
                  pl.BlockSpec((1, HEADS, tm, HEAD_DIM), lambda b, t: (b, 0, t, 0)),
                  pcol(COL_GO), pcol(COL_MM), pcol(COL_MH),
                  pl.BlockSpec((1, tm, d), tok),
                  pl.BlockSpec(mod.shape, const2),
                  pl.BlockSpec((1, HEAD_DIM), const2),
                  pl.BlockSpec((d, d), const2),
                  pl.BlockSpec((1, d), const2),
                  pl.BlockSpec((N_EXPERTS, d), const2),
                  pl.BlockSpec((N_EXPERTS, 1), const2)],
        out_specs=(pl.BlockSpec((1, tm, d), tok),
                   pl.BlockSpec((tm, d // 2), lambda b, t: (b * nt + t, 0)),
                   pl.BlockSpec((N_EXPERTS, tm), lambda b, t: (0, b * nt + t))),
        scratch_shapes=[pltpu.VMEM((tm, d), BF16)],
        compiler_params=_cparams(("parallel", "parallel"), VMEM_LIMIT),
        name="merge",
    )(y_mla, o_f, o_b, p, p, p, x, mod, hg_g, w_out, g_ffn, w_r_t, b_r)


def _router_kernel(lg_ref, upper_ref, idx_ref, prob_ref, rank_ref, cnt_ref, run_ref, *, tm):
    @pl.when(pl.program_id(0) == 0)
    def _():
        run_ref[...] = jnp.zeros_like(run_ref)

    l = lg_ref[...]
    eidx = lax.broadcasted_iota(jnp.int32, (N_EXPERTS, tm), 0)
    vals, sel = [], []
    for _ in range(TOP_K):
        m = jnp.max(l, axis=0, keepdims=True)
        first = jnp.min(jnp.where(l == m, eidx, N_EXPERTS), axis=0, keepdims=True)
        vals.append(m)
        sel.append(first)
        l = jnp.where(eidx == first, -jnp.inf, l)
    ex = [jnp.exp(v - vals[0]) for v in vals]
    inv = 1.0 / (ex[0] + ex[1] + ex[2] + ex[3])
    onehot = jnp.zeros((N_EXPERTS, tm), F32)
    for k in range(TOP_K):
        onehot = onehot + jnp.where(eidx == sel[k], 1.0, 0.0)
    before = _mm(onehot.astype(BF16), upper_ref[...]) + run_ref[:, 0:1]
    for k in range(TOP_K):
        idx_ref[k:k + 1, :] = sel[k]
        prob_ref[k:k + 1, :] = ex[k] * inv
        rank_ref[k:k + 1, :] = jnp.sum(jnp.where(eidx == sel[k], before, 0.0), axis=0, keepdims=True).astype(jnp.int32)
    run_ref[...] = run_ref[...] + jnp.sum(onehot, axis=1, keepdims=True)
    cnt_ref[...] = run_ref[...]


def _router(logits_t):
    _, t = logits_t.shape
    tm = _tile(t, 1024, 128)
    upper = jnp.asarray(np.triu(np.ones((tm, tm), np.float32), 1), dtype=BF16)
    kern = functools.partial(_router_kernel, tm=tm)
    tok = pl.BlockSpec((TOP_K, tm), lambda i: (0, i))
    return pl.pallas_call(
        kern,
        out_shape=(jax.ShapeDtypeStruct((TOP_K, t), jnp.int32),
                   jax.ShapeDtypeStruct((TOP_K, t), F32),
                   jax.ShapeDtypeStruct((TOP_K, t), jnp.int32),
                   jax.ShapeDtypeStruct((N_EXPERTS, 128), F32)),
        grid=(t // tm,),
        in_specs=[pl.BlockSpec((N_EXPERTS, tm), lambda i: (0, i)),
                  pl.BlockSpec((tm, tm), lambda i: (0, 0))],
        out_specs=(tok, tok, tok, pl.BlockSpec((N_EXPERTS, 128), lambda i: (0, 0))),
        scratch_shapes=[pltpu.VMEM((N_EXPERTS, 128), F32)],
        compiler_params=_cparams(("arbitrary",), VMEM_LIMIT),
        name="router",
    )(logits_t, upper)


def _dest_kernel(ps_ref, idx_ref, rank_ref, o_ref):
    idx = idx_ref[...]
    dest = rank_ref[...]
    for e in range(N_EXPERTS):
        dest = dest + jnp.where(idx == e, ps_ref[e], 0)
    o_ref[...] = dest


def _dest_rows(pad_start, idx_t, rank_t):
    k, t = idx_t.shape
    tm = _tile(t, 4096, 128)
    tok = pl.BlockSpec((k, tm), lambda i, ps: (0, i))
    return pl.pallas_call(
        _dest_kernel,
        out_shape=jax.ShapeDtypeStruct((k, t), jnp.int32),
        grid_spec=pltpu.PrefetchScalarGridSpec(num_scalar_prefetch=1, grid=(t // tm,), in_specs=[tok, tok], out_specs=tok),
        compiler_params=_cparams(("parallel",), VMEM_LIMIT),
        name="dest_rows",
    )(pad_start, idx_t, rank_t)


def _dispatch_kernel(dest_ref, h2_ref, xs_in_ref, xs_ref, sem, *, tm):
    del xs_in_ref

    def body(r, carry):
        for k in range(TOP_K):
            pltpu.make_async_copy(h2_ref.at[pl.ds(r, 1)], xs_ref.at[pl.ds(dest_ref[0, 0, k * tm + r], 1)], sem).start()
        return carry

    lax.fori_loop(0, tm, body, 0)
    for _ in range(TOP_K):
        pltpu.make_async_copy(h2_ref, xs_ref.at[pl.ds(0, tm)], sem).wait()


def _dispatch(dest_tiles, h2p, n_rows, tm):
    t, w = h2p.shape
    xs0 = jnp.zeros((n_rows, w), h2p.dtype)
    kern = functools.partial(_dispatch_kernel, tm=tm)
    return pl.pallas_call(
        kern,
        out_shape=jax.ShapeDtypeStruct((n_rows, w), h2p.dtype),
        grid=(t // tm,),
        in_specs=[pl.BlockSpec((1, 1, TOP_K * tm), lambda i: (i, 0, 0), memory_space=pltpu.SMEM),
                  pl.BlockSpec((tm, w), lambda i: (i, 0)),
                  pl.BlockSpec(memory_space=pl.ANY)],
        out_specs=pl.BlockSpec(memory_space=pl.ANY),
        scratch_shapes=[pltpu.SemaphoreType.DMA],
        input_output_aliases={2: 0},
        compiler_params=_cparams(("arbitrary",), VMEM_LIMIT),
        name="dispatch",
    )(dest_tiles, h2p, xs0)


GU_GROUP = 256


def _ffn_kernel(be_ref, nu_ref, x_ref, wgu_ref, bgu_ref, wd_ref, bd_ref, perm_ref, y_ref, wgu_s, wd_s):
    i = pl.program_id(0)
    live = i < nu_ref[0]
    new_expert = jnp.logical_or(i == 0, be_ref[i] != be_ref[jnp.maximum(i - 1, 0)])

    @pl.when(jnp.logical_and(live, new_expert))
    def _():
        for g in range(2 * D_EXPERT // GU_GROUP):
            sl = slice(g * GU_GROUP, (g + 1) * GU_GROUP)
            wgu_s[:, sl] = _mm(wgu_ref[0, :, sl].astype(BF16), perm_ref[...]).astype(BF16)
        wd_s[...] = wd_ref[0].astype(BF16)

    @pl.when(live)
    def _():
        u = x_ref[...]
        lo = pltpu.bitcast(u << 16, F32).astype(BF16)
        hi = pltpu.bitcast(u & jnp.uint32(0xFFFF0000), F32).astype(BF16)
        x = jnp.concatenate([lo, hi], axis=1)
        gu = _mm(x, wgu_s[...]) + bgu_ref[0]
        half = GU_GROUP // 2
        n_grp = 2 * D_EXPERT // GU_GROUP
        glu = jnp.concatenate([gu[:, g * GU_GROUP:g * GU_GROUP + half] for g in range(n_grp)], axis=1)
        lin = jnp.concatenate([gu[:, g * GU_GROUP + half:(g + 1) * GU_GROUP] for g in range(n_grp)], axis=1)
        glu = jnp.minimum(glu, SWIGLU_LIMIT)
        lin = jnp.clip(lin, -SWIGLU_LIMIT, SWIGLU_LIMIT)
        act = glu * _sigmoid(SWIGLU_ALPHA * glu) * (lin + 1.0)
        y_ref[...] = _mm(act.astype(BF16), wd_s[...]) + bd_ref[0]

    @pl.when(jnp.logical_not(live))
    def _():
        y_ref[...] = jnp.zeros_like(y_ref)


def _ffn(block_e, n_used, xs, w_gu, b_gu, w_d, b_d):
    n_rows, w = xs.shape
    bm = FFN_ROWS
    nblk = n_rows // bm
    d = w * 2
    half = GU_GROUP // 2
    perm = np.zeros((GU_GROUP, GU_GROUP), np.float32)
    perm[2 * np.arange(half), np.arange(half)] = 1.0
    perm[2 * np.arange(half) + 1, half + np.arange(half)] = 1.0

    def xmap(i, be, nu):
        return (jnp.minimum(i, nu[0] - 1), 0)

    def wmap(i, be, nu):
        return (be[i], 0, 0)

    grid_spec = pltpu.PrefetchScalarGridSpec(
        num_scalar_prefetch=2,
        grid=(nblk,),
        in_specs=[pl.BlockSpec((bm, w), xmap),
                  pl.BlockSpec((1, d, 2 * D_EXPERT), wmap),
                  pl.BlockSpec((1, 1, 2 * D_EXPERT), wmap),
                  pl.BlockSpec((1, D_EXPERT, d), wmap),
                  pl.BlockSpec((1, 1, d), wmap),
                  pl.BlockSpec((GU_GROUP, GU_GROUP), lambda i, be, nu: (0, 0))],
        out_specs=pl.BlockSpec((bm, d), lambda i, be, nu: (i, 0)),
        scratch_shapes=[pltpu.VMEM((d, 2 * D_EXPERT), BF16), pltpu.VMEM((D_EXPERT, d), BF16)],
    )
    return pl.pallas_call(
        _ffn_kernel,
        out_shape=jax.ShapeDtypeStruct((n_rows, d), F32),
        grid_spec=grid_spec,
        compiler_params=_cparams(("arbitrary",), VMEM_LIMIT),
        name="ffn",
    )(block_e, n_used, xs, w_gu, b_gu, w_d, b_d, jnp.asarray(perm, dtype=BF16))


def _combine_kernel(dest_ref, dest_next_ref, y_ref, p_ref, x1_ref, mod_ref, g_ref, o_ref, buf_ref, sem, *, tm, nt):
    b = pl.program_id(0)
    step = b * nt + pl.program_id(1)
    n_steps = pl.num_programs(0) * nt
    slot = step % 2

    def gather(d_ref, s):
        def body(r, carry):
            for k in range(TOP_K):
                pltpu.make_async_copy(y_ref.at[pl.ds(d_ref[0, 0, k * tm + r], 1)],
                                      buf_ref.at[s, k, pl.ds(r, 1)], sem.at[s]).start()
            return carry
        lax.fori_loop(0, tm, body, 0)

    @pl.when(step == 0)
    def _():
        gather(dest_ref, 0)

    @pl.when(step + 1 < n_steps)
    def _():
        gather(dest_next_ref, 1 - slot)

    for k in range(TOP_K):
        pltpu.make_async_copy(y_ref.at[pl.ds(0, tm)], buf_ref.at[slot, k], sem.at[slot]).wait()
    p = p_ref[...]
    acc = p[:, 0:1] * buf_ref[slot, 0]
    for k in range(1, TOP_K):
        acc = acc + p[:, k:k + 1] * buf_ref[slot, k]
    m = mod_ref[pl.ds(b, 1), :]
    xo = x1_ref[0] + m[:, 5 * D_MODEL:6 * D_MODEL] * acc
    ms = jnp.mean(xo * xo, axis=-1, keepdims=True)
    o_ref[0] = xo * lax.rsqrt(ms + EPS) * g_ref[...]


def _combine(dest_tiles, y, probs_t, x1, mod, g_fin, tm):
    bsz, n, d = x1.shape
    nt = n // tm
    kern = functools.partial(_combine_kernel, tm=tm, nt=nt)
    last = bsz * nt - 1
    return pl.pallas_call(
        kern,
        out_shape=jax.ShapeDtypeStruct((bsz, n, d), F32),
        grid=(bsz, nt),
        in_specs=[pl.BlockSpec((1, 1, TOP_K * tm), lambda b, t: (b * nt + t, 0, 0), memory_space=pltpu.SMEM),
                  pl.BlockSpec((1, 1, TOP_K * tm), lambda b, t: (jnp.minimum(b * nt + t + 1, last), 0, 0),
                               memory_space=pltpu.SMEM),
                  pl.BlockSpec(memory_space=pl.ANY),
                  pl.BlockSpec((tm, TOP_K), lambda b, t: (b * nt + t, 0)),
                  pl.BlockSpec((1, tm, d), lambda b, t: (b, t, 0)),
                  pl.BlockSpec(mod.shape, lambda b, t: (0, 0)),
                  pl.BlockSpec((1, d), lambda b, t: (0, 0))],
        out_specs=pl.BlockSpec((1, tm, d), lambda b, t: (b, t, 0)),
        scratch_shapes=[pltpu.VMEM((2, TOP_K, tm, d), F32), pltpu.SemaphoreType.DMA((2,))],
        compiler_params=_cparams(("arbitrary", "arbitrary"), VMEM_LIMIT),
        name="combine",
    )(dest_tiles, dest_tiles, y, probs_t, x1, mod, g_fin)


def _prep_w_in(w_in):
    q_a, kv, hq, ff, fb, hi, go, mm, mh = jnp.split(
        w_in, np.cumsum((MLA_Q_RANK, MLA_KV_RANK + MLA_ROPE) + (D_MODEL,) * 6).tolist(), axis=1)
    c_kv, kr = kv[:, :MLA_KV_RANK], kv[:, MLA_KV_RANK:]
    half = MLA_ROPE // 2
    kr_sw = jnp.concatenate([-kr[:, half:], kr[:, :half]], axis=1)
    w16 = jnp.concatenate([hq, hi, go, mm, mh, q_a, c_kv, kr, kr_sw], axis=1)
    w16 = jnp.pad(w16, ((0, 0), (0, N_PROJ16 - w16.shape[1])))
    return w16.astype(BF16), jnp.concatenate([ff, fb], axis=1).astype(BF16)


def _prep_w_q(w_q_b):
    w = w_q_b.reshape(MLA_Q_RANK, HEADS, MLA_QK)
    w = jnp.pad(w, ((0, 0), (0, 0), (0, MLA_QK_PAD - MLA_QK)))
    return w.reshape(MLA_Q_RANK, HEADS * MLA_QK_PAD).astype(BF16)


def _prep_w_kv(w_kv_b):
    w = w_kv_b.reshape(MLA_KV_RANK, HEADS, 2 * HEAD_DIM)
    k = w[:, :, :HEAD_DIM].reshape(MLA_KV_RANK, HEADS * HEAD_DIM)
    v = w[:, :, HEAD_DIM:].reshape(MLA_KV_RANK, HEADS * HEAD_DIM)
    return jnp.concatenate([k, v], axis=1).astype(BF16)


def _rope_tables(n_lat, n_ctx):
    f32 = np.float32
    rows = n_lat // GRID_W
    row = np.repeat(np.arange(rows), GRID_W).astype(f32)
    col = np.tile(np.arange(GRID_W), rows).astype(f32)
    n_freq = MLA_ROPE // 4
    inv = (f32(ROPE_BASE) ** (-np.arange(n_freq, dtype=f32) / f32(n_freq))).astype(f32)
    ang = np.concatenate([row[:, None] * inv, col[:, None] * inv], axis=-1).astype(f32)
    cos, sin = np.cos(ang).astype(f32), np.sin(ang).astype(f32)
    z64 = np.zeros((n_lat, 64), f32)
    cos_q = np.concatenate([cos, cos, z64], axis=1)
    sin_q = np.concatenate([-sin, sin, z64], axis=1)
    cos_k = np.concatenate([cos, cos, z64], axis=1)
    sin_k = np.concatenate([sin, sin, z64], axis=1)
    ctx_cos = np.concatenate([np.ones((n_ctx, 64), f32), np.zeros((n_ctx, 64), f32)], axis=1)
    cos_k = np.concatenate([cos_k, ctx_cos], axis=0)
    sin_k = np.concatenate([sin_k, np.zeros((n_ctx, 128), f32)], axis=0)
    return jnp.asarray(cos_q), jnp.asarray(sin_q), jnp.asarray(cos_k), jnp.asarray(sin_k)


def kernel(x, c, ctx, c_ctx, w_mod, b_mod, norm_mix_g, w_in, mla_q_norm_g, w_q_b, mla_kv_norm_g, w_kv_b,
           hg_lb_logits, hg_norm_g, w_out, norm_ffn_g, w_router, b_router, w_gate_up, b_gate_up, w_down,
           b_down, final_norm_g):
    bsz, n_lat, d = x.shape
    n_ctx = ctx.shape[1]
    assert d == D_MODEL and w_mod.shape[0] == 1
    assert n_lat % HG_CHUNK == 0 and n_ctx % HG_CHUNK == 0 and n_lat % GRID_W == 0
    t_tok = bsz * n_lat

    mod_rows = -(-(bsz + 1) // 8) * 8
    cc = jnp.concatenate([c, c_ctx[None, :], jnp.zeros((mod_rows - bsz - 1, d), F32)], axis=0)
    mod = _modulation(cc, w_mod[0], b_mod[0][None, :])

    xa = jnp.concatenate([x, ctx], axis=1)
    w16, w32 = _prep_w_in(w_in[0])
    p, pf = _inproj(xa, mod, norm_mix_g[0][None, :], w16, w32, n_lat)

    cos_q, sin_q, cos_k, sin_k = _rope_tables(n_lat, n_ctx)
    q = _mla_q(p, mla_q_norm_g[0][None, :], _prep_w_q(w_q_b[0]), cos_q, sin_q, n_lat)
    k, v = _mla_kv(p, mla_kv_norm_g[0][None, :], _prep_w_kv(w_kv_b[0]), cos_k, sin_k)
    y_mla = _attention(q, k, v)

    lb = jax.nn.softmax(hg_lb_logits.astype(F32), axis=1)[:, 0, :]
    o_f, o_b = _hgrn_scans(p, pf, lb.reshape(2, HEADS, 1, HEAD_DIM), n_lat)

    x1, h2p, logits_t = _merge(y_mla, o_f, o_b, p, x, mod, hg_norm_g[0][None, :], w_out[0].astype(BF16),
                               norm_ffn_g[0][None, :], w_router[0].T, b_router[0][:, None])

    idx_t, prob_t, rank_t, cnt = _router(logits_t)

    counts = cnt[:, 0].astype(jnp.int32)
    padded = (counts + FFN_ROWS - 1) // FFN_ROWS * FFN_ROWS
    pad_end = jnp.cumsum(padded)
    pad_start = pad_end - padded
    dest = _dest_rows(pad_start.astype(jnp.int32), idx_t, rank_t)
    n_rows = -(-(t_tok * TOP_K + N_EXPERTS * (FFN_ROWS - 1)) // FFN_ROWS) * FFN_ROWS
    nblk = n_rows // FFN_ROWS
    block_start = jnp.arange(nblk, dtype=jnp.int32) * FFN_ROWS
    block_e = jnp.minimum(jnp.sum(pad_end[None, :] <= block_start[:, None], axis=1), N_EXPERTS - 1).astype(jnp.int32)
    n_used = (pad_end[-1:] // FFN_ROWS).astype(jnp.int32)

    tm_c = _tile(n_lat, 256)
    dest_tiles = dest.reshape(TOP_K, t_tok // tm_c, tm_c).transpose(1, 0, 2).reshape(t_tok // tm_c, 1, TOP_K * tm_c)

    xs = _dispatch(dest_tiles, h2p, n_rows, tm_c)

    n_grp = 2 * D_EXPERT // GU_GROUP
    b_gu = b_gate_up[0].reshape(N_EXPERTS, n_grp, GU_GROUP // 2, 2).transpose(0, 1, 3, 2).reshape(N_EXPERTS, 1, 2 * D_EXPERT)
    y = _ffn(block_e, n_used, xs, w_gate_up[0], b_gu, w_down[0], b_down[0][:, None, :])

    return _combine(dest_tiles, y, prob_t.T, x1, mod, final_norm_g[None, :], tm_c)
```

```python
import functools

import jax
import jax.numpy as jnp
import numpy as np
from jax import lax
from jax.experimental import pallas as pl
from jax.experimental.pallas import tpu as pltpu

F32 = jnp.float32
BF16 = jnp.bfloat16

D_MODEL = 1024
EPS = 1e-6
LOG2_E = 1.4426950408889634
N_MOD = 6
GRID_W = 64
ROPE_BASE = 10000.0

HEADS = 8
HEAD_DIM = 128
MLA_ROPE = 64
MLA_QK = HEAD_DIM + MLA_ROPE
MLA_QK_PAD = 256
MLA_Q_RANK = 768
MLA_KV_RANK = 256

N_EXPERTS = 32
TOP_K = 4
D_EXPERT = 1024
SWIGLU_LIMIT = 7.0
SWIGLU_ALPHA = 1.702

HG_CHUNK = 128
FFN_ROWS = 512

COL_HQ, COL_I, COL_GO, COL_MM, COL_MH = (i * D_MODEL for i in range(5))
COL_QA = 5 * D_MODEL
COL_CKV = COL_QA + MLA_Q_RANK
COL_KR = COL_CKV + MLA_KV_RANK
PROJ16_TN = 1280
N_PROJ16 = -(-(COL_KR + 2 * MLA_ROPE) // PROJ16_TN) * PROJ16_TN
COL_FF, COL_FB = 0, D_MODEL
N_PROJ32 = 2 * D_MODEL
PROJ32_TN = 1024

VMEM_LIMIT = 56 * 1024 * 1024


def _cparams(sem, vmem=None):
    return pltpu.CompilerParams(dimension_semantics=sem, vmem_limit_bytes=vmem)


def _tile(n, pref, mult=8):
    best = None
    for t in range(mult, min(n, pref) + 1, mult):
        if n % t == 0:
            best = t
    assert best is not None, (n, pref, mult)
    return best


def _nt(a, b):
    return lax.dot_general(a, b, (((1,), (1,)), ((), ())), preferred_element_type=F32)


def _mm(a, b):
    return jnp.dot(a, b, preferred_element_type=F32)


def _split2(a):
    hi = a.astype(BF16)
    lo = (a - hi.astype(F32)).astype(BF16)
    return hi, lo


def _sigmoid(x):
    return 1.0 / (1.0 + jnp.exp(-x))


def _mod_kernel(c_ref, w_ref, b_ref, o_ref):
    c = c_ref[...]
    s = c * _sigmoid(c)
    s_hi, s_lo = _split2(s)
    w_hi, w_lo = _split2(w_ref[...])
    o_ref[...] = _mm(s_hi, w_hi) + _mm(s_lo, w_hi) + _mm(s_hi, w_lo) + b_ref[...]


def _modulation(cc, w_mod, b_mod):
    r, d = cc.shape
    n = w_mod.shape[1]
    tn = _tile(n, 1536, 128)
    return pl.pallas_call(
        _mod_kernel,
        out_shape=jax.ShapeDtypeStruct((r, n), F32),
        grid=(n // tn,),
        in_specs=[pl.BlockSpec((r, d), lambda j: (0, 0)),
                  pl.BlockSpec((d, tn), lambda j: (0, j)),
                  pl.BlockSpec((1, tn), lambda j: (0, j))],
        out_specs=pl.BlockSpec((r, tn), lambda j: (0, j)),
        compiler_params=_cparams(("parallel",), VMEM_LIMIT),
        name="mod",
    )(cc, w_mod, b_mod)


def _inproj_kernel(x_ref, mod_ref, g_ref, w16_ref, w32_ref, o16_ref, o32_ref, hn_ref, *, n_lat, tm, ctx_row, n16):
    b = pl.program_id(0)
    t = pl.program_id(1)
    j = pl.program_id(2)

    @pl.when(j == 0)
    def _():
        x = x_ref[0]
        ms = jnp.mean(x * x, axis=-1, keepdims=True)
        y = x * lax.rsqrt(ms + EPS) * g_ref[...]
        row = t * tm + lax.broadcasted_iota(jnp.int32, (tm, 1), 0)
        is_ctx = row >= n_lat
        m_lat = mod_ref[pl.ds(b, 1), :]
        m_ctx = mod_ref[pl.ds(ctx_row, 1), :]
        shift = jnp.where(is_ctx, m_ctx[:, 0:D_MODEL], m_lat[:, 0:D_MODEL])
        scale = jnp.where(is_ctx, m_ctx[:, D_MODEL:2 * D_MODEL], m_lat[:, D_MODEL:2 * D_MODEL])
        hn_ref[...] = (y * (1.0 + scale) + shift).astype(BF16)

    @pl.when(j < n16)
    def _():
        o16_ref[0] = _mm(hn_ref[...], w16_ref[...]).astype(BF16)

    @pl.when(j >= n16)
    def _():
        o32_ref[0] = _mm(hn_ref[...], w32_ref[...])


def _inproj(xa, mod, g, w16, w32, n_lat):
    bsz, rows, d = xa.shape
    tm = _tile(rows, 1152)
    n16 = N_PROJ16 // PROJ16_TN
    n32 = N_PROJ32 // PROJ32_TN

    def j16(j):
        return jnp.minimum(j, n16 - 1)

    def j32(j):
        return jnp.maximum(j - n16, 0)

    kern = functools.partial(_inproj_kernel, n_lat=n_lat, tm=tm, ctx_row=bsz, n16=n16)
    return pl.pallas_call(
        kern,
        out_shape=(jax.ShapeDtypeStruct((bsz, rows, N_PROJ16), BF16),
                   jax.ShapeDtypeStruct((bsz, rows, N_PROJ32), F32)),
        grid=(bsz, rows // tm, n16 + n32),
        in_specs=[pl.BlockSpec((1, tm, d), lambda b, t, j: (b, t, 0)),
                  pl.BlockSpec(mod.shape, lambda b, t, j: (0, 0)),
                  pl.BlockSpec((1, d), lambda b, t, j: (0, 0)),
                  pl.BlockSpec((d, PROJ16_TN), lambda b, t, j: (0, j16(j))),
                  pl.BlockSpec((d, PROJ32_TN), lambda b, t, j: (0, j32(j)))],
        out_specs=(pl.BlockSpec((1, tm, PROJ16_TN), lambda b, t, j: (b, t, j16(j))),
                   pl.BlockSpec((1, tm, PROJ32_TN), lambda b, t, j: (b, t, j32(j)))),
        scratch_shapes=[pltpu.VMEM((tm, d), BF16)],
        compiler_params=_cparams(("parallel", "parallel", "arbitrary"), VMEM_LIMIT),
        name="inproj",
    )(xa, mod, g, w16, w32)


def _mlaq_kernel(a0_ref, a1_ref, a2_ref, g_ref, w_ref, cos_ref, sin_ref, o_ref, *, tm):
    parts = [a0_ref[0].astype(F32), a1_ref[0].astype(F32), a2_ref[0].astype(F32)]
    ss = sum(jnp.sum(p * p, axis=-1, keepdims=True) for p in parts)
    r = lax.rsqrt(ss * (1.0 / MLA_Q_RANK) + EPS)
    acc = None
    for j, p in enumerate(parts):
        hj = (p * r * g_ref[:, j * 256:(j + 1) * 256]).astype(BF16)
        d = _mm(hj, w_ref[j * 256:(j + 1) * 256, :])
        acc = d if acc is None else acc + d
    lane = lax.broadcasted_iota(jnp.int32, (tm, HEAD_DIM), 1)
    scale = MLA_QK ** -0.5 * LOG2_E
    cos = cos_ref[...]
    sin = sin_ref[...]
    for h in range(HEADS):
        nope = acc[:, h * MLA_QK_PAD:h * MLA_QK_PAD + HEAD_DIM]
        rp = acc[:, h * MLA_QK_PAD + HEAD_DIM:(h + 1) * MLA_QK_PAD]
        swapped = jnp.where(lane < 32, pltpu.roll(rp, 96, 1), pltpu.roll(rp, 32, 1))
        rot = rp * cos + swapped * sin
        o_ref[0, h, :, 0:HEAD_DIM] = (nope * scale).astype(BF16)
        o_ref[0, h, :, HEAD_DIM:MLA_QK_PAD] = (rot * scale).astype(BF16)


def _mla_q(p, g, w, cos_q, sin_q, n_lat):
    bsz = p.shape[0]
    tm = _tile(n_lat, 512)
    cb = COL_QA // 256
    kern = functools.partial(_mlaq_kernel, tm=tm)
    return pl.pallas_call(
        kern,
        out_shape=jax.ShapeDtypeStruct((bsz, HEADS, n_lat, MLA_QK_PAD), BF16),
        grid=(bsz, n_lat // tm),
        in_specs=[pl.BlockSpec((1, tm, 256), lambda b, t: (b, t, cb)),
                  pl.BlockSpec((1, tm, 256), lambda b, t: (b, t, cb + 1)),
                  pl.BlockSpec((1, tm, 256), lambda b, t: (b, t, cb + 2)),
                  pl.BlockSpec((1, MLA_Q_RANK), lambda b, t: (0, 0)),
                  pl.BlockSpec(w.shape, lambda b, t: (0, 0)),
                  pl.BlockSpec((tm, HEAD_DIM), lambda b, t: (t, 0)),
                  pl.BlockSpec((tm, HEAD_DIM), lambda b, t: (t, 0))],
        out_specs=pl.BlockSpec((1, HEADS, tm, MLA_QK_PAD), lambda b, t: (b, 0, t, 0)),
        compiler_params=_cparams(("parallel", "parallel"), VMEM_LIMIT),
        name="mla_q",
    )(p, p, p, g, w, cos_q, sin_q)


def _mlakv_kernel(c_ref, kr_ref, g_ref, w_ref, cos_ref, sin_ref, k_ref, v_ref):
    c = c_ref[0].astype(F32)
    ms = jnp.mean(c * c, axis=-1, keepdims=True)
    hn = (c * lax.rsqrt(ms + EPS) * g_ref[...]).astype(BF16)
    kv = _mm(hn, w_ref[...])
    grp = kr_ref[0].astype(F32)
    rot = (grp * cos_ref[...] + pltpu.roll(grp, 64, 1) * sin_ref[...]).astype(BF16)
    for h in range(HEADS):
        k_ref[0, h, :, 0:HEAD_DIM] = kv[:, h * HEAD_DIM:(h + 1) * HEAD_DIM].astype(BF16)
        k_ref[0, h, :, HEAD_DIM:MLA_QK_PAD] = rot
        v_ref[0, h] = kv[:, D_MODEL + h * HEAD_DIM:D_MODEL + (h + 1) * HEAD_DIM].astype(BF16)


def _mla_kv(p, g, w, cos_k, sin_k):
    bsz, rows, _ = p.shape
    tm = _tile(rows, 768)
    return pl.pallas_call(
        _mlakv_kernel,
        out_shape=(jax.ShapeDtypeStruct((bsz, HEADS, rows, MLA_QK_PAD), BF16),
                   jax.ShapeDtypeStruct((bsz, HEADS, rows, HEAD_DIM), BF16)),
        grid=(bsz, rows // tm),
        in_specs=[pl.BlockSpec((1, tm, MLA_KV_RANK), lambda b, t: (b, t, COL_CKV // MLA_KV_RANK)),
                  pl.BlockSpec((1, tm, 128), lambda b, t: (b, t, COL_KR // 128)),
                  pl.BlockSpec((1, MLA_KV_RANK), lambda b, t: (0, 0)),
                  pl.BlockSpec(w.shape, lambda b, t: (0, 0)),
                  pl.BlockSpec((tm, 128), lambda b, t: (t, 0)),
                  pl.BlockSpec((tm, 128), lambda b, t: (t, 0))],
        out_specs=(pl.BlockSpec((1, HEADS, tm, MLA_QK_PAD), lambda b, t: (b, 0, t, 0)),
                   pl.BlockSpec((1, HEADS, tm, HEAD_DIM), lambda b, t: (b, 0, t, 0))),
        compiler_params=_cparams(("parallel", "parallel"), VMEM_LIMIT),
        name="mla_kv",
    )(p, p, g, w, cos_k, sin_k)


def _attn_kernel(q_ref, k_ref, v_ref, o_ref, *, tq, sub):
    k = k_ref[0, 0]
    v = v_ref[0, 0]
    v1 = jnp.concatenate([v, jnp.ones_like(v)], axis=1)

    def sub_tile(r):
        rows = slice(r * sub, (r + 1) * sub)
        s = _nt(q_ref[0, 0, rows, :], k)
        yield
        m = jnp.max(s, axis=-1, keepdims=True)
        p = jnp.exp2(s - m).astype(BF16)
        yield
        ol = _mm(p, v1)
        o_ref[0, rows, :] = (ol[:, 0:HEAD_DIM] * (1.0 / ol[:, HEAD_DIM:HEAD_DIM + 1])).astype(BF16)

    _round_robin([sub_tile(r) for r in range(tq // sub)])


def _attention(q, k, v):
    bsz, _, n, _ = q.shape
    m = k.shape[2]
    tq = _tile(n, 2048)
    sub = _tile(tq, 512)
    kern = functools.partial(_attn_kernel, tq=tq, sub=sub)
    return pl.pallas_call(
        kern,
        out_shape=jax.ShapeDtypeStruct((bsz, n, HEADS * HEAD_DIM), BF16),
        grid=(bsz, HEADS, n // tq),
        in_specs=[pl.BlockSpec((1, 1, tq, MLA_QK_PAD), lambda b, h, t: (b, h, t, 0)),
                  pl.BlockSpec((1, 1, m, MLA_QK_PAD), lambda b, h, t: (b, h, 0, 0)),
                  pl.BlockSpec((1, 1, m, HEAD_DIM), lambda b, h, t: (b, h, 0, 0))],
        out_specs=pl.BlockSpec((1, tq, HEAD_DIM), lambda b, h, t: (b, t, h)),
        compiler_params=_cparams(("parallel", "parallel", "arbitrary"), VMEM_LIMIT),
        name="attention",
    )(q, k, v)


def _level_ref(cum, blk, reverse):
    c = cum.shape[0]
    half = blk // 2
    r = half if reverse else half - 1
    if blk >= 8:
        x = cum.reshape(c // blk, blk, HEAD_DIM)
        e = jnp.broadcast_to(x[:, r:r + 1, :], x.shape)
        return e.reshape(c, HEAD_DIM)
    x = cum.reshape(c // 8, 8, HEAD_DIM)
    sub = lax.broadcasted_iota(jnp.int32, x.shape, 1)
    e = None
    for jb in range(8 // blk):
        cand = jnp.broadcast_to(x[:, jb * blk + r:jb * blk + r + 1, :], x.shape)
        e = cand if e is None else jnp.where(sub >= jb * blk, cand, e)
    return e.reshape(c, HEAD_DIM)


HG_SUB = 32
HG_SUB_LEVEL = HG_SUB.bit_length() - 1
HG_MAX_EXP2 = 100.0


def _hgrn_state(z, v, lb, tri, st, *, chunk, reverse):
    f = lb + (1.0 - lb) * _sigmoid(z)
    kb = (1.0 - f).astype(BF16)
    g = jnp.log(f) * LOG2_E

    g_hi, g_lo = _split2(g)
    yield
    cum = _mm(tri, jnp.concatenate([g_hi, g_lo], axis=0))
    yield
    last = 0 if reverse else chunk - 1
    tot = cum[last:last + 1, :]
    kt = kb * jnp.exp2(tot - cum).astype(BF16)
    vt = v.astype(F32).T.astype(BF16)
    yield
    st_new = st * jnp.exp2(tot) + _mm(vt, kt)
    return kb, cum, st_new


def _sub_block_decay(cum, reverse):
    c = cum.shape[0]
    x = cum.reshape(c // HG_SUB, HG_SUB, HEAD_DIM)
    zero = jnp.zeros((1, 1, HEAD_DIM), F32)
    if reverse:
        edge = jnp.concatenate([x[1:, 0:1, :], zero], axis=0)
    else:
        edge = jnp.concatenate([zero, x[:-1, HG_SUB - 1:HG_SUB, :]], axis=0)
    return (x - edge).reshape(c, HEAD_DIM)


def _hgrn_readout(q, v, kb, cum, sub, lvl, st, *, chunk, reverse, shared):
    q = q.astype(F32)
    qb = (q * _sigmoid(q) * (HEAD_DIM ** -0.5)).astype(BF16)
    yield
    if shared:
        att = jnp.where(lvl == 0, _nt(qb * jnp.exp2(sub).astype(BF16), kb * jnp.exp2(-sub).astype(BF16)).astype(BF16),
                        jnp.zeros((), BF16))
        first = HG_SUB_LEVEL + 1
    else:
        att = jnp.where(lvl == 0, _nt(qb, kb).astype(BF16), jnp.zeros((), BF16))
        first = 1
    for lv in range(first, chunk.bit_length()):
        zrel = cum - _level_ref(cum, 1 << lv, reverse)
        neg_abs = pltpu.bitcast(pltpu.bitcast(zrel, jnp.uint32) | jnp.uint32(0x80000000), F32)
        e = jnp.exp2(neg_abs).astype(BF16)
        yield
        att = jnp.where(lvl == lv, _nt(qb * e, kb * e).astype(BF16), att)
        yield
    return _nt(qb * jnp.exp2(cum).astype(BF16), st.astype(BF16)) + _mm(att, v.astype(BF16))


def _round_robin(gens):
    out = [None] * len(gens)
    active = list(range(len(gens)))
    while active:
        for i in list(active):
            try:
                next(gens[i])
            except StopIteration as e:
                out[i] = e.value
                active.remove(i)
    return out


def _hgrn_kernel(qf_ref, zf_ref, if_ref, qb_ref, zb_ref, ib_ref, lb_ref, tri_ref, lvl_ref, of_ref, ob_ref, st_ref,
                 *, chunk, group, n_ctx_chunks):
    step = pl.program_id(2)

    @pl.when(step == 0)
    def _():
        st_ref[...] = jnp.zeros_like(st_ref)

    ins = ((qf_ref, zf_ref, if_ref), (qb_ref, zb_ref, ib_ref))
    outs = (of_ref, ob_ref)
    chains = [(d, j) for j in range(group) for d in range(2)]

    def cols(j):
        return slice(j * HEAD_DIM, (j + 1) * HEAD_DIM)

    def advance(want_out):
        old = [st_ref[d, j] for d, j in chains]
        parts = _round_robin([_hgrn_state(ins[d][1][0, :, cols(j)], ins[d][2][0, :, cols(j)], lb_ref[d, j],
                                          tri_ref[d], st, chunk=chunk, reverse=bool(d))
                              for (d, j), st in zip(chains, old)])
        for (d, j), (_, _, st_new) in zip(chains, parts):
            st_ref[d, j] = st_new
        if not want_out:
            return
        subs = [_sub_block_decay(cum, bool(d)) for (d, j), (_, cum, _) in zip(chains, parts)]
        low = subs[0]
        for s in subs[1:]:
            low = jnp.minimum(low, s)
        low = jnp.min(jnp.min(low, axis=0, keepdims=True), axis=1, keepdims=True)
        in_range = low[0, 0] >= -HG_MAX_EXP2

        def readouts(shared):
            os_ = _round_robin([_hgrn_readout(ins[d][0][0, :, cols(j)], ins[d][2][0, :, cols(j)], kb, cum, sub,
                                              lvl_ref[d, int(shared)], st, chunk=chunk, reverse=bool(d), shared=shared)
                                for (d, j), (kb, cum, _), sub, st in zip(chains, parts, subs, old)])
            for (d, j), o in zip(chains, os_):
                outs[d][0, j] = o.astype(BF16)

        @pl.when(in_range)
        def _():
            readouts(True)

        @pl.when(jnp.logical_not(in_range))
        def _():
            readouts(False)

    @pl.when(step < n_ctx_chunks)
    def _():
        advance(False)

    @pl.when(step >= n_ctx_chunks)
    def _():
        advance(True)


def _hgrn_consts(chunk, reverse):
    t = np.arange(chunk)[:, None]
    s = np.arange(chunk)[None, :]
    x = t ^ s
    bitlen = np.zeros_like(x)
    for b in range(chunk.bit_length()):
        bitlen = np.where(x >> b > 0, b + 1, bitlen)
    valid = (s > t) if reverse else (t > s)
    lvl = np.where(t == s, 0, np.where(valid, bitlen, -1)).astype(np.float32)
    lvl_shared = np.where((lvl >= 0) & (lvl <= HG_SUB_LEVEL), 0, lvl)
    tri = ((s >= t) if reverse else (t >= s)).astype(np.float32)
    return np.concatenate([tri, tri], axis=1), np.stack([lvl, lvl_shared])


HG_GROUP = 8


def _hgrn_scans(p, pf, lb, n_lat):
    bsz, rows, _ = p.shape
    c = HG_CHUNK
    grp = HG_GROUP
    n_chunks = rows // c
    n_lat_c = n_lat // c
    n_ctx_c = n_chunks - n_lat_c
    consts = [_hgrn_consts(c, False), _hgrn_consts(c, True)]
    tri = jnp.asarray(np.stack([consts[0][0], consts[1][0]]), dtype=BF16)
    lvl = jnp.asarray(np.stack([consts[0][1], consts[1][1]]), dtype=BF16)

    def cidx_f(i):
        return jnp.where(i < n_ctx_c, n_lat_c + i, i - n_ctx_c)

    def cidx_b(i):
        return n_chunks - 1 - i

    def col_spec(col, cidx):
        return pl.BlockSpec((1, c, grp * HEAD_DIM), lambda b, h, i: (b, cidx(i), col // (grp * HEAD_DIM) + h))

    def out_spec(cidx):
        return pl.BlockSpec((1, grp, c, HEAD_DIM), lambda b, h, i: (b, h, cidx(jnp.maximum(i, n_ctx_c)), 0))

    o_shape = jax.ShapeDtypeStruct((bsz, HEADS, n_lat, HEAD_DIM), BF16)
    kern = functools.partial(_hgrn_kernel, chunk=c, group=grp, n_ctx_chunks=n_ctx_c)
    return pl.pallas_call(
        kern,
        out_shape=(o_shape, o_shape),
        grid=(bsz, HEADS // grp, n_chunks),
        in_specs=[col_spec(COL_HQ, cidx_f), col_spec(COL_FF, cidx_f), col_spec(COL_I, cidx_f),
                  col_spec(COL_HQ, cidx_b), col_spec(COL_FB, cidx_b), col_spec(COL_I, cidx_b),
                  pl.BlockSpec((2, grp, 1, HEAD_DIM), lambda b, h, i: (0, h, 0, 0)),
                  pl.BlockSpec((2, c, 2 * c), lambda b, h, i: (0, 0, 0)),
                  pl.BlockSpec((2, 2, c, c), lambda b, h, i: (0, 0, 0, 0))],
        out_specs=(out_spec(cidx_f), out_spec(cidx_b)),
        scratch_shapes=[pltpu.VMEM((2, grp, HEAD_DIM, HEAD_DIM), F32)],
        compiler_params=_cparams(("parallel", "parallel", "arbitrary"), VMEM_LIMIT),
        name="hgrn",
    )(p, pf, p, p, pf, p, lb, tri, lvl)


def _merge_kernel(ym_ref, of_ref, ob_ref, go_ref, gm_ref, gh_ref, x_ref, mod_ref, hgg_ref, wout_ref,
                  gffn_ref, wr_ref, br_ref, x1_ref, h2_ref, lg_ref, y_scr):
    b = pl.program_id(0)
    for h in range(HEADS):
        sl = slice(h * HEAD_DIM, (h + 1) * HEAD_DIM)
        o = of_ref[0, h].astype(F32) + ob_ref[0, h].astype(F32)
        ms = jnp.mean(o * o, axis=-1, keepdims=True)
        g = go_ref[0, :, sl].astype(F32)
        yh = o * lax.rsqrt(ms + EPS) * hgg_ref[...] * (g * _sigmoid(g))
        y = (_sigmoid(gm_ref[0, :, sl].astype(F32)) * ym_ref[0, :, sl].astype(F32)
             + _sigmoid(gh_ref[0, :, sl].astype(F32)) * yh)
        y_scr[:, sl] = y.astype(BF16)
    mix = _mm(y_scr[...], wout_ref[...])
    m = mod_ref[pl.ds(b, 1), :]
    x1 = x_ref[0] + m[:, 2 * D_MODEL:3 * D_MODEL] * mix
    x1_ref[0] = x1
    ms = jnp.mean(x1 * x1, axis=-1, keepdims=True)
    h2 = (x1 * lax.rsqrt(ms + EPS) * gffn_ref[...]) * (1.0 + m[:, 4 * D_MODEL:5 * D_MODEL]) + m[:, 3 * D_MODEL:4 * D_MODEL]
    h_hi = h2.astype(BF16)
    h2_ref[...] = h_hi
    h_lo = (h2 - h_hi.astype(F32)).astype(BF16)
    w_hi, w_lo = _split2(wr_ref[...])
    lg_ref[...] = _nt(w_hi, h_hi) + _nt(w_lo, h_hi) + _nt(w_hi, h_lo) + br_ref[...]


def _merge(y_mla, o_f, o_b, p, x, mod, hg_g, w_out, g_ffn, w_r_t, b_r):
    bsz, n, d = x.shape
    tm = _tile(n, 256, 128)
    nt = n // tm

    def pcol(col):
        return pl.BlockSpec((1, tm, d), lambda b, t: (b, t, col // d))

    tok = lambda b, t: (b, t, 0)
    const2 = lambda b, t: (0, 0)
    return pl.pallas_call(
        _merge_kernel,
        out_shape=(jax.ShapeDtypeStruct((bsz, n, d), F32),
                   jax.ShapeDtypeStruct((bsz * n, d), BF16),
                   jax.ShapeDtypeStruct((N_EXPERTS, bsz * n), F32)),
        grid=(bsz, nt),
        in_specs=[pl.BlockSpec((1, tm, d), tok),
                  pl.BlockSpec((1, HEADS, tm, HEAD_DIM), lambda b, t: (b, 0, t, 0)),
                  pl.BlockSpec((1, HEADS, tm, HEAD_DIM), lambda b, t: (b, 0, t, 0)),
                  pcol(COL_GO), pcol(COL_MM), pcol(COL_MH),
                  pl.BlockSpec((1, tm, d), tok),
                  pl.BlockSpec(mod.shape, const2),
                  pl.BlockSpec((1, HEAD_DIM), const2),
                  pl.BlockSpec((d, d), const2),
                  pl.BlockSpec((1, d), const2),
                  pl.BlockSpec((N_EXPERTS, d), const2),
                  pl.BlockSpec((N_EXPERTS, 1), const2)],
        out_specs=(pl.BlockSpec((1, tm, d), tok),
                   pl.BlockSpec((tm, d), lambda b, t: (b * nt + t, 0)),
                   pl.BlockSpec((N_EXPERTS, tm), lambda b, t: (0, b * nt + t))),
        scratch_shapes=[pltpu.VMEM((tm, d), BF16)],
        compiler_params=_cparams(("parallel", "parallel"), VMEM_LIMIT),
        name="merge",
    )(y_mla, o_f, o_b, p, p, p, x, mod, hg_g, w_out, g_ffn, w_r_t, b_r)


def _router_kernel(lg_ref, upper_ref, idx_ref, prob_ref, rank_ref, base_ref, cnt_ref, run_ref, *, tm):
    @pl.when(pl.program_id(0) == 0)
    def _():
        run_ref[...] = jnp.zeros_like(run_ref)

    l = lg_ref[...]
    eidx = lax.broadcasted_iota(jnp.int32, (N_EXPERTS, tm), 0)
    vals, sel = [], []
    for _ in range(TOP_K):
        m = jnp.max(l, axis=0, keepdims=True)
        first = jnp.min(jnp.where(l == m, eidx, N_EXPERTS), axis=0, keepdims=True)
        vals.append(m)
        sel.append(first)
        l = jnp.where(eidx == first, -jnp.inf, l)
    ex = [jnp.exp(v - vals[0]) for v in vals]
    inv = 1.0 / (ex[0] + ex[1] + ex[2] + ex[3])
    onehot = jnp.zeros((N_EXPERTS, tm), F32)
    for k in range(TOP_K):
        onehot = onehot + jnp.where(eidx == sel[k], 1.0, 0.0)
    before = _mm(onehot.astype(BF16), upper_ref[...])
    base_ref[0] = run_ref[...]
    for k in range(TOP_K):
        idx_ref[k:k + 1, :] = sel[k]
        prob_ref[k:k + 1, :] = ex[k] * inv
        rank_ref[k:k + 1, :] = jnp.sum(jnp.where(eidx == sel[k], before, 0.0), axis=0, keepdims=True).astype(jnp.int32)
    run_ref[...] = run_ref[...] + jnp.sum(onehot, axis=1, keepdims=True)
    cnt_ref[...] = run_ref[...]


def _router(logits_t, tm):
    _, t = logits_t.shape
    upper = jnp.asarray(np.triu(np.ones((tm, tm), np.float32), 1), dtype=BF16)
    kern = functools.partial(_router_kernel, tm=tm)
    tok = pl.BlockSpec((TOP_K, tm), lambda i: (0, i))
    return pl.pallas_call(
        kern,
        out_shape=(jax.ShapeDtypeStruct((TOP_K, t), jnp.int32),
                   jax.ShapeDtypeStruct((TOP_K, t), F32),
                   jax.ShapeDtypeStruct((TOP_K, t), jnp.int32),
                   jax.ShapeDtypeStruct((t // tm, N_EXPERTS, 128), F32),
                   jax.ShapeDtypeStruct((N_EXPERTS, 128), F32)),
        grid=(t // tm,),
        in_specs=[pl.BlockSpec((N_EXPERTS, tm), lambda i: (0, i)),
                  pl.BlockSpec((tm, tm), lambda i: (0, 0))],
        out_specs=(tok, tok, tok, pl.BlockSpec((1, N_EXPERTS, 128), lambda i: (i, 0, 0)),
                   pl.BlockSpec((N_EXPERTS, 128), lambda i: (0, 0))),
        scratch_shapes=[pltpu.VMEM((N_EXPERTS, 128), F32)],
        compiler_params=_cparams(("arbitrary",), VMEM_LIMIT),
        name="router",
    )(logits_t, upper)


ROUTE_TILE = 256
ROW_ALIGN = 8
SLOT_ROWS = 48
N_SLOTS = N_EXPERTS * SLOT_ROWS
ZERO_ROWS = FFN_ROWS + SLOT_ROWS


def _slot_ids(idx, lrank, c):
    r = lrank - c * SLOT_ROWS
    return jnp.where(jnp.logical_and(r >= 0, r < SLOT_ROWS), idx * SLOT_ROWS + r, -1)


def _pack_bf16_pairs(x):
    u = pltpu.bitcast(x, jnp.uint32)
    w = x.shape[1] // 2
    return (u[:, 0:w] >> 16) | u[:, w:2 * w]


def _unpack_bf16_pairs(u):
    lo = pltpu.bitcast(u << 16, F32).astype(BF16)
    hi = pltpu.bitcast(u & jnp.uint32(0xFFFF0000), F32).astype(BF16)
    return jnp.concatenate([lo, hi], axis=1)


def _dispatch_kernel(tstart_ref, tcnt_ref, npass_ref, fill_ref, nused_ref, idx_ref, lrank_ref, h2_ref, xs_ref,
                     xbuf, zbuf, sem, zsem, state, *, tm, n_blocks):
    j = pl.program_id(0)

    def slot_copy(slot, e, row):
        return pltpu.make_async_copy(xbuf.at[slot, pl.ds(e * SLOT_ROWS, SLOT_ROWS)],
                                     xs_ref.at[pl.ds(pl.multiple_of(row, ROW_ALIGN), SLOT_ROWS)], sem.at[slot])

    def wait_pass(slot, n):
        def body(i, carry):
            slot_copy(slot, 0, 0).wait()
            return carry
        lax.fori_loop(0, n, body, 0)

    @pl.when(j == 0)
    def _():
        zbuf[...] = jnp.zeros_like(zbuf)
        state[0] = 0
        state[1] = 0
        for parity in range(2):
            for e in range(parity, N_EXPERTS, 2):
                pltpu.make_async_copy(zbuf, xs_ref.at[pl.ds(pl.multiple_of(fill_ref[e], ROW_ALIGN), ZERO_ROWS)],
                                      zsem).start()
            for e in range(parity, N_EXPERTS, 2):
                pltpu.make_async_copy(zbuf, xs_ref.at[pl.ds(0, ZERO_ROWS)], zsem).wait()

        def tail(i, carry):
            pltpu.make_async_copy(zbuf.at[pl.ds(0, FFN_ROWS)],
                                  xs_ref.at[pl.ds(pl.multiple_of(i * FFN_ROWS, ROW_ALIGN), FFN_ROWS)], zsem).start()
            return carry
        lax.fori_loop(nused_ref[0], n_blocks, tail, 0)

        def tail_wait(i, carry):
            pltpu.make_async_copy(zbuf.at[pl.ds(0, FFN_ROWS)], xs_ref.at[pl.ds(0, FFN_ROWS)], zsem).wait()
            return carry
        lax.fori_loop(nused_ref[0], n_blocks, tail_wait, 0)

    def one_pass(c, carry):
        rows = lax.broadcasted_iota(jnp.int32, (N_SLOTS, tm), 0)
        sel = jnp.zeros((N_SLOTS, tm), F32)
        for k in range(TOP_K):
            sel = jnp.where(rows == _slot_ids(idx_ref[k:k + 1, :], lrank_ref[k:k + 1, :], c), 1.0, sel)
        packed = _pack_bf16_pairs(_mm(sel.astype(BF16), h2_ref[...]))
        p = state[0]
        slot = p % 2
        wait_pass(1 - slot, state[1])
        xbuf[slot] = packed
        n = 0
        for e in range(N_EXPERTS):
            live = tcnt_ref[j * N_EXPERTS + e] > c * SLOT_ROWS

            @pl.when(live)
            def _():
                slot_copy(slot, e, tstart_ref[j * N_EXPERTS + e] + c * SLOT_ROWS).start()
            n = n + live.astype(jnp.int32)
        state[0] = p + 1
        state[1] = n
        return carry

    lax.fori_loop(0, npass_ref[j], one_pass, 0)

    @pl.when(j == pl.num_programs(0) - 1)
    def _():
        wait_pass((state[0] - 1) % 2, state[1])


def _dispatch(tile_start, tile_count, n_pass, fill_start, n_used, idx_t, lrank_t, h2, n_rows):
    t, d = h2.shape
    tm = ROUTE_TILE
    w = d // 2
    n_blocks = n_rows // FFN_ROWS
    kern = functools.partial(_dispatch_kernel, tm=tm, n_blocks=n_blocks)
    tok = pl.BlockSpec((TOP_K, tm), lambda i, *_: (0, i))
    grid_spec = pltpu.PrefetchScalarGridSpec(
        num_scalar_prefetch=5,
        grid=(t // tm,),
        in_specs=[tok, tok, pl.BlockSpec((tm, d), lambda i, *_: (i, 0))],
        out_specs=pl.BlockSpec(memory_space=pl.ANY),
        scratch_shapes=[pltpu.VMEM((2, N_SLOTS, w), jnp.uint32), pltpu.VMEM((ZERO_ROWS, w), jnp.uint32),
                        pltpu.SemaphoreType.DMA((2,)), pltpu.SemaphoreType.DMA, pltpu.SMEM((2,), jnp.int32)],
    )
    return pl.pallas_call(
        kern,
        out_shape=jax.ShapeDtypeStruct((n_rows, w), jnp.uint32),
        grid_spec=grid_spec,
        compiler_params=_cparams(("arbitrary",), VMEM_LIMIT),
        name="dispatch",
    )(tile_start, tile_count, n_pass, fill_start, n_used, idx_t, lrank_t, h2)


GU_GROUP = 256


def _ffn_kernel(be_ref, nu_ref, x_ref, wgu_ref, bgu_ref, wd_ref, bd_ref, perm_ref, y_ref, wgu_s, wd_s):
    i = pl.program_id(0)
    live = i < nu_ref[0]
    new_expert = jnp.logical_or(i == 0, be_ref[i] != be_ref[jnp.maximum(i - 1, 0)])

    @pl.when(jnp.logical_and(live, new_expert))
    def _():
        for g in range(2 * D_EXPERT // GU_GROUP):
            sl = slice(g * GU_GROUP, (g + 1) * GU_GROUP)
            wgu_s[:, sl] = _mm(wgu_ref[0, :, sl].astype(BF16), perm_ref[...]).astype(BF16)
        wd_s[...] = wd_ref[0].astype(BF16)

    @pl.when(live)
    def _():
        x = _unpack_bf16_pairs(x_ref[...])
        gu = _mm(x, wgu_s[...]) + bgu_ref[0]
        half = GU_GROUP // 2
        n_grp = 2 * D_EXPERT // GU_GROUP
        glu = jnp.concatenate([gu[:, g * GU_GROUP:g * GU_GROUP + half] for g in range(n_grp)], axis=1)
        lin = jnp.concatenate([gu[:, g * GU_GROUP + half:(g + 1) * GU_GROUP] for g in range(n_grp)], axis=1)
        glu = jnp.minimum(glu, SWIGLU_LIMIT)
        lin = jnp.clip(lin, -SWIGLU_LIMIT, SWIGLU_LIMIT)
        act = glu * _sigmoid(SWIGLU_ALPHA * glu) * (lin + 1.0)
        y = _mm(act.astype(BF16), wd_s[...]) + bd_ref[0]
        y_ref[...] = _pack_bf16_pairs(y.astype(BF16).astype(F32))

    @pl.when(jnp.logical_not(live))
    def _():
        y_ref[...] = jnp.zeros_like(y_ref)


def _ffn(block_e, n_used, xs, w_gu, b_gu, w_d, b_d):
    n_rows, w = xs.shape
    bm = FFN_ROWS
    nblk = n_rows // bm
    d = w * 2
    half = GU_GROUP // 2
    perm = np.zeros((GU_GROUP, GU_GROUP), np.float32)
    perm[2 * np.arange(half), np.arange(half)] = 1.0
    perm[2 * np.arange(half) + 1, half + np.arange(half)] = 1.0

    def xmap(i, be, nu):
        return (jnp.minimum(i, nu[0] - 1), 0)

    def wmap(i, be, nu):
        return (be[i], 0, 0)

    grid_spec = pltpu.PrefetchScalarGridSpec(
        num_scalar_prefetch=2,
        grid=(nblk,),
        in_specs=[pl.BlockSpec((bm, w), xmap),
                  pl.BlockSpec((1, d, 2 * D_EXPERT), wmap),
                  pl.BlockSpec((1, 1, 2 * D_EXPERT), wmap),
                  pl.BlockSpec((1, D_EXPERT, d), wmap),
                  pl.BlockSpec((1, 1, d), wmap),
                  pl.BlockSpec((GU_GROUP, GU_GROUP), lambda i, be, nu: (0, 0))],
        out_specs=pl.BlockSpec((bm, w), lambda i, be, nu: (i, 0)),
        scratch_shapes=[pltpu.VMEM((d, 2 * D_EXPERT), BF16), pltpu.VMEM((D_EXPERT, d), BF16)],
    )
    return pl.pallas_call(
        _ffn_kernel,
        out_shape=jax.ShapeDtypeStruct((n_rows, w), jnp.uint32),
        grid_spec=grid_spec,
        compiler_params=_cparams(("arbitrary",), VMEM_LIMIT),
        name="ffn",
    )(block_e, n_used, xs, w_gu, b_gu, w_d, b_d, jnp.asarray(perm, dtype=BF16))


def _combine_kernel(tstart_ref, tcnt_ref, npass_ref, idx_ref, lrank_ref, p_ref, y_ref, x1_ref, mod_ref, g_ref, o_ref,
                    ybuf, sem, *, tm, nt):
    b = pl.program_id(0)
    j = b * nt + pl.program_id(1)
    n_steps = pl.num_programs(0) * nt
    slot = j % 2

    def slot_copy(s, e, row):
        return pltpu.make_async_copy(y_ref.at[pl.ds(pl.multiple_of(row, ROW_ALIGN), SLOT_ROWS)], ybuf.at[s, pl.ds(e * SLOT_ROWS, SLOT_ROWS)],
                                     sem.at[s])

    def fetch(tile, c, s):
        for e in range(N_EXPERTS):
            @pl.when(tcnt_ref[tile * N_EXPERTS + e] > c * SLOT_ROWS)
            def _():
                slot_copy(s, e, tstart_ref[tile * N_EXPERTS + e] + c * SLOT_ROWS).start()

    def arrive(tile, c, s):
        for e in range(N_EXPERTS):
            @pl.when(tcnt_ref[tile * N_EXPERTS + e] > c * SLOT_ROWS)
            def _():
                slot_copy(s, e, 0).wait()

    @pl.when(j == 0)
    def _():
        ybuf[...] = jnp.zeros_like(ybuf)
        fetch(0, 0, 0)

    @pl.when(j + 1 < n_steps)
    def _():
        fetch(j + 1, 0, 1 - slot)

    def weights(c):
        cols = lax.broadcasted_iota(jnp.int32, (tm, N_SLOTS), 1)
        w = jnp.zeros((tm, N_SLOTS), F32)
        for k in range(TOP_K):
            w = jnp.where(cols == _slot_ids(idx_ref[:, k:k + 1], lrank_ref[:, k:k + 1], c), p_ref[:, k:k + 1], w)
        return w.astype(BF16)

    arrive(j, 0, slot)
    acc = _mm(weights(0), _unpack_bf16_pairs(ybuf[slot]))

    def extra_pass(c, acc):
        fetch(j, c, 2)
        arrive(j, c, 2)
        return acc + _mm(weights(c), _unpack_bf16_pairs(ybuf[2]))

    acc = lax.fori_loop(1, npass_ref[j], extra_pass, acc)
    m = mod_ref[pl.ds(b, 1), :]
    xo = x1_ref[0] + m[:, 5 * D_MODEL:6 * D_MODEL] * acc
    ms = jnp.mean(xo * xo, axis=-1, keepdims=True)
    o_ref[0] = xo * lax.rsqrt(ms + EPS) * g_ref[...]


def _combine(tile_start, tile_count, n_pass, idx_c, lrank_c, prob_c, y, x1, mod, g_fin):
    bsz, n, d = x1.shape
    tm = ROUTE_TILE
    nt = n // tm
    kern = functools.partial(_combine_kernel, tm=tm, nt=nt)
    tok = pl.BlockSpec((tm, TOP_K), lambda b, t, *_: (b * nt + t, 0))
    grid_spec = pltpu.PrefetchScalarGridSpec(
        num_scalar_prefetch=3,
        grid=(bsz, nt),
        in_specs=[tok, tok, tok,
                  pl.BlockSpec(memory_space=pl.ANY),
                  pl.BlockSpec((1, tm, d), lambda b, t, *_: (b, t, 0)),
                  pl.BlockSpec(mod.shape, lambda b, t, *_: (0, 0)),
                  pl.BlockSpec((1, d), lambda b, t, *_: (0, 0))],
        out_specs=pl.BlockSpec((1, tm, d), lambda b, t, *_: (b, t, 0)),
        scratch_shapes=[pltpu.VMEM((3, N_SLOTS, d // 2), jnp.uint32), pltpu.SemaphoreType.DMA((3,))],
    )
    return pl.pallas_call(
        kern,
        out_shape=jax.ShapeDtypeStruct((bsz, n, d), F32),
        grid_spec=grid_spec,
        compiler_params=_cparams(("arbitrary", "arbitrary"), VMEM_LIMIT),
        name="combine",
    )(tile_start, tile_count, n_pass, idx_c, lrank_c, prob_c, y, x1, mod, g_fin)


def _prep_w_in(w_in):
    q_a, kv, hq, ff, fb, hi, go, mm, mh = jnp.split(
        w_in, np.cumsum((MLA_Q_RANK, MLA_KV_RANK + MLA_ROPE) + (D_MODEL,) * 6).tolist(), axis=1)
    c_kv, kr = kv[:, :MLA_KV_RANK], kv[:, MLA_KV_RANK:]
    half = MLA_ROPE // 2
    kr_sw = jnp.concatenate([-kr[:, half:], kr[:, :half]], axis=1)
    w16 = jnp.concatenate([hq, hi, go, mm, mh, q_a, c_kv, kr, kr_sw], axis=1)
    w16 = jnp.pad(w16, ((0, 0), (0, N_PROJ16 - w16.shape[1])))
    return w16.astype(BF16), jnp.concatenate([ff, fb], axis=1).astype(BF16)


def _prep_w_q(w_q_b):
    w = w_q_b.reshape(MLA_Q_RANK, HEADS, MLA_QK)
    w = jnp.pad(w, ((0, 0), (0, 0), (0, MLA_QK_PAD - MLA_QK)))
    return w.reshape(MLA_Q_RANK, HEADS * MLA_QK_PAD).astype(BF16)


def _prep_w_kv(w_kv_b):
    w = w_kv_b.reshape(MLA_KV_RANK, HEADS, 2 * HEAD_DIM)
    k = w[:, :, :HEAD_DIM].reshape(MLA_KV_RANK, HEADS * HEAD_DIM)
    v = w[:, :, HEAD_DIM:].reshape(MLA_KV_RANK, HEADS * HEAD_DIM)
    return jnp.concatenate([k, v], axis=1).astype(BF16)


def _rope_tables(n_lat, n_ctx):
    f32 = np.float32
    rows = n_lat // GRID_W
    row = np.repeat(np.arange(rows), GRID_W).astype(f32)
    col = np.tile(np.arange(GRID_W), rows).astype(f32)
    n_freq = MLA_ROPE // 4
    inv = (f32(ROPE_BASE) ** (-np.arange(n_freq, dtype=f32) / f32(n_freq))).astype(f32)
    ang = np.concatenate([row[:, None] * inv, col[:, None] * inv], axis=-1).astype(f32)
    cos, sin = np.cos(ang).astype(f32), np.sin(ang).astype(f32)
    z64 = np.zeros((n_lat, 64), f32)
    cos_q = np.concatenate([cos, cos, z64], axis=1)
    sin_q = np.concatenate([-sin, sin, z64], axis=1)
    cos_k = np.concatenate([cos, cos, z64], axis=1)
    sin_k = np.concatenate([sin, sin, z64], axis=1)
    ctx_cos = np.concatenate([np.ones((n_ctx, 64), f32), np.zeros((n_ctx, 64), f32)], axis=1)
    cos_k = np.concatenate([cos_k, ctx_cos], axis=0)
    sin_k = np.concatenate([sin_k, np.zeros((n_ctx, 128), f32)], axis=0)
    return jnp.asarray(cos_q), jnp.asarray(sin_q), jnp.asarray(cos_k), jnp.asarray(sin_k)


def kernel(x, c, ctx, c_ctx, w_mod, b_mod, norm_mix_g, w_in, mla_q_norm_g, w_q_b, mla_kv_norm_g, w_kv_b,
           hg_lb_logits, hg_norm_g, w_out, norm_ffn_g, w_router, b_router, w_gate_up, b_gate_up, w_down,
           b_down, final_norm_g):
    bsz, n_lat, d = x.shape
    n_ctx = ctx.shape[1]
    assert d == D_MODEL and w_mod.shape[0] == 1
    assert n_lat % HG_CHUNK == 0 and n_ctx % HG_CHUNK == 0 and n_lat % GRID_W == 0
    t_tok = bsz * n_lat

    mod_rows = -(-(bsz + 1) // 8) * 8
    cc = jnp.concatenate([c, c_ctx[None, :], jnp.zeros((mod_rows - bsz - 1, d), F32)], axis=0)
    mod = _modulation(cc, w_mod[0], b_mod[0][None, :])

    xa = jnp.concatenate([x, ctx], axis=1)
    w16, w32 = _prep_w_in(w_in[0])
    p, pf = _inproj(xa, mod, norm_mix_g[0][None, :], w16, w32, n_lat)

    cos_q, sin_q, cos_k, sin_k = _rope_tables(n_lat, n_ctx)
    q = _mla_q(p, mla_q_norm_g[0][None, :], _prep_w_q(w_q_b[0]), cos_q, sin_q, n_lat)
    k, v = _mla_kv(p, mla_kv_norm_g[0][None, :], _prep_w_kv(w_kv_b[0]), cos_k, sin_k)
    y_mla = _attention(q, k, v)

    lb = jax.nn.softmax(hg_lb_logits.astype(F32), axis=1)[:, 0, :]
    o_f, o_b = _hgrn_scans(p, pf, lb.reshape(2, HEADS, 1, HEAD_DIM), n_lat)

    x1, h2, logits_t = _merge(y_mla, o_f, o_b, p, x, mod, hg_norm_g[0][None, :], w_out[0].astype(BF16),
                               norm_ffn_g[0][None, :], w_router[0].T, b_router[0][:, None])

    assert n_lat % ROUTE_TILE == 0
    idx_t, prob_t, lrank_t, base, cnt = _router(logits_t, ROUTE_TILE)

    n_tiles = t_tok // ROUTE_TILE
    counts = cnt[:, 0].astype(jnp.int32)
    base = base[:, :, 0].astype(jnp.int32)
    tile_count = jnp.concatenate([base[1:] - base[:-1], (counts - base[-1])[None, :]], axis=0)
    seg = (tile_count + ROW_ALIGN - 1) // ROW_ALIGN * ROW_ALIGN
    seg_end = jnp.cumsum(seg, axis=0)
    rows_e = seg_end[-1]
    padded = (rows_e + SLOT_ROWS + FFN_ROWS - 1) // FFN_ROWS * FFN_ROWS
    pad_end = jnp.cumsum(padded)
    pad_start = pad_end - padded
    max_rows = t_tok * TOP_K + N_EXPERTS * (n_tiles * (ROW_ALIGN - 1) + SLOT_ROWS + FFN_ROWS - 1)
    n_rows = -(-max_rows // FFN_ROWS) * FFN_ROWS
    nblk = n_rows // FFN_ROWS
    block_start = jnp.arange(nblk, dtype=jnp.int32) * FFN_ROWS
    block_e = jnp.minimum(jnp.sum(pad_end[None, :] <= block_start[:, None], axis=1), N_EXPERTS - 1).astype(jnp.int32)
    n_used = (pad_end[-1:] // FFN_ROWS).astype(jnp.int32)
    tile_start = (pad_start[None, :] + seg_end - seg).reshape(-1).astype(jnp.int32)
    n_pass = jnp.maximum(-(-jnp.max(tile_count, axis=1) // SLOT_ROWS), 1).astype(jnp.int32)
    tile_count = tile_count.reshape(-1)
    fill_start = jnp.minimum(pad_start + rows_e, n_rows - ZERO_ROWS).astype(jnp.int32)

    xs = _dispatch(tile_start, tile_count, n_pass, fill_start, n_used, idx_t, lrank_t, h2, n_rows)

    n_grp = 2 * D_EXPERT // GU_GROUP
    b_gu = b_gate_up[0].reshape(N_EXPERTS, n_grp, GU_GROUP // 2, 2).transpose(0, 1, 3, 2).reshape(N_EXPERTS, 1, 2 * D_EXPERT)
    y = _ffn(block_e, n_used, xs, w_gate_up[0], b_gu, w_down[0], b_down[0][:, None, :])

    return _combine(tile_start, tile_count, n_pass, idx_t.T, lrank_t.T, prob_t.T, y, x1, mod, final_norm_g[None, :])
```

```python
import functools

import jax
import jax.numpy as jnp
import numpy as np
from jax import lax
from jax.experimental import pallas as pl
from jax.experimental.pallas import tpu as pltpu

F32 = jnp.float32
BF16 = jnp.bfloat16

D_MODEL = 1024
EPS = 1e-6
LOG2_E = 1.4426950408889634
N_MOD = 6
GRID_W = 64
ROPE_BASE = 10000.0

HEADS = 8
HEAD_DIM = 128
MLA_ROPE = 64
MLA_QK = HEAD_DIM + MLA_ROPE
MLA_QK_PAD = 256
MLA_Q_RANK = 768
MLA_KV_RANK = 256

N_EXPERTS = 32
TOP_K = 4
D_EXPERT = 1024
SWIGLU_LIMIT = 7.0
SWIGLU_ALPHA = 1.702

HG_CHUNK = 128
FFN_ROWS = 512

COL_HQ, COL_I, COL_GO, COL_MM, COL_MH = (i * D_MODEL for i in range(5))
COL_QA = 5 * D_MODEL
COL_CKV = COL_QA + MLA_Q_RANK
COL_KR = COL_CKV + MLA_KV_RANK
PROJ16_TN = 1280
N_PROJ16 = -(-(COL_KR + 2 * MLA_ROPE) // PROJ16_TN) * PROJ16_TN
COL_FF, COL_FB = 0, D_MODEL
N_PROJ32 = 2 * D_MODEL
PROJ32_TN = 1024

VMEM_LIMIT = 56 * 1024 * 1024


def _cparams(sem, vmem=None):
    return pltpu.CompilerParams(dimension_semantics=sem, vmem_limit_bytes=vmem)


def _tile(n, pref, mult=8):
    best = None
    for t in range(mult, min(n, pref) + 1, mult):
        if n % t == 0:
            best = t
    assert best is not None, (n, pref, mult)
    return best


def _nt(a, b):
    return lax.dot_general(a, b, (((1,), (1,)), ((), ())), preferred_element_type=F32)


def _mm(a, b):
    return jnp.dot(a, b, preferred_element_type=F32)


def _split2(a):
    hi = a.astype(BF16)
    lo = (a - hi.astype(F32)).astype(BF16)
    return hi, lo


def _sigmoid(x):
    return 1.0 / (1.0 + jnp.exp(-x))


def _mod_kernel(c_ref, w_ref, b_ref, o_ref):
    c = c_ref[...]
    s = c * _sigmoid(c)
    s_hi, s_lo = _split2(s)
    w_hi, w_lo = _split2(w_ref[...])
    o_ref[...] = _mm(s_hi, w_hi) + _mm(s_lo, w_hi) + _mm(s_hi, w_lo) + b_ref[...]


def _modulation(cc, w_mod, b_mod):
    r, d = cc.shape
    n = w_mod.shape[1]
    tn = _tile(n, 1536, 128)
    return pl.pallas_call(
        _mod_kernel,
        out_shape=jax.ShapeDtypeStruct((r, n), F32),
        grid=(n // tn,),
        in_specs=[pl.BlockSpec((r, d), lambda j: (0, 0)),
                  pl.BlockSpec((d, tn), lambda j: (0, j)),
                  pl.BlockSpec((1, tn), lambda j: (0, j))],
        out_specs=pl.BlockSpec((r, tn), lambda j: (0, j)),
        compiler_params=_cparams(("parallel",), VMEM_LIMIT),
        name="mod",
    )(cc, w_mod, b_mod)


def _inproj_kernel(x_ref, mod_ref, g_ref, w16_ref, w32_ref, o16_ref, o32_ref, hn_ref, *, n_lat, tm, ctx_row, n16):
    b = pl.program_id(0)
    t = pl.program_id(1)
    j = pl.program_id(2)

    @pl.when(j == 0)
    def _():
        x = x_ref[0]
        ms = jnp.mean(x * x, axis=-1, keepdims=True)
        y = x * lax.rsqrt(ms + EPS) * g_ref[...]
        row = t * tm + lax.broadcasted_iota(jnp.int32, (tm, 1), 0)
        is_ctx = row >= n_lat
        m_lat = mod_ref[pl.ds(b, 1), :]
        m_ctx = mod_ref[pl.ds(ctx_row, 1), :]
        shift = jnp.where(is_ctx, m_ctx[:, 0:D_MODEL], m_lat[:, 0:D_MODEL])
        scale = jnp.where(is_ctx, m_ctx[:, D_MODEL:2 * D_MODEL], m_lat[:, D_MODEL:2 * D_MODEL])
        hn_ref[...] = (y * (1.0 + scale) + shift).astype(BF16)

    @pl.when(j < n16)
    def _():
        o16_ref[0] = _mm(hn_ref[...], w16_ref[...]).astype(BF16)

    @pl.when(j >= n16)
    def _():
        o32_ref[0] = _mm(hn_ref[...], w32_ref[...])


def _inproj(xa, mod, g, w16, w32, n_lat):
    bsz, rows, d = xa.shape
    tm = _tile(rows, 1152)
    n16 = N_PROJ16 // PROJ16_TN
    n32 = N_PROJ32 // PROJ32_TN

    def j16(j):
        return jnp.minimum(j, n16 - 1)

    def j32(j):
        return jnp.maximum(j - n16, 0)

    kern = functools.partial(_inproj_kernel, n_lat=n_lat, tm=tm, ctx_row=bsz, n16=n16)
    return pl.pallas_call(
        kern,
        out_shape=(jax.ShapeDtypeStruct((bsz, rows, N_PROJ16), BF16),
                   jax.ShapeDtypeStruct((bsz, rows, N_PROJ32), F32)),
        grid=(bsz, rows // tm, n16 + n32),
        in_specs=[pl.BlockSpec((1, tm, d), lambda b, t, j: (b, t, 0)),
                  pl.BlockSpec(mod.shape, lambda b, t, j: (0, 0)),
                  pl.BlockSpec((1, d), lambda b, t, j: (0, 0)),
                  pl.BlockSpec((d, PROJ16_TN), lambda b, t, j: (0, j16(j))),
                  pl.BlockSpec((d, PROJ32_TN), lambda b, t, j: (0, j32(j)))],
        out_specs=(pl.BlockSpec((1, tm, PROJ16_TN), lambda b, t, j: (b, t, j16(j))),
                   pl.BlockSpec((1, tm, PROJ32_TN), lambda b, t, j: (b, t, j32(j)))),
        scratch_shapes=[pltpu.VMEM((tm, d), BF16)],
        compiler_params=_cparams(("parallel", "parallel", "arbitrary"), VMEM_LIMIT),
        name="inproj",
    )(xa, mod, g, w16, w32)


def _mlaq_kernel(a0_ref, a1_ref, a2_ref, g_ref, w_ref, cos_ref, sin_ref, o_ref, *, tm):
    parts = [a0_ref[0].astype(F32), a1_ref[0].astype(F32), a2_ref[0].astype(F32)]
    ss = sum(jnp.sum(p * p, axis=-1, keepdims=True) for p in parts)
    r = lax.rsqrt(ss * (1.0 / MLA_Q_RANK) + EPS)
    acc = None
    for j, p in enumerate(parts):
        hj = (p * r * g_ref[:, j * 256:(j + 1) * 256]).astype(BF16)
        d = _mm(hj, w_ref[j * 256:(j + 1) * 256, :])
        acc = d if acc is None else acc + d
    lane = lax.broadcasted_iota(jnp.int32, (tm, HEAD_DIM), 1)
    scale = MLA_QK ** -0.5 * LOG2_E
    cos = cos_ref[...]
    sin = sin_ref[...]
    for h in range(HEADS):
        nope = acc[:, h * MLA_QK_PAD:h * MLA_QK_PAD + HEAD_DIM]
        rp = acc[:, h * MLA_QK_PAD + HEAD_DIM:(h + 1) * MLA_QK_PAD]
        swapped = jnp.where(lane < 32, pltpu.roll(rp, 96, 1), pltpu.roll(rp, 32, 1))
        rot = rp * cos + swapped * sin
        o_ref[0, h, :, 0:HEAD_DIM] = (nope * scale).astype(BF16)
        o_ref[0, h, :, HEAD_DIM:MLA_QK_PAD] = (rot * scale).astype(BF16)


def _mla_q(p, g, w, cos_q, sin_q, n_lat):
    bsz = p.shape[0]
    tm = _tile(n_lat, 512)
    cb = COL_QA // 256
    kern = functools.partial(_mlaq_kernel, tm=tm)
    return pl.pallas_call(
        kern,
        out_shape=jax.ShapeDtypeStruct((bsz, HEADS, n_lat, MLA_QK_PAD), BF16),
        grid=(bsz, n_lat // tm),
        in_specs=[pl.BlockSpec((1, tm, 256), lambda b, t: (b, t, cb)),
                  pl.BlockSpec((1, tm, 256), lambda b, t: (b, t, cb + 1)),
                  pl.BlockSpec((1, tm, 256), lambda b, t: (b, t, cb + 2)),
                  pl.BlockSpec((1, MLA_Q_RANK), lambda b, t: (0, 0)),
                  pl.BlockSpec(w.shape, lambda b, t: (0, 0)),
                  pl.BlockSpec((tm, HEAD_DIM), lambda b, t: (t, 0)),
                  pl.BlockSpec((tm, HEAD_DIM), lambda b, t: (t, 0))],
        out_specs=pl.BlockSpec((1, HEADS, tm, MLA_QK_PAD), lambda b, t: (b, 0, t, 0)),
        compiler_params=_cparams(("parallel", "parallel"), VMEM_LIMIT),
        name="mla_q",
    )(p, p, p, g, w, cos_q, sin_q)


def _mlakv_kernel(c_ref, kr_ref, g_ref, w_ref, cos_ref, sin_ref, k_ref, v_ref):
    c = c_ref[0].astype(F32)
    ms = jnp.mean(c * c, axis=-1, keepdims=True)
    hn = (c * lax.rsqrt(ms + EPS) * g_ref[...]).astype(BF16)
    kv = _mm(hn, w_ref[...])
    grp = kr_ref[0].astype(F32)
    rot = (grp * cos_ref[...] + pltpu.roll(grp, 64, 1) * sin_ref[...]).astype(BF16)
    for h in range(HEADS):
        k_ref[0, h, :, 0:HEAD_DIM] = kv[:, h * HEAD_DIM:(h + 1) * HEAD_DIM].astype(BF16)
        k_ref[0, h, :, HEAD_DIM:MLA_QK_PAD] = rot
        v_ref[0, h] = kv[:, D_MODEL + h * HEAD_DIM:D_MODEL + (h + 1) * HEAD_DIM].astype(BF16)


def _mla_kv(p, g, w, cos_k, sin_k):
    bsz, rows, _ = p.shape
    tm = _tile(rows, 768)
    return pl.pallas_call(
        _mlakv_kernel,
        out_shape=(jax.ShapeDtypeStruct((bsz, HEADS, rows, MLA_QK_PAD), BF16),
                   jax.ShapeDtypeStruct((bsz, HEADS, rows, HEAD_DIM), BF16)),
        grid=(bsz, rows // tm),
        in_specs=[pl.BlockSpec((1, tm, MLA_KV_RANK), lambda b, t: (b, t, COL_CKV // MLA_KV_RANK)),
                  pl.BlockSpec((1, tm, 128), lambda b, t: (b, t, COL_KR // 128)),
                  pl.BlockSpec((1, MLA_KV_RANK), lambda b, t: (0, 0)),
                  pl.BlockSpec(w.shape, lambda b, t: (0, 0)),
                  pl.BlockSpec((tm, 128), lambda b, t: (t, 0)),
                  pl.BlockSpec((tm, 128), lambda b, t: (t, 0))],
        out_specs=(pl.BlockSpec((1, HEADS, tm, MLA_QK_PAD), lambda b, t: (b, 0, t, 0)),
                   pl.BlockSpec((1, HEADS, tm, HEAD_DIM), lambda b, t: (b, 0, t, 0))),
        compiler_params=_cparams(("parallel", "parallel"), VMEM_LIMIT),
        name="mla_kv",
    )(p, p, g, w, cos_k, sin_k)


def _attn_kernel(q_ref, k_ref, v_ref, o_ref, *, tq, sub):
    k = k_ref[0, 0]
    v = v_ref[0, 0]
    v1 = jnp.concatenate([v, jnp.ones_like(v)], axis=1)

    def sub_tile(r):
        rows = slice(r * sub, (r + 1) * sub)
        s = _nt(q_ref[0, 0, rows, :], k)
        yield
        m = jnp.max(s, axis=-1, keepdims=True)
        p = jnp.exp2(s - m).astype(BF16)
        yield
        ol = _mm(p, v1)
        o_ref[0, rows, :] = (ol[:, 0:HEAD_DIM] * (1.0 / ol[:, HEAD_DIM:HEAD_DIM + 1])).astype(BF16)

    _round_robin([sub_tile(r) for r in range(tq // sub)])


def _attention(q, k, v):
    bsz, _, n, _ = q.shape
    m = k.shape[2]
    tq = _tile(n, 2048)
    sub = _tile(tq, 512)
    kern = functools.partial(_attn_kernel, tq=tq, sub=sub)
    return pl.pallas_call(
        kern,
        out_shape=jax.ShapeDtypeStruct((bsz, n, HEADS * HEAD_DIM), BF16),
        grid=(bsz, HEADS, n // tq),
        in_specs=[pl.BlockSpec((1, 1, tq, MLA_QK_PAD), lambda b, h, t: (b, h, t, 0)),
                  pl.BlockSpec((1, 1, m, MLA_QK_PAD), lambda b, h, t: (b, h, 0, 0)),
                  pl.BlockSpec((1, 1, m, HEAD_DIM), lambda b, h, t: (b, h, 0, 0))],
        out_specs=pl.BlockSpec((1, tq, HEAD_DIM), lambda b, h, t: (b, t, h)),
        compiler_params=_cparams(("parallel", "parallel", "arbitrary"), VMEM_LIMIT),
        name="attention",
    )(q, k, v)


def _level_ref(cum, blk, reverse):
    c = cum.shape[0]
    half = blk // 2
    r = half if reverse else half - 1
    if blk >= 8:
        x = cum.reshape(c // blk, blk, HEAD_DIM)
        e = jnp.broadcast_to(x[:, r:r + 1, :], x.shape)
        return e.reshape(c, HEAD_DIM)
    x = cum.reshape(c // 8, 8, HEAD_DIM)
    sub = lax.broadcasted_iota(jnp.int32, x.shape, 1)
    e = None
    for jb in range(8 // blk):
        cand = jnp.broadcast_to(x[:, jb * blk + r:jb * blk + r + 1, :], x.shape)
        e = cand if e is None else jnp.where(sub >= jb * blk, cand, e)
    return e.reshape(c, HEAD_DIM)


HG_SUB = 32
HG_SUB_LEVEL = HG_SUB.bit_length() - 1
HG_MAX_EXP2 = 100.0


def _hgrn_state(z, v, lb, tri, st, *, chunk, reverse):
    f = lb + (1.0 - lb) * _sigmoid(z)
    kb = (1.0 - f).astype(BF16)
    g = jnp.log(f) * LOG2_E

    g_hi, g_lo = _split2(g)
    yield
    cum = _mm(tri, jnp.concatenate([g_hi, g_lo], axis=0))
    yield
    last = 0 if reverse else chunk - 1
    tot = cum[last:last + 1, :]
    kt = kb * jnp.exp2(tot - cum).astype(BF16)
    vt = v.astype(F32).T.astype(BF16)
    yield
    st_new = st * jnp.exp2(tot) + _mm(vt, kt)
    return kb, cum, st_new


def _sub_block_decay(cum, reverse):
    c = cum.shape[0]
    x = cum.reshape(c // HG_SUB, HG_SUB, HEAD_DIM)
    zero = jnp.zeros((1, 1, HEAD_DIM), F32)
    if reverse:
        edge = jnp.concatenate([x[1:, 0:1, :], zero], axis=0)
    else:
        edge = jnp.concatenate([zero, x[:-1, HG_SUB - 1:HG_SUB, :]], axis=0)
    return (x - edge).reshape(c, HEAD_DIM)


def _hgrn_readout(q, v, kb, cum, sub, lvl, st, *, chunk, reverse, shared):
    q = q.astype(F32)
    qb = (q * _sigmoid(q) * (HEAD_DIM ** -0.5)).astype(BF16)
    yield
    if shared:
        att = jnp.where(lvl == 0, _nt(qb * jnp.exp2(sub).astype(BF16), kb * jnp.exp2(-sub).astype(BF16)).astype(BF16),
                        jnp.zeros((), BF16))
        first = HG_SUB_LEVEL + 1
    else:
        att = jnp.where(lvl == 0, _nt(qb, kb).astype(BF16), jnp.zeros((), BF16))
        first = 1
    for lv in range(first, chunk.bit_length()):
        zrel = cum - _level_ref(cum, 1 << lv, reverse)
        neg_abs = pltpu.bitcast(pltpu.bitcast(zrel, jnp.uint32) | jnp.uint32(0x80000000), F32)
        e = jnp.exp2(neg_abs).astype(BF16)
        yield
        att = jnp.where(lvl == lv, _nt(qb * e, kb * e).astype(BF16), att)
        yield
    return _nt(qb * jnp.exp2(cum).astype(BF16), st.astype(BF16)) + _mm(att, v.astype(BF16))


def _round_robin(gens):
    out = [None] * len(gens)
    active = list(range(len(gens)))
    while active:
        for i in list(active):
            try:
                next(gens[i])
            except StopIteration as e:
                out[i] = e.value
                active.remove(i)
    return out


def _hgrn_kernel(qf_ref, zf_ref, if_ref, qb_ref, zb_ref, ib_ref, lb_ref, tri_ref, lvl_ref, of_ref, ob_ref, st_ref,
                 *, chunk, group, n_ctx_chunks):
    step = pl.program_id(2)

    @pl.when(step == 0)
    def _():
        st_ref[...] = jnp.zeros_like(st_ref)

    ins = ((qf_ref, zf_ref, if_ref), (qb_ref, zb_ref, ib_ref))
    outs = (of_ref, ob_ref)
    chains = [(d, j) for j in range(group) for d in range(2)]

    def cols(j):
        return slice(j * HEAD_DIM, (j + 1) * HEAD_DIM)

    def advance(want_out):
        old = [st_ref[d, j] for d, j in chains]
        parts = _round_robin([_hgrn_state(ins[d][1][0, :, cols(j)], ins[d][2][0, :, cols(j)], lb_ref[d, j],
                                          tri_ref[d], st, chunk=chunk, reverse=bool(d))
                              for (d, j), st in zip(chains, old)])
        for (d, j), (_, _, st_new) in zip(chains, parts):
            st_ref[d, j] = st_new
        if not want_out:
            return
        subs = [_sub_block_decay(cum, bool(d)) for (d, j), (_, cum, _) in zip(chains, parts)]
        low = subs[0]
        for s in subs[1:]:
            low = jnp.minimum(low, s)
        low = jnp.min(jnp.min(low, axis=0, keepdims=True), axis=1, keepdims=True)
        in_range = low[0, 0] >= -HG_MAX_EXP2

        def readouts(shared):
            os_ = _round_robin([_hgrn_readout(ins[d][0][0, :, cols(j)], ins[d][2][0, :, cols(j)], kb, cum, sub,
                                              lvl_ref[d, int(shared)], st, chunk=chunk, reverse=bool(d), shared=shared)
                                for (d, j), (kb, cum, _), sub, st in zip(chains, parts, subs, old)])
            for (d, j), o in zip(chains, os_):
                outs[d][0, j] = o.astype(BF16)

        @pl.when(in_range)
        def _():
            readouts(True)

        @pl.when(jnp.logical_not(in_range))
        def _():
            readouts(False)

    @pl.when(step < n_ctx_chunks)
    def _():
        advance(False)

    @pl.when(step >= n_ctx_chunks)
    def _():
        advance(True)


def _hgrn_consts(chunk, reverse):
    t = np.arange(chunk)[:, None]
    s = np.arange(chunk)[None, :]
    x = t ^ s
    bitlen = np.zeros_like(x)
    for b in range(chunk.bit_length()):
        bitlen = np.where(x >> b > 0, b + 1, bitlen)
    valid = (s > t) if reverse else (t > s)
    lvl = np.where(t == s, 0, np.where(valid, bitlen, -1)).astype(np.float32)
    lvl_shared = np.where((lvl >= 0) & (lvl <= HG_SUB_LEVEL), 0, lvl)
    tri = ((s >= t) if reverse else (t >= s)).astype(np.float32)
    return np.concatenate([tri, tri], axis=1), np.stack([lvl, lvl_shared])


HG_GROUP = 8


def _hgrn_scans(p, pf, lb, n_lat):
    bsz, rows, _ = p.shape
    c = HG_CHUNK
    grp = HG_GROUP
    n_chunks = rows // c
    n_lat_c = n_lat // c
    n_ctx_c = n_chunks - n_lat_c
    consts = [_hgrn_consts(c, False), _hgrn_consts(c, True)]
    tri = jnp.asarray(np.stack([consts[0][0], consts[1][0]]), dtype=BF16)
    lvl = jnp.asarray(np.stack([consts[0][1], consts[1][1]]), dtype=BF16)

    def cidx_f(i):
        return jnp.where(i < n_ctx_c, n_lat_c + i, i - n_ctx_c)

    def cidx_b(i):
        return n_chunks - 1 - i

    def col_spec(col, cidx):
        return pl.BlockSpec((1, c, grp * HEAD_DIM), lambda b, h, i: (b, cidx(i), col // (grp * HEAD_DIM) + h))

    def out_spec(cidx):
        return pl.BlockSpec((1, grp, c, HEAD_DIM), lambda b, h, i: (b, h, cidx(jnp.maximum(i, n_ctx_c)), 0))

    o_shape = jax.ShapeDtypeStruct((bsz, HEADS, n_lat, HEAD_DIM), BF16)
    kern = functools.partial(_hgrn_kernel, chunk=c, group=grp, n_ctx_chunks=n_ctx_c)
    return pl.pallas_call(
        kern,
        out_shape=(o_shape, o_shape),
        grid=(bsz, HEADS // grp, n_chunks),
        in_specs=[col_spec(COL_HQ, cidx_f), col_spec(COL_FF, cidx_f), col_spec(COL_I, cidx_f),
                  col_spec(COL_HQ, cidx_b), col_spec(COL_FB, cidx_b), col_spec(COL_I, cidx_b),
                  pl.BlockSpec((2, grp, 1, HEAD_DIM), lambda b, h, i: (0, h, 0, 0)),
                  pl.BlockSpec((2, c, 2 * c), lambda b, h, i: (0, 0, 0)),
                  pl.BlockSpec((2, 2, c, c), lambda b, h, i: (0, 0, 0, 0))],
        out_specs=(out_spec(cidx_f), out_spec(cidx_b)),
        scratch_shapes=[pltpu.VMEM((2, grp, HEAD_DIM, HEAD_DIM), F32)],
        compiler_params=_cparams(("parallel", "parallel", "arbitrary"), VMEM_LIMIT),
        name="hgrn",
    )(p, pf, p, p, pf, p, lb, tri, lvl)


def _merge_kernel(ym_ref, of_ref, ob_ref, go_ref, gm_ref, gh_ref, x_ref, mod_ref, hgg_ref, wout_ref,
                  gffn_ref, wr_ref, br_ref, x1_ref, h2_ref, lg_ref, y_scr):
    b = pl.program_id(0)
    for h in range(HEADS):
        sl = slice(h * HEAD_DIM, (h + 1) * HEAD_DIM)
        o = of_ref[0, h].astype(F32) + ob_ref[0, h].astype(F32)
        ms = jnp.mean(o * o, axis=-1, keepdims=True)
        g = go_ref[0, :, sl].astype(F32)
        yh = o * lax.rsqrt(ms + EPS) * hgg_ref[...] * (g * _sigmoid(g))
        y = (_sigmoid(gm_ref[0, :, sl].astype(F32)) * ym_ref[0, :, sl].astype(F32)
             + _sigmoid(gh_ref[0, :, sl].astype(F32)) * yh)
        y_scr[:, sl] = y.astype(BF16)
    mix = _mm(y_scr[...], wout_ref[...])
    m = mod_ref[pl.ds(b, 1), :]
    x1 = x_ref[0] + m[:, 2 * D_MODEL:3 * D_MODEL] * mix
    x1_ref[0] = x1
    ms = jnp.mean(x1 * x1, axis=-1, keepdims=True)
    h2 = (x1 * lax.rsqrt(ms + EPS) * gffn_ref[...]) * (1.0 + m[:, 4 * D_MODEL:5 * D_MODEL]) + m[:, 3 * D_MODEL:4 * D_MODEL]
    h_hi = h2.astype(BF16)
    h2_ref[...] = h_hi
    h_lo = (h2 - h_hi.astype(F32)).astype(BF16)
    w_hi, w_lo = _split2(wr_ref[...])
    lg_ref[...] = _nt(w_hi, h_hi) + _nt(w_lo, h_hi) + _nt(w_hi, h_lo) + br_ref[...]


def _merge(y_mla, o_f, o_b, p, x, mod, hg_g, w_out, g_ffn, w_r_t, b_r):
    bsz, n, d = x.shape
    tm = _tile(n, 256, 128)
    nt = n // tm

    def pcol(col):
        return pl.BlockSpec((1, tm, d), lambda b, t: (b, t, col // d))

    tok = lambda b, t: (b, t, 0)
    const2 = lambda b, t: (0, 0)
    return pl.pallas_call(
        _merge_kernel,
        out_shape=(jax.ShapeDtypeStruct((bsz, n, d), F32),
                   jax.ShapeDtypeStruct((bsz * n, d), BF16),
                   jax.ShapeDtypeStruct((N_EXPERTS, bsz * n), F32)),
        grid=(bsz, nt),
        in_specs=[pl.BlockSpec((1, tm, d), tok),
                  pl.BlockSpec((1, HEADS, tm, HEAD_DIM), lambda b, t: (b, 0, t, 0)),
                  pl.BlockSpec((1, HEADS, tm, HEAD_DIM), lambda b, t: (b, 0, t, 0)),
                  pcol(COL_GO), pcol(COL_MM), pcol(COL_MH),
                  pl.BlockSpec((1, tm, d), tok),
                  pl.BlockSpec(mod.shape, const2),
                  pl.BlockSpec((1, HEAD_DIM), const2),
                  pl.BlockSpec((d, d), const2),
                  pl.BlockSpec((1, d), const2),
                  pl.BlockSpec((N_EXPERTS, d), const2),
                  pl.BlockSpec((N_EXPERTS, 1), const2)],
        out_specs=(pl.BlockSpec((1, tm, d), tok),
                   pl.BlockSpec((tm, d), lambda b, t: (b * nt + t, 0)),
                   pl.BlockSpec((N_EXPERTS, tm), lambda b, t: (0, b * nt + t))),
        scratch_shapes=[pltpu.VMEM((tm, d), BF16)],
        compiler_params=_cparams(("parallel", "parallel"), VMEM_LIMIT),
        name="merge",
    )(y_mla, o_f, o_b, p, p, p, x, mod, hg_g, w_out, g_ffn, w_r_t, b_r)


def _router_kernel(lg_ref, upper_ref, idx_ref, prob_ref, rank_ref, base_ref, cnt_ref, run_ref, *, tm):
    @pl.when(pl.program_id(0) == 0)
    def _():
        run_ref[...] = jnp.zeros_like(run_ref)

    l = lg_ref[...]
    eidx = lax.broadcasted_iota(jnp.int32, (N_EXPERTS, tm), 0)
    vals, sel = [], []
    for _ in range(TOP_K):
        m = jnp.max(l, axis=0, keepdims=True)
        first = jnp.min(jnp.where(l == m, eidx, N_EXPERTS), axis=0, keepdims=True)
        vals.append(m)
        sel.append(first)
        l = jnp.where(eidx == first, -jnp.inf, l)
    ex = [jnp.exp(v - vals[0]) for v in vals]
    inv = 1.0 / (ex[0] + ex[1] + ex[2] + ex[3])
    onehot = jnp.zeros((N_EXPERTS, tm), F32)
    for k in range(TOP_K):
        onehot = onehot + jnp.where(eidx == sel[k], 1.0, 0.0)
    before = _mm(onehot.astype(BF16), upper_ref[...])
    base_ref[0] = run_ref[...]
    for k in range(TOP_K):
        idx_ref[k:k + 1, :] = sel[k]
        prob_ref[k:k + 1, :] = ex[k] * inv
        rank_ref[k:k + 1, :] = jnp.sum(jnp.where(eidx == sel[k], before, 0.0), axis=0, keepdims=True).astype(jnp.int32)
    run_ref[...] = run_ref[...] + jnp.sum(onehot, axis=1, keepdims=True)
    cnt_ref[...] = run_ref[...]


def _router(logits_t, tm):
    _, t = logits_t.shape
    upper = jnp.asarray(np.triu(np.ones((tm, tm), np.float32), 1), dtype=BF16)
    kern = functools.partial(_router_kernel, tm=tm)
    tok = pl.BlockSpec((TOP_K, tm), lambda i: (0, i))
    return pl.pallas_call(
        kern,
        out_shape=(jax.ShapeDtypeStruct((TOP_K, t), jnp.int32),
                   jax.ShapeDtypeStruct((TOP_K, t), F32),
                   jax.ShapeDtypeStruct((TOP_K, t), jnp.int32),
                   jax.ShapeDtypeStruct((t // tm, N_EXPERTS, 128), F32),
                   jax.ShapeDtypeStruct((N_EXPERTS, 128), F32)),
        grid=(t // tm,),
        in_specs=[pl.BlockSpec((N_EXPERTS, tm), lambda i: (0, i)),
                  pl.BlockSpec((tm, tm), lambda i: (0, 0))],
        out_specs=(tok, tok, tok, pl.BlockSpec((1, N_EXPERTS, 128), lambda i: (i, 0, 0)),
                   pl.BlockSpec((N_EXPERTS, 128), lambda i: (0, 0))),
        scratch_shapes=[pltpu.VMEM((N_EXPERTS, 128), F32)],
        compiler_params=_cparams(("arbitrary",), VMEM_LIMIT),
        name="router",
    )(logits_t, upper)


ROUTE_TILE = 512
ROW_ALIGN = 8
SLOT_ROWS = 88
N_SLOTS = N_EXPERTS * SLOT_ROWS
ZERO_ROWS = FFN_ROWS + SLOT_ROWS


def _slot_ids(idx, lrank, c):
    r = lrank - c * SLOT_ROWS
    return jnp.where(jnp.logical_and(r >= 0, r < SLOT_ROWS), idx * SLOT_ROWS + r, -1)


def _pack_bf16_pairs(x):
    u = pltpu.bitcast(x, jnp.uint32)
    w = x.shape[1] // 2
    return (u[:, 0:w] >> 16) | u[:, w:2 * w]


def _unpack_bf16_pairs(u):
    lo = pltpu.bitcast(u << 16, F32).astype(BF16)
    hi = pltpu.bitcast(u & jnp.uint32(0xFFFF0000), F32).astype(BF16)
    return jnp.concatenate([lo, hi], axis=1)


def _dispatch_kernel(tstart_ref, tcnt_ref, npass_ref, fill_ref, nused_ref, idx_ref, lrank_ref, h2_ref, xs_ref,
                     xbuf, zbuf, sem, zsem, state, *, tm, n_blocks):
    j = pl.program_id(0)

    def slot_copy(slot, e, row):
        return pltpu.make_async_copy(xbuf.at[slot, pl.ds(e * SLOT_ROWS, SLOT_ROWS)],
                                     xs_ref.at[pl.ds(pl.multiple_of(row, ROW_ALIGN), SLOT_ROWS)], sem.at[slot])

    def wait_pass(slot, n):
        def body(i, carry):
            slot_copy(slot, 0, 0).wait()
            return carry
        lax.fori_loop(0, n, body, 0)

    @pl.when(j == 0)
    def _():
        zbuf[...] = jnp.zeros_like(zbuf)
        state[0] = 0
        state[1] = 0
        for parity in range(2):
            for e in range(parity, N_EXPERTS, 2):
                pltpu.make_async_copy(zbuf, xs_ref.at[pl.ds(pl.multiple_of(fill_ref[e], ROW_ALIGN), ZERO_ROWS)],
                                      zsem).start()
            for e in range(parity, N_EXPERTS, 2):
                pltpu.make_async_copy(zbuf, xs_ref.at[pl.ds(0, ZERO_ROWS)], zsem).wait()

        def tail(i, carry):
            pltpu.make_async_copy(zbuf.at[pl.ds(0, FFN_ROWS)],
                                  xs_ref.at[pl.ds(pl.multiple_of(i * FFN_ROWS, ROW_ALIGN), FFN_ROWS)], zsem).start()
            return carry
        lax.fori_loop(nused_ref[0], n_blocks, tail, 0)

        def tail_wait(i, carry):
            pltpu.make_async_copy(zbuf.at[pl.ds(0, FFN_ROWS)], xs_ref.at[pl.ds(0, FFN_ROWS)], zsem).wait()
            return carry
        lax.fori_loop(nused_ref[0], n_blocks, tail_wait, 0)

    def one_pass(c, carry):
        rows = lax.broadcasted_iota(jnp.int32, (N_SLOTS, tm), 0)
        sel = jnp.zeros((N_SLOTS, tm), F32)
        for k in range(TOP_K):
            sel = jnp.where(rows == _slot_ids(idx_ref[k:k + 1, :], lrank_ref[k:k + 1, :], c), 1.0, sel)
        packed = _pack_bf16_pairs(_mm(sel.astype(BF16), h2_ref[...]))
        p = state[0]
        slot = p % 2
        wait_pass(1 - slot, state[1])
        xbuf[slot] = packed
        n = 0
        for e in range(N_EXPERTS):
            live = tcnt_ref[j * N_EXPERTS + e] > c * SLOT_ROWS

            @pl.when(live)
            def _():
                slot_copy(slot, e, tstart_ref[j * N_EXPERTS + e] + c * SLOT_ROWS).start()
            n = n + live.astype(jnp.int32)
        state[0] = p + 1
        state[1] = n
        return carry

    lax.fori_loop(0, npass_ref[j], one_pass, 0)

    @pl.when(j == pl.num_programs(0) - 1)
    def _():
        wait_pass((state[0] - 1) % 2, state[1])


def _dispatch(tile_start, tile_count, n_pass, fill_start, n_used, idx_t, lrank_t, h2, n_rows):
    t, d = h2.shape
    tm = ROUTE_TILE
    w = d // 2
    n_blocks = n_rows // FFN_ROWS
    kern = functools.partial(_dispatch_kernel, tm=tm, n_blocks=n_blocks)
    tok = pl.BlockSpec((TOP_K, tm), lambda i, *_: (0, i))
    grid_spec = pltpu.PrefetchScalarGridSpec(
        num_scalar_prefetch=5,
        grid=(t // tm,),
        in_specs=[tok, tok, pl.BlockSpec((tm, d), lambda i, *_: (i, 0))],
        out_specs=pl.BlockSpec(memory_space=pl.ANY),
        scratch_shapes=[pltpu.VMEM((2, N_SLOTS, w), jnp.uint32), pltpu.VMEM((ZERO_ROWS, w), jnp.uint32),
                        pltpu.SemaphoreType.DMA((2,)), pltpu.SemaphoreType.DMA, pltpu.SMEM((2,), jnp.int32)],
    )
    return pl.pallas_call(
        kern,
        out_shape=jax.ShapeDtypeStruct((n_rows, w), jnp.uint32),
        grid_spec=grid_spec,
        compiler_params=_cparams(("arbitrary",), VMEM_LIMIT),
        name="dispatch",
    )(tile_start, tile_count, n_pass, fill_start, n_used, idx_t, lrank_t, h2)


GU_GROUP = 256


def _ffn_kernel(be_ref, nu_ref, x_ref, wgu_ref, bgu_ref, wd_ref, bd_ref, perm_ref, y_ref, wgu_s, wd_s):
    i = pl.program_id(0)
    live = i < nu_ref[0]
    new_expert = jnp.logical_or(i == 0, be_ref[i] != be_ref[jnp.maximum(i - 1, 0)])

    @pl.when(jnp.logical_and(live, new_expert))
    def _():
        for g in range(2 * D_EXPERT // GU_GROUP):
            sl = slice(g * GU_GROUP, (g + 1) * GU_GROUP)
            wgu_s[:, sl] = _mm(wgu_ref[0, :, sl].astype(BF16), perm_ref[...]).astype(BF16)
        wd_s[...] = wd_ref[0].astype(BF16)

    @pl.when(live)
    def _():
        x = _unpack_bf16_pairs(x_ref[...])
        gu = _mm(x, wgu_s[...]) + bgu_ref[0]
        half = GU_GROUP // 2
        n_grp = 2 * D_EXPERT // GU_GROUP
        glu = jnp.concatenate([gu[:, g * GU_GROUP:g * GU_GROUP + half] for g in range(n_grp)], axis=1)
        lin = jnp.concatenate([gu[:, g * GU_GROUP + half:(g + 1) * GU_GROUP] for g in range(n_grp)], axis=1)
        glu = jnp.minimum(glu, SWIGLU_LIMIT)
        lin = jnp.clip(lin, -SWIGLU_LIMIT, SWIGLU_LIMIT)
        act = glu * _sigmoid(SWIGLU_ALPHA * glu) * (lin + 1.0)
        y = _mm(act.astype(BF16), wd_s[...]) + bd_ref[0]
        y_ref[...] = _pack_bf16_pairs(y.astype(BF16).astype(F32))

    @pl.when(jnp.logical_not(live))
    def _():
        y_ref[...] = jnp.zeros_like(y_ref)


def _ffn(block_e, n_used, xs, w_gu, b_gu, w_d, b_d):
    n_rows, w = xs.shape
    bm = FFN_ROWS
    nblk = n_rows // bm
    d = w * 2
    half = GU_GROUP // 2
    perm = np.zeros((GU_GROUP, GU_GROUP), np.float32)
    perm[2 * np.arange(half), np.arange(half)] = 1.0
    perm[2 * np.arange(half) + 1, half + np.arange(half)] = 1.0

    def xmap(i, be, nu):
        return (jnp.minimum(i, nu[0] - 1), 0)

    def wmap(i, be, nu):
        return (be[i], 0, 0)

    grid_spec = pltpu.PrefetchScalarGridSpec(
        num_scalar_prefetch=2,
        grid=(nblk,),
        in_specs=[pl.BlockSpec((bm, w), xmap),
                  pl.BlockSpec((1, d, 2 * D_EXPERT), wmap),
                  pl.BlockSpec((1, 1, 2 * D_EXPERT), wmap),
                  pl.BlockSpec((1, D_EXPERT, d), wmap),
                  pl.BlockSpec((1, 1, d), wmap),
                  pl.BlockSpec((GU_GROUP, GU_GROUP), lambda i, be, nu: (0, 0))],
        out_specs=pl.BlockSpec((bm, w), lambda i, be, nu: (i, 0)),
        scratch_shapes=[pltpu.VMEM((d, 2 * D_EXPERT), BF16), pltpu.VMEM((D_EXPERT, d), BF16)],
    )
    return pl.pallas_call(
        _ffn_kernel,
        out_shape=jax.ShapeDtypeStruct((n_rows, w), jnp.uint32),
        grid_spec=grid_spec,
        compiler_params=_cparams(("arbitrary",), VMEM_LIMIT),
        name="ffn",
    )(block_e, n_used, xs, w_gu, b_gu, w_d, b_d, jnp.asarray(perm, dtype=BF16))


def _combine_kernel(tstart_ref, tcnt_ref, npass_ref, idx_ref, lrank_ref, p_ref, y_ref, x1_ref, mod_ref, g_ref, o_ref,
                    ybuf, sem, *, tm, nt):
    b = pl.program_id(0)
    j = b * nt + pl.program_id(1)
    n_steps = pl.num_programs(0) * nt
    slot = j % 2

    def slot_copy(s, e, row):
        return pltpu.make_async_copy(y_ref.at[pl.ds(pl.multiple_of(row, ROW_ALIGN), SLOT_ROWS)], ybuf.at[s, pl.ds(e * SLOT_ROWS, SLOT_ROWS)],
                                     sem.at[s])

    def fetch(tile, c, s):
        for e in range(N_EXPERTS):
            @pl.when(tcnt_ref[tile * N_EXPERTS + e] > c * SLOT_ROWS)
            def _():
                slot_copy(s, e, tstart_ref[tile * N_EXPERTS + e] + c * SLOT_ROWS).start()

    def arrive(tile, c, s):
        for e in range(N_EXPERTS):
            @pl.when(tcnt_ref[tile * N_EXPERTS + e] > c * SLOT_ROWS)
            def _():
                slot_copy(s, e, 0).wait()

    @pl.when(j == 0)
    def _():
        ybuf[...] = jnp.zeros_like(ybuf)
        fetch(0, 0, 0)

    @pl.when(j + 1 < n_steps)
    def _():
        fetch(j + 1, 0, 1 - slot)

    def weights(c):
        cols = lax.broadcasted_iota(jnp.int32, (tm, N_SLOTS), 1)
        w = jnp.zeros((tm, N_SLOTS), F32)
        for k in range(TOP_K):
            w = jnp.where(cols == _slot_ids(idx_ref[:, k:k + 1], lrank_ref[:, k:k + 1], c), p_ref[:, k:k + 1], w)
        return w.astype(BF16)

    arrive(j, 0, slot)
    acc = _mm(weights(0), _unpack_bf16_pairs(ybuf[slot]))

    def extra_pass(c, acc):
        fetch(j, c, 2)
        arrive(j, c, 2)
        return acc + _mm(weights(c), _unpack_bf16_pairs(ybuf[2]))

    acc = lax.fori_loop(1, npass_ref[j], extra_pass, acc)
    m = mod_ref[pl.ds(b, 1), :]
    xo = x1_ref[0] + m[:, 5 * D_MODEL:6 * D_MODEL] * acc
    ms = jnp.mean(xo * xo, axis=-1, keepdims=True)
    o_ref[0] = xo * lax.rsqrt(ms + EPS) * g_ref[...]


def _combine(tile_start, tile_count, n_pass, idx_c, lrank_c, prob_c, y, x1, mod, g_fin):
    bsz, n, d = x1.shape
    tm = ROUTE_TILE
    nt = n // tm
    kern = functools.partial(_combine_kernel, tm=tm, nt=nt)
    tok = pl.BlockSpec((tm, TOP_K), lambda b, t, *_: (b * nt + t, 0))
    grid_spec = pltpu.PrefetchScalarGridSpec(
        num_scalar_prefetch=3,
        grid=(bsz, nt),
        in_specs=[tok, tok, tok,
                  pl.BlockSpec(memory_space=pl.ANY),
                  pl.BlockSpec((1, tm, d), lambda b, t, *_: (b, t, 0)),
                  pl.BlockSpec(mod.shape, lambda b, t, *_: (0, 0)),
                  pl.BlockSpec((1, d), lambda b, t, *_: (0, 0))],
        out_specs=pl.BlockSpec((1, tm, d), lambda b, t, *_: (b, t, 0)),
        scratch_shapes=[pltpu.VMEM((3, N_SLOTS, d // 2), jnp.uint32), pltpu.SemaphoreType.DMA((3,))],
    )
    return pl.pallas_call(
        kern,
        out_shape=jax.ShapeDtypeStruct((bsz, n, d), F32),
        grid_spec=grid_spec,
        compiler_params=_cparams(("arbitrary", "arbitrary"), VMEM_LIMIT),
        name="combine",
    )(tile_start, tile_count, n_pass, idx_c, lrank_c, prob_c, y, x1, mod, g_fin)


def _prep_w_in(w_in):
    q_a, kv, hq, ff, fb, hi, go, mm, mh = jnp.split(
        w_in, np.cumsum((MLA_Q_RANK, MLA_KV_RANK + MLA_ROPE) + (D_MODEL,) * 6).tolist(), axis=1)
    c_kv, kr = kv[:, :MLA_KV_RANK], kv[:, MLA_KV_RANK:]
    half = MLA_ROPE // 2
    kr_sw = jnp.concatenate([-kr[:, half:], kr[:, :half]], axis=1)
    w16 = jnp.concatenate([hq, hi, go, mm, mh, q_a, c_kv, kr, kr_sw], axis=1)
    w16 = jnp.pad(w16, ((0, 0), (0, N_PROJ16 - w16.shape[1])))
    return w16.astype(BF16), jnp.concatenate([ff, fb], axis=1).astype(BF16)


def _prep_w_q(w_q_b):
    w = w_q_b.reshape(MLA_Q_RANK, HEADS, MLA_QK)
    w = jnp.pad(w, ((0, 0), (0, 0), (0, MLA_QK_PAD - MLA_QK)))
    return w.reshape(MLA_Q_RANK, HEADS * MLA_QK_PAD).astype(BF16)


def _prep_w_kv(w_kv_b):
    w = w_kv_b.reshape(MLA_KV_RANK, HEADS, 2 * HEAD_DIM)
    k = w[:, :, :HEAD_DIM].reshape(MLA_KV_RANK, HEADS * HEAD_DIM)
    v = w[:, :, HEAD_DIM:].reshape(MLA_KV_RANK, HEADS * HEAD_DIM)
    return jnp.concatenate([k, v], axis=1).astype(BF16)


def _rope_tables(n_lat, n_ctx):
    f32 = np.float32
    rows = n_lat // GRID_W
    row = np.repeat(np.arange(rows), GRID_W).astype(f32)
    col = np.tile(np.arange(GRID_W), rows).astype(f32)
    n_freq = MLA_ROPE // 4
    inv = (f32(ROPE_BASE) ** (-np.arange(n_freq, dtype=f32) / f32(n_freq))).astype(f32)
    ang = np.concatenate([row[:, None] * inv, col[:, None] * inv], axis=-1).astype(f32)
    cos, sin = np.cos(ang).astype(f32), np.sin(ang).astype(f32)
    z64 = np.zeros((n_lat, 64), f32)
    cos_q = np.concatenate([cos, cos, z64], axis=1)
    sin_q = np.concatenate([-sin, sin, z64], axis=1)
    cos_k = np.concatenate([cos, cos, z64], axis=1)
    sin_k = np.concatenate([sin, sin, z64], axis=1)
    ctx_cos = np.concatenate([np.ones((n_ctx, 64), f32), np.zeros((n_ctx, 64), f32)], axis=1)
    cos_k = np.concatenate([cos_k, ctx_cos], axis=0)
    sin_k = np.concatenate([sin_k, np.zeros((n_ctx, 128), f32)], axis=0)
    return jnp.asarray(cos_q), jnp.asarray(sin_q), jnp.asarray(cos_k), jnp.asarray(sin_k)


def kernel(x, c, ctx, c_ctx, w_mod, b_mod, norm_mix_g, w_in, mla_q_norm_g, w_q_b, mla_kv_norm_g, w_kv_b,
           hg_lb_logits, hg_norm_g, w_out, norm_ffn_g, w_router, b_router, w_gate_up, b_gate_up, w_down,
           b_down, final_norm_g):
    bsz, n_lat, d = x.shape
    n_ctx = ctx.shape[1]
    assert d == D_MODEL and w_mod.shape[0] == 1
    assert n_lat % HG_CHUNK == 0 and n_ctx % HG_CHUNK == 0 and n_lat % GRID_W == 0
    t_tok = bsz * n_lat

    mod_rows = -(-(bsz + 1) // 8) * 8
    cc = jnp.concatenate([c, c_ctx[None, :], jnp.zeros((mod_rows - bsz - 1, d), F32)], axis=0)
    mod = _modulation(cc, w_mod[0], b_mod[0][None, :])

    xa = jnp.concatenate([x, ctx], axis=1)
    w16, w32 = _prep_w_in(w_in[0])
    p, pf = _inproj(xa, mod, norm_mix_g[0][None, :], w16, w32, n_lat)

    cos_q, sin_q, cos_k, sin_k = _rope_tables(n_lat, n_ctx)
    q = _mla_q(p, mla_q_norm_g[0][None, :], _prep_w_q(w_q_b[0]), cos_q, sin_q, n_lat)
    k, v = _mla_kv(p, mla_kv_norm_g[0][None, :], _prep_w_kv(w_kv_b[0]), cos_k, sin_k)
    y_mla = _attention(q, k, v)

    lb = jax.nn.softmax(hg_lb_logits.astype(F32), axis=1)[:, 0, :]
    o_f, o_b = _hgrn_scans(p, pf, lb.reshape(2, HEADS, 1, HEAD_DIM), n_lat)

    x1, h2, logits_t = _merge(y_mla, o_f, o_b, p, x, mod, hg_norm_g[0][None, :], w_out[0].astype(BF16),
                               norm_ffn_g[0][None, :], w_router[0].T, b_router[0][:, None])

    assert n_lat % ROUTE_TILE == 0
    idx_t, prob_t, lrank_t, base, cnt = _router(logits_t, ROUTE_TILE)

    n_tiles = t_tok // ROUTE_TILE
    counts = cnt[:, 0].astype(jnp.int32)
    base = base[:, :, 0].astype(jnp.int32)
    tile_count = jnp.concatenate([base[1:] - base[:-1], (counts - base[-1])[None, :]], axis=0)
    seg = (tile_count + ROW_ALIGN - 1) // ROW_ALIGN * ROW_ALIGN
    seg_end = jnp.cumsum(seg, axis=0)
    rows_e = seg_end[-1]
    padded = (rows_e + SLOT_ROWS + FFN_ROWS - 1) // FFN_ROWS * FFN_ROWS
    pad_end = jnp.cumsum(padded)
    pad_start = pad_end - padded
    max_rows = t_tok * TOP_K + N_EXPERTS * (n_tiles * (ROW_ALIGN - 1) + SLOT_ROWS + FFN_ROWS - 1)
    n_rows = -(-max_rows // FFN_ROWS) * FFN_ROWS
    nblk = n_rows // FFN_ROWS
    block_start = jnp.arange(nblk, dtype=jnp.int32) * FFN_ROWS
    block_e = jnp.minimum(jnp.sum(pad_end[None, :] <= block_start[:, None], axis=1), N_EXPERTS - 1).astype(jnp.int32)
    n_used = (pad_end[-1:] // FFN_ROWS).astype(jnp.int32)
    tile_start = (pad_start[None, :] + seg_end - seg).reshape(-1).astype(jnp.int32)
    n_pass = jnp.maximum(-(-jnp.max(tile_count, axis=1) // SLOT_ROWS), 1).astype(jnp.int32)
    tile_count = tile_count.reshape(-1)
    fill_start = jnp.minimum(pad_start + rows_e, n_rows - ZERO_ROWS).astype(jnp.int32)

    xs = _dispatch(tile_start, tile_count, n_pass, fill_start, n_used, idx_t, lrank_t, h2, n_rows)

    n_grp = 2 * D_EXPERT // GU_GROUP
    b_gu = b_gate_up[0].reshape(N_EXPERTS, n_grp, GU_GROUP // 2, 2).transpose(0, 1, 3, 2).reshape(N_EXPERTS, 1, 2 * D_EXPERT)
    y = _ffn(block_e, n_used, xs, w_gate_up[0], b_gu, w_down[0], b_down[0][:, None, :])

    return _combine(tile_start, tile_count, n_pass, idx_t.T, lrank_t.T, prob_t.T, y, x1, mod, final_norm_g[None, :])
```

```python
import functools

import jax
import jax.numpy as jnp
import numpy as np
from jax import lax
from jax.experimental import pallas as pl
from jax.experimental.pallas import tpu as pltpu

F32 = jnp.float32
BF16 = jnp.bfloat16

D_MODEL = 1024
EPS = 1e-6
LOG2_E = 1.4426950408889634
N_MOD = 6
GRID_W = 64
ROPE_BASE = 10000.0

HEADS = 8
HEAD_DIM = 128
MLA_ROPE = 64
MLA_QK = HEAD_DIM + MLA_ROPE
MLA_QK_PAD = 256
MLA_Q_RANK = 768
MLA_KV_RANK = 256

N_EXPERTS = 32
TOP_K = 4
D_EXPERT = 1024
SWIGLU_LIMIT = 7.0
SWIGLU_ALPHA = 1.702

HG_CHUNK = 128
FFN_ROWS = 512

COL_HQ, COL_I, COL_GO, COL_MM, COL_MH = (i * D_MODEL for i in range(5))
COL_QA = 5 * D_MODEL
COL_CKV = COL_QA + MLA_Q_RANK
COL_KR = COL_CKV + MLA_KV_RANK
PROJ16_TN = 1280
N_PROJ16 = -(-(COL_KR + 2 * MLA_ROPE) // PROJ16_TN) * PROJ16_TN
COL_FF, COL_FB = 0, D_MODEL
N_PROJ32 = 2 * D_MODEL
PROJ32_TN = 1024

VMEM_LIMIT = 56 * 1024 * 1024


def _cparams(sem, vmem=None):
    return pltpu.CompilerParams(dimension_semantics=sem, vmem_limit_bytes=vmem)


def _tile(n, pref, mult=8):
    best = None
    for t in range(mult, min(n, pref) + 1, mult):
        if n % t == 0:
            best = t
    assert best is not None, (n, pref, mult)
    return best


def _nt(a, b):
    return lax.dot_general(a, b, (((1,), (1,)), ((), ())), preferred_element_type=F32)


def _mm(a, b):
    return jnp.dot(a, b, preferred_element_type=F32)


def _split2(a):
    hi = a.astype(BF16)
    lo = (a - hi.astype(F32)).astype(BF16)
    return hi, lo


def _sigmoid(x):
    return 1.0 / (1.0 + jnp.exp(-x))


def _mod_kernel(c_ref, w_ref, b_ref, o_ref):
    c = c_ref[...]
    s = c * _sigmoid(c)
    s_hi, s_lo = _split2(s)
    w_hi, w_lo = _split2(w_ref[...])
    o_ref[...] = _mm(s_hi, w_hi) + _mm(s_lo, w_hi) + _mm(s_hi, w_lo) + b_ref[...]


def _modulation(cc, w_mod, b_mod):
    r, d = cc.shape
    n = w_mod.shape[1]
    tn = _tile(n, 1536, 128)
    return pl.pallas_call(
        _mod_kernel,
        out_shape=jax.ShapeDtypeStruct((r, n), F32),
        grid=(n // tn,),
        in_specs=[pl.BlockSpec((r, d), lambda j: (0, 0)),
                  pl.BlockSpec((d, tn), lambda j: (0, j)),
                  pl.BlockSpec((1, tn), lambda j: (0, j))],
        out_specs=pl.BlockSpec((r, tn), lambda j: (0, j)),
        compiler_params=_cparams(("parallel",), VMEM_LIMIT),
        name="mod",
    )(cc, w_mod, b_mod)


def _inproj_kernel(x_ref, mod_ref, g_ref, w16_ref, w32_ref, o16_ref, o32_ref, hn_ref, *, n_lat, tm, ctx_row, n16):
    b = pl.program_id(0)
    t = pl.program_id(1)
    j = pl.program_id(2)

    @pl.when(j == 0)
    def _():
        x = x_ref[0]
        ms = jnp.mean(x * x, axis=-1, keepdims=True)
        y = x * lax.rsqrt(ms + EPS) * g_ref[...]
        row = t * tm + lax.broadcasted_iota(jnp.int32, (tm, 1), 0)
        is_ctx = row >= n_lat
        m_lat = mod_ref[pl.ds(b, 1), :]
        m_ctx = mod_ref[pl.ds(ctx_row, 1), :]
        shift = jnp.where(is_ctx, m_ctx[:, 0:D_MODEL], m_lat[:, 0:D_MODEL])
        scale = jnp.where(is_ctx, m_ctx[:, D_MODEL:2 * D_MODEL], m_lat[:, D_MODEL:2 * D_MODEL])
        hn_ref[...] = (y * (1.0 + scale) + shift).astype(BF16)

    @pl.when(j < n16)
    def _():
        o16_ref[0] = _mm(hn_ref[...], w16_ref[...]).astype(BF16)

    @pl.when(j >= n16)
    def _():
        o32_ref[0] = _mm(hn_ref[...], w32_ref[...])


def _inproj(xa, mod, g, w16, w32, n_lat):
    bsz, rows, d = xa.shape
    tm = _tile(rows, 1152)
    n16 = N_PROJ16 // PROJ16_TN
    n32 = N_PROJ32 // PROJ32_TN

    def j16(j):
        return jnp.minimum(j, n16 - 1)

    def j32(j):
        return jnp.maximum(j - n16, 0)

    kern = functools.partial(_inproj_kernel, n_lat=n_lat, tm=tm, ctx_row=bsz, n16=n16)
    return pl.pallas_call(
        kern,
        out_shape=(jax.ShapeDtypeStruct((bsz, rows, N_PROJ16), BF16),
                   jax.ShapeDtypeStruct((bsz, rows, N_PROJ32), F32)),
        grid=(bsz, rows // tm, n16 + n32),
        in_specs=[pl.BlockSpec((1, tm, d), lambda b, t, j: (b, t, 0)),
                  pl.BlockSpec(mod.shape, lambda b, t, j: (0, 0)),
                  pl.BlockSpec((1, d), lambda b, t, j: (0, 0)),
                  pl.BlockSpec((d, PROJ16_TN), lambda b, t, j: (0, j16(j))),
                  pl.BlockSpec((d, PROJ32_TN), lambda b, t, j: (0, j32(j)))],
        out_specs=(pl.BlockSpec((1, tm, PROJ16_TN), lambda b, t, j: (b, t, j16(j))),
                   pl.BlockSpec((1, tm, PROJ32_TN), lambda b, t, j: (b, t, j32(j)))),
        scratch_shapes=[pltpu.VMEM((tm, d), BF16)],
        compiler_params=_cparams(("parallel", "parallel", "arbitrary"), VMEM_LIMIT),
        name="inproj",
    )(xa, mod, g, w16, w32)


def _mlaq_kernel(a0_ref, a1_ref, a2_ref, g_ref, w_ref, cos_ref, sin_ref, o_ref, *, tm):
    parts = [a0_ref[0].astype(F32), a1_ref[0].astype(F32), a2_ref[0].astype(F32)]
    ss = sum(jnp.sum(p * p, axis=-1, keepdims=True) for p in parts)
    r = lax.rsqrt(ss * (1.0 / MLA_Q_RANK) + EPS)
    acc = None
    for j, p in enumerate(parts):
        hj = (p * r * g_ref[:, j * 256:(j + 1) * 256]).astype(BF16)
        d = _mm(hj, w_ref[j * 256:(j + 1) * 256, :])
        acc = d if acc is None else acc + d
    lane = lax.broadcasted_iota(jnp.int32, (tm, HEAD_DIM), 1)
    scale = MLA_QK ** -0.5 * LOG2_E
    cos = cos_ref[...]
    sin = sin_ref[...]
    for h in range(HEADS):
        nope = acc[:, h * MLA_QK_PAD:h * MLA_QK_PAD + HEAD_DIM]
        rp = acc[:, h * MLA_QK_PAD + HEAD_DIM:(h + 1) * MLA_QK_PAD]
        swapped = jnp.where(lane < 32, pltpu.roll(rp, 96, 1), pltpu.roll(rp, 32, 1))
        rot = rp * cos + swapped * sin
        o_ref[0, h, :, 0:HEAD_DIM] = (nope * scale).astype(BF16)
        o_ref[0, h, :, HEAD_DIM:MLA_QK_PAD] = (rot * scale).astype(BF16)


def _mla_q(p, g, w, cos_q, sin_q, n_lat):
    bsz = p.shape[0]
    tm = _tile(n_lat, 512)
    cb = COL_QA // 256
    kern = functools.partial(_mlaq_kernel, tm=tm)
    return pl.pallas_call(
        kern,
        out_shape=jax.ShapeDtypeStruct((bsz, HEADS, n_lat, MLA_QK_PAD), BF16),
        grid=(bsz, n_lat // tm),
        in_specs=[pl.BlockSpec((1, tm, 256), lambda b, t: (b, t, cb)),
                  pl.BlockSpec((1, tm, 256), lambda b, t: (b, t, cb + 1)),
                  pl.BlockSpec((1, tm, 256), lambda b, t: (b, t, cb + 2)),
                  pl.BlockSpec((1, MLA_Q_RANK), lambda b, t: (0, 0)),
                  pl.BlockSpec(w.shape, lambda b, t: (0, 0)),
                  pl.BlockSpec((tm, HEAD_DIM), lambda b, t: (t, 0)),
                  pl.BlockSpec((tm, HEAD_DIM), lambda b, t: (t, 0))],
        out_specs=pl.BlockSpec((1, HEADS, tm, MLA_QK_PAD), lambda b, t: (b, 0, t, 0)),
        compiler_params=_cparams(("parallel", "parallel"), VMEM_LIMIT),
        name="mla_q",
    )(p, p, p, g, w, cos_q, sin_q)


def _mlakv_kernel(c_ref, kr_ref, g_ref, w_ref, cos_ref, sin_ref, k_ref, v_ref):
    c = c_ref[0].astype(F32)
    ms = jnp.mean(c * c, axis=-1, keepdims=True)
    hn = (c * lax.rsqrt(ms + EPS) * g_ref[...]).astype(BF16)
    kv = _mm(hn, w_ref[...])
    grp = kr_ref[0].astype(F32)
    rot = (grp * cos_ref[...] + pltpu.roll(grp, 64, 1) * sin_ref[...]).astype(BF16)
    for h in range(HEADS):
        k_ref[0, h, :, 0:HEAD_DIM] = kv[:, h * HEAD_DIM:(h + 1) * HEAD_DIM].astype(BF16)
        k_ref[0, h, :, HEAD_DIM:MLA_QK_PAD] = rot
        v_ref[0, h] = kv[:, D_MODEL + h * HEAD_DIM:D_MODEL + (h + 1) * HEAD_DIM].astype(BF16)


def _mla_kv(p, g, w, cos_k, sin_k):
    bsz, rows, _ = p.shape
    tm = _tile(rows, 768)
    return pl.pallas_call(
        _mlakv_kernel,
        out_shape=(jax.ShapeDtypeStruct((bsz, HEADS, rows, MLA_QK_PAD), BF16),
                   jax.ShapeDtypeStruct((bsz, HEADS, rows, HEAD_DIM), BF16)),
        grid=(bsz, rows // tm),
        in_specs=[pl.BlockSpec((1, tm, MLA_KV_RANK), lambda b, t: (b, t, COL_CKV // MLA_KV_RANK)),
                  pl.BlockSpec((1, tm, 128), lambda b, t: (b, t, COL_KR // 128)),
                  pl.BlockSpec((1, MLA_KV_RANK), lambda b, t: (0, 0)),
                  pl.BlockSpec(w.shape, lambda b, t: (0, 0)),
                  pl.BlockSpec((tm, 128), lambda b, t: (t, 0)),
                  pl.BlockSpec((tm, 128), lambda b, t: (t, 0))],
        out_specs=(pl.BlockSpec((1, HEADS, tm, MLA_QK_PAD), lambda b, t: (b, 0, t, 0)),
                   pl.BlockSpec((1, HEADS, tm, HEAD_DIM), lambda b, t: (b, 0, t, 0))),
        compiler_params=_cparams(("parallel", "parallel"), VMEM_LIMIT),
        name="mla_kv",
    )(p, p, g, w, cos_k, sin_k)


def _attn_kernel(q_ref, k_ref, v_ref, o_ref, *, tq, sub):
    k = k_ref[0, 0]
    v = v_ref[0, 0]
    v1 = jnp.concatenate([v, jnp.ones_like(v)], axis=1)

    def sub_tile(r):
        rows = slice(r * sub, (r + 1) * sub)
        s = _nt(q_ref[0, 0, rows, :], k)
        yield
        m = jnp.max(s, axis=-1, keepdims=True)
        p = jnp.exp2(s - m).astype(BF16)
        yield
        ol = _mm(p, v1)
        o_ref[0, rows, :] = (ol[:, 0:HEAD_DIM] * (1.0 / ol[:, HEAD_DIM:HEAD_DIM + 1])).astype(BF16)

    _round_robin([sub_tile(r) for r in range(tq // sub)])


def _attention(q, k, v):
    bsz, _, n, _ = q.shape
    m = k.shape[2]
    tq = _tile(n, 2048)
    sub = _tile(tq, 512)
    kern = functools.partial(_attn_kernel, tq=tq, sub=sub)
    return pl.pallas_call(
        kern,
        out_shape=jax.ShapeDtypeStruct((bsz, n, HEADS * HEAD_DIM), BF16),
        grid=(bsz, HEADS, n // tq),
        in_specs=[pl.BlockSpec((1, 1, tq, MLA_QK_PAD), lambda b, h, t: (b, h, t, 0)),
                  pl.BlockSpec((1, 1, m, MLA_QK_PAD), lambda b, h, t: (b, h, 0, 0)),
                  pl.BlockSpec((1, 1, m, HEAD_DIM), lambda b, h, t: (b, h, 0, 0))],
        out_specs=pl.BlockSpec((1, tq, HEAD_DIM), lambda b, h, t: (b, t, h)),
        compiler_params=_cparams(("parallel", "parallel", "arbitrary"), VMEM_LIMIT),
        name="attention",
    )(q, k, v)


def _level_ref(cum, blk, reverse):
    c = cum.shape[0]
    half = blk // 2
    r = half if reverse else half - 1
    if blk >= 8:
        x = cum.reshape(c // blk, blk, HEAD_DIM)
        e = jnp.broadcast_to(x[:, r:r + 1, :], x.shape)
        return e.reshape(c, HEAD_DIM)
    x = cum.reshape(c // 8, 8, HEAD_DIM)
    sub = lax.broadcasted_iota(jnp.int32, x.shape, 1)
    e = None
    for jb in range(8 // blk):
        cand = jnp.broadcast_to(x[:, jb * blk + r:jb * blk + r + 1, :], x.shape)
        e = cand if e is None else jnp.where(sub >= jb * blk, cand, e)
    return e.reshape(c, HEAD_DIM)


HG_SUB = 32
HG_SUB_LEVEL = HG_SUB.bit_length() - 1
HG_MAX_EXP2 = 100.0


def _hgrn_state(z, v, lb, tri, st, *, chunk, reverse):
    f = lb + (1.0 - lb) * _sigmoid(z)
    kb = (1.0 - f).astype(BF16)
    g = jnp.log(f) * LOG2_E

    g_hi, g_lo = _split2(g)
    yield
    cum = _mm(tri, jnp.concatenate([g_hi, g_lo], axis=0))
    yield
    last = 0 if reverse else chunk - 1
    tot = cum[last:last + 1, :]
    kt = kb * jnp.exp2(tot - cum).astype(BF16)
    vt = v.astype(F32).T.astype(BF16)
    yield
    st_new = st * jnp.exp2(tot) + _mm(vt, kt)
    return kb, cum, st_new


def _sub_block_decay(cum, reverse):
    c = cum.shape[0]
    x = cum.reshape(c // HG_SUB, HG_SUB, HEAD_DIM)
    zero = jnp.zeros((1, 1, HEAD_DIM), F32)
    if reverse:
        edge = jnp.concatenate([x[1:, 0:1, :], zero], axis=0)
    else:
        edge = jnp.concatenate([zero, x[:-1, HG_SUB - 1:HG_SUB, :]], axis=0)
    return (x - edge).reshape(c, HEAD_DIM)


def _hgrn_readout(q, v, kb, cum, sub, lvl, st, *, chunk, reverse, shared):
    q = q.astype(F32)
    qb = (q * _sigmoid(q) * (HEAD_DIM ** -0.5)).astype(BF16)
    yield
    if shared:
        att = jnp.where(lvl == 0, _nt(qb * jnp.exp2(sub).astype(BF16), kb * jnp.exp2(-sub).astype(BF16)).astype(BF16),
                        jnp.zeros((), BF16))
        first = HG_SUB_LEVEL + 1
    else:
        att = jnp.where(lvl == 0, _nt(qb, kb).astype(BF16), jnp.zeros((), BF16))
        first = 1
    for lv in range(first, chunk.bit_length()):
        zrel = cum - _level_ref(cum, 1 << lv, reverse)
        neg_abs = pltpu.bitcast(pltpu.bitcast(zrel, jnp.uint32) | jnp.uint32(0x80000000), F32)
        e = jnp.exp2(neg_abs).astype(BF16)
        yield
        att = jnp.where(lvl == lv, _nt(qb * e, kb * e).astype(BF16), att)
        yield
    return _nt(qb * jnp.exp2(cum).astype(BF16), st.astype(BF16)) + _mm(att, v.astype(BF16))


def _round_robin(gens):
    out = [None] * len(gens)
    active = list(range(len(gens)))
    while active:
        for i in list(active):
            try:
                next(gens[i])
            except StopIteration as e:
                out[i] = e.value
                active.remove(i)
    return out


def _hgrn_kernel(qf_ref, zf_ref, if_ref, qb_ref, zb_ref, ib_ref, lb_ref, tri_ref, lvl_ref, of_ref, ob_ref, st_ref,
                 *, chunk, group, n_ctx_chunks):
    step = pl.program_id(2)

    @pl.when(step == 0)
    def _():
        st_ref[...] = jnp.zeros_like(st_ref)

    ins = ((qf_ref, zf_ref, if_ref), (qb_ref, zb_ref, ib_ref))
    outs = (of_ref, ob_ref)
    chains = [(d, j) for j in range(group) for d in range(2)]

    def cols(j):
        return slice(j * HEAD_DIM, (j + 1) * HEAD_DIM)

    def advance(want_out):
        old = [st_ref[d, j] for d, j in chains]
        parts = _round_robin([_hgrn_state(ins[d][1][0, :, cols(j)], ins[d][2][0, :, cols(j)], lb_ref[d, j],
                                          tri_ref[d], st, chunk=chunk, reverse=bool(d))
                              for (d, j), st in zip(chains, old)])
        for (d, j), (_, _, st_new) in zip(chains, parts):
            st_ref[d, j] = st_new
        if not want_out:
            return
        subs = [_sub_block_decay(cum, bool(d)) for (d, j), (_, cum, _) in zip(chains, parts)]
        low = subs[0]
        for s in subs[1:]:
            low = jnp.minimum(low, s)
        low = jnp.min(jnp.min(low, axis=0, keepdims=True), axis=1, keepdims=True)
        in_range = low[0, 0] >= -HG_MAX_EXP2

        def readouts(shared):
            os_ = _round_robin([_hgrn_readout(ins[d][0][0, :, cols(j)], ins[d][2][0, :, cols(j)], kb, cum, sub,
                                              lvl_ref[d, int(shared)], st, chunk=chunk, reverse=bool(d), shared=shared)
                                for (d, j), (kb, cum, _), sub, st in zip(chains, parts, subs, old)])
            for (d, j), o in zip(chains, os_):
                outs[d][0, j] = o.astype(BF16)

        @pl.when(in_range)
        def _():
            readouts(True)

        @pl.when(jnp.logical_not(in_range))
        def _():
            readouts(False)

    @pl.when(step < n_ctx_chunks)
    def _():
        advance(False)

    @pl.when(step >= n_ctx_chunks)
    def _():
        advance(True)


def _hgrn_consts(chunk, reverse):
    t = np.arange(chunk)[:, None]
    s = np.arange(chunk)[None, :]
    x = t ^ s
    bitlen = np.zeros_like(x)
    for b in range(chunk.bit_length()):
        bitlen = np.where(x >> b > 0, b + 1, bitlen)
    valid = (s > t) if reverse else (t > s)
    lvl = np.where(t == s, 0, np.where(valid, bitlen, -1)).astype(np.float32)
    lvl_shared = np.where((lvl >= 0) & (lvl <= HG_SUB_LEVEL), 0, lvl)
    tri = ((s >= t) if reverse else (t >= s)).astype(np.float32)
    return np.concatenate([tri, tri], axis=1), np.stack([lvl, lvl_shared])


HG_GROUP = 8


def _hgrn_scans(p, pf, lb, n_lat):
    bsz, rows, _ = p.shape
    c = HG_CHUNK
    grp = HG_GROUP
    n_chunks = rows // c
    n_lat_c = n_lat // c
    n_ctx_c = n_chunks - n_lat_c
    consts = [_hgrn_consts(c, False), _hgrn_consts(c, True)]
    tri = jnp.asarray(np.stack([consts[0][0], consts[1][0]]), dtype=BF16)
    lvl = jnp.asarray(np.stack([consts[0][1], consts[1][1]]), dtype=BF16)

    def cidx_f(i):
        return jnp.where(i < n_ctx_c, n_lat_c + i, i - n_ctx_c)

    def cidx_b(i):
        return n_chunks - 1 - i

    def col_spec(col, cidx):
        return pl.BlockSpec((1, c, grp * HEAD_DIM), lambda b, h, i: (b, cidx(i), col // (grp * HEAD_DIM) + h))

    def out_spec(cidx):
        return pl.BlockSpec((1, grp, c, HEAD_DIM), lambda b, h, i: (b, h, cidx(jnp.maximum(i, n_ctx_c)), 0))

    o_shape = jax.ShapeDtypeStruct((bsz, HEADS, n_lat, HEAD_DIM), BF16)
    kern = functools.partial(_hgrn_kernel, chunk=c, group=grp, n_ctx_chunks=n_ctx_c)
    return pl.pallas_call(
        kern,
        out_shape=(o_shape, o_shape),
        grid=(bsz, HEADS // grp, n_chunks),
        in_specs=[col_spec(COL_HQ, cidx_f), col_spec(COL_FF, cidx_f), col_spec(COL_I, cidx_f),
                  col_spec(COL_HQ, cidx_b), col_spec(COL_FB, cidx_b), col_spec(COL_I, cidx_b),
                  pl.BlockSpec((2, grp, 1, HEAD_DIM), lambda b, h, i: (0, h, 0, 0)),
                  pl.BlockSpec((2, c, 2 * c), lambda b, h, i: (0, 0, 0)),
                  pl.BlockSpec((2, 2, c, c), lambda b, h, i: (0, 0, 0, 0))],
        out_specs=(out_spec(cidx_f), out_spec(cidx_b)),
        scratch_shapes=[pltpu.VMEM((2, grp, HEAD_DIM, HEAD_DIM), F32)],
        compiler_params=_cparams(("parallel", "parallel", "arbitrary"), VMEM_LIMIT),
        name="hgrn",
    )(p, pf, p, p, pf, p, lb, tri, lvl)


def _merge_kernel(ym_ref, of_ref, ob_ref, go_ref, gm_ref, gh_ref, x_ref, mod_ref, hgg_ref, wout_ref,
                  gffn_ref, wr_ref, br_ref, x1_ref, h2_ref, lg_ref, y_scr):
    b = pl.program_id(0)
    for h in range(HEADS):
        sl = slice(h * HEAD_DIM, (h + 1) * HEAD_DIM)
        o = of_ref[0, h].astype(F32) + ob_ref[0, h].astype(F32)
        ms = jnp.mean(o * o, axis=-1, keepdims=True)
        g = go_ref[0, :, sl].astype(F32)
        yh = o * lax.rsqrt(ms + EPS) * hgg_ref[...] * (g * _sigmoid(g))
        y = (_sigmoid(gm_ref[0, :, sl].astype(F32)) * ym_ref[0, :, sl].astype(F32)
             + _sigmoid(gh_ref[0, :, sl].astype(F32)) * yh)
        y_scr[:, sl] = y.astype(BF16)
    mix = _mm(y_scr[...], wout_ref[...])
    m = mod_ref[pl.ds(b, 1), :]
    x1 = x_ref[0] + m[:, 2 * D_MODEL:3 * D_MODEL] * mix
    x1_ref[0] = x1
    ms = jnp.mean(x1 * x1, axis=-1, keepdims=True)
    h2 = (x1 * lax.rsqrt(ms + EPS) * gffn_ref[...]) * (1.0 + m[:, 4 * D_MODEL:5 * D_MODEL]) + m[:, 3 * D_MODEL:4 * D_MODEL]
    h_hi = h2.astype(BF16)
    h2_ref[...] = h_hi
    h_lo = (h2 - h_hi.astype(F32)).astype(BF16)
    w_hi, w_lo = _split2(wr_ref[...])
    lg_ref[...] = _nt(w_hi, h_hi) + _nt(w_lo, h_hi) + _nt(w_hi, h_lo) + br_ref[...]


def _merge(y_mla, o_f, o_b, p, x, mod, hg_g, w_out, g_ffn, w_r_t, b_r):
    bsz, n, d = x.shape
    tm = _tile(n, 256, 128)
    nt = n // tm

    def pcol(col):
        return pl.BlockSpec((1, tm, d), lambda b, t: (b, t, col // d))

    tok = lambda b, t: (b, t, 0)
    const2 = lambda b, t: (0, 0)
    return pl.pallas_call(
        _merge_kernel,
        out_shape=(jax.ShapeDtypeStruct((bsz, n, d), F32),
                   jax.ShapeDtypeStruct((bsz * n, d), BF16),
                   jax.ShapeDtypeStruct((N_EXPERTS, bsz * n), F32)),
        grid=(bsz, nt),
        in_specs=[pl.BlockSpec((1, tm, d), tok),
                  pl.BlockSpec((1, HEADS, tm, HEAD_DIM), lambda b, t: (b, 0, t, 0)),
                  pl.BlockSpec((1, HEADS, tm, HEAD_DIM), lambda b, t: (b, 0, t, 0)),
                  pcol(COL_GO), pcol(COL_MM), pcol(COL_MH),
                  pl.BlockSpec((1, tm, d), tok),
                  pl.BlockSpec(mod.shape, const2),
                  pl.BlockSpec((1, HEAD_DIM), const2),
                  pl.BlockSpec((d, d), const2),
                  pl.BlockSpec((1, d), const2),
                  pl.BlockSpec((N_EXPERTS, d), const2),
                  pl.BlockSpec((N_EXPERTS, 1), const2)],
        out_specs=(pl.BlockSpec((1, tm, d), tok),
                   pl.BlockSpec((tm, d), lambda b, t: (b * nt + t, 0)),
                   pl.BlockSpec((N_EXPERTS, tm), lambda b, t: (0, b * nt + t))),
        scratch_shapes=[pltpu.VMEM((tm, d), BF16)],
        compiler_params=_cparams(("parallel", "parallel"), VMEM_LIMIT),
        name="merge",
    )(y_mla, o_f, o_b, p, p, p, x, mod, hg_g, w_out, g_ffn, w_r_t, b_r)


def _router_kernel(lg_ref, upper_ref, idx_ref, prob_ref, rank_ref, base_ref, cnt_ref, run_ref, *, tm):
    @pl.when(pl.program_id(0) == 0)
    def _():
        run_ref[...] = jnp.zeros_like(run_ref)

    l = lg_ref[...]
    eidx = lax.broadcasted_iota(jnp.int32, (N_EXPERTS, tm), 0)
    vals, sel = [], []
    for _ in range(TOP_K):
        m = jnp.max(l, axis=0, keepdims=True)
        first = jnp.min(jnp.where(l == m, eidx, N_EXPERTS), axis=0, keepdims=True)
        vals.append(m)
        sel.append(first)
        l = jnp.where(eidx == first, -jnp.inf, l)
    ex = [jnp.exp(v - vals[0]) for v in vals]
    inv = 1.0 / (ex[0] + ex[1] + ex[2] + ex[3])
    onehot = jnp.zeros((N_EXPERTS, tm), F32)
    for k in range(TOP_K):
        onehot = onehot + jnp.where(eidx == sel[k], 1.0, 0.0)
    before = _mm(onehot.astype(BF16), upper_ref[...])
    base_ref[0] = run_ref[...]
    for k in range(TOP_K):
        idx_ref[k:k + 1, :] = sel[k]
        prob_ref[k:k + 1, :] = ex[k] * inv
        rank_ref[k:k + 1, :] = jnp.sum(jnp.where(eidx == sel[k], before, 0.0), axis=0, keepdims=True).astype(jnp.int32)
    run_ref[...] = run_ref[...] + jnp.sum(onehot, axis=1, keepdims=True)
    cnt_ref[...] = run_ref[...]


def _router(logits_t, tm):
    _, t = logits_t.shape
    upper = jnp.asarray(np.triu(np.ones((tm, tm), np.float32), 1), dtype=BF16)
    kern = functools.partial(_router_kernel, tm=tm)
    tok = pl.BlockSpec((TOP_K, tm), lambda i: (0, i))
    return pl.pallas_call(
        kern,
        out_shape=(jax.ShapeDtypeStruct((TOP_K, t), jnp.int32),
                   jax.ShapeDtypeStruct((TOP_K, t), F32),
                   jax.ShapeDtypeStruct((TOP_K, t), jnp.int32),
                   jax.ShapeDtypeStruct((t // tm, N_EXPERTS, 128), F32),
                   jax.ShapeDtypeStruct((N_EXPERTS, 128), F32)),
        grid=(t // tm,),
        in_specs=[pl.BlockSpec((N_EXPERTS, tm), lambda i: (0, i)),
                  pl.BlockSpec((tm, tm), lambda i: (0, 0))],
        out_specs=(tok, tok, tok, pl.BlockSpec((1, N_EXPERTS, 128), lambda i: (i, 0, 0)),
                   pl.BlockSpec((N_EXPERTS, 128), lambda i: (0, 0))),
        scratch_shapes=[pltpu.VMEM((N_EXPERTS, 128), F32)],
        compiler_params=_cparams(("arbitrary",), VMEM_LIMIT),
        name="router",
    )(logits_t, upper)


ROUTE_TILE = 256
ROW_ALIGN = 8
SLOT_ROWS = 48
N_SLOTS = N_EXPERTS * SLOT_ROWS
ZERO_ROWS = FFN_ROWS + SLOT_ROWS


def _slot_ids(idx, lrank, c):
    r = lrank - c * SLOT_ROWS
    return jnp.where(jnp.logical_and(r >= 0, r < SLOT_ROWS), idx * SLOT_ROWS + r, -1)


def _pack_bf16_pairs(x):
    u = pltpu.bitcast(x, jnp.uint32)
    w = x.shape[1] // 2
    return (u[:, 0:w] >> 16) | u[:, w:2 * w]


def _unpack_bf16_pairs(u):
    lo = pltpu.bitcast(u << 16, F32).astype(BF16)
    hi = pltpu.bitcast(u & jnp.uint32(0xFFFF0000), F32).astype(BF16)
    return jnp.concatenate([lo, hi], axis=1)


def _dispatch_kernel(tstart_ref, tcnt_ref, npass_ref, fill_ref, nused_ref, idx_ref, lrank_ref, h2_ref, xs_ref,
                     xbuf, zbuf, sem, zsem, state, *, tm, n_blocks):
    j = pl.program_id(0)

    def slot_copy(slot, e, row):
        return pltpu.make_async_copy(xbuf.at[slot, pl.ds(e * SLOT_ROWS, SLOT_ROWS)],
                                     xs_ref.at[pl.ds(pl.multiple_of(row, ROW_ALIGN), SLOT_ROWS)], sem.at[slot])

    def wait_pass(slot, n):
        @pl.when(n == N_EXPERTS)
        def _():
            pltpu.make_async_copy(xbuf.at[slot], xs_ref.at[pl.ds(0, N_SLOTS)], sem.at[slot]).wait()

        @pl.when(n != N_EXPERTS)
        def _():
            def body(i, carry):
                slot_copy(slot, 0, 0).wait()
                return carry
            lax.fori_loop(0, n, body, 0)

    @pl.when(j == 0)
    def _():
        zbuf[...] = jnp.zeros_like(zbuf)
        state[0] = 0
        state[1] = 0
        for parity in range(2):
            for e in range(parity, N_EXPERTS, 2):
                pltpu.make_async_copy(zbuf, xs_ref.at[pl.ds(pl.multiple_of(fill_ref[e], ROW_ALIGN), ZERO_ROWS)],
                                      zsem).start()
            for e in range(parity, N_EXPERTS, 2):
                pltpu.make_async_copy(zbuf, xs_ref.at[pl.ds(0, ZERO_ROWS)], zsem).wait()

        def tail(i, carry):
            pltpu.make_async_copy(zbuf.at[pl.ds(0, FFN_ROWS)],
                                  xs_ref.at[pl.ds(pl.multiple_of(i * FFN_ROWS, ROW_ALIGN), FFN_ROWS)], zsem).start()
            return carry
        lax.fori_loop(nused_ref[0], n_blocks, tail, 0)

        def tail_wait(i, carry):
            pltpu.make_async_copy(zbuf.at[pl.ds(0, FFN_ROWS)], xs_ref.at[pl.ds(0, FFN_ROWS)], zsem).wait()
            return carry
        lax.fori_loop(nused_ref[0], n_blocks, tail_wait, 0)

    def one_pass(c, carry):
        rows = lax.broadcasted_iota(jnp.int32, (N_SLOTS, tm), 0)
        sel = jnp.zeros((N_SLOTS, tm), F32)
        for k in range(TOP_K):
            sel = jnp.where(rows == _slot_ids(idx_ref[k:k + 1, :], lrank_ref[k:k + 1, :], c), 1.0, sel)
        packed = _pack_bf16_pairs(_mm(sel.astype(BF16), h2_ref[...]))
        p = state[0]
        slot = p % 2
        wait_pass(1 - slot, state[1])
        xbuf[slot] = packed
        n = 0
        for e in range(N_EXPERTS):
            live = tcnt_ref[j * N_EXPERTS + e] > c * SLOT_ROWS

            @pl.when(live)
            def _():
                slot_copy(slot, e, tstart_ref[j * N_EXPERTS + e] + c * SLOT_ROWS).start()
            n = n + live.astype(jnp.int32)
        state[0] = p + 1
        state[1] = n
        return carry

    lax.fori_loop(0, npass_ref[j], one_pass, 0)

    @pl.when(j == pl.num_programs(0) - 1)
    def _():
        wait_pass((state[0] - 1) % 2, state[1])


def _dispatch(tile_start, tile_count, n_pass, fill_start, n_used, idx_t, lrank_t, h2, n_rows):
    t, d = h2.shape
    tm = ROUTE_TILE
    w = d // 2
    n_blocks = n_rows // FFN_ROWS
    kern = functools.partial(_dispatch_kernel, tm=tm, n_blocks=n_blocks)
    tok = pl.BlockSpec((TOP_K, tm), lambda i, *_: (0, i))
    grid_spec = pltpu.PrefetchScalarGridSpec(
        num_scalar_prefetch=5,
        grid=(t // tm,),
        in_specs=[tok, tok, pl.BlockSpec((tm, d), lambda i, *_: (i, 0))],
        out_specs=pl.BlockSpec(memory_space=pl.ANY),
        scratch_shapes=[pltpu.VMEM((2, N_SLOTS, w), jnp.uint32), pltpu.VMEM((ZERO_ROWS, w), jnp.uint32),
                        pltpu.SemaphoreType.DMA((2,)), pltpu.SemaphoreType.DMA, pltpu.SMEM((2,), jnp.int32)],
    )
    return pl.pallas_call(
        kern,
        out_shape=jax.ShapeDtypeStruct((n_rows, w), jnp.uint32),
        grid_spec=grid_spec,
        compiler_params=_cparams(("arbitrary",), VMEM_LIMIT),
        name="dispatch",
    )(tile_start, tile_count, n_pass, fill_start, n_used, idx_t, lrank_t, h2)


GU_GROUP = 256


def _ffn_kernel(be_ref, nu_ref, x_ref, wgu_ref, bgu_ref, wd_ref, bd_ref, perm_ref, y_ref, wgu_s, wd_s):
    i = pl.program_id(0)
    live = i < nu_ref[0]
    new_expert = jnp.logical_or(i == 0, be_ref[i] != be_ref[jnp.maximum(i - 1, 0)])

    @pl.when(jnp.logical_and(live, new_expert))
    def _():
        for g in range(2 * D_EXPERT // GU_GROUP):
            sl = slice(g * GU_GROUP, (g + 1) * GU_GROUP)
            wgu_s[:, sl] = _mm(wgu_ref[0, :, sl].astype(BF16), perm_ref[...]).astype(BF16)
        wd_s[...] = wd_ref[0].astype(BF16)

    @pl.when(live)
    def _():
        x = _unpack_bf16_pairs(x_ref[...])
        gu = _mm(x, wgu_s[...]) + bgu_ref[0]
        half = GU_GROUP // 2
        n_grp = 2 * D_EXPERT // GU_GROUP
        glu = jnp.concatenate([gu[:, g * GU_GROUP:g * GU_GROUP + half] for g in range(n_grp)], axis=1)
        lin = jnp.concatenate([gu[:, g * GU_GROUP + half:(g + 1) * GU_GROUP] for g in range(n_grp)], axis=1)
        glu = jnp.minimum(glu, SWIGLU_LIMIT)
        lin = jnp.clip(lin, -SWIGLU_LIMIT, SWIGLU_LIMIT)
        act = glu * _sigmoid(SWIGLU_ALPHA * glu) * (lin + 1.0)
        y = _mm(act.astype(BF16), wd_s[...]) + bd_ref[0]
        y_ref[...] = _pack_bf16_pairs(y.astype(BF16).astype(F32))

    @pl.when(jnp.logical_not(live))
    def _():
        y_ref[...] = jnp.zeros_like(y_ref)


def _ffn(block_e, n_used, xs, w_gu, b_gu, w_d, b_d):
    n_rows, w = xs.shape
    bm = FFN_ROWS
    nblk = n_rows // bm
    d = w * 2
    half = GU_GROUP // 2
    perm = np.zeros((GU_GROUP, GU_GROUP), np.float32)
    perm[2 * np.arange(half), np.arange(half)] = 1.0
    perm[2 * np.arange(half) + 1, half + np.arange(half)] = 1.0

    def xmap(i, be, nu):
        return (jnp.minimum(i, nu[0] - 1), 0)

    def wmap(i, be, nu):
        return (be[i], 0, 0)

    grid_spec = pltpu.PrefetchScalarGridSpec(
        num_scalar_prefetch=2,
        grid=(nblk,),
        in_specs=[pl.BlockSpec((bm, w), xmap),
                  pl.BlockSpec((1, d, 2 * D_EXPERT), wmap),
                  pl.BlockSpec((1, 1, 2 * D_EXPERT), wmap),
                  pl.BlockSpec((1, D_EXPERT, d), wmap),
                  pl.BlockSpec((1, 1, d), wmap),
                  pl.BlockSpec((GU_GROUP, GU_GROUP), lambda i, be, nu: (0, 0))],
        out_specs=pl.BlockSpec((bm, w), lambda i, be, nu: (i, 0)),
        scratch_shapes=[pltpu.VMEM((d, 2 * D_EXPERT), BF16), pltpu.VMEM((D_EXPERT, d), BF16)],
    )
    return pl.pallas_call(
        _ffn_kernel,
        out_shape=jax.ShapeDtypeStruct((n_rows, w), jnp.uint32),
        grid_spec=grid_spec,
        compiler_params=_cparams(("arbitrary",), VMEM_LIMIT),
        name="ffn",
    )(block_e, n_used, xs, w_gu, b_gu, w_d, b_d, jnp.asarray(perm, dtype=BF16))


def _combine_kernel(tstart_ref, tcnt_ref, npass_ref, idx_ref, lrank_ref, p_ref, y_ref, x1_ref, mod_ref, g_ref, o_ref,
                    ybuf, sem, *, tm, nt):
    b = pl.program_id(0)
    j = b * nt + pl.program_id(1)
    n_steps = pl.num_programs(0) * nt
    slot = j % 2

    def slot_copy(s, e, row):
        return pltpu.make_async_copy(y_ref.at[pl.ds(pl.multiple_of(row, ROW_ALIGN), SLOT_ROWS)], ybuf.at[s, pl.ds(e * SLOT_ROWS, SLOT_ROWS)],
                                     sem.at[s])

    def fetch(tile, c, s):
        for e in range(N_EXPERTS):
            @pl.when(tcnt_ref[tile * N_EXPERTS + e] > c * SLOT_ROWS)
            def _():
                slot_copy(s, e, tstart_ref[tile * N_EXPERTS + e] + c * SLOT_ROWS).start()

    def arrive(tile, c, s):
        live = [tcnt_ref[tile * N_EXPERTS + e] > c * SLOT_ROWS for e in range(N_EXPERTS)]
        n = sum(l.astype(jnp.int32) for l in live)

        @pl.when(n == N_EXPERTS)
        def _():
            pltpu.make_async_copy(y_ref.at[pl.ds(0, N_SLOTS)], ybuf.at[s], sem.at[s]).wait()

        @pl.when(n != N_EXPERTS)
        def _():
            for e in range(N_EXPERTS):
                @pl.when(live[e])
                def _():
                    slot_copy(s, e, 0).wait()

    @pl.when(j == 0)
    def _():
        ybuf[...] = jnp.zeros_like(ybuf)
        fetch(0, 0, 0)

    @pl.when(j + 1 < n_steps)
    def _():
        fetch(j + 1, 0, 1 - slot)

    def weights(c):
        cols = lax.broadcasted_iota(jnp.int32, (tm, N_SLOTS), 1)
        w = jnp.zeros((tm, N_SLOTS), F32)
        for k in range(TOP_K):
            w = jnp.where(cols == _slot_ids(idx_ref[:, k:k + 1], lrank_ref[:, k:k + 1], c), p_ref[:, k:k + 1], w)
        return w.astype(BF16)

    arrive(j, 0, slot)
    acc = _mm(weights(0), _unpack_bf16_pairs(ybuf[slot]))

    def extra_pass(c, acc):
        fetch(j, c, 2)
        arrive(j, c, 2)
        return acc + _mm(weights(c), _unpack_bf16_pairs(ybuf[2]))

    acc = lax.fori_loop(1, npass_ref[j], extra_pass, acc)
    m = mod_ref[pl.ds(b, 1), :]
    xo = x1_ref[0] + m[:, 5 * D_MODEL:6 * D_MODEL] * acc
    ms = jnp.mean(xo * xo, axis=-1, keepdims=True)
    o_ref[0] = xo * lax.rsqrt(ms + EPS) * g_ref[...]


def _combine(tile_start, tile_count, n_pass, idx_c, lrank_c, prob_c, y, x1, mod, g_fin):
    bsz, n, d = x1.shape
    tm = ROUTE_TILE
    nt = n // tm
    kern = functools.partial(_combine_kernel, tm=tm, nt=nt)
    tok = pl.BlockSpec((tm, TOP_K), lambda b, t, *_: (b * nt + t, 0))
    grid_spec = pltpu.PrefetchScalarGridSpec(
        num_scalar_prefetch=3,
        grid=(bsz, nt),
        in_specs=[tok, tok, tok,
                  pl.BlockSpec(memory_space=pl.ANY),
                  pl.BlockSpec((1, tm, d), lambda b, t, *_: (b, t, 0)),
                  pl.BlockSpec(mod.shape, lambda b, t, *_: (0, 0)),
                  pl.BlockSpec((1, d), lambda b, t, *_: (0, 0))],
        out_specs=pl.BlockSpec((1, tm, d), lambda b, t, *_: (b, t, 0)),
        scratch_shapes=[pltpu.VMEM((3, N_SLOTS, d // 2), jnp.uint32), pltpu.SemaphoreType.DMA((3,))],
    )
    return pl.pallas_call(
        kern,
        out_shape=jax.ShapeDtypeStruct((bsz, n, d), F32),
        grid_spec=grid_spec,
        compiler_params=_cparams(("arbitrary", "arbitrary"), VMEM_LIMIT),
        name="combine",
    )(tile_start, tile_count, n_pass, idx_c, lrank_c, prob_c, y, x1, mod, g_fin)


def _prep_w_in(w_in):
    q_a, kv, hq, ff, fb, hi, go, mm, mh = jnp.split(
        w_in, np.cumsum((MLA_Q_RANK, MLA_KV_RANK + MLA_ROPE) + (D_MODEL,) * 6).tolist(), axis=1)
    c_kv, kr = kv[:, :MLA_KV_RANK], kv[:, MLA_KV_RANK:]
    half = MLA_ROPE // 2
    kr_sw = jnp.concatenate([-kr[:, half:], kr[:, :half]], axis=1)
    w16 = jnp.concatenate([hq, hi, go, mm, mh, q_a, c_kv, kr, kr_sw], axis=1)
    w16 = jnp.pad(w16, ((0, 0), (0, N_PROJ16 - w16.shape[1])))
    return w16.astype(BF16), jnp.concatenate([ff, fb], axis=1).astype(BF16)


def _prep_w_q(w_q_b):
    w = w_q_b.reshape(MLA_Q_RANK, HEADS, MLA_QK)
    w = jnp.pad(w, ((0, 0), (0, 0), (0, MLA_QK_PAD - MLA_QK)))
    return w.reshape(MLA_Q_RANK, HEADS * MLA_QK_PAD).astype(BF16)


def _prep_w_kv(w_kv_b):
    w = w_kv_b.reshape(MLA_KV_RANK, HEADS, 2 * HEAD_DIM)
    k = w[:, :, :HEAD_DIM].reshape(MLA_KV_RANK, HEADS * HEAD_DIM)
    v = w[:, :, HEAD_DIM:].reshape(MLA_KV_RANK, HEADS * HEAD_DIM)
    return jnp.concatenate([k, v], axis=1).astype(BF16)


def _rope_tables(n_lat, n_ctx):
    f32 = np.float32
    rows = n_lat // GRID_W
    row = np.repeat(np.arange(rows), GRID_W).astype(f32)
    col = np.tile(np.arange(GRID_W), rows).astype(f32)
    n_freq = MLA_ROPE // 4
    inv = (f32(ROPE_BASE) ** (-np.arange(n_freq, dtype=f32) / f32(n_freq))).astype(f32)
    ang = np.concatenate([row[:, None] * inv, col[:, None] * inv], axis=-1).astype(f32)
    cos, sin = np.cos(ang).astype(f32), np.sin(ang).astype(f32)
    z64 = np.zeros((n_lat, 64), f32)
    cos_q = np.concatenate([cos, cos, z64], axis=1)
    sin_q = np.concatenate([-sin, sin, z64], axis=1)
    cos_k = np.concatenate([cos, cos, z64], axis=1)
    sin_k = np.concatenate([sin, sin, z64], axis=1)
    ctx_cos = np.concatenate([np.ones((n_ctx, 64), f32), np.zeros((n_ctx, 64), f32)], axis=1)
    cos_k = np.concatenate([cos_k, ctx_cos], axis=0)
    sin_k = np.concatenate([sin_k, np.zeros((n_ctx, 128), f32)], axis=0)
    return jnp.asarray(cos_q), jnp.asarray(sin_q), jnp.asarray(cos_k), jnp.asarray(sin_k)


def kernel(x, c, ctx, c_ctx, w_mod, b_mod, norm_mix_g, w_in, mla_q_norm_g, w_q_b, mla_kv_norm_g, w_kv_b,
           hg_lb_logits, hg_norm_g, w_out, norm_ffn_g, w_router, b_router, w_gate_up, b_gate_up, w_down,
           b_down, final_norm_g):
    bsz, n_lat, d = x.shape
    n_ctx = ctx.shape[1]
    assert d == D_MODEL and w_mod.shape[0] == 1
    assert n_lat % HG_CHUNK == 0 and n_ctx % HG_CHUNK == 0 and n_lat % GRID_W == 0
    t_tok = bsz * n_lat

    mod_rows = -(-(bsz + 1) // 8) * 8
    cc = jnp.concatenate([c, c_ctx[None, :], jnp.zeros((mod_rows - bsz - 1, d), F32)], axis=0)
    mod = _modulation(cc, w_mod[0], b_mod[0][None, :])

    xa = jnp.concatenate([x, ctx], axis=1)
    w16, w32 = _prep_w_in(w_in[0])
    p, pf = _inproj(xa, mod, norm_mix_g[0][None, :], w16, w32, n_lat)

    cos_q, sin_q, cos_k, sin_k = _rope_tables(n_lat, n_ctx)
    q = _mla_q(p, mla_q_norm_g[0][None, :], _prep_w_q(w_q_b[0]), cos_q, sin_q, n_lat)
    k, v = _mla_kv(p, mla_kv_norm_g[0][None, :], _prep_w_kv(w_kv_b[0]), cos_k, sin_k)
    y_mla = _attention(q, k, v)

    lb = jax.nn.softmax(hg_lb_logits.astype(F32), axis=1)[:, 0, :]
    o_f, o_b = _hgrn_scans(p, pf, lb.reshape(2, HEADS, 1, HEAD_DIM), n_lat)

    x1, h2, logits_t = _merge(y_mla, o_f, o_b, p, x, mod, hg_norm_g[0][None, :], w_out[0].astype(BF16),
                               norm_ffn_g[0][None, :], w_router[0].T, b_router[0][:, None])

    assert n_lat % ROUTE_TILE == 0
    idx_t, prob_t, lrank_t, base, cnt = _router(logits_t, ROUTE_TILE)

    n_tiles = t_tok // ROUTE_TILE
    counts = cnt[:, 0].astype(jnp.int32)
    base = base[:, :, 0].astype(jnp.int32)
    tile_count = jnp.concatenate([base[1:] - base[:-1], (counts - base[-1])[None, :]], axis=0)
    seg = (tile_count + ROW_ALIGN - 1) // ROW_ALIGN * ROW_ALIGN
    seg_end = jnp.cumsum(seg, axis=0)
    rows_e = seg_end[-1]
    padded = (rows_e + SLOT_ROWS + FFN_ROWS - 1) // FFN_ROWS * FFN_ROWS
    pad_end = jnp.cumsum(padded)
    pad_start = pad_end - padded
    max_rows = t_tok * TOP_K + N_EXPERTS * (n_tiles * (ROW_ALIGN - 1) + SLOT_ROWS + FFN_ROWS - 1)
    n_rows = -(-max_rows // FFN_ROWS) * FFN_ROWS
    nblk = n_rows // FFN_ROWS
    block_start = jnp.arange(nblk, dtype=jnp.int32) * FFN_ROWS
    block_e = jnp.minimum(jnp.sum(pad_end[None, :] <= block_start[:, None], axis=1), N_EXPERTS - 1).astype(jnp.int32)
    n_used = (pad_end[-1:] // FFN_ROWS).astype(jnp.int32)
    tile_start = (pad_start[None, :] + seg_end - seg).reshape(-1).astype(jnp.int32)
    n_pass = jnp.maximum(-(-jnp.max(tile_count, axis=1) // SLOT_ROWS), 1).astype(jnp.int32)
    tile_count = tile_count.reshape(-1)
    fill_start = jnp.minimum(pad_start + rows_e, n_rows - ZERO_ROWS).astype(jnp.int32)

    xs = _dispatch(tile_start, tile_count, n_pass, fill_start, n_used, idx_t, lrank_t, h2, n_rows)

    n_grp = 2 * D_EXPERT // GU_GROUP
    b_gu = b_gate_up[0].reshape(N_EXPERTS, n_grp, GU_GROUP // 2, 2).transpose(0, 1, 3, 2).reshape(N_EXPERTS, 1, 2 * D_EXPERT)
    y = _ffn(block_e, n_used, xs, w_gate_up[0], b_gu, w_down[0], b_down[0][:, None, :])

    return _combine(tile_start, tile_count, n_pass, idx_t.T, lrank_t.T, prob_t.T, y, x1, mod, final_norm_g[None, :])
```

```python
import functools

import jax
import jax.numpy as jnp
import numpy as np
from jax import lax
from jax.experimental import pallas as pl
from jax.experimental.pallas import tpu as pltpu

F32 = jnp.float32
BF16 = jnp.bfloat16

D_MODEL = 1024
EPS = 1e-6
LOG2_E = 1.4426950408889634
N_MOD = 6
GRID_W = 64
ROPE_BASE = 10000.0

HEADS = 8
HEAD_DIM = 128
MLA_ROPE = 64
MLA_QK = HEAD_DIM + MLA_ROPE
MLA_QK_PAD = 256
MLA_Q_RANK = 768
MLA_KV_RANK = 256

N_EXPERTS = 32
TOP_K = 4
D_EXPERT = 1024
SWIGLU_LIMIT = 7.0
SWIGLU_ALPHA = 1.702

HG_CHUNK = 128
FFN_ROWS = 512

COL_HQ, COL_I, COL_GO, COL_MM, COL_MH = (i * D_MODEL for i in range(5))
COL_QA = 5 * D_MODEL
COL_CKV = COL_QA + MLA_Q_RANK
COL_KR = COL_CKV + MLA_KV_RANK
PROJ16_TN = 1280
N_PROJ16 = -(-(COL_KR + 2 * MLA_ROPE) // PROJ16_TN) * PROJ16_TN
COL_FF, COL_FB = 0, D_MODEL
N_PROJ32 = 2 * D_MODEL
PROJ32_TN = 1024

VMEM_LIMIT = 56 * 1024 * 1024


def _cparams(sem, vmem=None):
    return pltpu.CompilerParams(dimension_semantics=sem, vmem_limit_bytes=vmem)


def _tile(n, pref, mult=8):
    best = None
    for t in range(mult, min(n, pref) + 1, mult):
        if n % t == 0:
            best = t
    assert best is not None, (n, pref, mult)
    return best


def _nt(a, b):
    return lax.dot_general(a, b, (((1,), (1,)), ((), ())), preferred_element_type=F32)


def _mm(a, b):
    return jnp.dot(a, b, preferred_element_type=F32)


def _split2(a):
    hi = a.astype(BF16)
    lo = (a - hi.astype(F32)).astype(BF16)
    return hi, lo


def _sigmoid(x):
    return 1.0 / (1.0 + jnp.exp(-x))


def _gate_sigmoid(x):
    return 0.5 * jnp.tanh(0.5 * x) + 0.5


def _mod_kernel(c_ref, w_ref, b_ref, o_ref):
    c = c_ref[...]
    s = c * _sigmoid(c)
    s_hi, s_lo = _split2(s)
    w_hi, w_lo = _split2(w_ref[...])
    o_ref[...] = _mm(s_hi, w_hi) + _mm(s_lo, w_hi) + _mm(s_hi, w_lo) + b_ref[...]


def _modulation(cc, w_mod, b_mod):
    r, d = cc.shape
    n = w_mod.shape[1]
    tn = _tile(n, 1536, 128)
    return pl.pallas_call(
        _mod_kernel,
        out_shape=jax.ShapeDtypeStruct((r, n), F32),
        grid=(n // tn,),
        in_specs=[pl.BlockSpec((r, d), lambda j: (0, 0)),
                  pl.BlockSpec((d, tn), lambda j: (0, j)),
                  pl.BlockSpec((1, tn), lambda j: (0, j))],
        out_specs=pl.BlockSpec((r, tn), lambda j: (0, j)),
        compiler_params=_cparams(("parallel",), VMEM_LIMIT),
        name="mod",
    )(cc, w_mod, b_mod)


def _inproj_kernel(x_ref, mod_ref, g_ref, w16_ref, w32_ref, o16_ref, o32_ref, hn_ref, *, n_lat, tm, ctx_row, n16):
    b = pl.program_id(0)
    t = pl.program_id(1)
    j = pl.program_id(2)

    @pl.when(j == 0)
    def _():
        x = x_ref[0]
        ms = jnp.mean(x * x, axis=-1, keepdims=True)
        y = x * lax.rsqrt(ms + EPS) * g_ref[...]
        row = t * tm + lax.broadcasted_iota(jnp.int32, (tm, 1), 0)
        is_ctx = row >= n_lat
        m_lat = mod_ref[pl.ds(b, 1), :]
        m_ctx = mod_ref[pl.ds(ctx_row, 1), :]
        shift = jnp.where(is_ctx, m_ctx[:, 0:D_MODEL], m_lat[:, 0:D_MODEL])
        scale = jnp.where(is_ctx, m_ctx[:, D_MODEL:2 * D_MODEL], m_lat[:, D_MODEL:2 * D_MODEL])
        hn_ref[...] = (y * (1.0 + scale) + shift).astype(BF16)

    @pl.when(j < n16)
    def _():
        o16_ref[0] = _mm(hn_ref[...], w16_ref[...]).astype(BF16)

    @pl.when(j >= n16)
    def _():
        o32_ref[0] = _mm(hn_ref[...], w32_ref[...])


def _inproj(xa, mod, g, w16, w32, n_lat):
    bsz, rows, d = xa.shape
    tm = _tile(rows, 1152)
    n16 = N_PROJ16 // PROJ16_TN
    n32 = N_PROJ32 // PROJ32_TN

    def j16(j):
        return jnp.minimum(j, n16 - 1)

    def j32(j):
        return jnp.maximum(j - n16, 0)

    kern = functools.partial(_inproj_kernel, n_lat=n_lat, tm=tm, ctx_row=bsz, n16=n16)
    return pl.pallas_call(
        kern,
        out_shape=(jax.ShapeDtypeStruct((bsz, rows, N_PROJ16), BF16),
                   jax.ShapeDtypeStruct((bsz, rows, N_PROJ32), F32)),
        grid=(bsz, rows // tm, n16 + n32),
        in_specs=[pl.BlockSpec((1, tm, d), lambda b, t, j: (b, t, 0)),
                  pl.BlockSpec(mod.shape, lambda b, t, j: (0, 0)),
                  pl.BlockSpec((1, d), lambda b, t, j: (0, 0)),
                  pl.BlockSpec((d, PROJ16_TN), lambda b, t, j: (0, j16(j))),
                  pl.BlockSpec((d, PROJ32_TN), lambda b, t, j: (0, j32(j)))],
        out_specs=(pl.BlockSpec((1, tm, PROJ16_TN), lambda b, t, j: (b, t, j16(j))),
                   pl.BlockSpec((1, tm, PROJ32_TN), lambda b, t, j: (b, t, j32(j)))),
        scratch_shapes=[pltpu.VMEM((tm, d), BF16)],
        compiler_params=_cparams(("parallel", "parallel", "arbitrary"), VMEM_LIMIT),
        name="inproj",
    )(xa, mod, g, w16, w32)


def _mlaq_kernel(a0_ref, a1_ref, a2_ref, g_ref, w_ref, cos_ref, sin_ref, o_ref, *, tm, n_parts):
    n_rows = tm // n_parts
    scale = MLA_QK ** -0.5 * LOG2_E

    def part(i):
        rows = slice(i * n_rows, (i + 1) * n_rows)
        parts = [a_ref[0, rows, :].astype(F32) for a_ref in (a0_ref, a1_ref, a2_ref)]
        ss = sum(jnp.sum(p * p, axis=-1, keepdims=True) for p in parts)
        r = lax.rsqrt(ss * (1.0 / MLA_Q_RANK) + EPS)
        acc = None
        for j, p in enumerate(parts):
            hj = (p * r * g_ref[:, j * 256:(j + 1) * 256]).astype(BF16)
            d = _mm(hj, w_ref[j * 256:(j + 1) * 256, :])
            acc = d if acc is None else acc + d
        yield
        lane = lax.broadcasted_iota(jnp.int32, (n_rows, HEAD_DIM), 1)
        cos = cos_ref[rows, :]
        sin = sin_ref[rows, :]
        for h in range(HEADS):
            nope = acc[:, h * MLA_QK_PAD:h * MLA_QK_PAD + HEAD_DIM]
            rp = acc[:, h * MLA_QK_PAD + HEAD_DIM:(h + 1) * MLA_QK_PAD]
            swapped = jnp.where(lane < 32, pltpu.roll(rp, 96, 1), pltpu.roll(rp, 32, 1))
            rot = rp * cos + swapped * sin
            o_ref[0, h, rows, 0:HEAD_DIM] = (nope * scale).astype(BF16)
            o_ref[0, h, rows, HEAD_DIM:MLA_QK_PAD] = (rot * scale).astype(BF16)

    _round_robin([part(i) for i in range(n_parts)])


def _mla_q(p, g, w, cos_q, sin_q, n_lat):
    bsz = p.shape[0]
    tm = _tile(n_lat, 512)
    cb = COL_QA // 256
    kern = functools.partial(_mlaq_kernel, tm=tm, n_parts=2 if tm % 32 == 0 else 1)
    return pl.pallas_call(
        kern,
        out_shape=jax.ShapeDtypeStruct((bsz, HEADS, n_lat, MLA_QK_PAD), BF16),
        grid=(bsz, n_lat // tm),
        in_specs=[pl.BlockSpec((1, tm, 256), lambda b, t: (b, t, cb)),
                  pl.BlockSpec((1, tm, 256), lambda b, t: (b, t, cb + 1)),
                  pl.BlockSpec((1, tm, 256), lambda b, t: (b, t, cb + 2)),
                  pl.BlockSpec((1, MLA_Q_RANK), lambda b, t: (0, 0)),
                  pl.BlockSpec(w.shape, lambda b, t: (0, 0)),
                  pl.BlockSpec((tm, HEAD_DIM), lambda b, t: (t, 0)),
                  pl.BlockSpec((tm, HEAD_DIM), lambda b, t: (t, 0))],
        out_specs=pl.BlockSpec((1, HEADS, tm, MLA_QK_PAD), lambda b, t: (b, 0, t, 0)),
        compiler_params=_cparams(("parallel", "parallel"), VMEM_LIMIT),
        name="mla_q",
    )(p, p, p, g, w, cos_q, sin_q)


def _mlakv_kernel(c_ref, kr_ref, g_ref, w_ref, cos_ref, sin_ref, k_ref, v_ref):
    c = c_ref[0].astype(F32)
    ms = jnp.mean(c * c, axis=-1, keepdims=True)
    hn = (c * lax.rsqrt(ms + EPS) * g_ref[...]).astype(BF16)
    kv = _mm(hn, w_ref[...])
    grp = kr_ref[0].astype(F32)
    rot = (grp * cos_ref[...] + pltpu.roll(grp, 64, 1) * sin_ref[...]).astype(BF16)
    for h in range(HEADS):
        k_ref[0, h, :, 0:HEAD_DIM] = kv[:, h * HEAD_DIM:(h + 1) * HEAD_DIM].astype(BF16)
        k_ref[0, h, :, HEAD_DIM:MLA_QK_PAD] = rot
        v_ref[0, h] = kv[:, D_MODEL + h * HEAD_DIM:D_MODEL + (h + 1) * HEAD_DIM].astype(BF16)


def _mla_kv(p, g, w, cos_k, sin_k):
    bsz, rows, _ = p.shape
    tm = _tile(rows, 768)
    return pl.pallas_call(
        _mlakv_kernel,
        out_shape=(jax.ShapeDtypeStruct((bsz, HEADS, rows, MLA_QK_PAD), BF16),
                   jax.ShapeDtypeStruct((bsz, HEADS, rows, HEAD_DIM), BF16)),
        grid=(bsz, rows // tm),
        in_specs=[pl.BlockSpec((1, tm, MLA_KV_RANK), lambda b, t: (b, t, COL_CKV // MLA_KV_RANK)),
                  pl.BlockSpec((1, tm, 128), lambda b, t: (b, t, COL_KR // 128)),
                  pl.BlockSpec((1, MLA_KV_RANK), lambda b, t: (0, 0)),
                  pl.BlockSpec(w.shape, lambda b, t: (0, 0)),
                  pl.BlockSpec((tm, 128), lambda b, t: (t, 0)),
                  pl.BlockSpec((tm, 128), lambda b, t: (t, 0))],
        out_specs=(pl.BlockSpec((1, HEADS, tm, MLA_QK_PAD), lambda b, t: (b, 0, t, 0)),
                   pl.BlockSpec((1, HEADS, tm, HEAD_DIM), lambda b, t: (b, 0, t, 0))),
        compiler_params=_cparams(("parallel", "parallel"), VMEM_LIMIT),
        name="mla_kv",
    )(p, p, g, w, cos_k, sin_k)


def _attn_kernel(q_ref, k_ref, v_ref, o_ref, *, tq, sub):
    k = k_ref[0, 0]
    v = v_ref[0, 0]
    v1 = jnp.concatenate([v, jnp.ones_like(v)], axis=1)

    def sub_tile(r):
        rows = slice(r * sub, (r + 1) * sub)
        s = _nt(q_ref[0, 0, rows, :], k)
        yield
        m = jnp.max(s, axis=-1, keepdims=True)
        p = jnp.exp2(s - m).astype(BF16)
        yield
        ol = _mm(p, v1)
        o_ref[0, rows, :] = (ol[:, 0:HEAD_DIM] * (1.0 / ol[:, HEAD_DIM:HEAD_DIM + 1])).astype(BF16)

    _round_robin([sub_tile(r) for r in range(tq // sub)])


def _attention(q, k, v):
    bsz, _, n, _ = q.shape
    m = k.shape[2]
    tq = _tile(n, 2048)
    sub = _tile(tq, 512)
    kern = functools.partial(_attn_kernel, tq=tq, sub=sub)
    return pl.pallas_call(
        kern,
        out_shape=jax.ShapeDtypeStruct((bsz, n, HEADS * HEAD_DIM), BF16),
        grid=(bsz, HEADS, n // tq),
        in_specs=[pl.BlockSpec((1, 1, tq, MLA_QK_PAD), lambda b, h, t: (b, h, t, 0)),
                  pl.BlockSpec((1, 1, m, MLA_QK_PAD), lambda b, h, t: (b, h, 0, 0)),
                  pl.BlockSpec((1, 1, m, HEAD_DIM), lambda b, h, t: (b, h, 0, 0))],
        out_specs=pl.BlockSpec((1, tq, HEAD_DIM), lambda b, h, t: (b, t, h)),
        compiler_params=_cparams(("parallel", "parallel", "arbitrary"), VMEM_LIMIT),
        name="attention",
    )(q, k, v)


def _level_ref(cum, blk, reverse):
    c = cum.shape[0]
    half = blk // 2
    r = half if reverse else half - 1
    if blk >= 8:
        x = cum.reshape(c // blk, blk, HEAD_DIM)
        e = jnp.broadcast_to(x[:, r:r + 1, :], x.shape)
        return e.reshape(c, HEAD_DIM)
    x = cum.reshape(c // 8, 8, HEAD_DIM)
    sub = lax.broadcasted_iota(jnp.int32, x.shape, 1)
    e = None
    for jb in range(8 // blk):
        cand = jnp.broadcast_to(x[:, jb * blk + r:jb * blk + r + 1, :], x.shape)
        e = cand if e is None else jnp.where(sub >= jb * blk, cand, e)
    return e.reshape(c, HEAD_DIM)


HG_SUB = 32
HG_SUB_LEVEL = HG_SUB.bit_length() - 1
HG_MAX_EXP2 = 100.0


def _hgrn_state(z, v, lb, tri, st, *, chunk, reverse):
    f = lb + (1.0 - lb) * _sigmoid(z)
    kb = (1.0 - f).astype(BF16)
    g = jnp.log(f) * LOG2_E

    g_hi, g_lo = _split2(g)
    yield
    cum = _mm(tri, jnp.concatenate([g_hi, g_lo], axis=0))
    yield
    last = 0 if reverse else chunk - 1
    tot = cum[last:last + 1, :]
    kt = kb * jnp.exp2(tot - cum).astype(BF16)
    vt = v.astype(F32).T.astype(BF16)
    yield
    st_new = st * jnp.exp2(tot) + _mm(vt, kt)
    return kb, cum, st_new


def _sub_block_decay(cum, reverse):
    c = cum.shape[0]
    x = cum.reshape(c // HG_SUB, HG_SUB, HEAD_DIM)
    zero = jnp.zeros((1, 1, HEAD_DIM), F32)
    if reverse:
        edge = jnp.concatenate([x[1:, 0:1, :], zero], axis=0)
    else:
        edge = jnp.concatenate([zero, x[:-1, HG_SUB - 1:HG_SUB, :]], axis=0)
    return (x - edge).reshape(c, HEAD_DIM)


def _hgrn_readout(q, v, kb, cum, sub, lvl, st, *, chunk, reverse, shared):
    q = q.astype(F32)
    qb = (q * _gate_sigmoid(q) * (HEAD_DIM ** -0.5)).astype(BF16)
    yield
    if shared:
        att = jnp.where(lvl == 0, _nt(qb * jnp.exp2(sub).astype(BF16), kb * jnp.exp2(-sub).astype(BF16)).astype(BF16),
                        jnp.zeros((), BF16))
        first = HG_SUB_LEVEL + 1
    else:
        att = jnp.where(lvl == 0, _nt(qb, kb).astype(BF16), jnp.zeros((), BF16))
        first = 1
    for lv in range(first, chunk.bit_length()):
        zrel = cum - _level_ref(cum, 1 << lv, reverse)
        neg_abs = pltpu.bitcast(pltpu.bitcast(zrel, jnp.uint32) | jnp.uint32(0x80000000), F32)
        e = jnp.exp2(neg_abs).astype(BF16)
        yield
        att = jnp.where(lvl == lv, _nt(qb * e, kb * e).astype(BF16), att)
        yield
    return _nt(qb * jnp.exp2(cum).astype(BF16), st.astype(BF16)) + _mm(att, v.astype(BF16))


def _round_robin(gens):
    out = [None] * len(gens)
    active = list(range(len(gens)))
    while active:
        for i in list(active):
            try:
                next(gens[i])
            except StopIteration as e:
                out[i] = e.value
                active.remove(i)
    return out


def _hgrn_kernel(qf_ref, zf_ref, if_ref, qb_ref, zb_ref, ib_ref, lb_ref, tri_ref, lvl_ref, of_ref, ob_ref, st_ref,
                 *, chunk, group, n_ctx_chunks):
    step = pl.program_id(2)

    @pl.when(step == 0)
    def _():
        st_ref[...] = jnp.zeros_like(st_ref)

    ins = ((qf_ref, zf_ref, if_ref), (qb_ref, zb_ref, ib_ref))
    outs = (of_ref, ob_ref)
    chains = [(d, j) for j in range(group) for d in range(2)]

    def cols(j):
        return slice(j * HEAD_DIM, (j + 1) * HEAD_DIM)

    def advance(want_out):
        old = [st_ref[d, j] for d, j in chains]
        parts = _round_robin([_hgrn_state(ins[d][1][0, :, cols(j)], ins[d][2][0, :, cols(j)], lb_ref[d, j],
                                          tri_ref[d], st, chunk=chunk, reverse=bool(d))
                              for (d, j), st in zip(chains, old)])
        for (d, j), (_, _, st_new) in zip(chains, parts):
            st_ref[d, j] = st_new
        if not want_out:
            return
        subs = [_sub_block_decay(cum, bool(d)) for (d, j), (_, cum, _) in zip(chains, parts)]
        low = subs[0]
        for s in subs[1:]:
            low = jnp.minimum(low, s)
        low = jnp.min(jnp.min(low, axis=0, keepdims=True), axis=1, keepdims=True)
        in_range = low[0, 0] >= -HG_MAX_EXP2

        def readouts(shared):
            os_ = _round_robin([_hgrn_readout(ins[d][0][0, :, cols(j)], ins[d][2][0, :, cols(j)], kb, cum, sub,
                                              lvl_ref[d, int(shared)], st, chunk=chunk, reverse=bool(d), shared=shared)
                                for (d, j), (kb, cum, _), sub, st in zip(chains, parts, subs, old)])
            for (d, j), o in zip(chains, os_):
                outs[d][0, j] = o.astype(BF16)

        @pl.when(in_range)
        def _():
            readouts(True)

        @pl.when(jnp.logical_not(in_range))
        def _():
            readouts(False)

    @pl.when(step < n_ctx_chunks)
    def _():
        advance(False)

    @pl.when(step >= n_ctx_chunks)
    def _():
        advance(True)


def _hgrn_consts(chunk, reverse):
    t = np.arange(chunk)[:, None]
    s = np.arange(chunk)[None, :]
    x = t ^ s
    bitlen = np.zeros_like(x)
    for b in range(chunk.bit_length()):
        bitlen = np.where(x >> b > 0, b + 1, bitlen)
    valid = (s > t) if reverse else (t > s)
    lvl = np.where(t == s, 0, np.where(valid, bitlen, -1)).astype(np.float32)
    lvl_shared = np.where((lvl >= 0) & (lvl <= HG_SUB_LEVEL), 0, lvl)
    tri = ((s >= t) if reverse else (t >= s)).astype(np.float32)
    return np.concatenate([tri, tri], axis=1), np.stack([lvl, lvl_shared])


HG_GROUP = 8


def _hgrn_scans(p, pf, lb, n_lat):
    bsz, rows, _ = p.shape
    c = HG_CHUNK
    grp = HG_GROUP
    n_chunks = rows // c
    n_lat_c = n_lat // c
    n_ctx_c = n_chunks - n_lat_c
    consts = [_hgrn_consts(c, False), _hgrn_consts(c, True)]
    tri = jnp.asarray(np.stack([consts[0][0], consts[1][0]]), dtype=BF16)
    lvl = jnp.asarray(np.stack([consts[0][1], consts[1][1]]), dtype=BF16)

    def cidx_f(i):
        return jnp.where(i < n_ctx_c, n_lat_c + i, i - n_ctx_c)

    def cidx_b(i):
        return n_chunks - 1 - i

    def col_spec(col, cidx):
        return pl.BlockSpec((1, c, grp * HEAD_DIM), lambda b, h, i: (b, cidx(i), col // (grp * HEAD_DIM) + h))

    def out_spec(cidx):
        return pl.BlockSpec((1, grp, c, HEAD_DIM), lambda b, h, i: (b, h, cidx(jnp.maximum(i, n_ctx_c)), 0))

    o_shape = jax.ShapeDtypeStruct((bsz, HEADS, n_lat, HEAD_DIM), BF16)
    kern = functools.partial(_hgrn_kernel, chunk=c, group=grp, n_ctx_chunks=n_ctx_c)
    return pl.pallas_call(
        kern,
        out_shape=(o_shape, o_shape),
        grid=(bsz, HEADS // grp, n_chunks),
        in_specs=[col_spec(COL_HQ, cidx_f), col_spec(COL_FF, cidx_f), col_spec(COL_I, cidx_f),
                  col_spec(COL_HQ, cidx_b), col_spec(COL_FB, cidx_b), col_spec(COL_I, cidx_b),
                  pl.BlockSpec((2, grp, 1, HEAD_DIM), lambda b, h, i: (0, h, 0, 0)),
                  pl.BlockSpec((2, c, 2 * c), lambda b, h, i: (0, 0, 0)),
                  pl.BlockSpec((2, 2, c, c), lambda b, h, i: (0, 0, 0, 0))],
        out_specs=(out_spec(cidx_f), out_spec(cidx_b)),
        scratch_shapes=[pltpu.VMEM((2, grp, HEAD_DIM, HEAD_DIM), F32)],
        compiler_params=_cparams(("parallel", "parallel", "arbitrary"), VMEM_LIMIT),
        name="hgrn",
    )(p, pf, p, p, pf, p, lb, tri, lvl)


def _merge_kernel(ym_ref, of_ref, ob_ref, go_ref, gm_ref, gh_ref, x_ref, mod_ref, hgg_ref, wout_ref,
                  gffn_ref, wr2_ref, wrhi_ref, br_ref, x1_ref, h2_ref, lg_ref, y_scr, *, tm, parts):
    b = pl.program_id(0)
    m = mod_ref[pl.ds(b, 1), :]
    n_rows = tm // parts

    def part(i):
        rows = slice(i * n_rows, (i + 1) * n_rows)
        for h in range(HEADS):
            sl = slice(h * HEAD_DIM, (h + 1) * HEAD_DIM)
            o = of_ref[0, h, rows, :].astype(F32) + ob_ref[0, h, rows, :].astype(F32)
            ms = jnp.mean(o * o, axis=-1, keepdims=True)
            g = go_ref[0, rows, sl].astype(F32)
            yh = o * lax.rsqrt(ms + EPS) * hgg_ref[...] * (g * _gate_sigmoid(g))
            y = (_gate_sigmoid(gm_ref[0, rows, sl].astype(F32)) * ym_ref[0, rows, sl].astype(F32)
                 + _gate_sigmoid(gh_ref[0, rows, sl].astype(F32)) * yh)
            y_scr[rows, sl] = y.astype(BF16)
        yield
        mix = _mm(y_scr[rows, :], wout_ref[...])
        yield
        x1 = x_ref[0, rows, :] + m[:, 2 * D_MODEL:3 * D_MODEL] * mix
        x1_ref[0, rows, :] = x1
        ms = jnp.mean(x1 * x1, axis=-1, keepdims=True)
        h2 = ((x1 * lax.rsqrt(ms + EPS) * gffn_ref[...]) * (1.0 + m[:, 4 * D_MODEL:5 * D_MODEL])
              + m[:, 3 * D_MODEL:4 * D_MODEL])
        h_hi = h2.astype(BF16)
        h2_ref[rows, :] = h_hi
        h_lo = (h2 - h_hi.astype(F32)).astype(BF16)
        a = _mm(h_hi, wr2_ref[...])
        lg_ref[rows, :] = (a[:, 0:N_EXPERTS] + a[:, N_EXPERTS:2 * N_EXPERTS] + _mm(h_lo, wrhi_ref[...])
                           + br_ref[...])

    _round_robin([part(i) for i in range(parts)])


def _merge(y_mla, o_f, o_b, p, x, mod, hg_g, w_out, g_ffn, w_r, b_r):
    bsz, n, d = x.shape
    tm = _tile(n, 256, 128)
    nt = n // tm
    w_r_hi, w_r_lo = _split2(w_r)
    w_r2 = jnp.concatenate([w_r_hi, w_r_lo], axis=1)
    kern = functools.partial(_merge_kernel, tm=tm, parts=2)

    def pcol(col):
        return pl.BlockSpec((1, tm, d), lambda b, t: (b, t, col // d))

    tok = lambda b, t: (b, t, 0)
    const2 = lambda b, t: (0, 0)
    return pl.pallas_call(
        kern,
        out_shape=(jax.ShapeDtypeStruct((bsz, n, d), F32),
                   jax.ShapeDtypeStruct((bsz * n, d), BF16),
                   jax.ShapeDtypeStruct((bsz * n, N_EXPERTS), F32)),
        grid=(bsz, nt),
        in_specs=[pl.BlockSpec((1, tm, d), tok),
                  pl.BlockSpec((1, HEADS, tm, HEAD_DIM), lambda b, t: (b, 0, t, 0)),
                  pl.BlockSpec((1, HEADS, tm, HEAD_DIM), lambda b, t: (b, 0, t, 0)),
                  pcol(COL_GO), pcol(COL_MM), pcol(COL_MH),
                  pl.BlockSpec((1, tm, d), tok),
                  pl.BlockSpec(mod.shape, const2),
                  pl.BlockSpec((1, HEAD_DIM), const2),
                  pl.BlockSpec((d, d), const2),
                  pl.BlockSpec((1, d), const2),
                  pl.BlockSpec((d, 2 * N_EXPERTS), const2),
                  pl.BlockSpec((d, N_EXPERTS), const2),
                  pl.BlockSpec((1, N_EXPERTS), const2)],
        out_specs=(pl.BlockSpec((1, tm, d), tok),
                   pl.BlockSpec((tm, d), lambda b, t: (b * nt + t, 0)),
                   pl.BlockSpec((tm, N_EXPERTS), lambda b, t: (b * nt + t, 0))),
        scratch_shapes=[pltpu.VMEM((tm, d), BF16)],
        compiler_params=_cparams(("parallel", "parallel"), VMEM_LIMIT),
        name="merge",
    )(y_mla, o_f, o_b, p, p, p, x, mod, hg_g, w_out, g_ffn, w_r2, w_r_hi, b_r)


def _router_kernel(lg_ref, upper_ref, idx_ref, prob_ref, rank_ref, base_ref, cnt_ref, run_ref, *, tm):
    @pl.when(pl.program_id(0) == 0)
    def _():
        run_ref[...] = jnp.zeros_like(run_ref)

    l = lg_ref[...]
    eidx = lax.broadcasted_iota(jnp.int32, (N_EXPERTS, tm), 0)
    vals, sel = [], []
    for _ in range(TOP_K):
        m = jnp.max(l, axis=0, keepdims=True)
        first = jnp.min(jnp.where(l == m, eidx, N_EXPERTS), axis=0, keepdims=True)
        vals.append(m)
        sel.append(first)
        l = jnp.where(eidx == first, -jnp.inf, l)
    ex = [jnp.exp(v - vals[0]) for v in vals]
    inv = 1.0 / (ex[0] + ex[1] + ex[2] + ex[3])
    onehot = jnp.zeros((N_EXPERTS, tm), F32)
    for k in range(TOP_K):
        onehot = onehot + jnp.where(eidx == sel[k], 1.0, 0.0)
    before = _mm(onehot.astype(BF16), upper_ref[...])
    base_ref[0] = run_ref[...]
    for k in range(TOP_K):
        idx_ref[k:k + 1, :] = sel[k]
        prob_ref[k:k + 1, :] = ex[k] * inv
        rank_ref[k:k + 1, :] = jnp.sum(jnp.where(eidx == sel[k], before, 0.0), axis=0, keepdims=True).astype(jnp.int32)
    run_ref[...] = run_ref[...] + jnp.sum(onehot, axis=1, keepdims=True)
    cnt_ref[...] = run_ref[...]


def _router(logits_t, tm):
    _, t = logits_t.shape
    upper = jnp.asarray(np.triu(np.ones((tm, tm), np.float32), 1), dtype=BF16)
    kern = functools.partial(_router_kernel, tm=tm)
    tok = pl.BlockSpec((TOP_K, tm), lambda i: (0, i))
    return pl.pallas_call(
        kern,
        out_shape=(jax.ShapeDtypeStruct((TOP_K, t), jnp.int32),
                   jax.ShapeDtypeStruct((TOP_K, t), F32),
                   jax.ShapeDtypeStruct((TOP_K, t), jnp.int32),
                   jax.ShapeDtypeStruct((t // tm, N_EXPERTS, 128), F32),
                   jax.ShapeDtypeStruct((N_EXPERTS, 128), F32)),
        grid=(t // tm,),
        in_specs=[pl.BlockSpec((N_EXPERTS, tm), lambda i: (0, i)),
                  pl.BlockSpec((tm, tm), lambda i: (0, 0))],
        out_specs=(tok, tok, tok, pl.BlockSpec((1, N_EXPERTS, 128), lambda i: (i, 0, 0)),
                   pl.BlockSpec((N_EXPERTS, 128), lambda i: (0, 0))),
        scratch_shapes=[pltpu.VMEM((N_EXPERTS, 128), F32)],
        compiler_params=_cparams(("arbitrary",), VMEM_LIMIT),
        name="router",
    )(logits_t, upper)


ROUTE_TILE = 256
ROW_ALIGN = 8
SLOT_ROWS = 48
N_SLOTS = N_EXPERTS * SLOT_ROWS
ZERO_ROWS = FFN_ROWS + SLOT_ROWS


def _slot_ids(idx, lrank, c):
    r = lrank - c * SLOT_ROWS
    return jnp.where(jnp.logical_and(r >= 0, r < SLOT_ROWS), idx * SLOT_ROWS + r, -1)


def _pack_bf16_pairs(x):
    u = pltpu.bitcast(x, jnp.uint32)
    w = x.shape[1] // 2
    return (u[:, 0:w] >> 16) | u[:, w:2 * w]


def _unpack_bf16_pairs(u):
    lo = pltpu.bitcast(u << 16, F32).astype(BF16)
    hi = pltpu.bitcast(u & jnp.uint32(0xFFFF0000), F32).astype(BF16)
    return jnp.concatenate([lo, hi], axis=1)


def _dispatch_kernel(tstart_ref, tcnt_ref, npass_ref, fill_ref, nused_ref, idx_ref, lrank_ref, h2_ref, xs_ref,
                     xbuf, zbuf, sem, zsem, state, *, tm, n_blocks):
    j = pl.program_id(0)

    def slot_copy(slot, e, row):
        return pltpu.make_async_copy(xbuf.at[slot, pl.ds(e * SLOT_ROWS, SLOT_ROWS)],
                                     xs_ref.at[pl.ds(pl.multiple_of(row, ROW_ALIGN), SLOT_ROWS)], sem.at[slot])

    def wait_pass(slot, n):
        @pl.when(n == N_EXPERTS)
        def _():
            pltpu.make_async_copy(xbuf.at[slot], xs_ref.at[pl.ds(0, N_SLOTS)], sem.at[slot]).wait()

        @pl.when(n != N_EXPERTS)
        def _():
            def body(i, carry):
                slot_copy(slot, 0, 0).wait()
                return carry
            lax.fori_loop(0, n, body, 0)

    @pl.when(j == 0)
    def _():
        zbuf[...] = jnp.zeros_like(zbuf)
        state[0] = 0
        state[1] = 0
        for parity in range(2):
            for e in range(parity, N_EXPERTS, 2):
                pltpu.make_async_copy(zbuf, xs_ref.at[pl.ds(pl.multiple_of(fill_ref[e], ROW_ALIGN), ZERO_ROWS)],
                                      zsem).start()
            for e in range(parity, N_EXPERTS, 2):
                pltpu.make_async_copy(zbuf, xs_ref.at[pl.ds(0, ZERO_ROWS)], zsem).wait()

        def tail(i, carry):
            pltpu.make_async_copy(zbuf.at[pl.ds(0, FFN_ROWS)],
                                  xs_ref.at[pl.ds(pl.multiple_of(i * FFN_ROWS, ROW_ALIGN), FFN_ROWS)], zsem).start()
            return carry
        lax.fori_loop(nused_ref[0], n_blocks, tail, 0)

        def tail_wait(i, carry):
            pltpu.make_async_copy(zbuf.at[pl.ds(0, FFN_ROWS)], xs_ref.at[pl.ds(0, FFN_ROWS)], zsem).wait()
            return carry
        lax.fori_loop(nused_ref[0], n_blocks, tail_wait, 0)

    def one_pass(c, carry):
        rows = lax.broadcasted_iota(jnp.int32, (N_SLOTS, tm), 0)
        sel = jnp.zeros((N_SLOTS, tm), F32)
        for k in range(TOP_K):
            sel = jnp.where(rows == _slot_ids(idx_ref[k:k + 1, :], lrank_ref[k:k + 1, :], c), 1.0, sel)
        packed = _pack_bf16_pairs(_mm(sel.astype(BF16), h2_ref[...]))
        p = state[0]
        slot = p % 2
        wait_pass(1 - slot, state[1])
        xbuf[slot] = packed
        n = 0
        for e in range(N_EXPERTS):
            live = tcnt_ref[j * N_EXPERTS + e] > c * SLOT_ROWS

            @pl.when(live)
            def _():
                slot_copy(slot, e, tstart_ref[j * N_EXPERTS + e] + c * SLOT_ROWS).start(priority=e % 2)
            n = n + live.astype(jnp.int32)
        state[0] = p + 1
        state[1] = n
        return carry

    lax.fori_loop(0, npass_ref[j], one_pass, 0)

    @pl.when(j == pl.num_programs(0) - 1)
    def _():
        wait_pass((state[0] - 1) % 2, state[1])


def _dispatch(tile_start, tile_count, n_pass, fill_start, n_used, idx_t, lrank_t, h2, n_rows):
    t, d = h2.shape
    tm = ROUTE_TILE
    w = d // 2
    n_blocks = n_rows // FFN_ROWS
    kern = functools.partial(_dispatch_kernel, tm=tm, n_blocks=n_blocks)
    tok = pl.BlockSpec((TOP_K, tm), lambda i, *_: (0, i))
    grid_spec = pltpu.PrefetchScalarGridSpec(
        num_scalar_prefetch=5,
        grid=(t // tm,),
        in_specs=[tok, tok, pl.BlockSpec((tm, d), lambda i, *_: (i, 0))],
        out_specs=pl.BlockSpec(memory_space=pl.ANY),
        scratch_shapes=[pltpu.VMEM((2, N_SLOTS, w), jnp.uint32), pltpu.VMEM((ZERO_ROWS, w), jnp.uint32),
                        pltpu.SemaphoreType.DMA((2,)), pltpu.SemaphoreType.DMA, pltpu.SMEM((2,), jnp.int32)],
    )
    return pl.pallas_call(
        kern,
        out_shape=jax.ShapeDtypeStruct((n_rows, w), jnp.uint32),
        grid_spec=grid_spec,
        compiler_params=_cparams(("arbitrary",), VMEM_LIMIT),
        name="dispatch",
    )(tile_start, tile_count, n_pass, fill_start, n_used, idx_t, lrank_t, h2)


GU_GROUP = 256


def _ffn_kernel(be_ref, nu_ref, x_ref, wgu_ref, bgu_ref, wd_ref, bd_ref, perm_ref, y_ref, wgu_s, wd_s):
    i = pl.program_id(0)
    live = i < nu_ref[0]
    new_expert = jnp.logical_or(i == 0, be_ref[i] != be_ref[jnp.maximum(i - 1, 0)])

    @pl.when(jnp.logical_and(live, new_expert))
    def _():
        for g in range(2 * D_EXPERT // GU_GROUP):
            sl = slice(g * GU_GROUP, (g + 1) * GU_GROUP)
            wgu_s[:, sl] = _mm(wgu_ref[0, :, sl].astype(BF16), perm_ref[...]).astype(BF16)
        wd_s[...] = wd_ref[0].astype(BF16)

    @pl.when(live)
    def _():
        x = _unpack_bf16_pairs(x_ref[...])
        gu = _mm(x, wgu_s[...]) + bgu_ref[0]
        half = GU_GROUP // 2
        n_grp = 2 * D_EXPERT // GU_GROUP
        glu = jnp.concatenate([gu[:, g * GU_GROUP:g * GU_GROUP + half] for g in range(n_grp)], axis=1)
        lin = jnp.concatenate([gu[:, g * GU_GROUP + half:(g + 1) * GU_GROUP] for g in range(n_grp)], axis=1)
        glu = jnp.minimum(glu, SWIGLU_LIMIT)
        lin = jnp.clip(lin, -SWIGLU_LIMIT, SWIGLU_LIMIT)
        act = glu * _sigmoid(SWIGLU_ALPHA * glu) * (lin + 1.0)
        y = _mm(act.astype(BF16), wd_s[...]) + bd_ref[0]
        y_ref[...] = _pack_bf16_pairs(y.astype(BF16).astype(F32))

    @pl.when(jnp.logical_not(live))
    def _():
        y_ref[...] = jnp.zeros_like(y_ref)


def _ffn(block_e, n_used, xs, w_gu, b_gu, w_d, b_d):
    n_rows, w = xs.shape
    bm = FFN_ROWS
    nblk = n_rows // bm
    d = w * 2
    half = GU_GROUP // 2
    perm = np.zeros((GU_GROUP, GU_GROUP), np.float32)
    perm[2 * np.arange(half), np.arange(half)] = 1.0
    perm[2 * np.arange(half) + 1, half + np.arange(half)] = 1.0

    def xmap(i, be, nu):
        return (jnp.minimum(i, nu[0] - 1), 0)

    def wmap(i, be, nu):
        return (be[i], 0, 0)

    grid_spec = pltpu.PrefetchScalarGridSpec(
        num_scalar_prefetch=2,
        grid=(nblk,),
        in_specs=[pl.BlockSpec((bm, w), xmap),
                  pl.BlockSpec((1, d, 2 * D_EXPERT), wmap),
                  pl.BlockSpec((1, 1, 2 * D_EXPERT), wmap),
                  pl.BlockSpec((1, D_EXPERT, d), wmap),
                  pl.BlockSpec((1, 1, d), wmap),
                  pl.BlockSpec((GU_GROUP, GU_GROUP), lambda i, be, nu: (0, 0))],
        out_specs=pl.BlockSpec((bm, w), lambda i, be, nu: (i, 0)),
        scratch_shapes=[pltpu.VMEM((d, 2 * D_EXPERT), BF16), pltpu.VMEM((D_EXPERT, d), BF16)],
    )
    return pl.pallas_call(
        _ffn_kernel,
        out_shape=jax.ShapeDtypeStruct((n_rows, w), jnp.uint32),
        grid_spec=grid_spec,
        compiler_params=_cparams(("arbitrary",), VMEM_LIMIT),
        name="ffn",
    )(block_e, n_used, xs, w_gu, b_gu, w_d, b_d, jnp.asarray(perm, dtype=BF16))


def _combine_kernel(tstart_ref, tcnt_ref, npass_ref, idx_ref, lrank_ref, p_ref, y_ref, x1_ref, mod_ref, g_ref, o_ref,
                    ybuf, sem, *, tm, nt):
    b = pl.program_id(0)
    j = b * nt + pl.program_id(1)
    n_steps = pl.num_programs(0) * nt
    slot = j % 2

    def slot_copy(s, e, row):
        return pltpu.make_async_copy(y_ref.at[pl.ds(pl.multiple_of(row, ROW_ALIGN), SLOT_ROWS)], ybuf.at[s, pl.ds(e * SLOT_ROWS, SLOT_ROWS)],
                                     sem.at[s])

    def fetch(tile, c, s):
        for e in range(N_EXPERTS):
            @pl.when(tcnt_ref[tile * N_EXPERTS + e] > c * SLOT_ROWS)
            def _():
                slot_copy(s, e, tstart_ref[tile * N_EXPERTS + e] + c * SLOT_ROWS).start(priority=e % 2)

    def arrive(tile, c, s):
        live = [tcnt_ref[tile * N_EXPERTS + e] > c * SLOT_ROWS for e in range(N_EXPERTS)]
        n = sum(l.astype(jnp.int32) for l in live)

        @pl.when(n == N_EXPERTS)
        def _():
            pltpu.make_async_copy(y_ref.at[pl.ds(0, N_SLOTS)], ybuf.at[s], sem.at[s]).wait()

        @pl.when(n != N_EXPERTS)
        def _():
            for e in range(N_EXPERTS):
                @pl.when(live[e])
                def _():
                    slot_copy(s, e, 0).wait()

    @pl.when(j == 0)
    def _():
        ybuf[...] = jnp.zeros_like(ybuf)
        fetch(0, 0, 0)

    @pl.when(j + 1 < n_steps)
    def _():
        fetch(j + 1, 0, 1 - slot)

    def weights(c):
        cols = lax.broadcasted_iota(jnp.int32, (tm, N_SLOTS), 1)
        w = jnp.zeros((tm, N_SLOTS), F32)
        for k in range(TOP_K):
            w = jnp.where(cols == _slot_ids(idx_ref[:, k:k + 1], lrank_ref[:, k:k + 1], c), p_ref[:, k:k + 1], w)
        return w.astype(BF16)

    arrive(j, 0, slot)
    acc = _mm(weights(0), _unpack_bf16_pairs(ybuf[slot]))

    def extra_pass(c, acc):
        fetch(j, c, 2)
        arrive(j, c, 2)
        return acc + _mm(weights(c), _unpack_bf16_pairs(ybuf[2]))

    acc = lax.fori_loop(1, npass_ref[j], extra_pass, acc)
    m = mod_ref[pl.ds(b, 1), :]
    xo = x1_ref[0] + m[:, 5 * D_MODEL:6 * D_MODEL] * acc
    ms = jnp.mean(xo * xo, axis=-1, keepdims=True)
    o_ref[0] = xo * lax.rsqrt(ms + EPS) * g_ref[...]


def _combine(tile_start, tile_count, n_pass, idx_c, lrank_c, prob_c, y, x1, mod, g_fin):
    bsz, n, d = x1.shape
    tm = ROUTE_TILE
    nt = n // tm
    kern = functools.partial(_combine_kernel, tm=tm, nt=nt)
    tok = pl.BlockSpec((tm, TOP_K), lambda b, t, *_: (b * nt + t, 0))
    grid_spec = pltpu.PrefetchScalarGridSpec(
        num_scalar_prefetch=3,
        grid=(bsz, nt),
        in_specs=[tok, tok, tok,
                  pl.BlockSpec(memory_space=pl.ANY),
                  pl.BlockSpec((1, tm, d), lambda b, t, *_: (b, t, 0)),
                  pl.BlockSpec(mod.shape, lambda b, t, *_: (0, 0)),
                  pl.BlockSpec((1, d), lambda b, t, *_: (0, 0))],
        out_specs=pl.BlockSpec((1, tm, d), lambda b, t, *_: (b, t, 0)),
        scratch_shapes=[pltpu.VMEM((3, N_SLOTS, d // 2), jnp.uint32), pltpu.SemaphoreType.DMA((3,))],
    )
    return pl.pallas_call(
        kern,
        out_shape=jax.ShapeDtypeStruct((bsz, n, d), F32),
        grid_spec=grid_spec,
        compiler_params=_cparams(("arbitrary", "arbitrary"), VMEM_LIMIT),
        name="combine",
    )(tile_start, tile_count, n_pass, idx_c, lrank_c, prob_c, y, x1, mod, g_fin)


def _prep_w_in(w_in):
    q_a, kv, hq, ff, fb, hi, go, mm, mh = jnp.split(
        w_in, np.cumsum((MLA_Q_RANK, MLA_KV_RANK + MLA_ROPE) + (D_MODEL,) * 6).tolist(), axis=1)
    c_kv, kr = kv[:, :MLA_KV_RANK], kv[:, MLA_KV_RANK:]
    half = MLA_ROPE // 2
    kr_sw = jnp.concatenate([-kr[:, half:], kr[:, :half]], axis=1)
    w16 = jnp.concatenate([hq, hi, go, mm, mh, q_a, c_kv, kr, kr_sw], axis=1)
    w16 = jnp.pad(w16, ((0, 0), (0, N_PROJ16 - w16.shape[1])))
    return w16.astype(BF16), jnp.concatenate([ff, fb], axis=1).astype(BF16)


def _prep_w_q(w_q_b):
    w = w_q_b.reshape(MLA_Q_RANK, HEADS, MLA_QK)
    w = jnp.pad(w, ((0, 0), (0, 0), (0, MLA_QK_PAD - MLA_QK)))
    return w.reshape(MLA_Q_RANK, HEADS * MLA_QK_PAD).astype(BF16)


def _prep_w_kv(w_kv_b):
    w = w_kv_b.reshape(MLA_KV_RANK, HEADS, 2 * HEAD_DIM)
    k = w[:, :, :HEAD_DIM].reshape(MLA_KV_RANK, HEADS * HEAD_DIM)
    v = w[:, :, HEAD_DIM:].reshape(MLA_KV_RANK, HEADS * HEAD_DIM)
    return jnp.concatenate([k, v], axis=1).astype(BF16)


def _rope_tables(n_lat, n_ctx):
    f32 = np.float32
    rows = n_lat // GRID_W
    row = np.repeat(np.arange(rows), GRID_W).astype(f32)
    col = np.tile(np.arange(GRID_W), rows).astype(f32)
    n_freq = MLA_ROPE // 4
    inv = (f32(ROPE_BASE) ** (-np.arange(n_freq, dtype=f32) / f32(n_freq))).astype(f32)
    ang = np.concatenate([row[:, None] * inv, col[:, None] * inv], axis=-1).astype(f32)
    cos, sin = np.cos(ang).astype(f32), np.sin(ang).astype(f32)
    z64 = np.zeros((n_lat, 64), f32)
    cos_q = np.concatenate([cos, cos, z64], axis=1)
    sin_q = np.concatenate([-sin, sin, z64], axis=1)
    cos_k = np.concatenate([cos, cos, z64], axis=1)
    sin_k = np.concatenate([sin, sin, z64], axis=1)
    ctx_cos = np.concatenate([np.ones((n_ctx, 64), f32), np.zeros((n_ctx, 64), f32)], axis=1)
    cos_k = np.concatenate([cos_k, ctx_cos], axis=0)
    sin_k = np.concatenate([sin_k, np.zeros((n_ctx, 128), f32)], axis=0)
    return jnp.asarray(cos_q), jnp.asarray(sin_q), jnp.asarray(cos_k), jnp.asarray(sin_k)


def kernel(x, c, ctx, c_ctx, w_mod, b_mod, norm_mix_g, w_in, mla_q_norm_g, w_q_b, mla_kv_norm_g, w_kv_b,
           hg_lb_logits, hg_norm_g, w_out, norm_ffn_g, w_router, b_router, w_gate_up, b_gate_up, w_down,
           b_down, final_norm_g):
    bsz, n_lat, d = x.shape
    n_ctx = ctx.shape[1]
    assert d == D_MODEL and w_mod.shape[0] == 1
    assert n_lat % HG_CHUNK == 0 and n_ctx % HG_CHUNK == 0 and n_lat % GRID_W == 0
    t_tok = bsz * n_lat

    mod_rows = -(-(bsz + 1) // 8) * 8
    cc = jnp.concatenate([c, c_ctx[None, :], jnp.zeros((mod_rows - bsz - 1, d), F32)], axis=0)
    mod = _modulation(cc, w_mod[0], b_mod[0][None, :])

    xa = jnp.concatenate([x, ctx], axis=1)
    w16, w32 = _prep_w_in(w_in[0])
    p, pf = _inproj(xa, mod, norm_mix_g[0][None, :], w16, w32, n_lat)

    cos_q, sin_q, cos_k, sin_k = _rope_tables(n_lat, n_ctx)
    q = _mla_q(p, mla_q_norm_g[0][None, :], _prep_w_q(w_q_b[0]), cos_q, sin_q, n_lat)
    k, v = _mla_kv(p, mla_kv_norm_g[0][None, :], _prep_w_kv(w_kv_b[0]), cos_k, sin_k)
    y_mla = _attention(q, k, v)

    lb = jax.nn.softmax(hg_lb_logits.astype(F32), axis=1)[:, 0, :]
    o_f, o_b = _hgrn_scans(p, pf, lb.reshape(2, HEADS, 1, HEAD_DIM), n_lat)

    x1, h2, logits = _merge(y_mla, o_f, o_b, p, x, mod, hg_norm_g[0][None, :], w_out[0].astype(BF16),
                            norm_ffn_g[0][None, :], w_router[0], b_router[0][None, :])
    logits_t = logits.T

    assert n_lat % ROUTE_TILE == 0
    idx_t, prob_t, lrank_t, base, cnt = _router(logits_t, ROUTE_TILE)

    n_tiles = t_tok // ROUTE_TILE
    counts = cnt[:, 0].astype(jnp.int32)
    base = base[:, :, 0].astype(jnp.int32)
    tile_count = jnp.concatenate([base[1:] - base[:-1], (counts - base[-1])[None, :]], axis=0)
    seg = (tile_count + ROW_ALIGN - 1) // ROW_ALIGN * ROW_ALIGN
    seg_end = jnp.cumsum(seg, axis=0)
    rows_e = seg_end[-1]
    padded = (rows_e + SLOT_ROWS + FFN_ROWS - 1) // FFN_ROWS * FFN_ROWS
    pad_end = jnp.cumsum(padded)
    pad_start = pad_end - padded
    max_rows = t_tok * TOP_K + N_EXPERTS * (n_tiles * (ROW_ALIGN - 1) + SLOT_ROWS + FFN_ROWS - 1)
    n_rows = -(-max_rows // FFN_ROWS) * FFN_ROWS
    nblk = n_rows // FFN_ROWS
    block_start = jnp.arange(nblk, dtype=jnp.int32) * FFN_ROWS
    block_e = jnp.minimum(jnp.sum(pad_end[None, :] <= block_start[:, None], axis=1), N_EXPERTS - 1).astype(jnp.int32)
    n_used = (pad_end[-1:] // FFN_ROWS).astype(jnp.int32)
    tile_start = (pad_start[None, :] + seg_end - seg).reshape(-1).astype(jnp.int32)
    n_pass = jnp.maximum(-(-jnp.max(tile_count, axis=1) // SLOT_ROWS), 1).astype(jnp.int32)
    tile_count = tile_count.reshape(-1)
    fill_start = jnp.minimum(pad_start + rows_e, n_rows - ZERO_ROWS).astype(jnp.int32)

    xs = _dispatch(tile_start, tile_count, n_pass, fill_start, n_used, idx_t, lrank_t, h2, n_rows)

    n_grp = 2 * D_EXPERT // GU_GROUP
    b_gu = b_gate_up[0].reshape(N_EXPERTS, n_grp, GU_GROUP // 2, 2).transpose(0, 1, 3, 2).reshape(N_EXPERTS, 1, 2 * D_EXPERT)
    y = _ffn(block_e, n_used, xs, w_gate_up[0], b_gu, w_down[0], b_down[0][:, None, :])

    return _combine(tile_start, tile_count, n_pass, idx_t.T, lrank_t.T, prob_t.T, y, x1, mod, final_norm_g[None, :])
```

```python
import functools

import jax
import jax.numpy as jnp
import numpy as np
from jax import lax
from jax.experimental import pallas as pl
from jax.experimental.pallas import tpu as pltpu

F32 = jnp.float32
BF16 = jnp.bfloat16

D_MODEL = 1024
EPS = 1e-6
LOG2_E = 1.4426950408889634
N_MOD = 6
GRID_W = 64
ROPE_BASE = 10000.0

HEADS = 8
HEAD_DIM = 128
MLA_ROPE = 64
MLA_QK = HEAD_DIM + MLA_ROPE
MLA_QK_PAD = 256
MLA_Q_RANK = 768
MLA_KV_RANK = 256

N_EXPERTS = 32
TOP_K = 4
D_EXPERT = 1024
SWIGLU_LIMIT = 7.0
SWIGLU_ALPHA = 1.702

HG_CHUNK = 128
FFN_ROWS = 512

COL_HQ, COL_I, COL_GO, COL_MM, COL_MH = (i * D_MODEL for i in range(5))
COL_QA = 5 * D_MODEL
COL_CKV = COL_QA + MLA_Q_RANK
COL_KR = COL_CKV + MLA_KV_RANK
PROJ16_TN = 1280
N_PROJ16 = -(-(COL_KR + 2 * MLA_ROPE) // PROJ16_TN) * PROJ16_TN
COL_FF, COL_FB = 0, D_MODEL
N_PROJ32 = 2 * D_MODEL
PROJ32_TN = 1024

VMEM_LIMIT = 56 * 1024 * 1024


def _cparams(sem, vmem=None):
    return pltpu.CompilerParams(dimension_semantics=sem, vmem_limit_bytes=vmem)


def _tile(n, pref, mult=8):
    best = None
    for t in range(mult, min(n, pref) + 1, mult):
        if n % t == 0:
            best = t
    assert best is not None, (n, pref, mult)
    return best


def _nt(a, b):
    return lax.dot_general(a, b, (((1,), (1,)), ((), ())), preferred_element_type=F32)


def _mm(a, b):
    return jnp.dot(a, b, preferred_element_type=F32)


def _split2(a):
    hi = a.astype(BF16)
    lo = (a - hi.astype(F32)).astype(BF16)
    return hi, lo


def _sigmoid(x):
    return 1.0 / (1.0 + jnp.exp(-x))


def _gate_sigmoid(x):
    return 0.5 * jnp.tanh(0.5 * x) + 0.5


def _mod_kernel(c_ref, w_ref, b_ref, o_ref):
    c = c_ref[...]
    s = c * _sigmoid(c)
    s_hi, s_lo = _split2(s)
    w_hi, w_lo = _split2(w_ref[...])
    o_ref[...] = _mm(s_hi, w_hi) + _mm(s_lo, w_hi) + _mm(s_hi, w_lo) + b_ref[...]


def _modulation(cc, w_mod, b_mod):
    r, d = cc.shape
    n = w_mod.shape[1]
    tn = _tile(n, 1536, 128)
    return pl.pallas_call(
        _mod_kernel,
        out_shape=jax.ShapeDtypeStruct((r, n), F32),
        grid=(n // tn,),
        in_specs=[pl.BlockSpec((r, d), lambda j: (0, 0)),
                  pl.BlockSpec((d, tn), lambda j: (0, j)),
                  pl.BlockSpec((1, tn), lambda j: (0, j))],
        out_specs=pl.BlockSpec((r, tn), lambda j: (0, j)),
        compiler_params=_cparams(("parallel",), VMEM_LIMIT),
        name="mod",
    )(cc, w_mod, b_mod)


def _inproj_kernel(x_ref, mod_ref, g_ref, w16_ref, w32_ref, o16_ref, o32_ref, hn_ref, *, n_lat, tm, ctx_row, n16):
    b = pl.program_id(0)
    t = pl.program_id(1)
    j = pl.program_id(2)

    @pl.when(j == 0)
    def _():
        x = x_ref[0]
        ms = jnp.mean(x * x, axis=-1, keepdims=True)
        y = x * lax.rsqrt(ms + EPS) * g_ref[...]
        row = t * tm + lax.broadcasted_iota(jnp.int32, (tm, 1), 0)
        is_ctx = row >= n_lat
        m_lat = mod_ref[pl.ds(b, 1), :]
        m_ctx = mod_ref[pl.ds(ctx_row, 1), :]
        shift = jnp.where(is_ctx, m_ctx[:, 0:D_MODEL], m_lat[:, 0:D_MODEL])
        scale = jnp.where(is_ctx, m_ctx[:, D_MODEL:2 * D_MODEL], m_lat[:, D_MODEL:2 * D_MODEL])
        hn_ref[...] = (y * (1.0 + scale) + shift).astype(BF16)

    @pl.when(j < n16)
    def _():
        o16_ref[0] = _mm(hn_ref[...], w16_ref[...]).astype(BF16)

    @pl.when(j >= n16)
    def _():
        o32_ref[0] = _mm(hn_ref[...], w32_ref[...])


def _inproj(xa, mod, g, w16, w32, n_lat):
    bsz, rows, d = xa.shape
    tm = _tile(rows, 1152)
    n16 = N_PROJ16 // PROJ16_TN
    n32 = N_PROJ32 // PROJ32_TN

    def j16(j):
        return jnp.minimum(j, n16 - 1)

    def j32(j):
        return jnp.maximum(j - n16, 0)

    kern = functools.partial(_inproj_kernel, n_lat=n_lat, tm=tm, ctx_row=bsz, n16=n16)
    return pl.pallas_call(
        kern,
        out_shape=(jax.ShapeDtypeStruct((bsz, rows, N_PROJ16), BF16),
                   jax.ShapeDtypeStruct((bsz, rows, N_PROJ32), F32)),
        grid=(bsz, rows // tm, n16 + n32),
        in_specs=[pl.BlockSpec((1, tm, d), lambda b, t, j: (b, t, 0)),
                  pl.BlockSpec(mod.shape, lambda b, t, j: (0, 0)),
                  pl.BlockSpec((1, d), lambda b, t, j: (0, 0)),
                  pl.BlockSpec((d, PROJ16_TN), lambda b, t, j: (0, j16(j))),
                  pl.BlockSpec((d, PROJ32_TN), lambda b, t, j: (0, j32(j)))],
        out_specs=(pl.BlockSpec((1, tm, PROJ16_TN), lambda b, t, j: (b, t, j16(j))),
                   pl.BlockSpec((1, tm, PROJ32_TN), lambda b, t, j: (b, t, j32(j)))),
        scratch_shapes=[pltpu.VMEM((tm, d), BF16)],
        compiler_params=_cparams(("parallel", "parallel", "arbitrary"), VMEM_LIMIT),
        name="inproj",
    )(xa, mod, g, w16, w32)


def _mlaq_kernel(a0_ref, a1_ref, a2_ref, g_ref, w_ref, cos_ref, sin_ref, o_ref, *, tm, n_parts):
    n_rows = tm // n_parts
    scale = MLA_QK ** -0.5 * LOG2_E

    def part(i):
        rows = slice(i * n_rows, (i + 1) * n_rows)
        parts = [a_ref[0, rows, :].astype(F32) for a_ref in (a0_ref, a1_ref, a2_ref)]
        ss = sum(jnp.sum(p * p, axis=-1, keepdims=True) for p in parts)
        r = lax.rsqrt(ss * (1.0 / MLA_Q_RANK) + EPS)
        acc = None
        for j, p in enumerate(parts):
            hj = (p * r * g_ref[:, j * 256:(j + 1) * 256]).astype(BF16)
            d = _mm(hj, w_ref[j * 256:(j + 1) * 256, :])
            acc = d if acc is None else acc + d
        yield
        lane = lax.broadcasted_iota(jnp.int32, (n_rows, HEAD_DIM), 1)
        cos = cos_ref[rows, :]
        sin = sin_ref[rows, :]
        for h in range(HEADS):
            nope = acc[:, h * MLA_QK_PAD:h * MLA_QK_PAD + HEAD_DIM]
            rp = acc[:, h * MLA_QK_PAD + HEAD_DIM:(h + 1) * MLA_QK_PAD]
            swapped = jnp.where(lane < 32, pltpu.roll(rp, 96, 1), pltpu.roll(rp, 32, 1))
            rot = rp * cos + swapped * sin
            o_ref[0, h, rows, 0:HEAD_DIM] = (nope * scale).astype(BF16)
            o_ref[0, h, rows, HEAD_DIM:MLA_QK_PAD] = (rot * scale).astype(BF16)

    _round_robin([part(i) for i in range(n_parts)])


def _mla_q(p, g, w, cos_q, sin_q, n_lat):
    bsz = p.shape[0]
    tm = _tile(n_lat, 512)
    cb = COL_QA // 256
    kern = functools.partial(_mlaq_kernel, tm=tm, n_parts=2 if tm % 32 == 0 else 1)
    return pl.pallas_call(
        kern,
        out_shape=jax.ShapeDtypeStruct((bsz, HEADS, n_lat, MLA_QK_PAD), BF16),
        grid=(bsz, n_lat // tm),
        in_specs=[pl.BlockSpec((1, tm, 256), lambda b, t: (b, t, cb)),
                  pl.BlockSpec((1, tm, 256), lambda b, t: (b, t, cb + 1)),
                  pl.BlockSpec((1, tm, 256), lambda b, t: (b, t, cb + 2)),
                  pl.BlockSpec((1, MLA_Q_RANK), lambda b, t: (0, 0)),
                  pl.BlockSpec(w.shape, lambda b, t: (0, 0)),
                  pl.BlockSpec((tm, HEAD_DIM), lambda b, t: (t, 0)),
                  pl.BlockSpec((tm, HEAD_DIM), lambda b, t: (t, 0))],
        out_specs=pl.BlockSpec((1, HEADS, tm, MLA_QK_PAD), lambda b, t: (b, 0, t, 0)),
        compiler_params=_cparams(("parallel", "parallel"), VMEM_LIMIT),
        name="mla_q",
    )(p, p, p, g, w, cos_q, sin_q)


def _mlakv_kernel(c_ref, kr_ref, g_ref, w_ref, cos_ref, sin_ref, k_ref, v_ref):
    c = c_ref[0].astype(F32)
    ms = jnp.mean(c * c, axis=-1, keepdims=True)
    hn = (c * lax.rsqrt(ms + EPS) * g_ref[...]).astype(BF16)
    kv = _mm(hn, w_ref[...])
    grp = kr_ref[0].astype(F32)
    rot = (grp * cos_ref[...] + pltpu.roll(grp, 64, 1) * sin_ref[...]).astype(BF16)
    for h in range(HEADS):
        k_ref[0, h, :, 0:HEAD_DIM] = kv[:, h * HEAD_DIM:(h + 1) * HEAD_DIM].astype(BF16)
        k_ref[0, h, :, HEAD_DIM:MLA_QK_PAD] = rot
        v_ref[0, h] = kv[:, D_MODEL + h * HEAD_DIM:D_MODEL + (h + 1) * HEAD_DIM].astype(BF16)


def _mla_kv(p, g, w, cos_k, sin_k):
    bsz, rows, _ = p.shape
    tm = _tile(rows, 768)
    return pl.pallas_call(
        _mlakv_kernel,
        out_shape=(jax.ShapeDtypeStruct((bsz, HEADS, rows, MLA_QK_PAD), BF16),
                   jax.ShapeDtypeStruct((bsz, HEADS, rows, HEAD_DIM), BF16)),
        grid=(bsz, rows // tm),
        in_specs=[pl.BlockSpec((1, tm, MLA_KV_RANK), lambda b, t: (b, t, COL_CKV // MLA_KV_RANK)),
                  pl.BlockSpec((1, tm, 128), lambda b, t: (b, t, COL_KR // 128)),
                  pl.BlockSpec((1, MLA_KV_RANK), lambda b, t: (0, 0)),
                  pl.BlockSpec(w.shape, lambda b, t: (0, 0)),
                  pl.BlockSpec((tm, 128), lambda b, t: (t, 0)),
                  pl.BlockSpec((tm, 128), lambda b, t: (t, 0))],
        out_specs=(pl.BlockSpec((1, HEADS, tm, MLA_QK_PAD), lambda b, t: (b, 0, t, 0)),
                   pl.BlockSpec((1, HEADS, tm, HEAD_DIM), lambda b, t: (b, 0, t, 0))),
        compiler_params=_cparams(("parallel", "parallel"), VMEM_LIMIT),
        name="mla_kv",
    )(p, p, g, w, cos_k, sin_k)


def _attn_kernel(q_ref, k_ref, v_ref, o_ref, *, tq, sub):
    k = k_ref[0, 0]
    v = v_ref[0, 0]
    v1 = jnp.concatenate([v, jnp.ones_like(v)], axis=1)

    def sub_tile(r):
        rows = slice(r * sub, (r + 1) * sub)
        s = _nt(q_ref[0, 0, rows, :], k)
        yield
        m = jnp.max(s, axis=-1, keepdims=True)
        p = jnp.exp2(s - m).astype(BF16)
        yield
        ol = _mm(p, v1)
        o_ref[0, rows, :] = (ol[:, 0:HEAD_DIM] * (1.0 / ol[:, HEAD_DIM:HEAD_DIM + 1])).astype(BF16)

    _round_robin([sub_tile(r) for r in range(tq // sub)])


def _attention(q, k, v):
    bsz, _, n, _ = q.shape
    m = k.shape[2]
    tq = _tile(n, 2048)
    sub = _tile(tq, 512)
    kern = functools.partial(_attn_kernel, tq=tq, sub=sub)
    return pl.pallas_call(
        kern,
        out_shape=jax.ShapeDtypeStruct((bsz, n, HEADS * HEAD_DIM), BF16),
        grid=(bsz, HEADS, n // tq),
        in_specs=[pl.BlockSpec((1, 1, tq, MLA_QK_PAD), lambda b, h, t: (b, h, t, 0)),
                  pl.BlockSpec((1, 1, m, MLA_QK_PAD), lambda b, h, t: (b, h, 0, 0)),
                  pl.BlockSpec((1, 1, m, HEAD_DIM), lambda b, h, t: (b, h, 0, 0))],
        out_specs=pl.BlockSpec((1, tq, HEAD_DIM), lambda b, h, t: (b, t, h)),
        compiler_params=_cparams(("parallel", "parallel", "arbitrary"), VMEM_LIMIT),
        name="attention",
    )(q, k, v)


def _level_ref(cum, blk, reverse):
    c = cum.shape[0]
    half = blk // 2
    r = half if reverse else half - 1
    if blk >= 8:
        x = cum.reshape(c // blk, blk, HEAD_DIM)
        e = jnp.broadcast_to(x[:, r:r + 1, :], x.shape)
        return e.reshape(c, HEAD_DIM)
    x = cum.reshape(c // 8, 8, HEAD_DIM)
    sub = lax.broadcasted_iota(jnp.int32, x.shape, 1)
    e = None
    for jb in range(8 // blk):
        cand = jnp.broadcast_to(x[:, jb * blk + r:jb * blk + r + 1, :], x.shape)
        e = cand if e is None else jnp.where(sub >= jb * blk, cand, e)
    return e.reshape(c, HEAD_DIM)


HG_SUB = 32
HG_SUB_LEVEL = HG_SUB.bit_length() - 1
HG_MAX_EXP2 = 100.0


def _hgrn_state(z, v, lb, tri, st, *, chunk, reverse):
    f = lb + (1.0 - lb) * _sigmoid(z)
    kb = (1.0 - f).astype(BF16)
    g = jnp.log(f) * LOG2_E

    g_hi, g_lo = _split2(g)
    yield
    cum = _mm(tri, jnp.concatenate([g_hi, g_lo], axis=0))
    yield
    last = 0 if reverse else chunk - 1
    tot = cum[last:last + 1, :]
    kt = kb * jnp.exp2(tot - cum).astype(BF16)
    vt = v.astype(F32).T.astype(BF16)
    yield
    st_new = st * jnp.exp2(tot) + _mm(vt, kt)
    return kb, cum, st_new


def _sub_block_decay(cum, reverse):
    c = cum.shape[0]
    x = cum.reshape(c // HG_SUB, HG_SUB, HEAD_DIM)
    zero = jnp.zeros((1, 1, HEAD_DIM), F32)
    if reverse:
        edge = jnp.concatenate([x[1:, 0:1, :], zero], axis=0)
    else:
        edge = jnp.concatenate([zero, x[:-1, HG_SUB - 1:HG_SUB, :]], axis=0)
    return (x - edge).reshape(c, HEAD_DIM)


def _hgrn_readout(q, v, kb, cum, sub, lvl, st, *, chunk, reverse, shared):
    q = q.astype(F32)
    qb = (q * _gate_sigmoid(q) * (HEAD_DIM ** -0.5)).astype(BF16)
    yield
    if shared:
        att = jnp.where(lvl == 0, _nt(qb * jnp.exp2(sub).astype(BF16), kb * jnp.exp2(-sub).astype(BF16)).astype(BF16),
                        jnp.zeros((), BF16))
        first = HG_SUB_LEVEL + 1
    else:
        att = jnp.where(lvl == 0, _nt(qb, kb).astype(BF16), jnp.zeros((), BF16))
        first = 1
    for lv in range(first, chunk.bit_length()):
        zrel = cum - _level_ref(cum, 1 << lv, reverse)
        neg_abs = pltpu.bitcast(pltpu.bitcast(zrel, jnp.uint32) | jnp.uint32(0x80000000), F32)
        e = jnp.exp2(neg_abs).astype(BF16)
        yield
        att = jnp.where(lvl == lv, _nt(qb * e, kb * e).astype(BF16), att)
        yield
    return _nt(qb * jnp.exp2(cum).astype(BF16), st.astype(BF16)) + _mm(att, v.astype(BF16))


def _round_robin(gens):
    out = [None] * len(gens)
    active = list(range(len(gens)))
    while active:
        for i in list(active):
            try:
                next(gens[i])
            except StopIteration as e:
                out[i] = e.value
                active.remove(i)
    return out


def _hgrn_kernel(qf_ref, zf_ref, if_ref, qb_ref, zb_ref, ib_ref, lb_ref, tri_ref, lvl_ref, of_ref, ob_ref, st_ref,
                 *, chunk, group, n_ctx_chunks):
    step = pl.program_id(2)

    @pl.when(step == 0)
    def _():
        st_ref[...] = jnp.zeros_like(st_ref)

    ins = ((qf_ref, zf_ref, if_ref), (qb_ref, zb_ref, ib_ref))
    outs = (of_ref, ob_ref)
    chains = [(d, j) for j in range(group) for d in range(2)]

    def cols(j):
        return slice(j * HEAD_DIM, (j + 1) * HEAD_DIM)

    def advance(want_out):
        old = [st_ref[d, j] for d, j in chains]
        parts = _round_robin([_hgrn_state(ins[d][1][0, :, cols(j)], ins[d][2][0, :, cols(j)], lb_ref[d, j],
                                          tri_ref[d], st, chunk=chunk, reverse=bool(d))
                              for (d, j), st in zip(chains, old)])
        for (d, j), (_, _, st_new) in zip(chains, parts):
            st_ref[d, j] = st_new
        if not want_out:
            return
        subs = [_sub_block_decay(cum, bool(d)) for (d, j), (_, cum, _) in zip(chains, parts)]
        low = subs[0]
        for s in subs[1:]:
            low = jnp.minimum(low, s)
        low = jnp.min(jnp.min(low, axis=0, keepdims=True), axis=1, keepdims=True)
        in_range = low[0, 0] >= -HG_MAX_EXP2

        def readouts(shared):
            os_ = _round_robin([_hgrn_readout(ins[d][0][0, :, cols(j)], ins[d][2][0, :, cols(j)], kb, cum, sub,
                                              lvl_ref[d, int(shared)], st, chunk=chunk, reverse=bool(d), shared=shared)
                                for (d, j), (kb, cum, _), sub, st in zip(chains, parts, subs, old)])
            for (d, j), o in zip(chains, os_):
                outs[d][0, j] = o.astype(BF16)

        @pl.when(in_range)
        def _():
            readouts(True)

        @pl.when(jnp.logical_not(in_range))
        def _():
            readouts(False)

    @pl.when(step < n_ctx_chunks)
    def _():
        advance(False)

    @pl.when(step >= n_ctx_chunks)
    def _():
        advance(True)


def _hgrn_consts(chunk, reverse):
    t = np.arange(chunk)[:, None]
    s = np.arange(chunk)[None, :]
    x = t ^ s
    bitlen = np.zeros_like(x)
    for b in range(chunk.bit_length()):
        bitlen = np.where(x >> b > 0, b + 1, bitlen)
    valid = (s > t) if reverse else (t > s)
    lvl = np.where(t == s, 0, np.where(valid, bitlen, -1)).astype(np.float32)
    lvl_shared = np.where((lvl >= 0) & (lvl <= HG_SUB_LEVEL), 0, lvl)
    tri = ((s >= t) if reverse else (t >= s)).astype(np.float32)
    return np.concatenate([tri, tri], axis=1), np.stack([lvl, lvl_shared])


HG_GROUP = 8


def _hgrn_scans(p, pf, lb, n_lat):
    bsz, rows, _ = p.shape
    c = HG_CHUNK
    grp = HG_GROUP
    n_chunks = rows // c
    n_lat_c = n_lat // c
    n_ctx_c = n_chunks - n_lat_c
    consts = [_hgrn_consts(c, False), _hgrn_consts(c, True)]
    tri = jnp.asarray(np.stack([consts[0][0], consts[1][0]]), dtype=BF16)
    lvl = jnp.asarray(np.stack([consts[0][1], consts[1][1]]), dtype=BF16)

    def cidx_f(i):
        return jnp.where(i < n_ctx_c, n_lat_c + i, i - n_ctx_c)

    def cidx_b(i):
        return n_chunks - 1 - i

    def col_spec(col, cidx):
        return pl.BlockSpec((1, c, grp * HEAD_DIM), lambda b, h, i: (b, cidx(i), col // (grp * HEAD_DIM) + h))

    def out_spec(cidx):
        return pl.BlockSpec((1, grp, c, HEAD_DIM), lambda b, h, i: (b, h, cidx(jnp.maximum(i, n_ctx_c)), 0))

    o_shape = jax.ShapeDtypeStruct((bsz, HEADS, n_lat, HEAD_DIM), BF16)
    kern = functools.partial(_hgrn_kernel, chunk=c, group=grp, n_ctx_chunks=n_ctx_c)
    return pl.pallas_call(
        kern,
        out_shape=(o_shape, o_shape),
        grid=(bsz, HEADS // grp, n_chunks),
        in_specs=[col_spec(COL_HQ, cidx_f), col_spec(COL_FF, cidx_f), col_spec(COL_I, cidx_f),
                  col_spec(COL_HQ, cidx_b), col_spec(COL_FB, cidx_b), col_spec(COL_I, cidx_b),
                  pl.BlockSpec((2, grp, 1, HEAD_DIM), lambda b, h, i: (0, h, 0, 0)),
                  pl.BlockSpec((2, c, 2 * c), lambda b, h, i: (0, 0, 0)),
                  pl.BlockSpec((2, 2, c, c), lambda b, h, i: (0, 0, 0, 0))],
        out_specs=(out_spec(cidx_f), out_spec(cidx_b)),
        scratch_shapes=[pltpu.VMEM((2, grp, HEAD_DIM, HEAD_DIM), F32)],
        compiler_params=_cparams(("parallel", "parallel", "arbitrary"), VMEM_LIMIT),
        name="hgrn",
    )(p, pf, p, p, pf, p, lb, tri, lvl)


def _merge_kernel(ym_ref, of_ref, ob_ref, go_ref, gm_ref, gh_ref, x_ref, mod_ref, hgg_ref, wout_ref,
                  gffn_ref, wr2_ref, wrhi_ref, br_ref, x1_ref, h2_ref, lg_ref, y_scr, *, tm, parts):
    b = pl.program_id(0)
    m = mod_ref[pl.ds(b, 1), :]
    n_rows = tm // parts

    def part(i):
        rows = slice(i * n_rows, (i + 1) * n_rows)
        for h in range(HEADS):
            sl = slice(h * HEAD_DIM, (h + 1) * HEAD_DIM)
            o = of_ref[0, h, rows, :].astype(F32) + ob_ref[0, h, rows, :].astype(F32)
            ms = jnp.mean(o * o, axis=-1, keepdims=True)
            g = go_ref[0, rows, sl].astype(F32)
            yh = o * lax.rsqrt(ms + EPS) * hgg_ref[...] * (g * _gate_sigmoid(g))
            y = (_gate_sigmoid(gm_ref[0, rows, sl].astype(F32)) * ym_ref[0, rows, sl].astype(F32)
                 + _gate_sigmoid(gh_ref[0, rows, sl].astype(F32)) * yh)
            y_scr[rows, sl] = y.astype(BF16)
        yield
        mix = _mm(y_scr[rows, :], wout_ref[...])
        yield
        x1 = x_ref[0, rows, :] + m[:, 2 * D_MODEL:3 * D_MODEL] * mix
        x1_ref[0, rows, :] = x1
        ms = jnp.mean(x1 * x1, axis=-1, keepdims=True)
        h2 = ((x1 * lax.rsqrt(ms + EPS) * gffn_ref[...]) * (1.0 + m[:, 4 * D_MODEL:5 * D_MODEL])
              + m[:, 3 * D_MODEL:4 * D_MODEL])
        h_hi = h2.astype(BF16)
        h2_ref[rows, :] = h_hi
        h_lo = (h2 - h_hi.astype(F32)).astype(BF16)
        a = _mm(h_hi, wr2_ref[...])
        lg_ref[rows, :] = (a[:, 0:N_EXPERTS] + a[:, N_EXPERTS:2 * N_EXPERTS] + _mm(h_lo, wrhi_ref[...])
                           + br_ref[...])

    _round_robin([part(i) for i in range(parts)])


def _merge(y_mla, o_f, o_b, p, x, mod, hg_g, w_out, g_ffn, w_r, b_r):
    bsz, n, d = x.shape
    tm = _tile(n, 256, 128)
    nt = n // tm
    w_r_hi, w_r_lo = _split2(w_r)
    w_r2 = jnp.concatenate([w_r_hi, w_r_lo], axis=1)
    kern = functools.partial(_merge_kernel, tm=tm, parts=2)

    def pcol(col):
        return pl.BlockSpec((1, tm, d), lambda b, t: (b, t, col // d))

    tok = lambda b, t: (b, t, 0)
    const2 = lambda b, t: (0, 0)
    return pl.pallas_call(
        kern,
        out_shape=(jax.ShapeDtypeStruct((bsz, n, d), F32),
                   jax.ShapeDtypeStruct((bsz * n, d), BF16),
                   jax.ShapeDtypeStruct((bsz * n, N_EXPERTS), F32)),
        grid=(bsz, nt),
        in_specs=[pl.BlockSpec((1, tm, d), tok),
                  pl.BlockSpec((1, HEADS, tm, HEAD_DIM), lambda b, t: (b, 0, t, 0)),
                  pl.BlockSpec((1, HEADS, tm, HEAD_DIM), lambda b, t: (b, 0, t, 0)),
                  pcol(COL_GO), pcol(COL_MM), pcol(COL_MH),
                  pl.BlockSpec((1, tm, d), tok),
                  pl.BlockSpec(mod.shape, const2),
                  pl.BlockSpec((1, HEAD_DIM), const2),
                  pl.BlockSpec((d, d), const2),
                  pl.BlockSpec((1, d), const2),
                  pl.BlockSpec((d, 2 * N_EXPERTS), const2),
                  pl.BlockSpec((d, N_EXPERTS), const2),
                  pl.BlockSpec((1, N_EXPERTS), const2)],
        out_specs=(pl.BlockSpec((1, tm, d), tok),
                   pl.BlockSpec((tm, d), lambda b, t: (b * nt + t, 0)),
                   pl.BlockSpec((tm, N_EXPERTS), lambda b, t: (b * nt + t, 0))),
        scratch_shapes=[pltpu.VMEM((tm, d), BF16)],
        compiler_params=_cparams(("parallel", "parallel"), VMEM_LIMIT),
        name="merge",
    )(y_mla, o_f, o_b, p, p, p, x, mod, hg_g, w_out, g_ffn, w_r2, w_r_hi, b_r)


def _router_kernel(lg_ref, upper_ref, idx_ref, prob_ref, rank_ref, base_ref, cnt_ref, run_ref, *, tm):
    @pl.when(pl.program_id(0) == 0)
    def _():
        run_ref[...] = jnp.zeros_like(run_ref)

    l = lg_ref[...]
    eidx = lax.broadcasted_iota(jnp.int32, (N_EXPERTS, tm), 0)
    vals, sel = [], []
    for _ in range(TOP_K):
        m = jnp.max(l, axis=0, keepdims=True)
        first = jnp.min(jnp.where(l == m, eidx, N_EXPERTS), axis=0, keepdims=True)
        vals.append(m)
        sel.append(first)
        l = jnp.where(eidx == first, -jnp.inf, l)
    ex = [jnp.exp(v - vals[0]) for v in vals]
    inv = 1.0 / (ex[0] + ex[1] + ex[2] + ex[3])
    onehot = jnp.zeros((N_EXPERTS, tm), F32)
    for k in range(TOP_K):
        onehot = onehot + jnp.where(eidx == sel[k], 1.0, 0.0)
    before = _mm(onehot.astype(BF16), upper_ref[...])
    base_ref[0] = run_ref[...]
    for k in range(TOP_K):
        idx_ref[k:k + 1, :] = sel[k]
        prob_ref[k:k + 1, :] = ex[k] * inv
        rank_ref[k:k + 1, :] = jnp.sum(jnp.where(eidx == sel[k], before, 0.0), axis=0, keepdims=True).astype(jnp.int32)
    run_ref[...] = run_ref[...] + jnp.sum(onehot, axis=1, keepdims=True)
    cnt_ref[...] = run_ref[...]


def _router(logits_t, tm):
    _, t = logits_t.shape
    upper = jnp.asarray(np.triu(np.ones((tm, tm), np.float32), 1), dtype=BF16)
    kern = functools.partial(_router_kernel, tm=tm)
    tok = pl.BlockSpec((TOP_K, tm), lambda i: (0, i))
    return pl.pallas_call(
        kern,
        out_shape=(jax.ShapeDtypeStruct((TOP_K, t), jnp.int32),
                   jax.ShapeDtypeStruct((TOP_K, t), F32),
                   jax.ShapeDtypeStruct((TOP_K, t), jnp.int32),
                   jax.ShapeDtypeStruct((t // tm, N_EXPERTS, 128), F32),
                   jax.ShapeDtypeStruct((N_EXPERTS, 128), F32)),
        grid=(t // tm,),
        in_specs=[pl.BlockSpec((N_EXPERTS, tm), lambda i: (0, i)),
                  pl.BlockSpec((tm, tm), lambda i: (0, 0))],
        out_specs=(tok, tok, tok, pl.BlockSpec((1, N_EXPERTS, 128), lambda i: (i, 0, 0)),
                   pl.BlockSpec((N_EXPERTS, 128), lambda i: (0, 0))),
        scratch_shapes=[pltpu.VMEM((N_EXPERTS, 128), F32)],
        compiler_params=_cparams(("arbitrary",), VMEM_LIMIT),
        name="router",
    )(logits_t, upper)


ROUTE_TILE = 256
ROW_ALIGN = 8
SLOT_ROWS = 48
N_SLOTS = N_EXPERTS * SLOT_ROWS
ZERO_ROWS = FFN_ROWS + SLOT_ROWS


def _slot_ids(idx, lrank, c):
    r = lrank - c * SLOT_ROWS
    return jnp.where(jnp.logical_and(r >= 0, r < SLOT_ROWS), idx * SLOT_ROWS + r, -1)


def _pack_bf16_pairs(x):
    u = pltpu.bitcast(x, jnp.uint32)
    w = x.shape[1] // 2
    return (u[:, 0:w] >> 16) | u[:, w:2 * w]


def _unpack_bf16_pairs(u):
    lo = pltpu.bitcast(u << 16, F32).astype(BF16)
    hi = pltpu.bitcast(u & jnp.uint32(0xFFFF0000), F32).astype(BF16)
    return jnp.concatenate([lo, hi], axis=1)


def _dispatch_kernel(tstart_ref, tcnt_ref, npass_ref, fill_ref, nused_ref, idx_ref, lrank_ref, h2_ref, xs_ref,
                     xbuf, zbuf, sem, zsem, state, *, tm, n_blocks):
    j = pl.program_id(0)

    def slot_copy(slot, e, row):
        return pltpu.make_async_copy(xbuf.at[slot, pl.ds(e * SLOT_ROWS, SLOT_ROWS)],
                                     xs_ref.at[pl.ds(pl.multiple_of(row, ROW_ALIGN), SLOT_ROWS)], sem.at[slot])

    def wait_pass(slot, n):
        @pl.when(n == N_EXPERTS)
        def _():
            pltpu.make_async_copy(xbuf.at[slot], xs_ref.at[pl.ds(0, N_SLOTS)], sem.at[slot]).wait()

        @pl.when(n != N_EXPERTS)
        def _():
            def body(i, carry):
                slot_copy(slot, 0, 0).wait()
                return carry
            lax.fori_loop(0, n, body, 0)

    @pl.when(j == 0)
    def _():
        zbuf[...] = jnp.zeros_like(zbuf)
        state[0] = 0
        state[1] = 0
        state[2] = 0
        state[3] = 0
        for parity in range(2):
            for e in range(parity, N_EXPERTS, 2):
                pltpu.make_async_copy(zbuf, xs_ref.at[pl.ds(pl.multiple_of(fill_ref[e], ROW_ALIGN), ZERO_ROWS)],
                                      zsem).start()
            for e in range(parity, N_EXPERTS, 2):
                pltpu.make_async_copy(zbuf, xs_ref.at[pl.ds(0, ZERO_ROWS)], zsem).wait()

        def tail(i, carry):
            pltpu.make_async_copy(zbuf.at[pl.ds(0, FFN_ROWS)],
                                  xs_ref.at[pl.ds(pl.multiple_of(i * FFN_ROWS, ROW_ALIGN), FFN_ROWS)], zsem).start()
            return carry
        lax.fori_loop(nused_ref[0], n_blocks, tail, 0)

        def tail_wait(i, carry):
            pltpu.make_async_copy(zbuf.at[pl.ds(0, FFN_ROWS)], xs_ref.at[pl.ds(0, FFN_ROWS)], zsem).wait()
            return carry
        lax.fori_loop(nused_ref[0], n_blocks, tail_wait, 0)

    def send(slot, tile, c, experts):
        for e in experts:
            @pl.when(tcnt_ref[tile * N_EXPERTS + e] > c * SLOT_ROWS)
            def _():
                slot_copy(slot, e, tstart_ref[tile * N_EXPERTS + e] + c * SLOT_ROWS).start(priority=e % 2)

    groups = [range(g, g + N_EXPERTS // 4) for g in range(0, N_EXPERTS, N_EXPERTS // 4)]

    def one_pass(c, carry):
        p = state[0]
        slot = p % 2
        prev_tile, prev_c = state[2], state[3]

        def send_prev(experts):
            @pl.when(p > 0)
            def _():
                send(1 - slot, prev_tile, prev_c, experts)

        send_prev(groups[0])
        rows = lax.broadcasted_iota(jnp.int32, (N_SLOTS, tm), 0)
        sel = jnp.zeros((N_SLOTS, tm), F32)
        for k in range(TOP_K):
            sel = jnp.where(rows == _slot_ids(idx_ref[k:k + 1, :], lrank_ref[k:k + 1, :], c), 1.0, sel)
        send_prev(groups[1])
        x = _mm(sel.astype(BF16), h2_ref[...])
        send_prev(groups[2])
        xbuf[slot] = _pack_bf16_pairs(x)
        send_prev(groups[3])
        wait_pass(1 - slot, state[1])
        n = 0
        for e in range(N_EXPERTS):
            n = n + (tcnt_ref[j * N_EXPERTS + e] > c * SLOT_ROWS).astype(jnp.int32)
        state[0] = p + 1
        state[1] = n
        state[2] = j
        state[3] = c
        return carry

    lax.fori_loop(0, npass_ref[j], one_pass, 0)

    @pl.when(j == pl.num_programs(0) - 1)
    def _():
        last = (state[0] - 1) % 2
        send(last, state[2], state[3], range(N_EXPERTS))
        wait_pass(last, state[1])


def _dispatch(tile_start, tile_count, n_pass, fill_start, n_used, idx_t, lrank_t, h2, n_rows):
    t, d = h2.shape
    tm = ROUTE_TILE
    w = d // 2
    n_blocks = n_rows // FFN_ROWS
    kern = functools.partial(_dispatch_kernel, tm=tm, n_blocks=n_blocks)
    tok = pl.BlockSpec((TOP_K, tm), lambda i, *_: (0, i))
    grid_spec = pltpu.PrefetchScalarGridSpec(
        num_scalar_prefetch=5,
        grid=(t // tm,),
        in_specs=[tok, tok, pl.BlockSpec((tm, d), lambda i, *_: (i, 0))],
        out_specs=pl.BlockSpec(memory_space=pl.ANY),
        scratch_shapes=[pltpu.VMEM((2, N_SLOTS, w), jnp.uint32), pltpu.VMEM((ZERO_ROWS, w), jnp.uint32),
                        pltpu.SemaphoreType.DMA((2,)), pltpu.SemaphoreType.DMA, pltpu.SMEM((4,), jnp.int32)],
    )
    return pl.pallas_call(
        kern,
        out_shape=jax.ShapeDtypeStruct((n_rows, w), jnp.uint32),
        grid_spec=grid_spec,
        compiler_params=_cparams(("arbitrary",), VMEM_LIMIT),
        name="dispatch",
    )(tile_start, tile_count, n_pass, fill_start, n_used, idx_t, lrank_t, h2)


GU_GROUP = 256


def _ffn_kernel(be_ref, nu_ref, x_ref, wgu_ref, bgu_ref, wd_ref, bd_ref, perm_ref, y_ref, wgu_s, wd_s):
    i = pl.program_id(0)
    live = i < nu_ref[0]
    new_expert = jnp.logical_or(i == 0, be_ref[i] != be_ref[jnp.maximum(i - 1, 0)])

    @pl.when(jnp.logical_and(live, new_expert))
    def _():
        for g in range(2 * D_EXPERT // GU_GROUP):
            sl = slice(g * GU_GROUP, (g + 1) * GU_GROUP)
            wgu_s[:, sl] = _mm(wgu_ref[0, :, sl].astype(BF16), perm_ref[...]).astype(BF16)
        wd_s[...] = wd_ref[0].astype(BF16)

    @pl.when(live)
    def _():
        x = _unpack_bf16_pairs(x_ref[...])
        gu = _mm(x, wgu_s[...]) + bgu_ref[0]
        half = GU_GROUP // 2
        n_grp = 2 * D_EXPERT // GU_GROUP
        glu = jnp.concatenate([gu[:, g * GU_GROUP:g * GU_GROUP + half] for g in range(n_grp)], axis=1)
        lin = jnp.concatenate([gu[:, g * GU_GROUP + half:(g + 1) * GU_GROUP] for g in range(n_grp)], axis=1)
        glu = jnp.minimum(glu, SWIGLU_LIMIT)
        lin = jnp.clip(lin, -SWIGLU_LIMIT, SWIGLU_LIMIT)
        act = glu * _sigmoid(SWIGLU_ALPHA * glu) * (lin + 1.0)
        y = _mm(act.astype(BF16), wd_s[...]) + bd_ref[0]
        y_ref[...] = _pack_bf16_pairs(y.astype(BF16).astype(F32))

    @pl.when(jnp.logical_not(live))
    def _():
        y_ref[...] = jnp.zeros_like(y_ref)


def _ffn(block_e, n_used, xs, w_gu, b_gu, w_d, b_d):
    n_rows, w = xs.shape
    bm = FFN_ROWS
    nblk = n_rows // bm
    d = w * 2
    half = GU_GROUP // 2
    perm = np.zeros((GU_GROUP, GU_GROUP), np.float32)
    perm[2 * np.arange(half), np.arange(half)] = 1.0
    perm[2 * np.arange(half) + 1, half + np.arange(half)] = 1.0

    def xmap(i, be, nu):
        return (jnp.minimum(i, nu[0] - 1), 0)

    def wmap(i, be, nu):
        return (be[i], 0, 0)

    grid_spec = pltpu.PrefetchScalarGridSpec(
        num_scalar_prefetch=2,
        grid=(nblk,),
        in_specs=[pl.BlockSpec((bm, w), xmap),
                  pl.BlockSpec((1, d, 2 * D_EXPERT), wmap),
                  pl.BlockSpec((1, 1, 2 * D_EXPERT), wmap),
                  pl.BlockSpec((1, D_EXPERT, d), wmap),
                  pl.BlockSpec((1, 1, d), wmap),
                  pl.BlockSpec((GU_GROUP, GU_GROUP), lambda i, be, nu: (0, 0))],
        out_specs=pl.BlockSpec((bm, w), lambda i, be, nu: (i, 0)),
        scratch_shapes=[pltpu.VMEM((d, 2 * D_EXPERT), BF16), pltpu.VMEM((D_EXPERT, d), BF16)],
    )
    return pl.pallas_call(
        _ffn_kernel,
        out_shape=jax.ShapeDtypeStruct((n_rows, w), jnp.uint32),
        grid_spec=grid_spec,
        compiler_params=_cparams(("arbitrary",), VMEM_LIMIT),
        name="ffn",
    )(block_e, n_used, xs, w_gu, b_gu, w_d, b_d, jnp.asarray(perm, dtype=BF16))


def _combine_kernel(tstart_ref, tcnt_ref, npass_ref, idx_ref, lrank_ref, p_ref, y_ref, x1_ref, mod_ref, g_ref, o_ref,
                    ybuf, sem, *, tm, nt):
    b = pl.program_id(0)
    j = b * nt + pl.program_id(1)
    n_steps = pl.num_programs(0) * nt
    slot = j % 2

    def slot_copy(s, e, row):
        return pltpu.make_async_copy(y_ref.at[pl.ds(pl.multiple_of(row, ROW_ALIGN), SLOT_ROWS)], ybuf.at[s, pl.ds(e * SLOT_ROWS, SLOT_ROWS)],
                                     sem.at[s])

    def fetch(tile, c, s, experts=range(N_EXPERTS)):
        for e in experts:
            @pl.when(tcnt_ref[tile * N_EXPERTS + e] > c * SLOT_ROWS)
            def _():
                slot_copy(s, e, tstart_ref[tile * N_EXPERTS + e] + c * SLOT_ROWS).start(priority=e % 2)

    def arrive(tile, c, s):
        live = [tcnt_ref[tile * N_EXPERTS + e] > c * SLOT_ROWS for e in range(N_EXPERTS)]
        n = sum(l.astype(jnp.int32) for l in live)

        @pl.when(n == N_EXPERTS)
        def _():
            pltpu.make_async_copy(y_ref.at[pl.ds(0, N_SLOTS)], ybuf.at[s], sem.at[s]).wait()

        @pl.when(n != N_EXPERTS)
        def _():
            for e in range(N_EXPERTS):
                @pl.when(live[e])
                def _():
                    slot_copy(s, e, 0).wait()

    @pl.when(j == 0)
    def _():
        ybuf[...] = jnp.zeros_like(ybuf)
        fetch(0, 0, 0)

    groups = [range(g, g + N_EXPERTS // 4) for g in range(0, N_EXPERTS, N_EXPERTS // 4)]

    def fetch_next(experts):
        @pl.when(j + 1 < n_steps)
        def _():
            fetch(j + 1, 0, 1 - slot, experts)

    def weights(c):
        cols = lax.broadcasted_iota(jnp.int32, (tm, N_SLOTS), 1)
        w = jnp.zeros((tm, N_SLOTS), F32)
        for k in range(TOP_K):
            w = jnp.where(cols == _slot_ids(idx_ref[:, k:k + 1], lrank_ref[:, k:k + 1], c), p_ref[:, k:k + 1], w)
        return w.astype(BF16)

    fetch_next(groups[0])
    w0 = weights(0)
    fetch_next(groups[1])
    arrive(j, 0, slot)
    rows0 = _unpack_bf16_pairs(ybuf[slot])
    fetch_next(groups[2])
    acc = _mm(w0, rows0)
    fetch_next(groups[3])

    def extra_pass(c, acc):
        fetch(j, c, 2)
        arrive(j, c, 2)
        return acc + _mm(weights(c), _unpack_bf16_pairs(ybuf[2]))

    acc = lax.fori_loop(1, npass_ref[j], extra_pass, acc)
    m = mod_ref[pl.ds(b, 1), :]
    xo = x1_ref[0] + m[:, 5 * D_MODEL:6 * D_MODEL] * acc
    ms = jnp.mean(xo * xo, axis=-1, keepdims=True)
    o_ref[0] = xo * lax.rsqrt(ms + EPS) * g_ref[...]


def _combine(tile_start, tile_count, n_pass, idx_c, lrank_c, prob_c, y, x1, mod, g_fin):
    bsz, n, d = x1.shape
    tm = ROUTE_TILE
    nt = n // tm
    kern = functools.partial(_combine_kernel, tm=tm, nt=nt)
    tok = pl.BlockSpec((tm, TOP_K), lambda b, t, *_: (b * nt + t, 0))
    grid_spec = pltpu.PrefetchScalarGridSpec(
        num_scalar_prefetch=3,
        grid=(bsz, nt),
        in_specs=[tok, tok, tok,
                  pl.BlockSpec(memory_space=pl.ANY),
                  pl.BlockSpec((1, tm, d), lambda b, t, *_: (b, t, 0)),
                  pl.BlockSpec(mod.shape, lambda b, t, *_: (0, 0)),
                  pl.BlockSpec((1, d), lambda b, t, *_: (0, 0))],
        out_specs=pl.BlockSpec((1, tm, d), lambda b, t, *_: (b, t, 0)),
        scratch_shapes=[pltpu.VMEM((3, N_SLOTS, d // 2), jnp.uint32), pltpu.SemaphoreType.DMA((3,))],
    )
    return pl.pallas_call(
        kern,
        out_shape=jax.ShapeDtypeStruct((bsz, n, d), F32),
        grid_spec=grid_spec,
        compiler_params=_cparams(("arbitrary", "arbitrary"), VMEM_LIMIT),
        name="combine",
    )(tile_start, tile_count, n_pass, idx_c, lrank_c, prob_c, y, x1, mod, g_fin)


def _prep_w_in(w_in):
    q_a, kv, hq, ff, fb, hi, go, mm, mh = jnp.split(
        w_in, np.cumsum((MLA_Q_RANK, MLA_KV_RANK + MLA_ROPE) + (D_MODEL,) * 6).tolist(), axis=1)
    c_kv, kr = kv[:, :MLA_KV_RANK], kv[:, MLA_KV_RANK:]
    half = MLA_ROPE // 2
    kr_sw = jnp.concatenate([-kr[:, half:], kr[:, :half]], axis=1)
    w16 = jnp.concatenate([hq, hi, go, mm, mh, q_a, c_kv, kr, kr_sw], axis=1)
    w16 = jnp.pad(w16, ((0, 0), (0, N_PROJ16 - w16.shape[1])))
    return w16.astype(BF16), jnp.concatenate([ff, fb], axis=1).astype(BF16)


def _prep_w_q(w_q_b):
    w = w_q_b.reshape(MLA_Q_RANK, HEADS, MLA_QK)
    w = jnp.pad(w, ((0, 0), (0, 0), (0, MLA_QK_PAD - MLA_QK)))
    return w.reshape(MLA_Q_RANK, HEADS * MLA_QK_PAD).astype(BF16)


def _prep_w_kv(w_kv_b):
    w = w_kv_b.reshape(MLA_KV_RANK, HEADS, 2 * HEAD_DIM)
    k = w[:, :, :HEAD_DIM].reshape(MLA_KV_RANK, HEADS * HEAD_DIM)
    v = w[:, :, HEAD_DIM:].reshape(MLA_KV_RANK, HEADS * HEAD_DIM)
    return jnp.concatenate([k, v], axis=1).astype(BF16)


def _rope_tables(n_lat, n_ctx):
    f32 = np.float32
    rows = n_lat // GRID_W
    row = np.repeat(np.arange(rows), GRID_W).astype(f32)
    col = np.tile(np.arange(GRID_W), rows).astype(f32)
    n_freq = MLA_ROPE // 4
    inv = (f32(ROPE_BASE) ** (-np.arange(n_freq, dtype=f32) / f32(n_freq))).astype(f32)
    ang = np.concatenate([row[:, None] * inv, col[:, None] * inv], axis=-1).astype(f32)
    cos, sin = np.cos(ang).astype(f32), np.sin(ang).astype(f32)
    z64 = np.zeros((n_lat, 64), f32)
    cos_q = np.concatenate([cos, cos, z64], axis=1)
    sin_q = np.concatenate([-sin, sin, z64], axis=1)
    cos_k = np.concatenate([cos, cos, z64], axis=1)
    sin_k = np.concatenate([sin, sin, z64], axis=1)
    ctx_cos = np.concatenate([np.ones((n_ctx, 64), f32), np.zeros((n_ctx, 64), f32)], axis=1)
    cos_k = np.concatenate([cos_k, ctx_cos], axis=0)
    sin_k = np.concatenate([sin_k, np.zeros((n_ctx, 128), f32)], axis=0)
    return jnp.asarray(cos_q), jnp.asarray(sin_q), jnp.asarray(cos_k), jnp.asarray(sin_k)


def kernel(x, c, ctx, c_ctx, w_mod, b_mod, norm_mix_g, w_in, mla_q_norm_g, w_q_b, mla_kv_norm_g, w_kv_b,
           hg_lb_logits, hg_norm_g, w_out, norm_ffn_g, w_router, b_router, w_gate_up, b_gate_up, w_down,
           b_down, final_norm_g):
    bsz, n_lat, d = x.shape
    n_ctx = ctx.shape[1]
    assert d == D_MODEL and w_mod.shape[0] == 1
    assert n_lat % HG_CHUNK == 0 and n_ctx % HG_CHUNK == 0 and n_lat % GRID_W == 0
    t_tok = bsz * n_lat

    mod_rows = -(-(bsz + 1) // 8) * 8
    cc = jnp.concatenate([c, c_ctx[None, :], jnp.zeros((mod_rows - bsz - 1, d), F32)], axis=0)
    mod = _modulation(cc, w_mod[0], b_mod[0][None, :])

    xa = jnp.concatenate([x, ctx], axis=1)
    w16, w32 = _prep_w_in(w_in[0])
    p, pf = _inproj(xa, mod, norm_mix_g[0][None, :], w16, w32, n_lat)

    cos_q, sin_q, cos_k, sin_k = _rope_tables(n_lat, n_ctx)
    q = _mla_q(p, mla_q_norm_g[0][None, :], _prep_w_q(w_q_b[0]), cos_q, sin_q, n_lat)
    k, v = _mla_kv(p, mla_kv_norm_g[0][None, :], _prep_w_kv(w_kv_b[0]), cos_k, sin_k)
    y_mla = _attention(q, k, v)

    lb = jax.nn.softmax(hg_lb_logits.astype(F32), axis=1)[:, 0, :]
    o_f, o_b = _hgrn_scans(p, pf, lb.reshape(2, HEADS, 1, HEAD_DIM), n_lat)

    x1, h2, logits = _merge(y_mla, o_f, o_b, p, x, mod, hg_norm_g[0][None, :], w_out[0].astype(BF16),
                            norm_ffn_g[0][None, :], w_router[0], b_router[0][None, :])
    logits_t = logits.T

    assert n_lat % ROUTE_TILE == 0
    idx_t, prob_t, lrank_t, base, cnt = _router(logits_t, ROUTE_TILE)

    n_tiles = t_tok // ROUTE_TILE
    counts = cnt[:, 0].astype(jnp.int32)
    base = base[:, :, 0].astype(jnp.int32)
    tile_count = jnp.concatenate([base[1:] - base[:-1], (counts - base[-1])[None, :]], axis=0)
    seg = (tile_count + ROW_ALIGN - 1) // ROW_ALIGN * ROW_ALIGN
    seg_end = jnp.cumsum(seg, axis=0)
    rows_e = seg_end[-1]
    padded = (rows_e + SLOT_ROWS + FFN_ROWS - 1) // FFN_ROWS * FFN_ROWS
    pad_end = jnp.cumsum(padded)
    pad_start = pad_end - padded
    max_rows = t_tok * TOP_K + N_EXPERTS * (n_tiles * (ROW_ALIGN - 1) + SLOT_ROWS + FFN_ROWS - 1)
    n_rows = -(-max_rows // FFN_ROWS) * FFN_ROWS
    nblk = n_rows // FFN_ROWS
    block_start = jnp.arange(nblk, dtype=jnp.int32) * FFN_ROWS
    block_e = jnp.minimum(jnp.sum(pad_end[None, :] <= block_start[:, None], axis=1), N_EXPERTS - 1).astype(jnp.int32)
    n_used = (pad_end[-1:] // FFN_ROWS).astype(jnp.int32)
    tile_start = (pad_start[None, :] + seg_end - seg).reshape(-1).astype(jnp.int32)
    n_pass = jnp.maximum(-(-jnp.max(tile_count, axis=1) // SLOT_ROWS), 1).astype(jnp.int32)
    tile_count = tile_count.reshape(-1)
    fill_start = jnp.minimum(pad_start + rows_e, n_rows - ZERO_ROWS).astype(jnp.int32)

    xs = _dispatch(tile_start, tile_count, n_pass, fill_start, n_used, idx_t, lrank_t, h2, n_rows)

    n_grp = 2 * D_EXPERT // GU_GROUP
    b_gu = b_gate_up[0].reshape(N_EXPERTS, n_grp, GU_GROUP // 2, 2).transpose(0, 1, 3, 2).reshape(N_EXPERTS, 1, 2 * D_EXPERT)
    y = _ffn(block_e, n_used, xs, w_gate_up[0], b_gu, w_down[0], b_down[0][:, None, :])

    return _combine(tile_start, tile_count, n_pass, idx_t.T, lrank_t.T, prob_t.T, y, x1, mod, final_norm_g[None, :])
```

```python
import functools

import jax
import jax.numpy as jnp
import numpy as np
from jax import lax
from jax.experimental import pallas as pl
from jax.experimental.pallas import tpu as pltpu

F32 = jnp.float32
BF16 = jnp.bfloat16

D_MODEL = 1024
EPS = 1e-6
LOG2_E = 1.4426950408889634
N_MOD = 6
GRID_W = 64
ROPE_BASE = 10000.0

HEADS = 8
HEAD_DIM = 128
MLA_ROPE = 64
MLA_QK = HEAD_DIM + MLA_ROPE
MLA_QK_PAD = 256
MLA_Q_RANK = 768
MLA_KV_RANK = 256
ROPE_HALF = MLA_ROPE // 2
QA_PIECE = 256

N_EXPERTS = 32
TOP_K = 4
D_EXPERT = 1024
SWIGLU_LIMIT = 7.0
SWIGLU_ALPHA = 1.702

HG_CHUNK = 128
FFN_ROWS = 512

COL_HQ, COL_I, COL_GO, COL_MM, COL_MH = (i * D_MODEL for i in range(5))
COL_QA = 5 * D_MODEL
COL_CKV = COL_QA + MLA_Q_RANK
COL_KR = COL_CKV + MLA_KV_RANK
PROJ16_TN = 1280
N_PROJ16 = -(-(COL_KR + 2 * MLA_ROPE) // PROJ16_TN) * PROJ16_TN
COL_FF, COL_FB = 0, D_MODEL
N_PROJ32 = 2 * D_MODEL
PROJ32_TN = 1024

VMEM_LIMIT = 56 * 1024 * 1024


def _cparams(sem, vmem=None):
    return pltpu.CompilerParams(dimension_semantics=sem, vmem_limit_bytes=vmem)


def _tile(n, pref, mult=8):
    best = None
    for t in range(mult, min(n, pref) + 1, mult):
        if n % t == 0:
            best = t
    assert best is not None, (n, pref, mult)
    return best


def _nt(a, b):
    return lax.dot_general(a, b, (((1,), (1,)), ((), ())), preferred_element_type=F32)


def _mm(a, b):
    return jnp.dot(a, b, preferred_element_type=F32)


def _split2(a):
    hi = a.astype(BF16)
    lo = (a - hi.astype(F32)).astype(BF16)
    return hi, lo


def _sigmoid(x):
    return 1.0 / (1.0 + jnp.exp(-x))


def _gate_sigmoid(x):
    return 0.5 * jnp.tanh(0.5 * x) + 0.5


def _mod_kernel(c_ref, w_ref, b_ref, o_ref):
    c = c_ref[...]
    s = c * _sigmoid(c)
    s_hi, s_lo = _split2(s)
    w_hi, w_lo = _split2(w_ref[...])
    o_ref[...] = _mm(s_hi, w_hi) + _mm(s_lo, w_hi) + _mm(s_hi, w_lo) + b_ref[...]


def _modulation(cc, w_mod, b_mod):
    r, d = cc.shape
    n = w_mod.shape[1]
    tn = _tile(n, 1536, 128)
    return pl.pallas_call(
        _mod_kernel,
        out_shape=jax.ShapeDtypeStruct((r, n), F32),
        grid=(n // tn,),
        in_specs=[pl.BlockSpec((r, d), lambda j: (0, 0)),
                  pl.BlockSpec((d, tn), lambda j: (0, j)),
                  pl.BlockSpec((1, tn), lambda j: (0, j))],
        out_specs=pl.BlockSpec((r, tn), lambda j: (0, j)),
        compiler_params=_cparams(("parallel",), VMEM_LIMIT),
        name="mod",
    )(cc, w_mod, b_mod)


def _inproj_kernel(x_ref, mod_ref, g_ref, w16_ref, w32_ref, o16_ref, o32_ref, hn_ref, *, n_lat, tm, ctx_row, n16):
    b = pl.program_id(0)
    t = pl.program_id(1)
    j = pl.program_id(2)

    @pl.when(j == 0)
    def _():
        x = x_ref[0]
        ms = jnp.mean(x * x, axis=-1, keepdims=True)
        y = x * lax.rsqrt(ms + EPS) * g_ref[...]
        row = t * tm + lax.broadcasted_iota(jnp.int32, (tm, 1), 0)
        is_ctx = row >= n_lat
        m_lat = mod_ref[pl.ds(b, 1), :]
        m_ctx = mod_ref[pl.ds(ctx_row, 1), :]
        shift = jnp.where(is_ctx, m_ctx[:, 0:D_MODEL], m_lat[:, 0:D_MODEL])
        scale = jnp.where(is_ctx, m_ctx[:, D_MODEL:2 * D_MODEL], m_lat[:, D_MODEL:2 * D_MODEL])
        hn_ref[...] = (y * (1.0 + scale) + shift).astype(BF16)

    @pl.when(j < n16)
    def _():
        o16_ref[0] = _mm(hn_ref[...], w16_ref[...]).astype(BF16)

    @pl.when(j >= n16)
    def _():
        o32_ref[0] = _mm(hn_ref[...], w32_ref[...])


def _inproj(xa, mod, g, w16, w32, n_lat):
    bsz, rows, d = xa.shape
    tm = _tile(rows, 1152)
    n16 = N_PROJ16 // PROJ16_TN
    n32 = N_PROJ32 // PROJ32_TN

    def j16(j):
        return jnp.minimum(j, n16 - 1)

    def j32(j):
        return jnp.maximum(j - n16, 0)

    kern = functools.partial(_inproj_kernel, n_lat=n_lat, tm=tm, ctx_row=bsz, n16=n16)
    return pl.pallas_call(
        kern,
        out_shape=(jax.ShapeDtypeStruct((bsz, rows, N_PROJ16), BF16),
                   jax.ShapeDtypeStruct((bsz, rows, N_PROJ32), F32)),
        grid=(bsz, rows // tm, n16 + n32),
        in_specs=[pl.BlockSpec((1, tm, d), lambda b, t, j: (b, t, 0)),
                  pl.BlockSpec(mod.shape, lambda b, t, j: (0, 0)),
                  pl.BlockSpec((1, d), lambda b, t, j: (0, 0)),
                  pl.BlockSpec((d, PROJ16_TN), lambda b, t, j: (0, j16(j))),
                  pl.BlockSpec((d, PROJ32_TN), lambda b, t, j: (0, j32(j)))],
        out_specs=(pl.BlockSpec((1, tm, PROJ16_TN), lambda b, t, j: (b, t, j16(j))),
                   pl.BlockSpec((1, tm, PROJ32_TN), lambda b, t, j: (b, t, j32(j)))),
        scratch_shapes=[pltpu.VMEM((tm, d), BF16)],
        compiler_params=_cparams(("parallel", "parallel", "arbitrary"), VMEM_LIMIT),
        name="inproj",
    )(xa, mod, g, w16, w32)


def _mlaq_kernel(a0_ref, a1_ref, a2_ref, g_ref, w_ref, cos_ref, sin_ref, o_ref, *, tm, n_parts):
    n_rows = tm // n_parts
    scale = MLA_QK ** -0.5 * LOG2_E

    def part(i):
        rows = slice(i * n_rows, (i + 1) * n_rows)
        parts = [a_ref[0, rows, :].astype(F32) for a_ref in (a0_ref, a1_ref, a2_ref)]
        ss = sum(jnp.sum(p * p, axis=-1, keepdims=True) for p in parts)
        r = lax.rsqrt(ss * (1.0 / MLA_Q_RANK) + EPS)
        acc = None
        for j, p in enumerate(parts):
            hj = (p * r * g_ref[:, j * QA_PIECE:(j + 1) * QA_PIECE]).astype(BF16)
            d = _mm(hj, w_ref[j * QA_PIECE:(j + 1) * QA_PIECE, :])
            acc = d if acc is None else acc + d
        yield
        lane = lax.broadcasted_iota(jnp.int32, (n_rows, HEAD_DIM), 1)
        cos = cos_ref[rows, :]
        sin = sin_ref[rows, :]
        for h in range(HEADS):
            nope = acc[:, h * MLA_QK_PAD:h * MLA_QK_PAD + HEAD_DIM]
            rp = acc[:, h * MLA_QK_PAD + HEAD_DIM:(h + 1) * MLA_QK_PAD]
            swapped = jnp.where(lane < ROPE_HALF, pltpu.roll(rp, HEAD_DIM - ROPE_HALF, 1),
                                pltpu.roll(rp, ROPE_HALF, 1))
            rot = rp * cos + swapped * sin
            o_ref[0, h, rows, 0:HEAD_DIM] = (nope * scale).astype(BF16)
            o_ref[0, h, rows, HEAD_DIM:MLA_QK_PAD] = (rot * scale).astype(BF16)

    _round_robin([part(i) for i in range(n_parts)])


def _mla_q(p, g, w, cos_q, sin_q, n_lat):
    bsz = p.shape[0]
    tm = _tile(n_lat, 512)
    cb = COL_QA // QA_PIECE
    kern = functools.partial(_mlaq_kernel, tm=tm, n_parts=2 if tm % 32 == 0 else 1)
    return pl.pallas_call(
        kern,
        out_shape=jax.ShapeDtypeStruct((bsz, HEADS, n_lat, MLA_QK_PAD), BF16),
        grid=(bsz, n_lat // tm),
        in_specs=[pl.BlockSpec((1, tm, QA_PIECE), lambda b, t: (b, t, cb)),
                  pl.BlockSpec((1, tm, QA_PIECE), lambda b, t: (b, t, cb + 1)),
                  pl.BlockSpec((1, tm, QA_PIECE), lambda b, t: (b, t, cb + 2)),
                  pl.BlockSpec((1, MLA_Q_RANK), lambda b, t: (0, 0)),
                  pl.BlockSpec(w.shape, lambda b, t: (0, 0)),
                  pl.BlockSpec((tm, HEAD_DIM), lambda b, t: (t, 0)),
                  pl.BlockSpec((tm, HEAD_DIM), lambda b, t: (t, 0))],
        out_specs=pl.BlockSpec((1, HEADS, tm, MLA_QK_PAD), lambda b, t: (b, 0, t, 0)),
        compiler_params=_cparams(("parallel", "parallel"), VMEM_LIMIT),
        name="mla_q",
    )(p, p, p, g, w, cos_q, sin_q)


def _mlakv_kernel(c_ref, kr_ref, g_ref, w_ref, cos_ref, sin_ref, k_ref, v_ref):
    c = c_ref[0].astype(F32)
    ms = jnp.mean(c * c, axis=-1, keepdims=True)
    hn = (c * lax.rsqrt(ms + EPS) * g_ref[...]).astype(BF16)
    kv = _mm(hn, w_ref[...])
    grp = kr_ref[0].astype(F32)
    rot = (grp * cos_ref[...] + pltpu.roll(grp, MLA_ROPE, 1) * sin_ref[...]).astype(BF16)
    for h in range(HEADS):
        k_ref[0, h, :, 0:HEAD_DIM] = kv[:, h * HEAD_DIM:(h + 1) * HEAD_DIM].astype(BF16)
        k_ref[0, h, :, HEAD_DIM:MLA_QK_PAD] = rot
        v_ref[0, h] = kv[:, D_MODEL + h * HEAD_DIM:D_MODEL + (h + 1) * HEAD_DIM].astype(BF16)


def _mla_kv(p, g, w, cos_k, sin_k):
    bsz, rows, _ = p.shape
    tm = _tile(rows, 768)
    return pl.pallas_call(
        _mlakv_kernel,
        out_shape=(jax.ShapeDtypeStruct((bsz, HEADS, rows, MLA_QK_PAD), BF16),
                   jax.ShapeDtypeStruct((bsz, HEADS, rows, HEAD_DIM), BF16)),
        grid=(bsz, rows // tm),
        in_specs=[pl.BlockSpec((1, tm, MLA_KV_RANK), lambda b, t: (b, t, COL_CKV // MLA_KV_RANK)),
                  pl.BlockSpec((1, tm, 128), lambda b, t: (b, t, COL_KR // 128)),
                  pl.BlockSpec((1, MLA_KV_RANK), lambda b, t: (0, 0)),
                  pl.BlockSpec(w.shape, lambda b, t: (0, 0)),
                  pl.BlockSpec((tm, 128), lambda b, t: (t, 0)),
                  pl.BlockSpec((tm, 128), lambda b, t: (t, 0))],
        out_specs=(pl.BlockSpec((1, HEADS, tm, MLA_QK_PAD), lambda b, t: (b, 0, t, 0)),
                   pl.BlockSpec((1, HEADS, tm, HEAD_DIM), lambda b, t: (b, 0, t, 0))),
        compiler_params=_cparams(("parallel", "parallel"), VMEM_LIMIT),
        name="mla_kv",
    )(p, p, g, w, cos_k, sin_k)


def _attn_kernel(q_ref, k_ref, v_ref, o_ref, *, tq, sub):
    k = k_ref[0, 0]
    v = v_ref[0, 0]
    v1 = jnp.concatenate([v, jnp.ones_like(v)], axis=1)

    def sub_tile(r):
        rows = slice(r * sub, (r + 1) * sub)
        s = _nt(q_ref[0, 0, rows, :], k)
        yield
        m = jnp.max(s, axis=-1, keepdims=True)
        p = jnp.exp2(s - m).astype(BF16)
        yield
        ol = _mm(p, v1)
        o_ref[0, rows, :] = (ol[:, 0:HEAD_DIM] * (1.0 / ol[:, HEAD_DIM:HEAD_DIM + 1])).astype(BF16)

    _round_robin([sub_tile(r) for r in range(tq // sub)])


def _attention(q, k, v):
    bsz, _, n, _ = q.shape
    m = k.shape[2]
    tq = _tile(n, 2048)
    sub = _tile(tq, 512)
    kern = functools.partial(_attn_kernel, tq=tq, sub=sub)
    return pl.pallas_call(
        kern,
        out_shape=jax.ShapeDtypeStruct((bsz, n, HEADS * HEAD_DIM), BF16),
        grid=(bsz, HEADS, n // tq),
        in_specs=[pl.BlockSpec((1, 1, tq, MLA_QK_PAD), lambda b, h, t: (b, h, t, 0)),
                  pl.BlockSpec((1, 1, m, MLA_QK_PAD), lambda b, h, t: (b, h, 0, 0)),
                  pl.BlockSpec((1, 1, m, HEAD_DIM), lambda b, h, t: (b, h, 0, 0))],
        out_specs=pl.BlockSpec((1, tq, HEAD_DIM), lambda b, h, t: (b, t, h)),
        compiler_params=_cparams(("parallel", "parallel", "arbitrary"), VMEM_LIMIT),
        name="attention",
    )(q, k, v)


def _level_ref(cum, blk, reverse):
    c = cum.shape[0]
    half = blk // 2
    r = half if reverse else half - 1
    if blk >= 8:
        x = cum.reshape(c // blk, blk, HEAD_DIM)
        e = jnp.broadcast_to(x[:, r:r + 1, :], x.shape)
        return e.reshape(c, HEAD_DIM)
    x = cum.reshape(c // 8, 8, HEAD_DIM)
    sub = lax.broadcasted_iota(jnp.int32, x.shape, 1)
    e = None
    for jb in range(8 // blk):
        cand = jnp.broadcast_to(x[:, jb * blk + r:jb * blk + r + 1, :], x.shape)
        e = cand if e is None else jnp.where(sub >= jb * blk, cand, e)
    return e.reshape(c, HEAD_DIM)


HG_SUB = 32
HG_SUB_LEVEL = HG_SUB.bit_length() - 1
HG_MAX_EXP2 = 100.0


def _hgrn_state(z, v, lb, tri, st, *, chunk, reverse):
    f = lb + (1.0 - lb) * _sigmoid(z)
    kb = (1.0 - f).astype(BF16)
    g = jnp.log(f) * LOG2_E

    g_hi, g_lo = _split2(g)
    yield
    cum = _mm(tri, jnp.concatenate([g_hi, g_lo], axis=0))
    yield
    last = 0 if reverse else chunk - 1
    tot = cum[last:last + 1, :]
    kt = kb * jnp.exp2(tot - cum).astype(BF16)
    vt = v.astype(F32).T.astype(BF16)
    yield
    st_new = st * jnp.exp2(tot) + _mm(vt, kt)
    return kb, cum, st_new


def _sub_block_decay(cum, reverse):
    c = cum.shape[0]
    x = cum.reshape(c // HG_SUB, HG_SUB, HEAD_DIM)
    zero = jnp.zeros((1, 1, HEAD_DIM), F32)
    if reverse:
        edge = jnp.concatenate([x[1:, 0:1, :], zero], axis=0)
    else:
        edge = jnp.concatenate([zero, x[:-1, HG_SUB - 1:HG_SUB, :]], axis=0)
    return (x - edge).reshape(c, HEAD_DIM)


def _hgrn_readout(q, v, kb, cum, sub, lvl, st, *, chunk, reverse, shared):
    q = q.astype(F32)
    qb = (q * _gate_sigmoid(q) * (HEAD_DIM ** -0.5)).astype(BF16)
    yield
    if shared:
        att = jnp.where(lvl == 0, _nt(qb * jnp.exp2(sub).astype(BF16), kb * jnp.exp2(-sub).astype(BF16)).astype(BF16),
                        jnp.zeros((), BF16))
        first = HG_SUB_LEVEL + 1
    else:
        att = jnp.where(lvl == 0, _nt(qb, kb).astype(BF16), jnp.zeros((), BF16))
        first = 1
    for lv in range(first, chunk.bit_length()):
        zrel = cum - _level_ref(cum, 1 << lv, reverse)
        neg_abs = pltpu.bitcast(pltpu.bitcast(zrel, jnp.uint32) | jnp.uint32(0x80000000), F32)
        e = jnp.exp2(neg_abs).astype(BF16)
        yield
        att = jnp.where(lvl == lv, _nt(qb * e, kb * e).astype(BF16), att)
        yield
    return _nt(qb * jnp.exp2(cum).astype(BF16), st.astype(BF16)) + _mm(att, v.astype(BF16))


def _round_robin(gens):
    out = [None] * len(gens)
    active = list(range(len(gens)))
    while active:
        for i in list(active):
            try:
                next(gens[i])
            except StopIteration as e:
                out[i] = e.value
                active.remove(i)
    return out


def _hgrn_kernel(qf_ref, zf_ref, if_ref, qb_ref, zb_ref, ib_ref, lb_ref, tri_ref, lvl_ref, of_ref, ob_ref, st_ref,
                 *, chunk, group, n_ctx_chunks):
    step = pl.program_id(2)

    @pl.when(step == 0)
    def _():
        st_ref[...] = jnp.zeros_like(st_ref)

    ins = ((qf_ref, zf_ref, if_ref), (qb_ref, zb_ref, ib_ref))
    outs = (of_ref, ob_ref)
    chains = [(d, j) for j in range(group) for d in range(2)]

    def cols(j):
        return slice(j * HEAD_DIM, (j + 1) * HEAD_DIM)

    def advance(want_out):
        old = [st_ref[d, j] for d, j in chains]
        parts = _round_robin([_hgrn_state(ins[d][1][0, :, cols(j)], ins[d][2][0, :, cols(j)], lb_ref[d, j],
                                          tri_ref[d], st, chunk=chunk, reverse=bool(d))
                              for (d, j), st in zip(chains, old)])
        for (d, j), (_, _, st_new) in zip(chains, parts):
            st_ref[d, j] = st_new
        if not want_out:
            return
        subs = [_sub_block_decay(cum, bool(d)) for (d, j), (_, cum, _) in zip(chains, parts)]
        low = subs[0]
        for s in subs[1:]:
            low = jnp.minimum(low, s)
        low = jnp.min(jnp.min(low, axis=0, keepdims=True), axis=1, keepdims=True)
        in_range = low[0, 0] >= -HG_MAX_EXP2

        def readouts(shared):
            os_ = _round_robin([_hgrn_readout(ins[d][0][0, :, cols(j)], ins[d][2][0, :, cols(j)], kb, cum, sub,
                                              lvl_ref[d, int(shared)], st, chunk=chunk, reverse=bool(d), shared=shared)
                                for (d, j), (kb, cum, _), sub, st in zip(chains, parts, subs, old)])
            for (d, j), o in zip(chains, os_):
                outs[d][0, j] = o.astype(BF16)

        @pl.when(in_range)
        def _():
            readouts(True)

        @pl.when(jnp.logical_not(in_range))
        def _():
            readouts(False)

    @pl.when(step < n_ctx_chunks)
    def _():
        advance(False)

    @pl.when(step >= n_ctx_chunks)
    def _():
        advance(True)


def _hgrn_consts(chunk, reverse):
    t = np.arange(chunk)[:, None]
    s = np.arange(chunk)[None, :]
    x = t ^ s
    bitlen = np.zeros_like(x)
    for b in range(chunk.bit_length()):
        bitlen = np.where(x >> b > 0, b + 1, bitlen)
    valid = (s > t) if reverse else (t > s)
    lvl = np.where(t == s, 0, np.where(valid, bitlen, -1)).astype(np.float32)
    lvl_shared = np.where((lvl >= 0) & (lvl <= HG_SUB_LEVEL), 0, lvl)
    tri = ((s >= t) if reverse else (t >= s)).astype(np.float32)
    return np.concatenate([tri, tri], axis=1), np.stack([lvl, lvl_shared])


HG_GROUP = 8


def _hgrn_scans(p, pf, lb, n_lat):
    bsz, rows, _ = p.shape
    c = HG_CHUNK
    grp = HG_GROUP
    n_chunks = rows // c
    n_lat_c = n_lat // c
    n_ctx_c = n_chunks - n_lat_c
    consts = [_hgrn_consts(c, False), _hgrn_consts(c, True)]
    tri = jnp.asarray(np.stack([consts[0][0], consts[1][0]]), dtype=BF16)
    lvl = jnp.asarray(np.stack([consts[0][1], consts[1][1]]), dtype=BF16)

    def cidx_f(i):
        return jnp.where(i < n_ctx_c, n_lat_c + i, i - n_ctx_c)

    def cidx_b(i):
        return n_chunks - 1 - i

    def col_spec(col, cidx):
        return pl.BlockSpec((1, c, grp * HEAD_DIM), lambda b, h, i: (b, cidx(i), col // (grp * HEAD_DIM) + h))

    def out_spec(cidx):
        return pl.BlockSpec((1, grp, c, HEAD_DIM), lambda b, h, i: (b, h, cidx(jnp.maximum(i, n_ctx_c)), 0))

    o_shape = jax.ShapeDtypeStruct((bsz, HEADS, n_lat, HEAD_DIM), BF16)
    kern = functools.partial(_hgrn_kernel, chunk=c, group=grp, n_ctx_chunks=n_ctx_c)
    return pl.pallas_call(
        kern,
        out_shape=(o_shape, o_shape),
        grid=(bsz, HEADS // grp, n_chunks),
        in_specs=[col_spec(COL_HQ, cidx_f), col_spec(COL_FF, cidx_f), col_spec(COL_I, cidx_f),
                  col_spec(COL_HQ, cidx_b), col_spec(COL_FB, cidx_b), col_spec(COL_I, cidx_b),
                  pl.BlockSpec((2, grp, 1, HEAD_DIM), lambda b, h, i: (0, h, 0, 0)),
                  pl.BlockSpec((2, c, 2 * c), lambda b, h, i: (0, 0, 0)),
                  pl.BlockSpec((2, 2, c, c), lambda b, h, i: (0, 0, 0, 0))],
        out_specs=(out_spec(cidx_f), out_spec(cidx_b)),
        scratch_shapes=[pltpu.VMEM((2, grp, HEAD_DIM, HEAD_DIM), F32)],
        compiler_params=_cparams(("parallel", "parallel", "arbitrary"), VMEM_LIMIT),
        name="hgrn",
    )(p, pf, p, p, pf, p, lb, tri, lvl)


def _merge_kernel(ym_ref, of_ref, ob_ref, go_ref, gm_ref, gh_ref, x_ref, mod_ref, hgg_ref, wout_ref,
                  gffn_ref, wr2_ref, wrhi_ref, br_ref, x1_ref, h2_ref, lg_ref, y_scr, *, tm, parts):
    b = pl.program_id(0)
    m = mod_ref[pl.ds(b, 1), :]
    n_rows = tm // parts

    def part(i):
        rows = slice(i * n_rows, (i + 1) * n_rows)
        for h in range(HEADS):
            sl = slice(h * HEAD_DIM, (h + 1) * HEAD_DIM)
            o = of_ref[0, h, rows, :].astype(F32) + ob_ref[0, h, rows, :].astype(F32)
            ms = jnp.mean(o * o, axis=-1, keepdims=True)
            g = go_ref[0, rows, sl].astype(F32)
            yh = o * lax.rsqrt(ms + EPS) * hgg_ref[...] * (g * _gate_sigmoid(g))
            y = (_gate_sigmoid(gm_ref[0, rows, sl].astype(F32)) * ym_ref[0, rows, sl].astype(F32)
                 + _gate_sigmoid(gh_ref[0, rows, sl].astype(F32)) * yh)
            y_scr[rows, sl] = y.astype(BF16)
        yield
        mix = _mm(y_scr[rows, :], wout_ref[...])
        yield
        x1 = x_ref[0, rows, :] + m[:, 2 * D_MODEL:3 * D_MODEL] * mix
        x1_ref[0, rows, :] = x1
        ms = jnp.mean(x1 * x1, axis=-1, keepdims=True)
        h2 = ((x1 * lax.rsqrt(ms + EPS) * gffn_ref[...]) * (1.0 + m[:, 4 * D_MODEL:5 * D_MODEL])
              + m[:, 3 * D_MODEL:4 * D_MODEL])
        h_hi = h2.astype(BF16)
        h2_ref[rows, :] = h_hi
        h_lo = (h2 - h_hi.astype(F32)).astype(BF16)
        a = _mm(h_hi, wr2_ref[...])
        lg_ref[rows, :] = (a[:, 0:N_EXPERTS] + a[:, N_EXPERTS:2 * N_EXPERTS] + _mm(h_lo, wrhi_ref[...])
                           + br_ref[...])

    _round_robin([part(i) for i in range(parts)])


def _merge(y_mla, o_f, o_b, p, x, mod, hg_g, w_out, g_ffn, w_r, b_r):
    bsz, n, d = x.shape
    tm = _tile(n, 256, 128)
    nt = n // tm
    w_r_hi, w_r_lo = _split2(w_r)
    w_r2 = jnp.concatenate([w_r_hi, w_r_lo], axis=1)
    kern = functools.partial(_merge_kernel, tm=tm, parts=2)

    def pcol(col):
        return pl.BlockSpec((1, tm, d), lambda b, t: (b, t, col // d))

    tok = lambda b, t: (b, t, 0)
    const2 = lambda b, t: (0, 0)
    return pl.pallas_call(
        kern,
        out_shape=(jax.ShapeDtypeStruct((bsz, n, d), F32),
                   jax.ShapeDtypeStruct((bsz * n, d), BF16),
                   jax.ShapeDtypeStruct((bsz * n, N_EXPERTS), F32)),
        grid=(bsz, nt),
        in_specs=[pl.BlockSpec((1, tm, d), tok),
                  pl.BlockSpec((1, HEADS, tm, HEAD_DIM), lambda b, t: (b, 0, t, 0)),
                  pl.BlockSpec((1, HEADS, tm, HEAD_DIM), lambda b, t: (b, 0, t, 0)),
                  pcol(COL_GO), pcol(COL_MM), pcol(COL_MH),
                  pl.BlockSpec((1, tm, d), tok),
                  pl.BlockSpec(mod.shape, const2),
                  pl.BlockSpec((1, HEAD_DIM), const2),
                  pl.BlockSpec((d, d), const2),
                  pl.BlockSpec((1, d), const2),
                  pl.BlockSpec((d, 2 * N_EXPERTS), const2),
                  pl.BlockSpec((d, N_EXPERTS), const2),
                  pl.BlockSpec((1, N_EXPERTS), const2)],
        out_specs=(pl.BlockSpec((1, tm, d), tok),
                   pl.BlockSpec((tm, d), lambda b, t: (b * nt + t, 0)),
                   pl.BlockSpec((tm, N_EXPERTS), lambda b, t: (b * nt + t, 0))),
        scratch_shapes=[pltpu.VMEM((tm, d), BF16)],
        compiler_params=_cparams(("parallel", "parallel"), VMEM_LIMIT),
        name="merge",
    )(y_mla, o_f, o_b, p, p, p, x, mod, hg_g, w_out, g_ffn, w_r2, w_r_hi, b_r)


def _router_kernel(lg_ref, upper_ref, idx_ref, prob_ref, rank_ref, base_ref, cnt_ref, run_ref, *, tm):
    @pl.when(pl.program_id(0) == 0)
    def _():
        run_ref[...] = jnp.zeros_like(run_ref)

    l = lg_ref[...]
    eidx = lax.broadcasted_iota(jnp.int32, (N_EXPERTS, tm), 0)
    vals, sel = [], []
    for _ in range(TOP_K):
        m = jnp.max(l, axis=0, keepdims=True)
        first = jnp.min(jnp.where(l == m, eidx, N_EXPERTS), axis=0, keepdims=True)
        vals.append(m)
        sel.append(first)
        l = jnp.where(eidx == first, -jnp.inf, l)
    ex = [jnp.exp(v - vals[0]) for v in vals]
    inv = 1.0 / (ex[0] + ex[1] + ex[2] + ex[3])
    onehot = jnp.zeros((N_EXPERTS, tm), F32)
    for k in range(TOP_K):
        onehot = onehot + jnp.where(eidx == sel[k], 1.0, 0.0)
    before = _mm(onehot.astype(BF16), upper_ref[...])
    base_ref[0] = run_ref[...]
    for k in range(TOP_K):
        idx_ref[k:k + 1, :] = sel[k]
        prob_ref[k:k + 1, :] = ex[k] * inv
        rank_ref[k:k + 1, :] = jnp.sum(jnp.where(eidx == sel[k], before, 0.0), axis=0, keepdims=True).astype(jnp.int32)
    run_ref[...] = run_ref[...] + jnp.sum(onehot, axis=1, keepdims=True)
    cnt_ref[...] = run_ref[...]


def _router(logits_t, tm):
    _, t = logits_t.shape
    upper = jnp.asarray(np.triu(np.ones((tm, tm), np.float32), 1), dtype=BF16)
    kern = functools.partial(_router_kernel, tm=tm)
    tok = pl.BlockSpec((TOP_K, tm), lambda i: (0, i))
    return pl.pallas_call(
        kern,
        out_shape=(jax.ShapeDtypeStruct((TOP_K, t), jnp.int32),
                   jax.ShapeDtypeStruct((TOP_K, t), F32),
                   jax.ShapeDtypeStruct((TOP_K, t), jnp.int32),
                   jax.ShapeDtypeStruct((t // tm, N_EXPERTS, 128), F32),
                   jax.ShapeDtypeStruct((N_EXPERTS, 128), F32)),
        grid=(t // tm,),
        in_specs=[pl.BlockSpec((N_EXPERTS, tm), lambda i: (0, i)),
                  pl.BlockSpec((tm, tm), lambda i: (0, 0))],
        out_specs=(tok, tok, tok, pl.BlockSpec((1, N_EXPERTS, 128), lambda i: (i, 0, 0)),
                   pl.BlockSpec((N_EXPERTS, 128), lambda i: (0, 0))),
        scratch_shapes=[pltpu.VMEM((N_EXPERTS, 128), F32)],
        compiler_params=_cparams(("arbitrary",), VMEM_LIMIT),
        name="router",
    )(logits_t, upper)


ROUTE_TILE = 256
ROW_ALIGN = 8
SLOT_ROWS = 48
N_SLOTS = N_EXPERTS * SLOT_ROWS
ZERO_ROWS = FFN_ROWS + SLOT_ROWS


def _slot_ids(idx, lrank, c):
    r = lrank - c * SLOT_ROWS
    return jnp.where(jnp.logical_and(r >= 0, r < SLOT_ROWS), idx * SLOT_ROWS + r, -1)


def _pack_bf16_pairs(x):
    u = pltpu.bitcast(x, jnp.uint32)
    w = x.shape[1] // 2
    return (u[:, 0:w] >> 16) | u[:, w:2 * w]


def _unpack_bf16_pairs(u):
    lo = pltpu.bitcast(u << 16, F32).astype(BF16)
    hi = pltpu.bitcast(u & jnp.uint32(0xFFFF0000), F32).astype(BF16)
    return jnp.concatenate([lo, hi], axis=1)


def _dispatch_kernel(tstart_ref, tcnt_ref, npass_ref, fill_ref, nused_ref, idx_ref, lrank_ref, h2_ref, xs_ref,
                     xbuf, zbuf, sem, zsem, state, *, tm, n_blocks):
    j = pl.program_id(0)

    def slot_copy(slot, e, row):
        return pltpu.make_async_copy(xbuf.at[slot, pl.ds(e * SLOT_ROWS, SLOT_ROWS)],
                                     xs_ref.at[pl.ds(pl.multiple_of(row, ROW_ALIGN), SLOT_ROWS)], sem.at[slot])

    def wait_pass(slot, n):
        @pl.when(n == N_EXPERTS)
        def _():
            pltpu.make_async_copy(xbuf.at[slot], xs_ref.at[pl.ds(0, N_SLOTS)], sem.at[slot]).wait()

        @pl.when(n != N_EXPERTS)
        def _():
            def body(i, carry):
                slot_copy(slot, 0, 0).wait()
                return carry
            lax.fori_loop(0, n, body, 0)

    @pl.when(j == 0)
    def _():
        zbuf[...] = jnp.zeros_like(zbuf)
        state[0] = 0
        state[1] = 0
        for parity in range(2):
            for e in range(parity, N_EXPERTS, 2):
                pltpu.make_async_copy(zbuf, xs_ref.at[pl.ds(pl.multiple_of(fill_ref[e], ROW_ALIGN), ZERO_ROWS)],
                                      zsem).start()
            for e in range(parity, N_EXPERTS, 2):
                pltpu.make_async_copy(zbuf, xs_ref.at[pl.ds(0, ZERO_ROWS)], zsem).wait()

        def tail(i, carry):
            pltpu.make_async_copy(zbuf.at[pl.ds(0, FFN_ROWS)],
                                  xs_ref.at[pl.ds(pl.multiple_of(i * FFN_ROWS, ROW_ALIGN), FFN_ROWS)], zsem).start()
            return carry
        lax.fori_loop(nused_ref[0], n_blocks, tail, 0)

        def tail_wait(i, carry):
            pltpu.make_async_copy(zbuf.at[pl.ds(0, FFN_ROWS)], xs_ref.at[pl.ds(0, FFN_ROWS)], zsem).wait()
            return carry
        lax.fori_loop(nused_ref[0], n_blocks, tail_wait, 0)

    def one_pass(c, carry):
        rows = lax.broadcasted_iota(jnp.int32, (N_SLOTS, tm), 0)
        sel = jnp.zeros((N_SLOTS, tm), F32)
        for k in range(TOP_K):
            sel = jnp.where(rows == _slot_ids(idx_ref[k:k + 1, :], lrank_ref[k:k + 1, :], c), 1.0, sel)
        packed = _pack_bf16_pairs(_mm(sel.astype(BF16), h2_ref[...]))
        p = state[0]
        slot = p % 2
        wait_pass(1 - slot, state[1])
        xbuf[slot] = packed
        n = 0
        for e in range(N_EXPERTS):
            live = tcnt_ref[j * N_EXPERTS + e] > c * SLOT_ROWS

            @pl.when(live)
            def _():
                slot_copy(slot, e, tstart_ref[j * N_EXPERTS + e] + c * SLOT_ROWS).start(priority=e % 2)
            n = n + live.astype(jnp.int32)
        state[0] = p + 1
        state[1] = n
        return carry

    lax.fori_loop(0, npass_ref[j], one_pass, 0)

    @pl.when(j == pl.num_programs(0) - 1)
    def _():
        wait_pass((state[0] - 1) % 2, state[1])


def _dispatch(tile_start, tile_count, n_pass, fill_start, n_used, idx_t, lrank_t, h2, n_rows):
    t, d = h2.shape
    tm = ROUTE_TILE
    w = d // 2
    n_blocks = n_rows // FFN_ROWS
    kern = functools.partial(_dispatch_kernel, tm=tm, n_blocks=n_blocks)
    tok = pl.BlockSpec((TOP_K, tm), lambda i, *_: (0, i))
    grid_spec = pltpu.PrefetchScalarGridSpec(
        num_scalar_prefetch=5,
        grid=(t // tm,),
        in_specs=[tok, tok, pl.BlockSpec((tm, d), lambda i, *_: (i, 0))],
        out_specs=pl.BlockSpec(memory_space=pl.ANY),
        scratch_shapes=[pltpu.VMEM((2, N_SLOTS, w), jnp.uint32), pltpu.VMEM((ZERO_ROWS, w), jnp.uint32),
                        pltpu.SemaphoreType.DMA((2,)), pltpu.SemaphoreType.DMA, pltpu.SMEM((2,), jnp.int32)],
    )
    return pl.pallas_call(
        kern,
        out_shape=jax.ShapeDtypeStruct((n_rows, w), jnp.uint32),
        grid_spec=grid_spec,
        compiler_params=_cparams(("arbitrary",), VMEM_LIMIT),
        name="dispatch",
    )(tile_start, tile_count, n_pass, fill_start, n_used, idx_t, lrank_t, h2)


GU_GROUP = 256


def _ffn_kernel(be_ref, nu_ref, x_ref, wgu_ref, bgu_ref, wd_ref, bd_ref, perm_ref, y_ref, wgu_s, wd_s):
    i = pl.program_id(0)
    live = i < nu_ref[0]
    new_expert = jnp.logical_or(i == 0, be_ref[i] != be_ref[jnp.maximum(i - 1, 0)])

    @pl.when(jnp.logical_and(live, new_expert))
    def _():
        for g in range(2 * D_EXPERT // GU_GROUP):
            sl = slice(g * GU_GROUP, (g + 1) * GU_GROUP)
            wgu_s[:, sl] = _mm(wgu_ref[0, :, sl].astype(BF16), perm_ref[...]).astype(BF16)
        wd_s[...] = wd_ref[0].astype(BF16)

    @pl.when(live)
    def _():
        x = _unpack_bf16_pairs(x_ref[...])
        gu = _mm(x, wgu_s[...]) + bgu_ref[0]
        half = GU_GROUP // 2
        n_grp = 2 * D_EXPERT // GU_GROUP
        glu = jnp.concatenate([gu[:, g * GU_GROUP:g * GU_GROUP + half] for g in range(n_grp)], axis=1)
        lin = jnp.concatenate([gu[:, g * GU_GROUP + half:(g + 1) * GU_GROUP] for g in range(n_grp)], axis=1)
        glu = jnp.minimum(glu, SWIGLU_LIMIT)
        lin = jnp.clip(lin, -SWIGLU_LIMIT, SWIGLU_LIMIT)
        act = glu * _sigmoid(SWIGLU_ALPHA * glu) * (lin + 1.0)
        y = _mm(act.astype(BF16), wd_s[...]) + bd_ref[0]
        y_ref[...] = _pack_bf16_pairs(y.astype(BF16).astype(F32))

    @pl.when(jnp.logical_not(live))
    def _():
        y_ref[...] = jnp.zeros_like(y_ref)


def _ffn(block_e, n_used, xs, w_gu, b_gu, w_d, b_d):
    n_rows, w = xs.shape
    bm = FFN_ROWS
    nblk = n_rows // bm
    d = w * 2
    half = GU_GROUP // 2
    perm = np.zeros((GU_GROUP, GU_GROUP), np.float32)
    perm[2 * np.arange(half), np.arange(half)] = 1.0
    perm[2 * np.arange(half) + 1, half + np.arange(half)] = 1.0

    def xmap(i, be, nu):
        return (jnp.minimum(i, nu[0] - 1), 0)

    def wmap(i, be, nu):
        return (be[i], 0, 0)

    grid_spec = pltpu.PrefetchScalarGridSpec(
        num_scalar_prefetch=2,
        grid=(nblk,),
        in_specs=[pl.BlockSpec((bm, w), xmap),
                  pl.BlockSpec((1, d, 2 * D_EXPERT), wmap),
                  pl.BlockSpec((1, 1, 2 * D_EXPERT), wmap),
                  pl.BlockSpec((1, D_EXPERT, d), wmap),
                  pl.BlockSpec((1, 1, d), wmap),
                  pl.BlockSpec((GU_GROUP, GU_GROUP), lambda i, be, nu: (0, 0))],
        out_specs=pl.BlockSpec((bm, w), lambda i, be, nu: (i, 0)),
        scratch_shapes=[pltpu.VMEM((d, 2 * D_EXPERT), BF16), pltpu.VMEM((D_EXPERT, d), BF16)],
    )
    return pl.pallas_call(
        _ffn_kernel,
        out_shape=jax.ShapeDtypeStruct((n_rows, w), jnp.uint32),
        grid_spec=grid_spec,
        compiler_params=_cparams(("arbitrary",), VMEM_LIMIT),
        name="ffn",
    )(block_e, n_used, xs, w_gu, b_gu, w_d, b_d, jnp.asarray(perm, dtype=BF16))


def _combine_kernel(tstart_ref, tcnt_ref, npass_ref, idx_ref, lrank_ref, p_ref, y_ref, x1_ref, mod_ref, g_ref, o_ref,
                    ybuf, sem, *, tm, nt):
    b = pl.program_id(0)
    j = b * nt + pl.program_id(1)
    n_steps = pl.num_programs(0) * nt
    slot = j % 2

    def slot_copy(s, e, row):
        return pltpu.make_async_copy(y_ref.at[pl.ds(pl.multiple_of(row, ROW_ALIGN), SLOT_ROWS)], ybuf.at[s, pl.ds(e * SLOT_ROWS, SLOT_ROWS)],
                                     sem.at[s])

    def fetch(tile, c, s):
        for e in range(N_EXPERTS):
            @pl.when(tcnt_ref[tile * N_EXPERTS + e] > c * SLOT_ROWS)
            def _():
                slot_copy(s, e, tstart_ref[tile * N_EXPERTS + e] + c * SLOT_ROWS).start(priority=e % 2)

    def arrive(tile, c, s):
        live = [tcnt_ref[tile * N_EXPERTS + e] > c * SLOT_ROWS for e in range(N_EXPERTS)]
        n = sum(l.astype(jnp.int32) for l in live)

        @pl.when(n == N_EXPERTS)
        def _():
            pltpu.make_async_copy(y_ref.at[pl.ds(0, N_SLOTS)], ybuf.at[s], sem.at[s]).wait()

        @pl.when(n != N_EXPERTS)
        def _():
            for e in range(N_EXPERTS):
                @pl.when(live[e])
                def _():
                    slot_copy(s, e, 0).wait()

    @pl.when(j == 0)
    def _():
        ybuf[...] = jnp.zeros_like(ybuf)
        fetch(0, 0, 0)

    @pl.when(j + 1 < n_steps)
    def _():
        fetch(j + 1, 0, 1 - slot)

    def weights(c):
        cols = lax.broadcasted_iota(jnp.int32, (tm, N_SLOTS), 1)
        w = jnp.zeros((tm, N_SLOTS), F32)
        for k in range(TOP_K):
            w = jnp.where(cols == _slot_ids(idx_ref[:, k:k + 1], lrank_ref[:, k:k + 1], c), p_ref[:, k:k + 1], w)
        return w.astype(BF16)

    arrive(j, 0, slot)
    acc = _mm(weights(0), _unpack_bf16_pairs(ybuf[slot]))

    def extra_pass(c, acc):
        fetch(j, c, 2)
        arrive(j, c, 2)
        return acc + _mm(weights(c), _unpack_bf16_pairs(ybuf[2]))

    acc = lax.fori_loop(1, npass_ref[j], extra_pass, acc)
    m = mod_ref[pl.ds(b, 1), :]
    xo = x1_ref[0] + m[:, 5 * D_MODEL:6 * D_MODEL] * acc
    ms = jnp.mean(xo * xo, axis=-1, keepdims=True)
    o_ref[0] = xo * lax.rsqrt(ms + EPS) * g_ref[...]


def _combine(tile_start, tile_count, n_pass, idx_c, lrank_c, prob_c, y, x1, mod, g_fin):
    bsz, n, d = x1.shape
    tm = ROUTE_TILE
    nt = n // tm
    kern = functools.partial(_combine_kernel, tm=tm, nt=nt)
    tok = pl.BlockSpec((tm, TOP_K), lambda b, t, *_: (b * nt + t, 0))
    grid_spec = pltpu.PrefetchScalarGridSpec(
        num_scalar_prefetch=3,
        grid=(bsz, nt),
        in_specs=[tok, tok, tok,
                  pl.BlockSpec(memory_space=pl.ANY),
                  pl.BlockSpec((1, tm, d), lambda b, t, *_: (b, t, 0)),
                  pl.BlockSpec(mod.shape, lambda b, t, *_: (0, 0)),
                  pl.BlockSpec((1, d), lambda b, t, *_: (0, 0))],
        out_specs=pl.BlockSpec((1, tm, d), lambda b, t, *_: (b, t, 0)),
        scratch_shapes=[pltpu.VMEM((3, N_SLOTS, d // 2), jnp.uint32), pltpu.SemaphoreType.DMA((3,))],
    )
    return pl.pallas_call(
        kern,
        out_shape=jax.ShapeDtypeStruct((bsz, n, d), F32),
        grid_spec=grid_spec,
        compiler_params=_cparams(("arbitrary", "arbitrary"), VMEM_LIMIT),
        name="combine",
    )(tile_start, tile_count, n_pass, idx_c, lrank_c, prob_c, y, x1, mod, g_fin)


def _prep_w_in(w_in):
    q_a, kv, hq, ff, fb, hi, go, mm, mh = jnp.split(
        w_in, np.cumsum((MLA_Q_RANK, MLA_KV_RANK + MLA_ROPE) + (D_MODEL,) * 6).tolist(), axis=1)
    c_kv, kr = kv[:, :MLA_KV_RANK], kv[:, MLA_KV_RANK:]
    half = MLA_ROPE // 2
    kr_sw = jnp.concatenate([-kr[:, half:], kr[:, :half]], axis=1)
    w16 = jnp.concatenate([hq, hi, go, mm, mh, q_a, c_kv, kr, kr_sw], axis=1)
    w16 = jnp.pad(w16, ((0, 0), (0, N_PROJ16 - w16.shape[1])))
    return w16.astype(BF16), jnp.concatenate([ff, fb], axis=1).astype(BF16)


def _prep_w_q(w_q_b):
    w = w_q_b.reshape(MLA_Q_RANK, HEADS, MLA_QK)
    w = jnp.pad(w, ((0, 0), (0, 0), (0, MLA_QK_PAD - MLA_QK)))
    return w.reshape(MLA_Q_RANK, HEADS * MLA_QK_PAD).astype(BF16)


def _prep_w_kv(w_kv_b):
    w = w_kv_b.reshape(MLA_KV_RANK, HEADS, 2 * HEAD_DIM)
    k = w[:, :, :HEAD_DIM].reshape(MLA_KV_RANK, HEADS * HEAD_DIM)
    v = w[:, :, HEAD_DIM:].reshape(MLA_KV_RANK, HEADS * HEAD_DIM)
    return jnp.concatenate([k, v], axis=1).astype(BF16)


def _rope_tables(n_lat, n_ctx):
    f32 = np.float32
    rows = n_lat // GRID_W
    row = np.repeat(np.arange(rows), GRID_W).astype(f32)
    col = np.tile(np.arange(GRID_W), rows).astype(f32)
    n_freq = MLA_ROPE // 4
    inv = (f32(ROPE_BASE) ** (-np.arange(n_freq, dtype=f32) / f32(n_freq))).astype(f32)
    ang = np.concatenate([row[:, None] * inv, col[:, None] * inv], axis=-1).astype(f32)
    cos, sin = np.cos(ang).astype(f32), np.sin(ang).astype(f32)
    pad = np.zeros((n_lat, HEAD_DIM - MLA_ROPE), f32)
    cos_q = np.concatenate([cos, cos, pad], axis=1)
    sin_q = np.concatenate([-sin, sin, pad], axis=1)
    cos_k = np.concatenate([cos, cos, pad], axis=1)
    sin_k = np.concatenate([sin, sin, pad], axis=1)
    ctx_cos = np.concatenate([np.ones((n_ctx, MLA_ROPE), f32), np.zeros((n_ctx, HEAD_DIM - MLA_ROPE), f32)], axis=1)
    cos_k = np.concatenate([cos_k, ctx_cos], axis=0)
    sin_k = np.concatenate([sin_k, np.zeros((n_ctx, HEAD_DIM), f32)], axis=0)
    return jnp.asarray(cos_q), jnp.asarray(sin_q), jnp.asarray(cos_k), jnp.asarray(sin_k)


def kernel(x, c, ctx, c_ctx, w_mod, b_mod, norm_mix_g, w_in, mla_q_norm_g, w_q_b, mla_kv_norm_g, w_kv_b,
           hg_lb_logits, hg_norm_g, w_out, norm_ffn_g, w_router, b_router, w_gate_up, b_gate_up, w_down,
           b_down, final_norm_g):
    bsz, n_lat, d = x.shape
    n_ctx = ctx.shape[1]
    assert d == D_MODEL and w_mod.shape[0] == 1
    assert n_lat % HG_CHUNK == 0 and n_ctx % HG_CHUNK == 0 and n_lat % GRID_W == 0
    t_tok = bsz * n_lat

    mod_rows = -(-(bsz + 1) // 8) * 8
    cc = jnp.concatenate([c, c_ctx[None, :], jnp.zeros((mod_rows - bsz - 1, d), F32)], axis=0)
    mod = _modulation(cc, w_mod[0], b_mod[0][None, :])

    xa = jnp.concatenate([x, ctx], axis=1)
    w16, w32 = _prep_w_in(w_in[0])
    p, pf = _inproj(xa, mod, norm_mix_g[0][None, :], w16, w32, n_lat)

    cos_q, sin_q, cos_k, sin_k = _rope_tables(n_lat, n_ctx)
    q = _mla_q(p, mla_q_norm_g[0][None, :], _prep_w_q(w_q_b[0]), cos_q, sin_q, n_lat)
    k, v = _mla_kv(p, mla_kv_norm_g[0][None, :], _prep_w_kv(w_kv_b[0]), cos_k, sin_k)
    y_mla = _attention(q, k, v)

    lb = jax.nn.softmax(hg_lb_logits.astype(F32), axis=1)[:, 0, :]
    o_f, o_b = _hgrn_scans(p, pf, lb.reshape(2, HEADS, 1, HEAD_DIM), n_lat)

    x1, h2, logits = _merge(y_mla, o_f, o_b, p, x, mod, hg_norm_g[0][None, :], w_out[0].astype(BF16),
                            norm_ffn_g[0][None, :], w_router[0], b_router[0][None, :])
    logits_t = logits.T

    assert n_lat % ROUTE_TILE == 0
    idx_t, prob_t, lrank_t, base, cnt = _router(logits_t, ROUTE_TILE)

    n_tiles = t_tok // ROUTE_TILE
    counts = cnt[:, 0].astype(jnp.int32)
    base = base[:, :, 0].astype(jnp.int32)
    tile_count = jnp.concatenate([base[1:] - base[:-1], (counts - base[-1])[None, :]], axis=0)
    seg = (tile_count + ROW_ALIGN - 1) // ROW_ALIGN * ROW_ALIGN
    seg_end = jnp.cumsum(seg, axis=0)
    rows_e = seg_end[-1]
    padded = (rows_e + SLOT_ROWS + FFN_ROWS - 1) // FFN_ROWS * FFN_ROWS
    pad_end = jnp.cumsum(padded)
    pad_start = pad_end - padded
    max_rows = t_tok * TOP_K + N_EXPERTS * (n_tiles * (ROW_ALIGN - 1) + SLOT_ROWS + FFN_ROWS - 1)
    n_rows = -(-max_rows // FFN_ROWS) * FFN_ROWS
    nblk = n_rows // FFN_ROWS
    block_start = jnp.arange(nblk, dtype=jnp.int32) * FFN_ROWS
    block_e = jnp.minimum(jnp.sum(pad_end[None, :] <= block_start[:, None], axis=1), N_EXPERTS - 1).astype(jnp.int32)
    n_used = (pad_end[-1:] // FFN_ROWS).astype(jnp.int32)
    tile_start = (pad_start[None, :] + seg_end - seg).reshape(-1).astype(jnp.int32)
    n_pass = jnp.maximum(-(-jnp.max(tile_count, axis=1) // SLOT_ROWS), 1).astype(jnp.int32)
    tile_count = tile_count.reshape(-1)
    fill_start = jnp.minimum(pad_start + rows_e, n_rows - ZERO_ROWS).astype(jnp.int32)

    xs = _dispatch(tile_start, tile_count, n_pass, fill_start, n_used, idx_t, lrank_t, h2, n_rows)

    n_grp = 2 * D_EXPERT // GU_GROUP
    b_gu = b_gate_up[0].reshape(N_EXPERTS, n_grp, GU_GROUP // 2, 2).transpose(0, 1, 3, 2).reshape(N_EXPERTS, 1, 2 * D_EXPERT)
    y = _ffn(block_e, n_used, xs, w_gate_up[0], b_gu, w_down[0], b_down[0][:, None, :])

    return _combine(tile_start, tile_count, n_pass, idx_t.T, lrank_t.T, prob_t.T, y, x1, mod, final_norm_g[None, :])
```

```python
import functools

import jax
import jax.numpy as jnp
import numpy as np
from jax import lax
from jax.experimental import pallas as pl
from jax.experimental.pallas import tpu as pltpu

F32 = jnp.float32
BF16 = jnp.bfloat16

D_MODEL = 1024
EPS = 1e-6
LOG2_E = 1.4426950408889634
N_MOD = 6
GRID_W = 64
ROPE_BASE = 10000.0

HEADS = 8
HEAD_DIM = 128
MLA_ROPE = 64
MLA_QK = HEAD_DIM + MLA_ROPE
MLA_QK_PAD = 256
MLA_Q_RANK = 768
MLA_KV_RANK = 256
ROPE_HALF = MLA_ROPE // 2
QA_PIECE = 256

N_EXPERTS = 32
TOP_K = 4
D_EXPERT = 1024
SWIGLU_LIMIT = 7.0
SWIGLU_ALPHA = 1.702

HG_CHUNK = 128
FFN_ROWS = 512

COL_HQ, COL_I, COL_GO, COL_MM, COL_MH = (i * D_MODEL for i in range(5))
COL_QA = 5 * D_MODEL
COL_CKV = COL_QA + MLA_Q_RANK
COL_KR = COL_CKV + MLA_KV_RANK
PROJ16_TN = 1280
N_PROJ16 = -(-(COL_KR + 2 * MLA_ROPE) // PROJ16_TN) * PROJ16_TN
COL_FF, COL_FB = 0, D_MODEL
N_PROJ32 = 2 * D_MODEL
PROJ32_TN = 1024

VMEM_LIMIT = 56 * 1024 * 1024


def _cparams(sem, vmem=None):
    return pltpu.CompilerParams(dimension_semantics=sem, vmem_limit_bytes=vmem)


def _tile(n, pref, mult=8):
    best = None
    for t in range(mult, min(n, pref) + 1, mult):
        if n % t == 0:
            best = t
    assert best is not None, (n, pref, mult)
    return best


def _nt(a, b):
    return lax.dot_general(a, b, (((1,), (1,)), ((), ())), preferred_element_type=F32)


def _mm(a, b):
    return jnp.dot(a, b, preferred_element_type=F32)


def _split2(a):
    hi = a.astype(BF16)
    lo = (a - hi.astype(F32)).astype(BF16)
    return hi, lo


def _sigmoid(x):
    return 1.0 / (1.0 + jnp.exp(-x))


def _gate_sigmoid(x):
    return 0.5 * jnp.tanh(0.5 * x) + 0.5


def _mod_kernel(c_ref, w_ref, b_ref, o_ref):
    c = c_ref[...]
    s = c * _sigmoid(c)
    s_hi, s_lo = _split2(s)
    w_hi, w_lo = _split2(w_ref[...])
    o_ref[...] = _mm(s_hi, w_hi) + _mm(s_lo, w_hi) + _mm(s_hi, w_lo) + b_ref[...]


def _modulation(cc, w_mod, b_mod):
    r, d = cc.shape
    n = w_mod.shape[1]
    tn = _tile(n, 1536, 128)
    return pl.pallas_call(
        _mod_kernel,
        out_shape=jax.ShapeDtypeStruct((r, n), F32),
        grid=(n // tn,),
        in_specs=[pl.BlockSpec((r, d), lambda j: (0, 0)),
                  pl.BlockSpec((d, tn), lambda j: (0, j)),
                  pl.BlockSpec((1, tn), lambda j: (0, j))],
        out_specs=pl.BlockSpec((r, tn), lambda j: (0, j)),
        compiler_params=_cparams(("parallel",), VMEM_LIMIT),
        name="mod",
    )(cc, w_mod, b_mod)


INPROJ_GROUP = 2


def _inproj_kernel(x_ref, mod_ref, g_ref, w16_ref, w32_ref, o16_ref, o32_ref, hn_ref, *, n_lat, tm, nt, grp, ctx_row,
                   n16):
    j = pl.program_id(1)
    r = pl.program_id(2)
    b = pl.program_id(0) * grp + r // nt
    t = r % nt

    @pl.when(j == 0)
    def _():
        x = x_ref[0]
        ms = jnp.mean(x * x, axis=-1, keepdims=True)
        y = x * lax.rsqrt(ms + EPS) * g_ref[...]
        row = t * tm + lax.broadcasted_iota(jnp.int32, (tm, 1), 0)
        is_ctx = row >= n_lat
        m_lat = mod_ref[pl.ds(b, 1), :]
        m_ctx = mod_ref[pl.ds(ctx_row, 1), :]
        shift = jnp.where(is_ctx, m_ctx[:, 0:D_MODEL], m_lat[:, 0:D_MODEL])
        scale = jnp.where(is_ctx, m_ctx[:, D_MODEL:2 * D_MODEL], m_lat[:, D_MODEL:2 * D_MODEL])
        hn_ref[r] = (y * (1.0 + scale) + shift).astype(BF16)

    @pl.when(j < n16)
    def _():
        o16_ref[0] = _mm(hn_ref[r], w16_ref[...]).astype(BF16)

    @pl.when(j >= n16)
    def _():
        o32_ref[0] = _mm(hn_ref[r], w32_ref[...])


def _inproj(xa, mod, g, w16, w32, n_lat):
    bsz, rows, d = xa.shape
    tm = _tile(rows, 1152)
    nt = rows // tm
    grp = INPROJ_GROUP if bsz % INPROJ_GROUP == 0 else 1
    n_r = grp * nt
    n16 = N_PROJ16 // PROJ16_TN
    n32 = N_PROJ32 // PROJ32_TN

    def row_tile(g_, r):
        return g_ * grp + r // nt, r % nt

    def x_map(g_, j, r):
        b, t = row_tile(g_, jnp.where(j == 0, r, n_r - 1))
        return b, t, 0

    def o16_map(g_, j, r):
        b, t = row_tile(g_, jnp.where(j < n16, r, n_r - 1))
        return b, t, jnp.minimum(j, n16 - 1)

    def o32_map(g_, j, r):
        b, t = row_tile(g_, jnp.where(j >= n16, r, 0))
        return b, t, jnp.maximum(j - n16, 0)

    kern = functools.partial(_inproj_kernel, n_lat=n_lat, tm=tm, nt=nt, grp=grp, ctx_row=bsz, n16=n16)
    return pl.pallas_call(
        kern,
        out_shape=(jax.ShapeDtypeStruct((bsz, rows, N_PROJ16), BF16),
                   jax.ShapeDtypeStruct((bsz, rows, N_PROJ32), F32)),
        grid=(bsz // grp, n16 + n32, n_r),
        in_specs=[pl.BlockSpec((1, tm, d), x_map),
                  pl.BlockSpec(mod.shape, lambda g_, j, r: (0, 0)),
                  pl.BlockSpec((1, d), lambda g_, j, r: (0, 0)),
                  pl.BlockSpec((d, PROJ16_TN), lambda g_, j, r: (0, jnp.minimum(j, n16 - 1))),
                  pl.BlockSpec((d, PROJ32_TN), lambda g_, j, r: (0, jnp.maximum(j - n16, 0)))],
        out_specs=(pl.BlockSpec((1, tm, PROJ16_TN), o16_map),
                   pl.BlockSpec((1, tm, PROJ32_TN), o32_map)),
        scratch_shapes=[pltpu.VMEM((n_r, tm, d), BF16)],
        compiler_params=_cparams(("parallel", "arbitrary", "arbitrary"), VMEM_LIMIT),
        name="inproj",
    )(xa, mod, g, w16, w32)


def _mlaq_kernel(a0_ref, a1_ref, a2_ref, g_ref, w_ref, cos_ref, sin_ref, o_ref, *, tm, n_parts):
    n_rows = tm // n_parts
    scale = MLA_QK ** -0.5 * LOG2_E

    def part(i):
        rows = slice(i * n_rows, (i + 1) * n_rows)
        parts = [a_ref[0, rows, :].astype(F32) for a_ref in (a0_ref, a1_ref, a2_ref)]
        ss = sum(jnp.sum(p * p, axis=-1, keepdims=True) for p in parts)
        r = lax.rsqrt(ss * (1.0 / MLA_Q_RANK) + EPS)
        acc = None
        for j, p in enumerate(parts):
            hj = (p * r * g_ref[:, j * QA_PIECE:(j + 1) * QA_PIECE]).astype(BF16)
            d = _mm(hj, w_ref[j * QA_PIECE:(j + 1) * QA_PIECE, :])
            acc = d if acc is None else acc + d
        yield
        lane = lax.broadcasted_iota(jnp.int32, (n_rows, HEAD_DIM), 1)
        cos = cos_ref[rows, :]
        sin = sin_ref[rows, :]
        for h in range(HEADS):
            nope = acc[:, h * MLA_QK_PAD:h * MLA_QK_PAD + HEAD_DIM]
            rp = acc[:, h * MLA_QK_PAD + HEAD_DIM:(h + 1) * MLA_QK_PAD]
            swapped = jnp.where(lane < ROPE_HALF, pltpu.roll(rp, HEAD_DIM - ROPE_HALF, 1),
                                pltpu.roll(rp, ROPE_HALF, 1))
            rot = rp * cos + swapped * sin
            o_ref[0, h, rows, 0:HEAD_DIM] = (nope * scale).astype(BF16)
            o_ref[0, h, rows, HEAD_DIM:MLA_QK_PAD] = (rot * scale).astype(BF16)

    _round_robin([part(i) for i in range(n_parts)])


def _mla_q(p, g, w, cos_q, sin_q, n_lat):
    bsz = p.shape[0]
    tm = _tile(n_lat, 512)
    cb = COL_QA // QA_PIECE
    kern = functools.partial(_mlaq_kernel, tm=tm, n_parts=2 if tm % 32 == 0 else 1)
    return pl.pallas_call(
        kern,
        out_shape=jax.ShapeDtypeStruct((bsz, HEADS, n_lat, MLA_QK_PAD), BF16),
        grid=(bsz, n_lat // tm),
        in_specs=[pl.BlockSpec((1, tm, QA_PIECE), lambda b, t: (b, t, cb)),
                  pl.BlockSpec((1, tm, QA_PIECE), lambda b, t: (b, t, cb + 1)),
                  pl.BlockSpec((1, tm, QA_PIECE), lambda b, t: (b, t, cb + 2)),
                  pl.BlockSpec((1, MLA_Q_RANK), lambda b, t: (0, 0)),
                  pl.BlockSpec(w.shape, lambda b, t: (0, 0)),
                  pl.BlockSpec((tm, HEAD_DIM), lambda b, t: (t, 0)),
                  pl.BlockSpec((tm, HEAD_DIM), lambda b, t: (t, 0))],
        out_specs=pl.BlockSpec((1, HEADS, tm, MLA_QK_PAD), lambda b, t: (b, 0, t, 0)),
        compiler_params=_cparams(("parallel", "parallel"), VMEM_LIMIT),
        name="mla_q",
    )(p, p, p, g, w, cos_q, sin_q)


def _mlakv_kernel(c_ref, kr_ref, g_ref, w_ref, cos_ref, sin_ref, k_ref, v_ref):
    c = c_ref[0].astype(F32)
    ms = jnp.mean(c * c, axis=-1, keepdims=True)
    hn = (c * lax.rsqrt(ms + EPS) * g_ref[...]).astype(BF16)
    kv = _mm(hn, w_ref[...])
    grp = kr_ref[0].astype(F32)
    rot = (grp * cos_ref[...] + pltpu.roll(grp, MLA_ROPE, 1) * sin_ref[...]).astype(BF16)
    for h in range(HEADS):
        k_ref[0, h, :, 0:HEAD_DIM] = kv[:, h * HEAD_DIM:(h + 1) * HEAD_DIM].astype(BF16)
        k_ref[0, h, :, HEAD_DIM:MLA_QK_PAD] = rot
        v_ref[0, h] = kv[:, D_MODEL + h * HEAD_DIM:D_MODEL + (h + 1) * HEAD_DIM].astype(BF16)


def _mla_kv(p, g, w, cos_k, sin_k):
    bsz, rows, _ = p.shape
    tm = _tile(rows, 768)
    return pl.pallas_call(
        _mlakv_kernel,
        out_shape=(jax.ShapeDtypeStruct((bsz, HEADS, rows, MLA_QK_PAD), BF16),
                   jax.ShapeDtypeStruct((bsz, HEADS, rows, HEAD_DIM), BF16)),
        grid=(bsz, rows // tm),
        in_specs=[pl.BlockSpec((1, tm, MLA_KV_RANK), lambda b, t: (b, t, COL_CKV // MLA_KV_RANK)),
                  pl.BlockSpec((1, tm, 128), lambda b, t: (b, t, COL_KR // 128)),
                  pl.BlockSpec((1, MLA_KV_RANK), lambda b, t: (0, 0)),
                  pl.BlockSpec(w.shape, lambda b, t: (0, 0)),
                  pl.BlockSpec((tm, 128), lambda b, t: (t, 0)),
                  pl.BlockSpec((tm, 128), lambda b, t: (t, 0))],
        out_specs=(pl.BlockSpec((1, HEADS, tm, MLA_QK_PAD), lambda b, t: (b, 0, t, 0)),
                   pl.BlockSpec((1, HEADS, tm, HEAD_DIM), lambda b, t: (b, 0, t, 0))),
        compiler_params=_cparams(("parallel", "parallel"), VMEM_LIMIT),
        name="mla_kv",
    )(p, p, g, w, cos_k, sin_k)


def _attn_kernel(q_ref, k_ref, v_ref, o_ref, *, tq, sub):
    k = k_ref[0, 0]
    v = v_ref[0, 0]
    v1 = jnp.concatenate([v, jnp.ones_like(v)], axis=1)

    def sub_tile(r):
        rows = slice(r * sub, (r + 1) * sub)
        s = _nt(q_ref[0, 0, rows, :], k)
        yield
        m = jnp.max(s, axis=-1, keepdims=True)
        p = jnp.exp2(s - m).astype(BF16)
        yield
        ol = _mm(p, v1)
        o_ref[0, rows, :] = (ol[:, 0:HEAD_DIM] * (1.0 / ol[:, HEAD_DIM:HEAD_DIM + 1])).astype(BF16)

    _round_robin([sub_tile(r) for r in range(tq // sub)])


def _attention(q, k, v):
    bsz, _, n, _ = q.shape
    m = k.shape[2]
    tq = _tile(n, 2048)
    sub = _tile(tq, 512)
    kern = functools.partial(_attn_kernel, tq=tq, sub=sub)
    return pl.pallas_call(
        kern,
        out_shape=jax.ShapeDtypeStruct((bsz, n, HEADS * HEAD_DIM), BF16),
        grid=(bsz, HEADS, n // tq),
        in_specs=[pl.BlockSpec((1, 1, tq, MLA_QK_PAD), lambda b, h, t: (b, h, t, 0)),
                  pl.BlockSpec((1, 1, m, MLA_QK_PAD), lambda b, h, t: (b, h, 0, 0)),
                  pl.BlockSpec((1, 1, m, HEAD_DIM), lambda b, h, t: (b, h, 0, 0))],
        out_specs=pl.BlockSpec((1, tq, HEAD_DIM), lambda b, h, t: (b, t, h)),
        compiler_params=_cparams(("parallel", "parallel", "arbitrary"), VMEM_LIMIT),
        name="attention",
    )(q, k, v)


def _level_ref(cum, blk, reverse):
    c = cum.shape[0]
    half = blk // 2
    r = half if reverse else half - 1
    if blk >= 8:
        x = cum.reshape(c // blk, blk, HEAD_DIM)
        e = jnp.broadcast_to(x[:, r:r + 1, :], x.shape)
        return e.reshape(c, HEAD_DIM)
    x = cum.reshape(c // 8, 8, HEAD_DIM)
    sub = lax.broadcasted_iota(jnp.int32, x.shape, 1)
    e = None
    for jb in range(8 // blk):
        cand = jnp.broadcast_to(x[:, jb * blk + r:jb * blk + r + 1, :], x.shape)
        e = cand if e is None else jnp.where(sub >= jb * blk, cand, e)
    return e.reshape(c, HEAD_DIM)


HG_SUB = 32
HG_SUB_LEVEL = HG_SUB.bit_length() - 1
HG_MAX_EXP2 = 100.0


def _hgrn_state(z, v, lb, tri, st, *, chunk, reverse):
    f = lb + (1.0 - lb) * _sigmoid(z)
    kb = (1.0 - f).astype(BF16)
    g = jnp.log(f) * LOG2_E

    g_hi, g_lo = _split2(g)
    yield
    cum = _mm(tri, jnp.concatenate([g_hi, g_lo], axis=0))
    yield
    last = 0 if reverse else chunk - 1
    tot = cum[last:last + 1, :]
    kt = kb * jnp.exp2(tot - cum).astype(BF16)
    vt = v.astype(F32).T.astype(BF16)
    yield
    st_new = st * jnp.exp2(tot) + _mm(vt, kt)
    return kb, cum, st_new


def _sub_block_decay(cum, reverse):
    c = cum.shape[0]
    x = cum.reshape(c // HG_SUB, HG_SUB, HEAD_DIM)
    zero = jnp.zeros((1, 1, HEAD_DIM), F32)
    if reverse:
        edge = jnp.concatenate([x[1:, 0:1, :], zero], axis=0)
    else:
        edge = jnp.concatenate([zero, x[:-1, HG_SUB - 1:HG_SUB, :]], axis=0)
    return (x - edge).reshape(c, HEAD_DIM)


def _hgrn_readout(q, v, kb, cum, sub, lvl, st, *, chunk, reverse, shared):
    q = q.astype(F32)
    qb = (q * _gate_sigmoid(q) * (HEAD_DIM ** -0.5)).astype(BF16)
    yield
    if shared:
        att = jnp.where(lvl == 0, _nt(qb * jnp.exp2(sub).astype(BF16), kb * jnp.exp2(-sub).astype(BF16)).astype(BF16),
                        jnp.zeros((), BF16))
        first = HG_SUB_LEVEL + 1
    else:
        att = jnp.where(lvl == 0, _nt(qb, kb).astype(BF16), jnp.zeros((), BF16))
        first = 1
    for lv in range(first, chunk.bit_length()):
        zrel = cum - _level_ref(cum, 1 << lv, reverse)
        neg_abs = pltpu.bitcast(pltpu.bitcast(zrel, jnp.uint32) | jnp.uint32(0x80000000), F32)
        e = jnp.exp2(neg_abs).astype(BF16)
        yield
        att = jnp.where(lvl == lv, _nt(qb * e, kb * e).astype(BF16), att)
        yield
    return _nt(qb * jnp.exp2(cum).astype(BF16), st.astype(BF16)) + _mm(att, v.astype(BF16))


def _round_robin(gens):
    out = [None] * len(gens)
    active = list(range(len(gens)))
    while active:
        for i in list(active):
            try:
                next(gens[i])
            except StopIteration as e:
                out[i] = e.value
                active.remove(i)
    return out


def _hgrn_kernel(qf_ref, zf_ref, if_ref, qb_ref, zb_ref, ib_ref, lb_ref, tri_ref, lvl_ref, of_ref, ob_ref, st_ref,
                 *, chunk, group, n_ctx_chunks):
    step = pl.program_id(2)

    @pl.when(step == 0)
    def _():
        st_ref[...] = jnp.zeros_like(st_ref)

    ins = ((qf_ref, zf_ref, if_ref), (qb_ref, zb_ref, ib_ref))
    outs = (of_ref, ob_ref)
    chains = [(d, j) for j in range(group) for d in range(2)]

    def cols(j):
        return slice(j * HEAD_DIM, (j + 1) * HEAD_DIM)

    def advance(want_out):
        old = [st_ref[d, j] for d, j in chains]
        parts = _round_robin([_hgrn_state(ins[d][1][0, :, cols(j)], ins[d][2][0, :, cols(j)], lb_ref[d, j],
                                          tri_ref[d], st, chunk=chunk, reverse=bool(d))
                              for (d, j), st in zip(chains, old)])
        for (d, j), (_, _, st_new) in zip(chains, parts):
            st_ref[d, j] = st_new
        if not want_out:
            return
        subs = [_sub_block_decay(cum, bool(d)) for (d, j), (_, cum, _) in zip(chains, parts)]
        low = subs[0]
        for s in subs[1:]:
            low = jnp.minimum(low, s)
        low = jnp.min(jnp.min(low, axis=0, keepdims=True), axis=1, keepdims=True)
        in_range = low[0, 0] >= -HG_MAX_EXP2

        def readouts(shared):
            os_ = _round_robin([_hgrn_readout(ins[d][0][0, :, cols(j)], ins[d][2][0, :, cols(j)], kb, cum, sub,
                                              lvl_ref[d, int(shared)], st, chunk=chunk, reverse=bool(d), shared=shared)
                                for (d, j), (kb, cum, _), sub, st in zip(chains, parts, subs, old)])
            for (d, j), o in zip(chains, os_):
                outs[d][0, j] = o.astype(BF16)

        @pl.when(in_range)
        def _():
            readouts(True)

        @pl.when(jnp.logical_not(in_range))
        def _():
            readouts(False)

    @pl.when(step < n_ctx_chunks)
    def _():
        advance(False)

    @pl.when(step >= n_ctx_chunks)
    def _():
        advance(True)


def _hgrn_consts(chunk, reverse):
    t = np.arange(chunk)[:, None]
    s = np.arange(chunk)[None, :]
    x = t ^ s
    bitlen = np.zeros_like(x)
    for b in range(chunk.bit_length()):
        bitlen = np.where(x >> b > 0, b + 1, bitlen)
    valid = (s > t) if reverse else (t > s)
    lvl = np.where(t == s, 0, np.where(valid, bitlen, -1)).astype(np.float32)
    lvl_shared = np.where((lvl >= 0) & (lvl <= HG_SUB_LEVEL), 0, lvl)
    tri = ((s >= t) if reverse else (t >= s)).astype(np.float32)
    return np.concatenate([tri, tri], axis=1), np.stack([lvl, lvl_shared])


HG_GROUP = 8


def _hgrn_scans(p, pf, lb, n_lat):
    bsz, rows, _ = p.shape
    c = HG_CHUNK
    grp = HG_GROUP
    n_chunks = rows // c
    n_lat_c = n_lat // c
    n_ctx_c = n_chunks - n_lat_c
    consts = [_hgrn_consts(c, False), _hgrn_consts(c, True)]
    tri = jnp.asarray(np.stack([consts[0][0], consts[1][0]]), dtype=BF16)
    lvl = jnp.asarray(np.stack([consts[0][1], consts[1][1]]), dtype=BF16)

    def cidx_f(i):
        return jnp.where(i < n_ctx_c, n_lat_c + i, i - n_ctx_c)

    def cidx_b(i):
        return n_chunks - 1 - i

    def col_spec(col, cidx):
        return pl.BlockSpec((1, c, grp * HEAD_DIM), lambda b, h, i: (b, cidx(i), col // (grp * HEAD_DIM) + h))

    def out_spec(cidx):
        return pl.BlockSpec((1, grp, c, HEAD_DIM), lambda b, h, i: (b, h, cidx(jnp.maximum(i, n_ctx_c)), 0))

    o_shape = jax.ShapeDtypeStruct((bsz, HEADS, n_lat, HEAD_DIM), BF16)
    kern = functools.partial(_hgrn_kernel, chunk=c, group=grp, n_ctx_chunks=n_ctx_c)
    return pl.pallas_call(
        kern,
        out_shape=(o_shape, o_shape),
        grid=(bsz, HEADS // grp, n_chunks),
        in_specs=[col_spec(COL_HQ, cidx_f), col_spec(COL_FF, cidx_f), col_spec(COL_I, cidx_f),
                  col_spec(COL_HQ, cidx_b), col_spec(COL_FB, cidx_b), col_spec(COL_I, cidx_b),
                  pl.BlockSpec((2, grp, 1, HEAD_DIM), lambda b, h, i: (0, h, 0, 0)),
                  pl.BlockSpec((2, c, 2 * c), lambda b, h, i: (0, 0, 0)),
                  pl.BlockSpec((2, 2, c, c), lambda b, h, i: (0, 0, 0, 0))],
        out_specs=(out_spec(cidx_f), out_spec(cidx_b)),
        scratch_shapes=[pltpu.VMEM((2, grp, HEAD_DIM, HEAD_DIM), F32)],
        compiler_params=_cparams(("parallel", "parallel", "arbitrary"), VMEM_LIMIT),
        name="hgrn",
    )(p, pf, p, p, pf, p, lb, tri, lvl)


def _merge_kernel(ym_ref, of_ref, ob_ref, go_ref, gm_ref, gh_ref, x_ref, mod_ref, hgg_ref, wout_ref,
                  gffn_ref, wr2_ref, wrhi_ref, br_ref, x1_ref, h2_ref, lg_ref, y_scr, *, tm, parts):
    b = pl.program_id(0)
    m = mod_ref[pl.ds(b, 1), :]
    n_rows = tm // parts

    def part(i):
        rows = slice(i * n_rows, (i + 1) * n_rows)
        for h in range(HEADS):
            sl = slice(h * HEAD_DIM, (h + 1) * HEAD_DIM)
            o = of_ref[0, h, rows, :].astype(F32) + ob_ref[0, h, rows, :].astype(F32)
            ms = jnp.mean(o * o, axis=-1, keepdims=True)
            g = go_ref[0, rows, sl].astype(F32)
            yh = o * lax.rsqrt(ms + EPS) * hgg_ref[...] * (g * _gate_sigmoid(g))
            y = (_gate_sigmoid(gm_ref[0, rows, sl].astype(F32)) * ym_ref[0, rows, sl].astype(F32)
                 + _gate_sigmoid(gh_ref[0, rows, sl].astype(F32)) * yh)
            y_scr[rows, sl] = y.astype(BF16)
        yield
        mix = _mm(y_scr[rows, :], wout_ref[...])
        yield
        x1 = x_ref[0, rows, :] + m[:, 2 * D_MODEL:3 * D_MODEL] * mix
        x1_ref[0, rows, :] = x1
        ms = jnp.mean(x1 * x1, axis=-1, keepdims=True)
        h2 = ((x1 * lax.rsqrt(ms + EPS) * gffn_ref[...]) * (1.0 + m[:, 4 * D_MODEL:5 * D_MODEL])
              + m[:, 3 * D_MODEL:4 * D_MODEL])
        h_hi = h2.astype(BF16)
        h2_ref[rows, :] = h_hi
        h_lo = (h2 - h_hi.astype(F32)).astype(BF16)
        a = _mm(h_hi, wr2_ref[...])
        lg_ref[rows, :] = (a[:, 0:N_EXPERTS] + a[:, N_EXPERTS:2 * N_EXPERTS] + _mm(h_lo, wrhi_ref[...])
                           + br_ref[...])

    _round_robin([part(i) for i in range(parts)])


def _merge(y_mla, o_f, o_b, p, x, mod, hg_g, w_out, g_ffn, w_r, b_r):
    bsz, n, d = x.shape
    tm = _tile(n, 256, 128)
    nt = n // tm
    w_r_hi, w_r_lo = _split2(w_r)
    w_r2 = jnp.concatenate([w_r_hi, w_r_lo], axis=1)
    kern = functools.partial(_merge_kernel, tm=tm, parts=2)

    def pcol(col):
        return pl.BlockSpec((1, tm, d), lambda b, t: (b, t, col // d))

    tok = lambda b, t: (b, t, 0)
    const2 = lambda b, t: (0, 0)
    return pl.pallas_call(
        kern,
        out_shape=(jax.ShapeDtypeStruct((bsz, n, d), F32),
                   jax.ShapeDtypeStruct((bsz * n, d), BF16),
                   jax.ShapeDtypeStruct((bsz * n, N_EXPERTS), F32)),
        grid=(bsz, nt),
        in_specs=[pl.BlockSpec((1, tm, d), tok),
                  pl.BlockSpec((1, HEADS, tm, HEAD_DIM), lambda b, t: (b, 0, t, 0)),
                  pl.BlockSpec((1, HEADS, tm, HEAD_DIM), lambda b, t: (b, 0, t, 0)),
                  pcol(COL_GO), pcol(COL_MM), pcol(COL_MH),
                  pl.BlockSpec((1, tm, d), tok),
                  pl.BlockSpec(mod.shape, const2),
                  pl.BlockSpec((1, HEAD_DIM), const2),
                  pl.BlockSpec((d, d), const2),
                  pl.BlockSpec((1, d), const2),
                  pl.BlockSpec((d, 2 * N_EXPERTS), const2),
                  pl.BlockSpec((d, N_EXPERTS), const2),
                  pl.BlockSpec((1, N_EXPERTS), const2)],
        out_specs=(pl.BlockSpec((1, tm, d), tok),
                   pl.BlockSpec((tm, d), lambda b, t: (b * nt + t, 0)),
                   pl.BlockSpec((tm, N_EXPERTS), lambda b, t: (b * nt + t, 0))),
        scratch_shapes=[pltpu.VMEM((tm, d), BF16)],
        compiler_params=_cparams(("parallel", "parallel"), VMEM_LIMIT),
        name="merge",
    )(y_mla, o_f, o_b, p, p, p, x, mod, hg_g, w_out, g_ffn, w_r2, w_r_hi, b_r)


def _router_kernel(lg_ref, upper_ref, idx_ref, prob_ref, rank_ref, base_ref, cnt_ref, run_ref, *, tm):
    @pl.when(pl.program_id(0) == 0)
    def _():
        run_ref[...] = jnp.zeros_like(run_ref)

    l = lg_ref[...]
    eidx = lax.broadcasted_iota(jnp.int32, (N_EXPERTS, tm), 0)
    vals, sel = [], []
    for _ in range(TOP_K):
        m = jnp.max(l, axis=0, keepdims=True)
        first = jnp.min(jnp.where(l == m, eidx, N_EXPERTS), axis=0, keepdims=True)
        vals.append(m)
        sel.append(first)
        l = jnp.where(eidx == first, -jnp.inf, l)
    ex = [jnp.exp(v - vals[0]) for v in vals]
    inv = 1.0 / (ex[0] + ex[1] + ex[2] + ex[3])
    onehot = jnp.zeros((N_EXPERTS, tm), F32)
    for k in range(TOP_K):
        onehot = onehot + jnp.where(eidx == sel[k], 1.0, 0.0)
    before = _mm(onehot.astype(BF16), upper_ref[...])
    base_ref[0] = run_ref[...]
    for k in range(TOP_K):
        idx_ref[k:k + 1, :] = sel[k]
        prob_ref[k:k + 1, :] = ex[k] * inv
        rank_ref[k:k + 1, :] = jnp.sum(jnp.where(eidx == sel[k], before, 0.0), axis=0, keepdims=True).astype(jnp.int32)
    run_ref[...] = run_ref[...] + jnp.sum(onehot, axis=1, keepdims=True)
    cnt_ref[...] = run_ref[...]


def _router(logits_t, tm):
    _, t = logits_t.shape
    upper = jnp.asarray(np.triu(np.ones((tm, tm), np.float32), 1), dtype=BF16)
    kern = functools.partial(_router_kernel, tm=tm)
    tok = pl.BlockSpec((TOP_K, tm), lambda i: (0, i))
    return pl.pallas_call(
        kern,
        out_shape=(jax.ShapeDtypeStruct((TOP_K, t), jnp.int32),
                   jax.ShapeDtypeStruct((TOP_K, t), F32),
                   jax.ShapeDtypeStruct((TOP_K, t), jnp.int32),
                   jax.ShapeDtypeStruct((t // tm, N_EXPERTS, 128), F32),
                   jax.ShapeDtypeStruct((N_EXPERTS, 128), F32)),
        grid=(t // tm,),
        in_specs=[pl.BlockSpec((N_EXPERTS, tm), lambda i: (0, i)),
                  pl.BlockSpec((tm, tm), lambda i: (0, 0))],
        out_specs=(tok, tok, tok, pl.BlockSpec((1, N_EXPERTS, 128), lambda i: (i, 0, 0)),
                   pl.BlockSpec((N_EXPERTS, 128), lambda i: (0, 0))),
        scratch_shapes=[pltpu.VMEM((N_EXPERTS, 128), F32)],
        compiler_params=_cparams(("arbitrary",), VMEM_LIMIT),
        name="router",
    )(logits_t, upper)


ROUTE_TILE = 256
ROW_ALIGN = 8
SLOT_ROWS = 48
N_SLOTS = N_EXPERTS * SLOT_ROWS
ZERO_ROWS = FFN_ROWS + SLOT_ROWS


def _slot_ids(idx, lrank, c):
    r = lrank - c * SLOT_ROWS
    return jnp.where(jnp.logical_and(r >= 0, r < SLOT_ROWS), idx * SLOT_ROWS + r, -1)


def _pack_bf16_pairs(x):
    u = pltpu.bitcast(x, jnp.uint32)
    w = x.shape[1] // 2
    return (u[:, 0:w] >> 16) | u[:, w:2 * w]


def _unpack_bf16_pairs(u):
    lo = pltpu.bitcast(u << 16, F32).astype(BF16)
    hi = pltpu.bitcast(u & jnp.uint32(0xFFFF0000), F32).astype(BF16)
    return jnp.concatenate([lo, hi], axis=1)


def _dispatch_kernel(tstart_ref, tcnt_ref, npass_ref, fill_ref, nused_ref, idx_ref, lrank_ref, h2_ref, xs_ref,
                     xbuf, zbuf, sem, zsem, state, *, tm, n_blocks):
    j = pl.program_id(0)

    def slot_copy(slot, e, row):
        return pltpu.make_async_copy(xbuf.at[slot, pl.ds(e * SLOT_ROWS, SLOT_ROWS)],
                                     xs_ref.at[pl.ds(pl.multiple_of(row, ROW_ALIGN), SLOT_ROWS)], sem.at[slot])

    def wait_pass(slot, n):
        @pl.when(n == N_EXPERTS)
        def _():
            pltpu.make_async_copy(xbuf.at[slot], xs_ref.at[pl.ds(0, N_SLOTS)], sem.at[slot]).wait()

        @pl.when(n != N_EXPERTS)
        def _():
            def body(i, carry):
                slot_copy(slot, 0, 0).wait()
                return carry
            lax.fori_loop(0, n, body, 0)

    @pl.when(j == 0)
    def _():
        zbuf[...] = jnp.zeros_like(zbuf)
        state[0] = 0
        state[1] = 0
        for parity in range(2):
            for e in range(parity, N_EXPERTS, 2):
                pltpu.make_async_copy(zbuf, xs_ref.at[pl.ds(pl.multiple_of(fill_ref[e], ROW_ALIGN), ZERO_ROWS)],
                                      zsem).start()
            for e in range(parity, N_EXPERTS, 2):
                pltpu.make_async_copy(zbuf, xs_ref.at[pl.ds(0, ZERO_ROWS)], zsem).wait()

        def tail(i, carry):
            pltpu.make_async_copy(zbuf.at[pl.ds(0, FFN_ROWS)],
                                  xs_ref.at[pl.ds(pl.multiple_of(i * FFN_ROWS, ROW_ALIGN), FFN_ROWS)], zsem).start()
            return carry
        lax.fori_loop(nused_ref[0], n_blocks, tail, 0)

        def tail_wait(i, carry):
            pltpu.make_async_copy(zbuf.at[pl.ds(0, FFN_ROWS)], xs_ref.at[pl.ds(0, FFN_ROWS)], zsem).wait()
            return carry
        lax.fori_loop(nused_ref[0], n_blocks, tail_wait, 0)

    def one_pass(c, carry):
        rows = lax.broadcasted_iota(jnp.int32, (N_SLOTS, tm), 0)
        sel = jnp.zeros((N_SLOTS, tm), F32)
        for k in range(TOP_K):
            sel = jnp.where(rows == _slot_ids(idx_ref[k:k + 1, :], lrank_ref[k:k + 1, :], c), 1.0, sel)
        packed = _pack_bf16_pairs(_mm(sel.astype(BF16), h2_ref[...]))
        p = state[0]
        slot = p % 2
        wait_pass(1 - slot, state[1])
        xbuf[slot] = packed
        n = 0
        for e in range(N_EXPERTS):
            live = tcnt_ref[j * N_EXPERTS + e] > c * SLOT_ROWS

            @pl.when(live)
            def _():
                slot_copy(slot, e, tstart_ref[j * N_EXPERTS + e] + c * SLOT_ROWS).start(priority=e % 2)
            n = n + live.astype(jnp.int32)
        state[0] = p + 1
        state[1] = n
        return carry

    lax.fori_loop(0, npass_ref[j], one_pass, 0)

    @pl.when(j == pl.num_programs(0) - 1)
    def _():
        wait_pass((state[0] - 1) % 2, state[1])


def _dispatch(tile_start, tile_count, n_pass, fill_start, n_used, idx_t, lrank_t, h2, n_rows):
    t, d = h2.shape
    tm = ROUTE_TILE
    w = d // 2
    n_blocks = n_rows // FFN_ROWS
    kern = functools.partial(_dispatch_kernel, tm=tm, n_blocks=n_blocks)
    tok = pl.BlockSpec((TOP_K, tm), lambda i, *_: (0, i))
    grid_spec = pltpu.PrefetchScalarGridSpec(
        num_scalar_prefetch=5,
        grid=(t // tm,),
        in_specs=[tok, tok, pl.BlockSpec((tm, d), lambda i, *_: (i, 0))],
        out_specs=pl.BlockSpec(memory_space=pl.ANY),
        scratch_shapes=[pltpu.VMEM((2, N_SLOTS, w), jnp.uint32), pltpu.VMEM((ZERO_ROWS, w), jnp.uint32),
                        pltpu.SemaphoreType.DMA((2,)), pltpu.SemaphoreType.DMA, pltpu.SMEM((2,), jnp.int32)],
    )
    return pl.pallas_call(
        kern,
        out_shape=jax.ShapeDtypeStruct((n_rows, w), jnp.uint32),
        grid_spec=grid_spec,
        compiler_params=_cparams(("arbitrary",), VMEM_LIMIT),
        name="dispatch",
    )(tile_start, tile_count, n_pass, fill_start, n_used, idx_t, lrank_t, h2)


GU_GROUP = 256


def _ffn_kernel(be_ref, nu_ref, x_ref, wgu_ref, bgu_ref, wd_ref, bd_ref, perm_ref, y_ref, wgu_s, wd_s):
    i = pl.program_id(0)
    live = i < nu_ref[0]
    new_expert = jnp.logical_or(i == 0, be_ref[i] != be_ref[jnp.maximum(i - 1, 0)])

    @pl.when(jnp.logical_and(live, new_expert))
    def _():
        for g in range(2 * D_EXPERT // GU_GROUP):
            sl = slice(g * GU_GROUP, (g + 1) * GU_GROUP)
            wgu_s[:, sl] = _mm(wgu_ref[0, :, sl].astype(BF16), perm_ref[...]).astype(BF16)
        wd_s[...] = wd_ref[0].astype(BF16)

    @pl.when(live)
    def _():
        x = _unpack_bf16_pairs(x_ref[...])
        gu = _mm(x, wgu_s[...]) + bgu_ref[0]
        half = GU_GROUP // 2
        n_grp = 2 * D_EXPERT // GU_GROUP
        glu = jnp.concatenate([gu[:, g * GU_GROUP:g * GU_GROUP + half] for g in range(n_grp)], axis=1)
        lin = jnp.concatenate([gu[:, g * GU_GROUP + half:(g + 1) * GU_GROUP] for g in range(n_grp)], axis=1)
        glu = jnp.minimum(glu, SWIGLU_LIMIT)
        lin = jnp.clip(lin, -SWIGLU_LIMIT, SWIGLU_LIMIT)
        act = glu * _sigmoid(SWIGLU_ALPHA * glu) * (lin + 1.0)
        y = _mm(act.astype(BF16), wd_s[...]) + bd_ref[0]
        y_ref[...] = _pack_bf16_pairs(y.astype(BF16).astype(F32))

    @pl.when(jnp.logical_not(live))
    def _():
        y_ref[...] = jnp.zeros_like(y_ref)


def _ffn(block_e, n_used, xs, w_gu, b_gu, w_d, b_d):
    n_rows, w = xs.shape
    bm = FFN_ROWS
    nblk = n_rows // bm
    d = w * 2
    half = GU_GROUP // 2
    perm = np.zeros((GU_GROUP, GU_GROUP), np.float32)
    perm[2 * np.arange(half), np.arange(half)] = 1.0
    perm[2 * np.arange(half) + 1, half + np.arange(half)] = 1.0

    def xmap(i, be, nu):
        return (jnp.minimum(i, nu[0] - 1), 0)

    def wmap(i, be, nu):
        return (be[i], 0, 0)

    grid_spec = pltpu.PrefetchScalarGridSpec(
        num_scalar_prefetch=2,
        grid=(nblk,),
        in_specs=[pl.BlockSpec((bm, w), xmap),
                  pl.BlockSpec((1, d, 2 * D_EXPERT), wmap),
                  pl.BlockSpec((1, 1, 2 * D_EXPERT), wmap),
                  pl.BlockSpec((1, D_EXPERT, d), wmap),
                  pl.BlockSpec((1, 1, d), wmap),
                  pl.BlockSpec((GU_GROUP, GU_GROUP), lambda i, be, nu: (0, 0))],
        out_specs=pl.BlockSpec((bm, w), lambda i, be, nu: (i, 0)),
        scratch_shapes=[pltpu.VMEM((d, 2 * D_EXPERT), BF16), pltpu.VMEM((D_EXPERT, d), BF16)],
    )
    return pl.pallas_call(
        _ffn_kernel,
        out_shape=jax.ShapeDtypeStruct((n_rows, w), jnp.uint32),
        grid_spec=grid_spec,
        compiler_params=_cparams(("arbitrary",), VMEM_LIMIT),
        name="ffn",
    )(block_e, n_used, xs, w_gu, b_gu, w_d, b_d, jnp.asarray(perm, dtype=BF16))


def _combine_kernel(tstart_ref, tcnt_ref, npass_ref, idx_ref, lrank_ref, p_ref, y_ref, x1_ref, mod_ref, g_ref, o_ref,
                    ybuf, sem, *, tm, nt):
    b = pl.program_id(0)
    j = b * nt + pl.program_id(1)
    n_steps = pl.num_programs(0) * nt
    slot = j % 2

    def slot_copy(s, e, row):
        return pltpu.make_async_copy(y_ref.at[pl.ds(pl.multiple_of(row, ROW_ALIGN), SLOT_ROWS)], ybuf.at[s, pl.ds(e * SLOT_ROWS, SLOT_ROWS)],
                                     sem.at[s])

    def fetch(tile, c, s):
        for e in range(N_EXPERTS):
            @pl.when(tcnt_ref[tile * N_EXPERTS + e] > c * SLOT_ROWS)
            def _():
                slot_copy(s, e, tstart_ref[tile * N_EXPERTS + e] + c * SLOT_ROWS).start(priority=e % 2)

    def arrive(tile, c, s):
        live = [tcnt_ref[tile * N_EXPERTS + e] > c * SLOT_ROWS for e in range(N_EXPERTS)]
        n = sum(l.astype(jnp.int32) for l in live)

        @pl.when(n == N_EXPERTS)
        def _():
            pltpu.make_async_copy(y_ref.at[pl.ds(0, N_SLOTS)], ybuf.at[s], sem.at[s]).wait()

        @pl.when(n != N_EXPERTS)
        def _():
            for e in range(N_EXPERTS):
                @pl.when(live[e])
                def _():
                    slot_copy(s, e, 0).wait()

    @pl.when(j == 0)
    def _():
        ybuf[...] = jnp.zeros_like(ybuf)
        fetch(0, 0, 0)

    @pl.when(j + 1 < n_steps)
    def _():
        fetch(j + 1, 0, 1 - slot)

    def weights(c):
        cols = lax.broadcasted_iota(jnp.int32, (tm, N_SLOTS), 1)
        w = jnp.zeros((tm, N_SLOTS), F32)
        for k in range(TOP_K):
            w = jnp.where(cols == _slot_ids(idx_ref[:, k:k + 1], lrank_ref[:, k:k + 1], c), p_ref[:, k:k + 1], w)
        return w.astype(BF16)

    arrive(j, 0, slot)
    acc = _mm(weights(0), _unpack_bf16_pairs(ybuf[slot]))

    def extra_pass(c, acc):
        fetch(j, c, 2)
        arrive(j, c, 2)
        return acc + _mm(weights(c), _unpack_bf16_pairs(ybuf[2]))

    acc = lax.fori_loop(1, npass_ref[j], extra_pass, acc)
    m = mod_ref[pl.ds(b, 1), :]
    xo = x1_ref[0] + m[:, 5 * D_MODEL:6 * D_MODEL] * acc
    ms = jnp.mean(xo * xo, axis=-1, keepdims=True)
    o_ref[0] = xo * lax.rsqrt(ms + EPS) * g_ref[...]


def _combine(tile_start, tile_count, n_pass, idx_c, lrank_c, prob_c, y, x1, mod, g_fin):
    bsz, n, d = x1.shape
    tm = ROUTE_TILE
    nt = n // tm
    kern = functools.partial(_combine_kernel, tm=tm, nt=nt)
    tok = pl.BlockSpec((tm, TOP_K), lambda b, t, *_: (b * nt + t, 0))
    grid_spec = pltpu.PrefetchScalarGridSpec(
        num_scalar_prefetch=3,
        grid=(bsz, nt),
        in_specs=[tok, tok, tok,
                  pl.BlockSpec(memory_space=pl.ANY),
                  pl.BlockSpec((1, tm, d), lambda b, t, *_: (b, t, 0)),
                  pl.BlockSpec(mod.shape, lambda b, t, *_: (0, 0)),
                  pl.BlockSpec((1, d), lambda b, t, *_: (0, 0))],
        out_specs=pl.BlockSpec((1, tm, d), lambda b, t, *_: (b, t, 0)),
        scratch_shapes=[pltpu.VMEM((3, N_SLOTS, d // 2), jnp.uint32), pltpu.SemaphoreType.DMA((3,))],
    )
    return pl.pallas_call(
        kern,
        out_shape=jax.ShapeDtypeStruct((bsz, n, d), F32),
        grid_spec=grid_spec,
        compiler_params=_cparams(("arbitrary", "arbitrary"), VMEM_LIMIT),
        name="combine",
    )(tile_start, tile_count, n_pass, idx_c, lrank_c, prob_c, y, x1, mod, g_fin)


def _prep_w_in(w_in):
    q_a, kv, hq, ff, fb, hi, go, mm, mh = jnp.split(
        w_in, np.cumsum((MLA_Q_RANK, MLA_KV_RANK + MLA_ROPE) + (D_MODEL,) * 6).tolist(), axis=1)
    c_kv, kr = kv[:, :MLA_KV_RANK], kv[:, MLA_KV_RANK:]
    half = MLA_ROPE // 2
    kr_sw = jnp.concatenate([-kr[:, half:], kr[:, :half]], axis=1)
    w16 = jnp.concatenate([hq, hi, go, mm, mh, q_a, c_kv, kr, kr_sw], axis=1)
    w16 = jnp.pad(w16, ((0, 0), (0, N_PROJ16 - w16.shape[1])))
    return w16.astype(BF16), jnp.concatenate([ff, fb], axis=1).astype(BF16)


def _prep_w_q(w_q_b):
    w = w_q_b.reshape(MLA_Q_RANK, HEADS, MLA_QK)
    w = jnp.pad(w, ((0, 0), (0, 0), (0, MLA_QK_PAD - MLA_QK)))
    return w.reshape(MLA_Q_RANK, HEADS * MLA_QK_PAD).astype(BF16)


def _prep_w_kv(w_kv_b):
    w = w_kv_b.reshape(MLA_KV_RANK, HEADS, 2 * HEAD_DIM)
    k = w[:, :, :HEAD_DIM].reshape(MLA_KV_RANK, HEADS * HEAD_DIM)
    v = w[:, :, HEAD_DIM:].reshape(MLA_KV_RANK, HEADS * HEAD_DIM)
    return jnp.concatenate([k, v], axis=1).astype(BF16)


def _rope_tables(n_lat, n_ctx):
    f32 = np.float32
    rows = n_lat // GRID_W
    row = np.repeat(np.arange(rows), GRID_W).astype(f32)
    col = np.tile(np.arange(GRID_W), rows).astype(f32)
    n_freq = MLA_ROPE // 4
    inv = (f32(ROPE_BASE) ** (-np.arange(n_freq, dtype=f32) / f32(n_freq))).astype(f32)
    ang = np.concatenate([row[:, None] * inv, col[:, None] * inv], axis=-1).astype(f32)
    cos, sin = np.cos(ang).astype(f32), np.sin(ang).astype(f32)
    pad = np.zeros((n_lat, HEAD_DIM - MLA_ROPE), f32)
    cos_q = np.concatenate([cos, cos, pad], axis=1)
    sin_q = np.concatenate([-sin, sin, pad], axis=1)
    cos_k = np.concatenate([cos, cos, pad], axis=1)
    sin_k = np.concatenate([sin, sin, pad], axis=1)
    ctx_cos = np.concatenate([np.ones((n_ctx, MLA_ROPE), f32), np.zeros((n_ctx, HEAD_DIM - MLA_ROPE), f32)], axis=1)
    cos_k = np.concatenate([cos_k, ctx_cos], axis=0)
    sin_k = np.concatenate([sin_k, np.zeros((n_ctx, HEAD_DIM), f32)], axis=0)
    return jnp.asarray(cos_q), jnp.asarray(sin_q), jnp.asarray(cos_k), jnp.asarray(sin_k)


def kernel(x, c, ctx, c_ctx, w_mod, b_mod, norm_mix_g, w_in, mla_q_norm_g, w_q_b, mla_kv_norm_g, w_kv_b,
           hg_lb_logits, hg_norm_g, w_out, norm_ffn_g, w_router, b_router, w_gate_up, b_gate_up, w_down,
           b_down, final_norm_g):
    bsz, n_lat, d = x.shape
    n_ctx = ctx.shape[1]
    assert d == D_MODEL and w_mod.shape[0] == 1
    assert n_lat % HG_CHUNK == 0 and n_ctx % HG_CHUNK == 0 and n_lat % GRID_W == 0
    t_tok = bsz * n_lat

    mod_rows = -(-(bsz + 1) // 8) * 8
    cc = jnp.concatenate([c, c_ctx[None, :], jnp.zeros((mod_rows - bsz - 1, d), F32)], axis=0)
    mod = _modulation(cc, w_mod[0], b_mod[0][None, :])

    xa = jnp.concatenate([x, ctx], axis=1)
    w16, w32 = _prep_w_in(w_in[0])
    p, pf = _inproj(xa, mod, norm_mix_g[0][None, :], w16, w32, n_lat)

    cos_q, sin_q, cos_k, sin_k = _rope_tables(n_lat, n_ctx)
    q = _mla_q(p, mla_q_norm_g[0][None, :], _prep_w_q(w_q_b[0]), cos_q, sin_q, n_lat)
    k, v = _mla_kv(p, mla_kv_norm_g[0][None, :], _prep_w_kv(w_kv_b[0]), cos_k, sin_k)
    y_mla = _attention(q, k, v)

    lb = jax.nn.softmax(hg_lb_logits.astype(F32), axis=1)[:, 0, :]
    o_f, o_b = _hgrn_scans(p, pf, lb.reshape(2, HEADS, 1, HEAD_DIM), n_lat)

    x1, h2, logits = _merge(y_mla, o_f, o_b, p, x, mod, hg_norm_g[0][None, :], w_out[0].astype(BF16),
                            norm_ffn_g[0][None, :], w_router[0], b_router[0][None, :])
    logits_t = logits.T

    assert n_lat % ROUTE_TILE == 0
    idx_t, prob_t, lrank_t, base, cnt = _router(logits_t, ROUTE_TILE)

    n_tiles = t_tok // ROUTE_TILE
    counts = cnt[:, 0].astype(jnp.int32)
    base = base[:, :, 0].astype(jnp.int32)
    tile_count = jnp.concatenate([base[1:] - base[:-1], (counts - base[-1])[None, :]], axis=0)
    seg = (tile_count + ROW_ALIGN - 1) // ROW_ALIGN * ROW_ALIGN
    seg_end = jnp.cumsum(seg, axis=0)
    rows_e = seg_end[-1]
    padded = (rows_e + SLOT_ROWS + FFN_ROWS - 1) // FFN_ROWS * FFN_ROWS
    pad_end = jnp.cumsum(padded)
    pad_start = pad_end - padded
    max_rows = t_tok * TOP_K + N_EXPERTS * (n_tiles * (ROW_ALIGN - 1) + SLOT_ROWS + FFN_ROWS - 1)
    n_rows = -(-max_rows // FFN_ROWS) * FFN_ROWS
    nblk = n_rows // FFN_ROWS
    block_start = jnp.arange(nblk, dtype=jnp.int32) * FFN_ROWS
    block_e = jnp.minimum(jnp.sum(pad_end[None, :] <= block_start[:, None], axis=1), N_EXPERTS - 1).astype(jnp.int32)
    n_used = (pad_end[-1:] // FFN_ROWS).astype(jnp.int32)
    tile_start = (pad_start[None, :] + seg_end - seg).reshape(-1).astype(jnp.int32)
    n_pass = jnp.maximum(-(-jnp.max(tile_count, axis=1) // SLOT_ROWS), 1).astype(jnp.int32)
    tile_count = tile_count.reshape(-1)
    fill_start = jnp.minimum(pad_start + rows_e, n_rows - ZERO_ROWS).astype(jnp.int32)

    xs = _dispatch(tile_start, tile_count, n_pass, fill_start, n_used, idx_t, lrank_t, h2, n_rows)

    n_grp = 2 * D_EXPERT // GU_GROUP
    b_gu = b_gate_up[0].reshape(N_EXPERTS, n_grp, GU_GROUP // 2, 2).transpose(0, 1, 3, 2).reshape(N_EXPERTS, 1, 2 * D_EXPERT)
    y = _ffn(block_e, n_used, xs, w_gate_up[0], b_gu, w_down[0], b_down[0][:, None, :])

    return _combine(tile_start, tile_count, n_pass, idx_t.T, lrank_t.T, prob_t.T, y, x1, mod, final_norm_g[None, :])
```

```python
import functools

import jax
import jax.numpy as jnp
import numpy as np
from jax import lax
from jax.experimental import pallas as pl
from jax.experimental.pallas import tpu as pltpu

F32 = jnp.float32
BF16 = jnp.bfloat16

D_MODEL = 1024
EPS = 1e-6
LOG2_E = 1.4426950408889634
N_MOD = 6
GRID_W = 64
ROPE_BASE = 10000.0

HEADS = 8
HEAD_DIM = 128
MLA_ROPE = 64
MLA_QK = HEAD_DIM + MLA_ROPE
MLA_QK_PAD = 256
MLA_Q_RANK = 768
MLA_KV_RANK = 256
ROPE_HALF = MLA_ROPE // 2
QA_PIECE = 256

N_EXPERTS = 32
TOP_K = 4
D_EXPERT = 1024
SWIGLU_LIMIT = 7.0
SWIGLU_ALPHA = 1.702

HG_CHUNK = 128
FFN_ROWS = 512

COL_HQ, COL_I, COL_GO, COL_MM, COL_MH = (i * D_MODEL for i in range(5))
COL_QA = 5 * D_MODEL
COL_CKV = COL_QA + MLA_Q_RANK
COL_KR = COL_CKV + MLA_KV_RANK
PROJ16_TN = 1280
N_PROJ16 = -(-(COL_KR + 2 * MLA_ROPE) // PROJ16_TN) * PROJ16_TN
COL_FF, COL_FB = 0, D_MODEL
N_PROJ32 = 2 * D_MODEL
PROJ32_TN = 1024

VMEM_LIMIT = 56 * 1024 * 1024


def _cparams(sem, vmem=None):
    return pltpu.CompilerParams(dimension_semantics=sem, vmem_limit_bytes=vmem)


def _tile(n, pref, mult=8):
    best = None
    for t in range(mult, min(n, pref) + 1, mult):
        if n % t == 0:
            best = t
    assert best is not None, (n, pref, mult)
    return best


def _nt(a, b):
    return lax.dot_general(a, b, (((1,), (1,)), ((), ())), preferred_element_type=F32)


def _mm(a, b):
    return jnp.dot(a, b, preferred_element_type=F32)


def _split2(a):
    hi = a.astype(BF16)
    lo = (a - hi.astype(F32)).astype(BF16)
    return hi, lo


def _sigmoid(x):
    return 1.0 / (1.0 + jnp.exp(-x))


def _gate_sigmoid(x):
    return 0.5 * jnp.tanh(0.5 * x) + 0.5


def _mod_kernel(c_ref, w_ref, b_ref, o_ref):
    c = c_ref[...]
    s = c * _sigmoid(c)
    s_hi, s_lo = _split2(s)
    w_hi, w_lo = _split2(w_ref[...])
    o_ref[...] = _mm(s_hi, w_hi) + _mm(s_lo, w_hi) + _mm(s_hi, w_lo) + b_ref[...]


def _modulation(cc, w_mod, b_mod):
    r, d = cc.shape
    n = w_mod.shape[1]
    tn = _tile(n, 1536, 128)
    return pl.pallas_call(
        _mod_kernel,
        out_shape=jax.ShapeDtypeStruct((r, n), F32),
        grid=(n // tn,),
        in_specs=[pl.BlockSpec((r, d), lambda j: (0, 0)),
                  pl.BlockSpec((d, tn), lambda j: (0, j)),
                  pl.BlockSpec((1, tn), lambda j: (0, j))],
        out_specs=pl.BlockSpec((r, tn), lambda j: (0, j)),
        compiler_params=_cparams(("parallel",), VMEM_LIMIT),
        name="mod",
    )(cc, w_mod, b_mod)


INPROJ_GROUP = 2


def _inproj_kernel(x_ref, mod_ref, g_ref, w16_ref, w32_ref, o16_ref, o32_ref, hn_ref, *, n_lat, tm, nt, grp, ctx_row,
                   n16):
    j = pl.program_id(1)
    r = pl.program_id(2)
    b = pl.program_id(0) * grp + r // nt
    t = r % nt

    @pl.when(j == 0)
    def _():
        x = x_ref[0]
        ms = jnp.mean(x * x, axis=-1, keepdims=True)
        y = x * lax.rsqrt(ms + EPS) * g_ref[...]
        row = t * tm + lax.broadcasted_iota(jnp.int32, (tm, 1), 0)
        is_ctx = row >= n_lat
        m_lat = mod_ref[pl.ds(b, 1), :]
        m_ctx = mod_ref[pl.ds(ctx_row, 1), :]
        shift = jnp.where(is_ctx, m_ctx[:, 0:D_MODEL], m_lat[:, 0:D_MODEL])
        scale = jnp.where(is_ctx, m_ctx[:, D_MODEL:2 * D_MODEL], m_lat[:, D_MODEL:2 * D_MODEL])
        hn_ref[r] = (y * (1.0 + scale) + shift).astype(BF16)

    @pl.when(j < n16)
    def _():
        o16_ref[0] = _mm(hn_ref[r], w16_ref[...]).astype(BF16)

    @pl.when(j >= n16)
    def _():
        o32_ref[0] = _mm(hn_ref[r], w32_ref[...])


def _inproj(xa, mod, g, w16, w32, n_lat):
    bsz, rows, d = xa.shape
    tm = _tile(rows, 1152)
    nt = rows // tm
    grp = INPROJ_GROUP if bsz % INPROJ_GROUP == 0 else 1
    n_r = grp * nt
    n16 = N_PROJ16 // PROJ16_TN
    n32 = N_PROJ32 // PROJ32_TN

    def row_tile(g_, r):
        return g_ * grp + r // nt, r % nt

    def x_map(g_, j, r):
        b, t = row_tile(g_, jnp.where(j == 0, r, n_r - 1))
        return b, t, 0

    def o16_map(g_, j, r):
        b, t = row_tile(g_, jnp.where(j < n16, r, n_r - 1))
        return b, t, jnp.minimum(j, n16 - 1)

    def o32_map(g_, j, r):
        b, t = row_tile(g_, jnp.where(j >= n16, r, 0))
        return b, t, jnp.maximum(j - n16, 0)

    kern = functools.partial(_inproj_kernel, n_lat=n_lat, tm=tm, nt=nt, grp=grp, ctx_row=bsz, n16=n16)
    return pl.pallas_call(
        kern,
        out_shape=(jax.ShapeDtypeStruct((bsz, rows, N_PROJ16), BF16),
                   jax.ShapeDtypeStruct((bsz, rows, N_PROJ32), F32)),
        grid=(bsz // grp, n16 + n32, n_r),
        in_specs=[pl.BlockSpec((1, tm, d), x_map),
                  pl.BlockSpec(mod.shape, lambda g_, j, r: (0, 0)),
                  pl.BlockSpec((1, d), lambda g_, j, r: (0, 0)),
                  pl.BlockSpec((d, PROJ16_TN), lambda g_, j, r: (0, jnp.minimum(j, n16 - 1))),
                  pl.BlockSpec((d, PROJ32_TN), lambda g_, j, r: (0, jnp.maximum(j - n16, 0)))],
        out_specs=(pl.BlockSpec((1, tm, PROJ16_TN), o16_map),
                   pl.BlockSpec((1, tm, PROJ32_TN), o32_map)),
        scratch_shapes=[pltpu.VMEM((n_r, tm, d), BF16)],
        compiler_params=_cparams(("parallel", "arbitrary", "arbitrary"), VMEM_LIMIT),
        name="inproj",
    )(xa, mod, g, w16, w32)


def _mlaq_kernel(a0_ref, a1_ref, a2_ref, g_ref, w_ref, cos_ref, sin_ref, o_ref, *, tm, n_parts):
    n_rows = tm // n_parts
    scale = MLA_QK ** -0.5 * LOG2_E

    def part(i):
        rows = slice(i * n_rows, (i + 1) * n_rows)
        parts = [a_ref[0, rows, :].astype(F32) for a_ref in (a0_ref, a1_ref, a2_ref)]
        ss = sum(jnp.sum(p * p, axis=-1, keepdims=True) for p in parts)
        r = lax.rsqrt(ss * (1.0 / MLA_Q_RANK) + EPS)
        acc = None
        for j, p in enumerate(parts):
            hj = (p * r * g_ref[:, j * QA_PIECE:(j + 1) * QA_PIECE]).astype(BF16)
            d = _mm(hj, w_ref[j * QA_PIECE:(j + 1) * QA_PIECE, :])
            acc = d if acc is None else acc + d
        yield
        lane = lax.broadcasted_iota(jnp.int32, (n_rows, HEAD_DIM), 1)
        cos = cos_ref[rows, :]
        sin = sin_ref[rows, :]
        for h in range(HEADS):
            nope = acc[:, h * MLA_QK_PAD:h * MLA_QK_PAD + HEAD_DIM]
            rp = acc[:, h * MLA_QK_PAD + HEAD_DIM:(h + 1) * MLA_QK_PAD]
            swapped = jnp.where(lane < ROPE_HALF, pltpu.roll(rp, HEAD_DIM - ROPE_HALF, 1),
                                pltpu.roll(rp, ROPE_HALF, 1))
            rot = rp * cos + swapped * sin
            o_ref[0, h, rows, 0:HEAD_DIM] = (nope * scale).astype(BF16)
            o_ref[0, h, rows, HEAD_DIM:MLA_QK_PAD] = (rot * scale).astype(BF16)

    _round_robin([part(i) for i in range(n_parts)])


def _mla_q(p, g, w, cos_q, sin_q, n_lat):
    bsz = p.shape[0]
    tm = _tile(n_lat, 512)
    cb = COL_QA // QA_PIECE
    kern = functools.partial(_mlaq_kernel, tm=tm, n_parts=2 if tm % 32 == 0 else 1)
    return pl.pallas_call(
        kern,
        out_shape=jax.ShapeDtypeStruct((bsz, HEADS, n_lat, MLA_QK_PAD), BF16),
        grid=(bsz, n_lat // tm),
        in_specs=[pl.BlockSpec((1, tm, QA_PIECE), lambda b, t: (b, t, cb)),
                  pl.BlockSpec((1, tm, QA_PIECE), lambda b, t: (b, t, cb + 1)),
                  pl.BlockSpec((1, tm, QA_PIECE), lambda b, t: (b, t, cb + 2)),
                  pl.BlockSpec((1, MLA_Q_RANK), lambda b, t: (0, 0)),
                  pl.BlockSpec(w.shape, lambda b, t: (0, 0)),
                  pl.BlockSpec((tm, HEAD_DIM), lambda b, t: (t, 0)),
                  pl.BlockSpec((tm, HEAD_DIM), lambda b, t: (t, 0))],
        out_specs=pl.BlockSpec((1, HEADS, tm, MLA_QK_PAD), lambda b, t: (b, 0, t, 0)),
        compiler_params=_cparams(("parallel", "parallel"), VMEM_LIMIT),
        name="mla_q",
    )(p, p, p, g, w, cos_q, sin_q)


def _mlakv_kernel(c_ref, kr_ref, g_ref, w_ref, cos_ref, sin_ref, k_ref, v_ref):
    c = c_ref[0].astype(F32)
    ms = jnp.mean(c * c, axis=-1, keepdims=True)
    hn = (c * lax.rsqrt(ms + EPS) * g_ref[...]).astype(BF16)
    kv = _mm(hn, w_ref[...])
    grp = kr_ref[0].astype(F32)
    rot = (grp * cos_ref[...] + pltpu.roll(grp, MLA_ROPE, 1) * sin_ref[...]).astype(BF16)
    for h in range(HEADS):
        k_ref[0, h, :, 0:HEAD_DIM] = kv[:, h * HEAD_DIM:(h + 1) * HEAD_DIM].astype(BF16)
        k_ref[0, h, :, HEAD_DIM:MLA_QK_PAD] = rot
        v_ref[0, h] = kv[:, D_MODEL + h * HEAD_DIM:D_MODEL + (h + 1) * HEAD_DIM].astype(BF16)


def _mla_kv(p, g, w, cos_k, sin_k):
    bsz, rows, _ = p.shape
    tm = _tile(rows, 768)
    return pl.pallas_call(
        _mlakv_kernel,
        out_shape=(jax.ShapeDtypeStruct((bsz, HEADS, rows, MLA_QK_PAD), BF16),
                   jax.ShapeDtypeStruct((bsz, HEADS, rows, HEAD_DIM), BF16)),
        grid=(bsz, rows // tm),
        in_specs=[pl.BlockSpec((1, tm, MLA_KV_RANK), lambda b, t: (b, t, COL_CKV // MLA_KV_RANK)),
                  pl.BlockSpec((1, tm, 128), lambda b, t: (b, t, COL_KR // 128)),
                  pl.BlockSpec((1, MLA_KV_RANK), lambda b, t: (0, 0)),
                  pl.BlockSpec(w.shape, lambda b, t: (0, 0)),
                  pl.BlockSpec((tm, 128), lambda b, t: (t, 0)),
                  pl.BlockSpec((tm, 128), lambda b, t: (t, 0))],
        out_specs=(pl.BlockSpec((1, HEADS, tm, MLA_QK_PAD), lambda b, t: (b, 0, t, 0)),
                   pl.BlockSpec((1, HEADS, tm, HEAD_DIM), lambda b, t: (b, 0, t, 0))),
        compiler_params=_cparams(("parallel", "parallel"), VMEM_LIMIT),
        name="mla_kv",
    )(p, p, g, w, cos_k, sin_k)


def _attn_kernel(q_ref, k_ref, v_ref, o_ref, *, tq, sub):
    k = k_ref[0, 0]
    v = v_ref[0, 0]
    v1 = jnp.concatenate([v, jnp.ones_like(v)], axis=1)

    def sub_tile(r):
        rows = slice(r * sub, (r + 1) * sub)
        s = _nt(q_ref[0, 0, rows, :], k)
        yield
        m = jnp.max(s, axis=-1, keepdims=True)
        p = jnp.exp2(s - m).astype(BF16)
        yield
        ol = _mm(p, v1)
        o_ref[0, rows, :] = (ol[:, 0:HEAD_DIM] * (1.0 / ol[:, HEAD_DIM:HEAD_DIM + 1])).astype(BF16)

    _round_robin([sub_tile(r) for r in range(tq // sub)])


def _attention(q, k, v):
    bsz, _, n, _ = q.shape
    m = k.shape[2]
    tq = _tile(n, 2048)
    sub = _tile(tq, 512)
    kern = functools.partial(_attn_kernel, tq=tq, sub=sub)
    return pl.pallas_call(
        kern,
        out_shape=jax.ShapeDtypeStruct((bsz, n, HEADS * HEAD_DIM), BF16),
        grid=(bsz, HEADS, n // tq),
        in_specs=[pl.BlockSpec((1, 1, tq, MLA_QK_PAD), lambda b, h, t: (b, h, t, 0)),
                  pl.BlockSpec((1, 1, m, MLA_QK_PAD), lambda b, h, t: (b, h, 0, 0)),
                  pl.BlockSpec((1, 1, m, HEAD_DIM), lambda b, h, t: (b, h, 0, 0))],
        out_specs=pl.BlockSpec((1, tq, HEAD_DIM), lambda b, h, t: (b, t, h)),
        compiler_params=_cparams(("parallel", "parallel", "arbitrary"), VMEM_LIMIT),
        name="attention",
    )(q, k, v)


def _level_ref(cum, blk, reverse):
    c = cum.shape[0]
    half = blk // 2
    r = half if reverse else half - 1
    if blk >= 8:
        x = cum.reshape(c // blk, blk, HEAD_DIM)
        e = jnp.broadcast_to(x[:, r:r + 1, :], x.shape)
        return e.reshape(c, HEAD_DIM)
    x = cum.reshape(c // 8, 8, HEAD_DIM)
    sub = lax.broadcasted_iota(jnp.int32, x.shape, 1)
    e = None
    for jb in range(8 // blk):
        cand = jnp.broadcast_to(x[:, jb * blk + r:jb * blk + r + 1, :], x.shape)
        e = cand if e is None else jnp.where(sub >= jb * blk, cand, e)
    return e.reshape(c, HEAD_DIM)


HG_SUB = 32
HG_SUB_LEVEL = HG_SUB.bit_length() - 1
HG_MAX_EXP2 = 100.0


def _hgrn_state(z, v, lb, tri, st, *, chunk, reverse):
    f = lb + (1.0 - lb) * _sigmoid(z)
    kb = (1.0 - f).astype(BF16)
    g = jnp.log(f) * LOG2_E

    g_hi, g_lo = _split2(g)
    yield
    cum = _mm(tri, jnp.concatenate([g_hi, g_lo], axis=0))
    yield
    last = 0 if reverse else chunk - 1
    tot = cum[last:last + 1, :]
    kt = kb * jnp.exp2(tot - cum).astype(BF16)
    vt = v.astype(F32).T.astype(BF16)
    yield
    st_new = st * jnp.exp2(tot) + _mm(vt, kt)
    return kb, cum, st_new


def _sub_block_decay(cum, reverse):
    c = cum.shape[0]
    x = cum.reshape(c // HG_SUB, HG_SUB, HEAD_DIM)
    zero = jnp.zeros((1, 1, HEAD_DIM), F32)
    if reverse:
        edge = jnp.concatenate([x[1:, 0:1, :], zero], axis=0)
    else:
        edge = jnp.concatenate([zero, x[:-1, HG_SUB - 1:HG_SUB, :]], axis=0)
    return (x - edge).reshape(c, HEAD_DIM)


def _hgrn_readout(q, v, kb, cum, sub, lvl, st, *, chunk, reverse, shared):
    q = q.astype(F32)
    qb = (q * _gate_sigmoid(q) * (HEAD_DIM ** -0.5)).astype(BF16)
    yield
    if shared:
        att = jnp.where(lvl == 0, _nt(qb * jnp.exp2(sub).astype(BF16), kb * jnp.exp2(-sub).astype(BF16)).astype(BF16),
                        jnp.zeros((), BF16))
        first = HG_SUB_LEVEL + 1
    else:
        att = jnp.where(lvl == 0, _nt(qb, kb).astype(BF16), jnp.zeros((), BF16))
        first = 1
    for lv in range(first, chunk.bit_length()):
        zrel = cum - _level_ref(cum, 1 << lv, reverse)
        neg_abs = pltpu.bitcast(pltpu.bitcast(zrel, jnp.uint32) | jnp.uint32(0x80000000), F32)
        e = jnp.exp2(neg_abs).astype(BF16)
        yield
        att = jnp.where(lvl == lv, _nt(qb * e, kb * e).astype(BF16), att)
        yield
    return _nt(qb * jnp.exp2(cum).astype(BF16), st.astype(BF16)) + _mm(att, v.astype(BF16))


def _round_robin(gens):
    out = [None] * len(gens)
    active = list(range(len(gens)))
    while active:
        for i in list(active):
            try:
                next(gens[i])
            except StopIteration as e:
                out[i] = e.value
                active.remove(i)
    return out


def _hgrn_kernel(qf_ref, zf_ref, if_ref, qb_ref, zb_ref, ib_ref, lb_ref, tri_ref, lvl_ref, of_ref, ob_ref, st_ref,
                 *, chunk, group, n_ctx_chunks):
    step = pl.program_id(2)

    @pl.when(step == 0)
    def _():
        st_ref[...] = jnp.zeros_like(st_ref)

    ins = ((qf_ref, zf_ref, if_ref), (qb_ref, zb_ref, ib_ref))
    outs = (of_ref, ob_ref)
    chains = [(d, j) for j in range(group) for d in range(2)]

    def cols(j):
        return slice(j * HEAD_DIM, (j + 1) * HEAD_DIM)

    def advance(want_out):
        old = [st_ref[d, j] for d, j in chains]
        parts = _round_robin([_hgrn_state(ins[d][1][0, :, cols(j)], ins[d][2][0, :, cols(j)], lb_ref[d, j],
                                          tri_ref[d], st, chunk=chunk, reverse=bool(d))
                              for (d, j), st in zip(chains, old)])
        for (d, j), (_, _, st_new) in zip(chains, parts):
            st_ref[d, j] = st_new
        if not want_out:
            return
        subs = [_sub_block_decay(cum, bool(d)) for (d, j), (_, cum, _) in zip(chains, parts)]
        low = subs[0]
        for s in subs[1:]:
            low = jnp.minimum(low, s)
        low = jnp.min(jnp.min(low, axis=0, keepdims=True), axis=1, keepdims=True)
        in_range = low[0, 0] >= -HG_MAX_EXP2

        def readouts(shared):
            os_ = _round_robin([_hgrn_readout(ins[d][0][0, :, cols(j)], ins[d][2][0, :, cols(j)], kb, cum, sub,
                                              lvl_ref[d, int(shared)], st, chunk=chunk, reverse=bool(d), shared=shared)
                                for (d, j), (kb, cum, _), sub, st in zip(chains, parts, subs, old)])
            for (d, j), o in zip(chains, os_):
                outs[d][0, j] = o.astype(BF16)

        @pl.when(in_range)
        def _():
            readouts(True)

        @pl.when(jnp.logical_not(in_range))
        def _():
            readouts(False)

    @pl.when(step < n_ctx_chunks)
    def _():
        advance(False)

    @pl.when(step >= n_ctx_chunks)
    def _():
        advance(True)


def _hgrn_consts(chunk, reverse):
    t = np.arange(chunk)[:, None]
    s = np.arange(chunk)[None, :]
    x = t ^ s
    bitlen = np.zeros_like(x)
    for b in range(chunk.bit_length()):
        bitlen = np.where(x >> b > 0, b + 1, bitlen)
    valid = (s > t) if reverse else (t > s)
    lvl = np.where(t == s, 0, np.where(valid, bitlen, -1)).astype(np.float32)
    lvl_shared = np.where((lvl >= 0) & (lvl <= HG_SUB_LEVEL), 0, lvl)
    tri = ((s >= t) if reverse else (t >= s)).astype(np.float32)
    return np.concatenate([tri, tri], axis=1), np.stack([lvl, lvl_shared])


HG_GROUP = 8


def _hgrn_scans(p, pf, lb, n_lat):
    bsz, rows, _ = p.shape
    c = HG_CHUNK
    grp = HG_GROUP
    n_chunks = rows // c
    n_lat_c = n_lat // c
    n_ctx_c = n_chunks - n_lat_c
    consts = [_hgrn_consts(c, False), _hgrn_consts(c, True)]
    tri = jnp.asarray(np.stack([consts[0][0], consts[1][0]]), dtype=BF16)
    lvl = jnp.asarray(np.stack([consts[0][1], consts[1][1]]), dtype=BF16)

    def cidx_f(i):
        return jnp.where(i < n_ctx_c, n_lat_c + i, i - n_ctx_c)

    def cidx_b(i):
        return n_chunks - 1 - i

    def col_spec(col, cidx):
        return pl.BlockSpec((1, c, grp * HEAD_DIM), lambda b, h, i: (b, cidx(i), col // (grp * HEAD_DIM) + h))

    def out_spec(cidx):
        return pl.BlockSpec((1, grp, c, HEAD_DIM), lambda b, h, i: (b, h, cidx(jnp.maximum(i, n_ctx_c)), 0))

    o_shape = jax.ShapeDtypeStruct((bsz, HEADS, n_lat, HEAD_DIM), BF16)
    kern = functools.partial(_hgrn_kernel, chunk=c, group=grp, n_ctx_chunks=n_ctx_c)
    return pl.pallas_call(
        kern,
        out_shape=(o_shape, o_shape),
        grid=(bsz, HEADS // grp, n_chunks),
        in_specs=[col_spec(COL_HQ, cidx_f), col_spec(COL_FF, cidx_f), col_spec(COL_I, cidx_f),
                  col_spec(COL_HQ, cidx_b), col_spec(COL_FB, cidx_b), col_spec(COL_I, cidx_b),
                  pl.BlockSpec((2, grp, 1, HEAD_DIM), lambda b, h, i: (0, h, 0, 0)),
                  pl.BlockSpec((2, c, 2 * c), lambda b, h, i: (0, 0, 0)),
                  pl.BlockSpec((2, 2, c, c), lambda b, h, i: (0, 0, 0, 0))],
        out_specs=(out_spec(cidx_f), out_spec(cidx_b)),
        scratch_shapes=[pltpu.VMEM((2, grp, HEAD_DIM, HEAD_DIM), F32)],
        compiler_params=_cparams(("parallel", "parallel", "arbitrary"), VMEM_LIMIT),
        name="hgrn",
    )(p, pf, p, p, pf, p, lb, tri, lvl)


def _merge_kernel(ym_ref, of_ref, ob_ref, go_ref, gm_ref, gh_ref, x_ref, mod_ref, hgg_ref, wout_ref,
                  gffn_ref, wr2_ref, wrhi_ref, br_ref, x1_ref, h2_ref, lg_ref, y_scr, *, tm, parts):
    b = pl.program_id(0)
    m = mod_ref[pl.ds(b, 1), :]
    n_rows = tm // parts

    def part(i):
        rows = slice(i * n_rows, (i + 1) * n_rows)
        for h in range(HEADS):
            sl = slice(h * HEAD_DIM, (h + 1) * HEAD_DIM)
            o = of_ref[0, h, rows, :].astype(F32) + ob_ref[0, h, rows, :].astype(F32)
            ms = jnp.mean(o * o, axis=-1, keepdims=True)
            g = go_ref[0, rows, sl].astype(F32)
            yh = o * lax.rsqrt(ms + EPS) * hgg_ref[...] * (g * _gate_sigmoid(g))
            y = (_gate_sigmoid(gm_ref[0, rows, sl].astype(F32)) * ym_ref[0, rows, sl].astype(F32)
                 + _gate_sigmoid(gh_ref[0, rows, sl].astype(F32)) * yh)
            y_scr[rows, sl] = y.astype(BF16)
        yield
        mix = _mm(y_scr[rows, :], wout_ref[...])
        yield
        x1 = x_ref[0, rows, :] + m[:, 2 * D_MODEL:3 * D_MODEL] * mix
        x1_ref[0, rows, :] = x1
        ms = jnp.mean(x1 * x1, axis=-1, keepdims=True)
        h2 = ((x1 * lax.rsqrt(ms + EPS) * gffn_ref[...]) * (1.0 + m[:, 4 * D_MODEL:5 * D_MODEL])
              + m[:, 3 * D_MODEL:4 * D_MODEL])
        h_hi = h2.astype(BF16)
        h2_ref[rows, :] = h_hi
        h_lo = (h2 - h_hi.astype(F32)).astype(BF16)
        a = _mm(h_hi, wr2_ref[...])
        lg_ref[rows, :] = (a[:, 0:N_EXPERTS] + a[:, N_EXPERTS:2 * N_EXPERTS] + _mm(h_lo, wrhi_ref[...])
                           + br_ref[...])

    _round_robin([part(i) for i in range(parts)])


def _merge(y_mla, o_f, o_b, p, x, mod, hg_g, w_out, g_ffn, w_r, b_r):
    bsz, n, d = x.shape
    tm = _tile(n, 256, 128)
    nt = n // tm
    w_r_hi, w_r_lo = _split2(w_r)
    w_r2 = jnp.concatenate([w_r_hi, w_r_lo], axis=1)
    kern = functools.partial(_merge_kernel, tm=tm, parts=2)

    def pcol(col):
        return pl.BlockSpec((1, tm, d), lambda b, t: (b, t, col // d))

    tok = lambda b, t: (b, t, 0)
    const2 = lambda b, t: (0, 0)
    return pl.pallas_call(
        kern,
        out_shape=(jax.ShapeDtypeStruct((bsz, n, d), F32),
                   jax.ShapeDtypeStruct((bsz * n, d), BF16),
                   jax.ShapeDtypeStruct((bsz * n, N_EXPERTS), F32)),
        grid=(bsz, nt),
        in_specs=[pl.BlockSpec((1, tm, d), tok),
                  pl.BlockSpec((1, HEADS, tm, HEAD_DIM), lambda b, t: (b, 0, t, 0)),
                  pl.BlockSpec((1, HEADS, tm, HEAD_DIM), lambda b, t: (b, 0, t, 0)),
                  pcol(COL_GO), pcol(COL_MM), pcol(COL_MH),
                  pl.BlockSpec((1, tm, d), tok),
                  pl.BlockSpec(mod.shape, const2),
                  pl.BlockSpec((1, HEAD_DIM), const2),
                  pl.BlockSpec((d, d), const2),
                  pl.BlockSpec((1, d), const2),
                  pl.BlockSpec((d, 2 * N_EXPERTS), const2),
                  pl.BlockSpec((d, N_EXPERTS), const2),
                  pl.BlockSpec((1, N_EXPERTS), const2)],
        out_specs=(pl.BlockSpec((1, tm, d), tok),
                   pl.BlockSpec((tm, d), lambda b, t: (b * nt + t, 0)),
                   pl.BlockSpec((tm, N_EXPERTS), lambda b, t: (b * nt + t, 0))),
        scratch_shapes=[pltpu.VMEM((tm, d), BF16)],
        compiler_params=_cparams(("parallel", "parallel"), VMEM_LIMIT),
        name="merge",
    )(y_mla, o_f, o_b, p, p, p, x, mod, hg_g, w_out, g_ffn, w_r2, w_r_hi, b_r)


def _router_kernel(lg_ref, upper_ref, idx_ref, prob_ref, rank_ref, base_ref, cnt_ref, run_ref, *, tm, n_sub):
    @pl.when(pl.program_id(0) == 0)
    def _():
        run_ref[...] = jnp.zeros_like(run_ref)

    eidx = lax.broadcasted_iota(jnp.int32, (N_EXPERTS, tm), 0)
    run = run_ref[...]
    for s in range(n_sub):
        cols = slice(s * tm, (s + 1) * tm)
        l = lg_ref[:, cols]
        vals, sel = [], []
        for _ in range(TOP_K):
            m = jnp.max(l, axis=0, keepdims=True)
            first = jnp.min(jnp.where(l == m, eidx, N_EXPERTS), axis=0, keepdims=True)
            vals.append(m)
            sel.append(first)
            l = jnp.where(eidx == first, -jnp.inf, l)
        ex = [jnp.exp(v - vals[0]) for v in vals]
        inv = 1.0 / (ex[0] + ex[1] + ex[2] + ex[3])
        onehot = jnp.zeros((N_EXPERTS, tm), F32)
        for k in range(TOP_K):
            onehot = onehot + jnp.where(eidx == sel[k], 1.0, 0.0)
        before = _mm(onehot.astype(BF16), upper_ref[...])
        base_ref[s] = run
        for k in range(TOP_K):
            idx_ref[k:k + 1, cols] = sel[k]
            prob_ref[k:k + 1, cols] = ex[k] * inv
            rank_ref[k:k + 1, cols] = jnp.sum(jnp.where(eidx == sel[k], before, 0.0), axis=0,
                                               keepdims=True).astype(jnp.int32)
        run = run + jnp.sum(onehot, axis=1, keepdims=True)
    run_ref[...] = run
    cnt_ref[...] = run


def _router(logits_t, tm):
    _, t = logits_t.shape
    n_tiles = t // tm
    n_sub = 4 if n_tiles % 4 == 0 else 1
    step = tm * n_sub
    upper = jnp.asarray(np.triu(np.ones((tm, tm), np.float32), 1), dtype=BF16)
    kern = functools.partial(_router_kernel, tm=tm, n_sub=n_sub)
    tok = pl.BlockSpec((TOP_K, step), lambda i: (0, i))
    return pl.pallas_call(
        kern,
        out_shape=(jax.ShapeDtypeStruct((TOP_K, t), jnp.int32),
                   jax.ShapeDtypeStruct((TOP_K, t), F32),
                   jax.ShapeDtypeStruct((TOP_K, t), jnp.int32),
                   jax.ShapeDtypeStruct((n_tiles, N_EXPERTS, 128), F32),
                   jax.ShapeDtypeStruct((N_EXPERTS, 128), F32)),
        grid=(n_tiles // n_sub,),
        in_specs=[pl.BlockSpec((N_EXPERTS, step), lambda i: (0, i)),
                  pl.BlockSpec((tm, tm), lambda i: (0, 0))],
        out_specs=(tok, tok, tok, pl.BlockSpec((n_sub, N_EXPERTS, 128), lambda i: (i, 0, 0)),
                   pl.BlockSpec((N_EXPERTS, 128), lambda i: (0, 0))),
        scratch_shapes=[pltpu.VMEM((N_EXPERTS, 128), F32)],
        compiler_params=_cparams(("arbitrary",), VMEM_LIMIT),
        name="router",
    )(logits_t, upper)


ROUTE_TILE = 256
ROW_ALIGN = 8
SLOT_ROWS = 48
N_SLOTS = N_EXPERTS * SLOT_ROWS
ZERO_ROWS = FFN_ROWS + SLOT_ROWS


def _slot_ids(idx, lrank, c):
    r = lrank - c * SLOT_ROWS
    return jnp.where(jnp.logical_and(r >= 0, r < SLOT_ROWS), idx * SLOT_ROWS + r, -1)


def _pack_bf16_pairs(x):
    u = pltpu.bitcast(x, jnp.uint32)
    w = x.shape[1] // 2
    return (u[:, 0:w] >> 16) | u[:, w:2 * w]


def _unpack_bf16_pairs(u):
    lo = pltpu.bitcast(u << 16, F32).astype(BF16)
    hi = pltpu.bitcast(u & jnp.uint32(0xFFFF0000), F32).astype(BF16)
    return jnp.concatenate([lo, hi], axis=1)


def _dispatch_kernel(tstart_ref, tcnt_ref, npass_ref, fill_ref, nused_ref, idx_ref, lrank_ref, h2_ref, xs_ref,
                     xbuf, zbuf, sem, zsem, state, *, tm, n_blocks):
    j = pl.program_id(0)

    def slot_copy(slot, e, row):
        return pltpu.make_async_copy(xbuf.at[slot, pl.ds(e * SLOT_ROWS, SLOT_ROWS)],
                                     xs_ref.at[pl.ds(pl.multiple_of(row, ROW_ALIGN), SLOT_ROWS)], sem.at[slot])

    def wait_pass(slot, n):
        @pl.when(n == N_EXPERTS)
        def _():
            pltpu.make_async_copy(xbuf.at[slot], xs_ref.at[pl.ds(0, N_SLOTS)], sem.at[slot]).wait()

        @pl.when(n != N_EXPERTS)
        def _():
            def body(i, carry):
                slot_copy(slot, 0, 0).wait()
                return carry
            lax.fori_loop(0, n, body, 0)

    @pl.when(j == 0)
    def _():
        zbuf[...] = jnp.zeros_like(zbuf)
        state[0] = 0
        state[1] = 0
        for parity in range(2):
            for e in range(parity, N_EXPERTS, 2):
                pltpu.make_async_copy(zbuf, xs_ref.at[pl.ds(pl.multiple_of(fill_ref[e], ROW_ALIGN), ZERO_ROWS)],
                                      zsem).start()
            for e in range(parity, N_EXPERTS, 2):
                pltpu.make_async_copy(zbuf, xs_ref.at[pl.ds(0, ZERO_ROWS)], zsem).wait()

        def tail(i, carry):
            pltpu.make_async_copy(zbuf.at[pl.ds(0, FFN_ROWS)],
                                  xs_ref.at[pl.ds(pl.multiple_of(i * FFN_ROWS, ROW_ALIGN), FFN_ROWS)], zsem).start()
            return carry
        lax.fori_loop(nused_ref[0], n_blocks, tail, 0)

        def tail_wait(i, carry):
            pltpu.make_async_copy(zbuf.at[pl.ds(0, FFN_ROWS)], xs_ref.at[pl.ds(0, FFN_ROWS)], zsem).wait()
            return carry
        lax.fori_loop(nused_ref[0], n_blocks, tail_wait, 0)

    def one_pass(c, carry):
        rows = lax.broadcasted_iota(jnp.int32, (N_SLOTS, tm), 0)
        sel = jnp.zeros((N_SLOTS, tm), F32)
        for k in range(TOP_K):
            sel = jnp.where(rows == _slot_ids(idx_ref[k:k + 1, :], lrank_ref[k:k + 1, :], c), 1.0, sel)
        packed = _pack_bf16_pairs(_mm(sel.astype(BF16), h2_ref[...]))
        p = state[0]
        slot = p % 2
        wait_pass(1 - slot, state[1])
        xbuf[slot] = packed
        n = 0
        for e in range(N_EXPERTS):
            live = tcnt_ref[j * N_EXPERTS + e] > c * SLOT_ROWS

            @pl.when(live)
            def _():
                slot_copy(slot, e, tstart_ref[j * N_EXPERTS + e] + c * SLOT_ROWS).start(priority=e % 2)
            n = n + live.astype(jnp.int32)
        state[0] = p + 1
        state[1] = n
        return carry

    lax.fori_loop(0, npass_ref[j], one_pass, 0)

    @pl.when(j == pl.num_programs(0) - 1)
    def _():
        wait_pass((state[0] - 1) % 2, state[1])


def _dispatch(tile_start, tile_count, n_pass, fill_start, n_used, idx_t, lrank_t, h2, n_rows):
    t, d = h2.shape
    tm = ROUTE_TILE
    w = d // 2
    n_blocks = n_rows // FFN_ROWS
    kern = functools.partial(_dispatch_kernel, tm=tm, n_blocks=n_blocks)
    tok = pl.BlockSpec((TOP_K, tm), lambda i, *_: (0, i))
    grid_spec = pltpu.PrefetchScalarGridSpec(
        num_scalar_prefetch=5,
        grid=(t // tm,),
        in_specs=[tok, tok, pl.BlockSpec((tm, d), lambda i, *_: (i, 0))],
        out_specs=pl.BlockSpec(memory_space=pl.ANY),
        scratch_shapes=[pltpu.VMEM((2, N_SLOTS, w), jnp.uint32), pltpu.VMEM((ZERO_ROWS, w), jnp.uint32),
                        pltpu.SemaphoreType.DMA((2,)), pltpu.SemaphoreType.DMA, pltpu.SMEM((2,), jnp.int32)],
    )
    return pl.pallas_call(
        kern,
        out_shape=jax.ShapeDtypeStruct((n_rows, w), jnp.uint32),
        grid_spec=grid_spec,
        compiler_params=_cparams(("arbitrary",), VMEM_LIMIT),
        name="dispatch",
    )(tile_start, tile_count, n_pass, fill_start, n_used, idx_t, lrank_t, h2)


GU_GROUP = 256


def _ffn_kernel(be_ref, nu_ref, x_ref, wgu_ref, bgu_ref, wd_ref, bd_ref, perm_ref, y_ref, wgu_s, wd_s):
    i = pl.program_id(0)
    live = i < nu_ref[0]
    new_expert = jnp.logical_or(i == 0, be_ref[i] != be_ref[jnp.maximum(i - 1, 0)])

    @pl.when(jnp.logical_and(live, new_expert))
    def _():
        for g in range(2 * D_EXPERT // GU_GROUP):
            sl = slice(g * GU_GROUP, (g + 1) * GU_GROUP)
            wgu_s[:, sl] = _mm(wgu_ref[0, :, sl].astype(BF16), perm_ref[...]).astype(BF16)
        wd_s[...] = wd_ref[0].astype(BF16)

    @pl.when(live)
    def _():
        x = _unpack_bf16_pairs(x_ref[...])
        gu = _mm(x, wgu_s[...]) + bgu_ref[0]
        half = GU_GROUP // 2
        n_grp = 2 * D_EXPERT // GU_GROUP
        glu = jnp.concatenate([gu[:, g * GU_GROUP:g * GU_GROUP + half] for g in range(n_grp)], axis=1)
        lin = jnp.concatenate([gu[:, g * GU_GROUP + half:(g + 1) * GU_GROUP] for g in range(n_grp)], axis=1)
        glu = jnp.minimum(glu, SWIGLU_LIMIT)
        lin = jnp.clip(lin, -SWIGLU_LIMIT, SWIGLU_LIMIT)
        act = glu * _sigmoid(SWIGLU_ALPHA * glu) * (lin + 1.0)
        y = _mm(act.astype(BF16), wd_s[...]) + bd_ref[0]
        y_ref[...] = _pack_bf16_pairs(y.astype(BF16).astype(F32))

    @pl.when(jnp.logical_not(live))
    def _():
        y_ref[...] = jnp.zeros_like(y_ref)


def _ffn(block_e, n_used, xs, w_gu, b_gu, w_d, b_d):
    n_rows, w = xs.shape
    bm = FFN_ROWS
    nblk = n_rows // bm
    d = w * 2
    half = GU_GROUP // 2
    perm = np.zeros((GU_GROUP, GU_GROUP), np.float32)
    perm[2 * np.arange(half), np.arange(half)] = 1.0
    perm[2 * np.arange(half) + 1, half + np.arange(half)] = 1.0

    def xmap(i, be, nu):
        return (jnp.minimum(i, nu[0] - 1), 0)

    def wmap(i, be, nu):
        return (be[i], 0, 0)

    grid_spec = pltpu.PrefetchScalarGridSpec(
        num_scalar_prefetch=2,
        grid=(nblk,),
        in_specs=[pl.BlockSpec((bm, w), xmap),
                  pl.BlockSpec((1, d, 2 * D_EXPERT), wmap),
                  pl.BlockSpec((1, 1, 2 * D_EXPERT), wmap),
                  pl.BlockSpec((1, D_EXPERT, d), wmap),
                  pl.BlockSpec((1, 1, d), wmap),
                  pl.BlockSpec((GU_GROUP, GU_GROUP), lambda i, be, nu: (0, 0))],
        out_specs=pl.BlockSpec((bm, w), lambda i, be, nu: (i, 0)),
        scratch_shapes=[pltpu.VMEM((d, 2 * D_EXPERT), BF16), pltpu.VMEM((D_EXPERT, d), BF16)],
    )
    return pl.pallas_call(
        _ffn_kernel,
        out_shape=jax.ShapeDtypeStruct((n_rows, w), jnp.uint32),
        grid_spec=grid_spec,
        compiler_params=_cparams(("arbitrary",), VMEM_LIMIT),
        name="ffn",
    )(block_e, n_used, xs, w_gu, b_gu, w_d, b_d, jnp.asarray(perm, dtype=BF16))


def _combine_kernel(tstart_ref, tcnt_ref, npass_ref, idx_ref, lrank_ref, p_ref, y_ref, x1_ref, mod_ref, g_ref, o_ref,
                    ybuf, sem, *, tm, nt):
    b = pl.program_id(0)
    j = b * nt + pl.program_id(1)
    n_steps = pl.num_programs(0) * nt
    slot = j % 2

    def slot_copy(s, e, row):
        return pltpu.make_async_copy(y_ref.at[pl.ds(pl.multiple_of(row, ROW_ALIGN), SLOT_ROWS)], ybuf.at[s, pl.ds(e * SLOT_ROWS, SLOT_ROWS)],
                                     sem.at[s])

    def fetch(tile, c, s):
        for e in range(N_EXPERTS):
            @pl.when(tcnt_ref[tile * N_EXPERTS + e] > c * SLOT_ROWS)
            def _():
                slot_copy(s, e, tstart_ref[tile * N_EXPERTS + e] + c * SLOT_ROWS).start(priority=e % 2)

    def arrive(tile, c, s):
        live = [tcnt_ref[tile * N_EXPERTS + e] > c * SLOT_ROWS for e in range(N_EXPERTS)]
        n = sum(l.astype(jnp.int32) for l in live)

        @pl.when(n == N_EXPERTS)
        def _():
            pltpu.make_async_copy(y_ref.at[pl.ds(0, N_SLOTS)], ybuf.at[s], sem.at[s]).wait()

        @pl.when(n != N_EXPERTS)
        def _():
            for e in range(N_EXPERTS):
                @pl.when(live[e])
                def _():
                    slot_copy(s, e, 0).wait()

    @pl.when(j == 0)
    def _():
        ybuf[...] = jnp.zeros_like(ybuf)
        fetch(0, 0, 0)

    @pl.when(j + 1 < n_steps)
    def _():
        fetch(j + 1, 0, 1 - slot)

    def weights(c):
        cols = lax.broadcasted_iota(jnp.int32, (tm, N_SLOTS), 1)
        w = jnp.zeros((tm, N_SLOTS), F32)
        for k in range(TOP_K):
            w = jnp.where(cols == _slot_ids(idx_ref[:, k:k + 1], lrank_ref[:, k:k + 1], c), p_ref[:, k:k + 1], w)
        return w.astype(BF16)

    arrive(j, 0, slot)
    acc = _mm(weights(0), _unpack_bf16_pairs(ybuf[slot]))

    def extra_pass(c, acc):
        fetch(j, c, 2)
        arrive(j, c, 2)
        return acc + _mm(weights(c), _unpack_bf16_pairs(ybuf[2]))

    acc = lax.fori_loop(1, npass_ref[j], extra_pass, acc)
    m = mod_ref[pl.ds(b, 1), :]
    xo = x1_ref[0] + m[:, 5 * D_MODEL:6 * D_MODEL] * acc
    ms = jnp.mean(xo * xo, axis=-1, keepdims=True)
    o_ref[0] = xo * lax.rsqrt(ms + EPS) * g_ref[...]


def _combine(tile_start, tile_count, n_pass, idx_c, lrank_c, prob_c, y, x1, mod, g_fin):
    bsz, n, d = x1.shape
    tm = ROUTE_TILE
    nt = n // tm
    kern = functools.partial(_combine_kernel, tm=tm, nt=nt)
    tok = pl.BlockSpec((tm, TOP_K), lambda b, t, *_: (b * nt + t, 0))
    grid_spec = pltpu.PrefetchScalarGridSpec(
        num_scalar_prefetch=3,
        grid=(bsz, nt),
        in_specs=[tok, tok, tok,
                  pl.BlockSpec(memory_space=pl.ANY),
                  pl.BlockSpec((1, tm, d), lambda b, t, *_: (b, t, 0)),
                  pl.BlockSpec(mod.shape, lambda b, t, *_: (0, 0)),
                  pl.BlockSpec((1, d), lambda b, t, *_: (0, 0))],
        out_specs=pl.BlockSpec((1, tm, d), lambda b, t, *_: (b, t, 0)),
        scratch_shapes=[pltpu.VMEM((3, N_SLOTS, d // 2), jnp.uint32), pltpu.SemaphoreType.DMA((3,))],
    )
    return pl.pallas_call(
        kern,
        out_shape=jax.ShapeDtypeStruct((bsz, n, d), F32),
        grid_spec=grid_spec,
        compiler_params=_cparams(("arbitrary", "arbitrary"), VMEM_LIMIT),
        name="combine",
    )(tile_start, tile_count, n_pass, idx_c, lrank_c, prob_c, y, x1, mod, g_fin)


def _prep_w_in(w_in):
    q_a, kv, hq, ff, fb, hi, go, mm, mh = jnp.split(
        w_in, np.cumsum((MLA_Q_RANK, MLA_KV_RANK + MLA_ROPE) + (D_MODEL,) * 6).tolist(), axis=1)
    c_kv, kr = kv[:, :MLA_KV_RANK], kv[:, MLA_KV_RANK:]
    half = MLA_ROPE // 2
    kr_sw = jnp.concatenate([-kr[:, half:], kr[:, :half]], axis=1)
    w16 = jnp.concatenate([hq, hi, go, mm, mh, q_a, c_kv, kr, kr_sw], axis=1)
    w16 = jnp.pad(w16, ((0, 0), (0, N_PROJ16 - w16.shape[1])))
    return w16.astype(BF16), jnp.concatenate([ff, fb], axis=1).astype(BF16)


def _prep_w_q(w_q_b):
    w = w_q_b.reshape(MLA_Q_RANK, HEADS, MLA_QK)
    w = jnp.pad(w, ((0, 0), (0, 0), (0, MLA_QK_PAD - MLA_QK)))
    return w.reshape(MLA_Q_RANK, HEADS * MLA_QK_PAD).astype(BF16)


def _prep_w_kv(w_kv_b):
    w = w_kv_b.reshape(MLA_KV_RANK, HEADS, 2 * HEAD_DIM)
    k = w[:, :, :HEAD_DIM].reshape(MLA_KV_RANK, HEADS * HEAD_DIM)
    v = w[:, :, HEAD_DIM:].reshape(MLA_KV_RANK, HEADS * HEAD_DIM)
    return jnp.concatenate([k, v], axis=1).astype(BF16)


def _rope_tables(n_lat, n_ctx):
    f32 = np.float32
    rows = n_lat // GRID_W
    row = np.repeat(np.arange(rows), GRID_W).astype(f32)
    col = np.tile(np.arange(GRID_W), rows).astype(f32)
    n_freq = MLA_ROPE // 4
    inv = (f32(ROPE_BASE) ** (-np.arange(n_freq, dtype=f32) / f32(n_freq))).astype(f32)
    ang = np.concatenate([row[:, None] * inv, col[:, None] * inv], axis=-1).astype(f32)
    cos, sin = np.cos(ang).astype(f32), np.sin(ang).astype(f32)
    pad = np.zeros((n_lat, HEAD_DIM - MLA_ROPE), f32)
    cos_q = np.concatenate([cos, cos, pad], axis=1)
    sin_q = np.concatenate([-sin, sin, pad], axis=1)
    cos_k = np.concatenate([cos, cos, pad], axis=1)
    sin_k = np.concatenate([sin, sin, pad], axis=1)
    ctx_cos = np.concatenate([np.ones((n_ctx, MLA_ROPE), f32), np.zeros((n_ctx, HEAD_DIM - MLA_ROPE), f32)], axis=1)
    cos_k = np.concatenate([cos_k, ctx_cos], axis=0)
    sin_k = np.concatenate([sin_k, np.zeros((n_ctx, HEAD_DIM), f32)], axis=0)
    return jnp.asarray(cos_q), jnp.asarray(sin_q), jnp.asarray(cos_k), jnp.asarray(sin_k)


def kernel(x, c, ctx, c_ctx, w_mod, b_mod, norm_mix_g, w_in, mla_q_norm_g, w_q_b, mla_kv_norm_g, w_kv_b,
           hg_lb_logits, hg_norm_g, w_out, norm_ffn_g, w_router, b_router, w_gate_up, b_gate_up, w_down,
           b_down, final_norm_g):
    bsz, n_lat, d = x.shape
    n_ctx = ctx.shape[1]
    assert d == D_MODEL and w_mod.shape[0] == 1
    assert n_lat % HG_CHUNK == 0 and n_ctx % HG_CHUNK == 0 and n_lat % GRID_W == 0
    t_tok = bsz * n_lat

    mod_rows = -(-(bsz + 1) // 8) * 8
    cc = jnp.concatenate([c, c_ctx[None, :], jnp.zeros((mod_rows - bsz - 1, d), F32)], axis=0)
    mod = _modulation(cc, w_mod[0], b_mod[0][None, :])

    xa = jnp.concatenate([x, ctx], axis=1)
    w16, w32 = _prep_w_in(w_in[0])
    p, pf = _inproj(xa, mod, norm_mix_g[0][None, :], w16, w32, n_lat)

    cos_q, sin_q, cos_k, sin_k = _rope_tables(n_lat, n_ctx)
    q = _mla_q(p, mla_q_norm_g[0][None, :], _prep_w_q(w_q_b[0]), cos_q, sin_q, n_lat)
    k, v = _mla_kv(p, mla_kv_norm_g[0][None, :], _prep_w_kv(w_kv_b[0]), cos_k, sin_k)
    y_mla = _attention(q, k, v)

    lb = jax.nn.softmax(hg_lb_logits.astype(F32), axis=1)[:, 0, :]
    o_f, o_b = _hgrn_scans(p, pf, lb.reshape(2, HEADS, 1, HEAD_DIM), n_lat)

    x1, h2, logits = _merge(y_mla, o_f, o_b, p, x, mod, hg_norm_g[0][None, :], w_out[0].astype(BF16),
                            norm_ffn_g[0][None, :], w_router[0], b_router[0][None, :])
    logits_t = logits.T

    assert n_lat % ROUTE_TILE == 0
    idx_t, prob_t, lrank_t, base, cnt = _router(logits_t, ROUTE_TILE)

    n_tiles = t_tok // ROUTE_TILE
    counts = cnt[:, 0].astype(jnp.int32)
    base = base[:, :, 0].astype(jnp.int32)
    tile_count = jnp.concatenate([base[1:] - base[:-1], (counts - base[-1])[None, :]], axis=0)
    seg = (tile_count + ROW_ALIGN - 1) // ROW_ALIGN * ROW_ALIGN
    seg_end = jnp.cumsum(seg, axis=0)
    rows_e = seg_end[-1]
    padded = (rows_e + SLOT_ROWS + FFN_ROWS - 1) // FFN_ROWS * FFN_ROWS
    pad_end = jnp.cumsum(padded)
    pad_start = pad_end - padded
    max_rows = t_tok * TOP_K + N_EXPERTS * (n_tiles * (ROW_ALIGN - 1) + SLOT_ROWS + FFN_ROWS - 1)
    n_rows = -(-max_rows // FFN_ROWS) * FFN_ROWS
    nblk = n_rows // FFN_ROWS
    block_start = jnp.arange(nblk, dtype=jnp.int32) * FFN_ROWS
    block_e = jnp.minimum(jnp.sum(pad_end[None, :] <= block_start[:, None], axis=1), N_EXPERTS - 1).astype(jnp.int32)
    n_used = (pad_end[-1:] // FFN_ROWS).astype(jnp.int32)
    tile_start = (pad_start[None, :] + seg_end - seg).reshape(-1).astype(jnp.int32)
    n_pass = jnp.maximum(-(-jnp.max(tile_count, axis=1) // SLOT_ROWS), 1).astype(jnp.int32)
    tile_count = tile_count.reshape(-1)
    fill_start = jnp.minimum(pad_start + rows_e, n_rows - ZERO_ROWS).astype(jnp.int32)

    xs = _dispatch(tile_start, tile_count, n_pass, fill_start, n_used, idx_t, lrank_t, h2, n_rows)

    n_grp = 2 * D_EXPERT // GU_GROUP
    b_gu = b_gate_up[0].reshape(N_EXPERTS, n_grp, GU_GROUP // 2, 2).transpose(0, 1, 3, 2).reshape(N_EXPERTS, 1, 2 * D_EXPERT)
    y = _ffn(block_e, n_used, xs, w_gate_up[0], b_gu, w_down[0], b_down[0][:, None, :])

    return _combine(tile_start, tile_count, n_pass, idx_t.T, lrank_t.T, prob_t.T, y, x1, mod, final_norm_g[None, :])
```

```python
import functools

import jax
import jax.numpy as jnp
import numpy as np
from jax import lax
from jax.experimental import pallas as pl
from jax.experimental.pallas import tpu as pltpu

F32 = jnp.float32
BF16 = jnp.bfloat16

D_MODEL = 1024
EPS = 1e-6
LOG2_E = 1.4426950408889634
N_MOD = 6
GRID_W = 64
ROPE_BASE = 10000.0

HEADS = 8
HEAD_DIM = 128
MLA_ROPE = 64
MLA_QK = HEAD_DIM + MLA_ROPE
MLA_QK_PAD = 256
MLA_Q_RANK = 768
MLA_KV_RANK = 256
ROPE_HALF = MLA_ROPE // 2
QA_PIECE = 256

N_EXPERTS = 32
TOP_K = 4
D_EXPERT = 1024
SWIGLU_LIMIT = 7.0
SWIGLU_ALPHA = 1.702

HG_CHUNK = 128
FFN_ROWS = 512

COL_HQ, COL_I, COL_GO, COL_MM, COL_MH = (i * D_MODEL for i in range(5))
COL_QA = 5 * D_MODEL
COL_CKV = COL_QA + MLA_Q_RANK
COL_KR = COL_CKV + MLA_KV_RANK
PROJ16_TN = 1280
N_PROJ16 = -(-(COL_KR + 2 * MLA_ROPE) // PROJ16_TN) * PROJ16_TN
COL_FF, COL_FB = 0, D_MODEL
N_PROJ32 = 2 * D_MODEL
PROJ32_TN = 1024

VMEM_LIMIT = 56 * 1024 * 1024


def _cparams(sem, vmem=None):
    return pltpu.CompilerParams(dimension_semantics=sem, vmem_limit_bytes=vmem)


def _tile(n, pref, mult=8):
    best = None
    for t in range(mult, min(n, pref) + 1, mult):
        if n % t == 0:
            best = t
    assert best is not None, (n, pref, mult)
    return best


def _nt(a, b):
    return lax.dot_general(a, b, (((1,), (1,)), ((), ())), preferred_element_type=F32)


def _mm(a, b):
    return jnp.dot(a, b, preferred_element_type=F32)


def _split2(a):
    hi = a.astype(BF16)
    lo = (a - hi.astype(F32)).astype(BF16)
    return hi, lo


def _sigmoid(x):
    return 1.0 / (1.0 + jnp.exp(-x))


def _gate_sigmoid(x):
    return 0.5 * jnp.tanh(0.5 * x) + 0.5


def _mod_kernel(c_ref, w_ref, b_ref, o_ref):
    c = c_ref[...]
    s = c * _sigmoid(c)
    s_hi, s_lo = _split2(s)
    w_hi, w_lo = _split2(w_ref[...])
    o_ref[...] = _mm(s_hi, w_hi) + _mm(s_lo, w_hi) + _mm(s_hi, w_lo) + b_ref[...]


def _modulation(cc, w_mod, b_mod):
    r, d = cc.shape
    n = w_mod.shape[1]
    tn = _tile(n, 1536, 128)
    return pl.pallas_call(
        _mod_kernel,
        out_shape=jax.ShapeDtypeStruct((r, n), F32),
        grid=(n // tn,),
        in_specs=[pl.BlockSpec((r, d), lambda j: (0, 0)),
                  pl.BlockSpec((d, tn), lambda j: (0, j)),
                  pl.BlockSpec((1, tn), lambda j: (0, j))],
        out_specs=pl.BlockSpec((r, tn), lambda j: (0, j)),
        compiler_params=_cparams(("parallel",), VMEM_LIMIT),
        name="mod",
    )(cc, w_mod, b_mod)


INPROJ_GROUP = 2


def _inproj_kernel(x_ref, c_ref, mod_ref, g_ref, w16_ref, w32_ref, o16_ref, o32_ref, hn_ref, *, n_lat, tm, nt, grp,
                   ctx_row, n16):
    j = pl.program_id(1)
    r = pl.program_id(2)
    b = pl.program_id(0) * grp + r // nt
    t = r % nt

    @pl.when(j == 0)
    def _():
        n_lat_last = n_lat - (nt - 1) * tm
        row = t * tm + lax.broadcasted_iota(jnp.int32, (tm, 1), 0)
        is_ctx = row >= n_lat
        ctx_rows = jnp.concatenate([jnp.zeros((n_lat_last, D_MODEL), F32), c_ref[0]], axis=0)
        x = jnp.where(is_ctx, ctx_rows, x_ref[0])
        ms = jnp.mean(x * x, axis=-1, keepdims=True)
        y = x * lax.rsqrt(ms + EPS) * g_ref[...]
        m_lat = mod_ref[pl.ds(b, 1), :]
        m_ctx = mod_ref[pl.ds(ctx_row, 1), :]
        shift = jnp.where(is_ctx, m_ctx[:, 0:D_MODEL], m_lat[:, 0:D_MODEL])
        scale = jnp.where(is_ctx, m_ctx[:, D_MODEL:2 * D_MODEL], m_lat[:, D_MODEL:2 * D_MODEL])
        hn_ref[r] = (y * (1.0 + scale) + shift).astype(BF16)

    @pl.when(j < n16)
    def _():
        o16_ref[0] = _mm(hn_ref[r], w16_ref[...]).astype(BF16)

    @pl.when(j >= n16)
    def _():
        o32_ref[0] = _mm(hn_ref[r], w32_ref[...])


def _inproj(x, ctx, mod, g, w16, w32):
    bsz, n_lat, d = x.shape
    n_ctx = ctx.shape[1]
    rows = n_lat + n_ctx
    tm = _tile(rows, 1152)
    nt = rows // tm
    assert n_ctx <= tm <= n_lat, (n_ctx, tm, n_lat)
    grp = INPROJ_GROUP if bsz % INPROJ_GROUP == 0 else 1
    n_r = grp * nt
    n16 = N_PROJ16 // PROJ16_TN
    n32 = N_PROJ32 // PROJ32_TN

    def row_tile(g_, r):
        return g_ * grp + r // nt, r % nt

    def x_map(g_, j, r):
        b, t = row_tile(g_, jnp.where(j == 0, r, n_r - 1))
        return b, t, 0

    def o16_map(g_, j, r):
        b, t = row_tile(g_, jnp.where(j < n16, r, n_r - 1))
        return b, t, jnp.minimum(j, n16 - 1)

    def o32_map(g_, j, r):
        b, t = row_tile(g_, jnp.where(j >= n16, r, 0))
        return b, t, jnp.maximum(j - n16, 0)

    kern = functools.partial(_inproj_kernel, n_lat=n_lat, tm=tm, nt=nt, grp=grp, ctx_row=bsz, n16=n16)
    return pl.pallas_call(
        kern,
        out_shape=(jax.ShapeDtypeStruct((bsz, rows, N_PROJ16), BF16),
                   jax.ShapeDtypeStruct((bsz, rows, N_PROJ32), F32)),
        grid=(bsz // grp, n16 + n32, n_r),
        in_specs=[pl.BlockSpec((1, tm, d), x_map),
                  pl.BlockSpec((1, n_ctx, d), lambda g_, j, r: (x_map(g_, j, r)[0], 0, 0)),
                  pl.BlockSpec(mod.shape, lambda g_, j, r: (0, 0)),
                  pl.BlockSpec((1, d), lambda g_, j, r: (0, 0)),
                  pl.BlockSpec((d, PROJ16_TN), lambda g_, j, r: (0, jnp.minimum(j, n16 - 1))),
                  pl.BlockSpec((d, PROJ32_TN), lambda g_, j, r: (0, jnp.maximum(j - n16, 0)))],
        out_specs=(pl.BlockSpec((1, tm, PROJ16_TN), o16_map),
                   pl.BlockSpec((1, tm, PROJ32_TN), o32_map)),
        scratch_shapes=[pltpu.VMEM((n_r, tm, d), BF16)],
        compiler_params=_cparams(("parallel", "arbitrary", "arbitrary"), VMEM_LIMIT),
        name="inproj",
    )(x, ctx, mod, g, w16, w32)


def _mlaq_kernel(a0_ref, a1_ref, a2_ref, g_ref, w_ref, cos_ref, sin_ref, o_ref, *, tm, n_parts):
    n_rows = tm // n_parts
    scale = MLA_QK ** -0.5 * LOG2_E

    def part(i):
        rows = slice(i * n_rows, (i + 1) * n_rows)
        parts = [a_ref[0, rows, :].astype(F32) for a_ref in (a0_ref, a1_ref, a2_ref)]
        ss = sum(jnp.sum(p * p, axis=-1, keepdims=True) for p in parts)
        r = lax.rsqrt(ss * (1.0 / MLA_Q_RANK) + EPS)
        acc = None
        for j, p in enumerate(parts):
            hj = (p * r * g_ref[:, j * QA_PIECE:(j + 1) * QA_PIECE]).astype(BF16)
            d = _mm(hj, w_ref[j * QA_PIECE:(j + 1) * QA_PIECE, :])
            acc = d if acc is None else acc + d
        yield
        lane = lax.broadcasted_iota(jnp.int32, (n_rows, HEAD_DIM), 1)
        cos = cos_ref[rows, :]
        sin = sin_ref[rows, :]
        for h in range(HEADS):
            nope = acc[:, h * MLA_QK_PAD:h * MLA_QK_PAD + HEAD_DIM]
            rp = acc[:, h * MLA_QK_PAD + HEAD_DIM:(h + 1) * MLA_QK_PAD]
            swapped = jnp.where(lane < ROPE_HALF, pltpu.roll(rp, HEAD_DIM - ROPE_HALF, 1),
                                pltpu.roll(rp, ROPE_HALF, 1))
            rot = rp * cos + swapped * sin
            o_ref[0, h, rows, 0:HEAD_DIM] = (nope * scale).astype(BF16)
            o_ref[0, h, rows, HEAD_DIM:MLA_QK_PAD] = (rot * scale).astype(BF16)

    _round_robin([part(i) for i in range(n_parts)])


def _mla_q(p, g, w, cos_q, sin_q, n_lat):
    bsz = p.shape[0]
    tm = _tile(n_lat, 512)
    cb = COL_QA // QA_PIECE
    kern = functools.partial(_mlaq_kernel, tm=tm, n_parts=2 if tm % 32 == 0 else 1)
    return pl.pallas_call(
        kern,
        out_shape=jax.ShapeDtypeStruct((bsz, HEADS, n_lat, MLA_QK_PAD), BF16),
        grid=(bsz, n_lat // tm),
        in_specs=[pl.BlockSpec((1, tm, QA_PIECE), lambda b, t: (b, t, cb)),
                  pl.BlockSpec((1, tm, QA_PIECE), lambda b, t: (b, t, cb + 1)),
                  pl.BlockSpec((1, tm, QA_PIECE), lambda b, t: (b, t, cb + 2)),
                  pl.BlockSpec((1, MLA_Q_RANK), lambda b, t: (0, 0)),
                  pl.BlockSpec(w.shape, lambda b, t: (0, 0)),
                  pl.BlockSpec((tm, HEAD_DIM), lambda b, t: (t, 0)),
                  pl.BlockSpec((tm, HEAD_DIM), lambda b, t: (t, 0))],
        out_specs=pl.BlockSpec((1, HEADS, tm, MLA_QK_PAD), lambda b, t: (b, 0, t, 0)),
        compiler_params=_cparams(("parallel", "parallel"), VMEM_LIMIT),
        name="mla_q",
    )(p, p, p, g, w, cos_q, sin_q)


def _mlakv_kernel(c_ref, kr_ref, g_ref, w_ref, cos_ref, sin_ref, k_ref, v_ref):
    c = c_ref[0].astype(F32)
    ms = jnp.mean(c * c, axis=-1, keepdims=True)
    hn = (c * lax.rsqrt(ms + EPS) * g_ref[...]).astype(BF16)
    kv = _mm(hn, w_ref[...])
    grp = kr_ref[0].astype(F32)
    rot = (grp * cos_ref[...] + pltpu.roll(grp, MLA_ROPE, 1) * sin_ref[...]).astype(BF16)
    for h in range(HEADS):
        k_ref[0, h, :, 0:HEAD_DIM] = kv[:, h * HEAD_DIM:(h + 1) * HEAD_DIM].astype(BF16)
        k_ref[0, h, :, HEAD_DIM:MLA_QK_PAD] = rot
        v_ref[0, h] = kv[:, D_MODEL + h * HEAD_DIM:D_MODEL + (h + 1) * HEAD_DIM].astype(BF16)


def _mla_kv(p, g, w, cos_k, sin_k):
    bsz, rows, _ = p.shape
    tm = _tile(rows, 768)
    return pl.pallas_call(
        _mlakv_kernel,
        out_shape=(jax.ShapeDtypeStruct((bsz, HEADS, rows, MLA_QK_PAD), BF16),
                   jax.ShapeDtypeStruct((bsz, HEADS, rows, HEAD_DIM), BF16)),
        grid=(bsz, rows // tm),
        in_specs=[pl.BlockSpec((1, tm, MLA_KV_RANK), lambda b, t: (b, t, COL_CKV // MLA_KV_RANK)),
                  pl.BlockSpec((1, tm, 128), lambda b, t: (b, t, COL_KR // 128)),
                  pl.BlockSpec((1, MLA_KV_RANK), lambda b, t: (0, 0)),
                  pl.BlockSpec(w.shape, lambda b, t: (0, 0)),
                  pl.BlockSpec((tm, 128), lambda b, t: (t, 0)),
                  pl.BlockSpec((tm, 128), lambda b, t: (t, 0))],
        out_specs=(pl.BlockSpec((1, HEADS, tm, MLA_QK_PAD), lambda b, t: (b, 0, t, 0)),
                   pl.BlockSpec((1, HEADS, tm, HEAD_DIM), lambda b, t: (b, 0, t, 0))),
        compiler_params=_cparams(("parallel", "parallel"), VMEM_LIMIT),
        name="mla_kv",
    )(p, p, g, w, cos_k, sin_k)


def _attn_kernel(q_ref, k_ref, v_ref, o_ref, *, tq, sub):
    k = k_ref[0, 0]
    v = v_ref[0, 0]
    v1 = jnp.concatenate([v, jnp.ones_like(v)], axis=1)

    def sub_tile(r):
        rows = slice(r * sub, (r + 1) * sub)
        s = _nt(q_ref[0, 0, rows, :], k)
        yield
        m = jnp.max(s, axis=-1, keepdims=True)
        p = jnp.exp2(s - m).astype(BF16)
        yield
        ol = _mm(p, v1)
        o_ref[0, rows, :] = (ol[:, 0:HEAD_DIM] * (1.0 / ol[:, HEAD_DIM:HEAD_DIM + 1])).astype(BF16)

    _round_robin([sub_tile(r) for r in range(tq // sub)])


def _attention(q, k, v):
    bsz, _, n, _ = q.shape
    m = k.shape[2]
    tq = _tile(n, 2048)
    sub = _tile(tq, 512)
    kern = functools.partial(_attn_kernel, tq=tq, sub=sub)
    return pl.pallas_call(
        kern,
        out_shape=jax.ShapeDtypeStruct((bsz, n, HEADS * HEAD_DIM), BF16),
        grid=(bsz, HEADS, n // tq),
        in_specs=[pl.BlockSpec((1, 1, tq, MLA_QK_PAD), lambda b, h, t: (b, h, t, 0)),
                  pl.BlockSpec((1, 1, m, MLA_QK_PAD), lambda b, h, t: (b, h, 0, 0)),
                  pl.BlockSpec((1, 1, m, HEAD_DIM), lambda b, h, t: (b, h, 0, 0))],
        out_specs=pl.BlockSpec((1, tq, HEAD_DIM), lambda b, h, t: (b, t, h)),
        compiler_params=_cparams(("parallel", "parallel", "arbitrary"), VMEM_LIMIT),
        name="attention",
    )(q, k, v)


def _level_ref(cum, blk, reverse):
    c = cum.shape[0]
    half = blk // 2
    r = half if reverse else half - 1
    if blk >= 8:
        x = cum.reshape(c // blk, blk, HEAD_DIM)
        e = jnp.broadcast_to(x[:, r:r + 1, :], x.shape)
        return e.reshape(c, HEAD_DIM)
    x = cum.reshape(c // 8, 8, HEAD_DIM)
    sub = lax.broadcasted_iota(jnp.int32, x.shape, 1)
    e = None
    for jb in range(8 // blk):
        cand = jnp.broadcast_to(x[:, jb * blk + r:jb * blk + r + 1, :], x.shape)
        e = cand if e is None else jnp.where(sub >= jb * blk, cand, e)
    return e.reshape(c, HEAD_DIM)


HG_SUB = 32
HG_SUB_LEVEL = HG_SUB.bit_length() - 1
HG_MAX_EXP2 = 100.0


def _hgrn_state(z, v, lb, tri, st, *, chunk, reverse):
    f = lb + (1.0 - lb) * _sigmoid(z)
    kb = (1.0 - f).astype(BF16)
    g = jnp.log(f) * LOG2_E

    g_hi, g_lo = _split2(g)
    yield
    cum = _mm(tri, jnp.concatenate([g_hi, g_lo], axis=0))
    yield
    last = 0 if reverse else chunk - 1
    tot = cum[last:last + 1, :]
    kt = kb * jnp.exp2(tot - cum).astype(BF16)
    vt = v.astype(F32).T.astype(BF16)
    yield
    st_new = st * jnp.exp2(tot) + _mm(vt, kt)
    return kb, cum, st_new


def _sub_block_decay(cum, reverse):
    c = cum.shape[0]
    x = cum.reshape(c // HG_SUB, HG_SUB, HEAD_DIM)
    zero = jnp.zeros((1, 1, HEAD_DIM), F32)
    if reverse:
        edge = jnp.concatenate([x[1:, 0:1, :], zero], axis=0)
    else:
        edge = jnp.concatenate([zero, x[:-1, HG_SUB - 1:HG_SUB, :]], axis=0)
    return (x - edge).reshape(c, HEAD_DIM)


def _hgrn_readout(q, v, kb, cum, sub, lvl, st, *, chunk, reverse, shared):
    q = q.astype(F32)
    qb = (q * _gate_sigmoid(q) * (HEAD_DIM ** -0.5)).astype(BF16)
    yield
    if shared:
        att = jnp.where(lvl == 0, _nt(qb * jnp.exp2(sub).astype(BF16), kb * jnp.exp2(-sub).astype(BF16)).astype(BF16),
                        jnp.zeros((), BF16))
        first = HG_SUB_LEVEL + 1
    else:
        att = jnp.where(lvl == 0, _nt(qb, kb).astype(BF16), jnp.zeros((), BF16))
        first = 1
    for lv in range(first, chunk.bit_length()):
        zrel = cum - _level_ref(cum, 1 << lv, reverse)
        neg_abs = pltpu.bitcast(pltpu.bitcast(zrel, jnp.uint32) | jnp.uint32(0x80000000), F32)
        e = jnp.exp2(neg_abs).astype(BF16)
        yield
        att = jnp.where(lvl == lv, _nt(qb * e, kb * e).astype(BF16), att)
        yield
    return _nt(qb * jnp.exp2(cum).astype(BF16), st.astype(BF16)) + _mm(att, v.astype(BF16))


def _round_robin(gens):
    out = [None] * len(gens)
    active = list(range(len(gens)))
    while active:
        for i in list(active):
            try:
                next(gens[i])
            except StopIteration as e:
                out[i] = e.value
                active.remove(i)
    return out


def _hgrn_kernel(qf_ref, zf_ref, if_ref, qb_ref, zb_ref, ib_ref, lb_ref, tri_ref, lvl_ref, of_ref, ob_ref, st_ref,
                 *, chunk, group, n_ctx_chunks):
    step = pl.program_id(2)

    @pl.when(step == 0)
    def _():
        st_ref[...] = jnp.zeros_like(st_ref)

    ins = ((qf_ref, zf_ref, if_ref), (qb_ref, zb_ref, ib_ref))
    outs = (of_ref, ob_ref)
    chains = [(d, j) for j in range(group) for d in range(2)]

    def cols(j):
        return slice(j * HEAD_DIM, (j + 1) * HEAD_DIM)

    def advance(want_out):
        old = [st_ref[d, j] for d, j in chains]
        parts = _round_robin([_hgrn_state(ins[d][1][0, :, cols(j)], ins[d][2][0, :, cols(j)], lb_ref[d, j],
                                          tri_ref[d], st, chunk=chunk, reverse=bool(d))
                              for (d, j), st in zip(chains, old)])
        for (d, j), (_, _, st_new) in zip(chains, parts):
            st_ref[d, j] = st_new
        if not want_out:
            return
        subs = [_sub_block_decay(cum, bool(d)) for (d, j), (_, cum, _) in zip(chains, parts)]
        low = subs[0]
        for s in subs[1:]:
            low = jnp.minimum(low, s)
        low = jnp.min(jnp.min(low, axis=0, keepdims=True), axis=1, keepdims=True)
        in_range = low[0, 0] >= -HG_MAX_EXP2

        def readouts(shared):
            os_ = _round_robin([_hgrn_readout(ins[d][0][0, :, cols(j)], ins[d][2][0, :, cols(j)], kb, cum, sub,
                                              lvl_ref[d, int(shared)], st, chunk=chunk, reverse=bool(d), shared=shared)
                                for (d, j), (kb, cum, _), sub, st in zip(chains, parts, subs, old)])
            for (d, j), o in zip(chains, os_):
                outs[d][0, j] = o.astype(BF16)

        @pl.when(in_range)
        def _():
            readouts(True)

        @pl.when(jnp.logical_not(in_range))
        def _():
            readouts(False)

    @pl.when(step < n_ctx_chunks)
    def _():
        advance(False)

    @pl.when(step >= n_ctx_chunks)
    def _():
        advance(True)


def _hgrn_consts(chunk, reverse):
    t = np.arange(chunk)[:, None]
    s = np.arange(chunk)[None, :]
    x = t ^ s
    bitlen = np.zeros_like(x)
    for b in range(chunk.bit_length()):
        bitlen = np.where(x >> b > 0, b + 1, bitlen)
    valid = (s > t) if reverse else (t > s)
    lvl = np.where(t == s, 0, np.where(valid, bitlen, -1)).astype(np.float32)
    lvl_shared = np.where((lvl >= 0) & (lvl <= HG_SUB_LEVEL), 0, lvl)
    tri = ((s >= t) if reverse else (t >= s)).astype(np.float32)
    return np.concatenate([tri, tri], axis=1), np.stack([lvl, lvl_shared])


HG_GROUP = 8


def _hgrn_scans(p, pf, lb, n_lat):
    bsz, rows, _ = p.shape
    c = HG_CHUNK
    grp = HG_GROUP
    n_chunks = rows // c
    n_lat_c = n_lat // c
    n_ctx_c = n_chunks - n_lat_c
    consts = [_hgrn_consts(c, False), _hgrn_consts(c, True)]
    tri = jnp.asarray(np.stack([consts[0][0], consts[1][0]]), dtype=BF16)
    lvl = jnp.asarray(np.stack([consts[0][1], consts[1][1]]), dtype=BF16)

    def cidx_f(i):
        return jnp.where(i < n_ctx_c, n_lat_c + i, i - n_ctx_c)

    def cidx_b(i):
        return n_chunks - 1 - i

    def col_spec(col, cidx):
        return pl.BlockSpec((1, c, grp * HEAD_DIM), lambda b, h, i: (b, cidx(i), col // (grp * HEAD_DIM) + h))

    def out_spec(cidx):
        return pl.BlockSpec((1, grp, c, HEAD_DIM), lambda b, h, i: (b, h, cidx(jnp.maximum(i, n_ctx_c)), 0))

    o_shape = jax.ShapeDtypeStruct((bsz, HEADS, n_lat, HEAD_DIM), BF16)
    kern = functools.partial(_hgrn_kernel, chunk=c, group=grp, n_ctx_chunks=n_ctx_c)
    return pl.pallas_call(
        kern,
        out_shape=(o_shape, o_shape),
        grid=(bsz, HEADS // grp, n_chunks),
        in_specs=[col_spec(COL_HQ, cidx_f), col_spec(COL_FF, cidx_f), col_spec(COL_I, cidx_f),
                  col_spec(COL_HQ, cidx_b), col_spec(COL_FB, cidx_b), col_spec(COL_I, cidx_b),
                  pl.BlockSpec((2, grp, 1, HEAD_DIM), lambda b, h, i: (0, h, 0, 0)),
                  pl.BlockSpec((2, c, 2 * c), lambda b, h, i: (0, 0, 0)),
                  pl.BlockSpec((2, 2, c, c), lambda b, h, i: (0, 0, 0, 0))],
        out_specs=(out_spec(cidx_f), out_spec(cidx_b)),
        scratch_shapes=[pltpu.VMEM((2, grp, HEAD_DIM, HEAD_DIM), F32)],
        compiler_params=_cparams(("parallel", "parallel", "arbitrary"), VMEM_LIMIT),
        name="hgrn",
    )(p, pf, p, p, pf, p, lb, tri, lvl)


def _merge_kernel(ym_ref, of_ref, ob_ref, go_ref, gm_ref, gh_ref, x_ref, mod_ref, hgg_ref, wout_ref,
                  gffn_ref, wr2_ref, wrhi_ref, br_ref, x1_ref, h2_ref, lg_ref, y_scr, *, tm, parts):
    b = pl.program_id(0)
    m = mod_ref[pl.ds(b, 1), :]
    n_rows = tm // parts

    def part(i):
        rows = slice(i * n_rows, (i + 1) * n_rows)
        for h in range(HEADS):
            sl = slice(h * HEAD_DIM, (h + 1) * HEAD_DIM)
            o = of_ref[0, h, rows, :].astype(F32) + ob_ref[0, h, rows, :].astype(F32)
            ms = jnp.mean(o * o, axis=-1, keepdims=True)
            g = go_ref[0, rows, sl].astype(F32)
            yh = o * lax.rsqrt(ms + EPS) * hgg_ref[...] * (g * _gate_sigmoid(g))
            y = (_gate_sigmoid(gm_ref[0, rows, sl].astype(F32)) * ym_ref[0, rows, sl].astype(F32)
                 + _gate_sigmoid(gh_ref[0, rows, sl].astype(F32)) * yh)
            y_scr[rows, sl] = y.astype(BF16)
        yield
        mix = _mm(y_scr[rows, :], wout_ref[...])
        yield
        x1 = x_ref[0, rows, :] + m[:, 2 * D_MODEL:3 * D_MODEL] * mix
        x1_ref[0, rows, :] = x1
        ms = jnp.mean(x1 * x1, axis=-1, keepdims=True)
        h2 = ((x1 * lax.rsqrt(ms + EPS) * gffn_ref[...]) * (1.0 + m[:, 4 * D_MODEL:5 * D_MODEL])
              + m[:, 3 * D_MODEL:4 * D_MODEL])
        h_hi = h2.astype(BF16)
        h2_ref[rows, :] = h_hi
        h_lo = (h2 - h_hi.astype(F32)).astype(BF16)
        a = _mm(h_hi, wr2_ref[...])
        lg_ref[rows, :] = (a[:, 0:N_EXPERTS] + a[:, N_EXPERTS:2 * N_EXPERTS] + _mm(h_lo, wrhi_ref[...])
                           + br_ref[...])

    _round_robin([part(i) for i in range(parts)])


def _merge(y_mla, o_f, o_b, p, x, mod, hg_g, w_out, g_ffn, w_r, b_r):
    bsz, n, d = x.shape
    tm = _tile(n, 256, 128)
    nt = n // tm
    w_r_hi, w_r_lo = _split2(w_r)
    w_r2 = jnp.concatenate([w_r_hi, w_r_lo], axis=1)
    kern = functools.partial(_merge_kernel, tm=tm, parts=2)

    def pcol(col):
        return pl.BlockSpec((1, tm, d), lambda b, t: (b, t, col // d))

    tok = lambda b, t: (b, t, 0)
    const2 = lambda b, t: (0, 0)
    return pl.pallas_call(
        kern,
        out_shape=(jax.ShapeDtypeStruct((bsz, n, d), F32),
                   jax.ShapeDtypeStruct((bsz * n, d), BF16),
                   jax.ShapeDtypeStruct((bsz * n, N_EXPERTS), F32)),
        grid=(bsz, nt),
        in_specs=[pl.BlockSpec((1, tm, d), tok),
                  pl.BlockSpec((1, HEADS, tm, HEAD_DIM), lambda b, t: (b, 0, t, 0)),
                  pl.BlockSpec((1, HEADS, tm, HEAD_DIM), lambda b, t: (b, 0, t, 0)),
                  pcol(COL_GO), pcol(COL_MM), pcol(COL_MH),
                  pl.BlockSpec((1, tm, d), tok),
                  pl.BlockSpec(mod.shape, const2),
                  pl.BlockSpec((1, HEAD_DIM), const2),
                  pl.BlockSpec((d, d), const2),
                  pl.BlockSpec((1, d), const2),
                  pl.BlockSpec((d, 2 * N_EXPERTS), const2),
                  pl.BlockSpec((d, N_EXPERTS), const2),
                  pl.BlockSpec((1, N_EXPERTS), const2)],
        out_specs=(pl.BlockSpec((1, tm, d), tok),
                   pl.BlockSpec((tm, d), lambda b, t: (b * nt + t, 0)),
                   pl.BlockSpec((tm, N_EXPERTS), lambda b, t: (b * nt + t, 0))),
        scratch_shapes=[pltpu.VMEM((tm, d), BF16)],
        compiler_params=_cparams(("parallel", "parallel"), VMEM_LIMIT),
        name="merge",
    )(y_mla, o_f, o_b, p, p, p, x, mod, hg_g, w_out, g_ffn, w_r2, w_r_hi, b_r)


def _router_kernel(lg_ref, upper_ref, idx_ref, prob_ref, rank_ref, base_ref, cnt_ref, run_ref, *, tm, n_sub):
    @pl.when(pl.program_id(0) == 0)
    def _():
        run_ref[...] = jnp.zeros_like(run_ref)

    eidx = lax.broadcasted_iota(jnp.int32, (N_EXPERTS, tm), 0)
    run = run_ref[...]
    for s in range(n_sub):
        cols = slice(s * tm, (s + 1) * tm)
        l = lg_ref[:, cols]
        vals, sel = [], []
        for _ in range(TOP_K):
            m = jnp.max(l, axis=0, keepdims=True)
            first = jnp.min(jnp.where(l == m, eidx, N_EXPERTS), axis=0, keepdims=True)
            vals.append(m)
            sel.append(first)
            l = jnp.where(eidx == first, -jnp.inf, l)
        ex = [jnp.exp(v - vals[0]) for v in vals]
        inv = 1.0 / (ex[0] + ex[1] + ex[2] + ex[3])
        onehot = jnp.zeros((N_EXPERTS, tm), F32)
        for k in range(TOP_K):
            onehot = onehot + jnp.where(eidx == sel[k], 1.0, 0.0)
        before = _mm(onehot.astype(BF16), upper_ref[...])
        base_ref[s] = run
        for k in range(TOP_K):
            idx_ref[k:k + 1, cols] = sel[k]
            prob_ref[k:k + 1, cols] = ex[k] * inv
            rank_ref[k:k + 1, cols] = jnp.sum(jnp.where(eidx == sel[k], before, 0.0), axis=0,
                                               keepdims=True).astype(jnp.int32)
        run = run + jnp.sum(onehot, axis=1, keepdims=True)
    run_ref[...] = run
    cnt_ref[...] = run


def _router(logits_t, tm):
    _, t = logits_t.shape
    n_tiles = t // tm
    n_sub = 4 if n_tiles % 4 == 0 else 1
    step = tm * n_sub
    upper = jnp.asarray(np.triu(np.ones((tm, tm), np.float32), 1), dtype=BF16)
    kern = functools.partial(_router_kernel, tm=tm, n_sub=n_sub)
    tok = pl.BlockSpec((TOP_K, step), lambda i: (0, i))
    return pl.pallas_call(
        kern,
        out_shape=(jax.ShapeDtypeStruct((TOP_K, t), jnp.int32),
                   jax.ShapeDtypeStruct((TOP_K, t), F32),
                   jax.ShapeDtypeStruct((TOP_K, t), jnp.int32),
                   jax.ShapeDtypeStruct((n_tiles, N_EXPERTS, 128), F32),
                   jax.ShapeDtypeStruct((N_EXPERTS, 128), F32)),
        grid=(n_tiles // n_sub,),
        in_specs=[pl.BlockSpec((N_EXPERTS, step), lambda i: (0, i)),
                  pl.BlockSpec((tm, tm), lambda i: (0, 0))],
        out_specs=(tok, tok, tok, pl.BlockSpec((n_sub, N_EXPERTS, 128), lambda i: (i, 0, 0)),
                   pl.BlockSpec((N_EXPERTS, 128), lambda i: (0, 0))),
        scratch_shapes=[pltpu.VMEM((N_EXPERTS, 128), F32)],
        compiler_params=_cparams(("arbitrary",), VMEM_LIMIT),
        name="router",
    )(logits_t, upper)


ROUTE_TILE = 256
ROW_ALIGN = 8
SLOT_ROWS = 48
N_SLOTS = N_EXPERTS * SLOT_ROWS
ZERO_ROWS = FFN_ROWS + SLOT_ROWS


def _slot_ids(idx, lrank, c):
    r = lrank - c * SLOT_ROWS
    return jnp.where(jnp.logical_and(r >= 0, r < SLOT_ROWS), idx * SLOT_ROWS + r, -1)


def _pack_bf16_pairs(x):
    u = pltpu.bitcast(x, jnp.uint32)
    w = x.shape[1] // 2
    return (u[:, 0:w] >> 16) | u[:, w:2 * w]


def _unpack_bf16_pairs(u):
    lo = pltpu.bitcast(u << 16, F32).astype(BF16)
    hi = pltpu.bitcast(u & jnp.uint32(0xFFFF0000), F32).astype(BF16)
    return jnp.concatenate([lo, hi], axis=1)


def _dispatch_kernel(tstart_ref, tcnt_ref, npass_ref, fill_ref, nused_ref, idx_ref, lrank_ref, h2_ref, xs_ref,
                     xbuf, zbuf, sem, zsem, state, *, tm, n_blocks):
    j = pl.program_id(0)

    def slot_copy(slot, e, row):
        return pltpu.make_async_copy(xbuf.at[slot, pl.ds(e * SLOT_ROWS, SLOT_ROWS)],
                                     xs_ref.at[pl.ds(pl.multiple_of(row, ROW_ALIGN), SLOT_ROWS)], sem.at[slot])

    def wait_pass(slot, n):
        @pl.when(n == N_EXPERTS)
        def _():
            pltpu.make_async_copy(xbuf.at[slot], xs_ref.at[pl.ds(0, N_SLOTS)], sem.at[slot]).wait()

        @pl.when(n != N_EXPERTS)
        def _():
            def body(i, carry):
                slot_copy(slot, 0, 0).wait()
                return carry
            lax.fori_loop(0, n, body, 0)

    @pl.when(j == 0)
    def _():
        zbuf[...] = jnp.zeros_like(zbuf)
        state[0] = 0
        state[1] = 0
        for parity in range(2):
            for e in range(parity, N_EXPERTS, 2):
                pltpu.make_async_copy(zbuf, xs_ref.at[pl.ds(pl.multiple_of(fill_ref[e], ROW_ALIGN), ZERO_ROWS)],
                                      zsem).start()
            for e in range(parity, N_EXPERTS, 2):
                pltpu.make_async_copy(zbuf, xs_ref.at[pl.ds(0, ZERO_ROWS)], zsem).wait()

        def tail(i, carry):
            pltpu.make_async_copy(zbuf.at[pl.ds(0, FFN_ROWS)],
                                  xs_ref.at[pl.ds(pl.multiple_of(i * FFN_ROWS, ROW_ALIGN), FFN_ROWS)], zsem).start()
            return carry
        lax.fori_loop(nused_ref[0], n_blocks, tail, 0)

        def tail_wait(i, carry):
            pltpu.make_async_copy(zbuf.at[pl.ds(0, FFN_ROWS)], xs_ref.at[pl.ds(0, FFN_ROWS)], zsem).wait()
            return carry
        lax.fori_loop(nused_ref[0], n_blocks, tail_wait, 0)

    def one_pass(c, carry):
        rows = lax.broadcasted_iota(jnp.int32, (N_SLOTS, tm), 0)
        sel = jnp.zeros((N_SLOTS, tm), F32)
        for k in range(TOP_K):
            sel = jnp.where(rows == _slot_ids(idx_ref[k:k + 1, :], lrank_ref[k:k + 1, :], c), 1.0, sel)
        packed = _pack_bf16_pairs(_mm(sel.astype(BF16), h2_ref[...]))
        p = state[0]
        slot = p % 2
        wait_pass(1 - slot, state[1])
        xbuf[slot] = packed
        n = 0
        for e in range(N_EXPERTS):
            live = tcnt_ref[j * N_EXPERTS + e] > c * SLOT_ROWS

            @pl.when(live)
            def _():
                slot_copy(slot, e, tstart_ref[j * N_EXPERTS + e] + c * SLOT_ROWS).start(priority=e % 2)
            n = n + live.astype(jnp.int32)
        state[0] = p + 1
        state[1] = n
        return carry

    lax.fori_loop(0, npass_ref[j], one_pass, 0)

    @pl.when(j == pl.num_programs(0) - 1)
    def _():
        wait_pass((state[0] - 1) % 2, state[1])


def _dispatch(tile_start, tile_count, n_pass, fill_start, n_used, idx_t, lrank_t, h2, n_rows):
    t, d = h2.shape
    tm = ROUTE_TILE
    w = d // 2
    n_blocks = n_rows // FFN_ROWS
    kern = functools.partial(_dispatch_kernel, tm=tm, n_blocks=n_blocks)
    tok = pl.BlockSpec((TOP_K, tm), lambda i, *_: (0, i))
    grid_spec = pltpu.PrefetchScalarGridSpec(
        num_scalar_prefetch=5,
        grid=(t // tm,),
        in_specs=[tok, tok, pl.BlockSpec((tm, d), lambda i, *_: (i, 0))],
        out_specs=pl.BlockSpec(memory_space=pl.ANY),
        scratch_shapes=[pltpu.VMEM((2, N_SLOTS, w), jnp.uint32), pltpu.VMEM((ZERO_ROWS, w), jnp.uint32),
                        pltpu.SemaphoreType.DMA((2,)), pltpu.SemaphoreType.DMA, pltpu.SMEM((2,), jnp.int32)],
    )
    return pl.pallas_call(
        kern,
        out_shape=jax.ShapeDtypeStruct((n_rows, w), jnp.uint32),
        grid_spec=grid_spec,
        compiler_params=_cparams(("arbitrary",), VMEM_LIMIT),
        name="dispatch",
    )(tile_start, tile_count, n_pass, fill_start, n_used, idx_t, lrank_t, h2)


GU_GROUP = 256


def _ffn_kernel(be_ref, nu_ref, x_ref, wgu_ref, bgu_ref, wd_ref, bd_ref, perm_ref, y_ref, wgu_s, wd_s):
    i = pl.program_id(0)
    live = i < nu_ref[0]
    new_expert = jnp.logical_or(i == 0, be_ref[i] != be_ref[jnp.maximum(i - 1, 0)])

    @pl.when(jnp.logical_and(live, new_expert))
    def _():
        for g in range(2 * D_EXPERT // GU_GROUP):
            sl = slice(g * GU_GROUP, (g + 1) * GU_GROUP)
            wgu_s[:, sl] = _mm(wgu_ref[0, :, sl].astype(BF16), perm_ref[...]).astype(BF16)
        wd_s[...] = wd_ref[0].astype(BF16)

    @pl.when(live)
    def _():
        x = _unpack_bf16_pairs(x_ref[...])
        gu = _mm(x, wgu_s[...]) + bgu_ref[0]
        half = GU_GROUP // 2
        n_grp = 2 * D_EXPERT // GU_GROUP
        glu = jnp.concatenate([gu[:, g * GU_GROUP:g * GU_GROUP + half] for g in range(n_grp)], axis=1)
        lin = jnp.concatenate([gu[:, g * GU_GROUP + half:(g + 1) * GU_GROUP] for g in range(n_grp)], axis=1)
        glu = jnp.minimum(glu, SWIGLU_LIMIT)
        lin = jnp.clip(lin, -SWIGLU_LIMIT, SWIGLU_LIMIT)
        act = glu * _sigmoid(SWIGLU_ALPHA * glu) * (lin + 1.0)
        y = _mm(act.astype(BF16), wd_s[...]) + bd_ref[0]
        y_ref[...] = _pack_bf16_pairs(y.astype(BF16).astype(F32))

    @pl.when(jnp.logical_not(live))
    def _():
        y_ref[...] = jnp.zeros_like(y_ref)


def _ffn(block_e, n_used, xs, w_gu, b_gu, w_d, b_d):
    n_rows, w = xs.shape
    bm = FFN_ROWS
    nblk = n_rows // bm
    d = w * 2
    half = GU_GROUP // 2
    perm = np.zeros((GU_GROUP, GU_GROUP), np.float32)
    perm[2 * np.arange(half), np.arange(half)] = 1.0
    perm[2 * np.arange(half) + 1, half + np.arange(half)] = 1.0

    def xmap(i, be, nu):
        return (jnp.minimum(i, nu[0] - 1), 0)

    def wmap(i, be, nu):
        return (be[i], 0, 0)

    grid_spec = pltpu.PrefetchScalarGridSpec(
        num_scalar_prefetch=2,
        grid=(nblk,),
        in_specs=[pl.BlockSpec((bm, w), xmap),
                  pl.BlockSpec((1, d, 2 * D_EXPERT), wmap),
                  pl.BlockSpec((1, 1, 2 * D_EXPERT), wmap),
                  pl.BlockSpec((1, D_EXPERT, d), wmap),
                  pl.BlockSpec((1, 1, d), wmap),
                  pl.BlockSpec((GU_GROUP, GU_GROUP), lambda i, be, nu: (0, 0))],
        out_specs=pl.BlockSpec((bm, w), lambda i, be, nu: (i, 0)),
        scratch_shapes=[pltpu.VMEM((d, 2 * D_EXPERT), BF16), pltpu.VMEM((D_EXPERT, d), BF16)],
    )
    return pl.pallas_call(
        _ffn_kernel,
        out_shape=jax.ShapeDtypeStruct((n_rows, w), jnp.uint32),
        grid_spec=grid_spec,
        compiler_params=_cparams(("arbitrary",), VMEM_LIMIT),
        name="ffn",
    )(block_e, n_used, xs, w_gu, b_gu, w_d, b_d, jnp.asarray(perm, dtype=BF16))


def _combine_kernel(tstart_ref, tcnt_ref, npass_ref, idx_ref, lrank_ref, p_ref, y_ref, x1_ref, mod_ref, g_ref, o_ref,
                    ybuf, sem, *, tm, nt):
    b = pl.program_id(0)
    j = b * nt + pl.program_id(1)
    n_steps = pl.num_programs(0) * nt
    slot = j % 2

    def slot_copy(s, e, row):
        return pltpu.make_async_copy(y_ref.at[pl.ds(pl.multiple_of(row, ROW_ALIGN), SLOT_ROWS)], ybuf.at[s, pl.ds(e * SLOT_ROWS, SLOT_ROWS)],
                                     sem.at[s])

    def fetch(tile, c, s):
        for e in range(N_EXPERTS):
            @pl.when(tcnt_ref[tile * N_EXPERTS + e] > c * SLOT_ROWS)
            def _():
                slot_copy(s, e, tstart_ref[tile * N_EXPERTS + e] + c * SLOT_ROWS).start(priority=e % 2)

    def arrive(tile, c, s):
        live = [tcnt_ref[tile * N_EXPERTS + e] > c * SLOT_ROWS for e in range(N_EXPERTS)]
        n = sum(l.astype(jnp.int32) for l in live)

        @pl.when(n == N_EXPERTS)
        def _():
            pltpu.make_async_copy(y_ref.at[pl.ds(0, N_SLOTS)], ybuf.at[s], sem.at[s]).wait()

        @pl.when(n != N_EXPERTS)
        def _():
            for e in range(N_EXPERTS):
                @pl.when(live[e])
                def _():
                    slot_copy(s, e, 0).wait()

    @pl.when(j == 0)
    def _():
        ybuf[...] = jnp.zeros_like(ybuf)
        fetch(0, 0, 0)

    @pl.when(j + 1 < n_steps)
    def _():
        fetch(j + 1, 0, 1 - slot)

    def weights(c):
        cols = lax.broadcasted_iota(jnp.int32, (tm, N_SLOTS), 1)
        w = jnp.zeros((tm, N_SLOTS), F32)
        for k in range(TOP_K):
            w = jnp.where(cols == _slot_ids(idx_ref[:, k:k + 1], lrank_ref[:, k:k + 1], c), p_ref[:, k:k + 1], w)
        return w.astype(BF16)

    arrive(j, 0, slot)
    acc = _mm(weights(0), _unpack_bf16_pairs(ybuf[slot]))

    def extra_pass(c, acc):
        fetch(j, c, 2)
        arrive(j, c, 2)
        return acc + _mm(weights(c), _unpack_bf16_pairs(ybuf[2]))

    acc = lax.fori_loop(1, npass_ref[j], extra_pass, acc)
    m = mod_ref[pl.ds(b, 1), :]
    xo = x1_ref[0] + m[:, 5 * D_MODEL:6 * D_MODEL] * acc
    ms = jnp.mean(xo * xo, axis=-1, keepdims=True)
    o_ref[0] = xo * lax.rsqrt(ms + EPS) * g_ref[...]


def _combine(tile_start, tile_count, n_pass, idx_c, lrank_c, prob_c, y, x1, mod, g_fin):
    bsz, n, d = x1.shape
    tm = ROUTE_TILE
    nt = n // tm
    kern = functools.partial(_combine_kernel, tm=tm, nt=nt)
    tok = pl.BlockSpec((tm, TOP_K), lambda b, t, *_: (b * nt + t, 0))
    grid_spec = pltpu.PrefetchScalarGridSpec(
        num_scalar_prefetch=3,
        grid=(bsz, nt),
        in_specs=[tok, tok, tok,
                  pl.BlockSpec(memory_space=pl.ANY),
                  pl.BlockSpec((1, tm, d), lambda b, t, *_: (b, t, 0)),
                  pl.BlockSpec(mod.shape, lambda b, t, *_: (0, 0)),
                  pl.BlockSpec((1, d), lambda b, t, *_: (0, 0))],
        out_specs=pl.BlockSpec((1, tm, d), lambda b, t, *_: (b, t, 0)),
        scratch_shapes=[pltpu.VMEM((3, N_SLOTS, d // 2), jnp.uint32), pltpu.SemaphoreType.DMA((3,))],
    )
    return pl.pallas_call(
        kern,
        out_shape=jax.ShapeDtypeStruct((bsz, n, d), F32),
        grid_spec=grid_spec,
        compiler_params=_cparams(("arbitrary", "arbitrary"), VMEM_LIMIT),
        name="combine",
    )(tile_start, tile_count, n_pass, idx_c, lrank_c, prob_c, y, x1, mod, g_fin)


def _prep_w_in(w_in):
    q_a, kv, hq, ff, fb, hi, go, mm, mh = jnp.split(
        w_in, np.cumsum((MLA_Q_RANK, MLA_KV_RANK + MLA_ROPE) + (D_MODEL,) * 6).tolist(), axis=1)
    c_kv, kr = kv[:, :MLA_KV_RANK], kv[:, MLA_KV_RANK:]
    half = MLA_ROPE // 2
    kr_sw = jnp.concatenate([-kr[:, half:], kr[:, :half]], axis=1)
    w16 = jnp.concatenate([hq, hi, go, mm, mh, q_a, c_kv, kr, kr_sw], axis=1)
    w16 = jnp.pad(w16, ((0, 0), (0, N_PROJ16 - w16.shape[1])))
    return w16.astype(BF16), jnp.concatenate([ff, fb], axis=1).astype(BF16)


def _prep_w_q(w_q_b):
    w = w_q_b.reshape(MLA_Q_RANK, HEADS, MLA_QK)
    w = jnp.pad(w, ((0, 0), (0, 0), (0, MLA_QK_PAD - MLA_QK)))
    return w.reshape(MLA_Q_RANK, HEADS * MLA_QK_PAD).astype(BF16)


def _prep_w_kv(w_kv_b):
    w = w_kv_b.reshape(MLA_KV_RANK, HEADS, 2 * HEAD_DIM)
    k = w[:, :, :HEAD_DIM].reshape(MLA_KV_RANK, HEADS * HEAD_DIM)
    v = w[:, :, HEAD_DIM:].reshape(MLA_KV_RANK, HEADS * HEAD_DIM)
    return jnp.concatenate([k, v], axis=1).astype(BF16)


def _rope_tables(n_lat, n_ctx):
    f32 = np.float32
    rows = n_lat // GRID_W
    row = np.repeat(np.arange(rows), GRID_W).astype(f32)
    col = np.tile(np.arange(GRID_W), rows).astype(f32)
    n_freq = MLA_ROPE // 4
    inv = (f32(ROPE_BASE) ** (-np.arange(n_freq, dtype=f32) / f32(n_freq))).astype(f32)
    ang = np.concatenate([row[:, None] * inv, col[:, None] * inv], axis=-1).astype(f32)
    cos, sin = np.cos(ang).astype(f32), np.sin(ang).astype(f32)
    pad = np.zeros((n_lat, HEAD_DIM - MLA_ROPE), f32)
    cos_q = np.concatenate([cos, cos, pad], axis=1)
    sin_q = np.concatenate([-sin, sin, pad], axis=1)
    cos_k = np.concatenate([cos, cos, pad], axis=1)
    sin_k = np.concatenate([sin, sin, pad], axis=1)
    ctx_cos = np.concatenate([np.ones((n_ctx, MLA_ROPE), f32), np.zeros((n_ctx, HEAD_DIM - MLA_ROPE), f32)], axis=1)
    cos_k = np.concatenate([cos_k, ctx_cos], axis=0)
    sin_k = np.concatenate([sin_k, np.zeros((n_ctx, HEAD_DIM), f32)], axis=0)
    return jnp.asarray(cos_q), jnp.asarray(sin_q), jnp.asarray(cos_k), jnp.asarray(sin_k)


def kernel(x, c, ctx, c_ctx, w_mod, b_mod, norm_mix_g, w_in, mla_q_norm_g, w_q_b, mla_kv_norm_g, w_kv_b,
           hg_lb_logits, hg_norm_g, w_out, norm_ffn_g, w_router, b_router, w_gate_up, b_gate_up, w_down,
           b_down, final_norm_g):
    bsz, n_lat, d = x.shape
    n_ctx = ctx.shape[1]
    assert d == D_MODEL and w_mod.shape[0] == 1
    assert n_lat % HG_CHUNK == 0 and n_ctx % HG_CHUNK == 0 and n_lat % GRID_W == 0
    t_tok = bsz * n_lat

    mod_rows = -(-(bsz + 1) // 8) * 8
    cc = jnp.concatenate([c, c_ctx[None, :], jnp.zeros((mod_rows - bsz - 1, d), F32)], axis=0)
    mod = _modulation(cc, w_mod[0], b_mod[0][None, :])

    w16, w32 = _prep_w_in(w_in[0])
    p, pf = _inproj(x, ctx, mod, norm_mix_g[0][None, :], w16, w32)

    cos_q, sin_q, cos_k, sin_k = _rope_tables(n_lat, n_ctx)
    q = _mla_q(p, mla_q_norm_g[0][None, :], _prep_w_q(w_q_b[0]), cos_q, sin_q, n_lat)
    k, v = _mla_kv(p, mla_kv_norm_g[0][None, :], _prep_w_kv(w_kv_b[0]), cos_k, sin_k)
    y_mla = _attention(q, k, v)

    lb = jax.nn.softmax(hg_lb_logits.astype(F32), axis=1)[:, 0, :]
    o_f, o_b = _hgrn_scans(p, pf, lb.reshape(2, HEADS, 1, HEAD_DIM), n_lat)

    x1, h2, logits = _merge(y_mla, o_f, o_b, p, x, mod, hg_norm_g[0][None, :], w_out[0].astype(BF16),
                            norm_ffn_g[0][None, :], w_router[0], b_router[0][None, :])
    logits_t = logits.T

    assert n_lat % ROUTE_TILE == 0
    idx_t, prob_t, lrank_t, base, cnt = _router(logits_t, ROUTE_TILE)

    n_tiles = t_tok // ROUTE_TILE
    counts = cnt[:, 0].astype(jnp.int32)
    base = base[:, :, 0].astype(jnp.int32)
    tile_count = jnp.concatenate([base[1:] - base[:-1], (counts - base[-1])[None, :]], axis=0)
    seg = (tile_count + ROW_ALIGN - 1) // ROW_ALIGN * ROW_ALIGN
    seg_end = jnp.cumsum(seg, axis=0)
    rows_e = seg_end[-1]
    padded = (rows_e + SLOT_ROWS + FFN_ROWS - 1) // FFN_ROWS * FFN_ROWS
    pad_end = jnp.cumsum(padded)
    pad_start = pad_end - padded
    max_rows = t_tok * TOP_K + N_EXPERTS * (n_tiles * (ROW_ALIGN - 1) + SLOT_ROWS + FFN_ROWS - 1)
    n_rows = -(-max_rows // FFN_ROWS) * FFN_ROWS
    nblk = n_rows // FFN_ROWS
    block_start = jnp.arange(nblk, dtype=jnp.int32) * FFN_ROWS
    block_e = jnp.minimum(jnp.sum(pad_end[None, :] <= block_start[:, None], axis=1), N_EXPERTS - 1).astype(jnp.int32)
    n_used = (pad_end[-1:] // FFN_ROWS).astype(jnp.int32)
    tile_start = (pad_start[None, :] + seg_end - seg).reshape(-1).astype(jnp.int32)
    n_pass = jnp.maximum(-(-jnp.max(tile_count, axis=1) // SLOT_ROWS), 1).astype(jnp.int32)
    tile_count = tile_count.reshape(-1)
    fill_start = jnp.minimum(pad_start + rows_e, n_rows - ZERO_ROWS).astype(jnp.int32)

    xs = _dispatch(tile_start, tile_count, n_pass, fill_start, n_used, idx_t, lrank_t, h2, n_rows)

    n_grp = 2 * D_EXPERT // GU_GROUP
    b_gu = b_gate_up[0].reshape(N_EXPERTS, n_grp, GU_GROUP // 2, 2).transpose(0, 1, 3, 2).reshape(N_EXPERTS, 1, 2 * D_EXPERT)
    y = _ffn(block_e, n_used, xs, w_gate_up[0], b_gu, w_down[0], b_down[0][:, None, :])

    return _combine(tile_start, tile_count, n_pass, idx_t.T, lrank_t.T, prob_t.T, y, x1, mod, final_norm_g[None, :])
```

```python
import functools

import jax
import jax.numpy as jnp
import numpy as np
from jax import lax
from jax.experimental import pallas as pl
from jax.experimental.pallas import tpu as pltpu

F32 = jnp.float32
BF16 = jnp.bfloat16

D_MODEL = 1024
EPS = 1e-6
LOG2_E = 1.4426950408889634
N_MOD = 6
GRID_W = 64
ROPE_BASE = 10000.0

HEADS = 8
HEAD_DIM = 128
MLA_ROPE = 64
MLA_QK = HEAD_DIM + MLA_ROPE
MLA_QK_PAD = 256
MLA_Q_RANK = 768
MLA_KV_RANK = 256
ROPE_HALF = MLA_ROPE // 2
QA_PIECE = 256

N_EXPERTS = 32
TOP_K = 4
D_EXPERT = 1024
SWIGLU_LIMIT = 7.0
SWIGLU_ALPHA = 1.702

HG_CHUNK = 128
FFN_ROWS = 512

COL_HQ, COL_I, COL_GO, COL_MM, COL_MH = (i * D_MODEL for i in range(5))
COL_QA = 5 * D_MODEL
COL_CKV = COL_QA + MLA_Q_RANK
COL_KR = COL_CKV + MLA_KV_RANK
PROJ16_TN = 1280
N_PROJ16 = -(-(COL_KR + 2 * MLA_ROPE) // PROJ16_TN) * PROJ16_TN
COL_FF, COL_FB = 0, D_MODEL
N_PROJ32 = 2 * D_MODEL
PROJ32_TN = 1024

VMEM_LIMIT = 56 * 1024 * 1024


def _cparams(sem, vmem=None):
    return pltpu.CompilerParams(dimension_semantics=sem, vmem_limit_bytes=vmem)


def _tile(n, pref, mult=8):
    best = None
    for t in range(mult, min(n, pref) + 1, mult):
        if n % t == 0:
            best = t
    assert best is not None, (n, pref, mult)
    return best


def _nt(a, b):
    return lax.dot_general(a, b, (((1,), (1,)), ((), ())), preferred_element_type=F32)


def _mm(a, b):
    return jnp.dot(a, b, preferred_element_type=F32)


def _split2(a):
    hi = a.astype(BF16)
    lo = (a - hi.astype(F32)).astype(BF16)
    return hi, lo


def _sigmoid(x):
    return 1.0 / (1.0 + jnp.exp(-x))


def _gate_sigmoid(x):
    return 0.5 * jnp.tanh(0.5 * x) + 0.5


def _mod_kernel(c_ref, w_ref, b_ref, o_ref):
    c = c_ref[...]
    s = c * _sigmoid(c)
    s_hi, s_lo = _split2(s)
    w_hi, w_lo = _split2(w_ref[...])
    o_ref[...] = _mm(s_hi, w_hi) + _mm(s_lo, w_hi) + _mm(s_hi, w_lo) + b_ref[...]


def _modulation(cc, w_mod, b_mod):
    r, d = cc.shape
    n = w_mod.shape[1]
    tn = _tile(n, 1536, 128)
    return pl.pallas_call(
        _mod_kernel,
        out_shape=jax.ShapeDtypeStruct((r, n), F32),
        grid=(n // tn,),
        in_specs=[pl.BlockSpec((r, d), lambda j: (0, 0)),
                  pl.BlockSpec((d, tn), lambda j: (0, j)),
                  pl.BlockSpec((1, tn), lambda j: (0, j))],
        out_specs=pl.BlockSpec((r, tn), lambda j: (0, j)),
        compiler_params=_cparams(("parallel",), VMEM_LIMIT),
        name="mod",
    )(cc, w_mod, b_mod)


INPROJ_GROUP = 2


def _inproj_kernel(x_ref, c_ref, mod_ref, g_ref, w16_ref, w32_ref, o16_ref, o32_ref, hn_ref, *, n_lat, tm, nt, grp,
                   ctx_row, n16):
    j = pl.program_id(1)
    r = pl.program_id(2)
    b = pl.program_id(0) * grp + r // nt
    t = r % nt

    @pl.when(j == 0)
    def _():
        n_lat_last = n_lat - (nt - 1) * tm
        row = t * tm + lax.broadcasted_iota(jnp.int32, (tm, 1), 0)
        is_ctx = row >= n_lat
        ctx_rows = jnp.concatenate([jnp.zeros((n_lat_last, D_MODEL), F32), c_ref[0]], axis=0)
        x = jnp.where(is_ctx, ctx_rows, x_ref[0])
        ms = jnp.mean(x * x, axis=-1, keepdims=True)
        y = x * lax.rsqrt(ms + EPS) * g_ref[...]
        m_lat = mod_ref[pl.ds(b, 1), :]
        m_ctx = mod_ref[pl.ds(ctx_row, 1), :]
        shift = jnp.where(is_ctx, m_ctx[:, 0:D_MODEL], m_lat[:, 0:D_MODEL])
        scale = jnp.where(is_ctx, m_ctx[:, D_MODEL:2 * D_MODEL], m_lat[:, D_MODEL:2 * D_MODEL])
        hn_ref[r] = (y * (1.0 + scale) + shift).astype(BF16)

    @pl.when(j < n16)
    def _():
        o16_ref[0] = _mm(hn_ref[r], w16_ref[...]).astype(BF16)

    @pl.when(j >= n16)
    def _():
        o32_ref[0] = _mm(hn_ref[r], w32_ref[...])


def _inproj(x, ctx, mod, g, w16, w32):
    bsz, n_lat, d = x.shape
    n_ctx = ctx.shape[1]
    rows = n_lat + n_ctx
    tm = _tile(rows, 1152)
    nt = rows // tm
    assert n_ctx <= tm <= n_lat, (n_ctx, tm, n_lat)
    grp = INPROJ_GROUP if bsz % INPROJ_GROUP == 0 else 1
    n_r = grp * nt
    n16 = N_PROJ16 // PROJ16_TN
    n32 = N_PROJ32 // PROJ32_TN

    def row_tile(g_, r):
        return g_ * grp + r // nt, r % nt

    def x_map(g_, j, r):
        b, t = row_tile(g_, jnp.where(j == 0, r, n_r - 1))
        return b, t, 0

    def o16_map(g_, j, r):
        b, t = row_tile(g_, jnp.where(j < n16, r, n_r - 1))
        return b, t, jnp.minimum(j, n16 - 1)

    def o32_map(g_, j, r):
        b, t = row_tile(g_, jnp.where(j >= n16, r, 0))
        return b, t, jnp.maximum(j - n16, 0)

    kern = functools.partial(_inproj_kernel, n_lat=n_lat, tm=tm, nt=nt, grp=grp, ctx_row=bsz, n16=n16)
    return pl.pallas_call(
        kern,
        out_shape=(jax.ShapeDtypeStruct((bsz, rows, N_PROJ16), BF16),
                   jax.ShapeDtypeStruct((bsz, rows, N_PROJ32), F32)),
        grid=(bsz // grp, n16 + n32, n_r),
        in_specs=[pl.BlockSpec((1, tm, d), x_map),
                  pl.BlockSpec((1, n_ctx, d), lambda g_, j, r: (x_map(g_, j, r)[0], 0, 0)),
                  pl.BlockSpec(mod.shape, lambda g_, j, r: (0, 0)),
                  pl.BlockSpec((1, d), lambda g_, j, r: (0, 0)),
                  pl.BlockSpec((d, PROJ16_TN), lambda g_, j, r: (0, jnp.minimum(j, n16 - 1))),
                  pl.BlockSpec((d, PROJ32_TN), lambda g_, j, r: (0, jnp.maximum(j - n16, 0)))],
        out_specs=(pl.BlockSpec((1, tm, PROJ16_TN), o16_map),
                   pl.BlockSpec((1, tm, PROJ32_TN), o32_map)),
        scratch_shapes=[pltpu.VMEM((n_r, tm, d), BF16)],
        compiler_params=_cparams(("parallel", "arbitrary", "arbitrary"), VMEM_LIMIT),
        name="inproj",
    )(x, ctx, mod, g, w16, w32)


def _mlaq_kernel(a0_ref, a1_ref, a2_ref, g_ref, w_ref, cos_ref, sin_ref, o_ref, *, tm, n_parts):
    n_rows = tm // n_parts
    scale = MLA_QK ** -0.5 * LOG2_E

    def part(i):
        rows = slice(i * n_rows, (i + 1) * n_rows)
        parts = [a_ref[0, rows, :].astype(F32) for a_ref in (a0_ref, a1_ref, a2_ref)]
        ss = sum(jnp.sum(p * p, axis=-1, keepdims=True) for p in parts)
        r = lax.rsqrt(ss * (1.0 / MLA_Q_RANK) + EPS)
        acc = None
        for j, p in enumerate(parts):
            hj = (p * r * g_ref[:, j * QA_PIECE:(j + 1) * QA_PIECE]).astype(BF16)
            d = _mm(hj, w_ref[j * QA_PIECE:(j + 1) * QA_PIECE, :])
            acc = d if acc is None else acc + d
        yield
        lane = lax.broadcasted_iota(jnp.int32, (n_rows, HEAD_DIM), 1)
        cos = cos_ref[rows, :]
        sin = sin_ref[rows, :]
        for h in range(HEADS):
            nope = acc[:, h * MLA_QK_PAD:h * MLA_QK_PAD + HEAD_DIM]
            rp = acc[:, h * MLA_QK_PAD + HEAD_DIM:(h + 1) * MLA_QK_PAD]
            swapped = jnp.where(lane < ROPE_HALF, pltpu.roll(rp, HEAD_DIM - ROPE_HALF, 1),
                                pltpu.roll(rp, ROPE_HALF, 1))
            rot = rp * cos + swapped * sin
            o_ref[0, h, rows, 0:HEAD_DIM] = (nope * scale).astype(BF16)
            o_ref[0, h, rows, HEAD_DIM:MLA_QK_PAD] = (rot * scale).astype(BF16)

    _round_robin([part(i) for i in range(n_parts)])


def _mla_q(p, g, w, cos_q, sin_q, n_lat):
    bsz = p.shape[0]
    tm = _tile(n_lat, 512)
    cb = COL_QA // QA_PIECE
    kern = functools.partial(_mlaq_kernel, tm=tm, n_parts=2 if tm % 32 == 0 else 1)
    return pl.pallas_call(
        kern,
        out_shape=jax.ShapeDtypeStruct((bsz, HEADS, n_lat, MLA_QK_PAD), BF16),
        grid=(bsz, n_lat // tm),
        in_specs=[pl.BlockSpec((1, tm, QA_PIECE), lambda b, t: (b, t, cb)),
                  pl.BlockSpec((1, tm, QA_PIECE), lambda b, t: (b, t, cb + 1)),
                  pl.BlockSpec((1, tm, QA_PIECE), lambda b, t: (b, t, cb + 2)),
                  pl.BlockSpec((1, MLA_Q_RANK), lambda b, t: (0, 0)),
                  pl.BlockSpec(w.shape, lambda b, t: (0, 0)),
                  pl.BlockSpec((tm, HEAD_DIM), lambda b, t: (t, 0)),
                  pl.BlockSpec((tm, HEAD_DIM), lambda b, t: (t, 0))],
        out_specs=pl.BlockSpec((1, HEADS, tm, MLA_QK_PAD), lambda b, t: (b, 0, t, 0)),
        compiler_params=_cparams(("parallel", "parallel"), VMEM_LIMIT),
        name="mla_q",
    )(p, p, p, g, w, cos_q, sin_q)


def _mlakv_kernel(c_ref, kr_ref, g_ref, w_ref, cos_ref, sin_ref, k_ref, v_ref):
    c = c_ref[0].astype(F32)
    ms = jnp.mean(c * c, axis=-1, keepdims=True)
    hn = (c * lax.rsqrt(ms + EPS) * g_ref[...]).astype(BF16)
    kv = _mm(hn, w_ref[...])
    grp = kr_ref[0].astype(F32)
    rot = (grp * cos_ref[...] + pltpu.roll(grp, MLA_ROPE, 1) * sin_ref[...]).astype(BF16)
    for h in range(HEADS):
        k_ref[0, h, :, 0:HEAD_DIM] = kv[:, h * HEAD_DIM:(h + 1) * HEAD_DIM].astype(BF16)
        k_ref[0, h, :, HEAD_DIM:MLA_QK_PAD] = rot
        v_ref[0, h] = kv[:, D_MODEL + h * HEAD_DIM:D_MODEL + (h + 1) * HEAD_DIM].astype(BF16)


def _mla_kv(p, g, w, cos_k, sin_k):
    bsz, rows, _ = p.shape
    tm = _tile(rows, 768)
    return pl.pallas_call(
        _mlakv_kernel,
        out_shape=(jax.ShapeDtypeStruct((bsz, HEADS, rows, MLA_QK_PAD), BF16),
                   jax.ShapeDtypeStruct((bsz, HEADS, rows, HEAD_DIM), BF16)),
        grid=(bsz, rows // tm),
        in_specs=[pl.BlockSpec((1, tm, MLA_KV_RANK), lambda b, t: (b, t, COL_CKV // MLA_KV_RANK)),
                  pl.BlockSpec((1, tm, 128), lambda b, t: (b, t, COL_KR // 128)),
                  pl.BlockSpec((1, MLA_KV_RANK), lambda b, t: (0, 0)),
                  pl.BlockSpec(w.shape, lambda b, t: (0, 0)),
                  pl.BlockSpec((tm, 128), lambda b, t: (t, 0)),
                  pl.BlockSpec((tm, 128), lambda b, t: (t, 0))],
        out_specs=(pl.BlockSpec((1, HEADS, tm, MLA_QK_PAD), lambda b, t: (b, 0, t, 0)),
                   pl.BlockSpec((1, HEADS, tm, HEAD_DIM), lambda b, t: (b, 0, t, 0))),
        compiler_params=_cparams(("parallel", "parallel"), VMEM_LIMIT),
        name="mla_kv",
    )(p, p, g, w, cos_k, sin_k)


def _attn_kernel(q_ref, k_ref, v_ref, o_ref, *, tq, sub):
    k = k_ref[0, 0]
    v = v_ref[0, 0]
    v1 = jnp.concatenate([v, jnp.ones_like(v)], axis=1)

    def sub_tile(r):
        rows = slice(r * sub, (r + 1) * sub)
        s = _nt(q_ref[0, 0, rows, :], k)
        yield
        m = jnp.max(s, axis=-1, keepdims=True)
        p = jnp.exp2(s - m).astype(BF16)
        yield
        ol = _mm(p, v1)
        o_ref[0, rows, :] = (ol[:, 0:HEAD_DIM] * (1.0 / ol[:, HEAD_DIM:HEAD_DIM + 1])).astype(BF16)

    _round_robin([sub_tile(r) for r in range(tq // sub)])


def _attention(q, k, v):
    bsz, _, n, _ = q.shape
    m = k.shape[2]
    tq = _tile(n, 2048)
    sub = _tile(tq, 512)
    kern = functools.partial(_attn_kernel, tq=tq, sub=sub)
    return pl.pallas_call(
        kern,
        out_shape=jax.ShapeDtypeStruct((bsz, n, HEADS * HEAD_DIM), BF16),
        grid=(bsz, HEADS, n // tq),
        in_specs=[pl.BlockSpec((1, 1, tq, MLA_QK_PAD), lambda b, h, t: (b, h, t, 0)),
                  pl.BlockSpec((1, 1, m, MLA_QK_PAD), lambda b, h, t: (b, h, 0, 0)),
                  pl.BlockSpec((1, 1, m, HEAD_DIM), lambda b, h, t: (b, h, 0, 0))],
        out_specs=pl.BlockSpec((1, tq, HEAD_DIM), lambda b, h, t: (b, t, h)),
        compiler_params=_cparams(("parallel", "parallel", "arbitrary"), VMEM_LIMIT),
        name="attention",
    )(q, k, v)


def _level_ref(cum, blk, reverse):
    c = cum.shape[0]
    half = blk // 2
    r = half if reverse else half - 1
    if blk >= 8:
        x = cum.reshape(c // blk, blk, HEAD_DIM)
        e = jnp.broadcast_to(x[:, r:r + 1, :], x.shape)
        return e.reshape(c, HEAD_DIM)
    x = cum.reshape(c // 8, 8, HEAD_DIM)
    sub = lax.broadcasted_iota(jnp.int32, x.shape, 1)
    e = None
    for jb in range(8 // blk):
        cand = jnp.broadcast_to(x[:, jb * blk + r:jb * blk + r + 1, :], x.shape)
        e = cand if e is None else jnp.where(sub >= jb * blk, cand, e)
    return e.reshape(c, HEAD_DIM)


HG_SUB = 32
HG_SUB_LEVEL = HG_SUB.bit_length() - 1
HG_MAX_EXP2 = 100.0


def _hgrn_state(z, v, lb, tri, st, *, chunk, reverse):
    f = lb + (1.0 - lb) * _sigmoid(z)
    kb = (1.0 - f).astype(BF16)
    g = jnp.log(f) * LOG2_E

    g_hi, g_lo = _split2(g)
    yield
    cum = _mm(tri, jnp.concatenate([g_hi, g_lo], axis=0))
    yield
    last = 0 if reverse else chunk - 1
    tot = cum[last:last + 1, :]
    kt = kb * jnp.exp2(tot - cum).astype(BF16)
    vt = v.astype(F32).T.astype(BF16)
    yield
    st_new = st * jnp.exp2(tot) + _mm(vt, kt)
    return kb, cum, st_new


def _sub_block_decay(cum, reverse):
    c = cum.shape[0]
    x = cum.reshape(c // HG_SUB, HG_SUB, HEAD_DIM)
    zero = jnp.zeros((1, 1, HEAD_DIM), F32)
    if reverse:
        edge = jnp.concatenate([x[1:, 0:1, :], zero], axis=0)
    else:
        edge = jnp.concatenate([zero, x[:-1, HG_SUB - 1:HG_SUB, :]], axis=0)
    return (x - edge).reshape(c, HEAD_DIM)


def _hgrn_readout(q, v, kb, cum, sub, lvl, st, *, chunk, reverse, shared):
    q = q.astype(F32)
    qb = (q * _gate_sigmoid(q) * (HEAD_DIM ** -0.5)).astype(BF16)
    yield
    if shared:
        att = jnp.where(lvl == 0, _nt(qb * jnp.exp2(sub).astype(BF16), kb * jnp.exp2(-sub).astype(BF16)).astype(BF16),
                        jnp.zeros((), BF16))
        first = HG_SUB_LEVEL + 1
    else:
        att = jnp.where(lvl == 0, _nt(qb, kb).astype(BF16), jnp.zeros((), BF16))
        first = 1
    for lv in range(first, chunk.bit_length()):
        zrel = cum - _level_ref(cum, 1 << lv, reverse)
        neg_abs = pltpu.bitcast(pltpu.bitcast(zrel, jnp.uint32) | jnp.uint32(0x80000000), F32)
        e = jnp.exp2(neg_abs).astype(BF16)
        yield
        att = jnp.where(lvl == lv, _nt(qb * e, kb * e).astype(BF16), att)
        yield
    return _nt(qb * jnp.exp2(cum).astype(BF16), st.astype(BF16)) + _mm(att, v.astype(BF16))


def _round_robin(gens):
    out = [None] * len(gens)
    active = list(range(len(gens)))
    while active:
        for i in list(active):
            try:
                next(gens[i])
            except StopIteration as e:
                out[i] = e.value
                active.remove(i)
    return out


def _hgrn_kernel(qf_ref, zf_ref, if_ref, qb_ref, zb_ref, ib_ref, lb_ref, tri_ref, lvl_ref, of_ref, ob_ref, st_ref,
                 *, chunk, group, n_ctx_chunks):
    step = pl.program_id(2)

    @pl.when(step == 0)
    def _():
        st_ref[...] = jnp.zeros_like(st_ref)

    ins = ((qf_ref, zf_ref, if_ref), (qb_ref, zb_ref, ib_ref))
    outs = (of_ref, ob_ref)
    chains = [(d, j) for j in range(group) for d in range(2)]

    def cols(j):
        return slice(j * HEAD_DIM, (j + 1) * HEAD_DIM)

    def advance(want_out):
        old = [st_ref[d, j] for d, j in chains]
        parts = _round_robin([_hgrn_state(ins[d][1][0, :, cols(j)], ins[d][2][0, :, cols(j)], lb_ref[d, j],
                                          tri_ref[d], st, chunk=chunk, reverse=bool(d))
                              for (d, j), st in zip(chains, old)])
        for (d, j), (_, _, st_new) in zip(chains, parts):
            st_ref[d, j] = st_new
        if not want_out:
            return
        subs = [_sub_block_decay(cum, bool(d)) for (d, j), (_, cum, _) in zip(chains, parts)]
        low = subs[0]
        for s in subs[1:]:
            low = jnp.minimum(low, s)
        low = jnp.min(jnp.min(low, axis=0, keepdims=True), axis=1, keepdims=True)
        in_range = low[0, 0] >= -HG_MAX_EXP2

        def readouts(shared):
            os_ = _round_robin([_hgrn_readout(ins[d][0][0, :, cols(j)], ins[d][2][0, :, cols(j)], kb, cum, sub,
                                              lvl_ref[d, int(shared)], st, chunk=chunk, reverse=bool(d), shared=shared)
                                for (d, j), (kb, cum, _), sub, st in zip(chains, parts, subs, old)])
            for (d, j), o in zip(chains, os_):
                outs[d][0, j] = o.astype(BF16)

        @pl.when(in_range)
        def _():
            readouts(True)

        @pl.when(jnp.logical_not(in_range))
        def _():
            readouts(False)

    @pl.when(step < n_ctx_chunks)
    def _():
        advance(False)

    @pl.when(step >= n_ctx_chunks)
    def _():
        advance(True)


def _hgrn_consts(chunk, reverse):
    t = np.arange(chunk)[:, None]
    s = np.arange(chunk)[None, :]
    x = t ^ s
    bitlen = np.zeros_like(x)
    for b in range(chunk.bit_length()):
        bitlen = np.where(x >> b > 0, b + 1, bitlen)
    valid = (s > t) if reverse else (t > s)
    lvl = np.where(t == s, 0, np.where(valid, bitlen, -1)).astype(np.float32)
    lvl_shared = np.where((lvl >= 0) & (lvl <= HG_SUB_LEVEL), 0, lvl)
    tri = ((s >= t) if reverse else (t >= s)).astype(np.float32)
    return np.concatenate([tri, tri], axis=1), np.stack([lvl, lvl_shared])


HG_GROUP = 8


def _hgrn_scans(p, pf, lb, n_lat):
    bsz, rows, _ = p.shape
    c = HG_CHUNK
    grp = HG_GROUP
    n_chunks = rows // c
    n_lat_c = n_lat // c
    n_ctx_c = n_chunks - n_lat_c
    consts = [_hgrn_consts(c, False), _hgrn_consts(c, True)]
    tri = jnp.asarray(np.stack([consts[0][0], consts[1][0]]), dtype=BF16)
    lvl = jnp.asarray(np.stack([consts[0][1], consts[1][1]]), dtype=BF16)

    def cidx_f(i):
        return jnp.where(i < n_ctx_c, n_lat_c + i, i - n_ctx_c)

    def cidx_b(i):
        return n_chunks - 1 - i

    def col_spec(col, cidx):
        return pl.BlockSpec((1, c, grp * HEAD_DIM), lambda b, h, i: (b, cidx(i), col // (grp * HEAD_DIM) + h))

    def out_spec(cidx):
        return pl.BlockSpec((1, grp, c, HEAD_DIM), lambda b, h, i: (b, h, cidx(jnp.maximum(i, n_ctx_c)), 0))

    o_shape = jax.ShapeDtypeStruct((bsz, HEADS, n_lat, HEAD_DIM), BF16)
    kern = functools.partial(_hgrn_kernel, chunk=c, group=grp, n_ctx_chunks=n_ctx_c)
    return pl.pallas_call(
        kern,
        out_shape=(o_shape, o_shape),
        grid=(bsz, HEADS // grp, n_chunks),
        in_specs=[col_spec(COL_HQ, cidx_f), col_spec(COL_FF, cidx_f), col_spec(COL_I, cidx_f),
                  col_spec(COL_HQ, cidx_b), col_spec(COL_FB, cidx_b), col_spec(COL_I, cidx_b),
                  pl.BlockSpec((2, grp, 1, HEAD_DIM), lambda b, h, i: (0, h, 0, 0)),
                  pl.BlockSpec((2, c, 2 * c), lambda b, h, i: (0, 0, 0)),
                  pl.BlockSpec((2, 2, c, c), lambda b, h, i: (0, 0, 0, 0))],
        out_specs=(out_spec(cidx_f), out_spec(cidx_b)),
        scratch_shapes=[pltpu.VMEM((2, grp, HEAD_DIM, HEAD_DIM), F32)],
        compiler_params=_cparams(("parallel", "parallel", "arbitrary"), VMEM_LIMIT),
        name="hgrn",
    )(p, pf, p, p, pf, p, lb, tri, lvl)


def _merge_kernel(ym_ref, of_ref, ob_ref, go_ref, gm_ref, gh_ref, x_ref, mod_ref, hgg_ref, wout_ref,
                  gffn_ref, wr2_ref, wrhi_ref, br_ref, x1_ref, h2_ref, lg_ref, y_scr, *, tm, parts):
    b = pl.program_id(0)
    m = mod_ref[pl.ds(b, 1), :]
    n_rows = tm // parts

    def part(i):
        rows = slice(i * n_rows, (i + 1) * n_rows)
        for h in range(HEADS):
            sl = slice(h * HEAD_DIM, (h + 1) * HEAD_DIM)
            o = of_ref[0, h, rows, :].astype(F32) + ob_ref[0, h, rows, :].astype(F32)
            ms = jnp.mean(o * o, axis=-1, keepdims=True)
            g = go_ref[0, rows, sl].astype(F32)
            yh = o * lax.rsqrt(ms + EPS) * hgg_ref[...] * (g * _gate_sigmoid(g))
            y = (_gate_sigmoid(gm_ref[0, rows, sl].astype(F32)) * ym_ref[0, rows, sl].astype(F32)
                 + _gate_sigmoid(gh_ref[0, rows, sl].astype(F32)) * yh)
            y_scr[rows, sl] = y.astype(BF16)
        yield
        mix = _mm(y_scr[rows, :], wout_ref[...])
        yield
        x1 = x_ref[0, rows, :] + m[:, 2 * D_MODEL:3 * D_MODEL] * mix
        x1_ref[0, rows, :] = x1
        ms = jnp.mean(x1 * x1, axis=-1, keepdims=True)
        h2 = ((x1 * lax.rsqrt(ms + EPS) * gffn_ref[...]) * (1.0 + m[:, 4 * D_MODEL:5 * D_MODEL])
              + m[:, 3 * D_MODEL:4 * D_MODEL])
        h_hi = h2.astype(BF16)
        h2_ref[rows, :] = h_hi
        h_lo = (h2 - h_hi.astype(F32)).astype(BF16)
        a = _mm(h_hi, wr2_ref[...])
        lg_ref[rows, :] = (a[:, 0:N_EXPERTS] + a[:, N_EXPERTS:2 * N_EXPERTS] + _mm(h_lo, wrhi_ref[...])
                           + br_ref[...])

    _round_robin([part(i) for i in range(parts)])


def _merge(y_mla, o_f, o_b, p, x, mod, hg_g, w_out, g_ffn, w_r, b_r):
    bsz, n, d = x.shape
    tm = _tile(n, 256, 128)
    nt = n // tm
    w_r_hi, w_r_lo = _split2(w_r)
    w_r2 = jnp.concatenate([w_r_hi, w_r_lo], axis=1)
    kern = functools.partial(_merge_kernel, tm=tm, parts=2)

    def pcol(col):
        return pl.BlockSpec((1, tm, d), lambda b, t: (b, t, col // d))

    tok = lambda b, t: (b, t, 0)
    const2 = lambda b, t: (0, 0)
    return pl.pallas_call(
        kern,
        out_shape=(jax.ShapeDtypeStruct((bsz, n, d), F32),
                   jax.ShapeDtypeStruct((bsz * n, d), BF16),
                   jax.ShapeDtypeStruct((bsz * n, N_EXPERTS), F32)),
        grid=(bsz, nt),
        in_specs=[pl.BlockSpec((1, tm, d), tok),
                  pl.BlockSpec((1, HEADS, tm, HEAD_DIM), lambda b, t: (b, 0, t, 0)),
                  pl.BlockSpec((1, HEADS, tm, HEAD_DIM), lambda b, t: (b, 0, t, 0)),
                  pcol(COL_GO), pcol(COL_MM), pcol(COL_MH),
                  pl.BlockSpec((1, tm, d), tok),
                  pl.BlockSpec(mod.shape, const2),
                  pl.BlockSpec((1, HEAD_DIM), const2),
                  pl.BlockSpec((d, d), const2),
                  pl.BlockSpec((1, d), const2),
                  pl.BlockSpec((d, 2 * N_EXPERTS), const2),
                  pl.BlockSpec((d, N_EXPERTS), const2),
                  pl.BlockSpec((1, N_EXPERTS), const2)],
        out_specs=(pl.BlockSpec((1, tm, d), tok),
                   pl.BlockSpec((tm, d), lambda b, t: (b * nt + t, 0)),
                   pl.BlockSpec((tm, N_EXPERTS), lambda b, t: (b * nt + t, 0))),
        scratch_shapes=[pltpu.VMEM((tm, d), BF16)],
        compiler_params=_cparams(("parallel", "parallel"), VMEM_LIMIT),
        name="merge",
    )(y_mla, o_f, o_b, p, p, p, x, mod, hg_g, w_out, g_ffn, w_r2, w_r_hi, b_r)


def _router_kernel(lg_ref, upper_ref, idx_ref, prob_ref, rank_ref, base_ref, cnt_ref, run_ref, *, tm, n_sub):
    @pl.when(pl.program_id(0) == 0)
    def _():
        run_ref[...] = jnp.zeros_like(run_ref)

    eidx = lax.broadcasted_iota(jnp.int32, (N_EXPERTS, tm), 0)
    run = run_ref[...]
    for s in range(n_sub):
        cols = slice(s * tm, (s + 1) * tm)
        l = lg_ref[:, cols]
        vals, sel = [], []
        for _ in range(TOP_K):
            m = jnp.max(l, axis=0, keepdims=True)
            first = jnp.min(jnp.where(l == m, eidx, N_EXPERTS), axis=0, keepdims=True)
            vals.append(m)
            sel.append(first)
            l = jnp.where(eidx == first, -jnp.inf, l)
        ex = [jnp.exp(v - vals[0]) for v in vals]
        inv = 1.0 / (ex[0] + ex[1] + ex[2] + ex[3])
        onehot = jnp.zeros((N_EXPERTS, tm), F32)
        for k in range(TOP_K):
            onehot = onehot + jnp.where(eidx == sel[k], 1.0, 0.0)
        before = _mm(onehot.astype(BF16), upper_ref[...])
        base_ref[s] = run
        for k in range(TOP_K):
            idx_ref[k:k + 1, cols] = sel[k]
            prob_ref[k:k + 1, cols] = ex[k] * inv
            rank_ref[k:k + 1, cols] = jnp.sum(jnp.where(eidx == sel[k], before, 0.0), axis=0,
                                               keepdims=True).astype(jnp.int32)
        run = run + jnp.sum(onehot, axis=1, keepdims=True)
    run_ref[...] = run
    cnt_ref[...] = run


def _router(logits_t, tm):
    _, t = logits_t.shape
    n_tiles = t // tm
    n_sub = 4 if n_tiles % 4 == 0 else 1
    step = tm * n_sub
    upper = jnp.asarray(np.triu(np.ones((tm, tm), np.float32), 1), dtype=BF16)
    kern = functools.partial(_router_kernel, tm=tm, n_sub=n_sub)
    tok = pl.BlockSpec((TOP_K, step), lambda i: (0, i))
    return pl.pallas_call(
        kern,
        out_shape=(jax.ShapeDtypeStruct((TOP_K, t), jnp.int32),
                   jax.ShapeDtypeStruct((TOP_K, t), F32),
                   jax.ShapeDtypeStruct((TOP_K, t), jnp.int32),
                   jax.ShapeDtypeStruct((n_tiles, N_EXPERTS, 128), F32),
                   jax.ShapeDtypeStruct((N_EXPERTS, 128), F32)),
        grid=(n_tiles // n_sub,),
        in_specs=[pl.BlockSpec((N_EXPERTS, step), lambda i: (0, i)),
                  pl.BlockSpec((tm, tm), lambda i: (0, 0))],
        out_specs=(tok, tok, tok, pl.BlockSpec((n_sub, N_EXPERTS, 128), lambda i: (i, 0, 0)),
                   pl.BlockSpec((N_EXPERTS, 128), lambda i: (0, 0))),
        scratch_shapes=[pltpu.VMEM((N_EXPERTS, 128), F32)],
        compiler_params=_cparams(("arbitrary",), VMEM_LIMIT),
        name="router",
    )(logits_t, upper)


ROUTE_TILE = 256
ROW_ALIGN = 8
SLOT_ROWS = 48
SHORT_ROWS = 32
N_SLOTS = N_EXPERTS * SLOT_ROWS
ZERO_ROWS = FFN_ROWS + SLOT_ROWS


def _slot_ids(idx, lrank, c):
    r = lrank - c * SLOT_ROWS
    return jnp.where(jnp.logical_and(r >= 0, r < SLOT_ROWS), idx * SLOT_ROWS + r, -1)


def _pack_bf16_pairs(x):
    u = pltpu.bitcast(x, jnp.uint32)
    w = x.shape[1] // 2
    return (u[:, 0:w] >> 16) | u[:, w:2 * w]


def _unpack_bf16_pairs(u):
    lo = pltpu.bitcast(u << 16, F32).astype(BF16)
    hi = pltpu.bitcast(u & jnp.uint32(0xFFFF0000), F32).astype(BF16)
    return jnp.concatenate([lo, hi], axis=1)


def _dispatch_kernel(tstart_ref, tcnt_ref, npass_ref, fill_ref, nused_ref, idx_ref, lrank_ref, h2_ref, xs_ref,
                     xbuf, zbuf, sem, zsem, state, *, tm, n_blocks):
    j = pl.program_id(0)

    def slot_copy(slot, e, row, n_rows):
        return pltpu.make_async_copy(xbuf.at[slot, pl.ds(e * SLOT_ROWS, n_rows)],
                                     xs_ref.at[pl.ds(pl.multiple_of(row, ROW_ALIGN), n_rows)], sem.at[slot])

    def wait_pass(slot, n_full, n_short):
        def full(i, carry):
            slot_copy(slot, 0, 0, SLOT_ROWS).wait()
            return carry

        def short(i, carry):
            slot_copy(slot, 0, 0, SHORT_ROWS).wait()
            return carry
        lax.fori_loop(0, n_full, full, 0)
        lax.fori_loop(0, n_short, short, 0)

    @pl.when(j == 0)
    def _():
        zbuf[...] = jnp.zeros_like(zbuf)
        state[0] = 0
        state[1] = 0
        state[2] = 0
        for parity in range(2):
            for e in range(parity, N_EXPERTS, 2):
                pltpu.make_async_copy(zbuf, xs_ref.at[pl.ds(pl.multiple_of(fill_ref[e], ROW_ALIGN), ZERO_ROWS)],
                                      zsem).start()
            for e in range(parity, N_EXPERTS, 2):
                pltpu.make_async_copy(zbuf, xs_ref.at[pl.ds(0, ZERO_ROWS)], zsem).wait()

        def tail(i, carry):
            pltpu.make_async_copy(zbuf.at[pl.ds(0, FFN_ROWS)],
                                  xs_ref.at[pl.ds(pl.multiple_of(i * FFN_ROWS, ROW_ALIGN), FFN_ROWS)], zsem).start()
            return carry
        lax.fori_loop(nused_ref[0], n_blocks, tail, 0)

        def tail_wait(i, carry):
            pltpu.make_async_copy(zbuf.at[pl.ds(0, FFN_ROWS)], xs_ref.at[pl.ds(0, FFN_ROWS)], zsem).wait()
            return carry
        lax.fori_loop(nused_ref[0], n_blocks, tail_wait, 0)

    def one_pass(c, carry):
        rows = lax.broadcasted_iota(jnp.int32, (N_SLOTS, tm), 0)
        sel = jnp.zeros((N_SLOTS, tm), F32)
        for k in range(TOP_K):
            sel = jnp.where(rows == _slot_ids(idx_ref[k:k + 1, :], lrank_ref[k:k + 1, :], c), 1.0, sel)
        packed = _pack_bf16_pairs(_mm(sel.astype(BF16), h2_ref[...]))
        p = state[0]
        slot = p % 2
        wait_pass(1 - slot, state[1], state[2])
        xbuf[slot] = packed
        n_full = 0
        n_short = 0
        for e in range(N_EXPERTS):
            left = tcnt_ref[j * N_EXPERTS + e] - c * SLOT_ROWS
            row = tstart_ref[j * N_EXPERTS + e] + c * SLOT_ROWS
            is_full = left > SHORT_ROWS
            is_short = jnp.logical_and(left > 0, left <= SHORT_ROWS)

            @pl.when(is_full)
            def _():
                slot_copy(slot, e, row, SLOT_ROWS).start(priority=e % 2)

            @pl.when(is_short)
            def _():
                slot_copy(slot, e, row, SHORT_ROWS).start(priority=e % 2)
            n_full = n_full + is_full.astype(jnp.int32)
            n_short = n_short + is_short.astype(jnp.int32)
        state[0] = p + 1
        state[1] = n_full
        state[2] = n_short
        return carry

    lax.fori_loop(0, npass_ref[j], one_pass, 0)

    @pl.when(j == pl.num_programs(0) - 1)
    def _():
        wait_pass((state[0] - 1) % 2, state[1], state[2])


def _dispatch(tile_start, tile_count, n_pass, fill_start, n_used, idx_t, lrank_t, h2, n_rows):
    t, d = h2.shape
    tm = ROUTE_TILE
    w = d // 2
    n_blocks = n_rows // FFN_ROWS
    kern = functools.partial(_dispatch_kernel, tm=tm, n_blocks=n_blocks)
    tok = pl.BlockSpec((TOP_K, tm), lambda i, *_: (0, i))
    grid_spec = pltpu.PrefetchScalarGridSpec(
        num_scalar_prefetch=5,
        grid=(t // tm,),
        in_specs=[tok, tok, pl.BlockSpec((tm, d), lambda i, *_: (i, 0))],
        out_specs=pl.BlockSpec(memory_space=pl.ANY),
        scratch_shapes=[pltpu.VMEM((2, N_SLOTS, w), jnp.uint32), pltpu.VMEM((ZERO_ROWS, w), jnp.uint32),
                        pltpu.SemaphoreType.DMA((2,)), pltpu.SemaphoreType.DMA, pltpu.SMEM((3,), jnp.int32)],
    )
    return pl.pallas_call(
        kern,
        out_shape=jax.ShapeDtypeStruct((n_rows, w), jnp.uint32),
        grid_spec=grid_spec,
        compiler_params=_cparams(("arbitrary",), VMEM_LIMIT),
        name="dispatch",
    )(tile_start, tile_count, n_pass, fill_start, n_used, idx_t, lrank_t, h2)


GU_GROUP = 256


def _ffn_kernel(be_ref, nu_ref, x_ref, wgu_ref, bgu_ref, wd_ref, bd_ref, perm_ref, y_ref, wgu_s, wd_s):
    i = pl.program_id(0)
    live = i < nu_ref[0]
    new_expert = jnp.logical_or(i == 0, be_ref[i] != be_ref[jnp.maximum(i - 1, 0)])

    @pl.when(jnp.logical_and(live, new_expert))
    def _():
        for g in range(2 * D_EXPERT // GU_GROUP):
            sl = slice(g * GU_GROUP, (g + 1) * GU_GROUP)
            wgu_s[:, sl] = _mm(wgu_ref[0, :, sl].astype(BF16), perm_ref[...]).astype(BF16)
        wd_s[...] = wd_ref[0].astype(BF16)

    @pl.when(live)
    def _():
        x = _unpack_bf16_pairs(x_ref[...])
        gu = _mm(x, wgu_s[...]) + bgu_ref[0]
        half = GU_GROUP // 2
        n_grp = 2 * D_EXPERT // GU_GROUP
        glu = jnp.concatenate([gu[:, g * GU_GROUP:g * GU_GROUP + half] for g in range(n_grp)], axis=1)
        lin = jnp.concatenate([gu[:, g * GU_GROUP + half:(g + 1) * GU_GROUP] for g in range(n_grp)], axis=1)
        glu = jnp.minimum(glu, SWIGLU_LIMIT)
        lin = jnp.clip(lin, -SWIGLU_LIMIT, SWIGLU_LIMIT)
        act = glu * _sigmoid(SWIGLU_ALPHA * glu) * (lin + 1.0)
        y = _mm(act.astype(BF16), wd_s[...]) + bd_ref[0]
        y_ref[...] = _pack_bf16_pairs(y.astype(BF16).astype(F32))

    @pl.when(jnp.logical_not(live))
    def _():
        y_ref[...] = jnp.zeros_like(y_ref)


def _ffn(block_e, n_used, xs, w_gu, b_gu, w_d, b_d):
    n_rows, w = xs.shape
    bm = FFN_ROWS
    nblk = n_rows // bm
    d = w * 2
    half = GU_GROUP // 2
    perm = np.zeros((GU_GROUP, GU_GROUP), np.float32)
    perm[2 * np.arange(half), np.arange(half)] = 1.0
    perm[2 * np.arange(half) + 1, half + np.arange(half)] = 1.0

    def xmap(i, be, nu):
        return (jnp.minimum(i, nu[0] - 1), 0)

    def wmap(i, be, nu):
        return (be[i], 0, 0)

    grid_spec = pltpu.PrefetchScalarGridSpec(
        num_scalar_prefetch=2,
        grid=(nblk,),
        in_specs=[pl.BlockSpec((bm, w), xmap),
                  pl.BlockSpec((1, d, 2 * D_EXPERT), wmap),
                  pl.BlockSpec((1, 1, 2 * D_EXPERT), wmap),
                  pl.BlockSpec((1, D_EXPERT, d), wmap),
                  pl.BlockSpec((1, 1, d), wmap),
                  pl.BlockSpec((GU_GROUP, GU_GROUP), lambda i, be, nu: (0, 0))],
        out_specs=pl.BlockSpec((bm, w), lambda i, be, nu: (i, 0)),
        scratch_shapes=[pltpu.VMEM((d, 2 * D_EXPERT), BF16), pltpu.VMEM((D_EXPERT, d), BF16)],
    )
    return pl.pallas_call(
        _ffn_kernel,
        out_shape=jax.ShapeDtypeStruct((n_rows, w), jnp.uint32),
        grid_spec=grid_spec,
        compiler_params=_cparams(("arbitrary",), VMEM_LIMIT),
        name="ffn",
    )(block_e, n_used, xs, w_gu, b_gu, w_d, b_d, jnp.asarray(perm, dtype=BF16))


def _combine_kernel(tstart_ref, tcnt_ref, npass_ref, idx_ref, lrank_ref, p_ref, y_ref, x1_ref, mod_ref, g_ref, o_ref,
                    ybuf, sem, *, tm, nt):
    b = pl.program_id(0)
    j = b * nt + pl.program_id(1)
    n_steps = pl.num_programs(0) * nt
    slot = j % 2

    def slot_copy(s, e, row, n_rows):
        return pltpu.make_async_copy(y_ref.at[pl.ds(pl.multiple_of(row, ROW_ALIGN), n_rows)],
                                     ybuf.at[s, pl.ds(e * SLOT_ROWS, n_rows)], sem.at[s])

    def per_slot(tile, c, s, act):
        for e in range(N_EXPERTS):
            left = tcnt_ref[tile * N_EXPERTS + e] - c * SLOT_ROWS
            row = tstart_ref[tile * N_EXPERTS + e] + c * SLOT_ROWS

            @pl.when(left > SHORT_ROWS)
            def _():
                act(slot_copy(s, e, row, SLOT_ROWS), e)

            @pl.when(jnp.logical_and(left > 0, left <= SHORT_ROWS))
            def _():
                act(slot_copy(s, e, row, SHORT_ROWS), e)

    def fetch(tile, c, s):
        per_slot(tile, c, s, lambda copy, e: copy.start(priority=e % 2))

    def arrive(tile, c, s):
        per_slot(tile, c, s, lambda copy, e: copy.wait())

    @pl.when(j == 0)
    def _():
        ybuf[...] = jnp.zeros_like(ybuf)
        fetch(0, 0, 0)

    @pl.when(j + 1 < n_steps)
    def _():
        fetch(j + 1, 0, 1 - slot)

    def weights(c):
        cols = lax.broadcasted_iota(jnp.int32, (tm, N_SLOTS), 1)
        w = jnp.zeros((tm, N_SLOTS), F32)
        for k in range(TOP_K):
            w = jnp.where(cols == _slot_ids(idx_ref[:, k:k + 1], lrank_ref[:, k:k + 1], c), p_ref[:, k:k + 1], w)
        return w.astype(BF16)

    arrive(j, 0, slot)
    acc = _mm(weights(0), _unpack_bf16_pairs(ybuf[slot]))

    def extra_pass(c, acc):
        fetch(j, c, 2)
        arrive(j, c, 2)
        return acc + _mm(weights(c), _unpack_bf16_pairs(ybuf[2]))

    acc = lax.fori_loop(1, npass_ref[j], extra_pass, acc)
    m = mod_ref[pl.ds(b, 1), :]
    xo = x1_ref[0] + m[:, 5 * D_MODEL:6 * D_MODEL] * acc
    ms = jnp.mean(xo * xo, axis=-1, keepdims=True)
    o_ref[0] = xo * lax.rsqrt(ms + EPS) * g_ref[...]


def _combine(tile_start, tile_count, n_pass, idx_c, lrank_c, prob_c, y, x1, mod, g_fin):
    bsz, n, d = x1.shape
    tm = ROUTE_TILE
    nt = n // tm
    kern = functools.partial(_combine_kernel, tm=tm, nt=nt)
    tok = pl.BlockSpec((tm, TOP_K), lambda b, t, *_: (b * nt + t, 0))
    grid_spec = pltpu.PrefetchScalarGridSpec(
        num_scalar_prefetch=3,
        grid=(bsz, nt),
        in_specs=[tok, tok, tok,
                  pl.BlockSpec(memory_space=pl.ANY),
                  pl.BlockSpec((1, tm, d), lambda b, t, *_: (b, t, 0)),
                  pl.BlockSpec(mod.shape, lambda b, t, *_: (0, 0)),
                  pl.BlockSpec((1, d), lambda b, t, *_: (0, 0))],
        out_specs=pl.BlockSpec((1, tm, d), lambda b, t, *_: (b, t, 0)),
        scratch_shapes=[pltpu.VMEM((3, N_SLOTS, d // 2), jnp.uint32), pltpu.SemaphoreType.DMA((3,))],
    )
    return pl.pallas_call(
        kern,
        out_shape=jax.ShapeDtypeStruct((bsz, n, d), F32),
        grid_spec=grid_spec,
        compiler_params=_cparams(("arbitrary", "arbitrary"), VMEM_LIMIT),
        name="combine",
    )(tile_start, tile_count, n_pass, idx_c, lrank_c, prob_c, y, x1, mod, g_fin)


def _prep_w_in(w_in):
    q_a, kv, hq, ff, fb, hi, go, mm, mh = jnp.split(
        w_in, np.cumsum((MLA_Q_RANK, MLA_KV_RANK + MLA_ROPE) + (D_MODEL,) * 6).tolist(), axis=1)
    c_kv, kr = kv[:, :MLA_KV_RANK], kv[:, MLA_KV_RANK:]
    half = MLA_ROPE // 2
    kr_sw = jnp.concatenate([-kr[:, half:], kr[:, :half]], axis=1)
    w16 = jnp.concatenate([hq, hi, go, mm, mh, q_a, c_kv, kr, kr_sw], axis=1)
    w16 = jnp.pad(w16, ((0, 0), (0, N_PROJ16 - w16.shape[1])))
    return w16.astype(BF16), jnp.concatenate([ff, fb], axis=1).astype(BF16)


def _prep_w_q(w_q_b):
    w = w_q_b.reshape(MLA_Q_RANK, HEADS, MLA_QK)
    w = jnp.pad(w, ((0, 0), (0, 0), (0, MLA_QK_PAD - MLA_QK)))
    return w.reshape(MLA_Q_RANK, HEADS * MLA_QK_PAD).astype(BF16)


def _prep_w_kv(w_kv_b):
    w = w_kv_b.reshape(MLA_KV_RANK, HEADS, 2 * HEAD_DIM)
    k = w[:, :, :HEAD_DIM].reshape(MLA_KV_RANK, HEADS * HEAD_DIM)
    v = w[:, :, HEAD_DIM:].reshape(MLA_KV_RANK, HEADS * HEAD_DIM)
    return jnp.concatenate([k, v], axis=1).astype(BF16)


def _rope_tables(n_lat, n_ctx):
    f32 = np.float32
    rows = n_lat // GRID_W
    row = np.repeat(np.arange(rows), GRID_W).astype(f32)
    col = np.tile(np.arange(GRID_W), rows).astype(f32)
    n_freq = MLA_ROPE // 4
    inv = (f32(ROPE_BASE) ** (-np.arange(n_freq, dtype=f32) / f32(n_freq))).astype(f32)
    ang = np.concatenate([row[:, None] * inv, col[:, None] * inv], axis=-1).astype(f32)
    cos, sin = np.cos(ang).astype(f32), np.sin(ang).astype(f32)
    pad = np.zeros((n_lat, HEAD_DIM - MLA_ROPE), f32)
    cos_q = np.concatenate([cos, cos, pad], axis=1)
    sin_q = np.concatenate([-sin, sin, pad], axis=1)
    cos_k = np.concatenate([cos, cos, pad], axis=1)
    sin_k = np.concatenate([sin, sin, pad], axis=1)
    ctx_cos = np.concatenate([np.ones((n_ctx, MLA_ROPE), f32), np.zeros((n_ctx, HEAD_DIM - MLA_ROPE), f32)], axis=1)
    cos_k = np.concatenate([cos_k, ctx_cos], axis=0)
    sin_k = np.concatenate([sin_k, np.zeros((n_ctx, HEAD_DIM), f32)], axis=0)
    return jnp.asarray(cos_q), jnp.asarray(sin_q), jnp.asarray(cos_k), jnp.asarray(sin_k)


def kernel(x, c, ctx, c_ctx, w_mod, b_mod, norm_mix_g, w_in, mla_q_norm_g, w_q_b, mla_kv_norm_g, w_kv_b,
           hg_lb_logits, hg_norm_g, w_out, norm_ffn_g, w_router, b_router, w_gate_up, b_gate_up, w_down,
           b_down, final_norm_g):
    bsz, n_lat, d = x.shape
    n_ctx = ctx.shape[1]
    assert d == D_MODEL and w_mod.shape[0] == 1
    assert n_lat % HG_CHUNK == 0 and n_ctx % HG_CHUNK == 0 and n_lat % GRID_W == 0
    t_tok = bsz * n_lat

    mod_rows = -(-(bsz + 1) // 8) * 8
    cc = jnp.concatenate([c, c_ctx[None, :], jnp.zeros((mod_rows - bsz - 1, d), F32)], axis=0)
    mod = _modulation(cc, w_mod[0], b_mod[0][None, :])

    w16, w32 = _prep_w_in(w_in[0])
    p, pf = _inproj(x, ctx, mod, norm_mix_g[0][None, :], w16, w32)

    cos_q, sin_q, cos_k, sin_k = _rope_tables(n_lat, n_ctx)
    q = _mla_q(p, mla_q_norm_g[0][None, :], _prep_w_q(w_q_b[0]), cos_q, sin_q, n_lat)
    k, v = _mla_kv(p, mla_kv_norm_g[0][None, :], _prep_w_kv(w_kv_b[0]), cos_k, sin_k)
    y_mla = _attention(q, k, v)

    lb = jax.nn.softmax(hg_lb_logits.astype(F32), axis=1)[:, 0, :]
    o_f, o_b = _hgrn_scans(p, pf, lb.reshape(2, HEADS, 1, HEAD_DIM), n_lat)

    x1, h2, logits = _merge(y_mla, o_f, o_b, p, x, mod, hg_norm_g[0][None, :], w_out[0].astype(BF16),
                            norm_ffn_g[0][None, :], w_router[0], b_router[0][None, :])
    logits_t = logits.T

    assert n_lat % ROUTE_TILE == 0
    idx_t, prob_t, lrank_t, base, cnt = _router(logits_t, ROUTE_TILE)

    n_tiles = t_tok // ROUTE_TILE
    counts = cnt[:, 0].astype(jnp.int32)
    base = base[:, :, 0].astype(jnp.int32)
    tile_count = jnp.concatenate([base[1:] - base[:-1], (counts - base[-1])[None, :]], axis=0)
    seg = (tile_count + ROW_ALIGN - 1) // ROW_ALIGN * ROW_ALIGN
    seg_end = jnp.cumsum(seg, axis=0)
    rows_e = seg_end[-1]
    padded = (rows_e + SLOT_ROWS + FFN_ROWS - 1) // FFN_ROWS * FFN_ROWS
    pad_end = jnp.cumsum(padded)
    pad_start = pad_end - padded
    max_rows = t_tok * TOP_K + N_EXPERTS * (n_tiles * (ROW_ALIGN - 1) + SLOT_ROWS + FFN_ROWS - 1)
    n_rows = -(-max_rows // FFN_ROWS) * FFN_ROWS
    nblk = n_rows // FFN_ROWS
    block_start = jnp.arange(nblk, dtype=jnp.int32) * FFN_ROWS
    block_e = jnp.minimum(jnp.sum(pad_end[None, :] <= block_start[:, None], axis=1), N_EXPERTS - 1).astype(jnp.int32)
    n_used = (pad_end[-1:] // FFN_ROWS).astype(jnp.int32)
    tile_start = (pad_start[None, :] + seg_end - seg).reshape(-1).astype(jnp.int32)
    n_pass = jnp.maximum(-(-jnp.max(tile_count, axis=1) // SLOT_ROWS), 1).astype(jnp.int32)
    tile_count = tile_count.reshape(-1)
    fill_start = jnp.minimum(pad_start + rows_e, n_rows - ZERO_ROWS).astype(jnp.int32)

    xs = _dispatch(tile_start, tile_count, n_pass, fill_start, n_used, idx_t, lrank_t, h2, n_rows)

    n_grp = 2 * D_EXPERT // GU_GROUP
    b_gu = b_gate_up[0].reshape(N_EXPERTS, n_grp, GU_GROUP // 2, 2).transpose(0, 1, 3, 2).reshape(N_EXPERTS, 1, 2 * D_EXPERT)
    y = _ffn(block_e, n_used, xs, w_gate_up[0], b_gu, w_down[0], b_down[0][:, None, :])

    return _combine(tile_start, tile_count, n_pass, idx_t.T, lrank_t.T, prob_t.T, y, x1, mod, final_norm_g[None, :])
```

```python
import functools

import jax
import jax.numpy as jnp
import numpy as np
from jax import lax
from jax.experimental import pallas as pl
from jax.experimental.pallas import tpu as pltpu

F32 = jnp.float32
BF16 = jnp.bfloat16

D_MODEL = 1024
EPS = 1e-6
LOG2_E = 1.4426950408889634
N_MOD = 6
GRID_W = 64
ROPE_BASE = 10000.0

HEADS = 8
HEAD_DIM = 128
MLA_ROPE = 64
MLA_QK = HEAD_DIM + MLA_ROPE
MLA_QK_PAD = 256
MLA_Q_RANK = 768
MLA_KV_RANK = 256
ROPE_HALF = MLA_ROPE // 2
QA_PIECE = 256

N_EXPERTS = 32
TOP_K = 4
D_EXPERT = 1024
SWIGLU_LIMIT = 7.0
SWIGLU_ALPHA = 1.702

HG_CHUNK = 128
FFN_ROWS = 512

COL_HQ, COL_I, COL_GO, COL_MM, COL_MH = (i * D_MODEL for i in range(5))
COL_QA = 5 * D_MODEL
COL_CKV = COL_QA + MLA_Q_RANK
COL_KR = COL_CKV + MLA_KV_RANK
PROJ16_TN = 1280
N_PROJ16 = -(-(COL_KR + 2 * MLA_ROPE) // PROJ16_TN) * PROJ16_TN
COL_FF, COL_FB = 0, D_MODEL
N_PROJ32 = 2 * D_MODEL
PROJ32_TN = 1024

VMEM_LIMIT = 56 * 1024 * 1024


def _cparams(sem, vmem=None):
    return pltpu.CompilerParams(dimension_semantics=sem, vmem_limit_bytes=vmem)


def _tile(n, pref, mult=8):
    best = None
    for t in range(mult, min(n, pref) + 1, mult):
        if n % t == 0:
            best = t
    assert best is not None, (n, pref, mult)
    return best


def _nt(a, b):
    return lax.dot_general(a, b, (((1,), (1,)), ((), ())), preferred_element_type=F32)


def _mm(a, b):
    return jnp.dot(a, b, preferred_element_type=F32)


def _split2(a):
    hi = a.astype(BF16)
    lo = (a - hi.astype(F32)).astype(BF16)
    return hi, lo


def _sigmoid(x):
    return 1.0 / (1.0 + jnp.exp(-x))


def _gate_sigmoid(x):
    return 0.5 * jnp.tanh(0.5 * x) + 0.5


def _mod_kernel(c_ref, w_ref, b_ref, o_ref):
    c = c_ref[...]
    s = c * _sigmoid(c)
    s_hi, s_lo = _split2(s)
    w_hi, w_lo = _split2(w_ref[...])
    o_ref[...] = _mm(s_hi, w_hi) + _mm(s_lo, w_hi) + _mm(s_hi, w_lo) + b_ref[...]


def _modulation(cc, w_mod, b_mod):
    r, d = cc.shape
    n = w_mod.shape[1]
    tn = _tile(n, 1536, 128)
    return pl.pallas_call(
        _mod_kernel,
        out_shape=jax.ShapeDtypeStruct((r, n), F32),
        grid=(n // tn,),
        in_specs=[pl.BlockSpec((r, d), lambda j: (0, 0)),
                  pl.BlockSpec((d, tn), lambda j: (0, j)),
                  pl.BlockSpec((1, tn), lambda j: (0, j))],
        out_specs=pl.BlockSpec((r, tn), lambda j: (0, j)),
        compiler_params=_cparams(("parallel",), VMEM_LIMIT),
        name="mod",
    )(cc, w_mod, b_mod)


INPROJ_GROUP = 2


def _inproj_kernel(x_ref, c_ref, mod_ref, g_ref, w16_ref, w32_ref, o16_ref, o32_ref, hn_ref, *, n_lat, tm, nt, grp,
                   ctx_row, n16):
    j = pl.program_id(1)
    r = pl.program_id(2)
    b = pl.program_id(0) * grp + r // nt
    t = r % nt

    @pl.when(j == 0)
    def _():
        n_lat_last = n_lat - (nt - 1) * tm
        row = t * tm + lax.broadcasted_iota(jnp.int32, (tm, 1), 0)
        is_ctx = row >= n_lat
        ctx_rows = jnp.concatenate([jnp.zeros((n_lat_last, D_MODEL), F32), c_ref[0]], axis=0)
        m_lat = mod_ref[pl.ds(b, 1), :]
        m_ctx = mod_ref[pl.ds(ctx_row, 1), :]
        half = tm // 2

        def part(i):
            rows = slice(i * half, (i + 1) * half)
            x = jnp.where(is_ctx[rows], ctx_rows[rows], x_ref[0, rows, :])
            ms = jnp.mean(x * x, axis=-1, keepdims=True)
            y = x * lax.rsqrt(ms + EPS) * g_ref[...]
            shift = jnp.where(is_ctx[rows], m_ctx[:, 0:D_MODEL], m_lat[:, 0:D_MODEL])
            scale = jnp.where(is_ctx[rows], m_ctx[:, D_MODEL:2 * D_MODEL], m_lat[:, D_MODEL:2 * D_MODEL])
            hn_ref[r, rows, :] = (y * (1.0 + scale) + shift).astype(BF16)
            yield
            o16_ref[0, rows, :] = _mm(hn_ref[r, rows, :], w16_ref[...]).astype(BF16)

        _round_robin([part(i) for i in range(2)])

    @pl.when(jnp.logical_and(j > 0, j < n16))
    def _():
        o16_ref[0] = _mm(hn_ref[r], w16_ref[...]).astype(BF16)

    @pl.when(j >= n16)
    def _():
        o32_ref[0] = _mm(hn_ref[r], w32_ref[...])


def _inproj(x, ctx, mod, g, w16, w32):
    bsz, n_lat, d = x.shape
    n_ctx = ctx.shape[1]
    rows = n_lat + n_ctx
    tm = _tile(rows, 1152)
    nt = rows // tm
    assert n_ctx <= tm <= n_lat, (n_ctx, tm, n_lat)
    grp = INPROJ_GROUP if bsz % INPROJ_GROUP == 0 else 1
    n_r = grp * nt
    n16 = N_PROJ16 // PROJ16_TN
    n32 = N_PROJ32 // PROJ32_TN

    def row_tile(g_, r):
        return g_ * grp + r // nt, r % nt

    def x_map(g_, j, r):
        b, t = row_tile(g_, jnp.where(j == 0, r, n_r - 1))
        return b, t, 0

    def o16_map(g_, j, r):
        b, t = row_tile(g_, jnp.where(j < n16, r, n_r - 1))
        return b, t, jnp.minimum(j, n16 - 1)

    def o32_map(g_, j, r):
        b, t = row_tile(g_, jnp.where(j >= n16, r, 0))
        return b, t, jnp.maximum(j - n16, 0)

    kern = functools.partial(_inproj_kernel, n_lat=n_lat, tm=tm, nt=nt, grp=grp, ctx_row=bsz, n16=n16)
    return pl.pallas_call(
        kern,
        out_shape=(jax.ShapeDtypeStruct((bsz, rows, N_PROJ16), BF16),
                   jax.ShapeDtypeStruct((bsz, rows, N_PROJ32), F32)),
        grid=(bsz // grp, n16 + n32, n_r),
        in_specs=[pl.BlockSpec((1, tm, d), x_map),
                  pl.BlockSpec((1, n_ctx, d), lambda g_, j, r: (x_map(g_, j, r)[0], 0, 0)),
                  pl.BlockSpec(mod.shape, lambda g_, j, r: (0, 0)),
                  pl.BlockSpec((1, d), lambda g_, j, r: (0, 0)),
                  pl.BlockSpec((d, PROJ16_TN), lambda g_, j, r: (0, jnp.minimum(j, n16 - 1))),
                  pl.BlockSpec((d, PROJ32_TN), lambda g_, j, r: (0, jnp.maximum(j - n16, 0)))],
        out_specs=(pl.BlockSpec((1, tm, PROJ16_TN), o16_map),
                   pl.BlockSpec((1, tm, PROJ32_TN), o32_map)),
        scratch_shapes=[pltpu.VMEM((n_r, tm, d), BF16)],
        compiler_params=_cparams(("parallel", "arbitrary", "arbitrary"), VMEM_LIMIT),
        name="inproj",
    )(x, ctx, mod, g, w16, w32)


def _mlaq_kernel(a0_ref, a1_ref, a2_ref, g_ref, w_ref, cos_ref, sin_ref, o_ref, *, tm, n_parts):
    n_rows = tm // n_parts
    scale = MLA_QK ** -0.5 * LOG2_E

    def part(i):
        rows = slice(i * n_rows, (i + 1) * n_rows)
        parts = [a_ref[0, rows, :].astype(F32) for a_ref in (a0_ref, a1_ref, a2_ref)]
        ss = sum(jnp.sum(p * p, axis=-1, keepdims=True) for p in parts)
        r = lax.rsqrt(ss * (1.0 / MLA_Q_RANK) + EPS)
        acc = None
        for j, p in enumerate(parts):
            hj = (p * r * g_ref[:, j * QA_PIECE:(j + 1) * QA_PIECE]).astype(BF16)
            d = _mm(hj, w_ref[j * QA_PIECE:(j + 1) * QA_PIECE, :])
            acc = d if acc is None else acc + d
        yield
        lane = lax.broadcasted_iota(jnp.int32, (n_rows, HEAD_DIM), 1)
        cos = cos_ref[rows, :]
        sin = sin_ref[rows, :]
        for h in range(HEADS):
            nope = acc[:, h * MLA_QK_PAD:h * MLA_QK_PAD + HEAD_DIM]
            rp = acc[:, h * MLA_QK_PAD + HEAD_DIM:(h + 1) * MLA_QK_PAD]
            swapped = jnp.where(lane < ROPE_HALF, pltpu.roll(rp, HEAD_DIM - ROPE_HALF, 1),
                                pltpu.roll(rp, ROPE_HALF, 1))
            rot = rp * cos + swapped * sin
            o_ref[0, h, rows, 0:HEAD_DIM] = (nope * scale).astype(BF16)
            o_ref[0, h, rows, HEAD_DIM:MLA_QK_PAD] = (rot * scale).astype(BF16)

    _round_robin([part(i) for i in range(n_parts)])


def _mla_q(p, g, w, cos_q, sin_q, n_lat):
    bsz = p.shape[0]
    tm = _tile(n_lat, 512)
    cb = COL_QA // QA_PIECE
    kern = functools.partial(_mlaq_kernel, tm=tm, n_parts=2 if tm % 32 == 0 else 1)
    return pl.pallas_call(
        kern,
        out_shape=jax.ShapeDtypeStruct((bsz, HEADS, n_lat, MLA_QK_PAD), BF16),
        grid=(bsz, n_lat // tm),
        in_specs=[pl.BlockSpec((1, tm, QA_PIECE), lambda b, t: (b, t, cb)),
                  pl.BlockSpec((1, tm, QA_PIECE), lambda b, t: (b, t, cb + 1)),
                  pl.BlockSpec((1, tm, QA_PIECE), lambda b, t: (b, t, cb + 2)),
                  pl.BlockSpec((1, MLA_Q_RANK), lambda b, t: (0, 0)),
                  pl.BlockSpec(w.shape, lambda b, t: (0, 0)),
                  pl.BlockSpec((tm, HEAD_DIM), lambda b, t: (t, 0)),
                  pl.BlockSpec((tm, HEAD_DIM), lambda b, t: (t, 0))],
        out_specs=pl.BlockSpec((1, HEADS, tm, MLA_QK_PAD), lambda b, t: (b, 0, t, 0)),
        compiler_params=_cparams(("parallel", "parallel"), VMEM_LIMIT),
        name="mla_q",
    )(p, p, p, g, w, cos_q, sin_q)


def _mlakv_kernel(c_ref, kr_ref, g_ref, w_ref, cos_ref, sin_ref, k_ref, v_ref):
    c = c_ref[0].astype(F32)
    ms = jnp.mean(c * c, axis=-1, keepdims=True)
    hn = (c * lax.rsqrt(ms + EPS) * g_ref[...]).astype(BF16)
    kv = _mm(hn, w_ref[...])
    grp = kr_ref[0].astype(F32)
    rot = (grp * cos_ref[...] + pltpu.roll(grp, MLA_ROPE, 1) * sin_ref[...]).astype(BF16)
    for h in range(HEADS):
        k_ref[0, h, :, 0:HEAD_DIM] = kv[:, h * HEAD_DIM:(h + 1) * HEAD_DIM].astype(BF16)
        k_ref[0, h, :, HEAD_DIM:MLA_QK_PAD] = rot
        v_ref[0, h] = kv[:, D_MODEL + h * HEAD_DIM:D_MODEL + (h + 1) * HEAD_DIM].astype(BF16)


def _mla_kv(p, g, w, cos_k, sin_k):
    bsz, rows, _ = p.shape
    tm = _tile(rows, 768)
    return pl.pallas_call(
        _mlakv_kernel,
        out_shape=(jax.ShapeDtypeStruct((bsz, HEADS, rows, MLA_QK_PAD), BF16),
                   jax.ShapeDtypeStruct((bsz, HEADS, rows, HEAD_DIM), BF16)),
        grid=(bsz, rows // tm),
        in_specs=[pl.BlockSpec((1, tm, MLA_KV_RANK), lambda b, t: (b, t, COL_CKV // MLA_KV_RANK)),
                  pl.BlockSpec((1, tm, 128), lambda b, t: (b, t, COL_KR // 128)),
                  pl.BlockSpec((1, MLA_KV_RANK), lambda b, t: (0, 0)),
                  pl.BlockSpec(w.shape, lambda b, t: (0, 0)),
                  pl.BlockSpec((tm, 128), lambda b, t: (t, 0)),
                  pl.BlockSpec((tm, 128), lambda b, t: (t, 0))],
        out_specs=(pl.BlockSpec((1, HEADS, tm, MLA_QK_PAD), lambda b, t: (b, 0, t, 0)),
                   pl.BlockSpec((1, HEADS, tm, HEAD_DIM), lambda b, t: (b, 0, t, 0))),
        compiler_params=_cparams(("parallel", "parallel"), VMEM_LIMIT),
        name="mla_kv",
    )(p, p, g, w, cos_k, sin_k)


def _attn_kernel(q_ref, k_ref, v_ref, o_ref, *, tq, sub):
    k = k_ref[0, 0]
    v = v_ref[0, 0]
    v1 = jnp.concatenate([v, jnp.ones_like(v)], axis=1)

    def sub_tile(r):
        rows = slice(r * sub, (r + 1) * sub)
        s = _nt(q_ref[0, 0, rows, :], k)
        yield
        m = jnp.max(s, axis=-1, keepdims=True)
        p = jnp.exp2(s - m).astype(BF16)
        yield
        ol = _mm(p, v1)
        o_ref[0, rows, :] = (ol[:, 0:HEAD_DIM] * (1.0 / ol[:, HEAD_DIM:HEAD_DIM + 1])).astype(BF16)

    _round_robin([sub_tile(r) for r in range(tq // sub)])


def _attention(q, k, v):
    bsz, _, n, _ = q.shape
    m = k.shape[2]
    tq = _tile(n, 2048)
    sub = _tile(tq, 512)
    kern = functools.partial(_attn_kernel, tq=tq, sub=sub)
    return pl.pallas_call(
        kern,
        out_shape=jax.ShapeDtypeStruct((bsz, n, HEADS * HEAD_DIM), BF16),
        grid=(bsz, HEADS, n // tq),
        in_specs=[pl.BlockSpec((1, 1, tq, MLA_QK_PAD), lambda b, h, t: (b, h, t, 0)),
                  pl.BlockSpec((1, 1, m, MLA_QK_PAD), lambda b, h, t: (b, h, 0, 0)),
                  pl.BlockSpec((1, 1, m, HEAD_DIM), lambda b, h, t: (b, h, 0, 0))],
        out_specs=pl.BlockSpec((1, tq, HEAD_DIM), lambda b, h, t: (b, t, h)),
        compiler_params=_cparams(("parallel", "parallel", "arbitrary"), VMEM_LIMIT),
        name="attention",
    )(q, k, v)


def _level_ref(cum, blk, reverse):
    c = cum.shape[0]
    half = blk // 2
    r = half if reverse else half - 1
    if blk >= 8:
        x = cum.reshape(c // blk, blk, HEAD_DIM)
        e = jnp.broadcast_to(x[:, r:r + 1, :], x.shape)
        return e.reshape(c, HEAD_DIM)
    x = cum.reshape(c // 8, 8, HEAD_DIM)
    sub = lax.broadcasted_iota(jnp.int32, x.shape, 1)
    e = None
    for jb in range(8 // blk):
        cand = jnp.broadcast_to(x[:, jb * blk + r:jb * blk + r + 1, :], x.shape)
        e = cand if e is None else jnp.where(sub >= jb * blk, cand, e)
    return e.reshape(c, HEAD_DIM)


HG_SUB = 32
HG_SUB_LEVEL = HG_SUB.bit_length() - 1
HG_MAX_EXP2 = 100.0


def _hgrn_state(z, v, lb, tri, st, *, chunk, reverse):
    f = lb + (1.0 - lb) * _sigmoid(z)
    kb = (1.0 - f).astype(BF16)
    g = jnp.log(f) * LOG2_E

    g_hi, g_lo = _split2(g)
    yield
    cum = _mm(tri, jnp.concatenate([g_hi, g_lo], axis=0))
    yield
    last = 0 if reverse else chunk - 1
    tot = cum[last:last + 1, :]
    kt = kb * jnp.exp2(tot - cum).astype(BF16)
    vt = v.astype(F32).T.astype(BF16)
    yield
    st_new = st * jnp.exp2(tot) + _mm(vt, kt)
    return kb, cum, st_new


def _sub_block_decay(cum, reverse):
    c = cum.shape[0]
    x = cum.reshape(c // HG_SUB, HG_SUB, HEAD_DIM)
    zero = jnp.zeros((1, 1, HEAD_DIM), F32)
    if reverse:
        edge = jnp.concatenate([x[1:, 0:1, :], zero], axis=0)
    else:
        edge = jnp.concatenate([zero, x[:-1, HG_SUB - 1:HG_SUB, :]], axis=0)
    return (x - edge).reshape(c, HEAD_DIM)


def _hgrn_readout(q, v, kb, cum, sub, lvl, st, *, chunk, reverse, shared):
    q = q.astype(F32)
    qb = (q * _gate_sigmoid(q) * (HEAD_DIM ** -0.5)).astype(BF16)
    yield
    if shared:
        att = jnp.where(lvl == 0, _nt(qb * jnp.exp2(sub).astype(BF16), kb * jnp.exp2(-sub).astype(BF16)).astype(BF16),
                        jnp.zeros((), BF16))
        first = HG_SUB_LEVEL + 1
    else:
        att = jnp.where(lvl == 0, _nt(qb, kb).astype(BF16), jnp.zeros((), BF16))
        first = 1
    for lv in range(first, chunk.bit_length()):
        zrel = cum - _level_ref(cum, 1 << lv, reverse)
        neg_abs = pltpu.bitcast(pltpu.bitcast(zrel, jnp.uint32) | jnp.uint32(0x80000000), F32)
        e = jnp.exp2(neg_abs).astype(BF16)
        yield
        att = jnp.where(lvl == lv, _nt(qb * e, kb * e).astype(BF16), att)
        yield
    return _nt(qb * jnp.exp2(cum).astype(BF16), st.astype(BF16)) + _mm(att, v.astype(BF16))


def _round_robin(gens):
    out = [None] * len(gens)
    active = list(range(len(gens)))
    while active:
        for i in list(active):
            try:
                next(gens[i])
            except StopIteration as e:
                out[i] = e.value
                active.remove(i)
    return out


def _hgrn_kernel(qf_ref, zf_ref, if_ref, qb_ref, zb_ref, ib_ref, lb_ref, tri_ref, lvl_ref, of_ref, ob_ref, st_ref,
                 *, chunk, group, n_ctx_chunks):
    step = pl.program_id(2)

    @pl.when(step == 0)
    def _():
        st_ref[...] = jnp.zeros_like(st_ref)

    ins = ((qf_ref, zf_ref, if_ref), (qb_ref, zb_ref, ib_ref))
    outs = (of_ref, ob_ref)
    chains = [(d, j) for j in range(group) for d in range(2)]

    def cols(j):
        return slice(j * HEAD_DIM, (j + 1) * HEAD_DIM)

    def advance(want_out):
        old = [st_ref[d, j] for d, j in chains]
        parts = _round_robin([_hgrn_state(ins[d][1][0, :, cols(j)], ins[d][2][0, :, cols(j)], lb_ref[d, j],
                                          tri_ref[d], st, chunk=chunk, reverse=bool(d))
                              for (d, j), st in zip(chains, old)])
        for (d, j), (_, _, st_new) in zip(chains, parts):
            st_ref[d, j] = st_new
        if not want_out:
            return
        subs = [_sub_block_decay(cum, bool(d)) for (d, j), (_, cum, _) in zip(chains, parts)]
        low = subs[0]
        for s in subs[1:]:
            low = jnp.minimum(low, s)
        low = jnp.min(jnp.min(low, axis=0, keepdims=True), axis=1, keepdims=True)
        in_range = low[0, 0] >= -HG_MAX_EXP2

        def readouts(shared):
            os_ = _round_robin([_hgrn_readout(ins[d][0][0, :, cols(j)], ins[d][2][0, :, cols(j)], kb, cum, sub,
                                              lvl_ref[d, int(shared)], st, chunk=chunk, reverse=bool(d), shared=shared)
                                for (d, j), (kb, cum, _), sub, st in zip(chains, parts, subs, old)])
            for (d, j), o in zip(chains, os_):
                outs[d][0, j] = o.astype(BF16)

        @pl.when(in_range)
        def _():
            readouts(True)

        @pl.when(jnp.logical_not(in_range))
        def _():
            readouts(False)

    @pl.when(step < n_ctx_chunks)
    def _():
        advance(False)

    @pl.when(step >= n_ctx_chunks)
    def _():
        advance(True)


def _hgrn_consts(chunk, reverse):
    t = np.arange(chunk)[:, None]
    s = np.arange(chunk)[None, :]
    x = t ^ s
    bitlen = np.zeros_like(x)
    for b in range(chunk.bit_length()):
        bitlen = np.where(x >> b > 0, b + 1, bitlen)
    valid = (s > t) if reverse else (t > s)
    lvl = np.where(t == s, 0, np.where(valid, bitlen, -1)).astype(np.float32)
    lvl_shared = np.where((lvl >= 0) & (lvl <= HG_SUB_LEVEL), 0, lvl)
    tri = ((s >= t) if reverse else (t >= s)).astype(np.float32)
    return np.concatenate([tri, tri], axis=1), np.stack([lvl, lvl_shared])


HG_GROUP = 8


def _hgrn_scans(p, pf, lb, n_lat):
    bsz, rows, _ = p.shape
    c = HG_CHUNK
    grp = HG_GROUP
    n_chunks = rows // c
    n_lat_c = n_lat // c
    n_ctx_c = n_chunks - n_lat_c
    consts = [_hgrn_consts(c, False), _hgrn_consts(c, True)]
    tri = jnp.asarray(np.stack([consts[0][0], consts[1][0]]), dtype=BF16)
    lvl = jnp.asarray(np.stack([consts[0][1], consts[1][1]]), dtype=BF16)

    def cidx_f(i):
        return jnp.where(i < n_ctx_c, n_lat_c + i, i - n_ctx_c)

    def cidx_b(i):
        return n_chunks - 1 - i

    def col_spec(col, cidx):
        return pl.BlockSpec((1, c, grp * HEAD_DIM), lambda b, h, i: (b, cidx(i), col // (grp * HEAD_DIM) + h))

    def out_spec(cidx):
        return pl.BlockSpec((1, grp, c, HEAD_DIM), lambda b, h, i: (b, h, cidx(jnp.maximum(i, n_ctx_c)), 0))

    o_shape = jax.ShapeDtypeStruct((bsz, HEADS, n_lat, HEAD_DIM), BF16)
    kern = functools.partial(_hgrn_kernel, chunk=c, group=grp, n_ctx_chunks=n_ctx_c)
    return pl.pallas_call(
        kern,
        out_shape=(o_shape, o_shape),
        grid=(bsz, HEADS // grp, n_chunks),
        in_specs=[col_spec(COL_HQ, cidx_f), col_spec(COL_FF, cidx_f), col_spec(COL_I, cidx_f),
                  col_spec(COL_HQ, cidx_b), col_spec(COL_FB, cidx_b), col_spec(COL_I, cidx_b),
                  pl.BlockSpec((2, grp, 1, HEAD_DIM), lambda b, h, i: (0, h, 0, 0)),
                  pl.BlockSpec((2, c, 2 * c), lambda b, h, i: (0, 0, 0)),
                  pl.BlockSpec((2, 2, c, c), lambda b, h, i: (0, 0, 0, 0))],
        out_specs=(out_spec(cidx_f), out_spec(cidx_b)),
        scratch_shapes=[pltpu.VMEM((2, grp, HEAD_DIM, HEAD_DIM), F32)],
        compiler_params=_cparams(("parallel", "parallel", "arbitrary"), VMEM_LIMIT),
        name="hgrn",
    )(p, pf, p, p, pf, p, lb, tri, lvl)


def _merge_kernel(ym_ref, of_ref, ob_ref, go_ref, gm_ref, gh_ref, x_ref, mod_ref, hgg_ref, wout_ref,
                  gffn_ref, wr2_ref, wrhi_ref, br_ref, x1_ref, h2_ref, lg_ref, y_scr, *, tm, parts):
    b = pl.program_id(0)
    m = mod_ref[pl.ds(b, 1), :]
    n_rows = tm // parts

    def part(i):
        rows = slice(i * n_rows, (i + 1) * n_rows)
        for h in range(HEADS):
            sl = slice(h * HEAD_DIM, (h + 1) * HEAD_DIM)
            o = of_ref[0, h, rows, :].astype(F32) + ob_ref[0, h, rows, :].astype(F32)
            ms = jnp.mean(o * o, axis=-1, keepdims=True)
            g = go_ref[0, rows, sl].astype(F32)
            yh = o * lax.rsqrt(ms + EPS) * hgg_ref[...] * (g * _gate_sigmoid(g))
            y = (_gate_sigmoid(gm_ref[0, rows, sl].astype(F32)) * ym_ref[0, rows, sl].astype(F32)
                 + _gate_sigmoid(gh_ref[0, rows, sl].astype(F32)) * yh)
            y_scr[rows, sl] = y.astype(BF16)
        yield
        mix = _mm(y_scr[rows, :], wout_ref[...])
        yield
        x1 = x_ref[0, rows, :] + m[:, 2 * D_MODEL:3 * D_MODEL] * mix
        x1_ref[0, rows, :] = x1
        ms = jnp.mean(x1 * x1, axis=-1, keepdims=True)
        h2 = ((x1 * lax.rsqrt(ms + EPS) * gffn_ref[...]) * (1.0 + m[:, 4 * D_MODEL:5 * D_MODEL])
              + m[:, 3 * D_MODEL:4 * D_MODEL])
        h_hi = h2.astype(BF16)
        h2_ref[rows, :] = h_hi
        h_lo = (h2 - h_hi.astype(F32)).astype(BF16)
        a = _mm(h_hi, wr2_ref[...])
        lg_ref[rows, :] = (a[:, 0:N_EXPERTS] + a[:, N_EXPERTS:2 * N_EXPERTS] + _mm(h_lo, wrhi_ref[...])
                           + br_ref[...])

    _round_robin([part(i) for i in range(parts)])


def _merge(y_mla, o_f, o_b, p, x, mod, hg_g, w_out, g_ffn, w_r, b_r):
    bsz, n, d = x.shape
    tm = _tile(n, 256, 128)
    nt = n // tm
    w_r_hi, w_r_lo = _split2(w_r)
    w_r2 = jnp.concatenate([w_r_hi, w_r_lo], axis=1)
    kern = functools.partial(_merge_kernel, tm=tm, parts=2)

    def pcol(col):
        return pl.BlockSpec((1, tm, d), lambda b, t: (b, t, col // d))

    tok = lambda b, t: (b, t, 0)
    const2 = lambda b, t: (0, 0)
    return pl.pallas_call(
        kern,
        out_shape=(jax.ShapeDtypeStruct((bsz, n, d), F32),
                   jax.ShapeDtypeStruct((bsz * n, d), BF16),
                   jax.ShapeDtypeStruct((bsz * n, N_EXPERTS), F32)),
        grid=(bsz, nt),
        in_specs=[pl.BlockSpec((1, tm, d), tok),
                  pl.BlockSpec((1, HEADS, tm, HEAD_DIM), lambda b, t: (b, 0, t, 0)),
                  pl.BlockSpec((1, HEADS, tm, HEAD_DIM), lambda b, t: (b, 0, t, 0)),
                  pcol(COL_GO), pcol(COL_MM), pcol(COL_MH),
                  pl.BlockSpec((1, tm, d), tok),
                  pl.BlockSpec(mod.shape, const2),
                  pl.BlockSpec((1, HEAD_DIM), const2),
                  pl.BlockSpec((d, d), const2),
                  pl.BlockSpec((1, d), const2),
                  pl.BlockSpec((d, 2 * N_EXPERTS), const2),
                  pl.BlockSpec((d, N_EXPERTS), const2),
                  pl.BlockSpec((1, N_EXPERTS), const2)],
        out_specs=(pl.BlockSpec((1, tm, d), tok),
                   pl.BlockSpec((tm, d), lambda b, t: (b * nt + t, 0)),
                   pl.BlockSpec((tm, N_EXPERTS), lambda b, t: (b * nt + t, 0))),
        scratch_shapes=[pltpu.VMEM((tm, d), BF16)],
        compiler_params=_cparams(("parallel", "parallel"), VMEM_LIMIT),
        name="merge",
    )(y_mla, o_f, o_b, p, p, p, x, mod, hg_g, w_out, g_ffn, w_r2, w_r_hi, b_r)


def _router_kernel(lg_ref, upper_ref, idx_ref, prob_ref, rank_ref, base_ref, cnt_ref, run_ref, *, tm, n_sub):
    @pl.when(pl.program_id(0) == 0)
    def _():
        run_ref[...] = jnp.zeros_like(run_ref)

    eidx = lax.broadcasted_iota(jnp.int32, (N_EXPERTS, tm), 0)
    run = run_ref[...]
    for s in range(n_sub):
        cols = slice(s * tm, (s + 1) * tm)
        l = lg_ref[:, cols]
        vals, sel = [], []
        for _ in range(TOP_K):
            m = jnp.max(l, axis=0, keepdims=True)
            first = jnp.min(jnp.where(l == m, eidx, N_EXPERTS), axis=0, keepdims=True)
            vals.append(m)
            sel.append(first)
            l = jnp.where(eidx == first, -jnp.inf, l)
        ex = [jnp.exp(v - vals[0]) for v in vals]
        inv = 1.0 / (ex[0] + ex[1] + ex[2] + ex[3])
        onehot = jnp.zeros((N_EXPERTS, tm), F32)
        for k in range(TOP_K):
            onehot = onehot + jnp.where(eidx == sel[k], 1.0, 0.0)
        before = _mm(onehot.astype(BF16), upper_ref[...])
        base_ref[s] = run
        for k in range(TOP_K):
            idx_ref[k:k + 1, cols] = sel[k]
            prob_ref[k:k + 1, cols] = ex[k] * inv
            rank_ref[k:k + 1, cols] = jnp.sum(jnp.where(eidx == sel[k], before, 0.0), axis=0,
                                               keepdims=True).astype(jnp.int32)
        run = run + jnp.sum(onehot, axis=1, keepdims=True)
    run_ref[...] = run
    cnt_ref[...] = run


def _router(logits_t, tm):
    _, t = logits_t.shape
    n_tiles = t // tm
    n_sub = 4 if n_tiles % 4 == 0 else 1
    step = tm * n_sub
    upper = jnp.asarray(np.triu(np.ones((tm, tm), np.float32), 1), dtype=BF16)
    kern = functools.partial(_router_kernel, tm=tm, n_sub=n_sub)
    tok = pl.BlockSpec((TOP_K, step), lambda i: (0, i))
    return pl.pallas_call(
        kern,
        out_shape=(jax.ShapeDtypeStruct((TOP_K, t), jnp.int32),
                   jax.ShapeDtypeStruct((TOP_K, t), F32),
                   jax.ShapeDtypeStruct((TOP_K, t), jnp.int32),
                   jax.ShapeDtypeStruct((n_tiles, N_EXPERTS, 128), F32),
                   jax.ShapeDtypeStruct((N_EXPERTS, 128), F32)),
        grid=(n_tiles // n_sub,),
        in_specs=[pl.BlockSpec((N_EXPERTS, step), lambda i: (0, i)),
                  pl.BlockSpec((tm, tm), lambda i: (0, 0))],
        out_specs=(tok, tok, tok, pl.BlockSpec((n_sub, N_EXPERTS, 128), lambda i: (i, 0, 0)),
                   pl.BlockSpec((N_EXPERTS, 128), lambda i: (0, 0))),
        scratch_shapes=[pltpu.VMEM((N_EXPERTS, 128), F32)],
        compiler_params=_cparams(("arbitrary",), VMEM_LIMIT),
        name="router",
    )(logits_t, upper)


ROUTE_TILE = 256
ROW_ALIGN = 8
SLOT_ROWS = 48
SHORT_ROWS = 32
N_SLOTS = N_EXPERTS * SLOT_ROWS
ZERO_ROWS = FFN_ROWS + SLOT_ROWS


def _slot_ids(idx, lrank, c):
    r = lrank - c * SLOT_ROWS
    return jnp.where(jnp.logical_and(r >= 0, r < SLOT_ROWS), idx * SLOT_ROWS + r, -1)


def _pack_bf16_pairs(x):
    u = pltpu.bitcast(x, jnp.uint32)
    w = x.shape[1] // 2
    return (u[:, 0:w] >> 16) | u[:, w:2 * w]


def _unpack_bf16_pairs(u):
    lo = pltpu.bitcast(u << 16, F32).astype(BF16)
    hi = pltpu.bitcast(u & jnp.uint32(0xFFFF0000), F32).astype(BF16)
    return jnp.concatenate([lo, hi], axis=1)


def _dispatch_kernel(tstart_ref, tcnt_ref, npass_ref, fill_ref, nused_ref, idx_ref, lrank_ref, h2_ref, xs_ref,
                     xbuf, zbuf, sem, zsem, state, *, tm, n_blocks):
    j = pl.program_id(0)

    def slot_copy(slot, e, row, n_rows):
        return pltpu.make_async_copy(xbuf.at[slot, pl.ds(e * SLOT_ROWS, n_rows)],
                                     xs_ref.at[pl.ds(pl.multiple_of(row, ROW_ALIGN), n_rows)], sem.at[slot])

    def wait_pass(slot, n_full, n_short):
        def full(i, carry):
            slot_copy(slot, 0, 0, SLOT_ROWS).wait()
            return carry

        def short(i, carry):
            slot_copy(slot, 0, 0, SHORT_ROWS).wait()
            return carry
        lax.fori_loop(0, n_full, full, 0)
        lax.fori_loop(0, n_short, short, 0)

    @pl.when(j == 0)
    def _():
        zbuf[...] = jnp.zeros_like(zbuf)
        state[0] = 0
        state[1] = 0
        state[2] = 0
        for parity in range(2):
            for e in range(parity, N_EXPERTS, 2):
                pltpu.make_async_copy(zbuf, xs_ref.at[pl.ds(pl.multiple_of(fill_ref[e], ROW_ALIGN), ZERO_ROWS)],
                                      zsem).start()
            for e in range(parity, N_EXPERTS, 2):
                pltpu.make_async_copy(zbuf, xs_ref.at[pl.ds(0, ZERO_ROWS)], zsem).wait()

        def tail(i, carry):
            pltpu.make_async_copy(zbuf.at[pl.ds(0, FFN_ROWS)],
                                  xs_ref.at[pl.ds(pl.multiple_of(i * FFN_ROWS, ROW_ALIGN), FFN_ROWS)], zsem).start()
            return carry
        lax.fori_loop(nused_ref[0], n_blocks, tail, 0)

        def tail_wait(i, carry):
            pltpu.make_async_copy(zbuf.at[pl.ds(0, FFN_ROWS)], xs_ref.at[pl.ds(0, FFN_ROWS)], zsem).wait()
            return carry
        lax.fori_loop(nused_ref[0], n_blocks, tail_wait, 0)

    def one_pass(c, carry):
        rows = lax.broadcasted_iota(jnp.int32, (N_SLOTS, tm), 0)
        sel = jnp.zeros((N_SLOTS, tm), F32)
        for k in range(TOP_K):
            sel = jnp.where(rows == _slot_ids(idx_ref[k:k + 1, :], lrank_ref[k:k + 1, :], c), 1.0, sel)
        packed = _pack_bf16_pairs(_mm(sel.astype(BF16), h2_ref[...]))
        p = state[0]
        slot = p % 2
        wait_pass(1 - slot, state[1], state[2])
        xbuf[slot] = packed
        n_full = 0
        n_short = 0
        for e in range(N_EXPERTS):
            left = tcnt_ref[j * N_EXPERTS + e] - c * SLOT_ROWS
            row = tstart_ref[j * N_EXPERTS + e] + c * SLOT_ROWS
            is_full = left > SHORT_ROWS
            is_short = jnp.logical_and(left > 0, left <= SHORT_ROWS)

            @pl.when(is_full)
            def _():
                slot_copy(slot, e, row, SLOT_ROWS).start(priority=e % 2)

            @pl.when(is_short)
            def _():
                slot_copy(slot, e, row, SHORT_ROWS).start(priority=e % 2)
            n_full = n_full + is_full.astype(jnp.int32)
            n_short = n_short + is_short.astype(jnp.int32)
        state[0] = p + 1
        state[1] = n_full
        state[2] = n_short
        return carry

    lax.fori_loop(0, npass_ref[j], one_pass, 0)

    @pl.when(j == pl.num_programs(0) - 1)
    def _():
        wait_pass((state[0] - 1) % 2, state[1], state[2])


def _dispatch(tile_start, tile_count, n_pass, fill_start, n_used, idx_t, lrank_t, h2, n_rows):
    t, d = h2.shape
    tm = ROUTE_TILE
    w = d // 2
    n_blocks = n_rows // FFN_ROWS
    kern = functools.partial(_dispatch_kernel, tm=tm, n_blocks=n_blocks)
    tok = pl.BlockSpec((TOP_K, tm), lambda i, *_: (0, i))
    grid_spec = pltpu.PrefetchScalarGridSpec(
        num_scalar_prefetch=5,
        grid=(t // tm,),
        in_specs=[tok, tok, pl.BlockSpec((tm, d), lambda i, *_: (i, 0))],
        out_specs=pl.BlockSpec(memory_space=pl.ANY),
        scratch_shapes=[pltpu.VMEM((2, N_SLOTS, w), jnp.uint32), pltpu.VMEM((ZERO_ROWS, w), jnp.uint32),
                        pltpu.SemaphoreType.DMA((2,)), pltpu.SemaphoreType.DMA, pltpu.SMEM((3,), jnp.int32)],
    )
    return pl.pallas_call(
        kern,
        out_shape=jax.ShapeDtypeStruct((n_rows, w), jnp.uint32),
        grid_spec=grid_spec,
        compiler_params=_cparams(("arbitrary",), VMEM_LIMIT),
        name="dispatch",
    )(tile_start, tile_count, n_pass, fill_start, n_used, idx_t, lrank_t, h2)


GU_GROUP = 256


def _ffn_kernel(be_ref, nu_ref, x_ref, wgu_ref, bgu_ref, wd_ref, bd_ref, perm_ref, y_ref, wgu_s, wd_s):
    i = pl.program_id(0)
    live = i < nu_ref[0]
    new_expert = jnp.logical_or(i == 0, be_ref[i] != be_ref[jnp.maximum(i - 1, 0)])

    @pl.when(jnp.logical_and(live, new_expert))
    def _():
        for g in range(2 * D_EXPERT // GU_GROUP):
            sl = slice(g * GU_GROUP, (g + 1) * GU_GROUP)
            wgu_s[:, sl] = _mm(wgu_ref[0, :, sl].astype(BF16), perm_ref[...]).astype(BF16)
        wd_s[...] = wd_ref[0].astype(BF16)

    @pl.when(live)
    def _():
        x = _unpack_bf16_pairs(x_ref[...])
        gu = _mm(x, wgu_s[...]) + bgu_ref[0]
        half = GU_GROUP // 2
        n_grp = 2 * D_EXPERT // GU_GROUP
        glu = jnp.concatenate([gu[:, g * GU_GROUP:g * GU_GROUP + half] for g in range(n_grp)], axis=1)
        lin = jnp.concatenate([gu[:, g * GU_GROUP + half:(g + 1) * GU_GROUP] for g in range(n_grp)], axis=1)
        glu = jnp.minimum(glu, SWIGLU_LIMIT)
        lin = jnp.clip(lin, -SWIGLU_LIMIT, SWIGLU_LIMIT)
        act = glu * _sigmoid(SWIGLU_ALPHA * glu) * (lin + 1.0)
        y = _mm(act.astype(BF16), wd_s[...]) + bd_ref[0]
        y_ref[...] = _pack_bf16_pairs(y.astype(BF16).astype(F32))

    @pl.when(jnp.logical_not(live))
    def _():
        y_ref[...] = jnp.zeros_like(y_ref)


def _ffn(block_e, n_used, xs, w_gu, b_gu, w_d, b_d):
    n_rows, w = xs.shape
    bm = FFN_ROWS
    nblk = n_rows // bm
    d = w * 2
    half = GU_GROUP // 2
    perm = np.zeros((GU_GROUP, GU_GROUP), np.float32)
    perm[2 * np.arange(half), np.arange(half)] = 1.0
    perm[2 * np.arange(half) + 1, half + np.arange(half)] = 1.0

    def xmap(i, be, nu):
        return (jnp.minimum(i, nu[0] - 1), 0)

    def wmap(i, be, nu):
        return (be[i], 0, 0)

    grid_spec = pltpu.PrefetchScalarGridSpec(
        num_scalar_prefetch=2,
        grid=(nblk,),
        in_specs=[pl.BlockSpec((bm, w), xmap),
                  pl.BlockSpec((1, d, 2 * D_EXPERT), wmap),
                  pl.BlockSpec((1, 1, 2 * D_EXPERT), wmap),
                  pl.BlockSpec((1, D_EXPERT, d), wmap),
                  pl.BlockSpec((1, 1, d), wmap),
                  pl.BlockSpec((GU_GROUP, GU_GROUP), lambda i, be, nu: (0, 0))],
        out_specs=pl.BlockSpec((bm, w), lambda i, be, nu: (i, 0)),
        scratch_shapes=[pltpu.VMEM((d, 2 * D_EXPERT), BF16), pltpu.VMEM((D_EXPERT, d), BF16)],
    )
    return pl.pallas_call(
        _ffn_kernel,
        out_shape=jax.ShapeDtypeStruct((n_rows, w), jnp.uint32),
        grid_spec=grid_spec,
        compiler_params=_cparams(("arbitrary",), VMEM_LIMIT),
        name="ffn",
    )(block_e, n_used, xs, w_gu, b_gu, w_d, b_d, jnp.asarray(perm, dtype=BF16))


def _combine_kernel(tstart_ref, tcnt_ref, npass_ref, idx_ref, lrank_ref, p_ref, y_ref, x1_ref, mod_ref, g_ref, o_ref,
                    ybuf, sem, *, tm, nt):
    b = pl.program_id(0)
    j = b * nt + pl.program_id(1)
    n_steps = pl.num_programs(0) * nt
    slot = j % 2

    def slot_copy(s, e, row, n_rows):
        return pltpu.make_async_copy(y_ref.at[pl.ds(pl.multiple_of(row, ROW_ALIGN), n_rows)],
                                     ybuf.at[s, pl.ds(e * SLOT_ROWS, n_rows)], sem.at[s])

    def per_slot(tile, c, s, act):
        for e in range(N_EXPERTS):
            left = tcnt_ref[tile * N_EXPERTS + e] - c * SLOT_ROWS
            row = tstart_ref[tile * N_EXPERTS + e] + c * SLOT_ROWS

            @pl.when(left > SHORT_ROWS)
            def _():
                act(slot_copy(s, e, row, SLOT_ROWS), e)

            @pl.when(jnp.logical_and(left > 0, left <= SHORT_ROWS))
            def _():
                act(slot_copy(s, e, row, SHORT_ROWS), e)

    def fetch(tile, c, s):
        per_slot(tile, c, s, lambda copy, e: copy.start(priority=e % 2))

    def arrive(tile, c, s):
        per_slot(tile, c, s, lambda copy, e: copy.wait())

    @pl.when(j == 0)
    def _():
        ybuf[...] = jnp.zeros_like(ybuf)
        fetch(0, 0, 0)

    @pl.when(j + 1 < n_steps)
    def _():
        fetch(j + 1, 0, 1 - slot)

    def weights(c):
        cols = lax.broadcasted_iota(jnp.int32, (tm, N_SLOTS), 1)
        w = jnp.zeros((tm, N_SLOTS), F32)
        for k in range(TOP_K):
            w = jnp.where(cols == _slot_ids(idx_ref[:, k:k + 1], lrank_ref[:, k:k + 1], c), p_ref[:, k:k + 1], w)
        return w.astype(BF16)

    arrive(j, 0, slot)
    acc = _mm(weights(0), _unpack_bf16_pairs(ybuf[slot]))

    def extra_pass(c, acc):
        fetch(j, c, 2)
        arrive(j, c, 2)
        return acc + _mm(weights(c), _unpack_bf16_pairs(ybuf[2]))

    acc = lax.fori_loop(1, npass_ref[j], extra_pass, acc)
    m = mod_ref[pl.ds(b, 1), :]
    xo = x1_ref[0] + m[:, 5 * D_MODEL:6 * D_MODEL] * acc
    ms = jnp.mean(xo * xo, axis=-1, keepdims=True)
    o_ref[0] = xo * lax.rsqrt(ms + EPS) * g_ref[...]


def _combine(tile_start, tile_count, n_pass, idx_c, lrank_c, prob_c, y, x1, mod, g_fin):
    bsz, n, d = x1.shape
    tm = ROUTE_TILE
    nt = n // tm
    kern = functools.partial(_combine_kernel, tm=tm, nt=nt)
    tok = pl.BlockSpec((tm, TOP_K), lambda b, t, *_: (b * nt + t, 0))
    grid_spec = pltpu.PrefetchScalarGridSpec(
        num_scalar_prefetch=3,
        grid=(bsz, nt),
        in_specs=[tok, tok, tok,
                  pl.BlockSpec(memory_space=pl.ANY),
                  pl.BlockSpec((1, tm, d), lambda b, t, *_: (b, t, 0)),
                  pl.BlockSpec(mod.shape, lambda b, t, *_: (0, 0)),
                  pl.BlockSpec((1, d), lambda b, t, *_: (0, 0))],
        out_specs=pl.BlockSpec((1, tm, d), lambda b, t, *_: (b, t, 0)),
        scratch_shapes=[pltpu.VMEM((3, N_SLOTS, d // 2), jnp.uint32), pltpu.SemaphoreType.DMA((3,))],
    )
    return pl.pallas_call(
        kern,
        out_shape=jax.ShapeDtypeStruct((bsz, n, d), F32),
        grid_spec=grid_spec,
        compiler_params=_cparams(("arbitrary", "arbitrary"), VMEM_LIMIT),
        name="combine",
    )(tile_start, tile_count, n_pass, idx_c, lrank_c, prob_c, y, x1, mod, g_fin)


def _prep_w_in(w_in):
    q_a, kv, hq, ff, fb, hi, go, mm, mh = jnp.split(
        w_in, np.cumsum((MLA_Q_RANK, MLA_KV_RANK + MLA_ROPE) + (D_MODEL,) * 6).tolist(), axis=1)
    c_kv, kr = kv[:, :MLA_KV_RANK], kv[:, MLA_KV_RANK:]
    half = MLA_ROPE // 2
    kr_sw = jnp.concatenate([-kr[:, half:], kr[:, :half]], axis=1)
    w16 = jnp.concatenate([hq, hi, go, mm, mh, q_a, c_kv, kr, kr_sw], axis=1)
    w16 = jnp.pad(w16, ((0, 0), (0, N_PROJ16 - w16.shape[1])))
    return w16.astype(BF16), jnp.concatenate([ff, fb], axis=1).astype(BF16)


def _prep_w_q(w_q_b):
    w = w_q_b.reshape(MLA_Q_RANK, HEADS, MLA_QK)
    w = jnp.pad(w, ((0, 0), (0, 0), (0, MLA_QK_PAD - MLA_QK)))
    return w.reshape(MLA_Q_RANK, HEADS * MLA_QK_PAD).astype(BF16)


def _prep_w_kv(w_kv_b):
    w = w_kv_b.reshape(MLA_KV_RANK, HEADS, 2 * HEAD_DIM)
    k = w[:, :, :HEAD_DIM].reshape(MLA_KV_RANK, HEADS * HEAD_DIM)
    v = w[:, :, HEAD_DIM:].reshape(MLA_KV_RANK, HEADS * HEAD_DIM)
    return jnp.concatenate([k, v], axis=1).astype(BF16)


def _rope_tables(n_lat, n_ctx):
    f32 = np.float32
    rows = n_lat // GRID_W
    row = np.repeat(np.arange(rows), GRID_W).astype(f32)
    col = np.tile(np.arange(GRID_W), rows).astype(f32)
    n_freq = MLA_ROPE // 4
    inv = (f32(ROPE_BASE) ** (-np.arange(n_freq, dtype=f32) / f32(n_freq))).astype(f32)
    ang = np.concatenate([row[:, None] * inv, col[:, None] * inv], axis=-1).astype(f32)
    cos, sin = np.cos(ang).astype(f32), np.sin(ang).astype(f32)
    pad = np.zeros((n_lat, HEAD_DIM - MLA_ROPE), f32)
    cos_q = np.concatenate([cos, cos, pad], axis=1)
    sin_q = np.concatenate([-sin, sin, pad], axis=1)
    cos_k = np.concatenate([cos, cos, pad], axis=1)
    sin_k = np.concatenate([sin, sin, pad], axis=1)
    ctx_cos = np.concatenate([np.ones((n_ctx, MLA_ROPE), f32), np.zeros((n_ctx, HEAD_DIM - MLA_ROPE), f32)], axis=1)
    cos_k = np.concatenate([cos_k, ctx_cos], axis=0)
    sin_k = np.concatenate([sin_k, np.zeros((n_ctx, HEAD_DIM), f32)], axis=0)
    return jnp.asarray(cos_q), jnp.asarray(sin_q), jnp.asarray(cos_k), jnp.asarray(sin_k)


def kernel(x, c, ctx, c_ctx, w_mod, b_mod, norm_mix_g, w_in, mla_q_norm_g, w_q_b, mla_kv_norm_g, w_kv_b,
           hg_lb_logits, hg_norm_g, w_out, norm_ffn_g, w_router, b_router, w_gate_up, b_gate_up, w_down,
           b_down, final_norm_g):
    bsz, n_lat, d = x.shape
    n_ctx = ctx.shape[1]
    assert d == D_MODEL and w_mod.shape[0] == 1
    assert n_lat % HG_CHUNK == 0 and n_ctx % HG_CHUNK == 0 and n_lat % GRID_W == 0
    t_tok = bsz * n_lat

    mod_rows = -(-(bsz + 1) // 8) * 8
    cc = jnp.concatenate([c, c_ctx[None, :], jnp.zeros((mod_rows - bsz - 1, d), F32)], axis=0)
    mod = _modulation(cc, w_mod[0], b_mod[0][None, :])

    w16, w32 = _prep_w_in(w_in[0])
    p, pf = _inproj(x, ctx, mod, norm_mix_g[0][None, :], w16, w32)

    cos_q, sin_q, cos_k, sin_k = _rope_tables(n_lat, n_ctx)
    q = _mla_q(p, mla_q_norm_g[0][None, :], _prep_w_q(w_q_b[0]), cos_q, sin_q, n_lat)
    k, v = _mla_kv(p, mla_kv_norm_g[0][None, :], _prep_w_kv(w_kv_b[0]), cos_k, sin_k)
    y_mla = _attention(q, k, v)

    lb = jax.nn.softmax(hg_lb_logits.astype(F32), axis=1)[:, 0, :]
    o_f, o_b = _hgrn_scans(p, pf, lb.reshape(2, HEADS, 1, HEAD_DIM), n_lat)

    x1, h2, logits = _merge(y_mla, o_f, o_b, p, x, mod, hg_norm_g[0][None, :], w_out[0].astype(BF16),
                            norm_ffn_g[0][None, :], w_router[0], b_router[0][None, :])
    logits_t = logits.T

    assert n_lat % ROUTE_TILE == 0
    idx_t, prob_t, lrank_t, base, cnt = _router(logits_t, ROUTE_TILE)

    n_tiles = t_tok // ROUTE_TILE
    counts = cnt[:, 0].astype(jnp.int32)
    base = base[:, :, 0].astype(jnp.int32)
    tile_count = jnp.concatenate([base[1:] - base[:-1], (counts - base[-1])[None, :]], axis=0)
    seg = (tile_count + ROW_ALIGN - 1) // ROW_ALIGN * ROW_ALIGN
    seg_end = jnp.cumsum(seg, axis=0)
    rows_e = seg_end[-1]
    padded = (rows_e + SLOT_ROWS + FFN_ROWS - 1) // FFN_ROWS * FFN_ROWS
    pad_end = jnp.cumsum(padded)
    pad_start = pad_end - padded
    max_rows = t_tok * TOP_K + N_EXPERTS * (n_tiles * (ROW_ALIGN - 1) + SLOT_ROWS + FFN_ROWS - 1)
    n_rows = -(-max_rows // FFN_ROWS) * FFN_ROWS
    nblk = n_rows // FFN_ROWS
    block_start = jnp.arange(nblk, dtype=jnp.int32) * FFN_ROWS
    block_e = jnp.minimum(jnp.sum(pad_end[None, :] <= block_start[:, None], axis=1), N_EXPERTS - 1).astype(jnp.int32)
    n_used = (pad_end[-1:] // FFN_ROWS).astype(jnp.int32)
    tile_start = (pad_start[None, :] + seg_end - seg).reshape(-1).astype(jnp.int32)
    n_pass = jnp.maximum(-(-jnp.max(tile_count, axis=1) // SLOT_ROWS), 1).astype(jnp.int32)
    tile_count = tile_count.reshape(-1)
    fill_start = jnp.minimum(pad_start + rows_e, n_rows - ZERO_ROWS).astype(jnp.int32)

    xs = _dispatch(tile_start, tile_count, n_pass, fill_start, n_used, idx_t, lrank_t, h2, n_rows)

    n_grp = 2 * D_EXPERT // GU_GROUP
    b_gu = b_gate_up[0].reshape(N_EXPERTS, n_grp, GU_GROUP // 2, 2).transpose(0, 1, 3, 2).reshape(N_EXPERTS, 1, 2 * D_EXPERT)
    y = _ffn(block_e, n_used, xs, w_gate_up[0], b_gu, w_down[0], b_down[0][:, None, :])

    return _combine(tile_start, tile_count, n_pass, idx_t.T, lrank_t.T, prob_t.T, y, x1, mod, final_norm_g[None, :])
```
